```python
import math
import jax, jax.numpy as jnp
from jax import lax
import numpy as np

D_MODEL = 2048
BATCH = 8
SEQ = 2048
DEPTH = 2

N_MIXERS = 2
N_POOL_LAYERS = (DEPTH + 1) // 2
N_SSM_LAYERS = DEPTH // 2

ALPHA = (2.0 * DEPTH) ** 0.25
BETA = (8.0 * DEPTH) ** -0.25
LN_EPS = 1e-5

POOL_WINDOWS = (2, 4, 8, 16)
N_POOL_GROUPS = len(POOL_WINDOWS)
POOL_GROUP_DIM = D_MODEL // N_POOL_GROUPS

SSM_EXPAND = 2
D_INNER = SSM_EXPAND * D_MODEL
SSM_HEAD_DIM = 64
SSM_HEADS = D_INNER // SSM_HEAD_DIM
SSM_GROUPS = 8
HEADS_PER_GROUP = SSM_HEADS // SSM_GROUPS
D_STATE = 128
CONV_WIDTH = 4
CHUNK = 128
CONV_DIM = D_INNER + 2 * SSM_GROUPS * D_STATE
D_IN_PROJ = D_INNER + CONV_DIM + SSM_HEADS
RMS_EPS = 1e-5

D_FF = 4 * D_MODEL

PLE_DIM = 256

kernel_name = "pool_ssd_interleaved_deepnorm_hybrid"


def layer_norm(x, g, b):
    xf = x.astype(jnp.float32)
    mu = jnp.mean(xf, axis=-1, keepdims=True)
    var = jnp.mean(jnp.square(xf - mu), axis=-1, keepdims=True)
    y = (xf - mu) * lax.rsqrt(var + LN_EPS) * g.astype(jnp.float32) + b.astype(jnp.float32)
    return y.astype(x.dtype)


def rms_norm(x, g):
    xf = x.astype(jnp.float32)
    y = xf * lax.rsqrt(jnp.mean(jnp.square(xf), axis=-1, keepdims=True) + RMS_EPS)
    return y * g.astype(jnp.float32)


def pool_mixer(x, w, scale):
    bsz, seq, _ = x.shape
    xf = x.astype(jnp.float32)
    cs = jnp.cumsum(xf, axis=1)
    pos = jnp.arange(seq)
    outs = []
    for g, win in enumerate(POOL_WINDOWS):
        sl = slice(g * POOL_GROUP_DIM, (g + 1) * POOL_GROUP_DIM)
        c = cs[..., sl]
        c_prev = jnp.pad(c, ((0, 0), (win, 0), (0, 0)))[:, :seq]
        cnt = jnp.minimum(pos + 1, win).astype(jnp.float32)[:, None]
        outs.append((c - c_prev) / cnt - xf[..., sl])
    pooled = jnp.stack(outs, axis=2).astype(x.dtype)
    y = jnp.einsum('bsgc,gcd->bsgd', pooled, w).reshape(bsz, seq, D_MODEL)
    return y * scale


def causal_depthwise_conv(u, w, b):
    seq = u.shape[1]
    up = jnp.pad(u, ((0, 0), (CONV_WIDTH - 1, 0), (0, 0)))
    out = b
    for k in range(CONV_WIDTH):
        out = out + up[:, k:k + seq] * w[k]
    return out


def ssd_mixer(x, in_w, conv_w, conv_b, dt_bias, a_log, d_skip, norm_w, out_w):
    bsz, seq, _ = x.shape
    nc = seq // CHUNK
    zxbcdt = x @ in_w
    z = zxbcdt[..., :D_INNER]
    xbc = zxbcdt[..., D_INNER:D_INNER + CONV_DIM]
    dt = zxbcdt[..., D_INNER + CONV_DIM:]
    xbc = jax.nn.silu(causal_depthwise_conv(xbc, conv_w, conv_b))
    xs = xbc[..., :D_INNER]
    bm = xbc[..., D_INNER:D_INNER + SSM_GROUPS * D_STATE]
    cm = xbc[..., D_INNER + SSM_GROUPS * D_STATE:]

    dt = jax.nn.softplus(dt.astype(jnp.float32) + dt_bias.astype(jnp.float32))
    a = -jnp.exp(a_log.astype(jnp.float32)).reshape(SSM_GROUPS, HEADS_PER_GROUP)

    xs = xs.astype(jnp.float32).reshape(bsz, nc, CHUNK, SSM_GROUPS, HEADS_PER_GROUP, SSM_HEAD_DIM)
    bm = bm.astype(jnp.float32).reshape(bsz, nc, CHUNK, SSM_GROUPS, D_STATE)
    cm = cm.astype(jnp.float32).reshape(bsz, nc, CHUNK, SSM_GROUPS, D_STATE)
    dt = dt.reshape(bsz, nc, CHUNK, SSM_GROUPS, HEADS_PER_GROUP)

    da = jnp.transpose(dt * a, (0, 3, 4, 1, 2))
    a_cs = jnp.cumsum(da, axis=-1)
    xdt = xs * dt[..., None]

    causal = jnp.tril(jnp.ones((CHUNK, CHUNK), dtype=bool))
    seg = a_cs[..., :, None] - a_cs[..., None, :]
    lmat = jnp.exp(jnp.where(causal, seg, -jnp.inf))
    cb = jnp.einsum('bclgn,bcsgn->bgcls', cm, bm)
    mmat = cb[:, :, None] * lmat
    y_diag = jnp.einsum('bghcls,bcsghp->bclghp', mmat, xdt)

    decay_states = jnp.exp(a_cs[..., -1:] - a_cs)
    xdt_dec = xdt * jnp.transpose(decay_states, (0, 3, 4, 1, 2))[..., None]
    states = jnp.einsum('bclgn,bclghp->bcghpn', bm, xdt_dec)
    chunk_decay = jnp.exp(a_cs[..., -1])

    def step(h, inp):
        s, d = inp
        return d[..., None, None] * h + s, h

    h0 = jnp.zeros((bsz, SSM_GROUPS, HEADS_PER_GROUP, SSM_HEAD_DIM, D_STATE), jnp.float32)
    _, prev = lax.scan(step, h0, (jnp.moveaxis(states, 1, 0), jnp.moveaxis(chunk_decay, 3, 0)))
    prev = jnp.moveaxis(prev, 0, 1)

    state_decay = jnp.transpose(jnp.exp(a_cs), (0, 3, 4, 1, 2))
    y_off = jnp.einsum('bclgn,bcghpn->bclghp', cm, prev) * state_decay[..., None]

    dsk = d_skip.astype(jnp.float32).reshape(SSM_GROUPS, HEADS_PER_GROUP)[..., None]
    y = (y_diag + y_off + xs * dsk).reshape(bsz, seq, D_INNER)
    y = rms_norm(y * jax.nn.silu(z.astype(jnp.float32)), norm_w).astype(x.dtype)
    return y @ out_w


def sq_relu_mlp(x, w1, w2):
    h = jax.nn.relu(x @ w1)
    return (h * h) @ w2


def _fwd_setup_inputs(seed: int = 0) -> dict:
    key = jax.random.key(seed)
    ks = iter(jax.random.split(key, 32))
    f32 = jnp.float32

    def nrm(shape, scale):
        return jax.random.normal(next(ks), shape, f32) * scale

    x = nrm((BATCH, SEQ, D_MODEL), 1.0)
    p = nrm((DEPTH, BATCH, SEQ, PLE_DIM), 1.0)

    pool_w = nrm((N_POOL_LAYERS, N_POOL_GROUPS, POOL_GROUP_DIM, POOL_GROUP_DIM), BETA * POOL_GROUP_DIM ** -0.5)
    pool_scale = 1.0 + nrm((N_POOL_LAYERS, D_MODEL), 0.1)

    ssm_in_w = nrm((N_SSM_LAYERS, D_MODEL, D_IN_PROJ), D_MODEL ** -0.5)
    ssm_conv_w = nrm((N_SSM_LAYERS, CONV_WIDTH, CONV_DIM), CONV_WIDTH ** -0.5)
    ssm_conv_b = nrm((N_SSM_LAYERS, CONV_DIM), 0.02)
    dt0 = jnp.exp(jax.random.uniform(next(ks), (N_SSM_LAYERS, SSM_HEADS), f32,
                                     math.log(1e-3), math.log(1e-1)))
    ssm_dt_bias = dt0 + jnp.log(-jnp.expm1(-dt0))
    ssm_a_log = jnp.log(jax.random.uniform(next(ks), (N_SSM_LAYERS, SSM_HEADS), f32, 1.0, 16.0))
    ssm_d = 1.0 + nrm((N_SSM_LAYERS, SSM_HEADS), 0.1)
    ssm_norm_w = 1.0 + nrm((N_SSM_LAYERS, D_INNER), 0.1)
    ssm_out_w = nrm((N_SSM_LAYERS, D_INNER, D_MODEL), BETA * D_INNER ** -0.5)

    mlp_w1 = nrm((DEPTH, D_MODEL, D_FF), D_MODEL ** -0.5)
    mlp_w2 = nrm((DEPTH, D_FF, D_MODEL), BETA * D_FF ** -0.5)

    ln_g = 1.0 + nrm((DEPTH, 2, D_MODEL), 0.1)
    ln_b = nrm((DEPTH, 2, D_MODEL), 0.02)

    ple_w = nrm((DEPTH, PLE_DIM, D_MODEL), PLE_DIM ** -0.5)
    ple_gate_w = nrm((DEPTH, D_MODEL, D_MODEL), D_MODEL ** -0.5)

    return {"x": x, "p": p,
            "pool_w": pool_w, "pool_scale": pool_scale,
            "ssm_in_w": ssm_in_w, "ssm_conv_w": ssm_conv_w, "ssm_conv_b": ssm_conv_b,
            "ssm_dt_bias": ssm_dt_bias, "ssm_a_log": ssm_a_log, "ssm_d": ssm_d,
            "ssm_norm_w": ssm_norm_w, "ssm_out_w": ssm_out_w,
            "mlp_w1": mlp_w1, "mlp_w2": mlp_w2,
            "ln_g": ln_g, "ln_b": ln_b,
            "ple_w": ple_w, "ple_gate_w": ple_gate_w}


def _fwd_reference(x, p, pool_w, pool_scale, ssm_in_w, ssm_conv_w, ssm_conv_b,
              ssm_dt_bias, ssm_a_log, ssm_d, ssm_norm_w, ssm_out_w,
              mlp_w1, mlp_w2, ln_g, ln_b, ple_w, ple_gate_w):
    for i in range(DEPTH):
        j = i // N_MIXERS
        if i % N_MIXERS == 0:
            h = pool_mixer(x, pool_w[j], pool_scale[j])
        else:
            h = ssd_mixer(x, ssm_in_w[j], ssm_conv_w[j], ssm_conv_b[j], ssm_dt_bias[j],
                          ssm_a_log[j], ssm_d[j], ssm_norm_w[j], ssm_out_w[j])
        x = layer_norm(ALPHA * x + h, ln_g[i, 0], ln_b[i, 0])
        h = sq_relu_mlp(x, mlp_w1[i], mlp_w2[i])
        x = layer_norm(ALPHA * x + h, ln_g[i, 1], ln_b[i, 1])
        gate = jax.nn.sigmoid(x @ ple_gate_w[i])
        x = x + gate * (p[i] @ ple_w[i])
    return x


import jax as _jax
import jax.numpy as _jnp

TWIN_FORMAT = 'train_step'
FWD_PARAMS = ['x', 'p', 'pool_w', 'pool_scale', 'ssm_in_w', 'ssm_conv_w', 'ssm_conv_b', 'ssm_dt_bias', 'ssm_a_log', 'ssm_d', 'ssm_norm_w', 'ssm_out_w', 'mlp_w1', 'mlp_w2', 'ln_g', 'ln_b', 'ple_w', 'ple_gate_w']
TWIN_WEIGHTS = ['pool_w', 'pool_scale', 'ssm_in_w', 'ssm_conv_w', 'ssm_conv_b', 'ssm_dt_bias', 'ssm_a_log', 'ssm_d', 'ssm_norm_w', 'ssm_out_w', 'mlp_w1', 'mlp_w2', 'ln_g', 'ln_b', 'ple_w', 'ple_gate_w']
TWIN_DIFF_INPUT = 'x'
TWIN_INPUTS = ['x', 'p', 'pool_w', 'pool_scale', 'ssm_in_w', 'ssm_conv_w', 'ssm_conv_b', 'ssm_dt_bias', 'ssm_a_log', 'ssm_d', 'ssm_norm_w', 'ssm_out_w', 'mlp_w1', 'mlp_w2', 'ln_g', 'ln_b', 'ple_w', 'ple_gate_w', 'loss_target', 'm_pool_w', 'm_pool_scale', 'm_ssm_in_w', 'm_ssm_conv_w', 'm_ssm_conv_b', 'm_ssm_dt_bias', 'm_ssm_a_log', 'm_ssm_d', 'm_ssm_norm_w', 'm_ssm_out_w', 'm_mlp_w1', 'm_mlp_w2', 'm_ln_g', 'm_ln_b', 'm_ple_w', 'm_ple_gate_w', 'v_pool_w', 'v_pool_scale', 'v_ssm_in_w', 'v_ssm_conv_w', 'v_ssm_conv_b', 'v_ssm_dt_bias', 'v_ssm_a_log', 'v_ssm_d', 'v_ssm_norm_w', 'v_ssm_out_w', 'v_mlp_w1', 'v_mlp_w2', 'v_ln_g', 'v_ln_b', 'v_ple_w', 'v_ple_gate_w']
TWIN_OUTPUTS = ['loss', 'grad_x', 'grad_pool_w', 'grad_pool_scale', 'grad_ssm_in_w', 'grad_ssm_conv_w', 'grad_ssm_conv_b', 'grad_ssm_dt_bias', 'grad_ssm_a_log', 'grad_ssm_d', 'grad_ssm_norm_w', 'grad_ssm_out_w', 'grad_mlp_w1', 'grad_mlp_w2', 'grad_ln_g', 'grad_ln_b', 'grad_ple_w', 'grad_ple_gate_w', 'delta_pool_w', 'delta_pool_scale', 'delta_ssm_in_w', 'delta_ssm_conv_w', 'delta_ssm_conv_b', 'delta_ssm_dt_bias', 'delta_ssm_a_log', 'delta_ssm_d', 'delta_ssm_norm_w', 'delta_ssm_out_w', 'delta_mlp_w1', 'delta_mlp_w2', 'delta_ln_g', 'delta_ln_b', 'delta_ple_w', 'delta_ple_gate_w', 'new_m_pool_w', 'new_m_pool_scale', 'new_m_ssm_in_w', 'new_m_ssm_conv_w', 'new_m_ssm_conv_b', 'new_m_ssm_dt_bias', 'new_m_ssm_a_log', 'new_m_ssm_d', 'new_m_ssm_norm_w', 'new_m_ssm_out_w', 'new_m_mlp_w1', 'new_m_mlp_w2', 'new_m_ln_g', 'new_m_ln_b', 'new_m_ple_w', 'new_m_ple_gate_w', 'new_v_pool_w', 'new_v_pool_scale', 'new_v_ssm_in_w', 'new_v_ssm_conv_w', 'new_v_ssm_conv_b', 'new_v_ssm_dt_bias', 'new_v_ssm_a_log', 'new_v_ssm_d', 'new_v_ssm_norm_w', 'new_v_ssm_out_w', 'new_v_mlp_w1', 'new_v_mlp_w2', 'new_v_ln_g', 'new_v_ln_b', 'new_v_ple_w', 'new_v_ple_gate_w']
TWIN_LEAF_KINDS = {'loss': 'loss', 'grad_x': 'grad_x', 'grad_pool_w': 'grad_w', 'grad_pool_scale': 'grad_w', 'grad_ssm_in_w': 'grad_w', 'grad_ssm_conv_w': 'grad_w', 'grad_ssm_conv_b': 'grad_w', 'grad_ssm_dt_bias': 'grad_w', 'grad_ssm_a_log': 'grad_w', 'grad_ssm_d': 'grad_w', 'grad_ssm_norm_w': 'grad_w', 'grad_ssm_out_w': 'grad_w', 'grad_mlp_w1': 'grad_w', 'grad_mlp_w2': 'grad_w', 'grad_ln_g': 'grad_w', 'grad_ln_b': 'grad_w', 'grad_ple_w': 'grad_w', 'grad_ple_gate_w': 'grad_w', 'delta_pool_w': 'delta_w', 'delta_pool_scale': 'delta_w', 'delta_ssm_in_w': 'delta_w', 'delta_ssm_conv_w': 'delta_w', 'delta_ssm_conv_b': 'delta_w', 'delta_ssm_dt_bias': 'delta_w', 'delta_ssm_a_log': 'delta_w', 'delta_ssm_d': 'delta_w', 'delta_ssm_norm_w': 'delta_w', 'delta_ssm_out_w': 'delta_w', 'delta_mlp_w1': 'delta_w', 'delta_mlp_w2': 'delta_w', 'delta_ln_g': 'delta_w', 'delta_ln_b': 'delta_w', 'delta_ple_w': 'delta_w', 'delta_ple_gate_w': 'delta_w', 'new_m_pool_w': 'new_m', 'new_m_pool_scale': 'new_m', 'new_m_ssm_in_w': 'new_m', 'new_m_ssm_conv_w': 'new_m', 'new_m_ssm_conv_b': 'new_m', 'new_m_ssm_dt_bias': 'new_m', 'new_m_ssm_a_log': 'new_m', 'new_m_ssm_d': 'new_m', 'new_m_ssm_norm_w': 'new_m', 'new_m_ssm_out_w': 'new_m', 'new_m_mlp_w1': 'new_m', 'new_m_mlp_w2': 'new_m', 'new_m_ln_g': 'new_m', 'new_m_ln_b': 'new_m', 'new_m_ple_w': 'new_m', 'new_m_ple_gate_w': 'new_m', 'new_v_pool_w': 'new_v', 'new_v_pool_scale': 'new_v', 'new_v_ssm_in_w': 'new_v', 'new_v_ssm_conv_w': 'new_v', 'new_v_ssm_conv_b': 'new_v', 'new_v_ssm_dt_bias': 'new_v', 'new_v_ssm_a_log': 'new_v', 'new_v_ssm_d': 'new_v', 'new_v_ssm_norm_w': 'new_v', 'new_v_ssm_out_w': 'new_v', 'new_v_mlp_w1': 'new_v', 'new_v_mlp_w2': 'new_v', 'new_v_ln_g': 'new_v', 'new_v_ln_b': 'new_v', 'new_v_ple_w': 'new_v', 'new_v_ple_gate_w': 'new_v'}


def _forward(args):
    return _fwd_reference(*[args[k] for k in FWD_PARAMS])


def _output_shape():
    out = _jax.eval_shape(lambda: _forward(_fwd_setup_inputs(0)))
    return out.shape, out.dtype

N_MICROBATCH = 1
ADAM_LR = 0.001
ADAM_B1 = 0.9
ADAM_B2 = 0.999
ADAM_EPS = 1e-08
ADAM_WD = 0.01
ADAM_STEP = 10
PER_EXAMPLE_BATCH_AXIS = {'x': 0, 'p': 1, 'loss_target': 0}
SHARED_INPUTS = []
_WEIGHT_DTYPES = {'pool_w': _jnp.float32, 'pool_scale': _jnp.float32, 'ssm_in_w': _jnp.float32, 'ssm_conv_w': _jnp.float32, 'ssm_conv_b': _jnp.float32, 'ssm_dt_bias': _jnp.float32, 'ssm_a_log': _jnp.float32, 'ssm_d': _jnp.float32, 'ssm_norm_w': _jnp.float32, 'ssm_out_w': _jnp.float32, 'mlp_w1': _jnp.float32, 'mlp_w2': _jnp.float32, 'ln_g': _jnp.float32, 'ln_b': _jnp.float32, 'ple_w': _jnp.float32, 'ple_gate_w': _jnp.float32}
MOMENT_SCALE = {'pool_w': 4.795731e-02, 'pool_scale': 7.132638e-02, 'ssm_in_w': 1.838835e-02, 'ssm_conv_w': 2.576104e-02, 'ssm_conv_b': 6.048379e-02, 'ssm_dt_bias': 5.425287e-02, 'ssm_a_log': 2.937551e-01, 'ssm_d': 1.068438e-01, 'ssm_norm_w': 4.587456e-02, 'ssm_out_w': 1.127824e-01, 'mlp_w1': 2.019196e-02, 'mlp_w2': 2.578465e-01, 'ln_g': 4.377219e+00, 'ln_b': 1.211068e+00, 'ple_w': 9.309043e-02, 'ple_gate_w': 1.249492e-01}


def _to_microbatches(a, axis):
    t = _jnp.moveaxis(a, axis, 0)
    t = t.reshape((N_MICROBATCH, t.shape[0] // N_MICROBATCH) + t.shape[1:])
    return _jnp.moveaxis(t, 1, axis + 1)


def setup_inputs(seed: int = 0) -> dict:
    inp = _fwd_setup_inputs(seed)
    key = _jax.random.fold_in(_jax.random.key(seed), 7919)
    shape, _ = _output_shape()
    out = dict(inp)
    out["loss_target"] = _jax.random.normal(_jax.random.fold_in(key, 0), shape, _jnp.float32)
    for i, name in enumerate(TWIN_WEIGHTS):
        w = inp[name].astype(_jnp.float32)
        if MOMENT_SCALE is None:
            s = _jnp.sqrt(_jnp.mean(_jnp.square(w)) + 1e-30)
        else:
            s = MOMENT_SCALE[name]
        km, kv = _jax.random.split(_jax.random.fold_in(key, i + 1))
        out[name] = w
        out["m_" + name] = s * _jax.random.normal(km, w.shape, _jnp.float32)
        out["v_" + name] = (s * s) * _jax.random.uniform(kv, w.shape, _jnp.float32, 0.5, 1.5)
    if N_MICROBATCH > 1:
        for name, axis in PER_EXAMPLE_BATCH_AXIS.items():
            out[name] = _to_microbatches(out[name], axis)
    return {'x': out['x'], 'p': out['p'], 'pool_w': out['pool_w'], 'pool_scale': out['pool_scale'], 'ssm_in_w': out['ssm_in_w'], 'ssm_conv_w': out['ssm_conv_w'], 'ssm_conv_b': out['ssm_conv_b'], 'ssm_dt_bias': out['ssm_dt_bias'], 'ssm_a_log': out['ssm_a_log'], 'ssm_d': out['ssm_d'], 'ssm_norm_w': out['ssm_norm_w'], 'ssm_out_w': out['ssm_out_w'], 'mlp_w1': out['mlp_w1'], 'mlp_w2': out['mlp_w2'], 'ln_g': out['ln_g'], 'ln_b': out['ln_b'], 'ple_w': out['ple_w'], 'ple_gate_w': out['ple_gate_w'], 'loss_target': out['loss_target'], 'm_pool_w': out['m_pool_w'], 'm_pool_scale': out['m_pool_scale'], 'm_ssm_in_w': out['m_ssm_in_w'], 'm_ssm_conv_w': out['m_ssm_conv_w'], 'm_ssm_conv_b': out['m_ssm_conv_b'], 'm_ssm_dt_bias': out['m_ssm_dt_bias'], 'm_ssm_a_log': out['m_ssm_a_log'], 'm_ssm_d': out['m_ssm_d'], 'm_ssm_norm_w': out['m_ssm_norm_w'], 'm_ssm_out_w': out['m_ssm_out_w'], 'm_mlp_w1': out['m_mlp_w1'], 'm_mlp_w2': out['m_mlp_w2'], 'm_ln_g': out['m_ln_g'], 'm_ln_b': out['m_ln_b'], 'm_ple_w': out['m_ple_w'], 'm_ple_gate_w': out['m_ple_gate_w'], 'v_pool_w': out['v_pool_w'], 'v_pool_scale': out['v_pool_scale'], 'v_ssm_in_w': out['v_ssm_in_w'], 'v_ssm_conv_w': out['v_ssm_conv_w'], 'v_ssm_conv_b': out['v_ssm_conv_b'], 'v_ssm_dt_bias': out['v_ssm_dt_bias'], 'v_ssm_a_log': out['v_ssm_a_log'], 'v_ssm_d': out['v_ssm_d'], 'v_ssm_norm_w': out['v_ssm_norm_w'], 'v_ssm_out_w': out['v_ssm_out_w'], 'v_mlp_w1': out['v_mlp_w1'], 'v_mlp_w2': out['v_mlp_w2'], 'v_ln_g': out['v_ln_g'], 'v_ln_b': out['v_ln_b'], 'v_ple_w': out['v_ple_w'], 'v_ple_gate_w': out['v_ple_gate_w']}


def _loss(weights, diff, rest, loss_target):
    with _jax.named_scope("forward"):
        args = {**rest, TWIN_DIFF_INPUT: diff, **{k: w.astype(_WEIGHT_DTYPES[k]) for k, w in weights.items()}}
        y = _forward(args)
    with _jax.named_scope("loss_head"):
        err = _jnp.square(y.astype(_jnp.float32) - loss_target)
        return 0.5 * _jnp.sum(_jnp.mean(err, axis=-1)) if err.ndim else 0.5 * err


def _adamw(w, g, m, v):
    m = ADAM_B1 * m + (1.0 - ADAM_B1) * g
    v = ADAM_B2 * v + (1.0 - ADAM_B2) * _jnp.square(g)
    m_hat = m / (1.0 - ADAM_B1 ** ADAM_STEP)
    v_hat = v / (1.0 - ADAM_B2 ** ADAM_STEP)
    delta = -ADAM_LR * (m_hat / (_jnp.sqrt(v_hat) + ADAM_EPS) + ADAM_WD * w)
    return delta, m, v


def reference(x, p, pool_w, pool_scale, ssm_in_w, ssm_conv_w, ssm_conv_b, ssm_dt_bias, ssm_a_log, ssm_d, ssm_norm_w, ssm_out_w, mlp_w1, mlp_w2, ln_g, ln_b, ple_w, ple_gate_w, loss_target, m_pool_w, m_pool_scale, m_ssm_in_w, m_ssm_conv_w, m_ssm_conv_b, m_ssm_dt_bias, m_ssm_a_log, m_ssm_d, m_ssm_norm_w, m_ssm_out_w, m_mlp_w1, m_mlp_w2, m_ln_g, m_ln_b, m_ple_w, m_ple_gate_w, v_pool_w, v_pool_scale, v_ssm_in_w, v_ssm_conv_w, v_ssm_conv_b, v_ssm_dt_bias, v_ssm_a_log, v_ssm_d, v_ssm_norm_w, v_ssm_out_w, v_mlp_w1, v_mlp_w2, v_ln_g, v_ln_b, v_ple_w, v_ple_gate_w):
    given = dict(x=x, p=p, pool_w=pool_w, pool_scale=pool_scale, ssm_in_w=ssm_in_w, ssm_conv_w=ssm_conv_w, ssm_conv_b=ssm_conv_b, ssm_dt_bias=ssm_dt_bias, ssm_a_log=ssm_a_log, ssm_d=ssm_d, ssm_norm_w=ssm_norm_w, ssm_out_w=ssm_out_w, mlp_w1=mlp_w1, mlp_w2=mlp_w2, ln_g=ln_g, ln_b=ln_b, ple_w=ple_w, ple_gate_w=ple_gate_w, loss_target=loss_target, m_pool_w=m_pool_w, m_pool_scale=m_pool_scale, m_ssm_in_w=m_ssm_in_w, m_ssm_conv_w=m_ssm_conv_w, m_ssm_conv_b=m_ssm_conv_b, m_ssm_dt_bias=m_ssm_dt_bias, m_ssm_a_log=m_ssm_a_log, m_ssm_d=m_ssm_d, m_ssm_norm_w=m_ssm_norm_w, m_ssm_out_w=m_ssm_out_w, m_mlp_w1=m_mlp_w1, m_mlp_w2=m_mlp_w2, m_ln_g=m_ln_g, m_ln_b=m_ln_b, m_ple_w=m_ple_w, m_ple_gate_w=m_ple_gate_w, v_pool_w=v_pool_w, v_pool_scale=v_pool_scale, v_ssm_in_w=v_ssm_in_w, v_ssm_conv_w=v_ssm_conv_w, v_ssm_conv_b=v_ssm_conv_b, v_ssm_dt_bias=v_ssm_dt_bias, v_ssm_a_log=v_ssm_a_log, v_ssm_d=v_ssm_d, v_ssm_norm_w=v_ssm_norm_w, v_ssm_out_w=v_ssm_out_w, v_mlp_w1=v_mlp_w1, v_mlp_w2=v_mlp_w2, v_ln_g=v_ln_g, v_ln_b=v_ln_b, v_ple_w=v_ple_w, v_ple_gate_w=v_ple_gate_w)
    weights = {n: given[n] for n in TWIN_WEIGHTS}
    shared = {n: given[n] for n in SHARED_INPUTS}
    per_example = {n: given[n] for n in ['x', 'p']}
    grad_fn = _jax.value_and_grad(_loss, argnums=(0, 1))

    def one_microbatch(ex, loss_target):
        ex = dict(ex)
        diff = ex.pop(TWIN_DIFF_INPUT)
        return grad_fn(weights, diff, {**shared, **ex}, loss_target)

    if N_MICROBATCH == 1:
        loss, (grad_w, grad_x) = one_microbatch(per_example, given["loss_target"])
    else:
        def body(carry, xs):
            loss_sum, grad_sum = carry
            l_k, (gw_k, gx_k) = one_microbatch(xs[0], xs[1])
            with _jax.named_scope("update"):
                return (loss_sum + l_k, _jax.tree.map(_jnp.add, grad_sum, gw_k)), gx_k

        init = (_jnp.zeros((), _jnp.float32), _jax.tree.map(_jnp.zeros_like, weights))
        (loss, grad_w), grad_x = _jax.lax.scan(body, init, (per_example, given["loss_target"]))
    with _jax.named_scope("update"):
        delta_w, new_m, new_v = {}, {}, {}
        for n in TWIN_WEIGHTS:
            delta_w[n], new_m[n], new_v[n] = _adamw(weights[n], grad_w[n], given["m_" + n], given["v_" + n])
    return (loss, grad_x, *[grad_w[n] for n in TWIN_WEIGHTS], *[delta_w[n] for n in TWIN_WEIGHTS],
            *[new_m[n] for n in TWIN_WEIGHTS], *[new_v[n] for n in TWIN_WEIGHTS])
```

```python
import functools

import jax
import jax.numpy as jnp
from jax import lax
from jax.experimental import pallas as pl
from jax.experimental.pallas import tpu as pltpu

F32 = jnp.float32
BF16 = jnp.bfloat16
HI = lax.Precision.HIGHEST
MESH = pl.DeviceIdType.MESH

DEPTH = 2
ALPHA = (2.0 * DEPTH) ** 0.25
LN_EPS = 1e-5
RMS_EPS = 1e-5
POOL_WINDOWS = (2, 4, 8, 16)
POOL_HALO = 16
HEAD_DIM = 64
N_GROUPS = 8
D_STATE = 128
CHUNK = 128
CONV_K = 4
CONV_HALO = 8
HEAD_PAD = 128
ROW_PAD = 8
N_CHIPS = 4
N_DEV = 8
ADAM_LR = 0.001
ADAM_B1 = 0.9
ADAM_B2 = 0.999
ADAM_EPS = 1e-08
ADAM_WD = 0.01
ADAM_STEP = 10
VMEM_LIMIT = 56 * 1024 * 1024
ADAM_BLOCK_BYTES = 1024 * 1024


def _cp(*sem):
    return pltpu.CompilerParams(dimension_semantics=sem, vmem_limit_bytes=VMEM_LIMIT)


def _pick(dim, pref):
    t = pref
    while t >= 128:
        if dim % t == 0:
            return t
        t //= 2
    return dim


def _rows(rows, row_bytes, budget):
    t = rows
    while t * row_bytes > budget and t % 16 == 0:
        t //= 2
    return t


def _sigmoid(v):
    return 1.0 / (1.0 + jnp.exp(-v))


_DIMS = {"nn": (((1,), (0,)), ((), ())), "nt": (((1,), (1,)), ((), ())), "tn": (((0,), (0,)), ((), ()))}


def _mm(name, form, a, b, M, N, K, *, tm=1024, tn=1024, tk=512, a_spec=None, b_spec=None,
        o_shape=None, o_spec=None, out_dtype=F32, pro=None, epi=None, extras=()):
    tm, tn, tk = _pick(M, tm), _pick(N, tn), _pick(K, tk)
    nk = K // tk
    if a_spec is None:
        a_spec = (pl.BlockSpec((tk, tm), lambda i, j, k: (k, i)) if form == "tn"
                  else pl.BlockSpec((tm, tk), lambda i, j, k: (i, k)))
    else:
        a_spec = a_spec(tm, tn, tk)
    if b_spec is None:
        b_spec = (pl.BlockSpec((tn, tk), lambda i, j, k: (j, k)) if form == "nt"
                  else pl.BlockSpec((tk, tn), lambda i, j, k: (k, j)))
    else:
        b_spec = b_spec(tm, tn, tk)
    if o_spec is None:
        o_spec = pl.BlockSpec((tm, tn), lambda i, j, k: (i, j))
        o_shape = (M, N)
    else:
        o_spec = o_spec(tm, tn, tk)
    ex_arrays = [e for e in extras]
    ex_specs = [pl.BlockSpec((tm, tn), lambda i, j, k: (i, j)) for _ in extras]
    ne = len(ex_arrays)
    dims = _DIMS[form]

    def body(a_ref, b_ref, *rest):
        ex_refs = rest[:ne]
        o_ref = rest[ne]
        at = a_ref[...]
        if pro is not None:
            at = pro(at)
        p = lax.dot_general(at.astype(BF16), b_ref[...].astype(BF16), dims, preferred_element_type=F32)

        def finish(acc):
            if epi is not None:
                acc = epi(acc, *[r[...] for r in ex_refs])
            o_ref[...] = acc.astype(out_dtype)

        if nk == 1:
            finish(p)
        else:
            acc_ref = rest[ne + 1]
            k = pl.program_id(2)

            @pl.when(k == 0)
            def _():
                acc_ref[...] = p

            @pl.when(k > 0)
            def _():
                acc_ref[...] += p

            @pl.when(k == nk - 1)
            def _():
                finish(acc_ref[...])

    return pl.pallas_call(
        body, name=name, grid=(M // tm, N // tn, nk),
        in_specs=[a_spec, b_spec] + ex_specs, out_specs=o_spec,
        out_shape=jax.ShapeDtypeStruct(o_shape, out_dtype),
        scratch_shapes=[pltpu.VMEM((tm, tn), F32)] if nk > 1 else [],
        compiler_params=_cp("parallel", "parallel", "arbitrary"),
    )(a, b, *ex_arrays)


def _colshard_b(l, n_per):
    def make(tm, tn, tk):
        nb = n_per // tn
        return pl.BlockSpec((None, None, tk, tn), lambda i, j, k: (j // nb, l, k, j % nb))
    return make


def _colshard_bt(l, n_per):
    def make(tm, tn, tk):
        nb = n_per // tk
        return pl.BlockSpec((None, None, tn, tk), lambda i, j, k: (k // nb, l, j, k % nb))
    return make


def _colshard_o(n_per):
    def make(tm, tn, tk):
        nb = n_per // tn
        return pl.BlockSpec((None, tm, tn), lambda i, j, k: (j // nb, i, j % nb))
    return make


def _rowshard_b(l, k_per):
    def make(tm, tn, tk):
        nb = k_per // tk
        return pl.BlockSpec((None, None, tk, tn), lambda i, j, k: (k // nb, l, k % nb, j))
    return make


def _rowshard_bt(l, k_per):
    def make(tm, tn, tk):
        nb = k_per // tn
        return pl.BlockSpec((None, None, tn, tk), lambda i, j, k: (j // nb, l, j % nb, k))
    return make


def _res_ln(name, xprev, h, g, b, scale=None):
    T, D = xprev.shape
    tr = _pick(T, 256)
    row = pl.BlockSpec((tr, D), lambda i: (i, 0))
    vec = pl.BlockSpec((1, D), lambda i: (0, 0))
    has_scale = scale is not None

    def body(*refs):
        if has_scale:
            x_ref, h_ref, s_ref, g_ref, b_ref, y_ref, yb_ref, xh_ref, rs_ref = refs
            hh = h_ref[...] * s_ref[...]
        else:
            x_ref, h_ref, g_ref, b_ref, y_ref, yb_ref, xh_ref, rs_ref = refs
            hh = h_ref[...]
        u = ALPHA * x_ref[...] + hh
        mu = jnp.mean(u, axis=-1, keepdims=True)
        d = u - mu
        var = jnp.mean(d * d, axis=-1, keepdims=True)
        rs = lax.rsqrt(var + LN_EPS)
        xh = d * rs
        y = xh * g_ref[...] + b_ref[...]
        y_ref[...] = y
        yb_ref[...] = y.astype(BF16)
        xh_ref[...] = xh
        rs_ref[...] = rs

    ins = [xprev, h] + ([scale] if has_scale else []) + [g, b]
    specs = [row, row] + ([vec] if has_scale else []) + [vec, vec]
    return pl.pallas_call(
        body, name=name, grid=(T // tr,), in_specs=specs,
        out_specs=[row, row, row, pl.BlockSpec((tr, 1), lambda i: (i, 0))],
        out_shape=[jax.ShapeDtypeStruct((T, D), F32), jax.ShapeDtypeStruct((T, D), BF16),
                   jax.ShapeDtypeStruct((T, D), F32), jax.ShapeDtypeStruct((T, 1), F32)],
        compiler_params=_cp("parallel"),
    )(*ins)


def _accum(ref, part, first):
    @pl.when(first)
    def _():
        ref[...] = part

    @pl.when(jnp.logical_not(first))
    def _():
        ref[...] += part


def _ln_bwd(name, dy, xh, rs, g, hraw=None, scale=None):
    T, D = dy.shape
    tr = _pick(T, 256)
    row = pl.BlockSpec((tr, D), lambda i: (i, 0))
    vec = pl.BlockSpec((1, D), lambda i: (0, 0))
    has_scale = scale is not None

    def body(*refs):
        if has_scale:
            dy_ref, xh_ref, rs_ref, g_ref, hr_ref, s_ref, du_ref, dub_ref, dg_ref, db_ref, ds_ref = refs
        else:
            dy_ref, xh_ref, rs_ref, g_ref, du_ref, dub_ref, dg_ref, db_ref = refs
        first = pl.program_id(0) == 0
        dyv = dy_ref[...]
        xhv = xh_ref[...]
        dxh = dyv * g_ref[...]
        m1 = jnp.mean(dxh, axis=-1, keepdims=True)
        m2 = jnp.mean(dxh * xhv, axis=-1, keepdims=True)
        du = rs_ref[...] * (dxh - m1 - xhv * m2)
        du_ref[...] = du
        if has_scale:
            dub_ref[...] = (du * s_ref[...]).astype(BF16)
            _accum(ds_ref, jnp.sum(du * hr_ref[...], axis=0, keepdims=True), first)
        else:
            dub_ref[...] = du.astype(BF16)
        _accum(dg_ref, jnp.sum(dyv * xhv, axis=0, keepdims=True), first)
        _accum(db_ref, jnp.sum(dyv, axis=0, keepdims=True), first)

    ins = [dy, xh, rs, g] + ([hraw, scale] if has_scale else [])
    specs = [row, row, pl.BlockSpec((tr, 1), lambda i: (i, 0)), vec] + ([row, vec] if has_scale else [])
    n_vec = 3 if has_scale else 2
    return pl.pallas_call(
        body, name=name, grid=(T // tr,), in_specs=specs,
        out_specs=[row, row] + [vec] * n_vec,
        out_shape=[jax.ShapeDtypeStruct((T, D), F32), jax.ShapeDtypeStruct((T, D), BF16)]
        + [jax.ShapeDtypeStruct((1, D), F32)] * n_vec,
        compiler_params=_cp("arbitrary"),
    )(*ins)


def _gate_fwd(name, x, gl, e):
    T, D = x.shape
    tr = _pick(T, 256)
    row = pl.BlockSpec((tr, D), lambda i: (i, 0))

    def body(x_ref, gl_ref, e_ref, xo_ref, xob_ref, gate_ref):
        gate = _sigmoid(gl_ref[...])
        xo = x_ref[...] + gate * e_ref[...]
        xo_ref[...] = xo
        xob_ref[...] = xo.astype(BF16)
        gate_ref[...] = gate

    return pl.pallas_call(
        body, name=name, grid=(T // tr,), in_specs=[row, row, row], out_specs=[row, row, row],
        out_shape=[jax.ShapeDtypeStruct((T, D), F32), jax.ShapeDtypeStruct((T, D), BF16),
                   jax.ShapeDtypeStruct((T, D), F32)],
        compiler_params=_cp("parallel"),
    )(x, gl, e)


def _gate_loss(name, x, gl, e, tgt):
    T, D = x.shape
    tr = _pick(T, 256)
    row = pl.BlockSpec((tr, D), lambda i: (i, 0))

    def body(x_ref, gl_ref, e_ref, t_ref, dy_ref, gate_ref, lp_ref):
        gate = _sigmoid(gl_ref[...])
        err = x_ref[...] + gate * e_ref[...] - t_ref[...]
        dy_ref[...] = err * (1.0 / D)
        gate_ref[...] = gate
        s = jnp.sum(jnp.mean(err * err, axis=-1, keepdims=True), axis=0, keepdims=True)
        lp_ref[...] = jnp.broadcast_to(0.5 * s, (8, 128))

    return pl.pallas_call(
        body, name=name, grid=(T // tr,), in_specs=[row] * 4,
        out_specs=[row, row, pl.BlockSpec((8, 128), lambda i: (i, 0))],
        out_shape=[jax.ShapeDtypeStruct((T, D), F32), jax.ShapeDtypeStruct((T, D), F32),
                   jax.ShapeDtypeStruct((T // tr * 8, 128), F32)],
        compiler_params=_cp("parallel"),
    )(x, gl, e, tgt)


def _gate_bwd(name, dxo, gate, e):
    T, D = dxo.shape
    tr = _pick(T, 256)
    row = pl.BlockSpec((tr, D), lambda i: (i, 0))

    def body(d_ref, gate_ref, e_ref, dgl_ref, de_ref):
        d = d_ref[...]
        gate = gate_ref[...]
        dgl_ref[...] = (d * e_ref[...] * gate * (1.0 - gate)).astype(BF16)
        de_ref[...] = (d * gate).astype(BF16)

    return pl.pallas_call(
        body, name=name, grid=(T // tr,), in_specs=[row] * 3, out_specs=[row, row],
        out_shape=[jax.ShapeDtypeStruct((T, D), BF16)] * 2,
        compiler_params=_cp("parallel"),
    )(dxo, gate, e)


def _gnorm_fwd(name, y, z, w):
    T, DI = y.shape
    tr = _pick(T, 128)
    row = pl.BlockSpec((tr, DI), lambda i: (i, 0))
    vec = pl.BlockSpec((1, DI), lambda i: (0, 0))
    col = pl.BlockSpec((tr, 1), lambda i: (i, 0))

    def body(y_ref, z_ref, w_ref, o_ref, rs_ref):
        zv = z_ref[...]
        yz = y_ref[...] * (zv * _sigmoid(zv))
        rs = lax.rsqrt(jnp.mean(yz * yz, axis=-1, keepdims=True) + RMS_EPS)
        o_ref[...] = (yz * rs * w_ref[...]).astype(BF16)
        rs_ref[...] = rs

    return pl.pallas_call(
        body, name=name, grid=(T // tr,), in_specs=[row, row, vec], out_specs=[row, col],
        out_shape=[jax.ShapeDtypeStruct((T, DI), BF16), jax.ShapeDtypeStruct((T, 1), F32)],
        compiler_params=_cp("parallel"),
    )(y, z, w)


def _gnorm_bwd(name, dyn, y, z, w, rs):
    T, DI = y.shape
    tr = _pick(T, 128)
    row = pl.BlockSpec((tr, DI), lambda i: (i, 0))
    vec = pl.BlockSpec((1, DI), lambda i: (0, 0))
    col = pl.BlockSpec((tr, 1), lambda i: (i, 0))

    def body(d_ref, y_ref, z_ref, w_ref, rs_ref, dy_ref, dz_ref, dw_ref):
        first = pl.program_id(0) == 0
        zv = z_ref[...]
        yv = y_ref[...]
        sg = _sigmoid(zv)
        sz = zv * sg
        rsv = rs_ref[...]
        yzh = yv * sz * rsv
        dv = d_ref[...]
        gw = dv * w_ref[...]
        m = jnp.mean(gw * yzh, axis=-1, keepdims=True)
        dyz = rsv * (gw - yzh * m)
        dy_ref[...] = dyz * sz
        dz_ref[...] = (dyz * yv * (sg * (1.0 + zv * (1.0 - sg)))).astype(BF16)
        _accum(dw_ref, jnp.sum(dv * yzh, axis=0, keepdims=True), first)

    return pl.pallas_call(
        body, name=name, grid=(T // tr,), in_specs=[row, row, row, vec, col], out_specs=[row, row, vec],
        out_shape=[jax.ShapeDtypeStruct((T, DI), F32), jax.ShapeDtypeStruct((T, DI), BF16),
                   jax.ShapeDtypeStruct((1, DI), F32)],
        compiler_params=_cp("arbitrary"),
    )(dyn, y, z, w, rs)


def _sel4(j, vals):
    return jnp.where(j == 0, vals[0], jnp.where(j == 1, vals[1], jnp.where(j == 2, vals[2], vals[3])))


def _pool_cnt(i, j, tr, rows, offset):
    t = i * tr + offset + lax.broadcasted_iota(jnp.int32, (rows, 1), 0)
    win = _sel4(j, POOL_WINDOWS)
    return jnp.minimum(t + 1, win).astype(F32)


def _pool_fwd(name, x):
    T, D = x.shape
    gd = D // len(POOL_WINDOWS)
    tr = _pick(T, 512)
    hb = tr // POOL_HALO

    def body(x_ref, h_ref, o_ref):
        i, j = pl.program_id(0), pl.program_id(1)
        xv = x_ref[...]
        halo = jnp.where(i > 0, h_ref[...], 0.0)
        cat = jnp.concatenate([halo, xv], axis=0)
        s2 = cat + pltpu.roll(cat, 1, 0)
        s4 = s2 + pltpu.roll(s2, 2, 0)
        s8 = s4 + pltpu.roll(s4, 4, 0)
        s16 = s8 + pltpu.roll(s8, 8, 0)
        sel = _sel4(j, (s2, s4, s8, s16))[POOL_HALO:]
        o_ref[...] = (sel / _pool_cnt(i, j, tr, tr, 0) - xv).astype(BF16)

    return pl.pallas_call(
        body, name=name, grid=(T // tr, len(POOL_WINDOWS)),
        in_specs=[pl.BlockSpec((tr, gd), lambda i, j: (i, j)),
                  pl.BlockSpec((POOL_HALO, gd), lambda i, j: (jnp.maximum(i * hb - 1, 0), j))],
        out_specs=pl.BlockSpec((tr, gd), lambda i, j: (i, j)),
        out_shape=jax.ShapeDtypeStruct((T, D), BF16),
        compiler_params=_cp("parallel", "parallel"),
    )(x, x)


def _pool_bwd(name, dp, du):
    T, D = dp.shape
    gd = D // len(POOL_WINDOWS)
    tr = _pick(T, 512)
    hb = tr // POOL_HALO
    last_h = T // POOL_HALO - 1
    n = tr + POOL_HALO

    def body(dp_ref, h_ref, du_ref, o_ref):
        i, j = pl.program_id(0), pl.program_id(1)
        dpv = dp_ref[...]
        q = dpv / _pool_cnt(i, j, tr, tr, 0)
        qh = jnp.where(i < pl.num_programs(0) - 1, h_ref[...] / _pool_cnt(i, j, tr, POOL_HALO, tr), 0.0)
        cat = jnp.concatenate([q, qh], axis=0)
        f2 = cat + pltpu.roll(cat, n - 1, 0)
        f4 = f2 + pltpu.roll(f2, n - 2, 0)
        f8 = f4 + pltpu.roll(f4, n - 4, 0)
        f16 = f8 + pltpu.roll(f8, n - 8, 0)
        sel = _sel4(j, (f2, f4, f8, f16))[:tr]
        o_ref[...] = ALPHA * du_ref[...] + sel - dpv

    return pl.pallas_call(
        body, name=name, grid=(T // tr, len(POOL_WINDOWS)),
        in_specs=[pl.BlockSpec((tr, gd), lambda i, j: (i, j)),
                  pl.BlockSpec((POOL_HALO, gd), lambda i, j: (jnp.minimum((i + 1) * hb, last_h), j)),
                  pl.BlockSpec((tr, gd), lambda i, j: (i, j))],
        out_specs=pl.BlockSpec((tr, gd), lambda i, j: (i, j)),
        out_shape=jax.ShapeDtypeStruct((T, D), F32),
        compiler_params=_cp("parallel", "parallel"),
    )(dp, dp, du)


def _conv_taps(cat, wv, rows):
    shifted = [cat[CONV_HALO:] if s == 0 else pltpu.roll(cat, s, 0)[CONV_HALO:] for s in range(CONV_K)]
    acc = shifted[0] * wv[CONV_K - 1:CONV_K]
    for s in range(1, CONV_K):
        acc = acc + shifted[s] * wv[CONV_K - 1 - s:CONV_K - s]
    return acc, shifted


def _conv_fwd(name, xp, w, b):
    T, CD = xp.shape
    tr, tc = _pick(T, 512), _pick(CD, 512)
    hb = tr // CONV_HALO

    def body(x_ref, h_ref, w_ref, b_ref, o_ref):
        i = pl.program_id(0)
        halo = jnp.where(i > 0, h_ref[...], 0.0)
        cat = jnp.concatenate([halo, x_ref[...]], axis=0)
        acc, _ = _conv_taps(cat, w_ref[...], tr)
        acc = acc + b_ref[...]
        o_ref[...] = acc * _sigmoid(acc)

    return pl.pallas_call(
        body, name=name, grid=(T // tr, CD // tc),
        in_specs=[pl.BlockSpec((tr, tc), lambda i, j: (i, j)),
                  pl.BlockSpec((CONV_HALO, tc), lambda i, j: (jnp.maximum(i * hb - 1, 0), j)),
                  pl.BlockSpec((CONV_K, tc), lambda i, j: (0, j)),
                  pl.BlockSpec((1, tc), lambda i, j: (0, j))],
        out_specs=pl.BlockSpec((tr, tc), lambda i, j: (i, j)),
        out_shape=jax.ShapeDtypeStruct((T, CD), F32),
        compiler_params=_cp("parallel", "parallel"),
    )(xp, xp, w, b)


def _conv_bwd_a(name, dxs, dbm, dcm, xp, w, b):
    T, CD = xp.shape
    tr = _pick(T, 512)
    tc = _pick(dbm.shape[1], 512)
    hb = tr // CONV_HALO
    nx, nb = dxs.shape[1] // tc, dbm.shape[1] // tc

    def part_spec(lo, n):
        def imap(j, i):
            inside = jnp.logical_and(j >= lo, j < lo + n)
            return (jnp.where(inside, i, 0), jnp.clip(j - lo, 0, n - 1))
        return pl.BlockSpec((tr, tc), imap)

    def body(dx_ref, db_ref, dc_ref, x_ref, h_ref, w_ref, b_ref, o_ref, dw_ref, dbias_ref):
        j, i = pl.program_id(0), pl.program_id(1)
        first = i == 0
        d = jnp.where(j < nx, dx_ref[...], jnp.where(j < nx + nb, db_ref[...], dc_ref[...]))
        halo = jnp.where(i > 0, h_ref[...], 0.0)
        cat = jnp.concatenate([halo, x_ref[...]], axis=0)
        acc, shifted = _conv_taps(cat, w_ref[...], tr)
        acc = acc + b_ref[...]
        sg = _sigmoid(acc)
        dconv = d * (sg * (1.0 + acc * (1.0 - sg)))
        o_ref[...] = dconv
        _accum(dbias_ref, jnp.sum(dconv, axis=0, keepdims=True), first)
        tap = lax.broadcasted_iota(jnp.int32, (CONV_K, tc), 0)
        dwv = jnp.zeros((CONV_K, tc), F32)
        for s in range(CONV_K):
            dwv = jnp.where(tap == CONV_K - 1 - s, jnp.sum(dconv * shifted[s], axis=0, keepdims=True), dwv)
        _accum(dw_ref, dwv, first)

    return pl.pallas_call(
        body, name=name, grid=(CD // tc, T // tr),
        in_specs=[part_spec(0, nx), part_spec(nx, nb), part_spec(nx + nb, nb),
                  pl.BlockSpec((tr, tc), lambda j, i: (i, j)),
                  pl.BlockSpec((CONV_HALO, tc), lambda j, i: (jnp.maximum(i * hb - 1, 0), j)),
                  pl.BlockSpec((CONV_K, tc), lambda j, i: (0, j)),
                  pl.BlockSpec((1, tc), lambda j, i: (0, j))],
        out_specs=[pl.BlockSpec((tr, tc), lambda j, i: (i, j)),
                   pl.BlockSpec((CONV_K, tc), lambda j, i: (0, j)),
                   pl.BlockSpec((1, tc), lambda j, i: (0, j))],
        out_shape=[jax.ShapeDtypeStruct((T, CD), F32), jax.ShapeDtypeStruct((CONV_K, CD), F32),
                   jax.ShapeDtypeStruct((1, CD), F32)],
        compiler_params=_cp("parallel", "arbitrary"),
    )(dxs, dbm, dcm, xp, xp, w, b)


def _conv_bwd_b(name, dconv, w):
    T, CD = dconv.shape
    tr, tc = _pick(T, 512), _pick(CD, 512)
    hb = tr // CONV_HALO
    last_h = T // CONV_HALO - 1
    n = tr + CONV_HALO

    def body(d_ref, h_ref, w_ref, o_ref):
        i = pl.program_id(0)
        halo = jnp.where(i < pl.num_programs(0) - 1, h_ref[...], 0.0)
        cat = jnp.concatenate([d_ref[...], halo], axis=0)
        wv = w_ref[...]
        acc = cat[:tr] * wv[CONV_K - 1:CONV_K]
        for s in range(1, CONV_K):
            acc = acc + pltpu.roll(cat, n - s, 0)[:tr] * wv[CONV_K - 1 - s:CONV_K - s]
        o_ref[...] = acc.astype(BF16)

    return pl.pallas_call(
        body, name=name, grid=(T // tr, CD // tc),
        in_specs=[pl.BlockSpec((tr, tc), lambda i, j: (i, j)),
                  pl.BlockSpec((CONV_HALO, tc), lambda i, j: (jnp.minimum((i + 1) * hb, last_h), j)),
                  pl.BlockSpec((CONV_K, tc), lambda i, j: (0, j))],
        out_specs=pl.BlockSpec((tr, tc), lambda i, j: (i, j)),
        out_shape=jax.ShapeDtypeStruct((T, CD), BF16),
        compiler_params=_cp("parallel", "parallel"),
    )(dconv, dconv, w)


def _dt_fwd(name, dtp, bias):
    T, H = dtp.shape

    def body(x_ref, b_ref, o_ref):
        v = x_ref[...] + b_ref[...]
        u = jnp.exp(-jnp.abs(v))
        w1 = 1.0 + u
        lp = jnp.where(w1 == 1.0, u, jnp.log(w1) * (u / jnp.where(w1 == 1.0, 1.0, w1 - 1.0)))
        o_ref[...] = jnp.maximum(v, 0.0) + lp

    return pl.pallas_call(body, name=name, out_shape=jax.ShapeDtypeStruct((T, H), F32))(dtp, bias)


def _dt_bwd(name, ddt, dtp, bias):
    T, H = dtp.shape

    def body(d_ref, x_ref, b_ref, o_ref, ob_ref, db_ref):
        g = d_ref[...] * _sigmoid(x_ref[...] + b_ref[...])
        o_ref[...] = g
        ob_ref[...] = g.astype(BF16)
        db_ref[...] = jnp.sum(g, axis=0, keepdims=True)

    return pl.pallas_call(
        body, name=name,
        out_shape=[jax.ShapeDtypeStruct((T, H), F32), jax.ShapeDtypeStruct((T, H), BF16),
                   jax.ShapeDtypeStruct((1, H), F32)],
    )(ddt, dtp, bias)


def _dotf(a, b):
    return jnp.dot(a, b, precision=HI, preferred_element_type=F32)


def _dotb(form, a, b):
    return lax.dot_general(a.astype(BF16), b.astype(BF16), _DIMS[form], preferred_element_type=F32)


def _ssd_common(dtc, dtr, alr, alc, gw):
    li = lax.broadcasted_iota(jnp.int32, (CHUNK, CHUNK), 0)
    si = lax.broadcasted_iota(jnp.int32, (CHUNK, CHUNK), 1)
    tri = (li >= si).astype(F32)
    trit = (li <= si).astype(F32)
    a_row = -jnp.exp(alr)
    a_col = -jnp.exp(alc)
    acs_c = _dotf(tri, dtc * a_row)
    acs_r = _dotf(dtr * a_col, trit)
    eh = lax.broadcasted_iota(jnp.int32, (HEAD_PAD, gw), 0)
    ec = lax.broadcasted_iota(jnp.int32, (HEAD_PAD, gw), 1) // HEAD_DIM
    expand = (eh == ec).astype(F32)
    th = lax.broadcasted_iota(jnp.int32, (gw, HEAD_PAD), 1)
    tc = lax.broadcasted_iota(jnp.int32, (gw, HEAD_PAD), 0) // HEAD_DIM
    reduce_ = (th == tc).astype(F32)
    acs_last = acs_c[CHUNK - 1:CHUNK, :]
    return dict(li=li, si=si, tri=tri, trit=trit, a_row=a_row, acs_c=acs_c, acs_r=acs_r,
                expand=expand, reduce=reduce_, acs_last=acs_last,
                dt_e=_dotf(dtc, expand), eacs_e=jnp.exp(_dotf(acs_c, expand)),
                dec_h=jnp.exp(acs_last - acs_c))


def _ssd_specs(T, DI, gw, nc, rev):
    nsb = DI // D_STATE
    cidx = (lambda c: nc - 1 - c) if rev else (lambda c: c)
    return dict(
        xs=pl.BlockSpec((CHUNK, gw), lambda g, c: (cidx(c), g)),
        bm=pl.BlockSpec((CHUNK, D_STATE), lambda g, c: (cidx(c), nsb + g)),
        cm=pl.BlockSpec((CHUNK, D_STATE), lambda g, c: (cidx(c), nsb + N_GROUPS + g)),
        dtc=pl.BlockSpec((None, CHUNK, HEAD_PAD), lambda g, c: (g, cidx(c), 0)),
        dtr=pl.BlockSpec((None, ROW_PAD, CHUNK), lambda g, c: (g, 0, cidx(c))),
        alr=pl.BlockSpec((None, 1, HEAD_PAD), lambda g, c: (g, 0, 0)),
        alc=pl.BlockSpec((None, ROW_PAD, 1), lambda g, c: (g, 0, 0)),
        de=pl.BlockSpec((None, 1, gw), lambda g, c: (g, 0, 0)),
        hp=pl.BlockSpec((None, None, gw, D_STATE), lambda g, c: (cidx(c), g, 0, 0)),
        bc=pl.BlockSpec((CHUNK, D_STATE), lambda g, c: (cidx(c), g)),
        acc=pl.BlockSpec((None, 1, HEAD_PAD), lambda g, c: (g, 0, 0)),
    )


def _ssd_fwd(name, xbc, dtc, dtr, alr, alc, d_e, DI):
    T = xbc.shape[0]
    nc = T // CHUNK
    gw = DI // N_GROUPS
    hpg = gw // HEAD_DIM
    sp = _ssd_specs(T, DI, gw, nc, False)

    def body(xs_ref, b_ref, c_ref, dtc_ref, dtr_ref, alr_ref, alc_ref, de_ref, y_ref, hp_ref, h_scr):
        @pl.when(pl.program_id(1) == 0)
        def _():
            h_scr[...] = jnp.zeros_like(h_scr)

        hpv = h_scr[...]
        hp_ref[...] = hpv
        xs = xs_ref[...]
        bb = b_ref[...].astype(BF16)
        cb_ = c_ref[...].astype(BF16)
        cm = _ssd_common(dtc_ref[...], dtr_ref[...], alr_ref[...], alc_ref[...], gw)
        x = xs * cm["dt_e"]
        xb = x.astype(BF16)
        cbm = _dotb("nt", cb_, bb)
        causal = cm["li"] >= cm["si"]
        lane_head = lax.broadcasted_iota(jnp.int32, (1, gw), 1) // HEAD_DIM
        ydiag = jnp.zeros((CHUNK, gw), F32)
        for j in range(hpg):
            seg = cm["acs_c"][:, j:j + 1] - cm["acs_r"][j:j + 1, :]
            lmat = jnp.exp(jnp.where(causal, seg, -1e30))
            yj = _dotb("nn", cbm * lmat, xb)
            ydiag = jnp.where(lane_head == j, yj, ydiag)
        dec_e = _dotf(cm["dec_h"], cm["expand"])
        states = _dotb("tn", x * dec_e, bb)
        yoff = _dotb("nt", cb_, hpv) * cm["eacs_e"]
        y_ref[...] = ydiag + yoff + xs * de_ref[...]
        cd_rows = jnp.sum(cm["reduce"] * jnp.exp(cm["acs_last"]), axis=1, keepdims=True)
        h_scr[...] = cd_rows * hpv + states

    return pl.pallas_call(
        body, name=name, grid=(N_GROUPS, nc),
        in_specs=[sp["xs"], sp["bm"], sp["cm"], sp["dtc"], sp["dtr"], sp["alr"], sp["alc"], sp["de"]],
        out_specs=[sp["xs"], sp["hp"]],
        out_shape=[jax.ShapeDtypeStruct((T, DI), F32), jax.ShapeDtypeStruct((nc, N_GROUPS, gw, D_STATE), F32)],
        scratch_shapes=[pltpu.VMEM((gw, D_STATE), F32)],
        compiler_params=_cp("parallel", "arbitrary"),
    )(xbc, xbc, xbc, dtc, dtr, alr, alc, d_e)


def _ssd_bwd(name, dy, xbc, dtc, dtr, alr, alc, d_e, hprev, DI):
    T = xbc.shape[0]
    nc = T // CHUNK
    gw = DI // N_GROUPS
    hpg = gw // HEAD_DIM
    sp = _ssd_specs(T, DI, gw, nc, True)

    def body(dy_ref, xs_ref, b_ref, c_ref, dtc_ref, dtr_ref, alr_ref, alc_ref, de_ref, hp_ref,
             dxs_ref, db_ref, dc_ref, ddt_ref, dal_ref, dd_ref, dh_scr):
        first = pl.program_id(1) == 0

        @pl.when(first)
        def _():
            dh_scr[...] = jnp.zeros_like(dh_scr)

        xs = xs_ref[...]
        dyv = dy_ref[...]
        bb = b_ref[...].astype(BF16)
        cb_ = c_ref[...].astype(BF16)
        dtc_v = dtc_ref[...]
        cm = _ssd_common(dtc_v, dtr_ref[...], alr_ref[...], alc_ref[...], gw)
        hpv = hp_ref[...]
        hpb = hpv.astype(BF16)
        dhn = dh_scr[...]
        dhnb = dhn.astype(BF16)
        x = xs * cm["dt_e"]
        xb = x.astype(BF16)
        cbm = _dotb("nt", cb_, bb)
        causal = cm["li"] >= cm["si"]
        lane_head = lax.broadcasted_iota(jnp.int32, (1, gw), 1) // HEAD_DIM
        lane_pad = lax.broadcasted_iota(jnp.int32, (1, HEAD_PAD), 1)
        sub_pad = lax.broadcasted_iota(jnp.int32, (ROW_PAD, 1), 0)

        dxs = dyv * de_ref[...]
        dd_part = jnp.sum(_dotf(dyv * xs, cm["reduce"]), axis=0, keepdims=True)
        _accum(dd_ref, dd_part, first)

        dx = jnp.zeros((CHUNK, gw), F32)
        dcb = jnp.zeros((CHUNK, CHUNK), F32)
        dacs_c = jnp.zeros((CHUNK, HEAD_PAD), F32)
        dacs_r = jnp.zeros((ROW_PAD, CHUNK), F32)
        for j in range(hpg):
            seg = cm["acs_c"][:, j:j + 1] - cm["acs_r"][j:j + 1, :]
            lmat = jnp.exp(jnp.where(causal, seg, -1e30))
            mmat = cbm * lmat
            dym = jnp.where(lane_head == j, dyv, 0.0).astype(BF16)
            dm = _dotb("nt", dym, xb)
            dx = dx + _dotb("tn", mmat, dym)
            dcb = dcb + dm * lmat
            dseg = dm * mmat
            dacs_c = dacs_c + jnp.where(lane_pad == j, jnp.sum(dseg, axis=1, keepdims=True), 0.0)
            dacs_r = dacs_r - jnp.where(sub_pad == j, jnp.sum(dseg, axis=0, keepdims=True), 0.0)
        dc = _dotb("nn", dcb, bb)
        db = _dotb("tn", dcb, cb_)

        gm = _dotb("nt", cb_, hpb)
        dgm = dyv * cm["eacs_e"]
        dacs_c = dacs_c + _dotf(dgm * gm, cm["reduce"])
        dc = dc + _dotb("nn", dgm, hpb)
        dhp = _dotb("tn", dgm, cb_)

        cd_row = jnp.exp(cm["acs_last"])
        cd_rows = jnp.sum(cm["reduce"] * cd_row, axis=1, keepdims=True)
        dhp = dhp + cd_rows * dhn
        rsum = jnp.sum(dhn * hpv, axis=1, keepdims=True)
        dacs_last = jnp.sum(cm["reduce"] * rsum, axis=0, keepdims=True) * cd_row
        dec_e = _dotf(cm["dec_h"], cm["expand"])
        xdec = x * dec_e
        dxdec = _dotb("nt", bb, dhnb)
        db = db + _dotb("nn", xdec, dhnb)
        dx = dx + dxdec * dec_e
        tdec = _dotf(dxdec * x, cm["reduce"]) * cm["dec_h"]
        dacs_c = dacs_c - tdec
        dacs_last = dacs_last + jnp.sum(tdec, axis=0, keepdims=True)
        row_id = lax.broadcasted_iota(jnp.int32, (CHUNK, 1), 0)
        dacs_c = dacs_c + jnp.where(row_id == CHUNK - 1, dacs_last, 0.0)

        dxs_ref[...] = dxs + dx * cm["dt_e"]
        ddt = _dotf(dx * xs, cm["reduce"])
        dda = _dotf(cm["trit"], dacs_c)
        dda_r = _dotf(dacs_r, cm["tri"])
        dda_rp = jnp.concatenate([dda_r, jnp.zeros((HEAD_PAD - ROW_PAD, CHUNK), F32)], axis=0)
        eye = (cm["li"] == cm["si"]).astype(F32)
        dda = dda + lax.dot_general(eye, dda_rp, _DIMS["nt"], precision=HI, preferred_element_type=F32)
        ddt_ref[...] = ddt + dda * cm["a_row"]
        _accum(dal_ref, jnp.sum(dda * dtc_v, axis=0, keepdims=True) * cm["a_row"], first)
        db_ref[...] = db
        dc_ref[...] = dc
        dh_scr[...] = dhp

    gs = D_STATE * N_GROUPS
    return pl.pallas_call(
        body, name=name, grid=(N_GROUPS, nc),
        in_specs=[sp["xs"], sp["xs"], sp["bm"], sp["cm"], sp["dtc"], sp["dtr"], sp["alr"], sp["alc"],
                  sp["de"], sp["hp"]],
        out_specs=[sp["xs"], sp["bc"], sp["bc"], sp["dtc"], sp["acc"], sp["acc"]],
        out_shape=[jax.ShapeDtypeStruct((T, DI), F32), jax.ShapeDtypeStruct((T, gs), F32),
                   jax.ShapeDtypeStruct((T, gs), F32), jax.ShapeDtypeStruct((N_GROUPS, T, HEAD_PAD), F32),
                   jax.ShapeDtypeStruct((N_GROUPS, 1, HEAD_PAD), F32),
                   jax.ShapeDtypeStruct((N_GROUPS, 1, HEAD_PAD), F32)],
        scratch_shapes=[pltpu.VMEM((gw, D_STATE), F32)],
        compiler_params=_cp("parallel", "arbitrary"),
    )(dy, xbc, xbc, xbc, dtc, dtr, alr, alc, d_e, hprev)


def _adam_math(w, g, m, v):
    m = ADAM_B1 * m + (1.0 - ADAM_B1) * g
    v = ADAM_B2 * v + (1.0 - ADAM_B2) * (g * g)
    m_hat = m / (1.0 - ADAM_B1 ** ADAM_STEP)
    v_hat = v / (1.0 - ADAM_B2 ** ADAM_STEP)
    delta = -ADAM_LR * (m_hat / (jnp.sqrt(v_hat) + ADAM_EPS) + ADAM_WD * w)
    return delta, m, v


def _adamw(name, w, m, v, grads):
    L, R, C = w.shape
    tr = _rows(R, C * 4, ADAM_BLOCK_BYTES)
    full = pl.BlockSpec((None, tr, C), lambda l, i: (l, i, 0))
    g_specs = [pl.BlockSpec((tr, C), functools.partial(lambda l, i, ll: (jnp.where(l == ll, i, 0), 0), ll=ll))
               for ll in range(L)]

    def body(w_ref, m_ref, v_ref, *rest):
        g_refs = rest[:L]
        go_ref, d_ref, mo_ref, vo_ref = rest[L:]
        l = pl.program_id(0)
        g = g_refs[0][...]
        for ll in range(1, L):
            g = jnp.where(l == ll, g_refs[ll][...], g)
        delta, mn, vn = _adam_math(w_ref[...], g, m_ref[...], v_ref[...])
        go_ref[...] = g
        d_ref[...] = delta
        mo_ref[...] = mn
        vo_ref[...] = vn

    return pl.pallas_call(
        body, name=name, grid=(L, R // tr), in_specs=[full] * 3 + g_specs, out_specs=[full] * 4,
        out_shape=[jax.ShapeDtypeStruct((L, R, C), F32)] * 4,
        compiler_params=_cp("arbitrary", "arbitrary"),
    )(w, m, v, *grads)


def _sum_peers(name, gathered):
    n, R, C = gathered.shape

    def body(g_ref, o_ref):
        acc = g_ref[0]
        for d in range(1, n):
            acc = acc + g_ref[d]
        o_ref[...] = acc

    return pl.pallas_call(body, name=name, out_shape=jax.ShapeDtypeStruct((R, C), F32))(gathered)


_ANY = pl.BlockSpec(memory_space=pl.ANY)


def _place():
    x, y, c = lax.axis_index("x"), lax.axis_index("y"), lax.axis_index("c")
    chips = [(1 - x, y), (x, 1 - y), (1 - x, 1 - y)]
    return x, y, c, chips


def _allgather_small(name, v):
    R, C = v.shape

    def body(v_ref, o_ref, send_sems, recv_sems):
        x, y, c, _ = _place()
        me = 4 * x + 2 * y + c
        o_ref[me] = v_ref[...]
        copies = []
        for k in range(1, N_DEV):
            px, py, pc = x ^ (k >> 2), y ^ ((k >> 1) & 1), c ^ (k & 1)
            copies.append(pltpu.make_async_remote_copy(
                src_ref=v_ref, dst_ref=o_ref.at[me], send_sem=send_sems.at[k - 1], recv_sem=recv_sems.at[k - 1],
                device_id=(px, py, pc), device_id_type=MESH))
        for cp in copies:
            cp.start()
        for cp in copies:
            cp.wait()

    return pl.pallas_call(
        body, name=name, out_shape=jax.ShapeDtypeStruct((N_DEV, R, C), F32),
        in_specs=[pl.BlockSpec(memory_space=pltpu.VMEM)], out_specs=pl.BlockSpec(memory_space=pltpu.VMEM),
        scratch_shapes=[pltpu.SemaphoreType.DMA((N_DEV - 1,)), pltpu.SemaphoreType.DMA((N_DEV - 1,))],
    )(v)


def _gather_weights(name, shards):
    n = len(shards)

    def body(*refs):
        src = refs[:n]
        dst = refs[n:2 * n]
        ici_s, ici_r, d2d_s, d2d_r, loc = refs[2 * n:]
        x, y, c, chips = _place()
        a = 2 * x + y
        local = [pltpu.make_async_copy(src[i], dst[i].at[a], loc.at[i]) for i in range(n)]
        for cp in local:
            cp.start()
        first = []
        for i in range(n):
            for k, (px, py) in enumerate(chips):
                first.append(pltpu.make_async_remote_copy(
                    src_ref=src[i].at[c], dst_ref=dst[i].at[a, c], send_sem=ici_s.at[i, k], recv_sem=ici_r.at[i, k],
                    device_id=(px, py, c), device_id_type=MESH))
        for cp in first:
            cp.start()
        passed = []
        for i in range(n):
            for k, (px, py) in enumerate(chips):
                b = 2 * px + py
                pltpu.make_async_remote_copy(
                    src_ref=src[i].at[c], dst_ref=dst[i].at[b, c], send_sem=ici_s.at[i, k], recv_sem=ici_r.at[i, k],
                    device_id=(px, py, c), device_id_type=MESH).wait_recv()
                fwd = pltpu.make_async_remote_copy(
                    src_ref=dst[i].at[b, c], dst_ref=dst[i].at[b, c], send_sem=d2d_s.at[i, k], recv_sem=d2d_r.at[i, k],
                    device_id=(x, y, 1 - c), device_id_type=MESH)
                fwd.start()
                passed.append(fwd)
        for i in range(n):
            for k, (px, py) in enumerate(chips):
                b = 2 * px + py
                pltpu.make_async_remote_copy(
                    src_ref=dst[i].at[b, 1 - c], dst_ref=dst[i].at[b, 1 - c], send_sem=d2d_s.at[i, k],
                    recv_sem=d2d_r.at[i, k], device_id=(x, y, 1 - c), device_id_type=MESH).wait_recv()
        for cp in first + passed:
            cp.wait_send()
        for cp in local:
            cp.wait()

    sems = [pltpu.SemaphoreType.DMA((n, 3))] * 4 + [pltpu.SemaphoreType.DMA((n,))]
    return pl.pallas_call(
        body, name=name, in_specs=[_ANY] * n, out_specs=[_ANY] * n,
        out_shape=[jax.ShapeDtypeStruct((N_CHIPS,) + s.shape, s.dtype) for s in shards],
        scratch_shapes=sems,
    )(*shards)


def _pair_exchange(name, parts):
    n = len(parts)

    def body(*refs):
        src = refs[:n]
        dst = refs[n:2 * n]
        send_sems, recv_sems = refs[2 * n:]
        x, y, c, _ = _place()
        copies = [pltpu.make_async_remote_copy(
            src_ref=src[i].at[:, 1 - c], dst_ref=dst[i], send_sem=send_sems.at[i], recv_sem=recv_sems.at[i],
            device_id=(x, y, 1 - c), device_id_type=MESH) for i in range(n)]
        for cp in copies:
            cp.start()
        for cp in copies:
            cp.wait()

    return pl.pallas_call(
        body, name=name, in_specs=[_ANY] * n, out_specs=[_ANY] * n,
        out_shape=[jax.ShapeDtypeStruct((p.shape[0],) + p.shape[2:], p.dtype) for p in parts],
        scratch_shapes=[pltpu.SemaphoreType.DMA((n,)), pltpu.SemaphoreType.DMA((n,))],
    )(*parts)


def _chip_exchange(name, parts):
    n = len(parts)

    def body(*refs):
        src = refs[:n]
        dst = refs[n:2 * n]
        send_sems, recv_sems = refs[2 * n:]
        x, y, c, chips = _place()
        copies = []
        for i in range(n):
            for k, (px, py) in enumerate(chips):
                copies.append(pltpu.make_async_remote_copy(
                    src_ref=src[i].at[2 * px + py], dst_ref=dst[i].at[k], send_sem=send_sems.at[i, k],
                    recv_sem=recv_sems.at[i, k], device_id=(px, py, c), device_id_type=MESH))
        for cp in copies:
            cp.start()
        for cp in copies:
            cp.wait()

    return pl.pallas_call(
        body, name=name, in_specs=[_ANY] * n, out_specs=[_ANY] * n,
        out_shape=[jax.ShapeDtypeStruct((3,) + p.shape[1:], p.dtype) for p in parts],
        scratch_shapes=[pltpu.SemaphoreType.DMA((n, 3)), pltpu.SemaphoreType.DMA((n, 3))],
    )(*parts)


def _half_exchange(name, halves):
    n = len(halves)

    def body(*refs):
        src = refs[:n]
        dst = refs[n:2 * n]
        send_sems, recv_sems, loc = refs[2 * n:]
        x, y, c, _ = _place()
        local = [pltpu.make_async_copy(src[i], dst[i].at[c], loc.at[i]) for i in range(n)]
        copies = [pltpu.make_async_remote_copy(
            src_ref=src[i], dst_ref=dst[i].at[c], send_sem=send_sems.at[i], recv_sem=recv_sems.at[i],
            device_id=(x, y, 1 - c), device_id_type=MESH) for i in range(n)]
        for cp in local + copies:
            cp.start()
        for cp in copies:
            cp.wait()
        for cp in local:
            cp.wait()

    return pl.pallas_call(
        body, name=name, in_specs=[_ANY] * n, out_specs=[_ANY] * n,
        out_shape=[jax.ShapeDtypeStruct((2,) + h.shape, h.dtype) for h in halves],
        scratch_shapes=[pltpu.SemaphoreType.DMA((n,)), pltpu.SemaphoreType.DMA((n,)), pltpu.SemaphoreType.DMA((n,))],
    )(*halves)


def _pair_add(name, pos, part, sib):
    Q, _, R2, C = part.shape
    tr = _pick(R2, 256)

    def body(pos_ref, p_ref, s_ref, o_ref):
        o_ref[...] = (p_ref[...].astype(F32) + s_ref[...].astype(F32)).astype(BF16)

    return pl.pallas_call(
        body, name=name,
        grid_spec=pltpu.PrefetchScalarGridSpec(
            num_scalar_prefetch=1, grid=(Q, R2 // tr),
            in_specs=[pl.BlockSpec((None, None, tr, C), lambda q, i, pos_ref: (q, pos_ref[1], i, 0)),
                      pl.BlockSpec((None, tr, C), lambda q, i, pos_ref: (q, i, 0))],
            out_specs=pl.BlockSpec((None, tr, C), lambda q, i, pos_ref: (q, i, 0))),
        out_shape=jax.ShapeDtypeStruct((Q, R2, C), BF16),
        compiler_params=_cp("parallel", "parallel"),
    )(pos, part, sib)


def _chip_sum(name, pos, own, got):
    _, R2, C = own.shape
    tr = _pick(R2, 256)

    def body(pos_ref, o_ref, g_ref, out_ref):
        acc = o_ref[...].astype(F32)
        for k in range(3):
            acc = acc + g_ref[k].astype(F32)
        out_ref[...] = acc

    return pl.pallas_call(
        body, name=name,
        grid_spec=pltpu.PrefetchScalarGridSpec(
            num_scalar_prefetch=1, grid=(R2 // tr,),
            in_specs=[pl.BlockSpec((None, tr, C), lambda i, pos_ref: (pos_ref[0], i, 0)),
                      pl.BlockSpec((3, tr, C), lambda i, pos_ref: (0, i, 0))],
            out_specs=pl.BlockSpec((tr, C), lambda i, pos_ref: (i, 0))),
        out_shape=jax.ShapeDtypeStruct((R2, C), F32),
        compiler_params=_cp("parallel"),
    )(pos, own, got)


def _reduce_scatter(tag, pos, parts):
    split = [p.reshape(p.shape[0], 2, p.shape[1] // 2, p.shape[2]) for p in parts]
    sib = _pair_exchange(tag + "_pair", split)
    chip = [_pair_add(f"{tag}_add{i}", pos, split[i], sib[i]) for i in range(len(parts))]
    got = _chip_exchange(tag + "_chip", chip)
    halves = [_chip_sum(f"{tag}_sum{i}", pos, chip[i], got[i]) for i in range(len(parts))]
    full = _half_exchange(tag + "_half", halves)
    return [f.reshape(p.shape[1], p.shape[2]) for f, p in zip(full, parts)]


def _pad_to(v, mult):
    n = v.shape[0]
    return jnp.pad(v, (0, (-n) % mult))


def kernel(x, p, pool_w, pool_scale, ssm_in_w, ssm_conv_w, ssm_conv_b, ssm_dt_bias, ssm_a_log, ssm_d, ssm_norm_w, ssm_out_w, mlp_w1, mlp_w2, ln_g, ln_b, ple_w, ple_gate_w, loss_target, m_pool_w, m_pool_scale, m_ssm_in_w, m_ssm_conv_w, m_ssm_conv_b, m_ssm_dt_bias, m_ssm_a_log, m_ssm_d, m_ssm_norm_w, m_ssm_out_w, m_mlp_w1, m_mlp_w2, m_ln_g, m_ln_b, m_ple_w, m_ple_gate_w, v_pool_w, v_pool_scale, v_ssm_in_w, v_ssm_conv_w, v_ssm_conv_b, v_ssm_dt_bias, v_ssm_a_log, v_ssm_d, v_ssm_norm_w, v_ssm_out_w, v_mlp_w1, v_mlp_w2, v_ln_g, v_ln_b, v_ple_w, v_ple_gate_w):
    T, D = x.shape[1], x.shape[2]
    NG = len(POOL_WINDOWS)
    GD = D // NG
    DI = ssm_out_w.shape[1] * N_CHIPS
    H = ssm_dt_bias.shape[1]
    HPG = H // N_GROUPS
    GW = DI // N_GROUPS
    GS = N_GROUPS * D_STATE
    CD = DI + 2 * GS
    DF = mlp_w1.shape[2] * N_CHIPS
    PD = ple_w.shape[1]
    NIN = ssm_in_w.shape[2]

    xi, yi, ci = lax.axis_index("x"), lax.axis_index("y"), lax.axis_index("c")
    chip = 2 * xi + yi
    pos = jnp.stack([chip, ci]).astype(jnp.int32)

    x0 = x[0]
    tgt = loss_target[0]
    p0b, p1b = p[0, 0].astype(BF16), p[1, 0].astype(BF16)

    def halves(w):
        return w.astype(BF16).reshape((2, w.shape[0] // 2) + w.shape[1:])

    sh_pool = pool_w[0].astype(BF16)
    sh_in = halves(ssm_in_w[0])
    sh_out = halves(ssm_out_w[0])
    sh_w1 = mlp_w1.astype(BF16)
    sh_w2 = mlp_w2.astype(BF16)
    sh_pw = ple_w.astype(BF16)
    sh_gw = ple_gate_w.astype(BF16)
    sh_pool = sh_pool.reshape((2, NG // 2) + sh_pool.shape[1:])
    g_pool, g_in, g_out, g_w1, g_w2, g_pw, g_gw = _gather_weights(
        "gather_weights", [sh_pool, sh_in, sh_out, sh_w1, sh_w2, sh_pw, sh_gw])
    w_pool = jnp.transpose(g_pool.reshape(N_CHIPS, NG, GD // N_CHIPS, GD), (1, 0, 2, 3)).reshape(NG, GD, GD)
    w_in = jnp.concatenate([g_in[q].reshape(D, NIN) for q in range(N_CHIPS)], axis=1)
    w_z, w_xbc, w_dt = w_in[:, :DI], w_in[:, DI:DI + CD], w_in[:, DI + CD:]
    w_out = g_out.reshape(DI, D)

    small_sh = jnp.concatenate([ssm_conv_w[0].reshape(-1), ssm_conv_b[0], ssm_norm_w[0],
                                ln_g.reshape(-1), ln_b.reshape(-1)])
    n_sh = small_sh.shape[0]
    small_all = _allgather_small("gather_small", _pad_to(small_sh, 1024).reshape(-1, 128))
    small_all = small_all.reshape(N_DEV, -1)[0::2, :n_sh]
    cdq, niq, dq = CD // N_CHIPS, DI // N_CHIPS, D // N_CHIPS
    o = 0
    conv_w = jnp.concatenate([small_all[q, o:o + CONV_K * cdq].reshape(CONV_K, cdq) for q in range(N_CHIPS)], axis=1)
    o += CONV_K * cdq
    conv_b = small_all[:, o:o + cdq].reshape(1, CD)
    o += cdq
    norm_w = small_all[:, o:o + niq].reshape(1, DI)
    o += niq
    lng = jnp.transpose(small_all[:, o:o + 4 * dq].reshape(N_CHIPS, 2, 2, dq), (1, 2, 0, 3)).reshape(2, 2, 1, D)
    o += 4 * dq
    lnb = jnp.transpose(small_all[:, o:o + 4 * dq].reshape(N_CHIPS, 2, 2, dq), (1, 2, 0, 3)).reshape(2, 2, 1, D)

    pooled = _pool_fwd("pool_fwd", x0)
    hraw = _mm("pool_mm", "nn", pooled, w_pool, T, D, GD, tn=GD, tk=GD,
               a_spec=lambda tm, tn, tk: pl.BlockSpec((tm, tk), lambda i, j, k: (i, j)),
               b_spec=lambda tm, tn, tk: pl.BlockSpec((None, tk, tn), lambda i, j, k: (j, 0, 0)))
    x1, x1b, xh1, rs1 = _res_ln("ln00", x0, hraw, lng[0, 0], lnb[0, 0], scale=pool_scale)

    def mlp_fwd(l, xb):
        a = _mm(f"mlp{l}_up", "nn", xb, g_w1, T, DF, D, tn=min(1024, DF // N_CHIPS),
                b_spec=_colshard_b(l, DF // N_CHIPS))
        h = _mm(f"mlp{l}_down", "nn", a, g_w2, T, D, DF, tk=min(512, DF // N_CHIPS),
                b_spec=_rowshard_b(l, DF // N_CHIPS),
                pro=lambda t: jnp.square(jnp.maximum(t, 0.0)))
        return a, h

    def ple_fwd(l, xb, pb):
        gl = _mm(f"gate{l}_logit", "nn", xb, g_gw, T, D, D, tk=min(512, D // N_CHIPS),
                 b_spec=_rowshard_b(l, D // N_CHIPS))
        e = _mm(f"gate{l}_emb", "nn", pb, g_pw, T, D, PD, tn=min(1024, D // N_CHIPS),
                b_spec=_colshard_b(l, D // N_CHIPS))
        return gl, e

    a0, h0 = mlp_fwd(0, x1b)
    x2, x2b, xh2, rs2 = _res_ln("ln01", x1, h0, lng[0, 1], lnb[0, 1])
    gl0, e0 = ple_fwd(0, x2b, p0b)
    x3, x3b, gate0 = _gate_fwd("gate0", x2, gl0, e0)

    z = _mm("ssm_in_z", "nn", x3b, w_z, T, DI, D)
    xbc_pre = _mm("ssm_in_xbc", "nn", x3b, w_xbc, T, CD, D)
    dt_pre = _mm("ssm_in_dt", "nn", x3b, w_dt, T, H, D)
    xbc = _conv_fwd("conv_fwd", xbc_pre, conv_w, conv_b)
    dt = _dt_fwd("dt_fwd", dt_pre, ssm_dt_bias)
    dt_g = jnp.transpose(dt.reshape(T, N_GROUPS, HPG), (1, 0, 2))
    dtc = jnp.pad(dt_g, ((0, 0), (0, 0), (0, HEAD_PAD - HPG)))
    dtr = jnp.pad(jnp.transpose(dt_g, (0, 2, 1)), ((0, 0), (0, ROW_PAD - HPG), (0, 0)))
    al_g = ssm_a_log.reshape(N_GROUPS, HPG)
    alr = jnp.pad(al_g, ((0, 0), (0, HEAD_PAD - HPG)))[:, None, :]
    alc = jnp.pad(al_g, ((0, 0), (0, ROW_PAD - HPG)))[:, :, None]
    d_e = jnp.repeat(ssm_d.reshape(N_GROUPS, HPG), HEAD_DIM, axis=1)[:, None, :]
    ysc, hprev = _ssd_fwd("ssd_fwd", xbc, dtc, dtr, alr, alc, d_e, DI)
    ynb, rsn = _gnorm_fwd("gnorm_fwd", ysc, z, norm_w)
    h1 = _mm("ssm_out", "nn", ynb, w_out, T, D, DI)
    x4, x4b, xh4, rs4 = _res_ln("ln10", x3, h1, lng[1, 0], lnb[1, 0])
    a1, h2 = mlp_fwd(1, x4b)
    x5, x5b, xh5, rs5 = _res_ln("ln11", x4, h2, lng[1, 1], lnb[1, 1])
    gl1, e1 = ple_fwd(1, x5b, p1b)
    dx6, gate1, loss_parts = _gate_loss("gate1_loss", x5, gl1, e1, tgt)
    loss_local = jnp.sum(loss_parts[0::8, 0])

    def ple_bwd(l, dxo, gate, e, xb, pb):
        dgl, de = _gate_bwd(f"gate{l}_bwd", dxo, gate, e)
        d_gw = _mm(f"gate{l}_dw", "tn", xb, dgl, D, D, T, out_dtype=BF16).reshape(N_CHIPS, D // N_CHIPS, D)
        d_pw = _mm(f"gate{l}_dpw", "tn", pb, de, PD, D, T, out_dtype=BF16, tn=min(1024, D // N_CHIPS),
                   o_shape=(N_CHIPS, PD, D // N_CHIPS), o_spec=_colshard_o(D // N_CHIPS))
        dx = _mm(f"gate{l}_dx", "nt", dgl, g_gw, T, D, D, tn=min(1024, D // N_CHIPS),
                 b_spec=_rowshard_bt(l, D // N_CHIPS), epi=lambda acc, r: acc + r, extras=[dxo])
        return dx, d_gw, d_pw

    def mlp_bwd(l, du, dub, a, xb):
        d_w2 = _mm(f"mlp{l}_dw2", "tn", a, dub, DF, D, T, out_dtype=BF16,
                   pro=lambda t: jnp.square(jnp.maximum(t, 0.0))).reshape(N_CHIPS, DF // N_CHIPS, D)
        da = _mm(f"mlp{l}_da", "nt", dub, g_w2, T, DF, D, tn=min(1024, DF // N_CHIPS),
                 b_spec=_rowshard_bt(l, DF // N_CHIPS), out_dtype=BF16,
                 epi=lambda acc, av: acc * (2.0 * jnp.maximum(av, 0.0)), extras=[a])
        d_w1 = _mm(f"mlp{l}_dw1", "tn", xb, da, D, DF, T, out_dtype=BF16, tn=min(1024, DF // N_CHIPS),
                   o_shape=(N_CHIPS, D, DF // N_CHIPS), o_spec=_colshard_o(DF // N_CHIPS))
        dx = _mm(f"mlp{l}_dx", "nt", da, g_w1, T, D, DF, tk=min(512, DF // N_CHIPS),
                 b_spec=_colshard_bt(l, DF // N_CHIPS), epi=lambda acc, r: acc + ALPHA * r, extras=[du])
        return dx, d_w1, d_w2

    dx5, d_gw1, d_pw1 = ple_bwd(1, dx6, gate1, e1, x5b, p1b)
    du5, du5b, dg11, db11 = _ln_bwd("ln11_bwd", dx5, xh5, rs5, lng[1, 1])
    dx4, d_w1_1, d_w2_1 = mlp_bwd(1, du5, du5b, a1, x4b)
    du4, du4b, dg10, db10 = _ln_bwd("ln10_bwd", dx4, xh4, rs4, lng[1, 0])
    d_wout = _mm("ssm_out_dw", "tn", ynb, du4b, DI, D, T, out_dtype=BF16).reshape(N_CHIPS, DI // N_CHIPS, D)
    dyn = _mm("ssm_out_dx", "nt", du4b, w_out, T, DI, D)
    dysc, dzb, dnorm_w = _gnorm_bwd("gnorm_bwd", dyn, ysc, z, norm_w, rsn)
    dxs, dbm, dcm, ddt_g, dalog_g, dd_g = _ssd_bwd("ssd_bwd", dysc, xbc, dtc, dtr, alr, alc, d_e, hprev, DI)
    dconv, dconv_w, dconv_b = _conv_bwd_a("conv_bwd_a", dxs, dbm, dcm, xbc_pre, conv_w, conv_b)
    dxbcb = _conv_bwd_b("conv_bwd_b", dconv, conv_w)
    ddt = jnp.transpose(ddt_g[:, :, :HPG], (1, 0, 2)).reshape(T, H)
    _, ddtpb, ddt_bias = _dt_bwd("dt_bwd", ddt, dt_pre, ssm_dt_bias)
    da_log = dalog_g[:, 0, :HPG].reshape(1, H)
    dd_skip = dd_g[:, 0, :HPG].reshape(1, H)
    d_wz = _mm("ssm_in_dwz", "tn", x3b, dzb, D, DI, T, out_dtype=BF16)
    d_wx = _mm("ssm_in_dwx", "tn", x3b, dxbcb, D, CD, T, out_dtype=BF16)
    d_wdt = _mm("ssm_in_dwdt", "tn", x3b, ddtpb, D, H, T, out_dtype=BF16)
    d_win = jnp.transpose(jnp.concatenate([d_wz, d_wx, d_wdt], axis=1).reshape(D, N_CHIPS, NIN), (1, 0, 2))
    dx3 = _mm("ssm_in_dxz", "nt", dzb, w_z, T, D, DI, epi=lambda acc, r: acc + ALPHA * r, extras=[du4])
    dx3 = _mm("ssm_in_dxx", "nt", dxbcb, w_xbc, T, D, CD, epi=lambda acc, r: acc + r, extras=[dx3])
    dx3 = _mm("ssm_in_dxdt", "nt", ddtpb, w_dt, T, D, H, epi=lambda acc, r: acc + r, extras=[dx3])

    dx2, d_gw0, d_pw0 = ple_bwd(0, dx3, gate0, e0, x2b, p0b)
    du2, du2b, dg01, db01 = _ln_bwd("ln01_bwd", dx2, xh2, rs2, lng[0, 1])
    dx1, d_w1_0, d_w2_0 = mlp_bwd(0, du2, du2b, a0, x1b)
    du1, dhrb, dg00, db00, dscale = _ln_bwd("ln00_bwd", dx1, xh1, rs1, lng[0, 0], hraw=hraw, scale=pool_scale)
    d_wpool = _mm("pool_dw", "tn", pooled, dhrb, D, GD, T, tm=GD, tn=GD, out_dtype=BF16,
                  b_spec=lambda tm, tn, tk: pl.BlockSpec((tk, tn), lambda i, j, k: (k, i)),
                  o_shape=(NG, GD, GD), o_spec=lambda tm, tn, tk: pl.BlockSpec((None, tm, tn), lambda i, j, k: (i, 0, 0)))
    dpooled = _mm("pool_dx", "nt", dhrb, w_pool, T, D, GD, tn=GD, tk=GD,
                  a_spec=lambda tm, tn, tk: pl.BlockSpec((tm, tk), lambda i, j, k: (i, j)),
                  b_spec=lambda tm, tn, tk: pl.BlockSpec((None, tn, tk), lambda i, j, k: (j, 0, 0)))
    grad_x = _pool_bwd("pool_bwd", dpooled, du1)
    d_wpool = jnp.transpose(d_wpool.reshape(NG, N_CHIPS, GD // N_CHIPS, GD), (1, 0, 2, 3)).reshape(N_CHIPS, NG * GD // N_CHIPS, GD)

    red = _reduce_scatter("rs", pos, [d_wpool, d_win, d_wout, d_w1_0, d_w1_1, d_w2_0, d_w2_1,
                                      d_gw0, d_gw1, d_pw0, d_pw1])
    r_pool, r_in, r_out, r_w1_0, r_w1_1, r_w2_0, r_w2_1, r_gw0, r_gw1, r_pw0, r_pw1 = red

    dln_g = jnp.stack([jnp.stack([dg00, dg01]), jnp.stack([dg10, dg11])]).reshape(-1)
    dln_b = jnp.stack([jnp.stack([db00, db01]), jnp.stack([db10, db11])]).reshape(-1)
    small_g = jnp.concatenate([dscale.reshape(-1), dconv_w.reshape(-1), dconv_b.reshape(-1), ddt_bias.reshape(-1),
                               da_log.reshape(-1), dd_skip.reshape(-1), dnorm_w.reshape(-1), dln_g, dln_b,
                               loss_local.reshape(1)])
    n_sg = small_g.shape[0]
    sg_all = _allgather_small("gather_small_grads", _pad_to(small_g, 1024).reshape(-1, 128))
    sg = _sum_peers("sum_small_grads", sg_all).reshape(-1)[:n_sg]
    o = 0

    def take(nel):
        nonlocal o
        v = sg[o:o + nel]
        o += nel
        return v

    g_scale = take(D).reshape(1, D)
    g_conv_w_full = take(CONV_K * CD).reshape(CONV_K, CD)
    g_conv_b_full = take(CD).reshape(1, CD)
    g_dt_bias = take(H).reshape(1, H)
    g_a_log = take(H).reshape(1, H)
    g_d = take(H).reshape(1, H)
    g_norm_full = take(DI).reshape(1, DI)
    g_lng_full = take(4 * D).reshape(2, 2, D)
    g_lnb_full = take(4 * D).reshape(2, 2, D)
    loss = take(1).reshape(())
    g_conv_w = lax.dynamic_slice_in_dim(g_conv_w_full, chip * cdq, cdq, axis=1)[None]
    g_conv_b = lax.dynamic_slice_in_dim(g_conv_b_full, chip * cdq, cdq, axis=1)
    g_norm = lax.dynamic_slice_in_dim(g_norm_full, chip * niq, niq, axis=1)
    g_lng = lax.dynamic_slice_in_dim(g_lng_full, chip * dq, dq, axis=2)
    g_lnb = lax.dynamic_slice_in_dim(g_lnb_full, chip * dq, dq, axis=2)

    def adam(name, w, m, v, grads):
        shp = w.shape
        L = len(grads)
        w3, m3, v3 = (t.reshape((L, -1, shp[-1])) for t in (w, m, v))
        return [t.reshape(shp) for t in _adamw(name, w3, m3, v3, grads)]

    big = {
        "pool_w": adam("adam_pool_w", pool_w, m_pool_w, v_pool_w, [r_pool]),
        "ssm_in_w": adam("adam_ssm_in_w", ssm_in_w, m_ssm_in_w, v_ssm_in_w, [r_in]),
        "ssm_out_w": adam("adam_ssm_out_w", ssm_out_w, m_ssm_out_w, v_ssm_out_w, [r_out]),
        "mlp_w1": adam("adam_mlp_w1", mlp_w1, m_mlp_w1, v_mlp_w1, [r_w1_0, r_w1_1]),
        "mlp_w2": adam("adam_mlp_w2", mlp_w2, m_mlp_w2, v_mlp_w2, [r_w2_0, r_w2_1]),
        "ple_w": adam("adam_ple_w", ple_w, m_ple_w, v_ple_w, [r_pw0, r_pw1]),
        "ple_gate_w": adam("adam_ple_gate_w", ple_gate_w, m_ple_gate_w, v_ple_gate_w, [r_gw0, r_gw1]),
    }

    small = [("pool_scale", pool_scale, m_pool_scale, v_pool_scale, g_scale),
             ("ssm_conv_w", ssm_conv_w, m_ssm_conv_w, v_ssm_conv_w, g_conv_w),
             ("ssm_conv_b", ssm_conv_b, m_ssm_conv_b, v_ssm_conv_b, g_conv_b),
             ("ssm_dt_bias", ssm_dt_bias, m_ssm_dt_bias, v_ssm_dt_bias, g_dt_bias),
             ("ssm_a_log", ssm_a_log, m_ssm_a_log, v_ssm_a_log, g_a_log),
             ("ssm_d", ssm_d, m_ssm_d, v_ssm_d, g_d),
             ("ssm_norm_w", ssm_norm_w, m_ssm_norm_w, v_ssm_norm_w, g_norm),
             ("ln_g", ln_g, m_ln_g, v_ln_g, g_lng),
             ("ln_b", ln_b, m_ln_b, v_ln_b, g_lnb)]

    def pack(idx):
        flat = _pad_to(jnp.concatenate([s[idx].reshape(-1) for s in small]), 1024)
        return flat.reshape(1, -1, 128)

    sm_out = _adamw("adam_small", pack(1), pack(2), pack(3), [pack(4)[0]])
    small_res = {}
    o = 0
    for s in small:
        nel = s[1].size
        small_res[s[0]] = [t.reshape(-1)[o:o + nel].reshape(s[1].shape) for t in sm_out]
        o += nel

    order = ["pool_w", "pool_scale", "ssm_in_w", "ssm_conv_w", "ssm_conv_b", "ssm_dt_bias", "ssm_a_log", "ssm_d",
             "ssm_norm_w", "ssm_out_w", "mlp_w1", "mlp_w2", "ln_g", "ln_b", "ple_w", "ple_gate_w"]
    res = {**big, **small_res}
    outs = [loss, grad_x[None]]
    for kind in range(4):
        outs += [res[nm][kind] for nm in order]
    return tuple(outs)
```

```python
import functools

import jax
import jax.numpy as jnp
from jax import lax
from jax.experimental import pallas as pl
from jax.experimental.pallas import tpu as pltpu

F32 = jnp.float32
BF16 = jnp.bfloat16
HI = lax.Precision.HIGHEST
MESH = pl.DeviceIdType.MESH

DEPTH = 2
ALPHA = (2.0 * DEPTH) ** 0.25
LN_EPS = 1e-5
RMS_EPS = 1e-5
POOL_WINDOWS = (2, 4, 8, 16)
POOL_HALO = 16
HEAD_DIM = 64
N_GROUPS = 8
D_STATE = 128
CHUNK = 128
CONV_K = 4
CONV_HALO = 8
HEAD_PAD = 128
ROW_PAD = 8
N_CHIPS = 4
N_DEV = 8
ADAM_LR = 0.001
ADAM_B1 = 0.9
ADAM_B2 = 0.999
ADAM_EPS = 1e-08
ADAM_WD = 0.01
ADAM_STEP = 10
VMEM_LIMIT = 56 * 1024 * 1024
ADAM_BLOCK_BYTES = 1024 * 1024


def _cp(*sem):
    return pltpu.CompilerParams(dimension_semantics=sem, vmem_limit_bytes=VMEM_LIMIT)


def _pick(dim, pref):
    t = pref
    while t >= 128:
        if dim % t == 0:
            return t
        t //= 2
    return dim


def _rows(rows, row_bytes, budget):
    t = rows
    while t * row_bytes > budget and t % 16 == 0:
        t //= 2
    return t


def _sigmoid(v):
    return 1.0 / (1.0 + jnp.exp(-v))


_DIMS = {"nn": (((1,), (0,)), ((), ())), "nt": (((1,), (1,)), ((), ())), "tn": (((0,), (0,)), ((), ()))}


def _mm(name, form, a, b, M, N, K, *, tm=1024, tn=1024, tk=512, a_spec=None, b_spec=None,
        o_shape=None, o_spec=None, out_dtype=F32, pro=None, epi=None, extras=()):
    tm, tn, tk = _pick(M, tm), _pick(N, tn), _pick(K, tk)
    nk = K // tk
    if a_spec is None:
        a_spec = (pl.BlockSpec((tk, tm), lambda i, j, k: (k, i)) if form == "tn"
                  else pl.BlockSpec((tm, tk), lambda i, j, k: (i, k)))
    else:
        a_spec = a_spec(tm, tn, tk)
    if b_spec is None:
        b_spec = (pl.BlockSpec((tn, tk), lambda i, j, k: (j, k)) if form == "nt"
                  else pl.BlockSpec((tk, tn), lambda i, j, k: (k, j)))
    else:
        b_spec = b_spec(tm, tn, tk)
    if o_spec is None:
        o_spec = pl.BlockSpec((tm, tn), lambda i, j, k: (i, j))
        o_shape = (M, N)
    else:
        o_spec = o_spec(tm, tn, tk)
    ex_arrays = [e for e in extras]
    ex_specs = [pl.BlockSpec((tm, tn), lambda i, j, k: (i, j)) for _ in extras]
    ne = len(ex_arrays)
    dims = _DIMS[form]

    def body(a_ref, b_ref, *rest):
        ex_refs = rest[:ne]
        o_ref = rest[ne]
        at = a_ref[...]
        if pro is not None:
            at = pro(at)
        p = lax.dot_general(at.astype(BF16), b_ref[...].astype(BF16), dims, preferred_element_type=F32)

        def finish(acc):
            if epi is not None:
                acc = epi(acc, *[r[...] for r in ex_refs])
            o_ref[...] = acc.astype(out_dtype)

        if nk == 1:
            finish(p)
        else:
            acc_ref = rest[ne + 1]
            k = pl.program_id(2)

            @pl.when(k == 0)
            def _():
                acc_ref[...] = p

            @pl.when(k > 0)
            def _():
                acc_ref[...] += p

            @pl.when(k == nk - 1)
            def _():
                finish(acc_ref[...])

    return pl.pallas_call(
        body, name=name, grid=(M // tm, N // tn, nk),
        in_specs=[a_spec, b_spec] + ex_specs, out_specs=o_spec,
        out_shape=jax.ShapeDtypeStruct(o_shape, out_dtype),
        scratch_shapes=[pltpu.VMEM((tm, tn), F32)] if nk > 1 else [],
        compiler_params=_cp("parallel", "parallel", "arbitrary"),
    )(a, b, *ex_arrays)


def _colshard_b(l, n_per):
    def make(tm, tn, tk):
        nb = n_per // tn
        return pl.BlockSpec((None, None, tk, tn), lambda i, j, k: (j // nb, l, k, j % nb))
    return make


def _colshard_bt(l, n_per):
    def make(tm, tn, tk):
        nb = n_per // tk
        return pl.BlockSpec((None, None, tn, tk), lambda i, j, k: (k // nb, l, j, k % nb))
    return make


def _colshard_o(n_per):
    def make(tm, tn, tk):
        nb = n_per // tn
        return pl.BlockSpec((None, tm, tn), lambda i, j, k: (j // nb, i, j % nb))
    return make


def _rowshard_b(l, k_per):
    def make(tm, tn, tk):
        nb = k_per // tk
        return pl.BlockSpec((None, None, tk, tn), lambda i, j, k: (k // nb, l, k % nb, j))
    return make


def _rowshard_bt(l, k_per):
    def make(tm, tn, tk):
        nb = k_per // tn
        return pl.BlockSpec((None, None, tn, tk), lambda i, j, k: (j // nb, l, j % nb, k))
    return make


def _res_ln(name, xprev, h, g, b, scale=None):
    T, D = xprev.shape
    tr = _pick(T, 256)
    row = pl.BlockSpec((tr, D), lambda i: (i, 0))
    vec = pl.BlockSpec((1, D), lambda i: (0, 0))
    has_scale = scale is not None

    def body(*refs):
        if has_scale:
            x_ref, h_ref, s_ref, g_ref, b_ref, y_ref, yb_ref, xh_ref, rs_ref = refs
            hh = h_ref[...] * s_ref[...]
        else:
            x_ref, h_ref, g_ref, b_ref, y_ref, yb_ref, xh_ref, rs_ref = refs
            hh = h_ref[...]
        u = ALPHA * x_ref[...] + hh
        mu = jnp.mean(u, axis=-1, keepdims=True)
        d = u - mu
        var = jnp.mean(d * d, axis=-1, keepdims=True)
        rs = lax.rsqrt(var + LN_EPS)
        xh = d * rs
        y = xh * g_ref[...] + b_ref[...]
        y_ref[...] = y
        yb_ref[...] = y.astype(BF16)
        xh_ref[...] = xh
        rs_ref[...] = rs

    ins = [xprev, h] + ([scale] if has_scale else []) + [g, b]
    specs = [row, row] + ([vec] if has_scale else []) + [vec, vec]
    return pl.pallas_call(
        body, name=name, grid=(T // tr,), in_specs=specs,
        out_specs=[row, row, row, pl.BlockSpec((tr, 1), lambda i: (i, 0))],
        out_shape=[jax.ShapeDtypeStruct((T, D), F32), jax.ShapeDtypeStruct((T, D), BF16),
                   jax.ShapeDtypeStruct((T, D), F32), jax.ShapeDtypeStruct((T, 1), F32)],
        compiler_params=_cp("parallel"),
    )(*ins)


def _accum(ref, part, first):
    @pl.when(first)
    def _():
        ref[...] = part

    @pl.when(jnp.logical_not(first))
    def _():
        ref[...] += part


def _ln_bwd(name, dy, xh, rs, g, hraw=None, scale=None):
    T, D = dy.shape
    tr = _pick(T, 256)
    row = pl.BlockSpec((tr, D), lambda i: (i, 0))
    vec = pl.BlockSpec((1, D), lambda i: (0, 0))
    has_scale = scale is not None

    def body(*refs):
        if has_scale:
            dy_ref, xh_ref, rs_ref, g_ref, hr_ref, s_ref, du_ref, dub_ref, dg_ref, db_ref, ds_ref = refs
        else:
            dy_ref, xh_ref, rs_ref, g_ref, du_ref, dub_ref, dg_ref, db_ref = refs
        first = pl.program_id(0) == 0
        dyv = dy_ref[...]
        xhv = xh_ref[...]
        dxh = dyv * g_ref[...]
        m1 = jnp.mean(dxh, axis=-1, keepdims=True)
        m2 = jnp.mean(dxh * xhv, axis=-1, keepdims=True)
        du = rs_ref[...] * (dxh - m1 - xhv * m2)
        du_ref[...] = du
        if has_scale:
            dub_ref[...] = (du * s_ref[...]).astype(BF16)
            _accum(ds_ref, jnp.sum(du * hr_ref[...], axis=0, keepdims=True), first)
        else:
            dub_ref[...] = du.astype(BF16)
        _accum(dg_ref, jnp.sum(dyv * xhv, axis=0, keepdims=True), first)
        _accum(db_ref, jnp.sum(dyv, axis=0, keepdims=True), first)

    ins = [dy, xh, rs, g] + ([hraw, scale] if has_scale else [])
    specs = [row, row, pl.BlockSpec((tr, 1), lambda i: (i, 0)), vec] + ([row, vec] if has_scale else [])
    n_vec = 3 if has_scale else 2
    return pl.pallas_call(
        body, name=name, grid=(T // tr,), in_specs=specs,
        out_specs=[row, row] + [vec] * n_vec,
        out_shape=[jax.ShapeDtypeStruct((T, D), F32), jax.ShapeDtypeStruct((T, D), BF16)]
        + [jax.ShapeDtypeStruct((1, D), F32)] * n_vec,
        compiler_params=_cp("arbitrary"),
    )(*ins)


def _gate_fwd(name, x, gl, e):
    T, D = x.shape
    tr = _pick(T, 256)
    row = pl.BlockSpec((tr, D), lambda i: (i, 0))

    def body(x_ref, gl_ref, e_ref, xo_ref, xob_ref, gate_ref):
        gate = _sigmoid(gl_ref[...])
        xo = x_ref[...] + gate * e_ref[...]
        xo_ref[...] = xo
        xob_ref[...] = xo.astype(BF16)
        gate_ref[...] = gate

    return pl.pallas_call(
        body, name=name, grid=(T // tr,), in_specs=[row, row, row], out_specs=[row, row, row],
        out_shape=[jax.ShapeDtypeStruct((T, D), F32), jax.ShapeDtypeStruct((T, D), BF16),
                   jax.ShapeDtypeStruct((T, D), F32)],
        compiler_params=_cp("parallel"),
    )(x, gl, e)


def _gate_loss(name, x, gl, e, tgt):
    T, D = x.shape
    tr = _pick(T, 256)
    row = pl.BlockSpec((tr, D), lambda i: (i, 0))

    def body(x_ref, gl_ref, e_ref, t_ref, dy_ref, gate_ref, lp_ref):
        gate = _sigmoid(gl_ref[...])
        err = x_ref[...] + gate * e_ref[...] - t_ref[...]
        dy_ref[...] = err * (1.0 / D)
        gate_ref[...] = gate
        s = jnp.sum(jnp.mean(err * err, axis=-1, keepdims=True), axis=0, keepdims=True)
        lp_ref[...] = jnp.broadcast_to(0.5 * s, (8, 128))

    return pl.pallas_call(
        body, name=name, grid=(T // tr,), in_specs=[row] * 4,
        out_specs=[row, row, pl.BlockSpec((8, 128), lambda i: (i, 0))],
        out_shape=[jax.ShapeDtypeStruct((T, D), F32), jax.ShapeDtypeStruct((T, D), F32),
                   jax.ShapeDtypeStruct((T // tr * 8, 128), F32)],
        compiler_params=_cp("parallel"),
    )(x, gl, e, tgt)


def _gate_bwd(name, dxo, gate, e):
    T, D = dxo.shape
    tr = _pick(T, 256)
    row = pl.BlockSpec((tr, D), lambda i: (i, 0))

    def body(d_ref, gate_ref, e_ref, dgl_ref, de_ref):
        d = d_ref[...]
        gate = gate_ref[...]
        dgl_ref[...] = (d * e_ref[...] * gate * (1.0 - gate)).astype(BF16)
        de_ref[...] = (d * gate).astype(BF16)

    return pl.pallas_call(
        body, name=name, grid=(T // tr,), in_specs=[row] * 3, out_specs=[row, row],
        out_shape=[jax.ShapeDtypeStruct((T, D), BF16)] * 2,
        compiler_params=_cp("parallel"),
    )(dxo, gate, e)


def _gnorm_fwd(name, y, z, w):
    T, DI = y.shape
    tr = _pick(T, 128)
    row = pl.BlockSpec((tr, DI), lambda i: (i, 0))
    vec = pl.BlockSpec((1, DI), lambda i: (0, 0))
    col = pl.BlockSpec((tr, 1), lambda i: (i, 0))

    def body(y_ref, z_ref, w_ref, o_ref, rs_ref):
        zv = z_ref[...]
        yz = y_ref[...] * (zv * _sigmoid(zv))
        rs = lax.rsqrt(jnp.mean(yz * yz, axis=-1, keepdims=True) + RMS_EPS)
        o_ref[...] = (yz * rs * w_ref[...]).astype(BF16)
        rs_ref[...] = rs

    return pl.pallas_call(
        body, name=name, grid=(T // tr,), in_specs=[row, row, vec], out_specs=[row, col],
        out_shape=[jax.ShapeDtypeStruct((T, DI), BF16), jax.ShapeDtypeStruct((T, 1), F32)],
        compiler_params=_cp("parallel"),
    )(y, z, w)


def _gnorm_bwd(name, dyn, y, z, w, rs):
    T, DI = y.shape
    tr = _pick(T, 128)
    row = pl.BlockSpec((tr, DI), lambda i: (i, 0))
    vec = pl.BlockSpec((1, DI), lambda i: (0, 0))
    col = pl.BlockSpec((tr, 1), lambda i: (i, 0))

    def body(d_ref, y_ref, z_ref, w_ref, rs_ref, dy_ref, dz_ref, dw_ref):
        first = pl.program_id(0) == 0
        zv = z_ref[...]
        yv = y_ref[...]
        sg = _sigmoid(zv)
        sz = zv * sg
        rsv = rs_ref[...]
        yzh = yv * sz * rsv
        dv = d_ref[...]
        gw = dv * w_ref[...]
        m = jnp.mean(gw * yzh, axis=-1, keepdims=True)
        dyz = rsv * (gw - yzh * m)
        dy_ref[...] = dyz * sz
        dz_ref[...] = (dyz * yv * (sg * (1.0 + zv * (1.0 - sg)))).astype(BF16)
        _accum(dw_ref, jnp.sum(dv * yzh, axis=0, keepdims=True), first)

    return pl.pallas_call(
        body, name=name, grid=(T // tr,), in_specs=[row, row, row, vec, col], out_specs=[row, row, vec],
        out_shape=[jax.ShapeDtypeStruct((T, DI), F32), jax.ShapeDtypeStruct((T, DI), BF16),
                   jax.ShapeDtypeStruct((1, DI), F32)],
        compiler_params=_cp("arbitrary"),
    )(dyn, y, z, w, rs)


def _sel4(j, vals):
    return jnp.where(j == 0, vals[0], jnp.where(j == 1, vals[1], jnp.where(j == 2, vals[2], vals[3])))


def _pool_cnt(i, j, tr, rows, offset):
    t = i * tr + offset + lax.broadcasted_iota(jnp.int32, (rows, 1), 0)
    win = _sel4(j, POOL_WINDOWS)
    return jnp.minimum(t + 1, win).astype(F32)


def _pool_fwd(name, x):
    T, D = x.shape
    gd = D // len(POOL_WINDOWS)
    tr = _pick(T, 512)
    hb = tr // POOL_HALO

    def body(x_ref, h_ref, o_ref):
        i, j = pl.program_id(0), pl.program_id(1)
        xv = x_ref[...]
        halo = jnp.where(i > 0, h_ref[...], 0.0)
        cat = jnp.concatenate([halo, xv], axis=0)
        s2 = cat + pltpu.roll(cat, 1, 0)
        s4 = s2 + pltpu.roll(s2, 2, 0)
        s8 = s4 + pltpu.roll(s4, 4, 0)
        s16 = s8 + pltpu.roll(s8, 8, 0)
        sel = _sel4(j, (s2, s4, s8, s16))[POOL_HALO:]
        o_ref[...] = (sel / _pool_cnt(i, j, tr, tr, 0) - xv).astype(BF16)

    return pl.pallas_call(
        body, name=name, grid=(T // tr, len(POOL_WINDOWS)),
        in_specs=[pl.BlockSpec((tr, gd), lambda i, j: (i, j)),
                  pl.BlockSpec((POOL_HALO, gd), lambda i, j: (jnp.maximum(i * hb - 1, 0), j))],
        out_specs=pl.BlockSpec((tr, gd), lambda i, j: (i, j)),
        out_shape=jax.ShapeDtypeStruct((T, D), BF16),
        compiler_params=_cp("parallel", "parallel"),
    )(x, x)


def _pool_bwd(name, dp, du):
    T, D = dp.shape
    gd = D // len(POOL_WINDOWS)
    tr = _pick(T, 512)
    hb = tr // POOL_HALO
    last_h = T // POOL_HALO - 1
    n = tr + POOL_HALO

    def body(dp_ref, h_ref, du_ref, o_ref):
        i, j = pl.program_id(0), pl.program_id(1)
        dpv = dp_ref[...]
        q = dpv / _pool_cnt(i, j, tr, tr, 0)
        qh = jnp.where(i < pl.num_programs(0) - 1, h_ref[...] / _pool_cnt(i, j, tr, POOL_HALO, tr), 0.0)
        cat = jnp.concatenate([q, qh], axis=0)
        f2 = cat + pltpu.roll(cat, n - 1, 0)
        f4 = f2 + pltpu.roll(f2, n - 2, 0)
        f8 = f4 + pltpu.roll(f4, n - 4, 0)
        f16 = f8 + pltpu.roll(f8, n - 8, 0)
        sel = _sel4(j, (f2, f4, f8, f16))[:tr]
        o_ref[...] = ALPHA * du_ref[...] + sel - dpv

    return pl.pallas_call(
        body, name=name, grid=(T // tr, len(POOL_WINDOWS)),
        in_specs=[pl.BlockSpec((tr, gd), lambda i, j: (i, j)),
                  pl.BlockSpec((POOL_HALO, gd), lambda i, j: (jnp.minimum((i + 1) * hb, last_h), j)),
                  pl.BlockSpec((tr, gd), lambda i, j: (i, j))],
        out_specs=pl.BlockSpec((tr, gd), lambda i, j: (i, j)),
        out_shape=jax.ShapeDtypeStruct((T, D), F32),
        compiler_params=_cp("parallel", "parallel"),
    )(dp, dp, du)


def _conv_taps(cat, wv, rows):
    shifted = [cat[CONV_HALO:] if s == 0 else pltpu.roll(cat, s, 0)[CONV_HALO:] for s in range(CONV_K)]
    acc = shifted[0] * wv[CONV_K - 1:CONV_K]
    for s in range(1, CONV_K):
        acc = acc + shifted[s] * wv[CONV_K - 1 - s:CONV_K - s]
    return acc, shifted


def _conv_fwd(name, xp, w, b):
    T, CD = xp.shape
    tr, tc = _pick(T, 512), _pick(CD, 512)
    hb = tr // CONV_HALO

    def body(x_ref, h_ref, w_ref, b_ref, o_ref):
        i = pl.program_id(0)
        halo = jnp.where(i > 0, h_ref[...], 0.0)
        cat = jnp.concatenate([halo, x_ref[...]], axis=0)
        acc, _ = _conv_taps(cat, w_ref[...], tr)
        acc = acc + b_ref[...]
        o_ref[...] = acc * _sigmoid(acc)

    return pl.pallas_call(
        body, name=name, grid=(T // tr, CD // tc),
        in_specs=[pl.BlockSpec((tr, tc), lambda i, j: (i, j)),
                  pl.BlockSpec((CONV_HALO, tc), lambda i, j: (jnp.maximum(i * hb - 1, 0), j)),
                  pl.BlockSpec((CONV_K, tc), lambda i, j: (0, j)),
                  pl.BlockSpec((1, tc), lambda i, j: (0, j))],
        out_specs=pl.BlockSpec((tr, tc), lambda i, j: (i, j)),
        out_shape=jax.ShapeDtypeStruct((T, CD), F32),
        compiler_params=_cp("parallel", "parallel"),
    )(xp, xp, w, b)


def _conv_bwd_a(name, dxs, dbm, dcm, xp, w, b):
    T, CD = xp.shape
    tr = _pick(T, 512)
    tc = _pick(dbm.shape[1], 512)
    hb = tr // CONV_HALO
    nx, nb = dxs.shape[1] // tc, dbm.shape[1] // tc

    def part_spec(lo, n):
        def imap(j, i):
            inside = jnp.logical_and(j >= lo, j < lo + n)
            return (jnp.where(inside, i, 0), jnp.clip(j - lo, 0, n - 1))
        return pl.BlockSpec((tr, tc), imap)

    def body(dx_ref, db_ref, dc_ref, x_ref, h_ref, w_ref, b_ref, o_ref, dw_ref, dbias_ref):
        j, i = pl.program_id(0), pl.program_id(1)
        first = i == 0
        d = jnp.where(j < nx, dx_ref[...], jnp.where(j < nx + nb, db_ref[...], dc_ref[...]))
        halo = jnp.where(i > 0, h_ref[...], 0.0)
        cat = jnp.concatenate([halo, x_ref[...]], axis=0)
        acc, shifted = _conv_taps(cat, w_ref[...], tr)
        acc = acc + b_ref[...]
        sg = _sigmoid(acc)
        dconv = d * (sg * (1.0 + acc * (1.0 - sg)))
        o_ref[...] = dconv
        _accum(dbias_ref, jnp.sum(dconv, axis=0, keepdims=True), first)
        tap = lax.broadcasted_iota(jnp.int32, (CONV_K, tc), 0)
        dwv = jnp.zeros((CONV_K, tc), F32)
        for s in range(CONV_K):
            dwv = jnp.where(tap == CONV_K - 1 - s, jnp.sum(dconv * shifted[s], axis=0, keepdims=True), dwv)
        _accum(dw_ref, dwv, first)

    return pl.pallas_call(
        body, name=name, grid=(CD // tc, T // tr),
        in_specs=[part_spec(0, nx), part_spec(nx, nb), part_spec(nx + nb, nb),
                  pl.BlockSpec((tr, tc), lambda j, i: (i, j)),
                  pl.BlockSpec((CONV_HALO, tc), lambda j, i: (jnp.maximum(i * hb - 1, 0), j)),
                  pl.BlockSpec((CONV_K, tc), lambda j, i: (0, j)),
                  pl.BlockSpec((1, tc), lambda j, i: (0, j))],
        out_specs=[pl.BlockSpec((tr, tc), lambda j, i: (i, j)),
                   pl.BlockSpec((CONV_K, tc), lambda j, i: (0, j)),
                   pl.BlockSpec((1, tc), lambda j, i: (0, j))],
        out_shape=[jax.ShapeDtypeStruct((T, CD), F32), jax.ShapeDtypeStruct((CONV_K, CD), F32),
                   jax.ShapeDtypeStruct((1, CD), F32)],
        compiler_params=_cp("parallel", "arbitrary"),
    )(dxs, dbm, dcm, xp, xp, w, b)


def _conv_bwd_b(name, dconv, w):
    T, CD = dconv.shape
    tr, tc = _pick(T, 512), _pick(CD, 512)
    hb = tr // CONV_HALO
    last_h = T // CONV_HALO - 1
    n = tr + CONV_HALO

    def body(d_ref, h_ref, w_ref, o_ref):
        i = pl.program_id(0)
        halo = jnp.where(i < pl.num_programs(0) - 1, h_ref[...], 0.0)
        cat = jnp.concatenate([d_ref[...], halo], axis=0)
        wv = w_ref[...]
        acc = cat[:tr] * wv[CONV_K - 1:CONV_K]
        for s in range(1, CONV_K):
            acc = acc + pltpu.roll(cat, n - s, 0)[:tr] * wv[CONV_K - 1 - s:CONV_K - s]
        o_ref[...] = acc.astype(BF16)

    return pl.pallas_call(
        body, name=name, grid=(T // tr, CD // tc),
        in_specs=[pl.BlockSpec((tr, tc), lambda i, j: (i, j)),
                  pl.BlockSpec((CONV_HALO, tc), lambda i, j: (jnp.minimum((i + 1) * hb, last_h), j)),
                  pl.BlockSpec((CONV_K, tc), lambda i, j: (0, j))],
        out_specs=pl.BlockSpec((tr, tc), lambda i, j: (i, j)),
        out_shape=jax.ShapeDtypeStruct((T, CD), BF16),
        compiler_params=_cp("parallel", "parallel"),
    )(dconv, dconv, w)


def _dt_fwd(name, dtp, bias):
    T, H = dtp.shape

    def body(x_ref, b_ref, o_ref):
        v = x_ref[...] + b_ref[...]
        u = jnp.exp(-jnp.abs(v))
        w1 = 1.0 + u
        lp = jnp.where(w1 == 1.0, u, jnp.log(w1) * (u / jnp.where(w1 == 1.0, 1.0, w1 - 1.0)))
        o_ref[...] = jnp.maximum(v, 0.0) + lp

    return pl.pallas_call(body, name=name, out_shape=jax.ShapeDtypeStruct((T, H), F32))(dtp, bias)


def _dt_bwd(name, ddt, dtp, bias):
    T, H = dtp.shape

    def body(d_ref, x_ref, b_ref, o_ref, ob_ref, db_ref):
        g = d_ref[...] * _sigmoid(x_ref[...] + b_ref[...])
        o_ref[...] = g
        ob_ref[...] = g.astype(BF16)
        db_ref[...] = jnp.sum(g, axis=0, keepdims=True)

    return pl.pallas_call(
        body, name=name,
        out_shape=[jax.ShapeDtypeStruct((T, H), F32), jax.ShapeDtypeStruct((T, H), BF16),
                   jax.ShapeDtypeStruct((1, H), F32)],
    )(ddt, dtp, bias)


def _dotf(a, b):
    return jnp.dot(a, b, precision=HI, preferred_element_type=F32)


def _dotb(form, a, b):
    return lax.dot_general(a.astype(BF16), b.astype(BF16), _DIMS[form], preferred_element_type=F32)


def _ssd_common(dtc, dtr, alr, alc, gw):
    li = lax.broadcasted_iota(jnp.int32, (CHUNK, CHUNK), 0)
    si = lax.broadcasted_iota(jnp.int32, (CHUNK, CHUNK), 1)
    tri = (li >= si).astype(F32)
    trit = (li <= si).astype(F32)
    a_row = -jnp.exp(alr)
    a_col = -jnp.exp(alc)
    acs_c = _dotf(tri, dtc * a_row)
    acs_r = _dotf(dtr * a_col, trit)
    eh = lax.broadcasted_iota(jnp.int32, (HEAD_PAD, gw), 0)
    ec = lax.broadcasted_iota(jnp.int32, (HEAD_PAD, gw), 1) // HEAD_DIM
    expand = (eh == ec).astype(F32)
    th = lax.broadcasted_iota(jnp.int32, (gw, HEAD_PAD), 1)
    tc = lax.broadcasted_iota(jnp.int32, (gw, HEAD_PAD), 0) // HEAD_DIM
    reduce_ = (th == tc).astype(F32)
    acs_last = acs_c[CHUNK - 1:CHUNK, :]
    return dict(li=li, si=si, tri=tri, trit=trit, a_row=a_row, acs_c=acs_c, acs_r=acs_r,
                expand=expand, reduce=reduce_, acs_last=acs_last,
                dt_e=_dotf(dtc, expand), eacs_e=jnp.exp(_dotf(acs_c, expand)),
                dec_h=jnp.exp(acs_last - acs_c))


def _ssd_specs(T, DI, gw, nc, rev):
    nsb = DI // D_STATE
    cidx = (lambda c: nc - 1 - c) if rev else (lambda c: c)
    return dict(
        xs=pl.BlockSpec((CHUNK, gw), lambda g, c: (cidx(c), g)),
        bm=pl.BlockSpec((CHUNK, D_STATE), lambda g, c: (cidx(c), nsb + g)),
        cm=pl.BlockSpec((CHUNK, D_STATE), lambda g, c: (cidx(c), nsb + N_GROUPS + g)),
        dtc=pl.BlockSpec((None, CHUNK, HEAD_PAD), lambda g, c: (g, cidx(c), 0)),
        dtr=pl.BlockSpec((None, ROW_PAD, CHUNK), lambda g, c: (g, 0, cidx(c))),
        alr=pl.BlockSpec((None, 1, HEAD_PAD), lambda g, c: (g, 0, 0)),
        alc=pl.BlockSpec((None, ROW_PAD, 1), lambda g, c: (g, 0, 0)),
        de=pl.BlockSpec((None, 1, gw), lambda g, c: (g, 0, 0)),
        hp=pl.BlockSpec((None, None, gw, D_STATE), lambda g, c: (cidx(c), g, 0, 0)),
        bc=pl.BlockSpec((CHUNK, D_STATE), lambda g, c: (cidx(c), g)),
        acc=pl.BlockSpec((None, 1, HEAD_PAD), lambda g, c: (g, 0, 0)),
    )


def _ssd_fwd(name, xbc, dtc, dtr, alr, alc, d_e, DI):
    T = xbc.shape[0]
    nc = T // CHUNK
    gw = DI // N_GROUPS
    hpg = gw // HEAD_DIM
    sp = _ssd_specs(T, DI, gw, nc, False)

    def body(xs_ref, b_ref, c_ref, dtc_ref, dtr_ref, alr_ref, alc_ref, de_ref, y_ref, hp_ref, h_scr):
        @pl.when(pl.program_id(1) == 0)
        def _():
            h_scr[...] = jnp.zeros_like(h_scr)

        hpv = h_scr[...]
        hp_ref[...] = hpv
        xs = xs_ref[...]
        bb = b_ref[...].astype(BF16)
        cb_ = c_ref[...].astype(BF16)
        cm = _ssd_common(dtc_ref[...], dtr_ref[...], alr_ref[...], alc_ref[...], gw)
        x = xs * cm["dt_e"]
        xb = x.astype(BF16)
        cbm = _dotb("nt", cb_, bb)
        causal = cm["li"] >= cm["si"]
        lane_head = lax.broadcasted_iota(jnp.int32, (1, gw), 1) // HEAD_DIM
        ydiag = jnp.zeros((CHUNK, gw), F32)
        for j in range(hpg):
            seg = cm["acs_c"][:, j:j + 1] - cm["acs_r"][j:j + 1, :]
            lmat = jnp.exp(jnp.where(causal, seg, -1e30))
            yj = _dotb("nn", cbm * lmat, xb)
            ydiag = jnp.where(lane_head == j, yj, ydiag)
        dec_e = _dotf(cm["dec_h"], cm["expand"])
        states = _dotb("tn", x * dec_e, bb)
        yoff = _dotb("nt", cb_, hpv) * cm["eacs_e"]
        y_ref[...] = ydiag + yoff + xs * de_ref[...]
        cd_rows = jnp.sum(cm["reduce"] * jnp.exp(cm["acs_last"]), axis=1, keepdims=True)
        h_scr[...] = cd_rows * hpv + states

    return pl.pallas_call(
        body, name=name, grid=(N_GROUPS, nc),
        in_specs=[sp["xs"], sp["bm"], sp["cm"], sp["dtc"], sp["dtr"], sp["alr"], sp["alc"], sp["de"]],
        out_specs=[sp["xs"], sp["hp"]],
        out_shape=[jax.ShapeDtypeStruct((T, DI), F32), jax.ShapeDtypeStruct((nc, N_GROUPS, gw, D_STATE), F32)],
        scratch_shapes=[pltpu.VMEM((gw, D_STATE), F32)],
        compiler_params=_cp("parallel", "arbitrary"),
    )(xbc, xbc, xbc, dtc, dtr, alr, alc, d_e)


def _ssd_bwd(name, dy, xbc, dtc, dtr, alr, alc, d_e, hprev, DI):
    T = xbc.shape[0]
    nc = T // CHUNK
    gw = DI // N_GROUPS
    hpg = gw // HEAD_DIM
    sp = _ssd_specs(T, DI, gw, nc, True)

    def body(dy_ref, xs_ref, b_ref, c_ref, dtc_ref, dtr_ref, alr_ref, alc_ref, de_ref, hp_ref,
             dxs_ref, db_ref, dc_ref, ddt_ref, dal_ref, dd_ref, dh_scr):
        first = pl.program_id(1) == 0

        @pl.when(first)
        def _():
            dh_scr[...] = jnp.zeros_like(dh_scr)

        xs = xs_ref[...]
        dyv = dy_ref[...]
        bb = b_ref[...].astype(BF16)
        cb_ = c_ref[...].astype(BF16)
        dtc_v = dtc_ref[...]
        cm = _ssd_common(dtc_v, dtr_ref[...], alr_ref[...], alc_ref[...], gw)
        hpv = hp_ref[...]
        hpb = hpv.astype(BF16)
        dhn = dh_scr[...]
        dhnb = dhn.astype(BF16)
        x = xs * cm["dt_e"]
        xb = x.astype(BF16)
        cbm = _dotb("nt", cb_, bb)
        causal = cm["li"] >= cm["si"]
        lane_head = lax.broadcasted_iota(jnp.int32, (1, gw), 1) // HEAD_DIM
        lane_pad = lax.broadcasted_iota(jnp.int32, (1, HEAD_PAD), 1)
        sub_pad = lax.broadcasted_iota(jnp.int32, (ROW_PAD, 1), 0)

        dxs = dyv * de_ref[...]
        dd_part = jnp.sum(_dotf(dyv * xs, cm["reduce"]), axis=0, keepdims=True)
        _accum(dd_ref, dd_part, first)

        dx = jnp.zeros((CHUNK, gw), F32)
        dcb = jnp.zeros((CHUNK, CHUNK), F32)
        dacs_c = jnp.zeros((CHUNK, HEAD_PAD), F32)
        dacs_r = jnp.zeros((ROW_PAD, CHUNK), F32)
        for j in range(hpg):
            seg = cm["acs_c"][:, j:j + 1] - cm["acs_r"][j:j + 1, :]
            lmat = jnp.exp(jnp.where(causal, seg, -1e30))
            mmat = cbm * lmat
            dym = jnp.where(lane_head == j, dyv, 0.0).astype(BF16)
            dm = _dotb("nt", dym, xb)
            dx = dx + _dotb("tn", mmat, dym)
            dcb = dcb + dm * lmat
            dseg = dm * mmat
            dacs_c = dacs_c + jnp.where(lane_pad == j, jnp.sum(dseg, axis=1, keepdims=True), 0.0)
            dacs_r = dacs_r - jnp.where(sub_pad == j, jnp.sum(dseg, axis=0, keepdims=True), 0.0)
        dc = _dotb("nn", dcb, bb)
        db = _dotb("tn", dcb, cb_)

        gm = _dotb("nt", cb_, hpb)
        dgm = dyv * cm["eacs_e"]
        dacs_c = dacs_c + _dotf(dgm * gm, cm["reduce"])
        dc = dc + _dotb("nn", dgm, hpb)
        dhp = _dotb("tn", dgm, cb_)

        cd_row = jnp.exp(cm["acs_last"])
        cd_rows = jnp.sum(cm["reduce"] * cd_row, axis=1, keepdims=True)
        dhp = dhp + cd_rows * dhn
        rsum = jnp.sum(dhn * hpv, axis=1, keepdims=True)
        dacs_last = jnp.sum(cm["reduce"] * rsum, axis=0, keepdims=True) * cd_row
        dec_e = _dotf(cm["dec_h"], cm["expand"])
        xdec = x * dec_e
        dxdec = _dotb("nt", bb, dhnb)
        db = db + _dotb("nn", xdec, dhnb)
        dx = dx + dxdec * dec_e
        tdec = _dotf(dxdec * x, cm["reduce"]) * cm["dec_h"]
        dacs_c = dacs_c - tdec
        dacs_last = dacs_last + jnp.sum(tdec, axis=0, keepdims=True)
        row_id = lax.broadcasted_iota(jnp.int32, (CHUNK, 1), 0)
        dacs_c = dacs_c + jnp.where(row_id == CHUNK - 1, dacs_last, 0.0)

        dxs_ref[...] = dxs + dx * cm["dt_e"]
        ddt = _dotf(dx * xs, cm["reduce"])
        dda = _dotf(cm["trit"], dacs_c)
        dda_r = _dotf(dacs_r, cm["tri"])
        dda_rp = jnp.concatenate([dda_r, jnp.zeros((HEAD_PAD - ROW_PAD, CHUNK), F32)], axis=0)
        eye = (cm["li"] == cm["si"]).astype(F32)
        dda = dda + lax.dot_general(eye, dda_rp, _DIMS["nt"], precision=HI, preferred_element_type=F32)
        ddt_ref[...] = ddt + dda * cm["a_row"]
        _accum(dal_ref, jnp.sum(dda * dtc_v, axis=0, keepdims=True) * cm["a_row"], first)
        db_ref[...] = db
        dc_ref[...] = dc
        dh_scr[...] = dhp

    gs = D_STATE * N_GROUPS
    return pl.pallas_call(
        body, name=name, grid=(N_GROUPS, nc),
        in_specs=[sp["xs"], sp["xs"], sp["bm"], sp["cm"], sp["dtc"], sp["dtr"], sp["alr"], sp["alc"],
                  sp["de"], sp["hp"]],
        out_specs=[sp["xs"], sp["bc"], sp["bc"], sp["dtc"], sp["acc"], sp["acc"]],
        out_shape=[jax.ShapeDtypeStruct((T, DI), F32), jax.ShapeDtypeStruct((T, gs), F32),
                   jax.ShapeDtypeStruct((T, gs), F32), jax.ShapeDtypeStruct((N_GROUPS, T, HEAD_PAD), F32),
                   jax.ShapeDtypeStruct((N_GROUPS, 1, HEAD_PAD), F32),
                   jax.ShapeDtypeStruct((N_GROUPS, 1, HEAD_PAD), F32)],
        scratch_shapes=[pltpu.VMEM((gw, D_STATE), F32)],
        compiler_params=_cp("parallel", "arbitrary"),
    )(dy, xbc, xbc, xbc, dtc, dtr, alr, alc, d_e, hprev)


def _adam_math(w, g, m, v):
    m = ADAM_B1 * m + (1.0 - ADAM_B1) * g
    v = ADAM_B2 * v + (1.0 - ADAM_B2) * (g * g)
    m_hat = m / (1.0 - ADAM_B1 ** ADAM_STEP)
    v_hat = v / (1.0 - ADAM_B2 ** ADAM_STEP)
    delta = -ADAM_LR * (m_hat / (jnp.sqrt(v_hat) + ADAM_EPS) + ADAM_WD * w)
    return delta, m, v


def _adamw(name, w, m, v, grads):
    L, R, C = w.shape
    tr = _rows(R, C * 4, ADAM_BLOCK_BYTES)
    full = pl.BlockSpec((None, tr, C), lambda l, i: (l, i, 0))
    g_specs = [pl.BlockSpec((tr, C), functools.partial(lambda l, i, ll: (jnp.where(l == ll, i, 0), 0), ll=ll))
               for ll in range(L)]

    def body(w_ref, m_ref, v_ref, *rest):
        g_refs = rest[:L]
        go_ref, d_ref, mo_ref, vo_ref = rest[L:]
        l = pl.program_id(0)
        g = g_refs[0][...]
        for ll in range(1, L):
            g = jnp.where(l == ll, g_refs[ll][...], g)
        delta, mn, vn = _adam_math(w_ref[...], g, m_ref[...], v_ref[...])
        go_ref[...] = g
        d_ref[...] = delta
        mo_ref[...] = mn
        vo_ref[...] = vn

    return pl.pallas_call(
        body, name=name, grid=(L, R // tr), in_specs=[full] * 3 + g_specs, out_specs=[full] * 4,
        out_shape=[jax.ShapeDtypeStruct((L, R, C), F32)] * 4,
        compiler_params=_cp("arbitrary", "arbitrary"),
    )(w, m, v, *grads)


def _sum_peers(name, gathered):
    n, R, C = gathered.shape

    def body(g_ref, o_ref):
        acc = g_ref[0]
        for d in range(1, n):
            acc = acc + g_ref[d]
        o_ref[...] = acc

    return pl.pallas_call(body, name=name, out_shape=jax.ShapeDtypeStruct((R, C), F32))(gathered)


_ANY = pl.BlockSpec(memory_space=pl.ANY)


def _place():
    x, y, c = lax.axis_index("x"), lax.axis_index("y"), lax.axis_index("c")
    chips = [(1 - x, y), (x, 1 - y), (1 - x, 1 - y)]
    return x, y, c, chips


def _allgather_small(name, v):
    R, C = v.shape

    def body(v_ref, o_ref, send_sems, recv_sems):
        x, y, c, _ = _place()
        me = 4 * x + 2 * y + c
        o_ref[me] = v_ref[...]
        copies = []
        for k in range(1, N_DEV):
            px, py, pc = x ^ (k >> 2), y ^ ((k >> 1) & 1), c ^ (k & 1)
            copies.append(pltpu.make_async_remote_copy(
                src_ref=v_ref, dst_ref=o_ref.at[me], send_sem=send_sems.at[k - 1], recv_sem=recv_sems.at[k - 1],
                device_id=(px, py, pc), device_id_type=MESH))
        for cp in copies:
            cp.start()
        for cp in copies:
            cp.wait()

    return pl.pallas_call(
        body, name=name, out_shape=jax.ShapeDtypeStruct((N_DEV, R, C), F32),
        in_specs=[pl.BlockSpec(memory_space=pltpu.VMEM)], out_specs=pl.BlockSpec(memory_space=pltpu.VMEM),
        scratch_shapes=[pltpu.SemaphoreType.DMA((N_DEV - 1,)), pltpu.SemaphoreType.DMA((N_DEV - 1,))],
    )(v)


def _gather_weights(name, shards):
    n = len(shards)

    def body(*refs):
        src = refs[:n]
        dst = refs[n:2 * n]
        ici_s, ici_r, d2d_s, d2d_r, own_s, own_r = refs[2 * n:]
        x, y, c, chips = _place()
        a = 2 * x + y
        own = [pltpu.make_async_remote_copy(
            src_ref=src[i], dst_ref=dst[i].at[a], send_sem=own_s.at[i], recv_sem=own_r.at[i],
            device_id=(x, y, 1 - c), device_id_type=MESH) for i in range(n)]
        for cp in own:
            cp.start()
        first = []
        for i in range(n):
            for k, (px, py) in enumerate(chips):
                first.append(pltpu.make_async_remote_copy(
                    src_ref=src[i].at[c], dst_ref=dst[i].at[a, c], send_sem=ici_s.at[i, k], recv_sem=ici_r.at[i, k],
                    device_id=(px, py, c), device_id_type=MESH))
        for cp in first:
            cp.start()
        passed = []
        for i in range(n):
            for k, (px, py) in enumerate(chips):
                b = 2 * px + py
                pltpu.make_async_remote_copy(
                    src_ref=src[i].at[c], dst_ref=dst[i].at[b, c], send_sem=ici_s.at[i, k], recv_sem=ici_r.at[i, k],
                    device_id=(px, py, c), device_id_type=MESH).wait_recv()
                fwd = pltpu.make_async_remote_copy(
                    src_ref=dst[i].at[b, c], dst_ref=dst[i].at[b, c], send_sem=d2d_s.at[i, k], recv_sem=d2d_r.at[i, k],
                    device_id=(x, y, 1 - c), device_id_type=MESH)
                fwd.start()
                passed.append(fwd)
        for i in range(n):
            for k, (px, py) in enumerate(chips):
                b = 2 * px + py
                pltpu.make_async_remote_copy(
                    src_ref=dst[i].at[b, 1 - c], dst_ref=dst[i].at[b, 1 - c], send_sem=d2d_s.at[i, k],
                    recv_sem=d2d_r.at[i, k], device_id=(x, y, 1 - c), device_id_type=MESH).wait_recv()
        for cp in first + passed:
            cp.wait_send()
        for cp in own:
            cp.wait()

    sems = [pltpu.SemaphoreType.DMA((n, 3))] * 4 + [pltpu.SemaphoreType.DMA((n,))] * 2
    return pl.pallas_call(
        body, name=name, in_specs=[_ANY] * n, out_specs=[_ANY] * n,
        out_shape=[jax.ShapeDtypeStruct((N_CHIPS,) + s.shape, s.dtype) for s in shards],
        scratch_shapes=sems,
    )(*shards)


def _pair_exchange(name, parts):
    n = len(parts)

    def body(*refs):
        src = refs[:n]
        dst = refs[n:2 * n]
        send_sems, recv_sems = refs[2 * n:]
        x, y, c, _ = _place()
        copies = [pltpu.make_async_remote_copy(
            src_ref=src[i].at[:, 1 - c], dst_ref=dst[i], send_sem=send_sems.at[i], recv_sem=recv_sems.at[i],
            device_id=(x, y, 1 - c), device_id_type=MESH) for i in range(n)]
        for cp in copies:
            cp.start()
        for cp in copies:
            cp.wait()

    return pl.pallas_call(
        body, name=name, in_specs=[_ANY] * n, out_specs=[_ANY] * n,
        out_shape=[jax.ShapeDtypeStruct((p.shape[0],) + p.shape[2:], p.dtype) for p in parts],
        scratch_shapes=[pltpu.SemaphoreType.DMA((n,)), pltpu.SemaphoreType.DMA((n,))],
    )(*parts)


def _chip_exchange(name, parts):
    n = len(parts)

    def body(*refs):
        src = refs[:n]
        dst = refs[n:2 * n]
        send_sems, recv_sems = refs[2 * n:]
        x, y, c, chips = _place()
        copies = []
        for i in range(n):
            for k, (px, py) in enumerate(chips):
                copies.append(pltpu.make_async_remote_copy(
                    src_ref=src[i].at[2 * px + py], dst_ref=dst[i].at[k], send_sem=send_sems.at[i, k],
                    recv_sem=recv_sems.at[i, k], device_id=(px, py, c), device_id_type=MESH))
        for cp in copies:
            cp.start()
        for cp in copies:
            cp.wait()

    return pl.pallas_call(
        body, name=name, in_specs=[_ANY] * n, out_specs=[_ANY] * n,
        out_shape=[jax.ShapeDtypeStruct((3,) + p.shape[1:], p.dtype) for p in parts],
        scratch_shapes=[pltpu.SemaphoreType.DMA((n, 3)), pltpu.SemaphoreType.DMA((n, 3))],
    )(*parts)


def _half_exchange(name, shards):
    n = len(shards)

    def body(*refs):
        buf = refs[n:2 * n]
        send_sems, recv_sems = refs[2 * n:]
        x, y, c, _ = _place()
        copies = [pltpu.make_async_remote_copy(
            src_ref=buf[i].at[c], dst_ref=buf[i].at[c], send_sem=send_sems.at[i], recv_sem=recv_sems.at[i],
            device_id=(x, y, 1 - c), device_id_type=MESH) for i in range(n)]
        for cp in copies:
            cp.start()
        for cp in copies:
            cp.wait()

    return pl.pallas_call(
        body, name=name, in_specs=[_ANY] * n, out_specs=[_ANY] * n,
        out_shape=[jax.ShapeDtypeStruct(s.shape, s.dtype) for s in shards],
        input_output_aliases={i: i for i in range(n)},
        scratch_shapes=[pltpu.SemaphoreType.DMA((n,)), pltpu.SemaphoreType.DMA((n,))],
    )(*shards)


def _pair_add(name, pos, part, sib):
    Q, _, R2, C = part.shape
    tr = _pick(R2, 256)

    def body(pos_ref, p_ref, s_ref, o_ref):
        o_ref[...] = (p_ref[...].astype(F32) + s_ref[...].astype(F32)).astype(BF16)

    return pl.pallas_call(
        body, name=name,
        grid_spec=pltpu.PrefetchScalarGridSpec(
            num_scalar_prefetch=1, grid=(Q, R2 // tr),
            in_specs=[pl.BlockSpec((None, None, tr, C), lambda q, i, pos_ref: (q, pos_ref[1], i, 0)),
                      pl.BlockSpec((None, tr, C), lambda q, i, pos_ref: (q, i, 0))],
            out_specs=pl.BlockSpec((None, tr, C), lambda q, i, pos_ref: (q, i, 0))),
        out_shape=jax.ShapeDtypeStruct((Q, R2, C), BF16),
        compiler_params=_cp("parallel", "parallel"),
    )(pos, part, sib)


def _chip_sum(name, pos, own, got):
    _, R2, C = own.shape
    tr = _pick(R2, 256)

    def body(pos_ref, o_ref, g_ref, out_ref):
        acc = o_ref[...].astype(F32)
        for k in range(3):
            acc = acc + g_ref[k].astype(F32)
        out_ref[...] = acc

    return pl.pallas_call(
        body, name=name,
        grid_spec=pltpu.PrefetchScalarGridSpec(
            num_scalar_prefetch=1, grid=(R2 // tr,),
            in_specs=[pl.BlockSpec((None, tr, C), lambda i, pos_ref: (pos_ref[0], i, 0)),
                      pl.BlockSpec((3, tr, C), lambda i, pos_ref: (0, i, 0))],
            out_specs=pl.BlockSpec((None, tr, C), lambda i, pos_ref: (pos_ref[1], i, 0))),
        out_shape=jax.ShapeDtypeStruct((2, R2, C), F32),
        compiler_params=_cp("parallel"),
    )(pos, own, got)


def _reduce_scatter(tag, pos, parts):
    split = [p.reshape(p.shape[0], 2, p.shape[1] // 2, p.shape[2]) for p in parts]
    sib = _pair_exchange(tag + "_pair", split)
    chip = [_pair_add(f"{tag}_add{i}", pos, split[i], sib[i]) for i in range(len(parts))]
    got = _chip_exchange(tag + "_chip", chip)
    halves = [_chip_sum(f"{tag}_sum{i}", pos, chip[i], got[i]) for i in range(len(parts))]
    full = _half_exchange(tag + "_half", halves)
    return [f.reshape(p.shape[1], p.shape[2]) for f, p in zip(full, parts)]


def _pad_to(v, mult):
    n = v.shape[0]
    return jnp.pad(v, (0, (-n) % mult))


def kernel(x, p, pool_w, pool_scale, ssm_in_w, ssm_conv_w, ssm_conv_b, ssm_dt_bias, ssm_a_log, ssm_d, ssm_norm_w, ssm_out_w, mlp_w1, mlp_w2, ln_g, ln_b, ple_w, ple_gate_w, loss_target, m_pool_w, m_pool_scale, m_ssm_in_w, m_ssm_conv_w, m_ssm_conv_b, m_ssm_dt_bias, m_ssm_a_log, m_ssm_d, m_ssm_norm_w, m_ssm_out_w, m_mlp_w1, m_mlp_w2, m_ln_g, m_ln_b, m_ple_w, m_ple_gate_w, v_pool_w, v_pool_scale, v_ssm_in_w, v_ssm_conv_w, v_ssm_conv_b, v_ssm_dt_bias, v_ssm_a_log, v_ssm_d, v_ssm_norm_w, v_ssm_out_w, v_mlp_w1, v_mlp_w2, v_ln_g, v_ln_b, v_ple_w, v_ple_gate_w):
    T, D = x.shape[1], x.shape[2]
    NG = len(POOL_WINDOWS)
    GD = D // NG
    DI = ssm_out_w.shape[1] * N_CHIPS
    H = ssm_dt_bias.shape[1]
    HPG = H // N_GROUPS
    GW = DI // N_GROUPS
    GS = N_GROUPS * D_STATE
    CD = DI + 2 * GS
    DF = mlp_w1.shape[2] * N_CHIPS
    PD = ple_w.shape[1]
    NIN = ssm_in_w.shape[2]

    xi, yi, ci = lax.axis_index("x"), lax.axis_index("y"), lax.axis_index("c")
    chip = 2 * xi + yi
    pos = jnp.stack([chip, ci]).astype(jnp.int32)

    x0 = x[0]
    tgt = loss_target[0]
    p0b, p1b = p[0, 0].astype(BF16), p[1, 0].astype(BF16)

    def halves(w):
        return w.astype(BF16).reshape((2, w.shape[0] // 2) + w.shape[1:])

    sh_pool = pool_w[0].astype(BF16)
    sh_in = halves(ssm_in_w[0])
    sh_out = halves(ssm_out_w[0])
    sh_w1 = mlp_w1.astype(BF16)
    sh_w2 = mlp_w2.astype(BF16)
    sh_pw = ple_w.astype(BF16)
    sh_gw = ple_gate_w.astype(BF16)
    sh_pool = sh_pool.reshape((2, NG // 2) + sh_pool.shape[1:])
    g_pool, g_in, g_out, g_w1, g_w2, g_pw, g_gw = _gather_weights(
        "gather_weights", [sh_pool, sh_in, sh_out, sh_w1, sh_w2, sh_pw, sh_gw])
    w_pool = jnp.transpose(g_pool.reshape(N_CHIPS, NG, GD // N_CHIPS, GD), (1, 0, 2, 3)).reshape(NG, GD, GD)
    w_in = jnp.concatenate([g_in[q].reshape(D, NIN) for q in range(N_CHIPS)], axis=1)
    w_z, w_xbc, w_dt = w_in[:, :DI], w_in[:, DI:DI + CD], w_in[:, DI + CD:]
    w_out = g_out.reshape(DI, D)

    small_sh = jnp.concatenate([ssm_conv_w[0].reshape(-1), ssm_conv_b[0], ssm_norm_w[0],
                                ln_g.reshape(-1), ln_b.reshape(-1)])
    n_sh = small_sh.shape[0]
    small_all = _allgather_small("gather_small", _pad_to(small_sh, 1024).reshape(-1, 128))
    small_all = small_all.reshape(N_DEV, -1)[0::2, :n_sh]
    cdq, niq, dq = CD // N_CHIPS, DI // N_CHIPS, D // N_CHIPS
    o = 0
    conv_w = jnp.concatenate([small_all[q, o:o + CONV_K * cdq].reshape(CONV_K, cdq) for q in range(N_CHIPS)], axis=1)
    o += CONV_K * cdq
    conv_b = small_all[:, o:o + cdq].reshape(1, CD)
    o += cdq
    norm_w = small_all[:, o:o + niq].reshape(1, DI)
    o += niq
    lng = jnp.transpose(small_all[:, o:o + 4 * dq].reshape(N_CHIPS, 2, 2, dq), (1, 2, 0, 3)).reshape(2, 2, 1, D)
    o += 4 * dq
    lnb = jnp.transpose(small_all[:, o:o + 4 * dq].reshape(N_CHIPS, 2, 2, dq), (1, 2, 0, 3)).reshape(2, 2, 1, D)

    pooled = _pool_fwd("pool_fwd", x0)
    hraw = _mm("pool_mm", "nn", pooled, w_pool, T, D, GD, tn=GD, tk=GD,
               a_spec=lambda tm, tn, tk: pl.BlockSpec((tm, tk), lambda i, j, k: (i, j)),
               b_spec=lambda tm, tn, tk: pl.BlockSpec((None, tk, tn), lambda i, j, k: (j, 0, 0)))
    x1, x1b, xh1, rs1 = _res_ln("ln00", x0, hraw, lng[0, 0], lnb[0, 0], scale=pool_scale)

    def mlp_fwd(l, xb):
        a = _mm(f"mlp{l}_up", "nn", xb, g_w1, T, DF, D, tn=min(1024, DF // N_CHIPS),
                b_spec=_colshard_b(l, DF // N_CHIPS))
        h = _mm(f"mlp{l}_down", "nn", a, g_w2, T, D, DF, tk=min(512, DF // N_CHIPS),
                b_spec=_rowshard_b(l, DF // N_CHIPS),
                pro=lambda t: jnp.square(jnp.maximum(t, 0.0)))
        return a, h

    def ple_fwd(l, xb, pb):
        gl = _mm(f"gate{l}_logit", "nn", xb, g_gw, T, D, D, tk=min(512, D // N_CHIPS),
                 b_spec=_rowshard_b(l, D // N_CHIPS))
        e = _mm(f"gate{l}_emb", "nn", pb, g_pw, T, D, PD, tn=min(1024, D // N_CHIPS),
                b_spec=_colshard_b(l, D // N_CHIPS))
        return gl, e

    a0, h0 = mlp_fwd(0, x1b)
    x2, x2b, xh2, rs2 = _res_ln("ln01", x1, h0, lng[0, 1], lnb[0, 1])
    gl0, e0 = ple_fwd(0, x2b, p0b)
    x3, x3b, gate0 = _gate_fwd("gate0", x2, gl0, e0)

    z = _mm("ssm_in_z", "nn", x3b, w_z, T, DI, D)
    xbc_pre = _mm("ssm_in_xbc", "nn", x3b, w_xbc, T, CD, D)
    dt_pre = _mm("ssm_in_dt", "nn", x3b, w_dt, T, H, D)
    xbc = _conv_fwd("conv_fwd", xbc_pre, conv_w, conv_b)
    dt = _dt_fwd("dt_fwd", dt_pre, ssm_dt_bias)
    dt_g = jnp.transpose(dt.reshape(T, N_GROUPS, HPG), (1, 0, 2))
    dtc = jnp.pad(dt_g, ((0, 0), (0, 0), (0, HEAD_PAD - HPG)))
    dtr = jnp.pad(jnp.transpose(dt_g, (0, 2, 1)), ((0, 0), (0, ROW_PAD - HPG), (0, 0)))
    al_g = ssm_a_log.reshape(N_GROUPS, HPG)
    alr = jnp.pad(al_g, ((0, 0), (0, HEAD_PAD - HPG)))[:, None, :]
    alc = jnp.pad(al_g, ((0, 0), (0, ROW_PAD - HPG)))[:, :, None]
    d_e = jnp.repeat(ssm_d.reshape(N_GROUPS, HPG), HEAD_DIM, axis=1)[:, None, :]
    ysc, hprev = _ssd_fwd("ssd_fwd", xbc, dtc, dtr, alr, alc, d_e, DI)
    ynb, rsn = _gnorm_fwd("gnorm_fwd", ysc, z, norm_w)
    h1 = _mm("ssm_out", "nn", ynb, w_out, T, D, DI)
    x4, x4b, xh4, rs4 = _res_ln("ln10", x3, h1, lng[1, 0], lnb[1, 0])
    a1, h2 = mlp_fwd(1, x4b)
    x5, x5b, xh5, rs5 = _res_ln("ln11", x4, h2, lng[1, 1], lnb[1, 1])
    gl1, e1 = ple_fwd(1, x5b, p1b)
    dx6, gate1, loss_parts = _gate_loss("gate1_loss", x5, gl1, e1, tgt)
    loss_local = jnp.sum(loss_parts[0::8, 0])

    def ple_bwd(l, dxo, gate, e, xb, pb):
        dgl, de = _gate_bwd(f"gate{l}_bwd", dxo, gate, e)
        d_gw = _mm(f"gate{l}_dw", "tn", xb, dgl, D, D, T, out_dtype=BF16).reshape(N_CHIPS, D // N_CHIPS, D)
        d_pw = _mm(f"gate{l}_dpw", "tn", pb, de, PD, D, T, out_dtype=BF16, tn=min(1024, D // N_CHIPS),
                   o_shape=(N_CHIPS, PD, D // N_CHIPS), o_spec=_colshard_o(D // N_CHIPS))
        dx = _mm(f"gate{l}_dx", "nt", dgl, g_gw, T, D, D, tn=min(1024, D // N_CHIPS),
                 b_spec=_rowshard_bt(l, D // N_CHIPS), epi=lambda acc, r: acc + r, extras=[dxo])
        return dx, d_gw, d_pw

    def mlp_bwd(l, du, dub, a, xb):
        d_w2 = _mm(f"mlp{l}_dw2", "tn", a, dub, DF, D, T, out_dtype=BF16,
                   pro=lambda t: jnp.square(jnp.maximum(t, 0.0))).reshape(N_CHIPS, DF // N_CHIPS, D)
        da = _mm(f"mlp{l}_da", "nt", dub, g_w2, T, DF, D, tn=min(1024, DF // N_CHIPS),
                 b_spec=_rowshard_bt(l, DF // N_CHIPS), out_dtype=BF16,
                 epi=lambda acc, av: acc * (2.0 * jnp.maximum(av, 0.0)), extras=[a])
        d_w1 = _mm(f"mlp{l}_dw1", "tn", xb, da, D, DF, T, out_dtype=BF16, tn=min(1024, DF // N_CHIPS),
                   o_shape=(N_CHIPS, D, DF // N_CHIPS), o_spec=_colshard_o(DF // N_CHIPS))
        dx = _mm(f"mlp{l}_dx", "nt", da, g_w1, T, D, DF, tk=min(512, DF // N_CHIPS),
                 b_spec=_colshard_bt(l, DF // N_CHIPS), epi=lambda acc, r: acc + ALPHA * r, extras=[du])
        return dx, d_w1, d_w2

    dx5, d_gw1, d_pw1 = ple_bwd(1, dx6, gate1, e1, x5b, p1b)
    du5, du5b, dg11, db11 = _ln_bwd("ln11_bwd", dx5, xh5, rs5, lng[1, 1])
    dx4, d_w1_1, d_w2_1 = mlp_bwd(1, du5, du5b, a1, x4b)
    du4, du4b, dg10, db10 = _ln_bwd("ln10_bwd", dx4, xh4, rs4, lng[1, 0])
    d_wout = _mm("ssm_out_dw", "tn", ynb, du4b, DI, D, T, out_dtype=BF16).reshape(N_CHIPS, DI // N_CHIPS, D)
    dyn = _mm("ssm_out_dx", "nt", du4b, w_out, T, DI, D)
    dysc, dzb, dnorm_w = _gnorm_bwd("gnorm_bwd", dyn, ysc, z, norm_w, rsn)
    dxs, dbm, dcm, ddt_g, dalog_g, dd_g = _ssd_bwd("ssd_bwd", dysc, xbc, dtc, dtr, alr, alc, d_e, hprev, DI)
    dconv, dconv_w, dconv_b = _conv_bwd_a("conv_bwd_a", dxs, dbm, dcm, xbc_pre, conv_w, conv_b)
    dxbcb = _conv_bwd_b("conv_bwd_b", dconv, conv_w)
    ddt = jnp.transpose(ddt_g[:, :, :HPG], (1, 0, 2)).reshape(T, H)
    _, ddtpb, ddt_bias = _dt_bwd("dt_bwd", ddt, dt_pre, ssm_dt_bias)
    da_log = dalog_g[:, 0, :HPG].reshape(1, H)
    dd_skip = dd_g[:, 0, :HPG].reshape(1, H)
    d_wz = _mm("ssm_in_dwz", "tn", x3b, dzb, D, DI, T, out_dtype=BF16)
    d_wx = _mm("ssm_in_dwx", "tn", x3b, dxbcb, D, CD, T, out_dtype=BF16)
    d_wdt = _mm("ssm_in_dwdt", "tn", x3b, ddtpb, D, H, T, out_dtype=BF16)
    d_win = jnp.transpose(jnp.concatenate([d_wz, d_wx, d_wdt], axis=1).reshape(D, N_CHIPS, NIN), (1, 0, 2))
    dx3 = _mm("ssm_in_dxz", "nt", dzb, w_z, T, D, DI, epi=lambda acc, r: acc + ALPHA * r, extras=[du4])
    dx3 = _mm("ssm_in_dxx", "nt", dxbcb, w_xbc, T, D, CD, epi=lambda acc, r: acc + r, extras=[dx3])
    dx3 = _mm("ssm_in_dxdt", "nt", ddtpb, w_dt, T, D, H, epi=lambda acc, r: acc + r, extras=[dx3])

    dx2, d_gw0, d_pw0 = ple_bwd(0, dx3, gate0, e0, x2b, p0b)
    du2, du2b, dg01, db01 = _ln_bwd("ln01_bwd", dx2, xh2, rs2, lng[0, 1])
    dx1, d_w1_0, d_w2_0 = mlp_bwd(0, du2, du2b, a0, x1b)
    du1, dhrb, dg00, db00, dscale = _ln_bwd("ln00_bwd", dx1, xh1, rs1, lng[0, 0], hraw=hraw, scale=pool_scale)
    d_wpool = _mm("pool_dw", "tn", pooled, dhrb, D, GD, T, tm=GD, tn=GD, out_dtype=BF16,
                  b_spec=lambda tm, tn, tk: pl.BlockSpec((tk, tn), lambda i, j, k: (k, i)),
                  o_shape=(NG, GD, GD), o_spec=lambda tm, tn, tk: pl.BlockSpec((None, tm, tn), lambda i, j, k: (i, 0, 0)))
    dpooled = _mm("pool_dx", "nt", dhrb, w_pool, T, D, GD, tn=GD, tk=GD,
                  a_spec=lambda tm, tn, tk: pl.BlockSpec((tm, tk), lambda i, j, k: (i, j)),
                  b_spec=lambda tm, tn, tk: pl.BlockSpec((None, tn, tk), lambda i, j, k: (j, 0, 0)))
    grad_x = _pool_bwd("pool_bwd", dpooled, du1)
    d_wpool = jnp.transpose(d_wpool.reshape(NG, N_CHIPS, GD // N_CHIPS, GD), (1, 0, 2, 3)).reshape(N_CHIPS, NG * GD // N_CHIPS, GD)

    red = _reduce_scatter("rs", pos, [d_wpool, d_win, d_wout, d_w1_0, d_w1_1, d_w2_0, d_w2_1,
                                      d_gw0, d_gw1, d_pw0, d_pw1])
    r_pool, r_in, r_out, r_w1_0, r_w1_1, r_w2_0, r_w2_1, r_gw0, r_gw1, r_pw0, r_pw1 = red

    dln_g = jnp.stack([jnp.stack([dg00, dg01]), jnp.stack([dg10, dg11])]).reshape(-1)
    dln_b = jnp.stack([jnp.stack([db00, db01]), jnp.stack([db10, db11])]).reshape(-1)
    small_g = jnp.concatenate([dscale.reshape(-1), dconv_w.reshape(-1), dconv_b.reshape(-1), ddt_bias.reshape(-1),
                               da_log.reshape(-1), dd_skip.reshape(-1), dnorm_w.reshape(-1), dln_g, dln_b,
                               loss_local.reshape(1)])
    n_sg = small_g.shape[0]
    sg_all = _allgather_small("gather_small_grads", _pad_to(small_g, 1024).reshape(-1, 128))
    sg = _sum_peers("sum_small_grads", sg_all).reshape(-1)[:n_sg]
    o = 0

    def take(nel):
        nonlocal o
        v = sg[o:o + nel]
        o += nel
        return v

    g_scale = take(D).reshape(1, D)
    g_conv_w_full = take(CONV_K * CD).reshape(CONV_K, CD)
    g_conv_b_full = take(CD).reshape(1, CD)
    g_dt_bias = take(H).reshape(1, H)
    g_a_log = take(H).reshape(1, H)
    g_d = take(H).reshape(1, H)
    g_norm_full = take(DI).reshape(1, DI)
    g_lng_full = take(4 * D).reshape(2, 2, D)
    g_lnb_full = take(4 * D).reshape(2, 2, D)
    loss = take(1).reshape(())
    g_conv_w = lax.dynamic_slice_in_dim(g_conv_w_full, chip * cdq, cdq, axis=1)[None]
    g_conv_b = lax.dynamic_slice_in_dim(g_conv_b_full, chip * cdq, cdq, axis=1)
    g_norm = lax.dynamic_slice_in_dim(g_norm_full, chip * niq, niq, axis=1)
    g_lng = lax.dynamic_slice_in_dim(g_lng_full, chip * dq, dq, axis=2)
    g_lnb = lax.dynamic_slice_in_dim(g_lnb_full, chip * dq, dq, axis=2)

    def adam(name, w, m, v, grads):
        shp = w.shape
        L = len(grads)
        w3, m3, v3 = (t.reshape((L, -1, shp[-1])) for t in (w, m, v))
        return [t.reshape(shp) for t in _adamw(name, w3, m3, v3, grads)]

    big = {
        "pool_w": adam("adam_pool_w", pool_w, m_pool_w, v_pool_w, [r_pool]),
        "ssm_in_w": adam("adam_ssm_in_w", ssm_in_w, m_ssm_in_w, v_ssm_in_w, [r_in]),
        "ssm_out_w": adam("adam_ssm_out_w", ssm_out_w, m_ssm_out_w, v_ssm_out_w, [r_out]),
        "mlp_w1": adam("adam_mlp_w1", mlp_w1, m_mlp_w1, v_mlp_w1, [r_w1_0, r_w1_1]),
        "mlp_w2": adam("adam_mlp_w2", mlp_w2, m_mlp_w2, v_mlp_w2, [r_w2_0, r_w2_1]),
        "ple_w": adam("adam_ple_w", ple_w, m_ple_w, v_ple_w, [r_pw0, r_pw1]),
        "ple_gate_w": adam("adam_ple_gate_w", ple_gate_w, m_ple_gate_w, v_ple_gate_w, [r_gw0, r_gw1]),
    }

    small = [("pool_scale", pool_scale, m_pool_scale, v_pool_scale, g_scale),
             ("ssm_conv_w", ssm_conv_w, m_ssm_conv_w, v_ssm_conv_w, g_conv_w),
             ("ssm_conv_b", ssm_conv_b, m_ssm_conv_b, v_ssm_conv_b, g_conv_b),
             ("ssm_dt_bias", ssm_dt_bias, m_ssm_dt_bias, v_ssm_dt_bias, g_dt_bias),
             ("ssm_a_log", ssm_a_log, m_ssm_a_log, v_ssm_a_log, g_a_log),
             ("ssm_d", ssm_d, m_ssm_d, v_ssm_d, g_d),
             ("ssm_norm_w", ssm_norm_w, m_ssm_norm_w, v_ssm_norm_w, g_norm),
             ("ln_g", ln_g, m_ln_g, v_ln_g, g_lng),
             ("ln_b", ln_b, m_ln_b, v_ln_b, g_lnb)]

    def pack(idx):
        flat = _pad_to(jnp.concatenate([s[idx].reshape(-1) for s in small]), 1024)
        return flat.reshape(1, -1, 128)

    sm_out = _adamw("adam_small", pack(1), pack(2), pack(3), [pack(4)[0]])
    small_res = {}
    o = 0
    for s in small:
        nel = s[1].size
        small_res[s[0]] = [t.reshape(-1)[o:o + nel].reshape(s[1].shape) for t in sm_out]
        o += nel

    order = ["pool_w", "pool_scale", "ssm_in_w", "ssm_conv_w", "ssm_conv_b", "ssm_dt_bias", "ssm_a_log", "ssm_d",
             "ssm_norm_w", "ssm_out_w", "mlp_w1", "mlp_w2", "ln_g", "ln_b", "ple_w", "ple_gate_w"]
    res = {**big, **small_res}
    outs = [loss, grad_x[None]]
    for kind in range(4):
        outs += [res[nm][kind] for nm in order]
    return tuple(outs)
```

```python
import functools

import jax
import jax.numpy as jnp
from jax import lax
from jax.experimental import pallas as pl
from jax.experimental.pallas import tpu as pltpu

F32 = jnp.float32
BF16 = jnp.bfloat16
HI = lax.Precision.HIGHEST
MESH = pl.DeviceIdType.MESH

DEPTH = 2
ALPHA = (2.0 * DEPTH) ** 0.25
LN_EPS = 1e-5
RMS_EPS = 1e-5
POOL_WINDOWS = (2, 4, 8, 16)
POOL_HALO = 16
HEAD_DIM = 64
N_GROUPS = 8
D_STATE = 128
CHUNK = 128
CONV_K = 4
CONV_HALO = 8
HEAD_PAD = 128
ROW_PAD = 8
N_CHIPS = 4
N_DEV = 8
ADAM_LR = 0.001
ADAM_B1 = 0.9
ADAM_B2 = 0.999
ADAM_EPS = 1e-08
ADAM_WD = 0.01
ADAM_STEP = 10
VMEM_LIMIT = 56 * 1024 * 1024
ADAM_BLOCK_BYTES = 1024 * 1024
MM_VMEM_BUDGET = 40 * 1024 * 1024


def _cp(*sem):
    return pltpu.CompilerParams(dimension_semantics=sem, vmem_limit_bytes=VMEM_LIMIT)


def _pick(dim, pref):
    t = pref
    while t >= 128:
        if dim % t == 0:
            return t
        t //= 2
    return dim


def _rows(rows, row_bytes, budget):
    t = rows
    while t * row_bytes > budget and t % 16 == 0:
        t //= 2
    return t


def _sigmoid(v):
    return 1.0 / (1.0 + jnp.exp(-v))


_DIMS = {"nn": (((1,), (0,)), ((), ())), "nt": (((1,), (1,)), ((), ())), "tn": (((0,), (0,)), ((), ()))}


def _pick_k(k_unit, fixed_bytes, per_k_bytes):
    for n in range(1, k_unit // 128 + 1):
        if k_unit % n == 0 and (n == 1 or (k_unit // n) % 128 == 0):
            if fixed_bytes + (k_unit // n) * per_k_bytes <= MM_VMEM_BUDGET:
                return k_unit // n
    return min(k_unit, 128)


def _mm(name, form, a, b, M, N, K, *, tm=1024, tn=1024, k_unit=None, a_spec=None, b_spec=None,
        o_shape=None, o_spec=None, out_dtype=F32, pro=None, epi=None, extras=()):
    tm, tn = _pick(M, tm), _pick(N, tn)
    out_dtypes = out_dtype if isinstance(out_dtype, tuple) else (out_dtype,)
    n_out = len(out_dtypes)
    in_place = n_out == 1 and out_dtypes[0] == F32
    fixed = 2 * tm * tn * (sum(jnp.dtype(d).itemsize for d in out_dtypes) + 4 * len(extras))
    fixed += 0 if in_place else 4 * tm * tn
    tk = _pick_k(K if k_unit is None else k_unit, fixed,
                 2 * (tm * a.dtype.itemsize + tn * b.dtype.itemsize))
    nk = K // tk
    if a_spec is None:
        a_spec = (pl.BlockSpec((tk, tm), lambda i, j, k: (k, i)) if form == "tn"
                  else pl.BlockSpec((tm, tk), lambda i, j, k: (i, k)))
    else:
        a_spec = a_spec(tm, tn, tk)
    if b_spec is None:
        b_spec = (pl.BlockSpec((tn, tk), lambda i, j, k: (j, k)) if form == "nt"
                  else pl.BlockSpec((tk, tn), lambda i, j, k: (k, j)))
    else:
        b_spec = b_spec(tm, tn, tk)
    if o_spec is None:
        o_spec = pl.BlockSpec((tm, tn), lambda i, j, k: (i, j))
        o_shape = (M, N)
    else:
        o_spec = o_spec(tm, tn, tk)
    ex_arrays = [e for e in extras]
    ex_specs = [pl.BlockSpec((tm, tn), lambda i, j, k: (i, j)) for _ in extras]
    ne = len(ex_arrays)
    dims = _DIMS[form]
    use_scratch = nk > 1 and not in_place

    def body(a_ref, b_ref, *rest):
        ex_refs = rest[:ne]
        o_refs = rest[ne:ne + n_out]
        at = a_ref[...]
        if pro is not None:
            at = pro(at)
        p = lax.dot_general(at.astype(BF16), b_ref[...].astype(BF16), dims, preferred_element_type=F32)

        def finish(acc):
            res = acc if epi is None else epi(acc, *[r[...] for r in ex_refs])
            res = res if isinstance(res, tuple) else (res,)
            for o_ref, r, d in zip(o_refs, res, out_dtypes):
                o_ref[...] = r.astype(d)

        if nk == 1:
            finish(p)
        else:
            acc_ref = rest[ne + n_out] if use_scratch else o_refs[0]
            k = pl.program_id(2)

            @pl.when(k == 0)
            def _():
                acc_ref[...] = p

            @pl.when(jnp.logical_and(k > 0, k < nk - 1))
            def _():
                acc_ref[...] += p

            @pl.when(k == nk - 1)
            def _():
                finish(acc_ref[...] + p)

    res = pl.pallas_call(
        body, name=name, grid=(M // tm, N // tn, nk),
        in_specs=[a_spec, b_spec] + ex_specs, out_specs=[o_spec] * n_out,
        out_shape=[jax.ShapeDtypeStruct(o_shape, d) for d in out_dtypes],
        scratch_shapes=[pltpu.VMEM((tm, tn), F32)] if use_scratch else [],
        compiler_params=_cp("parallel", "parallel", "arbitrary"),
    )(a, b, *ex_arrays)
    return res if n_out > 1 else res[0]


def _colshard_b(l, n_per):
    def make(tm, tn, tk):
        nb = n_per // tn
        return pl.BlockSpec((None, None, tk, tn), lambda i, j, k: (j // nb, l, k, j % nb))
    return make


def _colshard_bt(l, n_per):
    def make(tm, tn, tk):
        nb = n_per // tk
        return pl.BlockSpec((None, None, tn, tk), lambda i, j, k: (k // nb, l, j, k % nb))
    return make


def _colshard_o(n_per):
    def make(tm, tn, tk):
        nb = n_per // tn
        return pl.BlockSpec((None, tm, tn), lambda i, j, k: (j // nb, i, j % nb))
    return make


def _rowshard_b(l, k_per):
    def make(tm, tn, tk):
        nb = k_per // tk
        return pl.BlockSpec((None, None, tk, tn), lambda i, j, k: (k // nb, l, k % nb, j))
    return make


def _rowshard_bt(l, k_per):
    def make(tm, tn, tk):
        nb = k_per // tn
        return pl.BlockSpec((None, None, tn, tk), lambda i, j, k: (j // nb, l, j % nb, k))
    return make


def _res_ln(name, xprev, h, g, b, scale=None):
    T, D = xprev.shape
    tr = _pick(T, 256)
    row = pl.BlockSpec((tr, D), lambda i: (i, 0))
    vec = pl.BlockSpec((1, D), lambda i: (0, 0))
    has_scale = scale is not None

    def body(*refs):
        if has_scale:
            x_ref, h_ref, s_ref, g_ref, b_ref, y_ref, yb_ref, xh_ref, rs_ref = refs
            hh = h_ref[...] * s_ref[...]
        else:
            x_ref, h_ref, g_ref, b_ref, y_ref, yb_ref, xh_ref, rs_ref = refs
            hh = h_ref[...]
        u = ALPHA * x_ref[...] + hh
        mu = jnp.mean(u, axis=-1, keepdims=True)
        d = u - mu
        var = jnp.mean(d * d, axis=-1, keepdims=True)
        rs = lax.rsqrt(var + LN_EPS)
        xh = d * rs
        y = xh * g_ref[...] + b_ref[...]
        y_ref[...] = y
        yb_ref[...] = y.astype(BF16)
        xh_ref[...] = xh
        rs_ref[...] = rs

    ins = [xprev, h] + ([scale] if has_scale else []) + [g, b]
    specs = [row, row] + ([vec] if has_scale else []) + [vec, vec]
    return pl.pallas_call(
        body, name=name, grid=(T // tr,), in_specs=specs,
        out_specs=[row, row, row, pl.BlockSpec((tr, 1), lambda i: (i, 0))],
        out_shape=[jax.ShapeDtypeStruct((T, D), F32), jax.ShapeDtypeStruct((T, D), BF16),
                   jax.ShapeDtypeStruct((T, D), F32), jax.ShapeDtypeStruct((T, 1), F32)],
        compiler_params=_cp("parallel"),
    )(*ins)


def _accum(ref, part, first):
    @pl.when(first)
    def _():
        ref[...] = part

    @pl.when(jnp.logical_not(first))
    def _():
        ref[...] += part


def _ln_bwd(name, dy, xh, rs, g, hraw=None, scale=None):
    T, D = dy.shape
    tr = _pick(T, 256)
    row = pl.BlockSpec((tr, D), lambda i: (i, 0))
    vec = pl.BlockSpec((1, D), lambda i: (0, 0))
    has_scale = scale is not None

    def body(*refs):
        if has_scale:
            dy_ref, xh_ref, rs_ref, g_ref, hr_ref, s_ref, du_ref, dub_ref, dg_ref, db_ref, ds_ref = refs
        else:
            dy_ref, xh_ref, rs_ref, g_ref, du_ref, dub_ref, dg_ref, db_ref = refs
        first = pl.program_id(0) == 0
        dyv = dy_ref[...]
        xhv = xh_ref[...]
        dxh = dyv * g_ref[...]
        m1 = jnp.mean(dxh, axis=-1, keepdims=True)
        m2 = jnp.mean(dxh * xhv, axis=-1, keepdims=True)
        du = rs_ref[...] * (dxh - m1 - xhv * m2)
        du_ref[...] = du
        if has_scale:
            dub_ref[...] = (du * s_ref[...]).astype(BF16)
            _accum(ds_ref, jnp.sum(du * hr_ref[...], axis=0, keepdims=True), first)
        else:
            dub_ref[...] = du.astype(BF16)
        _accum(dg_ref, jnp.sum(dyv * xhv, axis=0, keepdims=True), first)
        _accum(db_ref, jnp.sum(dyv, axis=0, keepdims=True), first)

    ins = [dy, xh, rs, g] + ([hraw, scale] if has_scale else [])
    specs = [row, row, pl.BlockSpec((tr, 1), lambda i: (i, 0)), vec] + ([row, vec] if has_scale else [])
    n_vec = 3 if has_scale else 2
    return pl.pallas_call(
        body, name=name, grid=(T // tr,), in_specs=specs,
        out_specs=[row, row] + [vec] * n_vec,
        out_shape=[jax.ShapeDtypeStruct((T, D), F32), jax.ShapeDtypeStruct((T, D), BF16)]
        + [jax.ShapeDtypeStruct((1, D), F32)] * n_vec,
        compiler_params=_cp("arbitrary"),
    )(*ins)


def _gate_fwd(name, x, gl, e):
    T, D = x.shape
    tr = _pick(T, 256)
    row = pl.BlockSpec((tr, D), lambda i: (i, 0))

    def body(x_ref, gl_ref, e_ref, xo_ref, xob_ref, gate_ref):
        gate = _sigmoid(gl_ref[...])
        xo = x_ref[...] + gate * e_ref[...]
        xo_ref[...] = xo
        xob_ref[...] = xo.astype(BF16)
        gate_ref[...] = gate

    return pl.pallas_call(
        body, name=name, grid=(T // tr,), in_specs=[row, row, row], out_specs=[row, row, row],
        out_shape=[jax.ShapeDtypeStruct((T, D), F32), jax.ShapeDtypeStruct((T, D), BF16),
                   jax.ShapeDtypeStruct((T, D), F32)],
        compiler_params=_cp("parallel"),
    )(x, gl, e)


def _gate_loss(name, x, gl, e, tgt):
    T, D = x.shape
    tr = _pick(T, 256)
    row = pl.BlockSpec((tr, D), lambda i: (i, 0))

    def body(x_ref, gl_ref, e_ref, t_ref, dy_ref, gate_ref, lp_ref):
        gate = _sigmoid(gl_ref[...])
        err = x_ref[...] + gate * e_ref[...] - t_ref[...]
        dy_ref[...] = err * (1.0 / D)
        gate_ref[...] = gate
        s = jnp.sum(jnp.mean(err * err, axis=-1, keepdims=True), axis=0, keepdims=True)
        lp_ref[...] = jnp.broadcast_to(0.5 * s, (8, 128))

    return pl.pallas_call(
        body, name=name, grid=(T // tr,), in_specs=[row] * 4,
        out_specs=[row, row, pl.BlockSpec((8, 128), lambda i: (i, 0))],
        out_shape=[jax.ShapeDtypeStruct((T, D), F32), jax.ShapeDtypeStruct((T, D), F32),
                   jax.ShapeDtypeStruct((T // tr * 8, 128), F32)],
        compiler_params=_cp("parallel"),
    )(x, gl, e, tgt)


def _gate_bwd(name, dxo, gate, e):
    T, D = dxo.shape
    tr = _pick(T, 256)
    row = pl.BlockSpec((tr, D), lambda i: (i, 0))

    def body(d_ref, gate_ref, e_ref, dgl_ref, de_ref):
        d = d_ref[...]
        gate = gate_ref[...]
        dgl_ref[...] = (d * e_ref[...] * gate * (1.0 - gate)).astype(BF16)
        de_ref[...] = (d * gate).astype(BF16)

    return pl.pallas_call(
        body, name=name, grid=(T // tr,), in_specs=[row] * 3, out_specs=[row, row],
        out_shape=[jax.ShapeDtypeStruct((T, D), BF16)] * 2,
        compiler_params=_cp("parallel"),
    )(dxo, gate, e)


def _gnorm_fwd(name, y, z, w):
    T, DI = y.shape
    tr = _pick(T, 128)
    row = pl.BlockSpec((tr, DI), lambda i: (i, 0))
    vec = pl.BlockSpec((1, DI), lambda i: (0, 0))
    col = pl.BlockSpec((tr, 1), lambda i: (i, 0))

    def body(y_ref, z_ref, w_ref, o_ref, rs_ref):
        zv = z_ref[...]
        yz = y_ref[...] * (zv * _sigmoid(zv))
        rs = lax.rsqrt(jnp.mean(yz * yz, axis=-1, keepdims=True) + RMS_EPS)
        o_ref[...] = (yz * rs * w_ref[...]).astype(BF16)
        rs_ref[...] = rs

    return pl.pallas_call(
        body, name=name, grid=(T // tr,), in_specs=[row, row, vec], out_specs=[row, col],
        out_shape=[jax.ShapeDtypeStruct((T, DI), BF16), jax.ShapeDtypeStruct((T, 1), F32)],
        compiler_params=_cp("parallel"),
    )(y, z, w)


def _gnorm_bwd(name, dyn, y, z, w, rs):
    T, DI = y.shape
    tr = _pick(T, 128)
    row = pl.BlockSpec((tr, DI), lambda i: (i, 0))
    vec = pl.BlockSpec((1, DI), lambda i: (0, 0))
    col = pl.BlockSpec((tr, 1), lambda i: (i, 0))

    def body(d_ref, y_ref, z_ref, w_ref, rs_ref, dy_ref, dz_ref, dw_ref):
        first = pl.program_id(0) == 0
        zv = z_ref[...]
        yv = y_ref[...]
        sg = _sigmoid(zv)
        sz = zv * sg
        rsv = rs_ref[...]
        yzh = yv * sz * rsv
        dv = d_ref[...]
        gw = dv * w_ref[...]
        m = jnp.mean(gw * yzh, axis=-1, keepdims=True)
        dyz = rsv * (gw - yzh * m)
        dy_ref[...] = dyz * sz
        dz_ref[...] = (dyz * yv * (sg * (1.0 + zv * (1.0 - sg)))).astype(BF16)
        _accum(dw_ref, jnp.sum(dv * yzh, axis=0, keepdims=True), first)

    return pl.pallas_call(
        body, name=name, grid=(T // tr,), in_specs=[row, row, row, vec, col], out_specs=[row, row, vec],
        out_shape=[jax.ShapeDtypeStruct((T, DI), F32), jax.ShapeDtypeStruct((T, DI), BF16),
                   jax.ShapeDtypeStruct((1, DI), F32)],
        compiler_params=_cp("arbitrary"),
    )(dyn, y, z, w, rs)


def _sel4(j, vals):
    return jnp.where(j == 0, vals[0], jnp.where(j == 1, vals[1], jnp.where(j == 2, vals[2], vals[3])))


def _pool_cnt(i, j, tr, rows, offset):
    t = i * tr + offset + lax.broadcasted_iota(jnp.int32, (rows, 1), 0)
    win = _sel4(j, POOL_WINDOWS)
    return jnp.minimum(t + 1, win).astype(F32)


def _pool_fwd(name, x):
    T, D = x.shape
    gd = D // len(POOL_WINDOWS)
    tr = _pick(T, 512)
    hb = tr // POOL_HALO

    def body(x_ref, h_ref, o_ref):
        i, j = pl.program_id(0), pl.program_id(1)
        xv = x_ref[...]
        halo = jnp.where(i > 0, h_ref[...], 0.0)
        cat = jnp.concatenate([halo, xv], axis=0)
        s2 = cat + pltpu.roll(cat, 1, 0)
        s4 = s2 + pltpu.roll(s2, 2, 0)
        s8 = s4 + pltpu.roll(s4, 4, 0)
        s16 = s8 + pltpu.roll(s8, 8, 0)
        sel = _sel4(j, (s2, s4, s8, s16))[POOL_HALO:]
        o_ref[...] = (sel / _pool_cnt(i, j, tr, tr, 0) - xv).astype(BF16)

    return pl.pallas_call(
        body, name=name, grid=(T // tr, len(POOL_WINDOWS)),
        in_specs=[pl.BlockSpec((tr, gd), lambda i, j: (i, j)),
                  pl.BlockSpec((POOL_HALO, gd), lambda i, j: (jnp.maximum(i * hb - 1, 0), j))],
        out_specs=pl.BlockSpec((tr, gd), lambda i, j: (i, j)),
        out_shape=jax.ShapeDtypeStruct((T, D), BF16),
        compiler_params=_cp("parallel", "parallel"),
    )(x, x)


def _pool_bwd(name, dp, du):
    T, D = dp.shape
    gd = D // len(POOL_WINDOWS)
    tr = _pick(T, 512)
    hb = tr // POOL_HALO
    last_h = T // POOL_HALO - 1
    n = tr + POOL_HALO

    def body(dp_ref, h_ref, du_ref, o_ref):
        i, j = pl.program_id(0), pl.program_id(1)
        dpv = dp_ref[...]
        q = dpv / _pool_cnt(i, j, tr, tr, 0)
        qh = jnp.where(i < pl.num_programs(0) - 1, h_ref[...] / _pool_cnt(i, j, tr, POOL_HALO, tr), 0.0)
        cat = jnp.concatenate([q, qh], axis=0)
        f2 = cat + pltpu.roll(cat, n - 1, 0)
        f4 = f2 + pltpu.roll(f2, n - 2, 0)
        f8 = f4 + pltpu.roll(f4, n - 4, 0)
        f16 = f8 + pltpu.roll(f8, n - 8, 0)
        sel = _sel4(j, (f2, f4, f8, f16))[:tr]
        o_ref[...] = ALPHA * du_ref[...] + sel - dpv

    return pl.pallas_call(
        body, name=name, grid=(T // tr, len(POOL_WINDOWS)),
        in_specs=[pl.BlockSpec((tr, gd), lambda i, j: (i, j)),
                  pl.BlockSpec((POOL_HALO, gd), lambda i, j: (jnp.minimum((i + 1) * hb, last_h), j)),
                  pl.BlockSpec((tr, gd), lambda i, j: (i, j))],
        out_specs=pl.BlockSpec((tr, gd), lambda i, j: (i, j)),
        out_shape=jax.ShapeDtypeStruct((T, D), F32),
        compiler_params=_cp("parallel", "parallel"),
    )(dp, dp, du)


def _conv_taps(cat, wv, rows):
    shifted = [cat[CONV_HALO:] if s == 0 else pltpu.roll(cat, s, 0)[CONV_HALO:] for s in range(CONV_K)]
    acc = shifted[0] * wv[CONV_K - 1:CONV_K]
    for s in range(1, CONV_K):
        acc = acc + shifted[s] * wv[CONV_K - 1 - s:CONV_K - s]
    return acc, shifted


def _conv_fwd(name, xp, w, b):
    T, CD = xp.shape
    tr, tc = _pick(T, 512), _pick(CD, 512)
    hb = tr // CONV_HALO

    def body(x_ref, h_ref, w_ref, b_ref, o_ref):
        i = pl.program_id(0)
        halo = jnp.where(i > 0, h_ref[...], 0.0)
        cat = jnp.concatenate([halo, x_ref[...]], axis=0)
        acc, _ = _conv_taps(cat, w_ref[...], tr)
        acc = acc + b_ref[...]
        o_ref[...] = acc * _sigmoid(acc)

    return pl.pallas_call(
        body, name=name, grid=(T // tr, CD // tc),
        in_specs=[pl.BlockSpec((tr, tc), lambda i, j: (i, j)),
                  pl.BlockSpec((CONV_HALO, tc), lambda i, j: (jnp.maximum(i * hb - 1, 0), j)),
                  pl.BlockSpec((CONV_K, tc), lambda i, j: (0, j)),
                  pl.BlockSpec((1, tc), lambda i, j: (0, j))],
        out_specs=pl.BlockSpec((tr, tc), lambda i, j: (i, j)),
        out_shape=jax.ShapeDtypeStruct((T, CD), F32),
        compiler_params=_cp("parallel", "parallel"),
    )(xp, xp, w, b)


def _conv_bwd_a(name, dxs, dbm, dcm, xp, w, b):
    T, CD = xp.shape
    tr = _pick(T, 512)
    tc = _pick(dbm.shape[1], 512)
    hb = tr // CONV_HALO
    nx, nb = dxs.shape[1] // tc, dbm.shape[1] // tc

    def part_spec(lo, n):
        def imap(j, i):
            inside = jnp.logical_and(j >= lo, j < lo + n)
            return (jnp.where(inside, i, 0), jnp.clip(j - lo, 0, n - 1))
        return pl.BlockSpec((tr, tc), imap)

    def body(dx_ref, db_ref, dc_ref, x_ref, h_ref, w_ref, b_ref, o_ref, dw_ref, dbias_ref):
        j, i = pl.program_id(0), pl.program_id(1)
        first = i == 0
        d = jnp.where(j < nx, dx_ref[...], jnp.where(j < nx + nb, db_ref[...], dc_ref[...]))
        halo = jnp.where(i > 0, h_ref[...], 0.0)
        cat = jnp.concatenate([halo, x_ref[...]], axis=0)
        acc, shifted = _conv_taps(cat, w_ref[...], tr)
        acc = acc + b_ref[...]
        sg = _sigmoid(acc)
        dconv = d * (sg * (1.0 + acc * (1.0 - sg)))
        o_ref[...] = dconv
        _accum(dbias_ref, jnp.sum(dconv, axis=0, keepdims=True), first)
        tap = lax.broadcasted_iota(jnp.int32, (CONV_K, tc), 0)
        dwv = jnp.zeros((CONV_K, tc), F32)
        for s in range(CONV_K):
            dwv = jnp.where(tap == CONV_K - 1 - s, jnp.sum(dconv * shifted[s], axis=0, keepdims=True), dwv)
        _accum(dw_ref, dwv, first)

    return pl.pallas_call(
        body, name=name, grid=(CD // tc, T // tr),
        in_specs=[part_spec(0, nx), part_spec(nx, nb), part_spec(nx + nb, nb),
                  pl.BlockSpec((tr, tc), lambda j, i: (i, j)),
                  pl.BlockSpec((CONV_HALO, tc), lambda j, i: (jnp.maximum(i * hb - 1, 0), j)),
                  pl.BlockSpec((CONV_K, tc), lambda j, i: (0, j)),
                  pl.BlockSpec((1, tc), lambda j, i: (0, j))],
        out_specs=[pl.BlockSpec((tr, tc), lambda j, i: (i, j)),
                   pl.BlockSpec((CONV_K, tc), lambda j, i: (0, j)),
                   pl.BlockSpec((1, tc), lambda j, i: (0, j))],
        out_shape=[jax.ShapeDtypeStruct((T, CD), F32), jax.ShapeDtypeStruct((CONV_K, CD), F32),
                   jax.ShapeDtypeStruct((1, CD), F32)],
        compiler_params=_cp("parallel", "arbitrary"),
    )(dxs, dbm, dcm, xp, xp, w, b)


def _conv_bwd_b(name, dconv, w):
    T, CD = dconv.shape
    tr, tc = _pick(T, 512), _pick(CD, 512)
    hb = tr // CONV_HALO
    last_h = T // CONV_HALO - 1
    n = tr + CONV_HALO

    def body(d_ref, h_ref, w_ref, o_ref):
        i = pl.program_id(0)
        halo = jnp.where(i < pl.num_programs(0) - 1, h_ref[...], 0.0)
        cat = jnp.concatenate([d_ref[...], halo], axis=0)
        wv = w_ref[...]
        acc = cat[:tr] * wv[CONV_K - 1:CONV_K]
        for s in range(1, CONV_K):
            acc = acc + pltpu.roll(cat, n - s, 0)[:tr] * wv[CONV_K - 1 - s:CONV_K - s]
        o_ref[...] = acc.astype(BF16)

    return pl.pallas_call(
        body, name=name, grid=(T // tr, CD // tc),
        in_specs=[pl.BlockSpec((tr, tc), lambda i, j: (i, j)),
                  pl.BlockSpec((CONV_HALO, tc), lambda i, j: (jnp.minimum((i + 1) * hb, last_h), j)),
                  pl.BlockSpec((CONV_K, tc), lambda i, j: (0, j))],
        out_specs=pl.BlockSpec((tr, tc), lambda i, j: (i, j)),
        out_shape=jax.ShapeDtypeStruct((T, CD), BF16),
        compiler_params=_cp("parallel", "parallel"),
    )(dconv, dconv, w)


def _dt_fwd(name, dtp, bias):
    T, H = dtp.shape

    def body(x_ref, b_ref, o_ref):
        v = x_ref[...] + b_ref[...]
        u = jnp.exp(-jnp.abs(v))
        w1 = 1.0 + u
        lp = jnp.where(w1 == 1.0, u, jnp.log(w1) * (u / jnp.where(w1 == 1.0, 1.0, w1 - 1.0)))
        o_ref[...] = jnp.maximum(v, 0.0) + lp

    return pl.pallas_call(body, name=name, out_shape=jax.ShapeDtypeStruct((T, H), F32))(dtp, bias)


def _dt_bwd(name, ddt, dtp, bias):
    T, H = dtp.shape

    def body(d_ref, x_ref, b_ref, o_ref, ob_ref, db_ref):
        g = d_ref[...] * _sigmoid(x_ref[...] + b_ref[...])
        o_ref[...] = g
        ob_ref[...] = g.astype(BF16)
        db_ref[...] = jnp.sum(g, axis=0, keepdims=True)

    return pl.pallas_call(
        body, name=name,
        out_shape=[jax.ShapeDtypeStruct((T, H), F32), jax.ShapeDtypeStruct((T, H), BF16),
                   jax.ShapeDtypeStruct((1, H), F32)],
    )(ddt, dtp, bias)


def _dotf(a, b):
    return jnp.dot(a, b, precision=HI, preferred_element_type=F32)


def _dotb(form, a, b):
    return lax.dot_general(a.astype(BF16), b.astype(BF16), _DIMS[form], preferred_element_type=F32)


def _ssd_common(dtc, dtr, alr, alc, gw):
    li = lax.broadcasted_iota(jnp.int32, (CHUNK, CHUNK), 0)
    si = lax.broadcasted_iota(jnp.int32, (CHUNK, CHUNK), 1)
    tri = (li >= si).astype(F32)
    trit = (li <= si).astype(F32)
    a_row = -jnp.exp(alr)
    a_col = -jnp.exp(alc)
    acs_c = _dotf(tri, dtc * a_row)
    acs_r = _dotf(dtr * a_col, trit)
    eh = lax.broadcasted_iota(jnp.int32, (HEAD_PAD, gw), 0)
    ec = lax.broadcasted_iota(jnp.int32, (HEAD_PAD, gw), 1) // HEAD_DIM
    expand = (eh == ec).astype(F32)
    th = lax.broadcasted_iota(jnp.int32, (gw, HEAD_PAD), 1)
    tc = lax.broadcasted_iota(jnp.int32, (gw, HEAD_PAD), 0) // HEAD_DIM
    reduce_ = (th == tc).astype(F32)
    acs_last = acs_c[CHUNK - 1:CHUNK, :]
    return dict(li=li, si=si, tri=tri, trit=trit, a_row=a_row, acs_c=acs_c, acs_r=acs_r,
                expand=expand, reduce=reduce_, acs_last=acs_last,
                dt_e=_dotf(dtc, expand), eacs_e=jnp.exp(_dotf(acs_c, expand)),
                dec_h=jnp.exp(acs_last - acs_c))


def _ssd_specs(T, DI, gw, nc, rev):
    nsb = DI // D_STATE
    cidx = (lambda c: nc - 1 - c) if rev else (lambda c: c)
    return dict(
        xs=pl.BlockSpec((CHUNK, gw), lambda g, c: (cidx(c), g)),
        bm=pl.BlockSpec((CHUNK, D_STATE), lambda g, c: (cidx(c), nsb + g)),
        cm=pl.BlockSpec((CHUNK, D_STATE), lambda g, c: (cidx(c), nsb + N_GROUPS + g)),
        dtc=pl.BlockSpec((None, CHUNK, HEAD_PAD), lambda g, c: (g, cidx(c), 0)),
        dtr=pl.BlockSpec((None, ROW_PAD, CHUNK), lambda g, c: (g, 0, cidx(c))),
        alr=pl.BlockSpec((None, 1, HEAD_PAD), lambda g, c: (g, 0, 0)),
        alc=pl.BlockSpec((None, ROW_PAD, 1), lambda g, c: (g, 0, 0)),
        de=pl.BlockSpec((None, 1, gw), lambda g, c: (g, 0, 0)),
        hp=pl.BlockSpec((None, None, gw, D_STATE), lambda g, c: (cidx(c), g, 0, 0)),
        bc=pl.BlockSpec((CHUNK, D_STATE), lambda g, c: (cidx(c), g)),
        acc=pl.BlockSpec((None, 1, HEAD_PAD), lambda g, c: (g, 0, 0)),
    )


def _ssd_fwd(name, xbc, dtc, dtr, alr, alc, d_e, DI):
    T = xbc.shape[0]
    nc = T // CHUNK
    gw = DI // N_GROUPS
    hpg = gw // HEAD_DIM
    sp = _ssd_specs(T, DI, gw, nc, False)

    def body(xs_ref, b_ref, c_ref, dtc_ref, dtr_ref, alr_ref, alc_ref, de_ref, y_ref, hp_ref, h_scr):
        @pl.when(pl.program_id(1) == 0)
        def _():
            h_scr[...] = jnp.zeros_like(h_scr)

        hpv = h_scr[...]
        hp_ref[...] = hpv
        xs = xs_ref[...]
        bb = b_ref[...].astype(BF16)
        cb_ = c_ref[...].astype(BF16)
        cm = _ssd_common(dtc_ref[...], dtr_ref[...], alr_ref[...], alc_ref[...], gw)
        x = xs * cm["dt_e"]
        xb = x.astype(BF16)
        cbm = _dotb("nt", cb_, bb)
        causal = cm["li"] >= cm["si"]
        lane_head = lax.broadcasted_iota(jnp.int32, (1, gw), 1) // HEAD_DIM
        ydiag = jnp.zeros((CHUNK, gw), F32)
        for j in range(hpg):
            seg = cm["acs_c"][:, j:j + 1] - cm["acs_r"][j:j + 1, :]
            lmat = jnp.exp(jnp.where(causal, seg, -1e30))
            yj = _dotb("nn", cbm * lmat, xb)
            ydiag = jnp.where(lane_head == j, yj, ydiag)
        dec_e = _dotf(cm["dec_h"], cm["expand"])
        states = _dotb("tn", x * dec_e, bb)
        yoff = _dotb("nt", cb_, hpv) * cm["eacs_e"]
        y_ref[...] = ydiag + yoff + xs * de_ref[...]
        cd_rows = jnp.sum(cm["reduce"] * jnp.exp(cm["acs_last"]), axis=1, keepdims=True)
        h_scr[...] = cd_rows * hpv + states

    return pl.pallas_call(
        body, name=name, grid=(N_GROUPS, nc),
        in_specs=[sp["xs"], sp["bm"], sp["cm"], sp["dtc"], sp["dtr"], sp["alr"], sp["alc"], sp["de"]],
        out_specs=[sp["xs"], sp["hp"]],
        out_shape=[jax.ShapeDtypeStruct((T, DI), F32), jax.ShapeDtypeStruct((nc, N_GROUPS, gw, D_STATE), F32)],
        scratch_shapes=[pltpu.VMEM((gw, D_STATE), F32)],
        compiler_params=_cp("parallel", "arbitrary"),
    )(xbc, xbc, xbc, dtc, dtr, alr, alc, d_e)


def _ssd_bwd(name, dy, xbc, dtc, dtr, alr, alc, d_e, hprev, DI):
    T = xbc.shape[0]
    nc = T // CHUNK
    gw = DI // N_GROUPS
    hpg = gw // HEAD_DIM
    sp = _ssd_specs(T, DI, gw, nc, True)

    def body(dy_ref, xs_ref, b_ref, c_ref, dtc_ref, dtr_ref, alr_ref, alc_ref, de_ref, hp_ref,
             dxs_ref, db_ref, dc_ref, ddt_ref, dal_ref, dd_ref, dh_scr):
        first = pl.program_id(1) == 0

        @pl.when(first)
        def _():
            dh_scr[...] = jnp.zeros_like(dh_scr)

        xs = xs_ref[...]
        dyv = dy_ref[...]
        bb = b_ref[...].astype(BF16)
        cb_ = c_ref[...].astype(BF16)
        dtc_v = dtc_ref[...]
        cm = _ssd_common(dtc_v, dtr_ref[...], alr_ref[...], alc_ref[...], gw)
        hpv = hp_ref[...]
        hpb = hpv.astype(BF16)
        dhn = dh_scr[...]
        dhnb = dhn.astype(BF16)
        x = xs * cm["dt_e"]
        xb = x.astype(BF16)
        cbm = _dotb("nt", cb_, bb)
        causal = cm["li"] >= cm["si"]
        lane_head = lax.broadcasted_iota(jnp.int32, (1, gw), 1) // HEAD_DIM
        lane_pad = lax.broadcasted_iota(jnp.int32, (1, HEAD_PAD), 1)
        sub_pad = lax.broadcasted_iota(jnp.int32, (ROW_PAD, 1), 0)

        dxs = dyv * de_ref[...]
        dd_part = jnp.sum(_dotf(dyv * xs, cm["reduce"]), axis=0, keepdims=True)
        _accum(dd_ref, dd_part, first)

        dx = jnp.zeros((CHUNK, gw), F32)
        dcb = jnp.zeros((CHUNK, CHUNK), F32)
        dacs_c = jnp.zeros((CHUNK, HEAD_PAD), F32)
        dacs_r = jnp.zeros((ROW_PAD, CHUNK), F32)
        for j in range(hpg):
            seg = cm["acs_c"][:, j:j + 1] - cm["acs_r"][j:j + 1, :]
            lmat = jnp.exp(jnp.where(causal, seg, -1e30))
            mmat = cbm * lmat
            dym = jnp.where(lane_head == j, dyv, 0.0).astype(BF16)
            dm = _dotb("nt", dym, xb)
            dx = dx + _dotb("tn", mmat, dym)
            dcb = dcb + dm * lmat
            dseg = dm * mmat
            dacs_c = dacs_c + jnp.where(lane_pad == j, jnp.sum(dseg, axis=1, keepdims=True), 0.0)
            dacs_r = dacs_r - jnp.where(sub_pad == j, jnp.sum(dseg, axis=0, keepdims=True), 0.0)
        dc = _dotb("nn", dcb, bb)
        db = _dotb("tn", dcb, cb_)

        gm = _dotb("nt", cb_, hpb)
        dgm = dyv * cm["eacs_e"]
        dacs_c = dacs_c + _dotf(dgm * gm, cm["reduce"])
        dc = dc + _dotb("nn", dgm, hpb)
        dhp = _dotb("tn", dgm, cb_)

        cd_row = jnp.exp(cm["acs_last"])
        cd_rows = jnp.sum(cm["reduce"] * cd_row, axis=1, keepdims=True)
        dhp = dhp + cd_rows * dhn
        rsum = jnp.sum(dhn * hpv, axis=1, keepdims=True)
        dacs_last = jnp.sum(cm["reduce"] * rsum, axis=0, keepdims=True) * cd_row
        dec_e = _dotf(cm["dec_h"], cm["expand"])
        xdec = x * dec_e
        dxdec = _dotb("nt", bb, dhnb)
        db = db + _dotb("nn", xdec, dhnb)
        dx = dx + dxdec * dec_e
        tdec = _dotf(dxdec * x, cm["reduce"]) * cm["dec_h"]
        dacs_c = dacs_c - tdec
        dacs_last = dacs_last + jnp.sum(tdec, axis=0, keepdims=True)
        row_id = lax.broadcasted_iota(jnp.int32, (CHUNK, 1), 0)
        dacs_c = dacs_c + jnp.where(row_id == CHUNK - 1, dacs_last, 0.0)

        dxs_ref[...] = dxs + dx * cm["dt_e"]
        ddt = _dotf(dx * xs, cm["reduce"])
        dda = _dotf(cm["trit"], dacs_c)
        dda_r = _dotf(dacs_r, cm["tri"])
        dda_rp = jnp.concatenate([dda_r, jnp.zeros((HEAD_PAD - ROW_PAD, CHUNK), F32)], axis=0)
        eye = (cm["li"] == cm["si"]).astype(F32)
        dda = dda + lax.dot_general(eye, dda_rp, _DIMS["nt"], precision=HI, preferred_element_type=F32)
        ddt_ref[...] = ddt + dda * cm["a_row"]
        _accum(dal_ref, jnp.sum(dda * dtc_v, axis=0, keepdims=True) * cm["a_row"], first)
        db_ref[...] = db
        dc_ref[...] = dc
        dh_scr[...] = dhp

    gs = D_STATE * N_GROUPS
    return pl.pallas_call(
        body, name=name, grid=(N_GROUPS, nc),
        in_specs=[sp["xs"], sp["xs"], sp["bm"], sp["cm"], sp["dtc"], sp["dtr"], sp["alr"], sp["alc"],
                  sp["de"], sp["hp"]],
        out_specs=[sp["xs"], sp["bc"], sp["bc"], sp["dtc"], sp["acc"], sp["acc"]],
        out_shape=[jax.ShapeDtypeStruct((T, DI), F32), jax.ShapeDtypeStruct((T, gs), F32),
                   jax.ShapeDtypeStruct((T, gs), F32), jax.ShapeDtypeStruct((N_GROUPS, T, HEAD_PAD), F32),
                   jax.ShapeDtypeStruct((N_GROUPS, 1, HEAD_PAD), F32),
                   jax.ShapeDtypeStruct((N_GROUPS, 1, HEAD_PAD), F32)],
        scratch_shapes=[pltpu.VMEM((gw, D_STATE), F32)],
        compiler_params=_cp("parallel", "arbitrary"),
    )(dy, xbc, xbc, xbc, dtc, dtr, alr, alc, d_e, hprev)


def _adam_math(w, g, m, v):
    m = ADAM_B1 * m + (1.0 - ADAM_B1) * g
    v = ADAM_B2 * v + (1.0 - ADAM_B2) * (g * g)
    m_hat = m / (1.0 - ADAM_B1 ** ADAM_STEP)
    v_hat = v / (1.0 - ADAM_B2 ** ADAM_STEP)
    delta = -ADAM_LR * (m_hat / (jnp.sqrt(v_hat) + ADAM_EPS) + ADAM_WD * w)
    return delta, m, v


def _adamw(name, w, m, v, grads):
    L, R, C = w.shape
    tr = _rows(R, C * 4, ADAM_BLOCK_BYTES)
    full = pl.BlockSpec((None, tr, C), lambda l, i: (l, i, 0))
    g_specs = [pl.BlockSpec((tr, C), functools.partial(lambda l, i, ll: (jnp.where(l == ll, i, 0), 0), ll=ll))
               for ll in range(L)]

    def body(w_ref, m_ref, v_ref, *rest):
        g_refs = rest[:L]
        go_ref, d_ref, mo_ref, vo_ref = rest[L:]
        l = pl.program_id(0)
        g = g_refs[0][...]
        for ll in range(1, L):
            g = jnp.where(l == ll, g_refs[ll][...], g)
        delta, mn, vn = _adam_math(w_ref[...], g, m_ref[...], v_ref[...])
        go_ref[...] = g
        d_ref[...] = delta
        mo_ref[...] = mn
        vo_ref[...] = vn

    return pl.pallas_call(
        body, name=name, grid=(L, R // tr), in_specs=[full] * 3 + g_specs, out_specs=[full] * 4,
        out_shape=[jax.ShapeDtypeStruct((L, R, C), F32)] * 4,
        compiler_params=_cp("arbitrary", "arbitrary"),
    )(w, m, v, *grads)


def _sum_peers(name, gathered):
    n, R, C = gathered.shape

    def body(g_ref, o_ref):
        acc = g_ref[0]
        for d in range(1, n):
            acc = acc + g_ref[d]
        o_ref[...] = acc

    return pl.pallas_call(body, name=name, out_shape=jax.ShapeDtypeStruct((R, C), F32))(gathered)


_ANY = pl.BlockSpec(memory_space=pl.ANY)


def _place():
    x, y, c = lax.axis_index("x"), lax.axis_index("y"), lax.axis_index("c")
    chips = [(1 - x, y), (x, 1 - y), (1 - x, 1 - y)]
    return x, y, c, chips


def _allgather_small(name, v):
    R, C = v.shape

    def body(v_ref, o_ref, send_sems, recv_sems):
        x, y, c, _ = _place()
        me = 4 * x + 2 * y + c
        o_ref[me] = v_ref[...]
        copies = []
        for k in range(1, N_DEV):
            px, py, pc = x ^ (k >> 2), y ^ ((k >> 1) & 1), c ^ (k & 1)
            copies.append(pltpu.make_async_remote_copy(
                src_ref=v_ref, dst_ref=o_ref.at[me], send_sem=send_sems.at[k - 1], recv_sem=recv_sems.at[k - 1],
                device_id=(px, py, pc), device_id_type=MESH))
        for cp in copies:
            cp.start()
        for cp in copies:
            cp.wait()

    return pl.pallas_call(
        body, name=name, out_shape=jax.ShapeDtypeStruct((N_DEV, R, C), F32),
        in_specs=[pl.BlockSpec(memory_space=pltpu.VMEM)], out_specs=pl.BlockSpec(memory_space=pltpu.VMEM),
        scratch_shapes=[pltpu.SemaphoreType.DMA((N_DEV - 1,)), pltpu.SemaphoreType.DMA((N_DEV - 1,))],
    )(v)


def _gather_weights(name, shards):
    n = len(shards)

    def body(*refs):
        src = refs[:n]
        dst = refs[n:2 * n]
        ici_s, ici_r, d2d_s, d2d_r, own_s, own_r = refs[2 * n:]
        x, y, c, chips = _place()
        a = 2 * x + y
        own = [pltpu.make_async_remote_copy(
            src_ref=src[i], dst_ref=dst[i].at[a], send_sem=own_s.at[i], recv_sem=own_r.at[i],
            device_id=(x, y, 1 - c), device_id_type=MESH) for i in range(n)]
        for cp in own:
            cp.start()
        first = []
        for i in range(n):
            for k, (px, py) in enumerate(chips):
                first.append(pltpu.make_async_remote_copy(
                    src_ref=src[i].at[c], dst_ref=dst[i].at[a, c], send_sem=ici_s.at[i, k], recv_sem=ici_r.at[i, k],
                    device_id=(px, py, c), device_id_type=MESH))
        for cp in first:
            cp.start()
        passed = []
        for i in range(n):
            for k, (px, py) in enumerate(chips):
                b = 2 * px + py
                pltpu.make_async_remote_copy(
                    src_ref=src[i].at[c], dst_ref=dst[i].at[b, c], send_sem=ici_s.at[i, k], recv_sem=ici_r.at[i, k],
                    device_id=(px, py, c), device_id_type=MESH).wait_recv()
                fwd = pltpu.make_async_remote_copy(
                    src_ref=dst[i].at[b, c], dst_ref=dst[i].at[b, c], send_sem=d2d_s.at[i, k], recv_sem=d2d_r.at[i, k],
                    device_id=(x, y, 1 - c), device_id_type=MESH)
                fwd.start()
                passed.append(fwd)
        for i in range(n):
            for k, (px, py) in enumerate(chips):
                b = 2 * px + py
                pltpu.make_async_remote_copy(
                    src_ref=dst[i].at[b, 1 - c], dst_ref=dst[i].at[b, 1 - c], send_sem=d2d_s.at[i, k],
                    recv_sem=d2d_r.at[i, k], device_id=(x, y, 1 - c), device_id_type=MESH).wait_recv()
        for cp in first + passed:
            cp.wait_send()
        for cp in own:
            cp.wait()

    sems = [pltpu.SemaphoreType.DMA((n, 3))] * 4 + [pltpu.SemaphoreType.DMA((n,))] * 2
    return pl.pallas_call(
        body, name=name, in_specs=[_ANY] * n, out_specs=[_ANY] * n,
        out_shape=[jax.ShapeDtypeStruct((N_CHIPS,) + s.shape, s.dtype) for s in shards],
        scratch_shapes=sems,
    )(*shards)


def _pair_exchange(name, parts):
    n = len(parts)

    def body(*refs):
        src = refs[:n]
        dst = refs[n:2 * n]
        send_sems, recv_sems = refs[2 * n:]
        x, y, c, _ = _place()
        copies = [pltpu.make_async_remote_copy(
            src_ref=src[i].at[:, 1 - c], dst_ref=dst[i], send_sem=send_sems.at[i], recv_sem=recv_sems.at[i],
            device_id=(x, y, 1 - c), device_id_type=MESH) for i in range(n)]
        for cp in copies:
            cp.start()
        for cp in copies:
            cp.wait()

    return pl.pallas_call(
        body, name=name, in_specs=[_ANY] * n, out_specs=[_ANY] * n,
        out_shape=[jax.ShapeDtypeStruct((p.shape[0],) + p.shape[2:], p.dtype) for p in parts],
        scratch_shapes=[pltpu.SemaphoreType.DMA((n,)), pltpu.SemaphoreType.DMA((n,))],
    )(*parts)


def _chip_exchange(name, parts):
    n = len(parts)

    def body(*refs):
        src = refs[:n]
        dst = refs[n:2 * n]
        send_sems, recv_sems = refs[2 * n:]
        x, y, c, chips = _place()
        copies = []
        for i in range(n):
            for k, (px, py) in enumerate(chips):
                copies.append(pltpu.make_async_remote_copy(
                    src_ref=src[i].at[2 * px + py], dst_ref=dst[i].at[k], send_sem=send_sems.at[i, k],
                    recv_sem=recv_sems.at[i, k], device_id=(px, py, c), device_id_type=MESH))
        for cp in copies:
            cp.start()
        for cp in copies:
            cp.wait()

    return pl.pallas_call(
        body, name=name, in_specs=[_ANY] * n, out_specs=[_ANY] * n,
        out_shape=[jax.ShapeDtypeStruct((3,) + p.shape[1:], p.dtype) for p in parts],
        scratch_shapes=[pltpu.SemaphoreType.DMA((n, 3)), pltpu.SemaphoreType.DMA((n, 3))],
    )(*parts)


def _half_exchange(name, shards):
    n = len(shards)

    def body(*refs):
        buf = refs[n:2 * n]
        send_sems, recv_sems = refs[2 * n:]
        x, y, c, _ = _place()
        copies = [pltpu.make_async_remote_copy(
            src_ref=buf[i].at[c], dst_ref=buf[i].at[c], send_sem=send_sems.at[i], recv_sem=recv_sems.at[i],
            device_id=(x, y, 1 - c), device_id_type=MESH) for i in range(n)]
        for cp in copies:
            cp.start()
        for cp in copies:
            cp.wait()

    return pl.pallas_call(
        body, name=name, in_specs=[_ANY] * n, out_specs=[_ANY] * n,
        out_shape=[jax.ShapeDtypeStruct(s.shape, s.dtype) for s in shards],
        input_output_aliases={i: i for i in range(n)},
        scratch_shapes=[pltpu.SemaphoreType.DMA((n,)), pltpu.SemaphoreType.DMA((n,))],
    )(*shards)


def _pair_add(name, pos, part, sib):
    Q, _, R2, C = part.shape
    tr = _pick(R2, 256)

    def body(pos_ref, p_ref, s_ref, o_ref):
        o_ref[...] = (p_ref[...].astype(F32) + s_ref[...].astype(F32)).astype(BF16)

    return pl.pallas_call(
        body, name=name,
        grid_spec=pltpu.PrefetchScalarGridSpec(
            num_scalar_prefetch=1, grid=(Q, R2 // tr),
            in_specs=[pl.BlockSpec((None, None, tr, C), lambda q, i, pos_ref: (q, pos_ref[1], i, 0)),
                      pl.BlockSpec((None, tr, C), lambda q, i, pos_ref: (q, i, 0))],
            out_specs=pl.BlockSpec((None, tr, C), lambda q, i, pos_ref: (q, i, 0))),
        out_shape=jax.ShapeDtypeStruct((Q, R2, C), BF16),
        compiler_params=_cp("parallel", "parallel"),
    )(pos, part, sib)


def _chip_sum(name, pos, own, got):
    _, R2, C = own.shape
    tr = _pick(R2, 256)

    def body(pos_ref, o_ref, g_ref, out_ref):
        acc = o_ref[...].astype(F32)
        for k in range(3):
            acc = acc + g_ref[k].astype(F32)
        out_ref[...] = acc

    return pl.pallas_call(
        body, name=name,
        grid_spec=pltpu.PrefetchScalarGridSpec(
            num_scalar_prefetch=1, grid=(R2 // tr,),
            in_specs=[pl.BlockSpec((None, tr, C), lambda i, pos_ref: (pos_ref[0], i, 0)),
                      pl.BlockSpec((3, tr, C), lambda i, pos_ref: (0, i, 0))],
            out_specs=pl.BlockSpec((None, tr, C), lambda i, pos_ref: (pos_ref[1], i, 0))),
        out_shape=jax.ShapeDtypeStruct((2, R2, C), F32),
        compiler_params=_cp("parallel"),
    )(pos, own, got)


def _reduce_scatter(tag, pos, parts):
    split = [p.reshape(p.shape[0], 2, p.shape[1] // 2, p.shape[2]) for p in parts]
    sib = _pair_exchange(tag + "_pair", split)
    chip = [_pair_add(f"{tag}_add{i}", pos, split[i], sib[i]) for i in range(len(parts))]
    got = _chip_exchange(tag + "_chip", chip)
    halves = [_chip_sum(f"{tag}_sum{i}", pos, chip[i], got[i]) for i in range(len(parts))]
    full = _half_exchange(tag + "_half", halves)
    return [f.reshape(p.shape[1], p.shape[2]) for f, p in zip(full, parts)]


def _pad_to(v, mult):
    n = v.shape[0]
    return jnp.pad(v, (0, (-n) % mult))


def kernel(x, p, pool_w, pool_scale, ssm_in_w, ssm_conv_w, ssm_conv_b, ssm_dt_bias, ssm_a_log, ssm_d, ssm_norm_w, ssm_out_w, mlp_w1, mlp_w2, ln_g, ln_b, ple_w, ple_gate_w, loss_target, m_pool_w, m_pool_scale, m_ssm_in_w, m_ssm_conv_w, m_ssm_conv_b, m_ssm_dt_bias, m_ssm_a_log, m_ssm_d, m_ssm_norm_w, m_ssm_out_w, m_mlp_w1, m_mlp_w2, m_ln_g, m_ln_b, m_ple_w, m_ple_gate_w, v_pool_w, v_pool_scale, v_ssm_in_w, v_ssm_conv_w, v_ssm_conv_b, v_ssm_dt_bias, v_ssm_a_log, v_ssm_d, v_ssm_norm_w, v_ssm_out_w, v_mlp_w1, v_mlp_w2, v_ln_g, v_ln_b, v_ple_w, v_ple_gate_w):
    T, D = x.shape[1], x.shape[2]
    NG = len(POOL_WINDOWS)
    GD = D // NG
    DI = ssm_out_w.shape[1] * N_CHIPS
    H = ssm_dt_bias.shape[1]
    HPG = H // N_GROUPS
    GW = DI // N_GROUPS
    GS = N_GROUPS * D_STATE
    CD = DI + 2 * GS
    DF = mlp_w1.shape[2] * N_CHIPS
    PD = ple_w.shape[1]
    NIN = ssm_in_w.shape[2]

    xi, yi, ci = lax.axis_index("x"), lax.axis_index("y"), lax.axis_index("c")
    chip = 2 * xi + yi
    pos = jnp.stack([chip, ci]).astype(jnp.int32)

    x0 = x[0]
    tgt = loss_target[0]
    p0b, p1b = p[0, 0].astype(BF16), p[1, 0].astype(BF16)

    def halves(w):
        return w.astype(BF16).reshape((2, w.shape[0] // 2) + w.shape[1:])

    sh_pool = pool_w[0].astype(BF16)
    sh_in = halves(ssm_in_w[0])
    sh_out = halves(ssm_out_w[0])
    sh_w1 = mlp_w1.astype(BF16)
    sh_w2 = mlp_w2.astype(BF16)
    sh_pw = ple_w.astype(BF16)
    sh_gw = ple_gate_w.astype(BF16)
    sh_pool = sh_pool.reshape((2, NG // 2) + sh_pool.shape[1:])
    g_pool, g_in, g_out, g_w1, g_w2, g_pw, g_gw = _gather_weights(
        "gather_weights", [sh_pool, sh_in, sh_out, sh_w1, sh_w2, sh_pw, sh_gw])
    w_pool = jnp.transpose(g_pool.reshape(N_CHIPS, NG, GD // N_CHIPS, GD), (1, 0, 2, 3)).reshape(NG, GD, GD)
    w_in = jnp.concatenate([g_in[q].reshape(D, NIN) for q in range(N_CHIPS)], axis=1)
    w_z, w_xbc, w_dt = w_in[:, :DI], w_in[:, DI:DI + CD], w_in[:, DI + CD:]
    w_out = g_out.reshape(DI, D)

    small_sh = jnp.concatenate([ssm_conv_w[0].reshape(-1), ssm_conv_b[0], ssm_norm_w[0],
                                ln_g.reshape(-1), ln_b.reshape(-1)])
    n_sh = small_sh.shape[0]
    small_all = _allgather_small("gather_small", _pad_to(small_sh, 1024).reshape(-1, 128))
    small_all = small_all.reshape(N_DEV, -1)[0::2, :n_sh]
    cdq, niq, dq = CD // N_CHIPS, DI // N_CHIPS, D // N_CHIPS
    o = 0
    conv_w = jnp.concatenate([small_all[q, o:o + CONV_K * cdq].reshape(CONV_K, cdq) for q in range(N_CHIPS)], axis=1)
    o += CONV_K * cdq
    conv_b = small_all[:, o:o + cdq].reshape(1, CD)
    o += cdq
    norm_w = small_all[:, o:o + niq].reshape(1, DI)
    o += niq
    lng = jnp.transpose(small_all[:, o:o + 4 * dq].reshape(N_CHIPS, 2, 2, dq), (1, 2, 0, 3)).reshape(2, 2, 1, D)
    o += 4 * dq
    lnb = jnp.transpose(small_all[:, o:o + 4 * dq].reshape(N_CHIPS, 2, 2, dq), (1, 2, 0, 3)).reshape(2, 2, 1, D)

    pooled = _pool_fwd("pool_fwd", x0)
    hraw = _mm("pool_mm", "nn", pooled, w_pool, T, D, GD, tn=GD,
               a_spec=lambda tm, tn, tk: pl.BlockSpec((tm, tk), lambda i, j, k: (i, j)),
               b_spec=lambda tm, tn, tk: pl.BlockSpec((None, tk, tn), lambda i, j, k: (j, 0, 0)))
    x1, x1b, xh1, rs1 = _res_ln("ln00", x0, hraw, lng[0, 0], lnb[0, 0], scale=pool_scale)

    def mlp_fwd(l, xb):
        a, h2b = _mm(f"mlp{l}_up", "nn", xb, g_w1, T, DF, D, tn=min(1024, DF // N_CHIPS),
                     b_spec=_colshard_b(l, DF // N_CHIPS), out_dtype=(F32, BF16),
                     epi=lambda acc: (acc, jnp.square(jnp.maximum(acc, 0.0))))
        h = _mm(f"mlp{l}_down", "nn", h2b, g_w2, T, D, DF, k_unit=DF // N_CHIPS,
                b_spec=_rowshard_b(l, DF // N_CHIPS))
        return a, h2b, h

    def ple_fwd(l, xb, pb):
        gl = _mm(f"gate{l}_logit", "nn", xb, g_gw, T, D, D, k_unit=D // N_CHIPS,
                 b_spec=_rowshard_b(l, D // N_CHIPS))
        e = _mm(f"gate{l}_emb", "nn", pb, g_pw, T, D, PD, tn=min(1024, D // N_CHIPS),
                b_spec=_colshard_b(l, D // N_CHIPS))
        return gl, e

    a0, h2b0, h0 = mlp_fwd(0, x1b)
    x2, x2b, xh2, rs2 = _res_ln("ln01", x1, h0, lng[0, 1], lnb[0, 1])
    gl0, e0 = ple_fwd(0, x2b, p0b)
    x3, x3b, gate0 = _gate_fwd("gate0", x2, gl0, e0)

    z = _mm("ssm_in_z", "nn", x3b, w_z, T, DI, D)
    xbc_pre = _mm("ssm_in_xbc", "nn", x3b, w_xbc, T, CD, D)
    dt_pre = _mm("ssm_in_dt", "nn", x3b, w_dt, T, H, D)
    xbc = _conv_fwd("conv_fwd", xbc_pre, conv_w, conv_b)
    dt = _dt_fwd("dt_fwd", dt_pre, ssm_dt_bias)
    dt_g = jnp.transpose(dt.reshape(T, N_GROUPS, HPG), (1, 0, 2))
    dtc = jnp.pad(dt_g, ((0, 0), (0, 0), (0, HEAD_PAD - HPG)))
    dtr = jnp.pad(jnp.transpose(dt_g, (0, 2, 1)), ((0, 0), (0, ROW_PAD - HPG), (0, 0)))
    al_g = ssm_a_log.reshape(N_GROUPS, HPG)
    alr = jnp.pad(al_g, ((0, 0), (0, HEAD_PAD - HPG)))[:, None, :]
    alc = jnp.pad(al_g, ((0, 0), (0, ROW_PAD - HPG)))[:, :, None]
    d_e = jnp.repeat(ssm_d.reshape(N_GROUPS, HPG), HEAD_DIM, axis=1)[:, None, :]
    ysc, hprev = _ssd_fwd("ssd_fwd", xbc, dtc, dtr, alr, alc, d_e, DI)
    ynb, rsn = _gnorm_fwd("gnorm_fwd", ysc, z, norm_w)
    h1 = _mm("ssm_out", "nn", ynb, w_out, T, D, DI)
    x4, x4b, xh4, rs4 = _res_ln("ln10", x3, h1, lng[1, 0], lnb[1, 0])
    a1, h2b1, h2 = mlp_fwd(1, x4b)
    x5, x5b, xh5, rs5 = _res_ln("ln11", x4, h2, lng[1, 1], lnb[1, 1])
    gl1, e1 = ple_fwd(1, x5b, p1b)
    dx6, gate1, loss_parts = _gate_loss("gate1_loss", x5, gl1, e1, tgt)
    loss_local = jnp.sum(loss_parts[0::8, 0])

    def ple_bwd(l, dxo, gate, e, xb, pb):
        dgl, de = _gate_bwd(f"gate{l}_bwd", dxo, gate, e)
        d_gw = _mm(f"gate{l}_dw", "tn", xb, dgl, D, D, T, out_dtype=BF16).reshape(N_CHIPS, D // N_CHIPS, D)
        d_pw = _mm(f"gate{l}_dpw", "tn", pb, de, PD, D, T, out_dtype=BF16, tn=min(1024, D // N_CHIPS),
                   o_shape=(N_CHIPS, PD, D // N_CHIPS), o_spec=_colshard_o(D // N_CHIPS))
        dx = _mm(f"gate{l}_dx", "nt", dgl, g_gw, T, D, D, tn=min(1024, D // N_CHIPS),
                 b_spec=_rowshard_bt(l, D // N_CHIPS), epi=lambda acc, r: acc + r, extras=[dxo])
        return dx, d_gw, d_pw

    def mlp_bwd(l, du, dub, a, h2b, xb):
        d_w2 = _mm(f"mlp{l}_dw2", "tn", h2b, dub, DF, D, T, out_dtype=BF16).reshape(N_CHIPS, DF // N_CHIPS, D)
        da = _mm(f"mlp{l}_da", "nt", dub, g_w2, T, DF, D, tn=min(1024, DF // N_CHIPS),
                 b_spec=_rowshard_bt(l, DF // N_CHIPS), out_dtype=BF16,
                 epi=lambda acc, av: acc * (2.0 * jnp.maximum(av, 0.0)), extras=[a])
        d_w1 = _mm(f"mlp{l}_dw1", "tn", xb, da, D, DF, T, out_dtype=BF16, tn=min(1024, DF // N_CHIPS),
                   o_shape=(N_CHIPS, D, DF // N_CHIPS), o_spec=_colshard_o(DF // N_CHIPS))
        dx = _mm(f"mlp{l}_dx", "nt", da, g_w1, T, D, DF, k_unit=DF // N_CHIPS,
                 b_spec=_colshard_bt(l, DF // N_CHIPS), epi=lambda acc, r: acc + ALPHA * r, extras=[du])
        return dx, d_w1, d_w2

    dx5, d_gw1, d_pw1 = ple_bwd(1, dx6, gate1, e1, x5b, p1b)
    du5, du5b, dg11, db11 = _ln_bwd("ln11_bwd", dx5, xh5, rs5, lng[1, 1])
    dx4, d_w1_1, d_w2_1 = mlp_bwd(1, du5, du5b, a1, h2b1, x4b)
    du4, du4b, dg10, db10 = _ln_bwd("ln10_bwd", dx4, xh4, rs4, lng[1, 0])
    d_wout = _mm("ssm_out_dw", "tn", ynb, du4b, DI, D, T, out_dtype=BF16).reshape(N_CHIPS, DI // N_CHIPS, D)
    dyn = _mm("ssm_out_dx", "nt", du4b, w_out, T, DI, D)
    dysc, dzb, dnorm_w = _gnorm_bwd("gnorm_bwd", dyn, ysc, z, norm_w, rsn)
    dxs, dbm, dcm, ddt_g, dalog_g, dd_g = _ssd_bwd("ssd_bwd", dysc, xbc, dtc, dtr, alr, alc, d_e, hprev, DI)
    dconv, dconv_w, dconv_b = _conv_bwd_a("conv_bwd_a", dxs, dbm, dcm, xbc_pre, conv_w, conv_b)
    dxbcb = _conv_bwd_b("conv_bwd_b", dconv, conv_w)
    ddt = jnp.transpose(ddt_g[:, :, :HPG], (1, 0, 2)).reshape(T, H)
    _, ddtpb, ddt_bias = _dt_bwd("dt_bwd", ddt, dt_pre, ssm_dt_bias)
    da_log = dalog_g[:, 0, :HPG].reshape(1, H)
    dd_skip = dd_g[:, 0, :HPG].reshape(1, H)
    d_wz = _mm("ssm_in_dwz", "tn", x3b, dzb, D, DI, T, out_dtype=BF16)
    d_wx = _mm("ssm_in_dwx", "tn", x3b, dxbcb, D, CD, T, out_dtype=BF16)
    d_wdt = _mm("ssm_in_dwdt", "tn", x3b, ddtpb, D, H, T, out_dtype=BF16)
    d_win = jnp.transpose(jnp.concatenate([d_wz, d_wx, d_wdt], axis=1).reshape(D, N_CHIPS, NIN), (1, 0, 2))
    dx3 = _mm("ssm_in_dxz", "nt", dzb, w_z, T, D, DI, epi=lambda acc, r: acc + ALPHA * r, extras=[du4])
    dx3 = _mm("ssm_in_dxx", "nt", dxbcb, w_xbc, T, D, CD, epi=lambda acc, r: acc + r, extras=[dx3])
    dx3 = _mm("ssm_in_dxdt", "nt", ddtpb, w_dt, T, D, H, epi=lambda acc, r: acc + r, extras=[dx3])

    dx2, d_gw0, d_pw0 = ple_bwd(0, dx3, gate0, e0, x2b, p0b)
    du2, du2b, dg01, db01 = _ln_bwd("ln01_bwd", dx2, xh2, rs2, lng[0, 1])
    dx1, d_w1_0, d_w2_0 = mlp_bwd(0, du2, du2b, a0, h2b0, x1b)
    du1, dhrb, dg00, db00, dscale = _ln_bwd("ln00_bwd", dx1, xh1, rs1, lng[0, 0], hraw=hraw, scale=pool_scale)
    d_wpool = _mm("pool_dw", "tn", pooled, dhrb, D, GD, T, tm=GD, tn=GD, out_dtype=BF16,
                  b_spec=lambda tm, tn, tk: pl.BlockSpec((tk, tn), lambda i, j, k: (k, i)),
                  o_shape=(NG, GD, GD), o_spec=lambda tm, tn, tk: pl.BlockSpec((None, tm, tn), lambda i, j, k: (i, 0, 0)))
    dpooled = _mm("pool_dx", "nt", dhrb, w_pool, T, D, GD, tn=GD,
                  a_spec=lambda tm, tn, tk: pl.BlockSpec((tm, tk), lambda i, j, k: (i, j)),
                  b_spec=lambda tm, tn, tk: pl.BlockSpec((None, tn, tk), lambda i, j, k: (j, 0, 0)))
    grad_x = _pool_bwd("pool_bwd", dpooled, du1)
    d_wpool = jnp.transpose(d_wpool.reshape(NG, N_CHIPS, GD // N_CHIPS, GD), (1, 0, 2, 3)).reshape(N_CHIPS, NG * GD // N_CHIPS, GD)

    red = _reduce_scatter("rs", pos, [d_wpool, d_win, d_wout, d_w1_0, d_w1_1, d_w2_0, d_w2_1,
                                      d_gw0, d_gw1, d_pw0, d_pw1])
    r_pool, r_in, r_out, r_w1_0, r_w1_1, r_w2_0, r_w2_1, r_gw0, r_gw1, r_pw0, r_pw1 = red

    dln_g = jnp.stack([jnp.stack([dg00, dg01]), jnp.stack([dg10, dg11])]).reshape(-1)
    dln_b = jnp.stack([jnp.stack([db00, db01]), jnp.stack([db10, db11])]).reshape(-1)
    small_g = jnp.concatenate([dscale.reshape(-1), dconv_w.reshape(-1), dconv_b.reshape(-1), ddt_bias.reshape(-1),
                               da_log.reshape(-1), dd_skip.reshape(-1), dnorm_w.reshape(-1), dln_g, dln_b,
                               loss_local.reshape(1)])
    n_sg = small_g.shape[0]
    sg_all = _allgather_small("gather_small_grads", _pad_to(small_g, 1024).reshape(-1, 128))
    sg = _sum_peers("sum_small_grads", sg_all).reshape(-1)[:n_sg]
    o = 0

    def take(nel):
        nonlocal o
        v = sg[o:o + nel]
        o += nel
        return v

    g_scale = take(D).reshape(1, D)
    g_conv_w_full = take(CONV_K * CD).reshape(CONV_K, CD)
    g_conv_b_full = take(CD).reshape(1, CD)
    g_dt_bias = take(H).reshape(1, H)
    g_a_log = take(H).reshape(1, H)
    g_d = take(H).reshape(1, H)
    g_norm_full = take(DI).reshape(1, DI)
    g_lng_full = take(4 * D).reshape(2, 2, D)
    g_lnb_full = take(4 * D).reshape(2, 2, D)
    loss = take(1).reshape(())
    g_conv_w = lax.dynamic_slice_in_dim(g_conv_w_full, chip * cdq, cdq, axis=1)[None]
    g_conv_b = lax.dynamic_slice_in_dim(g_conv_b_full, chip * cdq, cdq, axis=1)
    g_norm = lax.dynamic_slice_in_dim(g_norm_full, chip * niq, niq, axis=1)
    g_lng = lax.dynamic_slice_in_dim(g_lng_full, chip * dq, dq, axis=2)
    g_lnb = lax.dynamic_slice_in_dim(g_lnb_full, chip * dq, dq, axis=2)

    def adam(name, w, m, v, grads):
        shp = w.shape
        L = len(grads)
        w3, m3, v3 = (t.reshape((L, -1, shp[-1])) for t in (w, m, v))
        return [t.reshape(shp) for t in _adamw(name, w3, m3, v3, grads)]

    big = {
        "pool_w": adam("adam_pool_w", pool_w, m_pool_w, v_pool_w, [r_pool]),
        "ssm_in_w": adam("adam_ssm_in_w", ssm_in_w, m_ssm_in_w, v_ssm_in_w, [r_in]),
        "ssm_out_w": adam("adam_ssm_out_w", ssm_out_w, m_ssm_out_w, v_ssm_out_w, [r_out]),
        "mlp_w1": adam("adam_mlp_w1", mlp_w1, m_mlp_w1, v_mlp_w1, [r_w1_0, r_w1_1]),
        "mlp_w2": adam("adam_mlp_w2", mlp_w2, m_mlp_w2, v_mlp_w2, [r_w2_0, r_w2_1]),
        "ple_w": adam("adam_ple_w", ple_w, m_ple_w, v_ple_w, [r_pw0, r_pw1]),
        "ple_gate_w": adam("adam_ple_gate_w", ple_gate_w, m_ple_gate_w, v_ple_gate_w, [r_gw0, r_gw1]),
    }

    small = [("pool_scale", pool_scale, m_pool_scale, v_pool_scale, g_scale),
             ("ssm_conv_w", ssm_conv_w, m_ssm_conv_w, v_ssm_conv_w, g_conv_w),
             ("ssm_conv_b", ssm_conv_b, m_ssm_conv_b, v_ssm_conv_b, g_conv_b),
             ("ssm_dt_bias", ssm_dt_bias, m_ssm_dt_bias, v_ssm_dt_bias, g_dt_bias),
             ("ssm_a_log", ssm_a_log, m_ssm_a_log, v_ssm_a_log, g_a_log),
             ("ssm_d", ssm_d, m_ssm_d, v_ssm_d, g_d),
             ("ssm_norm_w", ssm_norm_w, m_ssm_norm_w, v_ssm_norm_w, g_norm),
             ("ln_g", ln_g, m_ln_g, v_ln_g, g_lng),
             ("ln_b", ln_b, m_ln_b, v_ln_b, g_lnb)]

    def pack(idx):
        flat = _pad_to(jnp.concatenate([s[idx].reshape(-1) for s in small]), 1024)
        return flat.reshape(1, -1, 128)

    sm_out = _adamw("adam_small", pack(1), pack(2), pack(3), [pack(4)[0]])
    small_res = {}
    o = 0
    for s in small:
        nel = s[1].size
        small_res[s[0]] = [t.reshape(-1)[o:o + nel].reshape(s[1].shape) for t in sm_out]
        o += nel

    order = ["pool_w", "pool_scale", "ssm_in_w", "ssm_conv_w", "ssm_conv_b", "ssm_dt_bias", "ssm_a_log", "ssm_d",
             "ssm_norm_w", "ssm_out_w", "mlp_w1", "mlp_w2", "ln_g", "ln_b", "ple_w", "ple_gate_w"]
    res = {**big, **small_res}
    outs = [loss, grad_x[None]]
    for kind in range(4):
        outs += [res[nm][kind] for nm in order]
    return tuple(outs)
```

```python
import functools

import jax
import jax.numpy as jnp
from jax import lax
from jax.experimental import pallas as pl
from jax.experimental.pallas import tpu as pltpu

F32 = jnp.float32
BF16 = jnp.bfloat16
HI = lax.Precision.HIGHEST
MESH = pl.DeviceIdType.MESH

DEPTH = 2
ALPHA = (2.0 * DEPTH) ** 0.25
LN_EPS = 1e-5
RMS_EPS = 1e-5
POOL_WINDOWS = (2, 4, 8, 16)
POOL_HALO = 16
HEAD_DIM = 64
N_GROUPS = 8
D_STATE = 128
CHUNK = 128
CONV_K = 4
CONV_HALO = 8
HEAD_PAD = 128
ROW_PAD = 8
N_CHIPS = 4
N_DEV = 8
ADAM_LR = 0.001
ADAM_B1 = 0.9
ADAM_B2 = 0.999
ADAM_EPS = 1e-08
ADAM_WD = 0.01
ADAM_STEP = 10
VMEM_LIMIT = 56 * 1024 * 1024
ADAM_BLOCK_BYTES = 1024 * 1024
MM_VMEM_BUDGET = 40 * 1024 * 1024


def _cp(*sem):
    return pltpu.CompilerParams(dimension_semantics=sem, vmem_limit_bytes=VMEM_LIMIT)


def _pick(dim, pref):
    t = pref
    while t >= 128:
        if dim % t == 0:
            return t
        t //= 2
    return dim


def _rows(rows, row_bytes, budget):
    t = rows
    while t * row_bytes > budget and t % 16 == 0:
        t //= 2
    return t


def _sigmoid(v):
    return 1.0 / (1.0 + jnp.exp(-v))


_DIMS = {"nn": (((1,), (0,)), ((), ())), "nt": (((1,), (1,)), ((), ())), "tn": (((0,), (0,)), ((), ()))}


def _pick_k(k_unit, fixed_bytes, per_k_bytes):
    for n in range(1, k_unit // 128 + 1):
        if k_unit % n == 0 and (n == 1 or (k_unit // n) % 128 == 0):
            if fixed_bytes + (k_unit // n) * per_k_bytes <= MM_VMEM_BUDGET:
                return k_unit // n
    return min(k_unit, 128)


def _mm(name, form, a, b, M, N, K, *, tm=1024, tn=1024, k_unit=None, a_spec=None, b_spec=None,
        o_shape=None, o_spec=None, out_dtype=F32, pro=None, epi=None, extras=(), deps=()):
    tm, tn = _pick(M, tm), _pick(N, tn)
    out_dtypes = out_dtype if isinstance(out_dtype, tuple) else (out_dtype,)
    n_out = len(out_dtypes)
    in_place = n_out == 1 and out_dtypes[0] == F32
    fixed = 2 * tm * tn * (sum(jnp.dtype(d).itemsize for d in out_dtypes) + 4 * len(extras))
    fixed += 0 if in_place else 4 * tm * tn
    tk = _pick_k(K if k_unit is None else k_unit, fixed,
                 2 * (tm * a.dtype.itemsize + tn * b.dtype.itemsize))
    nk = K // tk
    if a_spec is None:
        a_spec = (pl.BlockSpec((tk, tm), lambda i, j, k: (k, i)) if form == "tn"
                  else pl.BlockSpec((tm, tk), lambda i, j, k: (i, k)))
    else:
        a_spec = a_spec(tm, tn, tk)
    if b_spec is None:
        b_spec = (pl.BlockSpec((tn, tk), lambda i, j, k: (j, k)) if form == "nt"
                  else pl.BlockSpec((tk, tn), lambda i, j, k: (k, j)))
    else:
        b_spec = b_spec(tm, tn, tk)
    if o_spec is None:
        o_spec = pl.BlockSpec((tm, tn), lambda i, j, k: (i, j))
        o_shape = (M, N)
    else:
        o_spec = o_spec(tm, tn, tk)
    ex_arrays = [e for e in extras]
    ex_specs = [pl.BlockSpec((tm, tn), lambda i, j, k: (i, j)) for _ in extras]
    ne = len(ex_arrays)
    nd = len(deps)
    dep_specs = [pl.BlockSpec((8, 128), lambda i, j, k: (0, 0)) for _ in deps]
    dims = _DIMS[form]
    use_scratch = nk > 1 and not in_place

    def body(a_ref, b_ref, *rest):
        ex_refs = rest[:ne]
        o_refs = rest[ne + nd:ne + nd + n_out]
        at = a_ref[...]
        if pro is not None:
            at = pro(at)
        p = lax.dot_general(at.astype(BF16), b_ref[...].astype(BF16), dims, preferred_element_type=F32)

        def finish(acc):
            res = acc if epi is None else epi(acc, *[r[...] for r in ex_refs])
            res = res if isinstance(res, tuple) else (res,)
            for o_ref, r, d in zip(o_refs, res, out_dtypes):
                o_ref[...] = r.astype(d)

        if nk == 1:
            finish(p)
        else:
            acc_ref = rest[ne + nd + n_out] if use_scratch else o_refs[0]
            k = pl.program_id(2)

            @pl.when(k == 0)
            def _():
                acc_ref[...] = p

            @pl.when(jnp.logical_and(k > 0, k < nk - 1))
            def _():
                acc_ref[...] += p

            @pl.when(k == nk - 1)
            def _():
                finish(acc_ref[...] + p)

    res = pl.pallas_call(
        body, name=name, grid=(M // tm, N // tn, nk),
        in_specs=[a_spec, b_spec] + ex_specs + dep_specs, out_specs=[o_spec] * n_out,
        out_shape=[jax.ShapeDtypeStruct(o_shape, d) for d in out_dtypes],
        scratch_shapes=[pltpu.VMEM((tm, tn), F32)] if use_scratch else [],
        compiler_params=_cp("parallel", "parallel", "arbitrary"),
    )(a, b, *ex_arrays, *deps)
    return res if n_out > 1 else res[0]


def _colshard_b(l, n_per):
    def make(tm, tn, tk):
        nb = n_per // tn
        return pl.BlockSpec((None, None, tk, tn), lambda i, j, k: (j // nb, l, k, j % nb))
    return make


def _colshard_bt(l, n_per):
    def make(tm, tn, tk):
        nb = n_per // tk
        return pl.BlockSpec((None, None, tn, tk), lambda i, j, k: (k // nb, l, j, k % nb))
    return make


def _colshard_o(n_per):
    def make(tm, tn, tk):
        nb = n_per // tn
        return pl.BlockSpec((None, tm, tn), lambda i, j, k: (j // nb, i, j % nb))
    return make


def _rowshard_b(l, k_per):
    def make(tm, tn, tk):
        nb = k_per // tk
        return pl.BlockSpec((None, None, tk, tn), lambda i, j, k: (k // nb, l, k % nb, j))
    return make


def _rowshard_bt(l, k_per):
    def make(tm, tn, tk):
        nb = k_per // tn
        return pl.BlockSpec((None, None, tn, tk), lambda i, j, k: (j // nb, l, j % nb, k))
    return make


def _res_ln(name, xprev, h, g, b, scale=None):
    T, D = xprev.shape
    tr = _pick(T, 256)
    row = pl.BlockSpec((tr, D), lambda i: (i, 0))
    vec = pl.BlockSpec((1, D), lambda i: (0, 0))
    has_scale = scale is not None

    def body(*refs):
        if has_scale:
            x_ref, h_ref, s_ref, g_ref, b_ref, y_ref, yb_ref, xh_ref, rs_ref = refs
            hh = h_ref[...] * s_ref[...]
        else:
            x_ref, h_ref, g_ref, b_ref, y_ref, yb_ref, xh_ref, rs_ref = refs
            hh = h_ref[...]
        u = ALPHA * x_ref[...] + hh
        mu = jnp.mean(u, axis=-1, keepdims=True)
        d = u - mu
        var = jnp.mean(d * d, axis=-1, keepdims=True)
        rs = lax.rsqrt(var + LN_EPS)
        xh = d * rs
        y = xh * g_ref[...] + b_ref[...]
        y_ref[...] = y
        yb_ref[...] = y.astype(BF16)
        xh_ref[...] = xh
        rs_ref[...] = rs

    ins = [xprev, h] + ([scale] if has_scale else []) + [g, b]
    specs = [row, row] + ([vec] if has_scale else []) + [vec, vec]
    return pl.pallas_call(
        body, name=name, grid=(T // tr,), in_specs=specs,
        out_specs=[row, row, row, pl.BlockSpec((tr, 1), lambda i: (i, 0))],
        out_shape=[jax.ShapeDtypeStruct((T, D), F32), jax.ShapeDtypeStruct((T, D), BF16),
                   jax.ShapeDtypeStruct((T, D), F32), jax.ShapeDtypeStruct((T, 1), F32)],
        compiler_params=_cp("parallel"),
    )(*ins)


def _accum(ref, part, first):
    @pl.when(first)
    def _():
        ref[...] = part

    @pl.when(jnp.logical_not(first))
    def _():
        ref[...] += part


def _ln_bwd(name, dy, xh, rs, g, hraw=None, scale=None):
    T, D = dy.shape
    tr = _pick(T, 256)
    row = pl.BlockSpec((tr, D), lambda i: (i, 0))
    vec = pl.BlockSpec((1, D), lambda i: (0, 0))
    has_scale = scale is not None

    def body(*refs):
        if has_scale:
            dy_ref, xh_ref, rs_ref, g_ref, hr_ref, s_ref, du_ref, dub_ref, dg_ref, db_ref, ds_ref = refs
        else:
            dy_ref, xh_ref, rs_ref, g_ref, du_ref, dub_ref, dg_ref, db_ref = refs
        first = pl.program_id(0) == 0
        dyv = dy_ref[...]
        xhv = xh_ref[...]
        dxh = dyv * g_ref[...]
        m1 = jnp.mean(dxh, axis=-1, keepdims=True)
        m2 = jnp.mean(dxh * xhv, axis=-1, keepdims=True)
        du = rs_ref[...] * (dxh - m1 - xhv * m2)
        du_ref[...] = du
        if has_scale:
            dub_ref[...] = (du * s_ref[...]).astype(BF16)
            _accum(ds_ref, jnp.sum(du * hr_ref[...], axis=0, keepdims=True), first)
        else:
            dub_ref[...] = du.astype(BF16)
        _accum(dg_ref, jnp.sum(dyv * xhv, axis=0, keepdims=True), first)
        _accum(db_ref, jnp.sum(dyv, axis=0, keepdims=True), first)

    ins = [dy, xh, rs, g] + ([hraw, scale] if has_scale else [])
    specs = [row, row, pl.BlockSpec((tr, 1), lambda i: (i, 0)), vec] + ([row, vec] if has_scale else [])
    n_vec = 3 if has_scale else 2
    return pl.pallas_call(
        body, name=name, grid=(T // tr,), in_specs=specs,
        out_specs=[row, row] + [vec] * n_vec,
        out_shape=[jax.ShapeDtypeStruct((T, D), F32), jax.ShapeDtypeStruct((T, D), BF16)]
        + [jax.ShapeDtypeStruct((1, D), F32)] * n_vec,
        compiler_params=_cp("arbitrary"),
    )(*ins)


def _gate_fwd(name, x, gl, e):
    T, D = x.shape
    tr = _pick(T, 256)
    row = pl.BlockSpec((tr, D), lambda i: (i, 0))

    def body(x_ref, gl_ref, e_ref, xo_ref, xob_ref, gate_ref):
        gate = _sigmoid(gl_ref[...])
        xo = x_ref[...] + gate * e_ref[...]
        xo_ref[...] = xo
        xob_ref[...] = xo.astype(BF16)
        gate_ref[...] = gate

    return pl.pallas_call(
        body, name=name, grid=(T // tr,), in_specs=[row, row, row], out_specs=[row, row, row],
        out_shape=[jax.ShapeDtypeStruct((T, D), F32), jax.ShapeDtypeStruct((T, D), BF16),
                   jax.ShapeDtypeStruct((T, D), F32)],
        compiler_params=_cp("parallel"),
    )(x, gl, e)


def _gate_loss(name, x, gl, e, tgt):
    T, D = x.shape
    tr = _pick(T, 256)
    row = pl.BlockSpec((tr, D), lambda i: (i, 0))

    def body(x_ref, gl_ref, e_ref, t_ref, dy_ref, gate_ref, lp_ref):
        gate = _sigmoid(gl_ref[...])
        err = x_ref[...] + gate * e_ref[...] - t_ref[...]
        dy_ref[...] = err * (1.0 / D)
        gate_ref[...] = gate
        s = jnp.sum(jnp.mean(err * err, axis=-1, keepdims=True), axis=0, keepdims=True)
        lp_ref[...] = jnp.broadcast_to(0.5 * s, (8, 128))

    return pl.pallas_call(
        body, name=name, grid=(T // tr,), in_specs=[row] * 4,
        out_specs=[row, row, pl.BlockSpec((8, 128), lambda i: (i, 0))],
        out_shape=[jax.ShapeDtypeStruct((T, D), F32), jax.ShapeDtypeStruct((T, D), F32),
                   jax.ShapeDtypeStruct((T // tr * 8, 128), F32)],
        compiler_params=_cp("parallel"),
    )(x, gl, e, tgt)


def _gate_bwd(name, dxo, gate, e):
    T, D = dxo.shape
    tr = _pick(T, 256)
    row = pl.BlockSpec((tr, D), lambda i: (i, 0))

    def body(d_ref, gate_ref, e_ref, dgl_ref, de_ref):
        d = d_ref[...]
        gate = gate_ref[...]
        dgl_ref[...] = (d * e_ref[...] * gate * (1.0 - gate)).astype(BF16)
        de_ref[...] = (d * gate).astype(BF16)

    return pl.pallas_call(
        body, name=name, grid=(T // tr,), in_specs=[row] * 3, out_specs=[row, row],
        out_shape=[jax.ShapeDtypeStruct((T, D), BF16)] * 2,
        compiler_params=_cp("parallel"),
    )(dxo, gate, e)


def _gnorm_fwd(name, y, z, w):
    T, DI = y.shape
    tr = _pick(T, 128)
    row = pl.BlockSpec((tr, DI), lambda i: (i, 0))
    vec = pl.BlockSpec((1, DI), lambda i: (0, 0))
    col = pl.BlockSpec((tr, 1), lambda i: (i, 0))

    def body(y_ref, z_ref, w_ref, o_ref, rs_ref):
        zv = z_ref[...]
        yz = y_ref[...] * (zv * _sigmoid(zv))
        rs = lax.rsqrt(jnp.mean(yz * yz, axis=-1, keepdims=True) + RMS_EPS)
        o_ref[...] = (yz * rs * w_ref[...]).astype(BF16)
        rs_ref[...] = rs

    return pl.pallas_call(
        body, name=name, grid=(T // tr,), in_specs=[row, row, vec], out_specs=[row, col],
        out_shape=[jax.ShapeDtypeStruct((T, DI), BF16), jax.ShapeDtypeStruct((T, 1), F32)],
        compiler_params=_cp("parallel"),
    )(y, z, w)


def _gnorm_bwd(name, dyn, y, z, w, rs):
    T, DI = y.shape
    tr = _pick(T, 128)
    row = pl.BlockSpec((tr, DI), lambda i: (i, 0))
    vec = pl.BlockSpec((1, DI), lambda i: (0, 0))
    col = pl.BlockSpec((tr, 1), lambda i: (i, 0))

    def body(d_ref, y_ref, z_ref, w_ref, rs_ref, dy_ref, dz_ref, dw_ref):
        first = pl.program_id(0) == 0
        zv = z_ref[...]
        yv = y_ref[...]
        sg = _sigmoid(zv)
        sz = zv * sg
        rsv = rs_ref[...]
        yzh = yv * sz * rsv
        dv = d_ref[...]
        gw = dv * w_ref[...]
        m = jnp.mean(gw * yzh, axis=-1, keepdims=True)
        dyz = rsv * (gw - yzh * m)
        dy_ref[...] = dyz * sz
        dz_ref[...] = (dyz * yv * (sg * (1.0 + zv * (1.0 - sg)))).astype(BF16)
        _accum(dw_ref, jnp.sum(dv * yzh, axis=0, keepdims=True), first)

    return pl.pallas_call(
        body, name=name, grid=(T // tr,), in_specs=[row, row, row, vec, col], out_specs=[row, row, vec],
        out_shape=[jax.ShapeDtypeStruct((T, DI), F32), jax.ShapeDtypeStruct((T, DI), BF16),
                   jax.ShapeDtypeStruct((1, DI), F32)],
        compiler_params=_cp("arbitrary"),
    )(dyn, y, z, w, rs)


def _sel4(j, vals):
    return jnp.where(j == 0, vals[0], jnp.where(j == 1, vals[1], jnp.where(j == 2, vals[2], vals[3])))


def _pool_cnt(i, j, tr, rows, offset):
    t = i * tr + offset + lax.broadcasted_iota(jnp.int32, (rows, 1), 0)
    win = _sel4(j, POOL_WINDOWS)
    return jnp.minimum(t + 1, win).astype(F32)


def _pool_fwd(name, x):
    T, D = x.shape
    gd = D // len(POOL_WINDOWS)
    tr = _pick(T, 512)
    hb = tr // POOL_HALO

    def body(x_ref, h_ref, o_ref):
        i, j = pl.program_id(0), pl.program_id(1)
        xv = x_ref[...]
        halo = jnp.where(i > 0, h_ref[...], 0.0)
        cat = jnp.concatenate([halo, xv], axis=0)
        s2 = cat + pltpu.roll(cat, 1, 0)
        s4 = s2 + pltpu.roll(s2, 2, 0)
        s8 = s4 + pltpu.roll(s4, 4, 0)
        s16 = s8 + pltpu.roll(s8, 8, 0)
        sel = _sel4(j, (s2, s4, s8, s16))[POOL_HALO:]
        o_ref[...] = (sel / _pool_cnt(i, j, tr, tr, 0) - xv).astype(BF16)

    return pl.pallas_call(
        body, name=name, grid=(T // tr, len(POOL_WINDOWS)),
        in_specs=[pl.BlockSpec((tr, gd), lambda i, j: (i, j)),
                  pl.BlockSpec((POOL_HALO, gd), lambda i, j: (jnp.maximum(i * hb - 1, 0), j))],
        out_specs=pl.BlockSpec((tr, gd), lambda i, j: (i, j)),
        out_shape=jax.ShapeDtypeStruct((T, D), BF16),
        compiler_params=_cp("parallel", "parallel"),
    )(x, x)


def _pool_bwd(name, dp, du):
    T, D = dp.shape
    gd = D // len(POOL_WINDOWS)
    tr = _pick(T, 512)
    hb = tr // POOL_HALO
    last_h = T // POOL_HALO - 1
    n = tr + POOL_HALO

    def body(dp_ref, h_ref, du_ref, o_ref):
        i, j = pl.program_id(0), pl.program_id(1)
        dpv = dp_ref[...]
        q = dpv / _pool_cnt(i, j, tr, tr, 0)
        qh = jnp.where(i < pl.num_programs(0) - 1, h_ref[...] / _pool_cnt(i, j, tr, POOL_HALO, tr), 0.0)
        cat = jnp.concatenate([q, qh], axis=0)
        f2 = cat + pltpu.roll(cat, n - 1, 0)
        f4 = f2 + pltpu.roll(f2, n - 2, 0)
        f8 = f4 + pltpu.roll(f4, n - 4, 0)
        f16 = f8 + pltpu.roll(f8, n - 8, 0)
        sel = _sel4(j, (f2, f4, f8, f16))[:tr]
        o_ref[...] = ALPHA * du_ref[...] + sel - dpv

    return pl.pallas_call(
        body, name=name, grid=(T // tr, len(POOL_WINDOWS)),
        in_specs=[pl.BlockSpec((tr, gd), lambda i, j: (i, j)),
                  pl.BlockSpec((POOL_HALO, gd), lambda i, j: (jnp.minimum((i + 1) * hb, last_h), j)),
                  pl.BlockSpec((tr, gd), lambda i, j: (i, j))],
        out_specs=pl.BlockSpec((tr, gd), lambda i, j: (i, j)),
        out_shape=jax.ShapeDtypeStruct((T, D), F32),
        compiler_params=_cp("parallel", "parallel"),
    )(dp, dp, du)


def _conv_taps(cat, wv, rows):
    shifted = [cat[CONV_HALO:] if s == 0 else pltpu.roll(cat, s, 0)[CONV_HALO:] for s in range(CONV_K)]
    acc = shifted[0] * wv[CONV_K - 1:CONV_K]
    for s in range(1, CONV_K):
        acc = acc + shifted[s] * wv[CONV_K - 1 - s:CONV_K - s]
    return acc, shifted


def _conv_fwd(name, xp, w, b):
    T, CD = xp.shape
    tr, tc = _pick(T, 512), _pick(CD, 512)
    hb = tr // CONV_HALO

    def body(x_ref, h_ref, w_ref, b_ref, o_ref):
        i = pl.program_id(0)
        halo = jnp.where(i > 0, h_ref[...], 0.0)
        cat = jnp.concatenate([halo, x_ref[...]], axis=0)
        acc, _ = _conv_taps(cat, w_ref[...], tr)
        acc = acc + b_ref[...]
        o_ref[...] = acc * _sigmoid(acc)

    return pl.pallas_call(
        body, name=name, grid=(T // tr, CD // tc),
        in_specs=[pl.BlockSpec((tr, tc), lambda i, j: (i, j)),
                  pl.BlockSpec((CONV_HALO, tc), lambda i, j: (jnp.maximum(i * hb - 1, 0), j)),
                  pl.BlockSpec((CONV_K, tc), lambda i, j: (0, j)),
                  pl.BlockSpec((1, tc), lambda i, j: (0, j))],
        out_specs=pl.BlockSpec((tr, tc), lambda i, j: (i, j)),
        out_shape=jax.ShapeDtypeStruct((T, CD), F32),
        compiler_params=_cp("parallel", "parallel"),
    )(xp, xp, w, b)


def _conv_bwd_a(name, dxs, dbm, dcm, xp, w, b):
    T, CD = xp.shape
    tr = _pick(T, 512)
    tc = _pick(dbm.shape[1], 512)
    hb = tr // CONV_HALO
    nx, nb = dxs.shape[1] // tc, dbm.shape[1] // tc

    def part_spec(lo, n):
        def imap(j, i):
            inside = jnp.logical_and(j >= lo, j < lo + n)
            return (jnp.where(inside, i, 0), jnp.clip(j - lo, 0, n - 1))
        return pl.BlockSpec((tr, tc), imap)

    def body(dx_ref, db_ref, dc_ref, x_ref, h_ref, w_ref, b_ref, o_ref, dw_ref, dbias_ref):
        j, i = pl.program_id(0), pl.program_id(1)
        first = i == 0
        d = jnp.where(j < nx, dx_ref[...], jnp.where(j < nx + nb, db_ref[...], dc_ref[...]))
        halo = jnp.where(i > 0, h_ref[...], 0.0)
        cat = jnp.concatenate([halo, x_ref[...]], axis=0)
        acc, shifted = _conv_taps(cat, w_ref[...], tr)
        acc = acc + b_ref[...]
        sg = _sigmoid(acc)
        dconv = d * (sg * (1.0 + acc * (1.0 - sg)))
        o_ref[...] = dconv
        _accum(dbias_ref, jnp.sum(dconv, axis=0, keepdims=True), first)
        tap = lax.broadcasted_iota(jnp.int32, (CONV_K, tc), 0)
        dwv = jnp.zeros((CONV_K, tc), F32)
        for s in range(CONV_K):
            dwv = jnp.where(tap == CONV_K - 1 - s, jnp.sum(dconv * shifted[s], axis=0, keepdims=True), dwv)
        _accum(dw_ref, dwv, first)

    return pl.pallas_call(
        body, name=name, grid=(CD // tc, T // tr),
        in_specs=[part_spec(0, nx), part_spec(nx, nb), part_spec(nx + nb, nb),
                  pl.BlockSpec((tr, tc), lambda j, i: (i, j)),
                  pl.BlockSpec((CONV_HALO, tc), lambda j, i: (jnp.maximum(i * hb - 1, 0), j)),
                  pl.BlockSpec((CONV_K, tc), lambda j, i: (0, j)),
                  pl.BlockSpec((1, tc), lambda j, i: (0, j))],
        out_specs=[pl.BlockSpec((tr, tc), lambda j, i: (i, j)),
                   pl.BlockSpec((CONV_K, tc), lambda j, i: (0, j)),
                   pl.BlockSpec((1, tc), lambda j, i: (0, j))],
        out_shape=[jax.ShapeDtypeStruct((T, CD), F32), jax.ShapeDtypeStruct((CONV_K, CD), F32),
                   jax.ShapeDtypeStruct((1, CD), F32)],
        compiler_params=_cp("parallel", "arbitrary"),
    )(dxs, dbm, dcm, xp, xp, w, b)


def _conv_bwd_b(name, dconv, w):
    T, CD = dconv.shape
    tr, tc = _pick(T, 512), _pick(CD, 512)
    hb = tr // CONV_HALO
    last_h = T // CONV_HALO - 1
    n = tr + CONV_HALO

    def body(d_ref, h_ref, w_ref, o_ref):
        i = pl.program_id(0)
        halo = jnp.where(i < pl.num_programs(0) - 1, h_ref[...], 0.0)
        cat = jnp.concatenate([d_ref[...], halo], axis=0)
        wv = w_ref[...]
        acc = cat[:tr] * wv[CONV_K - 1:CONV_K]
        for s in range(1, CONV_K):
            acc = acc + pltpu.roll(cat, n - s, 0)[:tr] * wv[CONV_K - 1 - s:CONV_K - s]
        o_ref[...] = acc.astype(BF16)

    return pl.pallas_call(
        body, name=name, grid=(T // tr, CD // tc),
        in_specs=[pl.BlockSpec((tr, tc), lambda i, j: (i, j)),
                  pl.BlockSpec((CONV_HALO, tc), lambda i, j: (jnp.minimum((i + 1) * hb, last_h), j)),
                  pl.BlockSpec((CONV_K, tc), lambda i, j: (0, j))],
        out_specs=pl.BlockSpec((tr, tc), lambda i, j: (i, j)),
        out_shape=jax.ShapeDtypeStruct((T, CD), BF16),
        compiler_params=_cp("parallel", "parallel"),
    )(dconv, dconv, w)


def _dt_fwd(name, dtp, bias):
    T, H = dtp.shape

    def body(x_ref, b_ref, o_ref):
        v = x_ref[...] + b_ref[...]
        u = jnp.exp(-jnp.abs(v))
        w1 = 1.0 + u
        lp = jnp.where(w1 == 1.0, u, jnp.log(w1) * (u / jnp.where(w1 == 1.0, 1.0, w1 - 1.0)))
        o_ref[...] = jnp.maximum(v, 0.0) + lp

    return pl.pallas_call(body, name=name, out_shape=jax.ShapeDtypeStruct((T, H), F32))(dtp, bias)


def _dt_bwd(name, ddt, dtp, bias):
    T, H = dtp.shape

    def body(d_ref, x_ref, b_ref, o_ref, ob_ref, db_ref):
        g = d_ref[...] * _sigmoid(x_ref[...] + b_ref[...])
        o_ref[...] = g
        ob_ref[...] = g.astype(BF16)
        db_ref[...] = jnp.sum(g, axis=0, keepdims=True)

    return pl.pallas_call(
        body, name=name,
        out_shape=[jax.ShapeDtypeStruct((T, H), F32), jax.ShapeDtypeStruct((T, H), BF16),
                   jax.ShapeDtypeStruct((1, H), F32)],
    )(ddt, dtp, bias)


def _dotf(a, b):
    return jnp.dot(a, b, precision=HI, preferred_element_type=F32)


def _dotb(form, a, b):
    return lax.dot_general(a.astype(BF16), b.astype(BF16), _DIMS[form], preferred_element_type=F32)


def _ssd_common(dtc, dtr, alr, alc, gw):
    li = lax.broadcasted_iota(jnp.int32, (CHUNK, CHUNK), 0)
    si = lax.broadcasted_iota(jnp.int32, (CHUNK, CHUNK), 1)
    tri = (li >= si).astype(F32)
    trit = (li <= si).astype(F32)
    a_row = -jnp.exp(alr)
    a_col = -jnp.exp(alc)
    acs_c = _dotf(tri, dtc * a_row)
    acs_r = _dotf(dtr * a_col, trit)
    eh = lax.broadcasted_iota(jnp.int32, (HEAD_PAD, gw), 0)
    ec = lax.broadcasted_iota(jnp.int32, (HEAD_PAD, gw), 1) // HEAD_DIM
    expand = (eh == ec).astype(F32)
    th = lax.broadcasted_iota(jnp.int32, (gw, HEAD_PAD), 1)
    tc = lax.broadcasted_iota(jnp.int32, (gw, HEAD_PAD), 0) // HEAD_DIM
    reduce_ = (th == tc).astype(F32)
    acs_last = acs_c[CHUNK - 1:CHUNK, :]
    return dict(li=li, si=si, tri=tri, trit=trit, a_row=a_row, acs_c=acs_c, acs_r=acs_r,
                expand=expand, reduce=reduce_, acs_last=acs_last,
                dt_e=_dotf(dtc, expand), eacs_e=jnp.exp(_dotf(acs_c, expand)),
                dec_h=jnp.exp(acs_last - acs_c))


def _ssd_specs(T, DI, gw, nc, rev):
    nsb = DI // D_STATE
    cidx = (lambda c: nc - 1 - c) if rev else (lambda c: c)
    return dict(
        xs=pl.BlockSpec((CHUNK, gw), lambda g, c: (cidx(c), g)),
        bm=pl.BlockSpec((CHUNK, D_STATE), lambda g, c: (cidx(c), nsb + g)),
        cm=pl.BlockSpec((CHUNK, D_STATE), lambda g, c: (cidx(c), nsb + N_GROUPS + g)),
        dtc=pl.BlockSpec((None, CHUNK, HEAD_PAD), lambda g, c: (g, cidx(c), 0)),
        dtr=pl.BlockSpec((None, ROW_PAD, CHUNK), lambda g, c: (g, 0, cidx(c))),
        alr=pl.BlockSpec((None, 1, HEAD_PAD), lambda g, c: (g, 0, 0)),
        alc=pl.BlockSpec((None, ROW_PAD, 1), lambda g, c: (g, 0, 0)),
        de=pl.BlockSpec((None, 1, gw), lambda g, c: (g, 0, 0)),
        hp=pl.BlockSpec((None, None, gw, D_STATE), lambda g, c: (cidx(c), g, 0, 0)),
        bc=pl.BlockSpec((CHUNK, D_STATE), lambda g, c: (cidx(c), g)),
        acc=pl.BlockSpec((None, 1, HEAD_PAD), lambda g, c: (g, 0, 0)),
    )


def _ssd_fwd(name, xbc, dtc, dtr, alr, alc, d_e, DI):
    T = xbc.shape[0]
    nc = T // CHUNK
    gw = DI // N_GROUPS
    hpg = gw // HEAD_DIM
    sp = _ssd_specs(T, DI, gw, nc, False)

    def body(xs_ref, b_ref, c_ref, dtc_ref, dtr_ref, alr_ref, alc_ref, de_ref, y_ref, hp_ref, h_scr):
        @pl.when(pl.program_id(1) == 0)
        def _():
            h_scr[...] = jnp.zeros_like(h_scr)

        hpv = h_scr[...]
        hp_ref[...] = hpv
        xs = xs_ref[...]
        bb = b_ref[...].astype(BF16)
        cb_ = c_ref[...].astype(BF16)
        cm = _ssd_common(dtc_ref[...], dtr_ref[...], alr_ref[...], alc_ref[...], gw)
        x = xs * cm["dt_e"]
        xb = x.astype(BF16)
        cbm = _dotb("nt", cb_, bb)
        causal = cm["li"] >= cm["si"]
        lane_head = lax.broadcasted_iota(jnp.int32, (1, gw), 1) // HEAD_DIM
        ydiag = jnp.zeros((CHUNK, gw), F32)
        for j in range(hpg):
            seg = cm["acs_c"][:, j:j + 1] - cm["acs_r"][j:j + 1, :]
            lmat = jnp.exp(jnp.where(causal, seg, -1e30))
            yj = _dotb("nn", cbm * lmat, xb)
            ydiag = jnp.where(lane_head == j, yj, ydiag)
        dec_e = _dotf(cm["dec_h"], cm["expand"])
        states = _dotb("tn", x * dec_e, bb)
        yoff = _dotb("nt", cb_, hpv) * cm["eacs_e"]
        y_ref[...] = ydiag + yoff + xs * de_ref[...]
        cd_rows = jnp.sum(cm["reduce"] * jnp.exp(cm["acs_last"]), axis=1, keepdims=True)
        h_scr[...] = cd_rows * hpv + states

    return pl.pallas_call(
        body, name=name, grid=(N_GROUPS, nc),
        in_specs=[sp["xs"], sp["bm"], sp["cm"], sp["dtc"], sp["dtr"], sp["alr"], sp["alc"], sp["de"]],
        out_specs=[sp["xs"], sp["hp"]],
        out_shape=[jax.ShapeDtypeStruct((T, DI), F32), jax.ShapeDtypeStruct((nc, N_GROUPS, gw, D_STATE), F32)],
        scratch_shapes=[pltpu.VMEM((gw, D_STATE), F32)],
        compiler_params=_cp("parallel", "arbitrary"),
    )(xbc, xbc, xbc, dtc, dtr, alr, alc, d_e)


def _ssd_bwd(name, dy, xbc, dtc, dtr, alr, alc, d_e, hprev, DI):
    T = xbc.shape[0]
    nc = T // CHUNK
    gw = DI // N_GROUPS
    hpg = gw // HEAD_DIM
    sp = _ssd_specs(T, DI, gw, nc, True)

    def body(dy_ref, xs_ref, b_ref, c_ref, dtc_ref, dtr_ref, alr_ref, alc_ref, de_ref, hp_ref,
             dxs_ref, db_ref, dc_ref, ddt_ref, dal_ref, dd_ref, dh_scr):
        first = pl.program_id(1) == 0

        @pl.when(first)
        def _():
            dh_scr[...] = jnp.zeros_like(dh_scr)

        xs = xs_ref[...]
        dyv = dy_ref[...]
        bb = b_ref[...].astype(BF16)
        cb_ = c_ref[...].astype(BF16)
        dtc_v = dtc_ref[...]
        cm = _ssd_common(dtc_v, dtr_ref[...], alr_ref[...], alc_ref[...], gw)
        hpv = hp_ref[...]
        hpb = hpv.astype(BF16)
        dhn = dh_scr[...]
        dhnb = dhn.astype(BF16)
        x = xs * cm["dt_e"]
        xb = x.astype(BF16)
        cbm = _dotb("nt", cb_, bb)
        causal = cm["li"] >= cm["si"]
        lane_head = lax.broadcasted_iota(jnp.int32, (1, gw), 1) // HEAD_DIM
        lane_pad = lax.broadcasted_iota(jnp.int32, (1, HEAD_PAD), 1)
        sub_pad = lax.broadcasted_iota(jnp.int32, (ROW_PAD, 1), 0)

        dxs = dyv * de_ref[...]
        dd_part = jnp.sum(_dotf(dyv * xs, cm["reduce"]), axis=0, keepdims=True)
        _accum(dd_ref, dd_part, first)

        dx = jnp.zeros((CHUNK, gw), F32)
        dcb = jnp.zeros((CHUNK, CHUNK), F32)
        dacs_c = jnp.zeros((CHUNK, HEAD_PAD), F32)
        dacs_r = jnp.zeros((ROW_PAD, CHUNK), F32)
        for j in range(hpg):
            seg = cm["acs_c"][:, j:j + 1] - cm["acs_r"][j:j + 1, :]
            lmat = jnp.exp(jnp.where(causal, seg, -1e30))
            mmat = cbm * lmat
            dym = jnp.where(lane_head == j, dyv, 0.0).astype(BF16)
            dm = _dotb("nt", dym, xb)
            dx = dx + _dotb("tn", mmat, dym)
            dcb = dcb + dm * lmat
            dseg = dm * mmat
            dacs_c = dacs_c + jnp.where(lane_pad == j, jnp.sum(dseg, axis=1, keepdims=True), 0.0)
            dacs_r = dacs_r - jnp.where(sub_pad == j, jnp.sum(dseg, axis=0, keepdims=True), 0.0)
        dc = _dotb("nn", dcb, bb)
        db = _dotb("tn", dcb, cb_)

        gm = _dotb("nt", cb_, hpb)
        dgm = dyv * cm["eacs_e"]
        dacs_c = dacs_c + _dotf(dgm * gm, cm["reduce"])
        dc = dc + _dotb("nn", dgm, hpb)
        dhp = _dotb("tn", dgm, cb_)

        cd_row = jnp.exp(cm["acs_last"])
        cd_rows = jnp.sum(cm["reduce"] * cd_row, axis=1, keepdims=True)
        dhp = dhp + cd_rows * dhn
        rsum = jnp.sum(dhn * hpv, axis=1, keepdims=True)
        dacs_last = jnp.sum(cm["reduce"] * rsum, axis=0, keepdims=True) * cd_row
        dec_e = _dotf(cm["dec_h"], cm["expand"])
        xdec = x * dec_e
        dxdec = _dotb("nt", bb, dhnb)
        db = db + _dotb("nn", xdec, dhnb)
        dx = dx + dxdec * dec_e
        tdec = _dotf(dxdec * x, cm["reduce"]) * cm["dec_h"]
        dacs_c = dacs_c - tdec
        dacs_last = dacs_last + jnp.sum(tdec, axis=0, keepdims=True)
        row_id = lax.broadcasted_iota(jnp.int32, (CHUNK, 1), 0)
        dacs_c = dacs_c + jnp.where(row_id == CHUNK - 1, dacs_last, 0.0)

        dxs_ref[...] = dxs + dx * cm["dt_e"]
        ddt = _dotf(dx * xs, cm["reduce"])
        dda = _dotf(cm["trit"], dacs_c)
        dda_r = _dotf(dacs_r, cm["tri"])
        dda_rp = jnp.concatenate([dda_r, jnp.zeros((HEAD_PAD - ROW_PAD, CHUNK), F32)], axis=0)
        eye = (cm["li"] == cm["si"]).astype(F32)
        dda = dda + lax.dot_general(eye, dda_rp, _DIMS["nt"], precision=HI, preferred_element_type=F32)
        ddt_ref[...] = ddt + dda * cm["a_row"]
        _accum(dal_ref, jnp.sum(dda * dtc_v, axis=0, keepdims=True) * cm["a_row"], first)
        db_ref[...] = db
        dc_ref[...] = dc
        dh_scr[...] = dhp

    gs = D_STATE * N_GROUPS
    return pl.pallas_call(
        body, name=name, grid=(N_GROUPS, nc),
        in_specs=[sp["xs"], sp["xs"], sp["bm"], sp["cm"], sp["dtc"], sp["dtr"], sp["alr"], sp["alc"],
                  sp["de"], sp["hp"]],
        out_specs=[sp["xs"], sp["bc"], sp["bc"], sp["dtc"], sp["acc"], sp["acc"]],
        out_shape=[jax.ShapeDtypeStruct((T, DI), F32), jax.ShapeDtypeStruct((T, gs), F32),
                   jax.ShapeDtypeStruct((T, gs), F32), jax.ShapeDtypeStruct((N_GROUPS, T, HEAD_PAD), F32),
                   jax.ShapeDtypeStruct((N_GROUPS, 1, HEAD_PAD), F32),
                   jax.ShapeDtypeStruct((N_GROUPS, 1, HEAD_PAD), F32)],
        scratch_shapes=[pltpu.VMEM((gw, D_STATE), F32)],
        compiler_params=_cp("parallel", "arbitrary"),
    )(dy, xbc, xbc, xbc, dtc, dtr, alr, alc, d_e, hprev)


def _adam_math(w, g, m, v):
    m = ADAM_B1 * m + (1.0 - ADAM_B1) * g
    v = ADAM_B2 * v + (1.0 - ADAM_B2) * (g * g)
    m_hat = m / (1.0 - ADAM_B1 ** ADAM_STEP)
    v_hat = v / (1.0 - ADAM_B2 ** ADAM_STEP)
    delta = -ADAM_LR * (m_hat / (jnp.sqrt(v_hat) + ADAM_EPS) + ADAM_WD * w)
    return delta, m, v


def _adamw(name, w, m, v, grads):
    L, R, C = w.shape
    tr = _rows(R, C * 4, ADAM_BLOCK_BYTES)
    full = pl.BlockSpec((None, tr, C), lambda l, i: (l, i, 0))
    g_specs = [pl.BlockSpec((tr, C), functools.partial(lambda l, i, ll: (jnp.where(l == ll, i, 0), 0), ll=ll))
               for ll in range(L)]

    def body(w_ref, m_ref, v_ref, *rest):
        g_refs = rest[:L]
        go_ref, d_ref, mo_ref, vo_ref = rest[L:]
        l = pl.program_id(0)
        g = g_refs[0][...]
        for ll in range(1, L):
            g = jnp.where(l == ll, g_refs[ll][...], g)
        delta, mn, vn = _adam_math(w_ref[...], g, m_ref[...], v_ref[...])
        go_ref[...] = g
        d_ref[...] = delta
        mo_ref[...] = mn
        vo_ref[...] = vn

    return pl.pallas_call(
        body, name=name, grid=(L, R // tr), in_specs=[full] * 3 + g_specs, out_specs=[full] * 4,
        out_shape=[jax.ShapeDtypeStruct((L, R, C), F32)] * 4,
        compiler_params=_cp("arbitrary", "arbitrary"),
    )(w, m, v, *grads)


def _sum_peers(name, gathered):
    n, R, C = gathered.shape

    def body(g_ref, o_ref):
        acc = g_ref[0]
        for d in range(1, n):
            acc = acc + g_ref[d]
        o_ref[...] = acc

    return pl.pallas_call(body, name=name, out_shape=jax.ShapeDtypeStruct((R, C), F32))(gathered)


_ANY = pl.BlockSpec(memory_space=pl.ANY)


def _place():
    x, y, c = lax.axis_index("x"), lax.axis_index("y"), lax.axis_index("c")
    chips = [(1 - x, y), (x, 1 - y), (1 - x, 1 - y)]
    return x, y, c, chips


def _allgather_small(name, v):
    R, C = v.shape

    def body(v_ref, o_ref, send_sems, recv_sems):
        x, y, c, _ = _place()
        me = 4 * x + 2 * y + c
        o_ref[me] = v_ref[...]
        copies = []
        for k in range(1, N_DEV):
            px, py, pc = x ^ (k >> 2), y ^ ((k >> 1) & 1), c ^ (k & 1)
            copies.append(pltpu.make_async_remote_copy(
                src_ref=v_ref, dst_ref=o_ref.at[me], send_sem=send_sems.at[k - 1], recv_sem=recv_sems.at[k - 1],
                device_id=(px, py, pc), device_id_type=MESH))
        for cp in copies:
            cp.start()
        for cp in copies:
            cp.wait()

    return pl.pallas_call(
        body, name=name, out_shape=jax.ShapeDtypeStruct((N_DEV, R, C), F32),
        in_specs=[pl.BlockSpec(memory_space=pltpu.VMEM)], out_specs=pl.BlockSpec(memory_space=pltpu.VMEM),
        scratch_shapes=[pltpu.SemaphoreType.DMA((N_DEV - 1,)), pltpu.SemaphoreType.DMA((N_DEV - 1,))],
    )(v)


_HBM = pl.BlockSpec(memory_space=pltpu.HBM)
_SEM = pl.BlockSpec(memory_space=pltpu.SEMAPHORE)
_VMEM = pl.BlockSpec(memory_space=pltpu.VMEM)
_EFFECT = pltpu.SideEffectType.DATAFLOW_SIDE_EFFECTING


def _in_hbm(v):
    return pltpu.with_memory_space_constraint(v, pltpu.HBM)


def _remote(src, dst, send_sem, recv_sem, device):
    return pltpu.make_async_remote_copy(src_ref=src, dst_ref=dst, send_sem=send_sem, recv_sem=recv_sem,
                                        device_id=device, device_id_type=MESH)


def _gather_copies(src, land, ici_s, ici_r, own_s, own_r, arrivals=True):
    x, y, c, chips = _place()
    a = 2 * x + y
    sends, lands = [], []
    for i in range(len(src)):
        own = _remote(src[i], land[i].at[a], own_s.at[i], own_r.at[i], (x, y, 1 - c))
        sends.append(own)
        if arrivals:
            lands.append(own)
        for k, (px, py) in enumerate(chips):
            s, r = ici_s.at[3 * i + k], ici_r.at[3 * i + k]
            sends.append(_remote(src[i].at[c], land[i].at[a, c], s, r, (px, py, c)))
            if arrivals:
                lands.append(_remote(src[i].at[c], land[i].at[2 * px + py, c], s, r, (px, py, c)))
    return sends, lands


def _gather_start(name, shards, after=None):
    n = len(shards)
    lands = [lax.empty((N_CHIPS,) + s.shape, s.dtype) for s in shards]
    n_in = 2 * n + (0 if after is None else 1)

    def body(*refs):
        src, land = refs[:n], refs[n:2 * n]
        sems = refs[n_in:n_in + 4]
        token = refs[-1]
        sends, _ = _gather_copies(src, land, *sems, arrivals=False)
        for cp in sends:
            cp.start()
        token[...] = jnp.zeros_like(token)

    outs = pl.pallas_call(
        body, name=name,
        out_shape=(pltpu.SemaphoreType.DMA((3 * n,)), pltpu.SemaphoreType.DMA((3 * n,)),
                   pltpu.SemaphoreType.DMA((n,)), pltpu.SemaphoreType.DMA((n,)),
                   *[pltpu.HBM(s.shape, s.dtype) for s in shards], *[pltpu.HBM(l.shape, l.dtype) for l in lands],
                   jax.ShapeDtypeStruct((8, 128), F32)),
        in_specs=[_HBM] * (2 * n) + ([] if after is None else [_ANY]),
        out_specs=(_SEM,) * 4 + (_HBM,) * (2 * n) + (_VMEM,),
        input_output_aliases={i: 4 + i for i in range(2 * n)},
        compiler_params=pltpu.CompilerParams(has_side_effects=_EFFECT),
    )(*[_in_hbm(s) for s in shards], *[_in_hbm(l) for l in lands], *([] if after is None else [after]))
    return dict(sems=outs[:4], src=outs[4:4 + n], land=outs[4 + n:4 + 2 * n], token=outs[-1])


def _gather_wait(name, started, after):
    n = len(started["src"])

    def body(*refs):
        src, land = refs[:n], refs[n:2 * n]
        sems = refs[2 * n:2 * n + 4]
        sends, lands = _gather_copies(src, land, *sems)
        for cp in sends:
            cp.wait_send()
        for cp in lands:
            cp.wait_recv()

    outs = pl.pallas_call(
        body, name=name,
        out_shape=[pltpu.HBM(v.shape, v.dtype) for v in list(started["src"]) + list(started["land"])],
        in_specs=[_HBM] * (2 * n) + [_SEM] * 4 + [_ANY], out_specs=[_HBM] * (2 * n),
        input_output_aliases={i: i for i in range(2 * n)},
        compiler_params=pltpu.CompilerParams(has_side_effects=_EFFECT),
    )(*started["src"], *started["land"], *started["sems"], after)
    return outs[n:]


def _gather_forward(name, lands):
    n = len(lands)

    def body(*refs):
        buf = refs[n:2 * n]
        send_sems, recv_sems = refs[2 * n:]
        x, y, c, chips = _place()
        sends, lands_ = [], []
        for i in range(n):
            for k, (px, py) in enumerate(chips):
                b = 2 * px + py
                sends.append(_remote(buf[i].at[b, c], buf[i].at[b, c], send_sems.at[i, k], recv_sems.at[i, k],
                                     (x, y, 1 - c)))
                lands_.append(_remote(buf[i].at[b, 1 - c], buf[i].at[b, 1 - c], send_sems.at[i, k],
                                      recv_sems.at[i, k], (x, y, 1 - c)))
        for cp in sends:
            cp.start()
        for cp in sends:
            cp.wait_send()
        for cp in lands_:
            cp.wait_recv()

    return pl.pallas_call(
        body, name=name, in_specs=[_ANY] * n, out_specs=[_ANY] * n,
        out_shape=[jax.ShapeDtypeStruct(l.shape, l.dtype) for l in lands],
        input_output_aliases={i: i for i in range(n)},
        scratch_shapes=[pltpu.SemaphoreType.DMA((n, 3)), pltpu.SemaphoreType.DMA((n, 3))],
    )(*lands)


def _pair_exchange(name, parts):
    n = len(parts)

    def body(*refs):
        src = refs[:n]
        dst = refs[n:2 * n]
        send_sems, recv_sems = refs[2 * n:]
        x, y, c, _ = _place()
        copies = [pltpu.make_async_remote_copy(
            src_ref=src[i].at[:, 1 - c], dst_ref=dst[i], send_sem=send_sems.at[i], recv_sem=recv_sems.at[i],
            device_id=(x, y, 1 - c), device_id_type=MESH) for i in range(n)]
        for cp in copies:
            cp.start()
        for cp in copies:
            cp.wait()

    return pl.pallas_call(
        body, name=name, in_specs=[_ANY] * n, out_specs=[_ANY] * n,
        out_shape=[jax.ShapeDtypeStruct((p.shape[0],) + p.shape[2:], p.dtype) for p in parts],
        scratch_shapes=[pltpu.SemaphoreType.DMA((n,)), pltpu.SemaphoreType.DMA((n,))],
    )(*parts)


def _chip_copies(src, land, send_sems, recv_sems):
    x, y, c, chips = _place()
    return [_remote(src[i].at[2 * px + py], land[i].at[k], send_sems.at[3 * i + k], recv_sems.at[3 * i + k],
                    (px, py, c))
            for i in range(len(src)) for k, (px, py) in enumerate(chips)]


def _chip_start(name, parts):
    n = len(parts)
    lands = [lax.empty((3,) + p.shape[1:], p.dtype) for p in parts]

    def body(*refs):
        src, land = refs[:n], refs[n:2 * n]
        send_sems, recv_sems = refs[2 * n:2 * n + 2]
        token = refs[-1]
        for cp in _chip_copies(src, land, send_sems, recv_sems):
            cp.start()
        token[...] = jnp.zeros_like(token)

    outs = pl.pallas_call(
        body, name=name,
        out_shape=(pltpu.SemaphoreType.DMA((3 * n,)), pltpu.SemaphoreType.DMA((3 * n,)),
                   *[pltpu.HBM(p.shape, p.dtype) for p in parts], *[pltpu.HBM(l.shape, l.dtype) for l in lands],
                   jax.ShapeDtypeStruct((8, 128), F32)),
        in_specs=[_HBM] * (2 * n), out_specs=(_SEM,) * 2 + (_HBM,) * (2 * n) + (_VMEM,),
        input_output_aliases={i: 2 + i for i in range(2 * n)},
        compiler_params=pltpu.CompilerParams(has_side_effects=_EFFECT),
    )(*[_in_hbm(p) for p in parts], *[_in_hbm(l) for l in lands])
    return dict(sems=outs[:2], src=outs[2:2 + n], land=outs[2 + n:2 + 2 * n], token=outs[-1])


def _chip_wait(name, started, after):
    n = len(started["src"])

    def body(*refs):
        src, land = refs[:n], refs[n:2 * n]
        send_sems, recv_sems = refs[2 * n:2 * n + 2]
        copies = _chip_copies(src, land, send_sems, recv_sems)
        for cp in copies:
            cp.wait_send()
        for cp in copies:
            cp.wait_recv()

    outs = pl.pallas_call(
        body, name=name,
        out_shape=[pltpu.HBM(v.shape, v.dtype) for v in list(started["src"]) + list(started["land"])],
        in_specs=[_HBM] * (2 * n) + [_SEM] * 2 + [_ANY], out_specs=[_HBM] * (2 * n),
        input_output_aliases={i: i for i in range(2 * n)},
        compiler_params=pltpu.CompilerParams(has_side_effects=_EFFECT),
    )(*started["src"], *started["land"], *started["sems"], after)
    return outs[:n], outs[n:]


def _half_exchange(name, shards):
    n = len(shards)

    def body(*refs):
        buf = refs[n:2 * n]
        send_sems, recv_sems = refs[2 * n:]
        x, y, c, _ = _place()
        copies = [pltpu.make_async_remote_copy(
            src_ref=buf[i].at[c], dst_ref=buf[i].at[c], send_sem=send_sems.at[i], recv_sem=recv_sems.at[i],
            device_id=(x, y, 1 - c), device_id_type=MESH) for i in range(n)]
        for cp in copies:
            cp.start()
        for cp in copies:
            cp.wait()

    return pl.pallas_call(
        body, name=name, in_specs=[_ANY] * n, out_specs=[_ANY] * n,
        out_shape=[jax.ShapeDtypeStruct(s.shape, s.dtype) for s in shards],
        input_output_aliases={i: i for i in range(n)},
        scratch_shapes=[pltpu.SemaphoreType.DMA((n,)), pltpu.SemaphoreType.DMA((n,))],
    )(*shards)


def _pair_add(name, pos, part, sib):
    Q, _, R2, C = part.shape
    tr = _pick(R2, 256)

    def body(pos_ref, p_ref, s_ref, o_ref):
        o_ref[...] = (p_ref[...].astype(F32) + s_ref[...].astype(F32)).astype(BF16)

    return pl.pallas_call(
        body, name=name,
        grid_spec=pltpu.PrefetchScalarGridSpec(
            num_scalar_prefetch=1, grid=(Q, R2 // tr),
            in_specs=[pl.BlockSpec((None, None, tr, C), lambda q, i, pos_ref: (q, pos_ref[1], i, 0)),
                      pl.BlockSpec((None, tr, C), lambda q, i, pos_ref: (q, i, 0))],
            out_specs=pl.BlockSpec((None, tr, C), lambda q, i, pos_ref: (q, i, 0))),
        out_shape=jax.ShapeDtypeStruct((Q, R2, C), BF16),
        compiler_params=_cp("parallel", "parallel"),
    )(pos, part, sib)


def _chip_sum(name, pos, own, got):
    _, R2, C = own.shape
    tr = _pick(R2, 256)

    def body(pos_ref, o_ref, g_ref, out_ref):
        acc = o_ref[...].astype(F32)
        for k in range(3):
            acc = acc + g_ref[k].astype(F32)
        out_ref[...] = acc

    return pl.pallas_call(
        body, name=name,
        grid_spec=pltpu.PrefetchScalarGridSpec(
            num_scalar_prefetch=1, grid=(R2 // tr,),
            in_specs=[pl.BlockSpec((None, tr, C), lambda i, pos_ref: (pos_ref[0], i, 0)),
                      pl.BlockSpec((3, tr, C), lambda i, pos_ref: (0, i, 0))],
            out_specs=pl.BlockSpec((None, tr, C), lambda i, pos_ref: (pos_ref[1], i, 0))),
        out_shape=jax.ShapeDtypeStruct((2, R2, C), F32),
        compiler_params=_cp("parallel"),
    )(pos, own, got)


def _reduce_begin(tag, pos, parts):
    split = [p.reshape(p.shape[0], 2, p.shape[1] // 2, p.shape[2]) for p in parts]
    sib = _pair_exchange(tag + "_pair", split)
    chip = [_pair_add(f"{tag}_add{i}", pos, split[i], sib[i]) for i in range(len(parts))]
    return _chip_start(tag + "_chip_start", chip)


def _reduce_end(tag, pos, started, after, shapes):
    chip, got = _chip_wait(tag + "_chip_wait", started, after)
    halves = [_chip_sum(f"{tag}_sum{i}", pos, chip[i], got[i]) for i in range(len(chip))]
    full = _half_exchange(tag + "_half", halves)
    return [f.reshape(shp) for f, shp in zip(full, shapes)]


def _pad_to(v, mult):
    n = v.shape[0]
    return jnp.pad(v, (0, (-n) % mult))


def kernel(x, p, pool_w, pool_scale, ssm_in_w, ssm_conv_w, ssm_conv_b, ssm_dt_bias, ssm_a_log, ssm_d, ssm_norm_w, ssm_out_w, mlp_w1, mlp_w2, ln_g, ln_b, ple_w, ple_gate_w, loss_target, m_pool_w, m_pool_scale, m_ssm_in_w, m_ssm_conv_w, m_ssm_conv_b, m_ssm_dt_bias, m_ssm_a_log, m_ssm_d, m_ssm_norm_w, m_ssm_out_w, m_mlp_w1, m_mlp_w2, m_ln_g, m_ln_b, m_ple_w, m_ple_gate_w, v_pool_w, v_pool_scale, v_ssm_in_w, v_ssm_conv_w, v_ssm_conv_b, v_ssm_dt_bias, v_ssm_a_log, v_ssm_d, v_ssm_norm_w, v_ssm_out_w, v_mlp_w1, v_mlp_w2, v_ln_g, v_ln_b, v_ple_w, v_ple_gate_w):
    T, D = x.shape[1], x.shape[2]
    NG = len(POOL_WINDOWS)
    GD = D // NG
    DI = ssm_out_w.shape[1] * N_CHIPS
    H = ssm_dt_bias.shape[1]
    HPG = H // N_GROUPS
    GW = DI // N_GROUPS
    GS = N_GROUPS * D_STATE
    CD = DI + 2 * GS
    DF = mlp_w1.shape[2] * N_CHIPS
    PD = ple_w.shape[1]
    NIN = ssm_in_w.shape[2]

    xi, yi, ci = lax.axis_index("x"), lax.axis_index("y"), lax.axis_index("c")
    chip = 2 * xi + yi
    pos = jnp.stack([chip, ci]).astype(jnp.int32)

    x0 = x[0]
    tgt = loss_target[0]
    p0b, p1b = p[0, 0].astype(BF16), p[1, 0].astype(BF16)

    def halves(w):
        return w.astype(BF16).reshape((2, w.shape[0] // 2) + w.shape[1:])

    sh_pool = pool_w[0].astype(BF16)
    sh_pool = sh_pool.reshape((2, NG // 2) + sh_pool.shape[1:])
    ag0 = _gather_start("ag0_start", [sh_pool, halves(mlp_w1[0]), halves(mlp_w2[0]), halves(ple_w[0]),
                                      halves(ple_gate_w[0])])
    g_pool, *g_l0 = _gather_forward("ag0_fwd", _gather_wait("ag0_wait", ag0, ag0["token"]))
    ag1 = _gather_start("ag1_start", [halves(ssm_in_w[0]), halves(ssm_out_w[0]), halves(mlp_w1[1]),
                                      halves(mlp_w2[1]), halves(ple_w[1]), halves(ple_gate_w[1])], after=g_pool)

    def by_layer(g4):
        return (g4[0].reshape(N_CHIPS, 1, D, DF // N_CHIPS), g4[1].reshape(N_CHIPS, 1, DF // N_CHIPS, D),
                g4[2].reshape(N_CHIPS, 1, PD, D // N_CHIPS), g4[3].reshape(N_CHIPS, 1, D // N_CHIPS, D))

    g_w1, g_w2, g_pw, g_gw = ({0: t} for t in by_layer(g_l0))
    w_pool = jnp.transpose(g_pool.reshape(N_CHIPS, NG, GD // N_CHIPS, GD), (1, 0, 2, 3)).reshape(NG, GD, GD)

    small_sh = jnp.concatenate([ssm_conv_w[0].reshape(-1), ssm_conv_b[0], ssm_norm_w[0],
                                ln_g.reshape(-1), ln_b.reshape(-1)])
    n_sh = small_sh.shape[0]
    small_all = _allgather_small("gather_small", _pad_to(small_sh, 1024).reshape(-1, 128))
    small_all = small_all.reshape(N_DEV, -1)[0::2, :n_sh]
    cdq, niq, dq = CD // N_CHIPS, DI // N_CHIPS, D // N_CHIPS
    o = 0
    conv_w = jnp.concatenate([small_all[q, o:o + CONV_K * cdq].reshape(CONV_K, cdq) for q in range(N_CHIPS)], axis=1)
    o += CONV_K * cdq
    conv_b = small_all[:, o:o + cdq].reshape(1, CD)
    o += cdq
    norm_w = small_all[:, o:o + niq].reshape(1, DI)
    o += niq
    lng = jnp.transpose(small_all[:, o:o + 4 * dq].reshape(N_CHIPS, 2, 2, dq), (1, 2, 0, 3)).reshape(2, 2, 1, D)
    o += 4 * dq
    lnb = jnp.transpose(small_all[:, o:o + 4 * dq].reshape(N_CHIPS, 2, 2, dq), (1, 2, 0, 3)).reshape(2, 2, 1, D)

    pooled = _pool_fwd("pool_fwd", x0)
    hraw = _mm("pool_mm", "nn", pooled, w_pool, T, D, GD, tn=GD, deps=[ag1["token"]],
               a_spec=lambda tm, tn, tk: pl.BlockSpec((tm, tk), lambda i, j, k: (i, j)),
               b_spec=lambda tm, tn, tk: pl.BlockSpec((None, tk, tn), lambda i, j, k: (j, 0, 0)))
    x1, x1b, xh1, rs1 = _res_ln("ln00", x0, hraw, lng[0, 0], lnb[0, 0], scale=pool_scale)

    def mlp_fwd(l, xb):
        a, h2b = _mm(f"mlp{l}_up", "nn", xb, g_w1[l], T, DF, D, tn=min(1024, DF // N_CHIPS),
                     b_spec=_colshard_b(0, DF // N_CHIPS), out_dtype=(F32, BF16),
                     epi=lambda acc: (acc, jnp.square(jnp.maximum(acc, 0.0))))
        h = _mm(f"mlp{l}_down", "nn", h2b, g_w2[l], T, D, DF, k_unit=DF // N_CHIPS,
                b_spec=_rowshard_b(0, DF // N_CHIPS))
        return a, h2b, h

    def ple_fwd(l, xb, pb):
        gl = _mm(f"gate{l}_logit", "nn", xb, g_gw[l], T, D, D, k_unit=D // N_CHIPS,
                 b_spec=_rowshard_b(0, D // N_CHIPS))
        e = _mm(f"gate{l}_emb", "nn", pb, g_pw[l], T, D, PD, tn=min(1024, D // N_CHIPS),
                b_spec=_colshard_b(0, D // N_CHIPS))
        return gl, e

    a0, h2b0, h0 = mlp_fwd(0, x1b)
    x2, x2b, xh2, rs2 = _res_ln("ln01", x1, h0, lng[0, 1], lnb[0, 1])
    gl0, e0 = ple_fwd(0, x2b, p0b)
    x3, x3b, gate0 = _gate_fwd("gate0", x2, gl0, e0)

    g_in, g_out, *g_l1 = _gather_forward("ag1_fwd", _gather_wait("ag1_wait", ag1, x3b))
    for dst, t in zip((g_w1, g_w2, g_pw, g_gw), by_layer(g_l1)):
        dst[1] = t
    w_in = jnp.concatenate([g_in[q].reshape(D, NIN) for q in range(N_CHIPS)], axis=1)
    w_z, w_xbc, w_dt = w_in[:, :DI], w_in[:, DI:DI + CD], w_in[:, DI + CD:]
    w_out = g_out.reshape(DI, D)
    z = _mm("ssm_in_z", "nn", x3b, w_z, T, DI, D)
    xbc_pre = _mm("ssm_in_xbc", "nn", x3b, w_xbc, T, CD, D)
    dt_pre = _mm("ssm_in_dt", "nn", x3b, w_dt, T, H, D)
    xbc = _conv_fwd("conv_fwd", xbc_pre, conv_w, conv_b)
    dt = _dt_fwd("dt_fwd", dt_pre, ssm_dt_bias)
    dt_g = jnp.transpose(dt.reshape(T, N_GROUPS, HPG), (1, 0, 2))
    dtc = jnp.pad(dt_g, ((0, 0), (0, 0), (0, HEAD_PAD - HPG)))
    dtr = jnp.pad(jnp.transpose(dt_g, (0, 2, 1)), ((0, 0), (0, ROW_PAD - HPG), (0, 0)))
    al_g = ssm_a_log.reshape(N_GROUPS, HPG)
    alr = jnp.pad(al_g, ((0, 0), (0, HEAD_PAD - HPG)))[:, None, :]
    alc = jnp.pad(al_g, ((0, 0), (0, ROW_PAD - HPG)))[:, :, None]
    d_e = jnp.repeat(ssm_d.reshape(N_GROUPS, HPG), HEAD_DIM, axis=1)[:, None, :]
    ysc, hprev = _ssd_fwd("ssd_fwd", xbc, dtc, dtr, alr, alc, d_e, DI)
    ynb, rsn = _gnorm_fwd("gnorm_fwd", ysc, z, norm_w)
    h1 = _mm("ssm_out", "nn", ynb, w_out, T, D, DI)
    x4, x4b, xh4, rs4 = _res_ln("ln10", x3, h1, lng[1, 0], lnb[1, 0])
    a1, h2b1, h2 = mlp_fwd(1, x4b)
    x5, x5b, xh5, rs5 = _res_ln("ln11", x4, h2, lng[1, 1], lnb[1, 1])
    gl1, e1 = ple_fwd(1, x5b, p1b)
    dx6, gate1, loss_parts = _gate_loss("gate1_loss", x5, gl1, e1, tgt)
    loss_local = jnp.sum(loss_parts[0::8, 0])

    def ple_bwd(l, dxo, gate, e, xb, pb, deps=()):
        dgl, de = _gate_bwd(f"gate{l}_bwd", dxo, gate, e)
        d_gw = _mm(f"gate{l}_dw", "tn", xb, dgl, D, D, T, out_dtype=BF16, deps=deps).reshape(N_CHIPS, D // N_CHIPS, D)
        d_pw = _mm(f"gate{l}_dpw", "tn", pb, de, PD, D, T, out_dtype=BF16, tn=min(1024, D // N_CHIPS),
                   o_shape=(N_CHIPS, PD, D // N_CHIPS), o_spec=_colshard_o(D // N_CHIPS))
        dx = _mm(f"gate{l}_dx", "nt", dgl, g_gw[l], T, D, D, tn=min(1024, D // N_CHIPS),
                 b_spec=_rowshard_bt(0, D // N_CHIPS), epi=lambda acc, r: acc + r, extras=[dxo])
        return dx, d_gw, d_pw

    def mlp_bwd(l, du, dub, a, h2b, xb):
        d_w2 = _mm(f"mlp{l}_dw2", "tn", h2b, dub, DF, D, T, out_dtype=BF16).reshape(N_CHIPS, DF // N_CHIPS, D)
        da = _mm(f"mlp{l}_da", "nt", dub, g_w2[l], T, DF, D, tn=min(1024, DF // N_CHIPS),
                 b_spec=_rowshard_bt(0, DF // N_CHIPS), out_dtype=BF16,
                 epi=lambda acc, av: acc * (2.0 * jnp.maximum(av, 0.0)), extras=[a])
        d_w1 = _mm(f"mlp{l}_dw1", "tn", xb, da, D, DF, T, out_dtype=BF16, tn=min(1024, DF // N_CHIPS),
                   o_shape=(N_CHIPS, D, DF // N_CHIPS), o_spec=_colshard_o(DF // N_CHIPS))
        dx = _mm(f"mlp{l}_dx", "nt", da, g_w1[l], T, D, DF, k_unit=DF // N_CHIPS,
                 b_spec=_colshard_bt(0, DF // N_CHIPS), epi=lambda acc, r: acc + ALPHA * r, extras=[du])
        return dx, d_w1, d_w2

    dx5, d_gw1, d_pw1 = ple_bwd(1, dx6, gate1, e1, x5b, p1b)
    du5, du5b, dg11, db11 = _ln_bwd("ln11_bwd", dx5, xh5, rs5, lng[1, 1])
    dx4, d_w1_1, d_w2_1 = mlp_bwd(1, du5, du5b, a1, h2b1, x4b)
    du4, du4b, dg10, db10 = _ln_bwd("ln10_bwd", dx4, xh4, rs4, lng[1, 0])
    d_wout = _mm("ssm_out_dw", "tn", ynb, du4b, DI, D, T, out_dtype=BF16).reshape(N_CHIPS, DI // N_CHIPS, D)
    dyn = _mm("ssm_out_dx", "nt", du4b, w_out, T, DI, D)
    dysc, dzb, dnorm_w = _gnorm_bwd("gnorm_bwd", dyn, ysc, z, norm_w, rsn)
    dxs, dbm, dcm, ddt_g, dalog_g, dd_g = _ssd_bwd("ssd_bwd", dysc, xbc, dtc, dtr, alr, alc, d_e, hprev, DI)
    dconv, dconv_w, dconv_b = _conv_bwd_a("conv_bwd_a", dxs, dbm, dcm, xbc_pre, conv_w, conv_b)
    dxbcb = _conv_bwd_b("conv_bwd_b", dconv, conv_w)
    ddt = jnp.transpose(ddt_g[:, :, :HPG], (1, 0, 2)).reshape(T, H)
    _, ddtpb, ddt_bias = _dt_bwd("dt_bwd", ddt, dt_pre, ssm_dt_bias)
    da_log = dalog_g[:, 0, :HPG].reshape(1, H)
    dd_skip = dd_g[:, 0, :HPG].reshape(1, H)
    d_wz = _mm("ssm_in_dwz", "tn", x3b, dzb, D, DI, T, out_dtype=BF16)
    d_wx = _mm("ssm_in_dwx", "tn", x3b, dxbcb, D, CD, T, out_dtype=BF16)
    d_wdt = _mm("ssm_in_dwdt", "tn", x3b, ddtpb, D, H, T, out_dtype=BF16)
    d_win = jnp.transpose(jnp.concatenate([d_wz, d_wx, d_wdt], axis=1).reshape(D, N_CHIPS, NIN), (1, 0, 2))
    dx3 = _mm("ssm_in_dxz", "nt", dzb, w_z, T, D, DI, epi=lambda acc, r: acc + ALPHA * r, extras=[du4])
    dx3 = _mm("ssm_in_dxx", "nt", dxbcb, w_xbc, T, D, CD, epi=lambda acc, r: acc + r, extras=[dx3])
    dx3 = _mm("ssm_in_dxdt", "nt", ddtpb, w_dt, T, D, H, epi=lambda acc, r: acc + r, extras=[dx3])

    parts1 = [d_win, d_wout, d_w1_1, d_w2_1, d_gw1, d_pw1]
    red1 = _reduce_begin("rs1", pos, parts1)

    dx2, d_gw0, d_pw0 = ple_bwd(0, dx3, gate0, e0, x2b, p0b, deps=[red1["token"]])
    du2, du2b, dg01, db01 = _ln_bwd("ln01_bwd", dx2, xh2, rs2, lng[0, 1])
    dx1, d_w1_0, d_w2_0 = mlp_bwd(0, du2, du2b, a0, h2b0, x1b)
    du1, dhrb, dg00, db00, dscale = _ln_bwd("ln00_bwd", dx1, xh1, rs1, lng[0, 0], hraw=hraw, scale=pool_scale)
    d_wpool = _mm("pool_dw", "tn", pooled, dhrb, D, GD, T, tm=GD, tn=GD, out_dtype=BF16,
                  b_spec=lambda tm, tn, tk: pl.BlockSpec((tk, tn), lambda i, j, k: (k, i)),
                  o_shape=(NG, GD, GD), o_spec=lambda tm, tn, tk: pl.BlockSpec((None, tm, tn), lambda i, j, k: (i, 0, 0)))
    dpooled = _mm("pool_dx", "nt", dhrb, w_pool, T, D, GD, tn=GD,
                  a_spec=lambda tm, tn, tk: pl.BlockSpec((tm, tk), lambda i, j, k: (i, j)),
                  b_spec=lambda tm, tn, tk: pl.BlockSpec((None, tn, tk), lambda i, j, k: (j, 0, 0)))
    grad_x = _pool_bwd("pool_bwd", dpooled, du1)
    d_wpool = jnp.transpose(d_wpool.reshape(NG, N_CHIPS, GD // N_CHIPS, GD), (1, 0, 2, 3)).reshape(N_CHIPS, NG * GD // N_CHIPS, GD)

    parts0 = [d_wpool, d_w1_0, d_w2_0, d_gw0, d_pw0]
    red0 = _reduce_begin("rs0", pos, parts0)
    r_in, r_out, r_w1_1, r_w2_1, r_gw1, r_pw1 = _reduce_end("rs1", pos, red1, red0["token"], [p.shape[1:] for p in parts1])

    dln_g = jnp.stack([jnp.stack([dg00, dg01]), jnp.stack([dg10, dg11])]).reshape(-1)
    dln_b = jnp.stack([jnp.stack([db00, db01]), jnp.stack([db10, db11])]).reshape(-1)
    small_g = jnp.concatenate([dscale.reshape(-1), dconv_w.reshape(-1), dconv_b.reshape(-1), ddt_bias.reshape(-1),
                               da_log.reshape(-1), dd_skip.reshape(-1), dnorm_w.reshape(-1), dln_g, dln_b,
                               loss_local.reshape(1)])
    n_sg = small_g.shape[0]
    sg_all = _allgather_small("gather_small_grads", _pad_to(small_g, 1024).reshape(-1, 128))
    sg = _sum_peers("sum_small_grads", sg_all).reshape(-1)[:n_sg]
    o = 0

    def take(nel):
        nonlocal o
        v = sg[o:o + nel]
        o += nel
        return v

    g_scale = take(D).reshape(1, D)
    g_conv_w_full = take(CONV_K * CD).reshape(CONV_K, CD)
    g_conv_b_full = take(CD).reshape(1, CD)
    g_dt_bias = take(H).reshape(1, H)
    g_a_log = take(H).reshape(1, H)
    g_d = take(H).reshape(1, H)
    g_norm_full = take(DI).reshape(1, DI)
    g_lng_full = take(4 * D).reshape(2, 2, D)
    g_lnb_full = take(4 * D).reshape(2, 2, D)
    loss = take(1).reshape(())
    g_conv_w = lax.dynamic_slice_in_dim(g_conv_w_full, chip * cdq, cdq, axis=1)[None]
    g_conv_b = lax.dynamic_slice_in_dim(g_conv_b_full, chip * cdq, cdq, axis=1)
    g_norm = lax.dynamic_slice_in_dim(g_norm_full, chip * niq, niq, axis=1)
    g_lng = lax.dynamic_slice_in_dim(g_lng_full, chip * dq, dq, axis=2)
    g_lnb = lax.dynamic_slice_in_dim(g_lnb_full, chip * dq, dq, axis=2)

    def adam(name, w, m, v, grads):
        shp = w.shape
        L = len(grads)
        w3, m3, v3 = (t.reshape((L, -1, shp[-1])) for t in (w, m, v))
        return [t.reshape(shp) for t in _adamw(name, w3, m3, v3, grads)]

    big = {
        "ssm_in_w": adam("adam_ssm_in_w", ssm_in_w, m_ssm_in_w, v_ssm_in_w, [r_in]),
        "ssm_out_w": adam("adam_ssm_out_w", ssm_out_w, m_ssm_out_w, v_ssm_out_w, [r_out]),
    }
    r_pool, r_w1_0, r_w2_0, r_gw0, r_pw0 = _reduce_end("rs0", pos, red0, big["ssm_in_w"][1],
                                                       [p.shape[1:] for p in parts0])
    big.update({
        "pool_w": adam("adam_pool_w", pool_w, m_pool_w, v_pool_w, [r_pool]),
        "mlp_w1": adam("adam_mlp_w1", mlp_w1, m_mlp_w1, v_mlp_w1, [r_w1_0, r_w1_1]),
        "mlp_w2": adam("adam_mlp_w2", mlp_w2, m_mlp_w2, v_mlp_w2, [r_w2_0, r_w2_1]),
        "ple_w": adam("adam_ple_w", ple_w, m_ple_w, v_ple_w, [r_pw0, r_pw1]),
        "ple_gate_w": adam("adam_ple_gate_w", ple_gate_w, m_ple_gate_w, v_ple_gate_w, [r_gw0, r_gw1]),
    })

    small = [("pool_scale", pool_scale, m_pool_scale, v_pool_scale, g_scale),
             ("ssm_conv_w", ssm_conv_w, m_ssm_conv_w, v_ssm_conv_w, g_conv_w),
             ("ssm_conv_b", ssm_conv_b, m_ssm_conv_b, v_ssm_conv_b, g_conv_b),
             ("ssm_dt_bias", ssm_dt_bias, m_ssm_dt_bias, v_ssm_dt_bias, g_dt_bias),
             ("ssm_a_log", ssm_a_log, m_ssm_a_log, v_ssm_a_log, g_a_log),
             ("ssm_d", ssm_d, m_ssm_d, v_ssm_d, g_d),
             ("ssm_norm_w", ssm_norm_w, m_ssm_norm_w, v_ssm_norm_w, g_norm),
             ("ln_g", ln_g, m_ln_g, v_ln_g, g_lng),
             ("ln_b", ln_b, m_ln_b, v_ln_b, g_lnb)]

    def pack(idx):
        flat = _pad_to(jnp.concatenate([s[idx].reshape(-1) for s in small]), 1024)
        return flat.reshape(1, -1, 128)

    sm_out = _adamw("adam_small", pack(1), pack(2), pack(3), [pack(4)[0]])
    small_res = {}
    o = 0
    for s in small:
        nel = s[1].size
        small_res[s[0]] = [t.reshape(-1)[o:o + nel].reshape(s[1].shape) for t in sm_out]
        o += nel

    order = ["pool_w", "pool_scale", "ssm_in_w", "ssm_conv_w", "ssm_conv_b", "ssm_dt_bias", "ssm_a_log", "ssm_d",
             "ssm_norm_w", "ssm_out_w", "mlp_w1", "mlp_w2", "ln_g", "ln_b", "ple_w", "ple_gate_w"]
    res = {**big, **small_res}
    outs = [loss, grad_x[None]]
    for kind in range(4):
        outs += [res[nm][kind] for nm in order]
    return tuple(outs)
```

```python
import functools

import jax
import jax.numpy as jnp
from jax import lax
from jax.experimental import pallas as pl
from jax.experimental.pallas import tpu as pltpu

F32 = jnp.float32
BF16 = jnp.bfloat16
HI = lax.Precision.HIGHEST
MESH = pl.DeviceIdType.MESH

DEPTH = 2
ALPHA = (2.0 * DEPTH) ** 0.25
LN_EPS = 1e-5
RMS_EPS = 1e-5
POOL_WINDOWS = (2, 4, 8, 16)
POOL_HALO = 16
HEAD_DIM = 64
N_GROUPS = 8
D_STATE = 128
CHUNK = 128
CONV_K = 4
CONV_HALO = 8
HEAD_PAD = 128
ROW_PAD = 8
N_CHIPS = 4
N_DEV = 8
ADAM_LR = 0.001
ADAM_B1 = 0.9
ADAM_B2 = 0.999
ADAM_EPS = 1e-08
ADAM_WD = 0.01
ADAM_STEP = 10
VMEM_LIMIT = 56 * 1024 * 1024
ADAM_BLOCK_BYTES = 1024 * 1024
MM_VMEM_BUDGET = 40 * 1024 * 1024


def _cp(*sem):
    return pltpu.CompilerParams(dimension_semantics=sem, vmem_limit_bytes=VMEM_LIMIT)


def _pick(dim, pref):
    t = pref
    while t >= 128:
        if dim % t == 0:
            return t
        t //= 2
    return dim


def _rows(rows, row_bytes, budget):
    t = rows
    while t * row_bytes > budget and t % 16 == 0:
        t //= 2
    return t


def _sigmoid(v):
    return 1.0 / (1.0 + jnp.exp(-v))


_DIMS = {"nn": (((1,), (0,)), ((), ())), "nt": (((1,), (1,)), ((), ())), "tn": (((0,), (0,)), ((), ()))}


def _pick_k(k_unit, fixed_bytes, per_k_bytes):
    for n in range(1, k_unit // 128 + 1):
        if k_unit % n == 0 and (n == 1 or (k_unit // n) % 128 == 0):
            if fixed_bytes + (k_unit // n) * per_k_bytes <= MM_VMEM_BUDGET:
                return k_unit // n
    return min(k_unit, 128)


def _mm(name, form, a, b, M, N, K, *, tm=1024, tn=1024, k_unit=None, a_spec=None, b_spec=None,
        o_shape=None, o_spec=None, out_dtype=F32, pro=None, epi=None, extras=(), deps=()):
    tm, tn = _pick(M, tm), _pick(N, tn)
    out_dtypes = out_dtype if isinstance(out_dtype, tuple) else (out_dtype,)
    n_out = len(out_dtypes)
    in_place = n_out == 1 and out_dtypes[0] == F32
    fixed = 2 * tm * tn * (sum(jnp.dtype(d).itemsize for d in out_dtypes) + 4 * len(extras))
    fixed += 0 if in_place else 4 * tm * tn
    tk = _pick_k(K if k_unit is None else k_unit, fixed,
                 2 * (tm * a.dtype.itemsize + tn * b.dtype.itemsize))
    nk = K // tk
    if a_spec is None:
        a_spec = (pl.BlockSpec((tk, tm), lambda i, j, k: (k, i)) if form == "tn"
                  else pl.BlockSpec((tm, tk), lambda i, j, k: (i, k)))
    else:
        a_spec = a_spec(tm, tn, tk)
    if b_spec is None:
        b_spec = (pl.BlockSpec((tn, tk), lambda i, j, k: (j, k)) if form == "nt"
                  else pl.BlockSpec((tk, tn), lambda i, j, k: (k, j)))
    else:
        b_spec = b_spec(tm, tn, tk)
    if o_spec is None:
        o_spec = pl.BlockSpec((tm, tn), lambda i, j, k: (i, j))
        o_shape = (M, N)
    else:
        o_spec = o_spec(tm, tn, tk)
    ex_arrays = [e for e in extras]
    ex_specs = [pl.BlockSpec((tm, tn), lambda i, j, k: (i, j)) for _ in extras]
    ne = len(ex_arrays)
    nd = len(deps)
    dep_specs = [pl.BlockSpec((8, 128), lambda i, j, k: (0, 0)) for _ in deps]
    dims = _DIMS[form]
    use_scratch = nk > 1 and not in_place

    def body(a_ref, b_ref, *rest):
        ex_refs = rest[:ne]
        o_refs = rest[ne + nd:ne + nd + n_out]
        at = a_ref[...]
        if pro is not None:
            at = pro(at)
        p = lax.dot_general(at.astype(BF16), b_ref[...].astype(BF16), dims, preferred_element_type=F32)

        def finish(acc):
            res = acc if epi is None else epi(acc, *[r[...] for r in ex_refs])
            res = res if isinstance(res, tuple) else (res,)
            for o_ref, r, d in zip(o_refs, res, out_dtypes):
                o_ref[...] = r.astype(d)

        if nk == 1:
            finish(p)
        else:
            acc_ref = rest[ne + nd + n_out] if use_scratch else o_refs[0]
            k = pl.program_id(2)

            @pl.when(k == 0)
            def _():
                acc_ref[...] = p

            @pl.when(jnp.logical_and(k > 0, k < nk - 1))
            def _():
                acc_ref[...] += p

            @pl.when(k == nk - 1)
            def _():
                finish(acc_ref[...] + p)

    res = pl.pallas_call(
        body, name=name, grid=(M // tm, N // tn, nk),
        in_specs=[a_spec, b_spec] + ex_specs + dep_specs, out_specs=[o_spec] * n_out,
        out_shape=[jax.ShapeDtypeStruct(o_shape, d) for d in out_dtypes],
        scratch_shapes=[pltpu.VMEM((tm, tn), F32)] if use_scratch else [],
        compiler_params=_cp("parallel", "parallel", "arbitrary"),
    )(a, b, *ex_arrays, *deps)
    return res if n_out > 1 else res[0]


def _colshard_b(l, n_per):
    def make(tm, tn, tk):
        nb = n_per // tn
        return pl.BlockSpec((None, None, tk, tn), lambda i, j, k: (j // nb, l, k, j % nb))
    return make


def _colshard_bt(l, n_per):
    def make(tm, tn, tk):
        nb = n_per // tk
        return pl.BlockSpec((None, None, tn, tk), lambda i, j, k: (k // nb, l, j, k % nb))
    return make


def _colshard_o(n_per):
    def make(tm, tn, tk):
        nb = n_per // tn
        return pl.BlockSpec((None, tm, tn), lambda i, j, k: (j // nb, i, j % nb))
    return make


def _rowshard_b(l, k_per):
    def make(tm, tn, tk):
        nb = k_per // tk
        return pl.BlockSpec((None, None, tk, tn), lambda i, j, k: (k // nb, l, k % nb, j))
    return make


def _rowshard_bt(l, k_per):
    def make(tm, tn, tk):
        nb = k_per // tn
        return pl.BlockSpec((None, None, tn, tk), lambda i, j, k: (j // nb, l, j % nb, k))
    return make


def _res_ln(name, xprev, h, g, b, scale=None):
    T, D = xprev.shape
    tr = _pick(T, 256)
    row = pl.BlockSpec((tr, D), lambda i: (i, 0))
    vec = pl.BlockSpec((1, D), lambda i: (0, 0))
    has_scale = scale is not None

    def body(*refs):
        if has_scale:
            x_ref, h_ref, s_ref, g_ref, b_ref, y_ref, yb_ref, xh_ref, rs_ref = refs
            hh = h_ref[...] * s_ref[...]
        else:
            x_ref, h_ref, g_ref, b_ref, y_ref, yb_ref, xh_ref, rs_ref = refs
            hh = h_ref[...]
        u = ALPHA * x_ref[...] + hh
        mu = jnp.mean(u, axis=-1, keepdims=True)
        d = u - mu
        var = jnp.mean(d * d, axis=-1, keepdims=True)
        rs = lax.rsqrt(var + LN_EPS)
        xh = d * rs
        y = xh * g_ref[...] + b_ref[...]
        y_ref[...] = y
        yb_ref[...] = y.astype(BF16)
        xh_ref[...] = xh
        rs_ref[...] = rs

    ins = [xprev, h] + ([scale] if has_scale else []) + [g, b]
    specs = [row, row] + ([vec] if has_scale else []) + [vec, vec]
    return pl.pallas_call(
        body, name=name, grid=(T // tr,), in_specs=specs,
        out_specs=[row, row, row, pl.BlockSpec((tr, 1), lambda i: (i, 0))],
        out_shape=[jax.ShapeDtypeStruct((T, D), F32), jax.ShapeDtypeStruct((T, D), BF16),
                   jax.ShapeDtypeStruct((T, D), F32), jax.ShapeDtypeStruct((T, 1), F32)],
        compiler_params=_cp("parallel"),
    )(*ins)


def _accum(ref, part, first):
    @pl.when(first)
    def _():
        ref[...] = part

    @pl.when(jnp.logical_not(first))
    def _():
        ref[...] += part


def _ln_bwd(name, dy, xh, rs, g, hraw=None, scale=None):
    T, D = dy.shape
    tr = _pick(T, 256)
    row = pl.BlockSpec((tr, D), lambda i: (i, 0))
    vec = pl.BlockSpec((1, D), lambda i: (0, 0))
    has_scale = scale is not None

    def body(*refs):
        if has_scale:
            dy_ref, xh_ref, rs_ref, g_ref, hr_ref, s_ref, du_ref, dub_ref, dg_ref, db_ref, ds_ref = refs
        else:
            dy_ref, xh_ref, rs_ref, g_ref, du_ref, dub_ref, dg_ref, db_ref = refs
        first = pl.program_id(0) == 0
        dyv = dy_ref[...]
        xhv = xh_ref[...]
        dxh = dyv * g_ref[...]
        m1 = jnp.mean(dxh, axis=-1, keepdims=True)
        m2 = jnp.mean(dxh * xhv, axis=-1, keepdims=True)
        du = rs_ref[...] * (dxh - m1 - xhv * m2)
        du_ref[...] = du
        if has_scale:
            dub_ref[...] = (du * s_ref[...]).astype(BF16)
            _accum(ds_ref, jnp.sum(du * hr_ref[...], axis=0, keepdims=True), first)
        else:
            dub_ref[...] = du.astype(BF16)
        _accum(dg_ref, jnp.sum(dyv * xhv, axis=0, keepdims=True), first)
        _accum(db_ref, jnp.sum(dyv, axis=0, keepdims=True), first)

    ins = [dy, xh, rs, g] + ([hraw, scale] if has_scale else [])
    specs = [row, row, pl.BlockSpec((tr, 1), lambda i: (i, 0)), vec] + ([row, vec] if has_scale else [])
    n_vec = 3 if has_scale else 2
    return pl.pallas_call(
        body, name=name, grid=(T // tr,), in_specs=specs,
        out_specs=[row, row] + [vec] * n_vec,
        out_shape=[jax.ShapeDtypeStruct((T, D), F32), jax.ShapeDtypeStruct((T, D), BF16)]
        + [jax.ShapeDtypeStruct((1, D), F32)] * n_vec,
        compiler_params=_cp("arbitrary"),
    )(*ins)


def _gate_fwd(name, x, gl, e):
    T, D = x.shape
    tr = _pick(T, 256)
    row = pl.BlockSpec((tr, D), lambda i: (i, 0))

    def body(x_ref, gl_ref, e_ref, xo_ref, xob_ref, gate_ref):
        gate = _sigmoid(gl_ref[...])
        xo = x_ref[...] + gate * e_ref[...]
        xo_ref[...] = xo
        xob_ref[...] = xo.astype(BF16)
        gate_ref[...] = gate

    return pl.pallas_call(
        body, name=name, grid=(T // tr,), in_specs=[row, row, row], out_specs=[row, row, row],
        out_shape=[jax.ShapeDtypeStruct((T, D), F32), jax.ShapeDtypeStruct((T, D), BF16),
                   jax.ShapeDtypeStruct((T, D), F32)],
        compiler_params=_cp("parallel"),
    )(x, gl, e)


def _gate_loss(name, x, gl, e, tgt):
    T, D = x.shape
    tr = _pick(T, 256)
    row = pl.BlockSpec((tr, D), lambda i: (i, 0))

    def body(x_ref, gl_ref, e_ref, t_ref, dy_ref, gate_ref, lp_ref):
        gate = _sigmoid(gl_ref[...])
        err = x_ref[...] + gate * e_ref[...] - t_ref[...]
        dy_ref[...] = err * (1.0 / D)
        gate_ref[...] = gate
        s = jnp.sum(jnp.mean(err * err, axis=-1, keepdims=True), axis=0, keepdims=True)
        lp_ref[...] = jnp.broadcast_to(0.5 * s, (8, 128))

    return pl.pallas_call(
        body, name=name, grid=(T // tr,), in_specs=[row] * 4,
        out_specs=[row, row, pl.BlockSpec((8, 128), lambda i: (i, 0))],
        out_shape=[jax.ShapeDtypeStruct((T, D), F32), jax.ShapeDtypeStruct((T, D), F32),
                   jax.ShapeDtypeStruct((T // tr * 8, 128), F32)],
        compiler_params=_cp("parallel"),
    )(x, gl, e, tgt)


def _gate_bwd(name, dxo, gate, e):
    T, D = dxo.shape
    tr = _pick(T, 256)
    row = pl.BlockSpec((tr, D), lambda i: (i, 0))

    def body(d_ref, gate_ref, e_ref, dgl_ref, de_ref):
        d = d_ref[...]
        gate = gate_ref[...]
        dgl_ref[...] = (d * e_ref[...] * gate * (1.0 - gate)).astype(BF16)
        de_ref[...] = (d * gate).astype(BF16)

    return pl.pallas_call(
        body, name=name, grid=(T // tr,), in_specs=[row] * 3, out_specs=[row, row],
        out_shape=[jax.ShapeDtypeStruct((T, D), BF16)] * 2,
        compiler_params=_cp("parallel"),
    )(dxo, gate, e)


def _gnorm_fwd(name, y, z, w):
    T, DI = y.shape
    tr = _pick(T, 128)
    row = pl.BlockSpec((tr, DI), lambda i: (i, 0))
    vec = pl.BlockSpec((1, DI), lambda i: (0, 0))
    col = pl.BlockSpec((tr, 1), lambda i: (i, 0))

    def body(y_ref, z_ref, w_ref, o_ref, rs_ref):
        zv = z_ref[...]
        yz = y_ref[...] * (zv * _sigmoid(zv))
        rs = lax.rsqrt(jnp.mean(yz * yz, axis=-1, keepdims=True) + RMS_EPS)
        o_ref[...] = (yz * rs * w_ref[...]).astype(BF16)
        rs_ref[...] = rs

    return pl.pallas_call(
        body, name=name, grid=(T // tr,), in_specs=[row, row, vec], out_specs=[row, col],
        out_shape=[jax.ShapeDtypeStruct((T, DI), BF16), jax.ShapeDtypeStruct((T, 1), F32)],
        compiler_params=_cp("parallel"),
    )(y, z, w)


def _gnorm_bwd(name, dyn, y, z, w, rs):
    T, DI = y.shape
    tr = _pick(T, 128)
    row = pl.BlockSpec((tr, DI), lambda i: (i, 0))
    vec = pl.BlockSpec((1, DI), lambda i: (0, 0))
    col = pl.BlockSpec((tr, 1), lambda i: (i, 0))

    def body(d_ref, y_ref, z_ref, w_ref, rs_ref, dy_ref, dz_ref, dw_ref):
        first = pl.program_id(0) == 0
        zv = z_ref[...]
        yv = y_ref[...]
        sg = _sigmoid(zv)
        sz = zv * sg
        rsv = rs_ref[...]
        yzh = yv * sz * rsv
        dv = d_ref[...]
        gw = dv * w_ref[...]
        m = jnp.mean(gw * yzh, axis=-1, keepdims=True)
        dyz = rsv * (gw - yzh * m)
        dy_ref[...] = dyz * sz
        dz_ref[...] = (dyz * yv * (sg * (1.0 + zv * (1.0 - sg)))).astype(BF16)
        _accum(dw_ref, jnp.sum(dv * yzh, axis=0, keepdims=True), first)

    return pl.pallas_call(
        body, name=name, grid=(T // tr,), in_specs=[row, row, row, vec, col], out_specs=[row, row, vec],
        out_shape=[jax.ShapeDtypeStruct((T, DI), F32), jax.ShapeDtypeStruct((T, DI), BF16),
                   jax.ShapeDtypeStruct((1, DI), F32)],
        compiler_params=_cp("arbitrary"),
    )(dyn, y, z, w, rs)


def _sel4(j, vals):
    return jnp.where(j == 0, vals[0], jnp.where(j == 1, vals[1], jnp.where(j == 2, vals[2], vals[3])))


def _pool_cnt(i, j, tr, rows, offset):
    t = i * tr + offset + lax.broadcasted_iota(jnp.int32, (rows, 1), 0)
    win = _sel4(j, POOL_WINDOWS)
    return jnp.minimum(t + 1, win).astype(F32)


def _pool_fwd(name, x):
    T, D = x.shape
    gd = D // len(POOL_WINDOWS)
    tr = _pick(T, 512)
    hb = tr // POOL_HALO

    def body(x_ref, h_ref, o_ref):
        i, j = pl.program_id(0), pl.program_id(1)
        xv = x_ref[...]
        halo = jnp.where(i > 0, h_ref[...], 0.0)
        cat = jnp.concatenate([halo, xv], axis=0)
        s2 = cat + pltpu.roll(cat, 1, 0)
        s4 = s2 + pltpu.roll(s2, 2, 0)
        s8 = s4 + pltpu.roll(s4, 4, 0)
        s16 = s8 + pltpu.roll(s8, 8, 0)
        sel = _sel4(j, (s2, s4, s8, s16))[POOL_HALO:]
        o_ref[...] = (sel / _pool_cnt(i, j, tr, tr, 0) - xv).astype(BF16)

    return pl.pallas_call(
        body, name=name, grid=(T // tr, len(POOL_WINDOWS)),
        in_specs=[pl.BlockSpec((tr, gd), lambda i, j: (i, j)),
                  pl.BlockSpec((POOL_HALO, gd), lambda i, j: (jnp.maximum(i * hb - 1, 0), j))],
        out_specs=pl.BlockSpec((tr, gd), lambda i, j: (i, j)),
        out_shape=jax.ShapeDtypeStruct((T, D), BF16),
        compiler_params=_cp("parallel", "parallel"),
    )(x, x)


def _pool_bwd(name, dp, du):
    T, D = dp.shape
    gd = D // len(POOL_WINDOWS)
    tr = _pick(T, 512)
    hb = tr // POOL_HALO
    last_h = T // POOL_HALO - 1
    n = tr + POOL_HALO

    def body(dp_ref, h_ref, du_ref, o_ref):
        i, j = pl.program_id(0), pl.program_id(1)
        dpv = dp_ref[...]
        q = dpv / _pool_cnt(i, j, tr, tr, 0)
        qh = jnp.where(i < pl.num_programs(0) - 1, h_ref[...] / _pool_cnt(i, j, tr, POOL_HALO, tr), 0.0)
        cat = jnp.concatenate([q, qh], axis=0)
        f2 = cat + pltpu.roll(cat, n - 1, 0)
        f4 = f2 + pltpu.roll(f2, n - 2, 0)
        f8 = f4 + pltpu.roll(f4, n - 4, 0)
        f16 = f8 + pltpu.roll(f8, n - 8, 0)
        sel = _sel4(j, (f2, f4, f8, f16))[:tr]
        o_ref[...] = ALPHA * du_ref[...] + sel - dpv

    return pl.pallas_call(
        body, name=name, grid=(T // tr, len(POOL_WINDOWS)),
        in_specs=[pl.BlockSpec((tr, gd), lambda i, j: (i, j)),
                  pl.BlockSpec((POOL_HALO, gd), lambda i, j: (jnp.minimum((i + 1) * hb, last_h), j)),
                  pl.BlockSpec((tr, gd), lambda i, j: (i, j))],
        out_specs=pl.BlockSpec((tr, gd), lambda i, j: (i, j)),
        out_shape=jax.ShapeDtypeStruct((T, D), F32),
        compiler_params=_cp("parallel", "parallel"),
    )(dp, dp, du)


def _conv_taps(cat, wv, rows):
    shifted = [cat[CONV_HALO:] if s == 0 else pltpu.roll(cat, s, 0)[CONV_HALO:] for s in range(CONV_K)]
    acc = shifted[0] * wv[CONV_K - 1:CONV_K]
    for s in range(1, CONV_K):
        acc = acc + shifted[s] * wv[CONV_K - 1 - s:CONV_K - s]
    return acc, shifted


def _conv_fwd(name, xp, w, b):
    T, CD = xp.shape
    tr, tc = _pick(T, 512), _pick(CD, 512)
    hb = tr // CONV_HALO

    def body(x_ref, h_ref, w_ref, b_ref, o_ref):
        i = pl.program_id(0)
        halo = jnp.where(i > 0, h_ref[...], 0.0)
        cat = jnp.concatenate([halo, x_ref[...]], axis=0)
        acc, _ = _conv_taps(cat, w_ref[...], tr)
        acc = acc + b_ref[...]
        o_ref[...] = acc * _sigmoid(acc)

    return pl.pallas_call(
        body, name=name, grid=(T // tr, CD // tc),
        in_specs=[pl.BlockSpec((tr, tc), lambda i, j: (i, j)),
                  pl.BlockSpec((CONV_HALO, tc), lambda i, j: (jnp.maximum(i * hb - 1, 0), j)),
                  pl.BlockSpec((CONV_K, tc), lambda i, j: (0, j)),
                  pl.BlockSpec((1, tc), lambda i, j: (0, j))],
        out_specs=pl.BlockSpec((tr, tc), lambda i, j: (i, j)),
        out_shape=jax.ShapeDtypeStruct((T, CD), F32),
        compiler_params=_cp("parallel", "parallel"),
    )(xp, xp, w, b)


def _conv_bwd_a(name, dxs, dbm, dcm, xp, w, b):
    T, CD = xp.shape
    tr = _pick(T, 512)
    tc = _pick(dbm.shape[1], 512)
    hb = tr // CONV_HALO
    nx, nb = dxs.shape[1] // tc, dbm.shape[1] // tc

    def part_spec(lo, n):
        def imap(j, i):
            inside = jnp.logical_and(j >= lo, j < lo + n)
            return (jnp.where(inside, i, 0), jnp.clip(j - lo, 0, n - 1))
        return pl.BlockSpec((tr, tc), imap)

    def body(dx_ref, db_ref, dc_ref, x_ref, h_ref, w_ref, b_ref, o_ref, dw_ref, dbias_ref):
        j, i = pl.program_id(0), pl.program_id(1)
        first = i == 0
        d = jnp.where(j < nx, dx_ref[...], jnp.where(j < nx + nb, db_ref[...], dc_ref[...]))
        halo = jnp.where(i > 0, h_ref[...], 0.0)
        cat = jnp.concatenate([halo, x_ref[...]], axis=0)
        acc, shifted = _conv_taps(cat, w_ref[...], tr)
        acc = acc + b_ref[...]
        sg = _sigmoid(acc)
        dconv = d * (sg * (1.0 + acc * (1.0 - sg)))
        o_ref[...] = dconv
        _accum(dbias_ref, jnp.sum(dconv, axis=0, keepdims=True), first)
        tap = lax.broadcasted_iota(jnp.int32, (CONV_K, tc), 0)
        dwv = jnp.zeros((CONV_K, tc), F32)
        for s in range(CONV_K):
            dwv = jnp.where(tap == CONV_K - 1 - s, jnp.sum(dconv * shifted[s], axis=0, keepdims=True), dwv)
        _accum(dw_ref, dwv, first)

    return pl.pallas_call(
        body, name=name, grid=(CD // tc, T // tr),
        in_specs=[part_spec(0, nx), part_spec(nx, nb), part_spec(nx + nb, nb),
                  pl.BlockSpec((tr, tc), lambda j, i: (i, j)),
                  pl.BlockSpec((CONV_HALO, tc), lambda j, i: (jnp.maximum(i * hb - 1, 0), j)),
                  pl.BlockSpec((CONV_K, tc), lambda j, i: (0, j)),
                  pl.BlockSpec((1, tc), lambda j, i: (0, j))],
        out_specs=[pl.BlockSpec((tr, tc), lambda j, i: (i, j)),
                   pl.BlockSpec((CONV_K, tc), lambda j, i: (0, j)),
                   pl.BlockSpec((1, tc), lambda j, i: (0, j))],
        out_shape=[jax.ShapeDtypeStruct((T, CD), F32), jax.ShapeDtypeStruct((CONV_K, CD), F32),
                   jax.ShapeDtypeStruct((1, CD), F32)],
        compiler_params=_cp("parallel", "arbitrary"),
    )(dxs, dbm, dcm, xp, xp, w, b)


def _conv_bwd_b(name, dconv, w):
    T, CD = dconv.shape
    tr, tc = _pick(T, 512), _pick(CD, 512)
    hb = tr // CONV_HALO
    last_h = T // CONV_HALO - 1
    n = tr + CONV_HALO

    def body(d_ref, h_ref, w_ref, o_ref):
        i = pl.program_id(0)
        halo = jnp.where(i < pl.num_programs(0) - 1, h_ref[...], 0.0)
        cat = jnp.concatenate([d_ref[...], halo], axis=0)
        wv = w_ref[...]
        acc = cat[:tr] * wv[CONV_K - 1:CONV_K]
        for s in range(1, CONV_K):
            acc = acc + pltpu.roll(cat, n - s, 0)[:tr] * wv[CONV_K - 1 - s:CONV_K - s]
        o_ref[...] = acc.astype(BF16)

    return pl.pallas_call(
        body, name=name, grid=(T // tr, CD // tc),
        in_specs=[pl.BlockSpec((tr, tc), lambda i, j: (i, j)),
                  pl.BlockSpec((CONV_HALO, tc), lambda i, j: (jnp.minimum((i + 1) * hb, last_h), j)),
                  pl.BlockSpec((CONV_K, tc), lambda i, j: (0, j))],
        out_specs=pl.BlockSpec((tr, tc), lambda i, j: (i, j)),
        out_shape=jax.ShapeDtypeStruct((T, CD), BF16),
        compiler_params=_cp("parallel", "parallel"),
    )(dconv, dconv, w)


def _dt_fwd(name, dtp, bias):
    T, H = dtp.shape

    def body(x_ref, b_ref, o_ref):
        v = x_ref[...] + b_ref[...]
        u = jnp.exp(-jnp.abs(v))
        w1 = 1.0 + u
        lp = jnp.where(w1 == 1.0, u, jnp.log(w1) * (u / jnp.where(w1 == 1.0, 1.0, w1 - 1.0)))
        o_ref[...] = jnp.maximum(v, 0.0) + lp

    return pl.pallas_call(body, name=name, out_shape=jax.ShapeDtypeStruct((T, H), F32))(dtp, bias)


def _dt_bwd(name, ddt, dtp, bias):
    T, H = dtp.shape

    def body(d_ref, x_ref, b_ref, o_ref, ob_ref, db_ref):
        g = d_ref[...] * _sigmoid(x_ref[...] + b_ref[...])
        o_ref[...] = g
        ob_ref[...] = g.astype(BF16)
        db_ref[...] = jnp.sum(g, axis=0, keepdims=True)

    return pl.pallas_call(
        body, name=name,
        out_shape=[jax.ShapeDtypeStruct((T, H), F32), jax.ShapeDtypeStruct((T, H), BF16),
                   jax.ShapeDtypeStruct((1, H), F32)],
    )(ddt, dtp, bias)


def _dotf(a, b):
    return jnp.dot(a, b, precision=HI, preferred_element_type=F32)


def _dotb(form, a, b):
    return lax.dot_general(a.astype(BF16), b.astype(BF16), _DIMS[form], preferred_element_type=F32)


def _ssd_common(dtc, dtr, alr, alc, gw):
    li = lax.broadcasted_iota(jnp.int32, (CHUNK, CHUNK), 0)
    si = lax.broadcasted_iota(jnp.int32, (CHUNK, CHUNK), 1)
    tri = (li >= si).astype(F32)
    trit = (li <= si).astype(F32)
    a_row = -jnp.exp(alr)
    a_col = -jnp.exp(alc)
    acs_c = _dotf(tri, dtc * a_row)
    acs_r = _dotf(dtr * a_col, trit)
    eh = lax.broadcasted_iota(jnp.int32, (HEAD_PAD, gw), 0)
    ec = lax.broadcasted_iota(jnp.int32, (HEAD_PAD, gw), 1) // HEAD_DIM
    expand = (eh == ec).astype(F32)
    th = lax.broadcasted_iota(jnp.int32, (gw, HEAD_PAD), 1)
    tc = lax.broadcasted_iota(jnp.int32, (gw, HEAD_PAD), 0) // HEAD_DIM
    reduce_ = (th == tc).astype(F32)
    acs_last = acs_c[CHUNK - 1:CHUNK, :]
    return dict(li=li, si=si, tri=tri, trit=trit, a_row=a_row, acs_c=acs_c, acs_r=acs_r,
                expand=expand, reduce=reduce_, acs_last=acs_last,
                dt_e=_dotf(dtc, expand), eacs_e=jnp.exp(_dotf(acs_c, expand)),
                dec_h=jnp.exp(acs_last - acs_c))


def _ssd_specs(T, DI, gw, nc, rev):
    nsb = DI // D_STATE
    cidx = (lambda c: nc - 1 - c) if rev else (lambda c: c)
    return dict(
        xs=pl.BlockSpec((CHUNK, gw), lambda g, c: (cidx(c), g)),
        bm=pl.BlockSpec((CHUNK, D_STATE), lambda g, c: (cidx(c), nsb + g)),
        cm=pl.BlockSpec((CHUNK, D_STATE), lambda g, c: (cidx(c), nsb + N_GROUPS + g)),
        dtc=pl.BlockSpec((None, CHUNK, HEAD_PAD), lambda g, c: (g, cidx(c), 0)),
        dtr=pl.BlockSpec((None, ROW_PAD, CHUNK), lambda g, c: (g, 0, cidx(c))),
        alr=pl.BlockSpec((None, 1, HEAD_PAD), lambda g, c: (g, 0, 0)),
        alc=pl.BlockSpec((None, ROW_PAD, 1), lambda g, c: (g, 0, 0)),
        de=pl.BlockSpec((None, 1, gw), lambda g, c: (g, 0, 0)),
        hp=pl.BlockSpec((None, None, gw, D_STATE), lambda g, c: (cidx(c), g, 0, 0)),
        bc=pl.BlockSpec((CHUNK, D_STATE), lambda g, c: (cidx(c), g)),
        acc=pl.BlockSpec((None, 1, HEAD_PAD), lambda g, c: (g, 0, 0)),
    )


def _ssd_fwd(name, xbc, dtc, dtr, alr, alc, d_e, DI):
    T = xbc.shape[0]
    nc = T // CHUNK
    gw = DI // N_GROUPS
    hpg = gw // HEAD_DIM
    sp = _ssd_specs(T, DI, gw, nc, False)

    def body(xs_ref, b_ref, c_ref, dtc_ref, dtr_ref, alr_ref, alc_ref, de_ref, y_ref, hp_ref, h_scr):
        @pl.when(pl.program_id(1) == 0)
        def _():
            h_scr[...] = jnp.zeros_like(h_scr)

        hpv = h_scr[...]
        hp_ref[...] = hpv
        xs = xs_ref[...]
        bb = b_ref[...].astype(BF16)
        cb_ = c_ref[...].astype(BF16)
        cm = _ssd_common(dtc_ref[...], dtr_ref[...], alr_ref[...], alc_ref[...], gw)
        x = xs * cm["dt_e"]
        xb = x.astype(BF16)
        cbm = _dotb("nt", cb_, bb)
        causal = cm["li"] >= cm["si"]
        lane_head = lax.broadcasted_iota(jnp.int32, (1, gw), 1) // HEAD_DIM
        ydiag = jnp.zeros((CHUNK, gw), F32)
        for j in range(hpg):
            seg = cm["acs_c"][:, j:j + 1] - cm["acs_r"][j:j + 1, :]
            lmat = jnp.exp(jnp.where(causal, seg, -1e30))
            yj = _dotb("nn", cbm * lmat, xb)
            ydiag = jnp.where(lane_head == j, yj, ydiag)
        dec_e = _dotf(cm["dec_h"], cm["expand"])
        states = _dotb("tn", x * dec_e, bb)
        yoff = _dotb("nt", cb_, hpv) * cm["eacs_e"]
        y_ref[...] = ydiag + yoff + xs * de_ref[...]
        cd_rows = jnp.sum(cm["reduce"] * jnp.exp(cm["acs_last"]), axis=1, keepdims=True)
        h_scr[...] = cd_rows * hpv + states

    return pl.pallas_call(
        body, name=name, grid=(N_GROUPS, nc),
        in_specs=[sp["xs"], sp["bm"], sp["cm"], sp["dtc"], sp["dtr"], sp["alr"], sp["alc"], sp["de"]],
        out_specs=[sp["xs"], sp["hp"]],
        out_shape=[jax.ShapeDtypeStruct((T, DI), F32), jax.ShapeDtypeStruct((nc, N_GROUPS, gw, D_STATE), F32)],
        scratch_shapes=[pltpu.VMEM((gw, D_STATE), F32)],
        compiler_params=_cp("parallel", "arbitrary"),
    )(xbc, xbc, xbc, dtc, dtr, alr, alc, d_e)


def _ssd_bwd(name, dy, xbc, dtc, dtr, alr, alc, d_e, hprev, DI):
    T = xbc.shape[0]
    nc = T // CHUNK
    gw = DI // N_GROUPS
    hpg = gw // HEAD_DIM
    sp = _ssd_specs(T, DI, gw, nc, True)

    def body(dy_ref, xs_ref, b_ref, c_ref, dtc_ref, dtr_ref, alr_ref, alc_ref, de_ref, hp_ref,
             dxs_ref, db_ref, dc_ref, ddt_ref, dal_ref, dd_ref, dh_scr):
        first = pl.program_id(1) == 0

        @pl.when(first)
        def _():
            dh_scr[...] = jnp.zeros_like(dh_scr)

        xs = xs_ref[...]
        dyv = dy_ref[...]
        bb = b_ref[...].astype(BF16)
        cb_ = c_ref[...].astype(BF16)
        dtc_v = dtc_ref[...]
        cm = _ssd_common(dtc_v, dtr_ref[...], alr_ref[...], alc_ref[...], gw)
        hpv = hp_ref[...]
        hpb = hpv.astype(BF16)
        dhn = dh_scr[...]
        dhnb = dhn.astype(BF16)
        x = xs * cm["dt_e"]
        xb = x.astype(BF16)
        cbm = _dotb("nt", cb_, bb)
        causal = cm["li"] >= cm["si"]
        lane_head = lax.broadcasted_iota(jnp.int32, (1, gw), 1) // HEAD_DIM
        lane_pad = lax.broadcasted_iota(jnp.int32, (1, HEAD_PAD), 1)
        sub_pad = lax.broadcasted_iota(jnp.int32, (ROW_PAD, 1), 0)

        dxs = dyv * de_ref[...]
        dd_part = jnp.sum(_dotf(dyv * xs, cm["reduce"]), axis=0, keepdims=True)
        _accum(dd_ref, dd_part, first)

        dx = jnp.zeros((CHUNK, gw), F32)
        dcb = jnp.zeros((CHUNK, CHUNK), F32)
        dacs_c = jnp.zeros((CHUNK, HEAD_PAD), F32)
        dacs_r = jnp.zeros((ROW_PAD, CHUNK), F32)
        for j in range(hpg):
            seg = cm["acs_c"][:, j:j + 1] - cm["acs_r"][j:j + 1, :]
            lmat = jnp.exp(jnp.where(causal, seg, -1e30))
            mmat = cbm * lmat
            dym = jnp.where(lane_head == j, dyv, 0.0).astype(BF16)
            dm = _dotb("nt", dym, xb)
            dx = dx + _dotb("tn", mmat, dym)
            dcb = dcb + dm * lmat
            dseg = dm * mmat
            dacs_c = dacs_c + jnp.where(lane_pad == j, jnp.sum(dseg, axis=1, keepdims=True), 0.0)
            dacs_r = dacs_r - jnp.where(sub_pad == j, jnp.sum(dseg, axis=0, keepdims=True), 0.0)
        dc = _dotb("nn", dcb, bb)
        db = _dotb("tn", dcb, cb_)

        gm = _dotb("nt", cb_, hpb)
        dgm = dyv * cm["eacs_e"]
        dacs_c = dacs_c + _dotf(dgm * gm, cm["reduce"])
        dc = dc + _dotb("nn", dgm, hpb)
        dhp = _dotb("tn", dgm, cb_)

        cd_row = jnp.exp(cm["acs_last"])
        cd_rows = jnp.sum(cm["reduce"] * cd_row, axis=1, keepdims=True)
        dhp = dhp + cd_rows * dhn
        rsum = jnp.sum(dhn * hpv, axis=1, keepdims=True)
        dacs_last = jnp.sum(cm["reduce"] * rsum, axis=0, keepdims=True) * cd_row
        dec_e = _dotf(cm["dec_h"], cm["expand"])
        xdec = x * dec_e
        dxdec = _dotb("nt", bb, dhnb)
        db = db + _dotb("nn", xdec, dhnb)
        dx = dx + dxdec * dec_e
        tdec = _dotf(dxdec * x, cm["reduce"]) * cm["dec_h"]
        dacs_c = dacs_c - tdec
        dacs_last = dacs_last + jnp.sum(tdec, axis=0, keepdims=True)
        row_id = lax.broadcasted_iota(jnp.int32, (CHUNK, 1), 0)
        dacs_c = dacs_c + jnp.where(row_id == CHUNK - 1, dacs_last, 0.0)

        dxs_ref[...] = dxs + dx * cm["dt_e"]
        ddt = _dotf(dx * xs, cm["reduce"])
        dda = _dotf(cm["trit"], dacs_c)
        dda_r = _dotf(dacs_r, cm["tri"])
        dda_rp = jnp.concatenate([dda_r, jnp.zeros((HEAD_PAD - ROW_PAD, CHUNK), F32)], axis=0)
        eye = (cm["li"] == cm["si"]).astype(F32)
        dda = dda + lax.dot_general(eye, dda_rp, _DIMS["nt"], precision=HI, preferred_element_type=F32)
        ddt_ref[...] = ddt + dda * cm["a_row"]
        _accum(dal_ref, jnp.sum(dda * dtc_v, axis=0, keepdims=True) * cm["a_row"], first)
        db_ref[...] = db
        dc_ref[...] = dc
        dh_scr[...] = dhp

    gs = D_STATE * N_GROUPS
    return pl.pallas_call(
        body, name=name, grid=(N_GROUPS, nc),
        in_specs=[sp["xs"], sp["xs"], sp["bm"], sp["cm"], sp["dtc"], sp["dtr"], sp["alr"], sp["alc"],
                  sp["de"], sp["hp"]],
        out_specs=[sp["xs"], sp["bc"], sp["bc"], sp["dtc"], sp["acc"], sp["acc"]],
        out_shape=[jax.ShapeDtypeStruct((T, DI), F32), jax.ShapeDtypeStruct((T, gs), F32),
                   jax.ShapeDtypeStruct((T, gs), F32), jax.ShapeDtypeStruct((N_GROUPS, T, HEAD_PAD), F32),
                   jax.ShapeDtypeStruct((N_GROUPS, 1, HEAD_PAD), F32),
                   jax.ShapeDtypeStruct((N_GROUPS, 1, HEAD_PAD), F32)],
        scratch_shapes=[pltpu.VMEM((gw, D_STATE), F32)],
        compiler_params=_cp("parallel", "arbitrary"),
    )(dy, xbc, xbc, xbc, dtc, dtr, alr, alc, d_e, hprev)


def _adam_math(w, g, m, v):
    m = ADAM_B1 * m + (1.0 - ADAM_B1) * g
    v = ADAM_B2 * v + (1.0 - ADAM_B2) * (g * g)
    m_hat = m / (1.0 - ADAM_B1 ** ADAM_STEP)
    v_hat = v / (1.0 - ADAM_B2 ** ADAM_STEP)
    delta = -ADAM_LR * (m_hat / (jnp.sqrt(v_hat) + ADAM_EPS) + ADAM_WD * w)
    return delta, m, v


def _adamw(name, w, m, v, grads):
    L, R, C = w.shape
    tr = _rows(R, C * 4, ADAM_BLOCK_BYTES)
    full = pl.BlockSpec((None, tr, C), lambda l, i: (l, i, 0))
    g_specs = [pl.BlockSpec((tr, C), functools.partial(lambda l, i, ll: (jnp.where(l == ll, i, 0), 0), ll=ll))
               for ll in range(L)]

    def body(w_ref, m_ref, v_ref, *rest):
        g_refs = rest[:L]
        go_ref, d_ref, mo_ref, vo_ref = rest[L:]
        l = pl.program_id(0)
        g = g_refs[0][...]
        for ll in range(1, L):
            g = jnp.where(l == ll, g_refs[ll][...], g)
        delta, mn, vn = _adam_math(w_ref[...], g, m_ref[...], v_ref[...])
        go_ref[...] = g
        d_ref[...] = delta
        mo_ref[...] = mn
        vo_ref[...] = vn

    return pl.pallas_call(
        body, name=name, grid=(L, R // tr), in_specs=[full] * 3 + g_specs, out_specs=[full] * 4,
        out_shape=[jax.ShapeDtypeStruct((L, R, C), F32)] * 4,
        compiler_params=_cp("arbitrary", "arbitrary"),
    )(w, m, v, *grads)


def _sum_peers(name, gathered):
    n, R, C = gathered.shape

    def body(g_ref, o_ref):
        acc = g_ref[0]
        for d in range(1, n):
            acc = acc + g_ref[d]
        o_ref[...] = acc

    return pl.pallas_call(body, name=name, out_shape=jax.ShapeDtypeStruct((R, C), F32))(gathered)


_ANY = pl.BlockSpec(memory_space=pl.ANY)


def _place():
    x, y, c = lax.axis_index("x"), lax.axis_index("y"), lax.axis_index("c")
    chips = [(1 - x, y), (x, 1 - y), (1 - x, 1 - y)]
    return x, y, c, chips


def _allgather_small(name, v):
    R, C = v.shape

    def body(v_ref, o_ref, send_sems, recv_sems):
        x, y, c, _ = _place()
        me = 4 * x + 2 * y + c
        o_ref[me] = v_ref[...]
        copies = []
        for k in range(1, N_DEV):
            px, py, pc = x ^ (k >> 2), y ^ ((k >> 1) & 1), c ^ (k & 1)
            copies.append(pltpu.make_async_remote_copy(
                src_ref=v_ref, dst_ref=o_ref.at[me], send_sem=send_sems.at[k - 1], recv_sem=recv_sems.at[k - 1],
                device_id=(px, py, pc), device_id_type=MESH))
        for cp in copies:
            cp.start()
        for cp in copies:
            cp.wait()

    return pl.pallas_call(
        body, name=name, out_shape=jax.ShapeDtypeStruct((N_DEV, R, C), F32),
        in_specs=[pl.BlockSpec(memory_space=pltpu.VMEM)], out_specs=pl.BlockSpec(memory_space=pltpu.VMEM),
        scratch_shapes=[pltpu.SemaphoreType.DMA((N_DEV - 1,)), pltpu.SemaphoreType.DMA((N_DEV - 1,))],
    )(v)


_HBM = pl.BlockSpec(memory_space=pltpu.HBM)
_SEM = pl.BlockSpec(memory_space=pltpu.SEMAPHORE)
_VMEM = pl.BlockSpec(memory_space=pltpu.VMEM)
_EFFECT = pltpu.SideEffectType.DATAFLOW_SIDE_EFFECTING


def _in_hbm(v):
    return pltpu.with_memory_space_constraint(v, pltpu.HBM)


def _remote(src, dst, send_sem, recv_sem, device):
    return pltpu.make_async_remote_copy(src_ref=src, dst_ref=dst, send_sem=send_sem, recv_sem=recv_sem,
                                        device_id=device, device_id_type=MESH)


def _gather_copies(src, land, ici_s, ici_r, own_s, own_r, arrivals=True):
    x, y, c, chips = _place()
    a = 2 * x + y
    sends, lands = [], []
    for i in range(len(src)):
        own = _remote(src[i], land[i].at[a], own_s.at[i], own_r.at[i], (x, y, 1 - c))
        sends.append(own)
        if arrivals:
            lands.append(own)
        for k, (px, py) in enumerate(chips):
            s, r = ici_s.at[3 * i + k], ici_r.at[3 * i + k]
            sends.append(_remote(src[i].at[c], land[i].at[a, c], s, r, (px, py, c)))
            if arrivals:
                lands.append(_remote(src[i].at[c], land[i].at[2 * px + py, c], s, r, (px, py, c)))
    return sends, lands


def _gather_start(name, shards, after=None):
    n = len(shards)
    lands = [lax.empty((N_CHIPS,) + s.shape, s.dtype) for s in shards]
    n_in = 2 * n + (0 if after is None else 1)

    def body(*refs):
        src, land = refs[:n], refs[n:2 * n]
        sems = refs[n_in:n_in + 4]
        token = refs[-1]
        sends, _ = _gather_copies(src, land, *sems, arrivals=False)
        for cp in sends:
            cp.start()
        token[...] = jnp.zeros_like(token)

    outs = pl.pallas_call(
        body, name=name,
        out_shape=(pltpu.SemaphoreType.DMA((3 * n,)), pltpu.SemaphoreType.DMA((3 * n,)),
                   pltpu.SemaphoreType.DMA((n,)), pltpu.SemaphoreType.DMA((n,)),
                   *[pltpu.HBM(s.shape, s.dtype) for s in shards], *[pltpu.HBM(l.shape, l.dtype) for l in lands],
                   jax.ShapeDtypeStruct((8, 128), F32)),
        in_specs=[_HBM] * (2 * n) + ([] if after is None else [_ANY]),
        out_specs=(_SEM,) * 4 + (_HBM,) * (2 * n) + (_VMEM,),
        input_output_aliases={i: 4 + i for i in range(2 * n)},
        compiler_params=pltpu.CompilerParams(has_side_effects=_EFFECT),
    )(*[_in_hbm(s) for s in shards], *[_in_hbm(l) for l in lands], *([] if after is None else [after]))
    return dict(sems=outs[:4], src=outs[4:4 + n], land=outs[4 + n:4 + 2 * n], token=outs[-1])


def _gather_wait(name, started, after):
    n = len(started["src"])

    def body(*refs):
        src, land = refs[:n], refs[n:2 * n]
        sems = refs[2 * n:2 * n + 4]
        sends, lands = _gather_copies(src, land, *sems)
        for cp in sends:
            cp.wait_send()
        for cp in lands:
            cp.wait_recv()

    outs = pl.pallas_call(
        body, name=name,
        out_shape=[pltpu.HBM(v.shape, v.dtype) for v in list(started["src"]) + list(started["land"])],
        in_specs=[_HBM] * (2 * n) + [_SEM] * 4 + [_ANY], out_specs=[_HBM] * (2 * n),
        input_output_aliases={i: i for i in range(2 * n)},
        compiler_params=pltpu.CompilerParams(has_side_effects=_EFFECT),
    )(*started["src"], *started["land"], *started["sems"], after)
    return outs[n:]


def _gather_forward(name, lands):
    n = len(lands)

    def body(*refs):
        buf = refs[n:2 * n]
        send_sems, recv_sems = refs[2 * n:]
        x, y, c, chips = _place()
        sends, lands_ = [], []
        for i in range(n):
            for k, (px, py) in enumerate(chips):
                b = 2 * px + py
                sends.append(_remote(buf[i].at[b, c], buf[i].at[b, c], send_sems.at[i, k], recv_sems.at[i, k],
                                     (x, y, 1 - c)))
                lands_.append(_remote(buf[i].at[b, 1 - c], buf[i].at[b, 1 - c], send_sems.at[i, k],
                                      recv_sems.at[i, k], (x, y, 1 - c)))
        for cp in sends:
            cp.start()
        for cp in sends:
            cp.wait_send()
        for cp in lands_:
            cp.wait_recv()

    return pl.pallas_call(
        body, name=name, in_specs=[_ANY] * n, out_specs=[_ANY] * n,
        out_shape=[jax.ShapeDtypeStruct(l.shape, l.dtype) for l in lands],
        input_output_aliases={i: i for i in range(n)},
        scratch_shapes=[pltpu.SemaphoreType.DMA((n, 3)), pltpu.SemaphoreType.DMA((n, 3))],
    )(*lands)


def _pair_exchange(name, parts):
    n = len(parts)

    def body(*refs):
        src = refs[:n]
        dst = refs[n:2 * n]
        send_sems, recv_sems = refs[2 * n:]
        x, y, c, _ = _place()
        copies = [pltpu.make_async_remote_copy(
            src_ref=src[i].at[:, 1 - c], dst_ref=dst[i], send_sem=send_sems.at[i], recv_sem=recv_sems.at[i],
            device_id=(x, y, 1 - c), device_id_type=MESH) for i in range(n)]
        for cp in copies:
            cp.start()
        for cp in copies:
            cp.wait()

    return pl.pallas_call(
        body, name=name, in_specs=[_ANY] * n, out_specs=[_ANY] * n,
        out_shape=[jax.ShapeDtypeStruct((p.shape[0],) + p.shape[2:], p.dtype) for p in parts],
        scratch_shapes=[pltpu.SemaphoreType.DMA((n,)), pltpu.SemaphoreType.DMA((n,))],
    )(*parts)


def _chip_copies(src, land, send_sems, recv_sems):
    x, y, c, chips = _place()
    return [_remote(src[i].at[2 * px + py], land[i].at[k], send_sems.at[3 * i + k], recv_sems.at[3 * i + k],
                    (px, py, c))
            for i in range(len(src)) for k, (px, py) in enumerate(chips)]


def _chip_start(name, parts):
    n = len(parts)
    lands = [lax.empty((3,) + p.shape[1:], p.dtype) for p in parts]

    def body(*refs):
        src, land = refs[:n], refs[n:2 * n]
        send_sems, recv_sems = refs[2 * n:2 * n + 2]
        token = refs[-1]
        for cp in _chip_copies(src, land, send_sems, recv_sems):
            cp.start()
        token[...] = jnp.zeros_like(token)

    outs = pl.pallas_call(
        body, name=name,
        out_shape=(pltpu.SemaphoreType.DMA((3 * n,)), pltpu.SemaphoreType.DMA((3 * n,)),
                   *[pltpu.HBM(p.shape, p.dtype) for p in parts], *[pltpu.HBM(l.shape, l.dtype) for l in lands],
                   jax.ShapeDtypeStruct((8, 128), F32)),
        in_specs=[_HBM] * (2 * n), out_specs=(_SEM,) * 2 + (_HBM,) * (2 * n) + (_VMEM,),
        input_output_aliases={i: 2 + i for i in range(2 * n)},
        compiler_params=pltpu.CompilerParams(has_side_effects=_EFFECT),
    )(*[_in_hbm(p) for p in parts], *[_in_hbm(l) for l in lands])
    return dict(sems=outs[:2], src=outs[2:2 + n], land=outs[2 + n:2 + 2 * n], token=outs[-1])


def _chip_wait(name, started, after):
    n = len(started["src"])

    def body(*refs):
        src, land = refs[:n], refs[n:2 * n]
        send_sems, recv_sems = refs[2 * n:2 * n + 2]
        copies = _chip_copies(src, land, send_sems, recv_sems)
        for cp in copies:
            cp.wait_send()
        for cp in copies:
            cp.wait_recv()

    outs = pl.pallas_call(
        body, name=name,
        out_shape=[pltpu.HBM(v.shape, v.dtype) for v in list(started["src"]) + list(started["land"])],
        in_specs=[_HBM] * (2 * n) + [_SEM] * 2 + [_ANY], out_specs=[_HBM] * (2 * n),
        input_output_aliases={i: i for i in range(2 * n)},
        compiler_params=pltpu.CompilerParams(has_side_effects=_EFFECT),
    )(*started["src"], *started["land"], *started["sems"], after)
    return outs[:n], outs[n:]


def _half_exchange(name, shards):
    n = len(shards)

    def body(*refs):
        buf = refs[n:2 * n]
        send_sems, recv_sems = refs[2 * n:]
        x, y, c, _ = _place()
        copies = [pltpu.make_async_remote_copy(
            src_ref=buf[i].at[c], dst_ref=buf[i].at[c], send_sem=send_sems.at[i], recv_sem=recv_sems.at[i],
            device_id=(x, y, 1 - c), device_id_type=MESH) for i in range(n)]
        for cp in copies:
            cp.start()
        for cp in copies:
            cp.wait()

    return pl.pallas_call(
        body, name=name, in_specs=[_ANY] * n, out_specs=[_ANY] * n,
        out_shape=[jax.ShapeDtypeStruct(s.shape, s.dtype) for s in shards],
        input_output_aliases={i: i for i in range(n)},
        scratch_shapes=[pltpu.SemaphoreType.DMA((n,)), pltpu.SemaphoreType.DMA((n,))],
    )(*shards)


def _pair_add(name, pos, part, sib):
    Q, _, R2, C = part.shape
    tr = _pick(R2, 256)

    def body(pos_ref, p_ref, s_ref, o_ref):
        o_ref[...] = (p_ref[...].astype(F32) + s_ref[...].astype(F32)).astype(BF16)

    return pl.pallas_call(
        body, name=name,
        grid_spec=pltpu.PrefetchScalarGridSpec(
            num_scalar_prefetch=1, grid=(Q, R2 // tr),
            in_specs=[pl.BlockSpec((None, None, tr, C), lambda q, i, pos_ref: (q, pos_ref[1], i, 0)),
                      pl.BlockSpec((None, tr, C), lambda q, i, pos_ref: (q, i, 0))],
            out_specs=pl.BlockSpec((None, tr, C), lambda q, i, pos_ref: (q, i, 0))),
        out_shape=jax.ShapeDtypeStruct((Q, R2, C), BF16),
        compiler_params=_cp("parallel", "parallel"),
    )(pos, part, sib)


def _chip_sum(name, pos, own, got):
    _, R2, C = own.shape
    tr = _pick(R2, 256)

    def body(pos_ref, o_ref, g_ref, out_ref):
        acc = o_ref[...].astype(F32)
        for k in range(3):
            acc = acc + g_ref[k].astype(F32)
        out_ref[...] = acc

    return pl.pallas_call(
        body, name=name,
        grid_spec=pltpu.PrefetchScalarGridSpec(
            num_scalar_prefetch=1, grid=(R2 // tr,),
            in_specs=[pl.BlockSpec((None, tr, C), lambda i, pos_ref: (pos_ref[0], i, 0)),
                      pl.BlockSpec((3, tr, C), lambda i, pos_ref: (0, i, 0))],
            out_specs=pl.BlockSpec((None, tr, C), lambda i, pos_ref: (pos_ref[1], i, 0))),
        out_shape=jax.ShapeDtypeStruct((2, R2, C), F32),
        compiler_params=_cp("parallel"),
    )(pos, own, got)


def _reduce_begin(tag, pos, parts):
    split = [p.reshape(p.shape[0], 2, p.shape[1] // 2, p.shape[2]) for p in parts]
    sib = _pair_exchange(tag + "_pair", split)
    chip = [_pair_add(f"{tag}_add{i}", pos, split[i], sib[i]) for i in range(len(parts))]
    return _chip_start(tag + "_chip_start", chip)


def _reduce_end(tag, pos, started, after, shapes):
    chip, got = _chip_wait(tag + "_chip_wait", started, after)
    halves = [_chip_sum(f"{tag}_sum{i}", pos, chip[i], got[i]) for i in range(len(chip))]
    full = _half_exchange(tag + "_half", halves)
    return [f.reshape(shp) for f, shp in zip(full, shapes)]


def _pad_to(v, mult):
    n = v.shape[0]
    return jnp.pad(v, (0, (-n) % mult))


def kernel(x, p, pool_w, pool_scale, ssm_in_w, ssm_conv_w, ssm_conv_b, ssm_dt_bias, ssm_a_log, ssm_d, ssm_norm_w, ssm_out_w, mlp_w1, mlp_w2, ln_g, ln_b, ple_w, ple_gate_w, loss_target, m_pool_w, m_pool_scale, m_ssm_in_w, m_ssm_conv_w, m_ssm_conv_b, m_ssm_dt_bias, m_ssm_a_log, m_ssm_d, m_ssm_norm_w, m_ssm_out_w, m_mlp_w1, m_mlp_w2, m_ln_g, m_ln_b, m_ple_w, m_ple_gate_w, v_pool_w, v_pool_scale, v_ssm_in_w, v_ssm_conv_w, v_ssm_conv_b, v_ssm_dt_bias, v_ssm_a_log, v_ssm_d, v_ssm_norm_w, v_ssm_out_w, v_mlp_w1, v_mlp_w2, v_ln_g, v_ln_b, v_ple_w, v_ple_gate_w):
    T, D = x.shape[1], x.shape[2]
    NG = len(POOL_WINDOWS)
    GD = D // NG
    DI = ssm_out_w.shape[1] * N_CHIPS
    H = ssm_dt_bias.shape[1]
    HPG = H // N_GROUPS
    GW = DI // N_GROUPS
    GS = N_GROUPS * D_STATE
    CD = DI + 2 * GS
    DF = mlp_w1.shape[2] * N_CHIPS
    PD = ple_w.shape[1]
    NIN = ssm_in_w.shape[2]

    xi, yi, ci = lax.axis_index("x"), lax.axis_index("y"), lax.axis_index("c")
    chip = 2 * xi + yi
    pos = jnp.stack([chip, ci]).astype(jnp.int32)

    x0 = x[0]
    tgt = loss_target[0]
    p0b, p1b = p[0, 0].astype(BF16), p[1, 0].astype(BF16)

    def halves(w):
        return w.astype(BF16).reshape((2, w.shape[0] // 2) + w.shape[1:])

    sh_pool = pool_w[0].astype(BF16)
    sh_pool = sh_pool.reshape((2, NG // 2) + sh_pool.shape[1:])
    groups = [("ag0a", [sh_pool, halves(mlp_w1[0])]),
              ("ag0b", [halves(mlp_w2[0]), halves(ple_gate_w[0]), halves(ple_w[0])]),
              ("ag1a", [halves(ssm_in_w[0])]),
              ("ag1b", [halves(ssm_out_w[0]), halves(mlp_w1[1])]),
              ("ag1c", [halves(mlp_w2[1]), halves(ple_gate_w[1]), halves(ple_w[1])])]
    started, prev = {}, None
    for tag, shards in groups:
        started[tag] = _gather_start(tag + "_start", shards, after=prev)
        prev = started[tag]["token"]
    all_started = prev

    def gather_end(tag, after):
        return _gather_forward(tag + "_fwd", _gather_wait(tag + "_wait", started[tag], after))

    g_w1, g_w2, g_pw, g_gw = {}, {}, {}, {}

    def set_w1(l, g):
        g_w1[l] = g.reshape(N_CHIPS, 1, D, DF // N_CHIPS)

    def set_rest(l, g2, ggw, gpw):
        g_w2[l] = g2.reshape(N_CHIPS, 1, DF // N_CHIPS, D)
        g_gw[l] = ggw.reshape(N_CHIPS, 1, D // N_CHIPS, D)
        g_pw[l] = gpw.reshape(N_CHIPS, 1, PD, D // N_CHIPS)

    g_pool, g = gather_end("ag0a", all_started)
    set_w1(0, g)
    w_pool = jnp.transpose(g_pool.reshape(N_CHIPS, NG, GD // N_CHIPS, GD), (1, 0, 2, 3)).reshape(NG, GD, GD)

    small_sh = jnp.concatenate([ssm_conv_w[0].reshape(-1), ssm_conv_b[0], ssm_norm_w[0],
                                ln_g.reshape(-1), ln_b.reshape(-1)])
    n_sh = small_sh.shape[0]
    small_all = _allgather_small("gather_small", _pad_to(small_sh, 1024).reshape(-1, 128))
    small_all = small_all.reshape(N_DEV, -1)[0::2, :n_sh]
    cdq, niq, dq = CD // N_CHIPS, DI // N_CHIPS, D // N_CHIPS
    o = 0
    conv_w = jnp.concatenate([small_all[q, o:o + CONV_K * cdq].reshape(CONV_K, cdq) for q in range(N_CHIPS)], axis=1)
    o += CONV_K * cdq
    conv_b = small_all[:, o:o + cdq].reshape(1, CD)
    o += cdq
    norm_w = small_all[:, o:o + niq].reshape(1, DI)
    o += niq
    lng = jnp.transpose(small_all[:, o:o + 4 * dq].reshape(N_CHIPS, 2, 2, dq), (1, 2, 0, 3)).reshape(2, 2, 1, D)
    o += 4 * dq
    lnb = jnp.transpose(small_all[:, o:o + 4 * dq].reshape(N_CHIPS, 2, 2, dq), (1, 2, 0, 3)).reshape(2, 2, 1, D)

    pooled = _pool_fwd("pool_fwd", x0)
    hraw = _mm("pool_mm", "nn", pooled, w_pool, T, D, GD, tn=GD,
               a_spec=lambda tm, tn, tk: pl.BlockSpec((tm, tk), lambda i, j, k: (i, j)),
               b_spec=lambda tm, tn, tk: pl.BlockSpec((None, tk, tn), lambda i, j, k: (j, 0, 0)))
    x1, x1b, xh1, rs1 = _res_ln("ln00", x0, hraw, lng[0, 0], lnb[0, 0], scale=pool_scale)

    def mlp_fwd(l, xb, rest_tag):
        a, h2b = _mm(f"mlp{l}_up", "nn", xb, g_w1[l], T, DF, D, tn=min(1024, DF // N_CHIPS),
                     b_spec=_colshard_b(0, DF // N_CHIPS), out_dtype=(F32, BF16),
                     epi=lambda acc: (acc, jnp.square(jnp.maximum(acc, 0.0))))
        set_rest(l, *gather_end(rest_tag, a))
        h = _mm(f"mlp{l}_down", "nn", h2b, g_w2[l], T, D, DF, k_unit=DF // N_CHIPS,
                b_spec=_rowshard_b(0, DF // N_CHIPS))
        return a, h2b, h

    def ple_fwd(l, xb, pb):
        gl = _mm(f"gate{l}_logit", "nn", xb, g_gw[l], T, D, D, k_unit=D // N_CHIPS,
                 b_spec=_rowshard_b(0, D // N_CHIPS))
        e = _mm(f"gate{l}_emb", "nn", pb, g_pw[l], T, D, PD, tn=min(1024, D // N_CHIPS),
                b_spec=_colshard_b(0, D // N_CHIPS))
        return gl, e

    a0, h2b0, h0 = mlp_fwd(0, x1b, "ag0b")
    x2, x2b, xh2, rs2 = _res_ln("ln01", x1, h0, lng[0, 1], lnb[0, 1])
    gl0, e0 = ple_fwd(0, x2b, p0b)
    x3, x3b, gate0 = _gate_fwd("gate0", x2, gl0, e0)

    g_in, = gather_end("ag1a", x3b)
    w_in = jnp.concatenate([g_in[q].reshape(D, NIN) for q in range(N_CHIPS)], axis=1)
    w_z, w_xbc, w_dt = w_in[:, :DI], w_in[:, DI:DI + CD], w_in[:, DI + CD:]
    z = _mm("ssm_in_z", "nn", x3b, w_z, T, DI, D)
    xbc_pre = _mm("ssm_in_xbc", "nn", x3b, w_xbc, T, CD, D)
    dt_pre = _mm("ssm_in_dt", "nn", x3b, w_dt, T, H, D)
    xbc = _conv_fwd("conv_fwd", xbc_pre, conv_w, conv_b)
    dt = _dt_fwd("dt_fwd", dt_pre, ssm_dt_bias)
    dt_g = jnp.transpose(dt.reshape(T, N_GROUPS, HPG), (1, 0, 2))
    dtc = jnp.pad(dt_g, ((0, 0), (0, 0), (0, HEAD_PAD - HPG)))
    dtr = jnp.pad(jnp.transpose(dt_g, (0, 2, 1)), ((0, 0), (0, ROW_PAD - HPG), (0, 0)))
    al_g = ssm_a_log.reshape(N_GROUPS, HPG)
    alr = jnp.pad(al_g, ((0, 0), (0, HEAD_PAD - HPG)))[:, None, :]
    alc = jnp.pad(al_g, ((0, 0), (0, ROW_PAD - HPG)))[:, :, None]
    d_e = jnp.repeat(ssm_d.reshape(N_GROUPS, HPG), HEAD_DIM, axis=1)[:, None, :]
    ysc, hprev = _ssd_fwd("ssd_fwd", xbc, dtc, dtr, alr, alc, d_e, DI)
    g_out, g = gather_end("ag1b", ysc)
    set_w1(1, g)
    w_out = g_out.reshape(DI, D)
    ynb, rsn = _gnorm_fwd("gnorm_fwd", ysc, z, norm_w)
    h1 = _mm("ssm_out", "nn", ynb, w_out, T, D, DI)
    x4, x4b, xh4, rs4 = _res_ln("ln10", x3, h1, lng[1, 0], lnb[1, 0])
    a1, h2b1, h2 = mlp_fwd(1, x4b, "ag1c")
    x5, x5b, xh5, rs5 = _res_ln("ln11", x4, h2, lng[1, 1], lnb[1, 1])
    gl1, e1 = ple_fwd(1, x5b, p1b)
    dx6, gate1, loss_parts = _gate_loss("gate1_loss", x5, gl1, e1, tgt)
    loss_local = jnp.sum(loss_parts[0::8, 0])

    def ple_bwd(l, dxo, gate, e, xb, pb, deps=()):
        dgl, de = _gate_bwd(f"gate{l}_bwd", dxo, gate, e)
        d_gw = _mm(f"gate{l}_dw", "tn", xb, dgl, D, D, T, out_dtype=BF16, deps=deps).reshape(N_CHIPS, D // N_CHIPS, D)
        d_pw = _mm(f"gate{l}_dpw", "tn", pb, de, PD, D, T, out_dtype=BF16, tn=min(1024, D // N_CHIPS),
                   o_shape=(N_CHIPS, PD, D // N_CHIPS), o_spec=_colshard_o(D // N_CHIPS))
        dx = _mm(f"gate{l}_dx", "nt", dgl, g_gw[l], T, D, D, tn=min(1024, D // N_CHIPS), deps=deps,
                 b_spec=_rowshard_bt(0, D // N_CHIPS), epi=lambda acc, r: acc + r, extras=[dxo])
        return dx, d_gw, d_pw

    def mlp_bwd(l, du, dub, a, h2b, xb):
        d_w2 = _mm(f"mlp{l}_dw2", "tn", h2b, dub, DF, D, T, out_dtype=BF16).reshape(N_CHIPS, DF // N_CHIPS, D)
        da = _mm(f"mlp{l}_da", "nt", dub, g_w2[l], T, DF, D, tn=min(1024, DF // N_CHIPS),
                 b_spec=_rowshard_bt(0, DF // N_CHIPS), out_dtype=BF16,
                 epi=lambda acc, av: acc * (2.0 * jnp.maximum(av, 0.0)), extras=[a])
        d_w1 = _mm(f"mlp{l}_dw1", "tn", xb, da, D, DF, T, out_dtype=BF16, tn=min(1024, DF // N_CHIPS),
                   o_shape=(N_CHIPS, D, DF // N_CHIPS), o_spec=_colshard_o(DF // N_CHIPS))
        dx = _mm(f"mlp{l}_dx", "nt", da, g_w1[l], T, D, DF, k_unit=DF // N_CHIPS,
                 b_spec=_colshard_bt(0, DF // N_CHIPS), epi=lambda acc, r: acc + ALPHA * r, extras=[du])
        return dx, d_w1, d_w2

    dx5, d_gw1, d_pw1 = ple_bwd(1, dx6, gate1, e1, x5b, p1b)
    du5, du5b, dg11, db11 = _ln_bwd("ln11_bwd", dx5, xh5, rs5, lng[1, 1])
    dx4, d_w1_1, d_w2_1 = mlp_bwd(1, du5, du5b, a1, h2b1, x4b)
    parts1a = [d_w1_1, d_w2_1, d_gw1, d_pw1]
    red1a = _reduce_begin("rs1a", pos, parts1a)
    du4, du4b, dg10, db10 = _ln_bwd("ln10_bwd", dx4, xh4, rs4, lng[1, 0])
    d_wout = _mm("ssm_out_dw", "tn", ynb, du4b, DI, D, T, out_dtype=BF16,
                 deps=[red1a["token"]]).reshape(N_CHIPS, DI // N_CHIPS, D)
    dyn = _mm("ssm_out_dx", "nt", du4b, w_out, T, DI, D, deps=[red1a["token"]])
    dysc, dzb, dnorm_w = _gnorm_bwd("gnorm_bwd", dyn, ysc, z, norm_w, rsn)
    dxs, dbm, dcm, ddt_g, dalog_g, dd_g = _ssd_bwd("ssd_bwd", dysc, xbc, dtc, dtr, alr, alc, d_e, hprev, DI)
    dconv, dconv_w, dconv_b = _conv_bwd_a("conv_bwd_a", dxs, dbm, dcm, xbc_pre, conv_w, conv_b)
    dxbcb = _conv_bwd_b("conv_bwd_b", dconv, conv_w)
    ddt = jnp.transpose(ddt_g[:, :, :HPG], (1, 0, 2)).reshape(T, H)
    _, ddtpb, ddt_bias = _dt_bwd("dt_bwd", ddt, dt_pre, ssm_dt_bias)
    da_log = dalog_g[:, 0, :HPG].reshape(1, H)
    dd_skip = dd_g[:, 0, :HPG].reshape(1, H)
    d_wz = _mm("ssm_in_dwz", "tn", x3b, dzb, D, DI, T, out_dtype=BF16)
    d_wx = _mm("ssm_in_dwx", "tn", x3b, dxbcb, D, CD, T, out_dtype=BF16)
    d_wdt = _mm("ssm_in_dwdt", "tn", x3b, ddtpb, D, H, T, out_dtype=BF16)
    d_win = jnp.transpose(jnp.concatenate([d_wz, d_wx, d_wdt], axis=1).reshape(D, N_CHIPS, NIN), (1, 0, 2))
    dx3 = _mm("ssm_in_dxz", "nt", dzb, w_z, T, D, DI, epi=lambda acc, r: acc + ALPHA * r, extras=[du4])
    dx3 = _mm("ssm_in_dxx", "nt", dxbcb, w_xbc, T, D, CD, epi=lambda acc, r: acc + r, extras=[dx3])
    dx3 = _mm("ssm_in_dxdt", "nt", ddtpb, w_dt, T, D, H, epi=lambda acc, r: acc + r, extras=[dx3])

    parts1b = [d_win, d_wout]
    red1b = _reduce_begin("rs1b", pos, parts1b)

    dx2, d_gw0, d_pw0 = ple_bwd(0, dx3, gate0, e0, x2b, p0b, deps=[red1b["token"]])
    du2, du2b, dg01, db01 = _ln_bwd("ln01_bwd", dx2, xh2, rs2, lng[0, 1])
    dx1, d_w1_0, d_w2_0 = mlp_bwd(0, du2, du2b, a0, h2b0, x1b)
    du1, dhrb, dg00, db00, dscale = _ln_bwd("ln00_bwd", dx1, xh1, rs1, lng[0, 0], hraw=hraw, scale=pool_scale)
    d_wpool = _mm("pool_dw", "tn", pooled, dhrb, D, GD, T, tm=GD, tn=GD, out_dtype=BF16,
                  b_spec=lambda tm, tn, tk: pl.BlockSpec((tk, tn), lambda i, j, k: (k, i)),
                  o_shape=(NG, GD, GD), o_spec=lambda tm, tn, tk: pl.BlockSpec((None, tm, tn), lambda i, j, k: (i, 0, 0)))
    dpooled = _mm("pool_dx", "nt", dhrb, w_pool, T, D, GD, tn=GD,
                  a_spec=lambda tm, tn, tk: pl.BlockSpec((tm, tk), lambda i, j, k: (i, j)),
                  b_spec=lambda tm, tn, tk: pl.BlockSpec((None, tn, tk), lambda i, j, k: (j, 0, 0)))
    grad_x = _pool_bwd("pool_bwd", dpooled, du1)
    d_wpool = jnp.transpose(d_wpool.reshape(NG, N_CHIPS, GD // N_CHIPS, GD), (1, 0, 2, 3)).reshape(N_CHIPS, NG * GD // N_CHIPS, GD)

    parts0 = [d_wpool, d_w1_0, d_w2_0, d_gw0, d_pw0]
    red0 = _reduce_begin("rs0", pos, parts0)
    r_w1_1, r_w2_1, r_gw1, r_pw1 = _reduce_end("rs1a", pos, red1a, red0["token"], [p.shape[1:] for p in parts1a])
    r_in, r_out = _reduce_end("rs1b", pos, red1b, r_w1_1, [p.shape[1:] for p in parts1b])

    dln_g = jnp.stack([jnp.stack([dg00, dg01]), jnp.stack([dg10, dg11])]).reshape(-1)
    dln_b = jnp.stack([jnp.stack([db00, db01]), jnp.stack([db10, db11])]).reshape(-1)
    small_g = jnp.concatenate([dscale.reshape(-1), dconv_w.reshape(-1), dconv_b.reshape(-1), ddt_bias.reshape(-1),
                               da_log.reshape(-1), dd_skip.reshape(-1), dnorm_w.reshape(-1), dln_g, dln_b,
                               loss_local.reshape(1)])
    n_sg = small_g.shape[0]
    sg_all = _allgather_small("gather_small_grads", _pad_to(small_g, 1024).reshape(-1, 128))
    sg = _sum_peers("sum_small_grads", sg_all).reshape(-1)[:n_sg]
    o = 0

    def take(nel):
        nonlocal o
        v = sg[o:o + nel]
        o += nel
        return v

    g_scale = take(D).reshape(1, D)
    g_conv_w_full = take(CONV_K * CD).reshape(CONV_K, CD)
    g_conv_b_full = take(CD).reshape(1, CD)
    g_dt_bias = take(H).reshape(1, H)
    g_a_log = take(H).reshape(1, H)
    g_d = take(H).reshape(1, H)
    g_norm_full = take(DI).reshape(1, DI)
    g_lng_full = take(4 * D).reshape(2, 2, D)
    g_lnb_full = take(4 * D).reshape(2, 2, D)
    loss = take(1).reshape(())
    g_conv_w = lax.dynamic_slice_in_dim(g_conv_w_full, chip * cdq, cdq, axis=1)[None]
    g_conv_b = lax.dynamic_slice_in_dim(g_conv_b_full, chip * cdq, cdq, axis=1)
    g_norm = lax.dynamic_slice_in_dim(g_norm_full, chip * niq, niq, axis=1)
    g_lng = lax.dynamic_slice_in_dim(g_lng_full, chip * dq, dq, axis=2)
    g_lnb = lax.dynamic_slice_in_dim(g_lnb_full, chip * dq, dq, axis=2)

    def adam(name, w, m, v, grads):
        shp = w.shape
        L = len(grads)
        w3, m3, v3 = (t.reshape((L, -1, shp[-1])) for t in (w, m, v))
        return [t.reshape(shp) for t in _adamw(name, w3, m3, v3, grads)]

    big = {
        "ssm_in_w": adam("adam_ssm_in_w", ssm_in_w, m_ssm_in_w, v_ssm_in_w, [r_in]),
        "ssm_out_w": adam("adam_ssm_out_w", ssm_out_w, m_ssm_out_w, v_ssm_out_w, [r_out]),
    }
    r_pool, r_w1_0, r_w2_0, r_gw0, r_pw0 = _reduce_end("rs0", pos, red0, big["ssm_in_w"][1],
                                                       [p.shape[1:] for p in parts0])
    big.update({
        "pool_w": adam("adam_pool_w", pool_w, m_pool_w, v_pool_w, [r_pool]),
        "mlp_w1": adam("adam_mlp_w1", mlp_w1, m_mlp_w1, v_mlp_w1, [r_w1_0, r_w1_1]),
        "mlp_w2": adam("adam_mlp_w2", mlp_w2, m_mlp_w2, v_mlp_w2, [r_w2_0, r_w2_1]),
        "ple_w": adam("adam_ple_w", ple_w, m_ple_w, v_ple_w, [r_pw0, r_pw1]),
        "ple_gate_w": adam("adam_ple_gate_w", ple_gate_w, m_ple_gate_w, v_ple_gate_w, [r_gw0, r_gw1]),
    })

    small = [("pool_scale", pool_scale, m_pool_scale, v_pool_scale, g_scale),
             ("ssm_conv_w", ssm_conv_w, m_ssm_conv_w, v_ssm_conv_w, g_conv_w),
             ("ssm_conv_b", ssm_conv_b, m_ssm_conv_b, v_ssm_conv_b, g_conv_b),
             ("ssm_dt_bias", ssm_dt_bias, m_ssm_dt_bias, v_ssm_dt_bias, g_dt_bias),
             ("ssm_a_log", ssm_a_log, m_ssm_a_log, v_ssm_a_log, g_a_log),
             ("ssm_d", ssm_d, m_ssm_d, v_ssm_d, g_d),
             ("ssm_norm_w", ssm_norm_w, m_ssm_norm_w, v_ssm_norm_w, g_norm),
             ("ln_g", ln_g, m_ln_g, v_ln_g, g_lng),
             ("ln_b", ln_b, m_ln_b, v_ln_b, g_lnb)]

    def pack(idx):
        flat = _pad_to(jnp.concatenate([s[idx].reshape(-1) for s in small]), 1024)
        return flat.reshape(1, -1, 128)

    sm_out = _adamw("adam_small", pack(1), pack(2), pack(3), [pack(4)[0]])
    small_res = {}
    o = 0
    for s in small:
        nel = s[1].size
        small_res[s[0]] = [t.reshape(-1)[o:o + nel].reshape(s[1].shape) for t in sm_out]
        o += nel

    order = ["pool_w", "pool_scale", "ssm_in_w", "ssm_conv_w", "ssm_conv_b", "ssm_dt_bias", "ssm_a_log", "ssm_d",
             "ssm_norm_w", "ssm_out_w", "mlp_w1", "mlp_w2", "ln_g", "ln_b", "ple_w", "ple_gate_w"]
    res = {**big, **small_res}
    outs = [loss, grad_x[None]]
    for kind in range(4):
        outs += [res[nm][kind] for nm in order]
    return tuple(outs)
```

```python
import functools

import jax
import jax.numpy as jnp
from jax import lax
from jax.experimental import pallas as pl
from jax.experimental.pallas import tpu as pltpu

F32 = jnp.float32
BF16 = jnp.bfloat16
MESH = pl.DeviceIdType.MESH

DEPTH = 2
ALPHA = (2.0 * DEPTH) ** 0.25
LN_EPS = 1e-5
RMS_EPS = 1e-5
POOL_WINDOWS = (2, 4, 8, 16)
POOL_HALO = 16
HEAD_DIM = 64
HEAD_PAIR = 2 * HEAD_DIM
N_GROUPS = 8
D_STATE = 128
CHUNK = 128
CONV_K = 4
CONV_HALO = 8
HEAD_PAD = 128
ROW_PAD = 8
N_CHIPS = 4
N_DEV = 8
ADAM_LR = 0.001
ADAM_B1 = 0.9
ADAM_B2 = 0.999
ADAM_EPS = 1e-08
ADAM_WD = 0.01
ADAM_STEP = 10
VMEM_LIMIT = 56 * 1024 * 1024
ADAM_BLOCK_BYTES = 1024 * 1024
MM_VMEM_BUDGET = 40 * 1024 * 1024


_ANY = pl.BlockSpec(memory_space=pl.ANY)


def _cp(*sem):
    return pltpu.CompilerParams(dimension_semantics=sem, vmem_limit_bytes=VMEM_LIMIT)


def _pick(dim, pref):
    t = pref
    while t >= 128:
        if dim % t == 0:
            return t
        t //= 2
    return dim


def _rows(rows, row_bytes, budget):
    t = rows
    while t * row_bytes > budget and t % 16 == 0:
        t //= 2
    return t


def _sigmoid(v):
    return 1.0 / (1.0 + jnp.exp(-v))


_DIMS = {"nn": (((1,), (0,)), ((), ())), "nt": (((1,), (1,)), ((), ())), "tn": (((0,), (0,)), ((), ()))}


def _pick_k(k_unit, fixed_bytes, per_k_bytes):
    for n in range(1, k_unit // 128 + 1):
        if k_unit % n == 0 and (n == 1 or (k_unit // n) % 128 == 0):
            if fixed_bytes + (k_unit // n) * per_k_bytes <= MM_VMEM_BUDGET:
                return k_unit // n
    return min(k_unit, 128)


def _mm(name, form, a, b, M, N, K, *, tm=1024, tn=1024, k_unit=None, a_spec=None, b_spec=None,
        o_shape=None, o_spec=None, out_dtype=F32, pro=None, epi=None, extras=(), deps=()):
    tm, tn = _pick(M, tm), _pick(N, tn)
    out_dtypes = out_dtype if isinstance(out_dtype, tuple) else (out_dtype,)
    n_out = len(out_dtypes)
    in_place = n_out == 1 and out_dtypes[0] == F32
    fixed = 2 * tm * tn * (sum(jnp.dtype(d).itemsize for d in out_dtypes) + 4 * len(extras))
    fixed += 0 if in_place else 4 * tm * tn
    tk = _pick_k(K if k_unit is None else k_unit, fixed,
                 2 * (tm * a.dtype.itemsize + tn * b.dtype.itemsize))
    nk = K // tk
    if a_spec is None:
        a_spec = (pl.BlockSpec((tk, tm), lambda i, j, k: (k, i)) if form == "tn"
                  else pl.BlockSpec((tm, tk), lambda i, j, k: (i, k)))
    else:
        a_spec = a_spec(tm, tn, tk)
    if b_spec is None:
        b_spec = (pl.BlockSpec((tn, tk), lambda i, j, k: (j, k)) if form == "nt"
                  else pl.BlockSpec((tk, tn), lambda i, j, k: (k, j)))
    else:
        b_spec = b_spec(tm, tn, tk)
    if o_spec is None:
        o_spec = pl.BlockSpec((tm, tn), lambda i, j, k: (i, j))
        o_shape = (M, N)
    else:
        o_spec = o_spec(tm, tn, tk)
    ex_arrays = [e for e in extras]
    ex_specs = [pl.BlockSpec((tm, tn), lambda i, j, k: (i, j)) for _ in extras]
    ne = len(ex_arrays)
    nd = len(deps)
    dep_specs = [pl.BlockSpec((8, 128), lambda i, j, k: (0, 0)) for _ in deps]
    dims = _DIMS[form]
    use_scratch = nk > 1 and not in_place

    def body(a_ref, b_ref, *rest):
        ex_refs = rest[:ne]
        o_refs = rest[ne + nd:ne + nd + n_out]
        at = a_ref[...]
        if pro is not None:
            at = pro(at)
        p = lax.dot_general(at.astype(BF16), b_ref[...].astype(BF16), dims, preferred_element_type=F32)

        def finish(acc):
            res = acc if epi is None else epi(acc, *[r[...] for r in ex_refs])
            res = res if isinstance(res, tuple) else (res,)
            for o_ref, r, d in zip(o_refs, res, out_dtypes):
                o_ref[...] = r.astype(d)

        if nk == 1:
            finish(p)
        else:
            acc_ref = rest[ne + nd + n_out] if use_scratch else o_refs[0]
            k = pl.program_id(2)

            @pl.when(k == 0)
            def _():
                acc_ref[...] = p

            @pl.when(jnp.logical_and(k > 0, k < nk - 1))
            def _():
                acc_ref[...] += p

            @pl.when(k == nk - 1)
            def _():
                finish(acc_ref[...] + p)

    res = pl.pallas_call(
        body, name=name, grid=(M // tm, N // tn, nk),
        in_specs=[a_spec, b_spec] + ex_specs + dep_specs, out_specs=[o_spec] * n_out,
        out_shape=[jax.ShapeDtypeStruct(o_shape, d) for d in out_dtypes],
        scratch_shapes=[pltpu.VMEM((tm, tn), F32)] if use_scratch else [],
        compiler_params=_cp("parallel", "parallel", "arbitrary"),
    )(a, b, *ex_arrays, *deps)
    return res if n_out > 1 else res[0]


def _colshard_b(l, n_per):
    def make(tm, tn, tk):
        nb = n_per // tn
        return pl.BlockSpec((None, None, tk, tn), lambda i, j, k: (j // nb, l, k, j % nb))
    return make


def _colshard_bt(l, n_per):
    def make(tm, tn, tk):
        nb = n_per // tk
        return pl.BlockSpec((None, None, tn, tk), lambda i, j, k: (k // nb, l, j, k % nb))
    return make


def _colshard_o(n_per):
    def make(tm, tn, tk):
        nb = n_per // tn
        return pl.BlockSpec((None, tm, tn), lambda i, j, k: (j // nb, i, j % nb))
    return make


def _rowshard_b(l, k_per):
    def make(tm, tn, tk):
        nb = k_per // tk
        return pl.BlockSpec((None, None, tk, tn), lambda i, j, k: (k // nb, l, k % nb, j))
    return make


def _rowshard_bt(l, k_per):
    def make(tm, tn, tk):
        nb = k_per // tn
        return pl.BlockSpec((None, None, tn, tk), lambda i, j, k: (j // nb, l, j % nb, k))
    return make


def _res_ln(name, xprev, h, g, b, scale=None):
    T, D = xprev.shape
    tr = _pick(T, 256)
    row = pl.BlockSpec((tr, D), lambda i: (i, 0))
    vec = pl.BlockSpec((1, D), lambda i: (0, 0))
    has_scale = scale is not None

    def body(*refs):
        if has_scale:
            x_ref, h_ref, s_ref, g_ref, b_ref, y_ref, yb_ref, xh_ref, rs_ref = refs
            hh = h_ref[...] * s_ref[...]
        else:
            x_ref, h_ref, g_ref, b_ref, y_ref, yb_ref, xh_ref, rs_ref = refs
            hh = h_ref[...]
        u = ALPHA * x_ref[...] + hh
        mu = jnp.mean(u, axis=-1, keepdims=True)
        d = u - mu
        var = jnp.mean(d * d, axis=-1, keepdims=True)
        rs = lax.rsqrt(var + LN_EPS)
        xh = d * rs
        y = xh * g_ref[...] + b_ref[...]
        y_ref[...] = y
        yb_ref[...] = y.astype(BF16)
        xh_ref[...] = xh
        rs_ref[...] = rs

    ins = [xprev, h] + ([scale] if has_scale else []) + [g, b]
    specs = [row, row] + ([vec] if has_scale else []) + [vec, vec]
    return pl.pallas_call(
        body, name=name, grid=(T // tr,), in_specs=specs,
        out_specs=[row, row, row, pl.BlockSpec((tr, 1), lambda i: (i, 0))],
        out_shape=[jax.ShapeDtypeStruct((T, D), F32), jax.ShapeDtypeStruct((T, D), BF16),
                   jax.ShapeDtypeStruct((T, D), F32), jax.ShapeDtypeStruct((T, 1), F32)],
        compiler_params=_cp("parallel"),
    )(*ins)


def _accum(ref, part, first):
    @pl.when(first)
    def _():
        ref[...] = part

    @pl.when(jnp.logical_not(first))
    def _():
        ref[...] += part


def _ln_bwd(name, dy, xh, rs, g, hraw=None, scale=None):
    T, D = dy.shape
    tr = _pick(T, 256)
    row = pl.BlockSpec((tr, D), lambda i: (i, 0))
    vec = pl.BlockSpec((1, D), lambda i: (0, 0))
    has_scale = scale is not None

    def body(*refs):
        if has_scale:
            dy_ref, xh_ref, rs_ref, g_ref, hr_ref, s_ref, du_ref, dub_ref, dg_ref, db_ref, ds_ref = refs
        else:
            dy_ref, xh_ref, rs_ref, g_ref, du_ref, dub_ref, dg_ref, db_ref = refs
        first = pl.program_id(0) == 0
        dyv = dy_ref[...]
        xhv = xh_ref[...]
        dxh = dyv * g_ref[...]
        m1 = jnp.mean(dxh, axis=-1, keepdims=True)
        m2 = jnp.mean(dxh * xhv, axis=-1, keepdims=True)
        du = rs_ref[...] * (dxh - m1 - xhv * m2)
        du_ref[...] = du
        if has_scale:
            dub_ref[...] = (du * s_ref[...]).astype(BF16)
            _accum(ds_ref, jnp.sum(du * hr_ref[...], axis=0, keepdims=True), first)
        else:
            dub_ref[...] = du.astype(BF16)
        _accum(dg_ref, jnp.sum(dyv * xhv, axis=0, keepdims=True), first)
        _accum(db_ref, jnp.sum(dyv, axis=0, keepdims=True), first)

    ins = [dy, xh, rs, g] + ([hraw, scale] if has_scale else [])
    specs = [row, row, pl.BlockSpec((tr, 1), lambda i: (i, 0)), vec] + ([row, vec] if has_scale else [])
    n_vec = 3 if has_scale else 2
    return pl.pallas_call(
        body, name=name, grid=(T // tr,), in_specs=specs,
        out_specs=[row, row] + [vec] * n_vec,
        out_shape=[jax.ShapeDtypeStruct((T, D), F32), jax.ShapeDtypeStruct((T, D), BF16)]
        + [jax.ShapeDtypeStruct((1, D), F32)] * n_vec,
        compiler_params=_cp("arbitrary"),
    )(*ins)


def _gate_fwd(name, x, gl, e):
    T, D = x.shape
    tr = _pick(T, 256)
    row = pl.BlockSpec((tr, D), lambda i: (i, 0))

    def body(x_ref, gl_ref, e_ref, xo_ref, xob_ref, gate_ref):
        gate = _sigmoid(gl_ref[...])
        xo = x_ref[...] + gate * e_ref[...]
        xo_ref[...] = xo
        xob_ref[...] = xo.astype(BF16)
        gate_ref[...] = gate

    return pl.pallas_call(
        body, name=name, grid=(T // tr,), in_specs=[row, row, row], out_specs=[row, row, row],
        out_shape=[jax.ShapeDtypeStruct((T, D), F32), jax.ShapeDtypeStruct((T, D), BF16),
                   jax.ShapeDtypeStruct((T, D), F32)],
        compiler_params=_cp("parallel"),
    )(x, gl, e)


def _gate_loss(name, x, gl, e, tgt):
    T, D = x.shape
    tr = _pick(T, 256)
    row = pl.BlockSpec((tr, D), lambda i: (i, 0))

    def body(x_ref, gl_ref, e_ref, t_ref, dy_ref, gate_ref, lp_ref):
        gate = _sigmoid(gl_ref[...])
        err = x_ref[...] + gate * e_ref[...] - t_ref[...]
        dy_ref[...] = err * (1.0 / D)
        gate_ref[...] = gate
        s = jnp.sum(jnp.mean(err * err, axis=-1, keepdims=True), axis=0, keepdims=True)
        lp_ref[...] = jnp.broadcast_to(0.5 * s, (8, 128))

    return pl.pallas_call(
        body, name=name, grid=(T // tr,), in_specs=[row] * 4,
        out_specs=[row, row, pl.BlockSpec((8, 128), lambda i: (i, 0))],
        out_shape=[jax.ShapeDtypeStruct((T, D), F32), jax.ShapeDtypeStruct((T, D), F32),
                   jax.ShapeDtypeStruct((T // tr * 8, 128), F32)],
        compiler_params=_cp("parallel"),
    )(x, gl, e, tgt)


def _gate_bwd(name, dxo, gate, e):
    T, D = dxo.shape
    tr = _pick(T, 256)
    row = pl.BlockSpec((tr, D), lambda i: (i, 0))

    def body(d_ref, gate_ref, e_ref, dgl_ref, de_ref):
        d = d_ref[...]
        gate = gate_ref[...]
        dgl_ref[...] = (d * e_ref[...] * gate * (1.0 - gate)).astype(BF16)
        de_ref[...] = (d * gate).astype(BF16)

    return pl.pallas_call(
        body, name=name, grid=(T // tr,), in_specs=[row] * 3, out_specs=[row, row],
        out_shape=[jax.ShapeDtypeStruct((T, D), BF16)] * 2,
        compiler_params=_cp("parallel"),
    )(dxo, gate, e)


def _gnorm_fwd(name, y, z, w):
    T, DI = y.shape
    tr = _pick(T, 128)
    row = pl.BlockSpec((tr, DI), lambda i: (i, 0))
    vec = pl.BlockSpec((1, DI), lambda i: (0, 0))
    col = pl.BlockSpec((tr, 1), lambda i: (i, 0))

    def body(y_ref, z_ref, w_ref, o_ref, rs_ref):
        zv = z_ref[...]
        yz = y_ref[...] * (zv * _sigmoid(zv))
        rs = lax.rsqrt(jnp.mean(yz * yz, axis=-1, keepdims=True) + RMS_EPS)
        o_ref[...] = (yz * rs * w_ref[...]).astype(BF16)
        rs_ref[...] = rs

    return pl.pallas_call(
        body, name=name, grid=(T // tr,), in_specs=[row, row, vec], out_specs=[row, col],
        out_shape=[jax.ShapeDtypeStruct((T, DI), BF16), jax.ShapeDtypeStruct((T, 1), F32)],
        compiler_params=_cp("parallel"),
    )(y, z, w)


def _gnorm_bwd(name, dyn, y, z, w, rs):
    T, DI = y.shape
    tr = _pick(T, 128)
    row = pl.BlockSpec((tr, DI), lambda i: (i, 0))
    vec = pl.BlockSpec((1, DI), lambda i: (0, 0))
    col = pl.BlockSpec((tr, 1), lambda i: (i, 0))

    def body(d_ref, y_ref, z_ref, w_ref, rs_ref, dy_ref, dz_ref, dw_ref):
        first = pl.program_id(0) == 0
        zv = z_ref[...]
        yv = y_ref[...]
        sg = _sigmoid(zv)
        sz = zv * sg
        rsv = rs_ref[...]
        yzh = yv * sz * rsv
        dv = d_ref[...]
        gw = dv * w_ref[...]
        m = jnp.mean(gw * yzh, axis=-1, keepdims=True)
        dyz = rsv * (gw - yzh * m)
        dy_ref[...] = dyz * sz
        dz_ref[...] = (dyz * yv * (sg * (1.0 + zv * (1.0 - sg)))).astype(BF16)
        _accum(dw_ref, jnp.sum(dv * yzh, axis=0, keepdims=True), first)

    return pl.pallas_call(
        body, name=name, grid=(T // tr,), in_specs=[row, row, row, vec, col], out_specs=[row, row, vec],
        out_shape=[jax.ShapeDtypeStruct((T, DI), F32), jax.ShapeDtypeStruct((T, DI), BF16),
                   jax.ShapeDtypeStruct((1, DI), F32)],
        compiler_params=_cp("arbitrary"),
    )(dyn, y, z, w, rs)


def _sel4(j, vals):
    return jnp.where(j == 0, vals[0], jnp.where(j == 1, vals[1], jnp.where(j == 2, vals[2], vals[3])))


def _pool_cnt(i, j, tr, rows, offset):
    t = i * tr + offset + lax.broadcasted_iota(jnp.int32, (rows, 1), 0)
    win = _sel4(j, POOL_WINDOWS)
    return jnp.minimum(t + 1, win).astype(F32)


def _pool_fwd(name, x):
    T, D = x.shape
    gd = D // len(POOL_WINDOWS)
    tr = _pick(T, 512)
    hb = tr // POOL_HALO

    def body(x_ref, h_ref, o_ref):
        i, j = pl.program_id(0), pl.program_id(1)
        xv = x_ref[...]
        halo = jnp.where(i > 0, h_ref[...], 0.0)
        cat = jnp.concatenate([halo, xv], axis=0)
        s2 = cat + pltpu.roll(cat, 1, 0)
        s4 = s2 + pltpu.roll(s2, 2, 0)
        s8 = s4 + pltpu.roll(s4, 4, 0)
        s16 = s8 + pltpu.roll(s8, 8, 0)
        sel = _sel4(j, (s2, s4, s8, s16))[POOL_HALO:]
        o_ref[...] = (sel / _pool_cnt(i, j, tr, tr, 0) - xv).astype(BF16)

    return pl.pallas_call(
        body, name=name, grid=(T // tr, len(POOL_WINDOWS)),
        in_specs=[pl.BlockSpec((tr, gd), lambda i, j: (i, j)),
                  pl.BlockSpec((POOL_HALO, gd), lambda i, j: (jnp.maximum(i * hb - 1, 0), j))],
        out_specs=pl.BlockSpec((tr, gd), lambda i, j: (i, j)),
        out_shape=jax.ShapeDtypeStruct((T, D), BF16),
        compiler_params=_cp("parallel", "parallel"),
    )(x, x)


def _pool_bwd(name, dp, du):
    T, D = dp.shape
    gd = D // len(POOL_WINDOWS)
    tr = _pick(T, 512)
    hb = tr // POOL_HALO
    last_h = T // POOL_HALO - 1
    n = tr + POOL_HALO

    def body(dp_ref, h_ref, du_ref, o_ref):
        i, j = pl.program_id(0), pl.program_id(1)
        dpv = dp_ref[...]
        q = dpv / _pool_cnt(i, j, tr, tr, 0)
        qh = jnp.where(i < pl.num_programs(0) - 1, h_ref[...] / _pool_cnt(i, j, tr, POOL_HALO, tr), 0.0)
        cat = jnp.concatenate([q, qh], axis=0)
        f2 = cat + pltpu.roll(cat, n - 1, 0)
        f4 = f2 + pltpu.roll(f2, n - 2, 0)
        f8 = f4 + pltpu.roll(f4, n - 4, 0)
        f16 = f8 + pltpu.roll(f8, n - 8, 0)
        sel = _sel4(j, (f2, f4, f8, f16))[:tr]
        o_ref[...] = ALPHA * du_ref[...] + sel - dpv

    return pl.pallas_call(
        body, name=name, grid=(T // tr, len(POOL_WINDOWS)),
        in_specs=[pl.BlockSpec((tr, gd), lambda i, j: (i, j)),
                  pl.BlockSpec((POOL_HALO, gd), lambda i, j: (jnp.minimum((i + 1) * hb, last_h), j)),
                  pl.BlockSpec((tr, gd), lambda i, j: (i, j))],
        out_specs=pl.BlockSpec((tr, gd), lambda i, j: (i, j)),
        out_shape=jax.ShapeDtypeStruct((T, D), F32),
        compiler_params=_cp("parallel", "parallel"),
    )(dp, dp, du)


def _conv_taps(cat, wv, rows):
    shifted = [cat[CONV_HALO:] if s == 0 else pltpu.roll(cat, s, 0)[CONV_HALO:] for s in range(CONV_K)]
    acc = shifted[0] * wv[CONV_K - 1:CONV_K]
    for s in range(1, CONV_K):
        acc = acc + shifted[s] * wv[CONV_K - 1 - s:CONV_K - s]
    return acc, shifted


def _conv_fwd(name, xp, w, b):
    T, CD = xp.shape
    tr, tc = _pick(T, 512), _pick(CD, 512)
    hb = tr // CONV_HALO

    def body(x_ref, h_ref, w_ref, b_ref, o_ref):
        i = pl.program_id(0)
        halo = jnp.where(i > 0, h_ref[...], 0.0)
        cat = jnp.concatenate([halo, x_ref[...]], axis=0)
        acc, _ = _conv_taps(cat, w_ref[...], tr)
        acc = acc + b_ref[...]
        o_ref[...] = acc * _sigmoid(acc)

    return pl.pallas_call(
        body, name=name, grid=(T // tr, CD // tc),
        in_specs=[pl.BlockSpec((tr, tc), lambda i, j: (i, j)),
                  pl.BlockSpec((CONV_HALO, tc), lambda i, j: (jnp.maximum(i * hb - 1, 0), j)),
                  pl.BlockSpec((CONV_K, tc), lambda i, j: (0, j)),
                  pl.BlockSpec((1, tc), lambda i, j: (0, j))],
        out_specs=pl.BlockSpec((tr, tc), lambda i, j: (i, j)),
        out_shape=jax.ShapeDtypeStruct((T, CD), F32),
        compiler_params=_cp("parallel", "parallel"),
    )(xp, xp, w, b)


def _conv_bwd_a(name, dxs, dbm, dcm, xp, w, b):
    T, CD = xp.shape
    tr = _pick(T, 512)
    tc = _pick(dbm.shape[1], 512)
    hb = tr // CONV_HALO
    nx, nb = dxs.shape[1] // tc, dbm.shape[1] // tc

    def part_spec(lo, n):
        def imap(j, i):
            inside = jnp.logical_and(j >= lo, j < lo + n)
            return (jnp.where(inside, i, 0), jnp.clip(j - lo, 0, n - 1))
        return pl.BlockSpec((tr, tc), imap)

    def body(dx_ref, db_ref, dc_ref, x_ref, h_ref, w_ref, b_ref, o_ref, dw_ref, dbias_ref):
        j, i = pl.program_id(0), pl.program_id(1)
        first = i == 0
        d = jnp.where(j < nx, dx_ref[...], jnp.where(j < nx + nb, db_ref[...], dc_ref[...]))
        halo = jnp.where(i > 0, h_ref[...], 0.0)
        cat = jnp.concatenate([halo, x_ref[...]], axis=0)
        acc, shifted = _conv_taps(cat, w_ref[...], tr)
        acc = acc + b_ref[...]
        sg = _sigmoid(acc)
        dconv = d * (sg * (1.0 + acc * (1.0 - sg)))
        o_ref[...] = dconv
        _accum(dbias_ref, jnp.sum(dconv, axis=0, keepdims=True), first)
        tap = lax.broadcasted_iota(jnp.int32, (CONV_K, tc), 0)
        dwv = jnp.zeros((CONV_K, tc), F32)
        for s in range(CONV_K):
            dwv = jnp.where(tap == CONV_K - 1 - s, jnp.sum(dconv * shifted[s], axis=0, keepdims=True), dwv)
        _accum(dw_ref, dwv, first)

    return pl.pallas_call(
        body, name=name, grid=(CD // tc, T // tr),
        in_specs=[part_spec(0, nx), part_spec(nx, nb), part_spec(nx + nb, nb),
                  pl.BlockSpec((tr, tc), lambda j, i: (i, j)),
                  pl.BlockSpec((CONV_HALO, tc), lambda j, i: (jnp.maximum(i * hb - 1, 0), j)),
                  pl.BlockSpec((CONV_K, tc), lambda j, i: (0, j)),
                  pl.BlockSpec((1, tc), lambda j, i: (0, j))],
        out_specs=[pl.BlockSpec((tr, tc), lambda j, i: (i, j)),
                   pl.BlockSpec((CONV_K, tc), lambda j, i: (0, j)),
                   pl.BlockSpec((1, tc), lambda j, i: (0, j))],
        out_shape=[jax.ShapeDtypeStruct((T, CD), F32), jax.ShapeDtypeStruct((CONV_K, CD), F32),
                   jax.ShapeDtypeStruct((1, CD), F32)],
        compiler_params=_cp("parallel", "arbitrary"),
    )(dxs, dbm, dcm, xp, xp, w, b)


def _conv_bwd_b(name, dconv, w):
    T, CD = dconv.shape
    tr, tc = _pick(T, 512), _pick(CD, 512)
    hb = tr // CONV_HALO
    last_h = T // CONV_HALO - 1
    n = tr + CONV_HALO

    def body(d_ref, h_ref, w_ref, o_ref):
        i = pl.program_id(0)
        halo = jnp.where(i < pl.num_programs(0) - 1, h_ref[...], 0.0)
        cat = jnp.concatenate([d_ref[...], halo], axis=0)
        wv = w_ref[...]
        acc = cat[:tr] * wv[CONV_K - 1:CONV_K]
        for s in range(1, CONV_K):
            acc = acc + pltpu.roll(cat, n - s, 0)[:tr] * wv[CONV_K - 1 - s:CONV_K - s]
        o_ref[...] = acc.astype(BF16)

    return pl.pallas_call(
        body, name=name, grid=(T // tr, CD // tc),
        in_specs=[pl.BlockSpec((tr, tc), lambda i, j: (i, j)),
                  pl.BlockSpec((CONV_HALO, tc), lambda i, j: (jnp.minimum((i + 1) * hb, last_h), j)),
                  pl.BlockSpec((CONV_K, tc), lambda i, j: (0, j))],
        out_specs=pl.BlockSpec((tr, tc), lambda i, j: (i, j)),
        out_shape=jax.ShapeDtypeStruct((T, CD), BF16),
        compiler_params=_cp("parallel", "parallel"),
    )(dconv, dconv, w)


def _dt_fwd(name, dtp, bias):
    T, H = dtp.shape

    def body(x_ref, b_ref, o_ref):
        v = x_ref[...] + b_ref[...]
        u = jnp.exp(-jnp.abs(v))
        w1 = 1.0 + u
        lp = jnp.where(w1 == 1.0, u, jnp.log(w1) * (u / jnp.where(w1 == 1.0, 1.0, w1 - 1.0)))
        o_ref[...] = jnp.maximum(v, 0.0) + lp

    return pl.pallas_call(body, name=name, out_shape=jax.ShapeDtypeStruct((T, H), F32))(dtp, bias)


def _dt_bwd(name, ddt, dtp, bias):
    T, H = dtp.shape

    def body(d_ref, x_ref, b_ref, o_ref, ob_ref, db_ref):
        g = d_ref[...] * _sigmoid(x_ref[...] + b_ref[...])
        o_ref[...] = g
        ob_ref[...] = g.astype(BF16)
        db_ref[...] = jnp.sum(g, axis=0, keepdims=True)

    return pl.pallas_call(
        body, name=name,
        out_shape=[jax.ShapeDtypeStruct((T, H), F32), jax.ShapeDtypeStruct((T, H), BF16),
                   jax.ShapeDtypeStruct((1, H), F32)],
    )(ddt, dtp, bias)


def _split(v):
    hi = v.astype(BF16)
    return hi, (v - hi.astype(F32)).astype(BF16)


def _dot01(form, a, b, mask):
    if mask == "b":
        hi, lo = _split(a)
        mb = b.astype(BF16)
        return (lax.dot_general(hi, mb, _DIMS[form], preferred_element_type=F32)
                + lax.dot_general(lo, mb, _DIMS[form], preferred_element_type=F32))
    hi, lo = _split(b)
    ma = a.astype(BF16)
    return (lax.dot_general(ma, hi, _DIMS[form], preferred_element_type=F32)
            + lax.dot_general(ma, lo, _DIMS[form], preferred_element_type=F32))


def _dotb(form, a, b):
    return lax.dot_general(a.astype(BF16), b.astype(BF16), _DIMS[form], preferred_element_type=F32)


def _ssd_common(dtc, dtr, alr, alc, gw):
    li = lax.broadcasted_iota(jnp.int32, (CHUNK, CHUNK), 0)
    si = lax.broadcasted_iota(jnp.int32, (CHUNK, CHUNK), 1)
    tri = (li >= si).astype(F32)
    trit = (li <= si).astype(F32)
    a_row = -jnp.exp(alr)
    a_col = -jnp.exp(alc)
    acs_c = _dot01("nn", tri, dtc * a_row, "a")
    acs_r = _dot01("nn", dtr * a_col, trit, "b")
    eh = lax.broadcasted_iota(jnp.int32, (HEAD_PAD, gw), 0)
    ec = lax.broadcasted_iota(jnp.int32, (HEAD_PAD, gw), 1) // HEAD_DIM
    expand = (eh == ec).astype(F32)
    th = lax.broadcasted_iota(jnp.int32, (gw, HEAD_PAD), 1)
    tc = lax.broadcasted_iota(jnp.int32, (gw, HEAD_PAD), 0) // HEAD_DIM
    reduce_ = (th == tc).astype(F32)
    acs_last = acs_c[CHUNK - 1:CHUNK, :]
    acs_e = _dot01("nn", acs_c, expand, "b")
    return dict(li=li, si=si, tri=tri, trit=trit, a_row=a_row, acs_c=acs_c, acs_r=acs_r,
                reduce=reduce_, acs_last=acs_last, dt_e=_dot01("nn", dtc, expand, "b"),
                eacs_e=jnp.exp(acs_e), dec_e=jnp.exp(acs_e[CHUNK - 1:CHUNK, :] - acs_e),
                dec_h=jnp.exp(acs_last - acs_c))


def _ssd_specs(T, DI, gw, nc, rev):
    nsb = DI // D_STATE
    cidx = (lambda c: nc - 1 - c) if rev else (lambda c: c)
    return dict(
        xs=pl.BlockSpec((CHUNK, gw), lambda g, c: (cidx(c), g)),
        bm=pl.BlockSpec((CHUNK, D_STATE), lambda g, c: (cidx(c), nsb + g)),
        cm=pl.BlockSpec((CHUNK, D_STATE), lambda g, c: (cidx(c), nsb + N_GROUPS + g)),
        dtc=pl.BlockSpec((None, CHUNK, HEAD_PAD), lambda g, c: (g, cidx(c), 0)),
        dtr=pl.BlockSpec((None, ROW_PAD, CHUNK), lambda g, c: (g, 0, cidx(c))),
        alr=pl.BlockSpec((None, 1, HEAD_PAD), lambda g, c: (g, 0, 0)),
        alc=pl.BlockSpec((None, ROW_PAD, 1), lambda g, c: (g, 0, 0)),
        de=pl.BlockSpec((None, 1, gw), lambda g, c: (g, 0, 0)),
        hp=pl.BlockSpec((None, None, gw, D_STATE), lambda g, c: (cidx(c), g, 0, 0)),
        bc=pl.BlockSpec((CHUNK, D_STATE), lambda g, c: (cidx(c), g)),
        acc=pl.BlockSpec((None, 1, HEAD_PAD), lambda g, c: (g, 0, 0)),
    )


def _ssd_fwd(name, xbc, dtc, dtr, alr, alc, d_e, DI):
    T = xbc.shape[0]
    nc = T // CHUNK
    gw = DI // N_GROUPS
    sp = _ssd_specs(T, DI, gw, nc, False)

    def body(xs_ref, b_ref, c_ref, dtc_ref, dtr_ref, alr_ref, alc_ref, de_ref, y_ref, hp_ref, h_scr):
        @pl.when(pl.program_id(1) == 0)
        def _():
            h_scr[...] = jnp.zeros_like(h_scr)

        hpv = h_scr[...]
        hp_ref[...] = hpv
        xs = xs_ref[...]
        bb = b_ref[...].astype(BF16)
        cb_ = c_ref[...].astype(BF16)
        cm = _ssd_common(dtc_ref[...], dtr_ref[...], alr_ref[...], alc_ref[...], gw)
        x = xs * cm["dt_e"]
        xb = x.astype(BF16)
        cbm = _dotb("nt", cb_, bb)
        causal = cm["li"] >= cm["si"]
        second = lax.broadcasted_iota(jnp.int32, (1, HEAD_PAIR), 1) >= HEAD_DIM
        pieces = []
        for pr in range(gw // HEAD_PAIR):
            xp = xb[:, pr * HEAD_PAIR:(pr + 1) * HEAD_PAIR]
            for h2 in range(2):
                j = 2 * pr + h2
                seg = cm["acs_c"][:, j:j + 1] - cm["acs_r"][j:j + 1, :]
                lmat = jnp.exp(jnp.where(causal, seg, -1e30))
                yj = _dotb("nn", cbm * lmat, xp)
                yp = yj if h2 == 0 else jnp.where(second, yj, yp)
            pieces.append(yp)
        ydiag = pieces[0] if len(pieces) == 1 else jnp.concatenate(pieces, axis=1)
        states = _dotb("tn", x * cm["dec_e"], bb)
        yoff = _dotb("nt", cb_, hpv) * cm["eacs_e"]
        y_ref[...] = ydiag + yoff + xs * de_ref[...]
        cd_rows = jnp.sum(cm["reduce"] * jnp.exp(cm["acs_last"]), axis=1, keepdims=True)
        h_scr[...] = cd_rows * hpv + states

    return pl.pallas_call(
        body, name=name, grid=(N_GROUPS, nc),
        in_specs=[sp["xs"], sp["bm"], sp["cm"], sp["dtc"], sp["dtr"], sp["alr"], sp["alc"], sp["de"]],
        out_specs=[sp["xs"], sp["hp"]],
        out_shape=[jax.ShapeDtypeStruct((T, DI), F32), jax.ShapeDtypeStruct((nc, N_GROUPS, gw, D_STATE), F32)],
        scratch_shapes=[pltpu.VMEM((gw, D_STATE), F32)],
        compiler_params=_cp("parallel", "arbitrary"),
    )(xbc, xbc, xbc, dtc, dtr, alr, alc, d_e)


def _ssd_bwd(name, dy, xbc, dtc, dtr, alr, alc, d_e, hprev, DI):
    T = xbc.shape[0]
    nc = T // CHUNK
    gw = DI // N_GROUPS
    sp = _ssd_specs(T, DI, gw, nc, True)

    def body(dy_ref, xs_ref, b_ref, c_ref, dtc_ref, dtr_ref, alr_ref, alc_ref, de_ref, hp_ref,
             dxs_ref, db_ref, dc_ref, ddt_ref, dal_ref, dd_ref, dh_scr):
        first = pl.program_id(1) == 0

        @pl.when(first)
        def _():
            dh_scr[...] = jnp.zeros_like(dh_scr)

        xs = xs_ref[...]
        dyv = dy_ref[...]
        bb = b_ref[...].astype(BF16)
        cb_ = c_ref[...].astype(BF16)
        dtc_v = dtc_ref[...]
        cm = _ssd_common(dtc_v, dtr_ref[...], alr_ref[...], alc_ref[...], gw)
        hpv = hp_ref[...]
        hpb = hpv.astype(BF16)
        dhn = dh_scr[...]
        dhnb = dhn.astype(BF16)
        x = xs * cm["dt_e"]
        xb = x.astype(BF16)
        cbm = _dotb("nt", cb_, bb)
        causal = cm["li"] >= cm["si"]
        second = lax.broadcasted_iota(jnp.int32, (1, HEAD_PAIR), 1) >= HEAD_DIM
        lane_pad = lax.broadcasted_iota(jnp.int32, (1, HEAD_PAD), 1)
        sub_pad = lax.broadcasted_iota(jnp.int32, (ROW_PAD, 1), 0)

        dxs = dyv * de_ref[...]
        dd_part = jnp.sum(_dot01("nn", dyv * xs, cm["reduce"], "b"), axis=0, keepdims=True)
        _accum(dd_ref, dd_part, first)

        dcb = jnp.zeros((CHUNK, CHUNK), F32)
        dacs_c = jnp.zeros((CHUNK, HEAD_PAD), F32)
        dacs_r = jnp.zeros((ROW_PAD, CHUNK), F32)
        pieces = []
        for pr in range(gw // HEAD_PAIR):
            xp = xb[:, pr * HEAD_PAIR:(pr + 1) * HEAD_PAIR]
            dyp = dyv[:, pr * HEAD_PAIR:(pr + 1) * HEAD_PAIR]
            for h2 in range(2):
                j = 2 * pr + h2
                seg = cm["acs_c"][:, j:j + 1] - cm["acs_r"][j:j + 1, :]
                lmat = jnp.exp(jnp.where(causal, seg, -1e30))
                mmat = cbm * lmat
                dym = jnp.where(second if h2 == 1 else jnp.logical_not(second), dyp, 0.0).astype(BF16)
                dm = _dotb("nt", dym, xp)
                dxj = _dotb("tn", mmat, dym)
                dxp = dxj if h2 == 0 else dxp + dxj
                dcb = dcb + dm * lmat
                dseg = dm * mmat
                dacs_c = dacs_c + jnp.where(lane_pad == j, jnp.sum(dseg, axis=1, keepdims=True), 0.0)
                dacs_r = dacs_r - jnp.where(sub_pad == j, jnp.sum(dseg, axis=0, keepdims=True), 0.0)
            pieces.append(dxp)
        dx = pieces[0] if len(pieces) == 1 else jnp.concatenate(pieces, axis=1)
        dc = _dotb("nn", dcb, bb)
        db = _dotb("tn", dcb, cb_)

        gm = _dotb("nt", cb_, hpb)
        dgm = dyv * cm["eacs_e"]
        dacs_c = dacs_c + _dot01("nn", dgm * gm, cm["reduce"], "b")
        dc = dc + _dotb("nn", dgm, hpb)
        dhp = _dotb("tn", dgm, cb_)

        cd_row = jnp.exp(cm["acs_last"])
        cd_rows = jnp.sum(cm["reduce"] * cd_row, axis=1, keepdims=True)
        dhp = dhp + cd_rows * dhn
        rsum = jnp.sum(dhn * hpv, axis=1, keepdims=True)
        dacs_last = jnp.sum(cm["reduce"] * rsum, axis=0, keepdims=True) * cd_row
        dec_e = cm["dec_e"]
        xdec = x * dec_e
        dxdec = _dotb("nt", bb, dhnb)
        db = db + _dotb("nn", xdec, dhnb)
        dx = dx + dxdec * dec_e
        tdec = _dot01("nn", dxdec * x, cm["reduce"], "b") * cm["dec_h"]
        dacs_c = dacs_c - tdec
        dacs_last = dacs_last + jnp.sum(tdec, axis=0, keepdims=True)
        row_id = lax.broadcasted_iota(jnp.int32, (CHUNK, 1), 0)
        dacs_c = dacs_c + jnp.where(row_id == CHUNK - 1, dacs_last, 0.0)

        dxs_ref[...] = dxs + dx * cm["dt_e"]
        ddt = _dot01("nn", dx * xs, cm["reduce"], "b")
        dda = _dot01("nn", cm["trit"], dacs_c, "a")
        dda_r = _dot01("nn", dacs_r, cm["tri"], "b")
        dda_rp = jnp.concatenate([dda_r, jnp.zeros((HEAD_PAD - ROW_PAD, CHUNK), F32)], axis=0)
        eye = (cm["li"] == cm["si"]).astype(F32)
        dda = dda + _dot01("nt", eye, dda_rp, "a")
        ddt_ref[...] = ddt + dda * cm["a_row"]
        _accum(dal_ref, jnp.sum(dda * dtc_v, axis=0, keepdims=True) * cm["a_row"], first)
        db_ref[...] = db
        dc_ref[...] = dc
        dh_scr[...] = dhp

    gs = D_STATE * N_GROUPS
    return pl.pallas_call(
        body, name=name, grid=(N_GROUPS, nc),
        in_specs=[sp["xs"], sp["xs"], sp["bm"], sp["cm"], sp["dtc"], sp["dtr"], sp["alr"], sp["alc"],
                  sp["de"], sp["hp"]],
        out_specs=[sp["xs"], sp["bc"], sp["bc"], sp["dtc"], sp["acc"], sp["acc"]],
        out_shape=[jax.ShapeDtypeStruct((T, DI), F32), jax.ShapeDtypeStruct((T, gs), F32),
                   jax.ShapeDtypeStruct((T, gs), F32), jax.ShapeDtypeStruct((N_GROUPS, T, HEAD_PAD), F32),
                   jax.ShapeDtypeStruct((N_GROUPS, 1, HEAD_PAD), F32),
                   jax.ShapeDtypeStruct((N_GROUPS, 1, HEAD_PAD), F32)],
        scratch_shapes=[pltpu.VMEM((gw, D_STATE), F32)],
        compiler_params=_cp("parallel", "arbitrary"),
    )(dy, xbc, xbc, xbc, dtc, dtr, alr, alc, d_e, hprev)


def _adam_math(w, g, m, v):
    m = ADAM_B1 * m + (1.0 - ADAM_B1) * g
    v = ADAM_B2 * v + (1.0 - ADAM_B2) * (g * g)
    m_hat = m / (1.0 - ADAM_B1 ** ADAM_STEP)
    v_hat = v / (1.0 - ADAM_B2 ** ADAM_STEP)
    delta = -ADAM_LR * (m_hat / (jnp.sqrt(v_hat) + ADAM_EPS) + ADAM_WD * w)
    return delta, m, v


def _adamw(name, w, m, v, g, layer=0, prev=None):
    L, R, C = w.shape
    tr = _rows(R, C * 4, ADAM_BLOCK_BYTES)
    tc = C
    while tr * tc * 4 > ADAM_BLOCK_BYTES and tc % 256 == 0:
        tc //= 2
    blk = pl.BlockSpec((None, tr, tc), lambda i, j: (layer, i, j))
    prev = [] if prev is None else list(prev)

    def body(w_ref, m_ref, v_ref, g_ref, *rest):
        go_ref, d_ref, mo_ref, vo_ref = rest[len(prev):]
        gv = g_ref[...]
        delta, mn, vn = _adam_math(w_ref[...], gv, m_ref[...], v_ref[...])
        go_ref[...] = gv
        d_ref[...] = delta
        mo_ref[...] = mn
        vo_ref[...] = vn

    return pl.pallas_call(
        body, name=name, grid=(R // tr, C // tc),
        in_specs=[blk] * 3 + [pl.BlockSpec((tr, tc), lambda i, j: (i, j))] + [_ANY] * len(prev),
        out_specs=[blk] * 4, out_shape=[jax.ShapeDtypeStruct((L, R, C), F32)] * 4,
        input_output_aliases={4 + k: k for k in range(len(prev))},
        compiler_params=_cp("parallel", "parallel"),
    )(w, m, v, g, *prev)


def _sum_peers(name, gathered):
    n, R, C = gathered.shape

    def body(g_ref, o_ref):
        acc = g_ref[0]
        for d in range(1, n):
            acc = acc + g_ref[d]
        o_ref[...] = acc

    return pl.pallas_call(body, name=name, out_shape=jax.ShapeDtypeStruct((R, C), F32))(gathered)


def _place():
    x, y, c = lax.axis_index("x"), lax.axis_index("y"), lax.axis_index("c")
    chips = [(1 - x, y), (x, 1 - y), (1 - x, 1 - y)]
    return x, y, c, chips


def _allgather_small(name, v):
    R, C = v.shape

    def body(v_ref, o_ref, send_sems, recv_sems):
        x, y, c, _ = _place()
        me = 4 * x + 2 * y + c
        o_ref[me] = v_ref[...]
        copies = []
        for k in range(1, N_DEV):
            px, py, pc = x ^ (k >> 2), y ^ ((k >> 1) & 1), c ^ (k & 1)
            copies.append(pltpu.make_async_remote_copy(
                src_ref=v_ref, dst_ref=o_ref.at[me], send_sem=send_sems.at[k - 1], recv_sem=recv_sems.at[k - 1],
                device_id=(px, py, pc), device_id_type=MESH))
        for cp in copies:
            cp.start()
        for cp in copies:
            cp.wait()

    return pl.pallas_call(
        body, name=name, out_shape=jax.ShapeDtypeStruct((N_DEV, R, C), F32),
        in_specs=[pl.BlockSpec(memory_space=pltpu.VMEM)], out_specs=pl.BlockSpec(memory_space=pltpu.VMEM),
        scratch_shapes=[pltpu.SemaphoreType.DMA((N_DEV - 1,)), pltpu.SemaphoreType.DMA((N_DEV - 1,))],
    )(v)


_HBM = pl.BlockSpec(memory_space=pltpu.HBM)
_SEM = pl.BlockSpec(memory_space=pltpu.SEMAPHORE)
_VMEM = pl.BlockSpec(memory_space=pltpu.VMEM)
_EFFECT = pltpu.SideEffectType.DATAFLOW_SIDE_EFFECTING


def _in_hbm(v):
    return pltpu.with_memory_space_constraint(v, pltpu.HBM)


def _remote(src, dst, send_sem, recv_sem, device):
    return pltpu.make_async_remote_copy(src_ref=src, dst_ref=dst, send_sem=send_sem, recv_sem=recv_sem,
                                        device_id=device, device_id_type=MESH)


def _gather_copies(src, land, ici_s, ici_r, own_s, own_r, arrivals=True):
    x, y, c, chips = _place()
    a = 2 * x + y
    sends, lands = [], []
    for i in range(len(src)):
        own = _remote(src[i], land[i].at[a], own_s.at[i], own_r.at[i], (x, y, 1 - c))
        sends.append(own)
        if arrivals:
            lands.append(own)
        for k, (px, py) in enumerate(chips):
            s, r = ici_s.at[3 * i + k], ici_r.at[3 * i + k]
            sends.append(_remote(src[i].at[c], land[i].at[a, c], s, r, (px, py, c)))
            if arrivals:
                lands.append(_remote(src[i].at[c], land[i].at[2 * px + py, c], s, r, (px, py, c)))
    return sends, lands


def _gather_start(name, shards, after=None):
    n = len(shards)
    lands = [lax.empty((N_CHIPS,) + s.shape, s.dtype) for s in shards]
    n_in = 2 * n + (0 if after is None else 1)

    def body(*refs):
        src, land = refs[:n], refs[n:2 * n]
        sems = refs[n_in:n_in + 4]
        token = refs[-1]
        sends, _ = _gather_copies(src, land, *sems, arrivals=False)
        for cp in sends:
            cp.start()
        token[...] = jnp.zeros_like(token)

    outs = pl.pallas_call(
        body, name=name,
        out_shape=(pltpu.SemaphoreType.DMA((3 * n,)), pltpu.SemaphoreType.DMA((3 * n,)),
                   pltpu.SemaphoreType.DMA((n,)), pltpu.SemaphoreType.DMA((n,)),
                   *[pltpu.HBM(s.shape, s.dtype) for s in shards], *[pltpu.HBM(l.shape, l.dtype) for l in lands],
                   jax.ShapeDtypeStruct((8, 128), F32)),
        in_specs=[_HBM] * (2 * n) + ([] if after is None else [_ANY]),
        out_specs=(_SEM,) * 4 + (_HBM,) * (2 * n) + (_VMEM,),
        input_output_aliases={i: 4 + i for i in range(2 * n)},
        compiler_params=pltpu.CompilerParams(has_side_effects=_EFFECT),
    )(*[_in_hbm(s) for s in shards], *[_in_hbm(l) for l in lands], *([] if after is None else [after]))
    return dict(sems=outs[:4], src=outs[4:4 + n], land=outs[4 + n:4 + 2 * n], token=outs[-1])


def _gather_wait(name, started, after):
    n = len(started["src"])

    def body(*refs):
        src, land = refs[:n], refs[n:2 * n]
        sems = refs[2 * n:2 * n + 4]
        sends, lands = _gather_copies(src, land, *sems)
        for cp in sends:
            cp.wait_send()
        for cp in lands:
            cp.wait_recv()

    outs = pl.pallas_call(
        body, name=name,
        out_shape=[pltpu.HBM(v.shape, v.dtype) for v in list(started["src"]) + list(started["land"])],
        in_specs=[_HBM] * (2 * n) + [_SEM] * 4 + [_ANY], out_specs=[_HBM] * (2 * n),
        input_output_aliases={i: i for i in range(2 * n)},
        compiler_params=pltpu.CompilerParams(has_side_effects=_EFFECT),
    )(*started["src"], *started["land"], *started["sems"], after)
    return outs[n:]


def _gather_forward(name, lands):
    n = len(lands)

    def body(*refs):
        buf = refs[n:2 * n]
        send_sems, recv_sems = refs[2 * n:]
        x, y, c, chips = _place()
        sends, lands_ = [], []
        for i in range(n):
            for k, (px, py) in enumerate(chips):
                b = 2 * px + py
                sends.append(_remote(buf[i].at[b, c], buf[i].at[b, c], send_sems.at[i, k], recv_sems.at[i, k],
                                     (x, y, 1 - c)))
                lands_.append(_remote(buf[i].at[b, 1 - c], buf[i].at[b, 1 - c], send_sems.at[i, k],
                                      recv_sems.at[i, k], (x, y, 1 - c)))
        for cp in sends:
            cp.start()
        for cp in sends:
            cp.wait_send()
        for cp in lands_:
            cp.wait_recv()

    return pl.pallas_call(
        body, name=name, in_specs=[_ANY] * n, out_specs=[_ANY] * n,
        out_shape=[jax.ShapeDtypeStruct(l.shape, l.dtype) for l in lands],
        input_output_aliases={i: i for i in range(n)},
        scratch_shapes=[pltpu.SemaphoreType.DMA((n, 3)), pltpu.SemaphoreType.DMA((n, 3))],
    )(*lands)


def _pair_exchange(name, parts):
    n = len(parts)

    def body(*refs):
        src = refs[:n]
        dst = refs[n:2 * n]
        send_sems, recv_sems = refs[2 * n:]
        x, y, c, _ = _place()
        copies = [pltpu.make_async_remote_copy(
            src_ref=src[i].at[:, 1 - c], dst_ref=dst[i], send_sem=send_sems.at[i], recv_sem=recv_sems.at[i],
            device_id=(x, y, 1 - c), device_id_type=MESH) for i in range(n)]
        for cp in copies:
            cp.start()
        for cp in copies:
            cp.wait()

    return pl.pallas_call(
        body, name=name, in_specs=[_ANY] * n, out_specs=[_ANY] * n,
        out_shape=[jax.ShapeDtypeStruct((p.shape[0],) + p.shape[2:], p.dtype) for p in parts],
        scratch_shapes=[pltpu.SemaphoreType.DMA((n,)), pltpu.SemaphoreType.DMA((n,))],
    )(*parts)


def _chip_copies(src, land, send_sems, recv_sems):
    x, y, c, chips = _place()
    return [_remote(src[i].at[2 * px + py], land[i].at[k], send_sems.at[3 * i + k], recv_sems.at[3 * i + k],
                    (px, py, c))
            for i in range(len(src)) for k, (px, py) in enumerate(chips)]


def _chip_start(name, parts):
    n = len(parts)
    lands = [lax.empty((3,) + p.shape[1:], p.dtype) for p in parts]

    def body(*refs):
        src, land = refs[:n], refs[n:2 * n]
        send_sems, recv_sems = refs[2 * n:2 * n + 2]
        token = refs[-1]
        for cp in _chip_copies(src, land, send_sems, recv_sems):
            cp.start()
        token[...] = jnp.zeros_like(token)

    outs = pl.pallas_call(
        body, name=name,
        out_shape=(pltpu.SemaphoreType.DMA((3 * n,)), pltpu.SemaphoreType.DMA((3 * n,)),
                   *[pltpu.HBM(p.shape, p.dtype) for p in parts], *[pltpu.HBM(l.shape, l.dtype) for l in lands],
                   jax.ShapeDtypeStruct((8, 128), F32)),
        in_specs=[_HBM] * (2 * n), out_specs=(_SEM,) * 2 + (_HBM,) * (2 * n) + (_VMEM,),
        input_output_aliases={i: 2 + i for i in range(2 * n)},
        compiler_params=pltpu.CompilerParams(has_side_effects=_EFFECT),
    )(*[_in_hbm(p) for p in parts], *[_in_hbm(l) for l in lands])
    return dict(sems=outs[:2], src=outs[2:2 + n], land=outs[2 + n:2 + 2 * n], token=outs[-1])


def _chip_wait(name, started, after):
    n = len(started["src"])
    after = list(after)

    def body(*refs):
        src, land = refs[:n], refs[n:2 * n]
        send_sems, recv_sems = refs[2 * n:2 * n + 2]
        copies = _chip_copies(src, land, send_sems, recv_sems)
        for cp in copies:
            cp.wait_send()
        for cp in copies:
            cp.wait_recv()

    outs = pl.pallas_call(
        body, name=name,
        out_shape=[pltpu.HBM(v.shape, v.dtype) for v in list(started["src"]) + list(started["land"])],
        in_specs=[_HBM] * (2 * n) + [_SEM] * 2 + [_ANY] * len(after), out_specs=[_HBM] * (2 * n),
        input_output_aliases={i: i for i in range(2 * n)},
        compiler_params=pltpu.CompilerParams(has_side_effects=_EFFECT),
    )(*started["src"], *started["land"], *started["sems"], *after)
    return outs[:n], outs[n:]


def _half_exchange(name, shards):
    n = len(shards)

    def body(*refs):
        buf = refs[n:2 * n]
        send_sems, recv_sems = refs[2 * n:]
        x, y, c, _ = _place()
        copies = [pltpu.make_async_remote_copy(
            src_ref=buf[i].at[c], dst_ref=buf[i].at[c], send_sem=send_sems.at[i], recv_sem=recv_sems.at[i],
            device_id=(x, y, 1 - c), device_id_type=MESH) for i in range(n)]
        for cp in copies:
            cp.start()
        for cp in copies:
            cp.wait()

    return pl.pallas_call(
        body, name=name, in_specs=[_ANY] * n, out_specs=[_ANY] * n,
        out_shape=[jax.ShapeDtypeStruct(s.shape, s.dtype) for s in shards],
        input_output_aliases={i: i for i in range(n)},
        scratch_shapes=[pltpu.SemaphoreType.DMA((n,)), pltpu.SemaphoreType.DMA((n,))],
    )(*shards)


def _pair_add(name, pos, part, sib):
    Q, _, R2, C = part.shape
    tr = _pick(R2, 256)

    def body(pos_ref, p_ref, s_ref, o_ref):
        o_ref[...] = (p_ref[...].astype(F32) + s_ref[...].astype(F32)).astype(BF16)

    return pl.pallas_call(
        body, name=name,
        grid_spec=pltpu.PrefetchScalarGridSpec(
            num_scalar_prefetch=1, grid=(Q, R2 // tr),
            in_specs=[pl.BlockSpec((None, None, tr, C), lambda q, i, pos_ref: (q, pos_ref[1], i, 0)),
                      pl.BlockSpec((None, tr, C), lambda q, i, pos_ref: (q, i, 0))],
            out_specs=pl.BlockSpec((None, tr, C), lambda q, i, pos_ref: (q, i, 0))),
        out_shape=jax.ShapeDtypeStruct((Q, R2, C), BF16),
        compiler_params=_cp("parallel", "parallel"),
    )(pos, part, sib)


def _chip_sum(name, pos, own, got):
    _, R2, C = own.shape
    tr = _pick(R2, 256)

    def body(pos_ref, o_ref, g_ref, out_ref):
        acc = o_ref[...].astype(F32)
        for k in range(3):
            acc = acc + g_ref[k].astype(F32)
        out_ref[...] = acc

    return pl.pallas_call(
        body, name=name,
        grid_spec=pltpu.PrefetchScalarGridSpec(
            num_scalar_prefetch=1, grid=(R2 // tr,),
            in_specs=[pl.BlockSpec((None, tr, C), lambda i, pos_ref: (pos_ref[0], i, 0)),
                      pl.BlockSpec((3, tr, C), lambda i, pos_ref: (0, i, 0))],
            out_specs=pl.BlockSpec((None, tr, C), lambda i, pos_ref: (pos_ref[1], i, 0))),
        out_shape=jax.ShapeDtypeStruct((2, R2, C), F32),
        compiler_params=_cp("parallel"),
    )(pos, own, got)


def _reduce_begin(tag, pos, parts):
    split = [p.reshape(p.shape[0], 2, p.shape[1] // 2, p.shape[2]) for p in parts]
    sib = _pair_exchange(tag + "_pair", split)
    chip = [_pair_add(f"{tag}_add{i}", pos, split[i], sib[i]) for i in range(len(parts))]
    return _chip_start(tag + "_chip_start", chip)


def _reduce_end(tag, pos, started, after, shapes):
    chip, got = _chip_wait(tag + "_chip_wait", started, after)
    halves = [_chip_sum(f"{tag}_sum{i}", pos, chip[i], got[i]) for i in range(len(chip))]
    full = _half_exchange(tag + "_half", halves)
    return [f.reshape(shp) for f, shp in zip(full, shapes)]


def _pad_to(v, mult):
    n = v.shape[0]
    return jnp.pad(v, (0, (-n) % mult))


def kernel(x, p, pool_w, pool_scale, ssm_in_w, ssm_conv_w, ssm_conv_b, ssm_dt_bias, ssm_a_log, ssm_d, ssm_norm_w, ssm_out_w, mlp_w1, mlp_w2, ln_g, ln_b, ple_w, ple_gate_w, loss_target, m_pool_w, m_pool_scale, m_ssm_in_w, m_ssm_conv_w, m_ssm_conv_b, m_ssm_dt_bias, m_ssm_a_log, m_ssm_d, m_ssm_norm_w, m_ssm_out_w, m_mlp_w1, m_mlp_w2, m_ln_g, m_ln_b, m_ple_w, m_ple_gate_w, v_pool_w, v_pool_scale, v_ssm_in_w, v_ssm_conv_w, v_ssm_conv_b, v_ssm_dt_bias, v_ssm_a_log, v_ssm_d, v_ssm_norm_w, v_ssm_out_w, v_mlp_w1, v_mlp_w2, v_ln_g, v_ln_b, v_ple_w, v_ple_gate_w):
    T, D = x.shape[1], x.shape[2]
    NG = len(POOL_WINDOWS)
    GD = D // NG
    DI = ssm_out_w.shape[1] * N_CHIPS
    H = ssm_dt_bias.shape[1]
    HPG = H // N_GROUPS
    GW = DI // N_GROUPS
    GS = N_GROUPS * D_STATE
    CD = DI + 2 * GS
    DF = mlp_w1.shape[2] * N_CHIPS
    PD = ple_w.shape[1]
    NIN = ssm_in_w.shape[2]

    xi, yi, ci = lax.axis_index("x"), lax.axis_index("y"), lax.axis_index("c")
    chip = 2 * xi + yi
    pos = jnp.stack([chip, ci]).astype(jnp.int32)

    x0 = x[0]
    tgt = loss_target[0]
    p0b, p1b = p[0, 0].astype(BF16), p[1, 0].astype(BF16)

    def halves(w):
        return w.astype(BF16).reshape((2, w.shape[0] // 2) + w.shape[1:])

    sh_pool = pool_w[0].astype(BF16)
    sh_pool = sh_pool.reshape((2, NG // 2) + sh_pool.shape[1:])
    groups = [("ag0a", [sh_pool, halves(mlp_w1[0])]),
              ("ag0b", [halves(mlp_w2[0]), halves(ple_gate_w[0]), halves(ple_w[0])]),
              ("ag1a", [halves(ssm_in_w[0])]),
              ("ag1b", [halves(ssm_out_w[0]), halves(mlp_w1[1])]),
              ("ag1c", [halves(mlp_w2[1]), halves(ple_gate_w[1]), halves(ple_w[1])])]
    started, prev = {}, None
    for tag, shards in groups:
        started[tag] = _gather_start(tag + "_start", shards, after=prev)
        prev = started[tag]["token"]
    all_started = prev

    def gather_end(tag, after):
        return _gather_forward(tag + "_fwd", _gather_wait(tag + "_wait", started[tag], after))

    g_w1, g_w2, g_pw, g_gw = {}, {}, {}, {}

    def set_w1(l, g):
        g_w1[l] = g.reshape(N_CHIPS, 1, D, DF // N_CHIPS)

    def set_rest(l, g2, ggw, gpw):
        g_w2[l] = g2.reshape(N_CHIPS, 1, DF // N_CHIPS, D)
        g_gw[l] = ggw.reshape(N_CHIPS, 1, D // N_CHIPS, D)
        g_pw[l] = gpw.reshape(N_CHIPS, 1, PD, D // N_CHIPS)

    g_pool, g = gather_end("ag0a", all_started)
    set_w1(0, g)
    w_pool = jnp.transpose(g_pool.reshape(N_CHIPS, NG, GD // N_CHIPS, GD), (1, 0, 2, 3)).reshape(NG, GD, GD)

    small_sh = jnp.concatenate([ssm_conv_w[0].reshape(-1), ssm_conv_b[0], ssm_norm_w[0],
                                ln_g.reshape(-1), ln_b.reshape(-1)])
    n_sh = small_sh.shape[0]
    small_all = _allgather_small("gather_small", _pad_to(small_sh, 1024).reshape(-1, 128))
    small_all = small_all.reshape(N_DEV, -1)[0::2, :n_sh]
    cdq, niq, dq = CD // N_CHIPS, DI // N_CHIPS, D // N_CHIPS
    o = 0
    conv_w = jnp.concatenate([small_all[q, o:o + CONV_K * cdq].reshape(CONV_K, cdq) for q in range(N_CHIPS)], axis=1)
    o += CONV_K * cdq
    conv_b = small_all[:, o:o + cdq].reshape(1, CD)
    o += cdq
    norm_w = small_all[:, o:o + niq].reshape(1, DI)
    o += niq
    lng = jnp.transpose(small_all[:, o:o + 4 * dq].reshape(N_CHIPS, 2, 2, dq), (1, 2, 0, 3)).reshape(2, 2, 1, D)
    o += 4 * dq
    lnb = jnp.transpose(small_all[:, o:o + 4 * dq].reshape(N_CHIPS, 2, 2, dq), (1, 2, 0, 3)).reshape(2, 2, 1, D)

    pooled = _pool_fwd("pool_fwd", x0)
    hraw = _mm("pool_mm", "nn", pooled, w_pool, T, D, GD, tn=GD,
               a_spec=lambda tm, tn, tk: pl.BlockSpec((tm, tk), lambda i, j, k: (i, j)),
               b_spec=lambda tm, tn, tk: pl.BlockSpec((None, tk, tn), lambda i, j, k: (j, 0, 0)))
    x1, x1b, xh1, rs1 = _res_ln("ln00", x0, hraw, lng[0, 0], lnb[0, 0], scale=pool_scale)

    def mlp_fwd(l, xb, rest_tag):
        a, h2b = _mm(f"mlp{l}_up", "nn", xb, g_w1[l], T, DF, D, tn=min(1024, DF // N_CHIPS),
                     b_spec=_colshard_b(0, DF // N_CHIPS), out_dtype=(F32, BF16),
                     epi=lambda acc: (acc, jnp.square(jnp.maximum(acc, 0.0))))
        set_rest(l, *gather_end(rest_tag, a))
        h = _mm(f"mlp{l}_down", "nn", h2b, g_w2[l], T, D, DF, k_unit=DF // N_CHIPS,
                b_spec=_rowshard_b(0, DF // N_CHIPS))
        return a, h2b, h

    def ple_fwd(l, xb, pb):
        gl = _mm(f"gate{l}_logit", "nn", xb, g_gw[l], T, D, D, k_unit=D // N_CHIPS,
                 b_spec=_rowshard_b(0, D // N_CHIPS))
        e = _mm(f"gate{l}_emb", "nn", pb, g_pw[l], T, D, PD, tn=min(1024, D // N_CHIPS),
                b_spec=_colshard_b(0, D // N_CHIPS))
        return gl, e

    a0, h2b0, h0 = mlp_fwd(0, x1b, "ag0b")
    x2, x2b, xh2, rs2 = _res_ln("ln01", x1, h0, lng[0, 1], lnb[0, 1])
    gl0, e0 = ple_fwd(0, x2b, p0b)
    x3, x3b, gate0 = _gate_fwd("gate0", x2, gl0, e0)

    g_in, = gather_end("ag1a", x3b)
    w_in = jnp.concatenate([g_in[q].reshape(D, NIN) for q in range(N_CHIPS)], axis=1)
    w_z, w_xbc, w_dt = w_in[:, :DI], w_in[:, DI:DI + CD], w_in[:, DI + CD:]
    z = _mm("ssm_in_z", "nn", x3b, w_z, T, DI, D)
    xbc_pre = _mm("ssm_in_xbc", "nn", x3b, w_xbc, T, CD, D)
    dt_pre = _mm("ssm_in_dt", "nn", x3b, w_dt, T, H, D)
    xbc = _conv_fwd("conv_fwd", xbc_pre, conv_w, conv_b)
    dt = _dt_fwd("dt_fwd", dt_pre, ssm_dt_bias)
    dt_g = jnp.transpose(dt.reshape(T, N_GROUPS, HPG), (1, 0, 2))
    dtc = jnp.pad(dt_g, ((0, 0), (0, 0), (0, HEAD_PAD - HPG)))
    dtr = jnp.pad(jnp.transpose(dt_g, (0, 2, 1)), ((0, 0), (0, ROW_PAD - HPG), (0, 0)))
    al_g = ssm_a_log.reshape(N_GROUPS, HPG)
    alr = jnp.pad(al_g, ((0, 0), (0, HEAD_PAD - HPG)))[:, None, :]
    alc = jnp.pad(al_g, ((0, 0), (0, ROW_PAD - HPG)))[:, :, None]
    d_e = jnp.repeat(ssm_d.reshape(N_GROUPS, HPG), HEAD_DIM, axis=1)[:, None, :]
    ysc, hprev = _ssd_fwd("ssd_fwd", xbc, dtc, dtr, alr, alc, d_e, DI)
    g_out, g = gather_end("ag1b", ysc)
    set_w1(1, g)
    w_out = g_out.reshape(DI, D)
    ynb, rsn = _gnorm_fwd("gnorm_fwd", ysc, z, norm_w)
    h1 = _mm("ssm_out", "nn", ynb, w_out, T, D, DI)
    x4, x4b, xh4, rs4 = _res_ln("ln10", x3, h1, lng[1, 0], lnb[1, 0])
    a1, h2b1, h2 = mlp_fwd(1, x4b, "ag1c")
    x5, x5b, xh5, rs5 = _res_ln("ln11", x4, h2, lng[1, 1], lnb[1, 1])
    gl1, e1 = ple_fwd(1, x5b, p1b)
    dx6, gate1, loss_parts = _gate_loss("gate1_loss", x5, gl1, e1, tgt)
    loss_local = jnp.sum(loss_parts[0::8, 0])

    def ple_bwd(l, dxo, gate, e, xb, pb, deps=()):
        dgl, de = _gate_bwd(f"gate{l}_bwd", dxo, gate, e)
        d_gw = _mm(f"gate{l}_dw", "tn", xb, dgl, D, D, T, out_dtype=BF16, deps=deps).reshape(N_CHIPS, D // N_CHIPS, D)
        d_pw = _mm(f"gate{l}_dpw", "tn", pb, de, PD, D, T, out_dtype=BF16, tn=min(1024, D // N_CHIPS),
                   o_shape=(N_CHIPS, PD, D // N_CHIPS), o_spec=_colshard_o(D // N_CHIPS))
        dx = _mm(f"gate{l}_dx", "nt", dgl, g_gw[l], T, D, D, tn=min(1024, D // N_CHIPS), deps=deps,
                 b_spec=_rowshard_bt(0, D // N_CHIPS), epi=lambda acc, r: acc + r, extras=[dxo])
        return dx, d_gw, d_pw

    def mlp_bwd(l, du, dub, a, h2b, xb):
        d_w2 = _mm(f"mlp{l}_dw2", "tn", h2b, dub, DF, D, T, out_dtype=BF16).reshape(N_CHIPS, DF // N_CHIPS, D)
        da = _mm(f"mlp{l}_da", "nt", dub, g_w2[l], T, DF, D, tn=min(1024, DF // N_CHIPS),
                 b_spec=_rowshard_bt(0, DF // N_CHIPS), out_dtype=BF16,
                 epi=lambda acc, av: acc * (2.0 * jnp.maximum(av, 0.0)), extras=[a])
        d_w1 = _mm(f"mlp{l}_dw1", "tn", xb, da, D, DF, T, out_dtype=BF16, tn=min(1024, DF // N_CHIPS),
                   o_shape=(N_CHIPS, D, DF // N_CHIPS), o_spec=_colshard_o(DF // N_CHIPS))
        dx = _mm(f"mlp{l}_dx", "nt", da, g_w1[l], T, D, DF, k_unit=DF // N_CHIPS,
                 b_spec=_colshard_bt(0, DF // N_CHIPS), epi=lambda acc, r: acc + ALPHA * r, extras=[du])
        return dx, d_w1, d_w2

    dx5, d_gw1, d_pw1 = ple_bwd(1, dx6, gate1, e1, x5b, p1b)
    du5, du5b, dg11, db11 = _ln_bwd("ln11_bwd", dx5, xh5, rs5, lng[1, 1])
    dx4, d_w1_1, d_w2_1 = mlp_bwd(1, du5, du5b, a1, h2b1, x4b)
    parts1a = [d_w1_1, d_w2_1, d_gw1, d_pw1]
    red1a = _reduce_begin("rs1a", pos, parts1a)
    du4, du4b, dg10, db10 = _ln_bwd("ln10_bwd", dx4, xh4, rs4, lng[1, 0])
    d_wout = _mm("ssm_out_dw", "tn", ynb, du4b, DI, D, T, out_dtype=BF16,
                 deps=[red1a["token"]]).reshape(N_CHIPS, DI // N_CHIPS, D)
    dyn = _mm("ssm_out_dx", "nt", du4b, w_out, T, DI, D, deps=[red1a["token"]])
    dysc, dzb, dnorm_w = _gnorm_bwd("gnorm_bwd", dyn, ysc, z, norm_w, rsn)
    dxs, dbm, dcm, ddt_g, dalog_g, dd_g = _ssd_bwd("ssd_bwd", dysc, xbc, dtc, dtr, alr, alc, d_e, hprev, DI)
    dconv, dconv_w, dconv_b = _conv_bwd_a("conv_bwd_a", dxs, dbm, dcm, xbc_pre, conv_w, conv_b)
    dxbcb = _conv_bwd_b("conv_bwd_b", dconv, conv_w)
    ddt = jnp.transpose(ddt_g[:, :, :HPG], (1, 0, 2)).reshape(T, H)
    _, ddtpb, ddt_bias = _dt_bwd("dt_bwd", ddt, dt_pre, ssm_dt_bias)
    da_log = dalog_g[:, 0, :HPG].reshape(1, H)
    dd_skip = dd_g[:, 0, :HPG].reshape(1, H)
    d_wz = _mm("ssm_in_dwz", "tn", x3b, dzb, D, DI, T, out_dtype=BF16)
    d_wx = _mm("ssm_in_dwx", "tn", x3b, dxbcb, D, CD, T, out_dtype=BF16)
    d_wdt = _mm("ssm_in_dwdt", "tn", x3b, ddtpb, D, H, T, out_dtype=BF16)
    d_win = jnp.transpose(jnp.concatenate([d_wz, d_wx, d_wdt], axis=1).reshape(D, N_CHIPS, NIN), (1, 0, 2))
    dx3 = _mm("ssm_in_dxz", "nt", dzb, w_z, T, D, DI, epi=lambda acc, r: acc + ALPHA * r, extras=[du4])
    dx3 = _mm("ssm_in_dxx", "nt", dxbcb, w_xbc, T, D, CD, epi=lambda acc, r: acc + r, extras=[dx3])
    dx3 = _mm("ssm_in_dxdt", "nt", ddtpb, w_dt, T, D, H, epi=lambda acc, r: acc + r, extras=[dx3])

    parts1b = [d_win, d_wout]
    red1b = _reduce_begin("rs1b", pos, parts1b)

    dx2, d_gw0, d_pw0 = ple_bwd(0, dx3, gate0, e0, x2b, p0b, deps=[red1b["token"]])
    du2, du2b, dg01, db01 = _ln_bwd("ln01_bwd", dx2, xh2, rs2, lng[0, 1])
    dx1, d_w1_0, d_w2_0 = mlp_bwd(0, du2, du2b, a0, h2b0, x1b)
    du1, dhrb, dg00, db00, dscale = _ln_bwd("ln00_bwd", dx1, xh1, rs1, lng[0, 0], hraw=hraw, scale=pool_scale)
    d_wpool = _mm("pool_dw", "tn", pooled, dhrb, D, GD, T, tm=GD, tn=GD, out_dtype=BF16,
                  b_spec=lambda tm, tn, tk: pl.BlockSpec((tk, tn), lambda i, j, k: (k, i)),
                  o_shape=(NG, GD, GD), o_spec=lambda tm, tn, tk: pl.BlockSpec((None, tm, tn), lambda i, j, k: (i, 0, 0)))
    dpooled = _mm("pool_dx", "nt", dhrb, w_pool, T, D, GD, tn=GD,
                  a_spec=lambda tm, tn, tk: pl.BlockSpec((tm, tk), lambda i, j, k: (i, j)),
                  b_spec=lambda tm, tn, tk: pl.BlockSpec((None, tn, tk), lambda i, j, k: (j, 0, 0)))
    grad_x = _pool_bwd("pool_bwd", dpooled, du1)
    d_wpool = jnp.transpose(d_wpool.reshape(NG, N_CHIPS, GD // N_CHIPS, GD), (1, 0, 2, 3)).reshape(N_CHIPS, NG * GD // N_CHIPS, GD)

    parts0 = [d_wpool, d_w1_0, d_w2_0, d_gw0, d_pw0]
    red0 = _reduce_begin("rs0", pos, parts0)

    def adam(name, w, m, v, g, layer=0, prev=None):
        w3, m3, v3 = (t.reshape((t.shape[0], -1, t.shape[-1])) for t in (w, m, v))
        return _adamw(name, w3, m3, v3, g, layer, prev)

    r_w1_1, r_w2_1, r_gw1, r_pw1 = _reduce_end("rs1a", pos, red1a, [red0["token"]], [p.shape[1:] for p in parts1a])
    stacked = {"mlp_w1": (mlp_w1, m_mlp_w1, v_mlp_w1), "mlp_w2": (mlp_w2, m_mlp_w2, v_mlp_w2),
               "ple_w": (ple_w, m_ple_w, v_ple_w), "ple_gate_w": (ple_gate_w, m_ple_gate_w, v_ple_gate_w)}
    grads1 = {"mlp_w1": r_w1_1, "mlp_w2": r_w2_1, "ple_w": r_pw1, "ple_gate_w": r_gw1}
    upper = {nm: adam(f"adam_{nm}_l1", *stacked[nm], grads1[nm], layer=1) for nm in stacked}
    r_in, r_out = _reduce_end("rs1b", pos, red1b, [upper["mlp_w1"][1]], [p.shape[1:] for p in parts1b])
    big = {"ssm_in_w": [jnp.swapaxes(t, 1, 2) for t in adam(
               "adam_ssm_in_w", *(jnp.swapaxes(t, 1, 2) for t in (ssm_in_w, m_ssm_in_w, v_ssm_in_w)), r_in.T)],
           "ssm_out_w": adam("adam_ssm_out_w", ssm_out_w, m_ssm_out_w, v_ssm_out_w, r_out)}

    dln_g = jnp.stack([jnp.stack([dg00, dg01]), jnp.stack([dg10, dg11])]).reshape(-1)
    dln_b = jnp.stack([jnp.stack([db00, db01]), jnp.stack([db10, db11])]).reshape(-1)
    small_g = jnp.concatenate([dscale.reshape(-1), dconv_w.reshape(-1), dconv_b.reshape(-1), ddt_bias.reshape(-1),
                               da_log.reshape(-1), dd_skip.reshape(-1), dnorm_w.reshape(-1), dln_g, dln_b,
                               loss_local.reshape(1)])
    n_sg = small_g.shape[0]
    sg_all = _allgather_small("gather_small_grads", _pad_to(small_g, 1024).reshape(-1, 128))
    sg = _sum_peers("sum_small_grads", sg_all).reshape(-1)[:n_sg]
    o = 0

    def take(nel):
        nonlocal o
        v = sg[o:o + nel]
        o += nel
        return v

    g_scale = take(D).reshape(1, D)
    g_conv_w_full = take(CONV_K * CD).reshape(CONV_K, CD)
    g_conv_b_full = take(CD).reshape(1, CD)
    g_dt_bias = take(H).reshape(1, H)
    g_a_log = take(H).reshape(1, H)
    g_d = take(H).reshape(1, H)
    g_norm_full = take(DI).reshape(1, DI)
    g_lng_full = take(4 * D).reshape(2, 2, D)
    g_lnb_full = take(4 * D).reshape(2, 2, D)
    loss = take(1).reshape(())
    g_conv_w = lax.dynamic_slice_in_dim(g_conv_w_full, chip * cdq, cdq, axis=1)[None]
    g_conv_b = lax.dynamic_slice_in_dim(g_conv_b_full, chip * cdq, cdq, axis=1)
    g_norm = lax.dynamic_slice_in_dim(g_norm_full, chip * niq, niq, axis=1)
    g_lng = lax.dynamic_slice_in_dim(g_lng_full, chip * dq, dq, axis=2)
    g_lnb = lax.dynamic_slice_in_dim(g_lnb_full, chip * dq, dq, axis=2)

    small = [("pool_scale", pool_scale, m_pool_scale, v_pool_scale, g_scale),
             ("ssm_conv_w", ssm_conv_w, m_ssm_conv_w, v_ssm_conv_w, g_conv_w),
             ("ssm_conv_b", ssm_conv_b, m_ssm_conv_b, v_ssm_conv_b, g_conv_b),
             ("ssm_dt_bias", ssm_dt_bias, m_ssm_dt_bias, v_ssm_dt_bias, g_dt_bias),
             ("ssm_a_log", ssm_a_log, m_ssm_a_log, v_ssm_a_log, g_a_log),
             ("ssm_d", ssm_d, m_ssm_d, v_ssm_d, g_d),
             ("ssm_norm_w", ssm_norm_w, m_ssm_norm_w, v_ssm_norm_w, g_norm),
             ("ln_g", ln_g, m_ln_g, v_ln_g, g_lng),
             ("ln_b", ln_b, m_ln_b, v_ln_b, g_lnb)]

    def pack(idx):
        flat = _pad_to(jnp.concatenate([s[idx].reshape(-1) for s in small]), 1024)
        return flat.reshape(1, -1, 128)

    sm_out = _adamw("adam_small", pack(1), pack(2), pack(3), pack(4)[0])
    small_res = {}
    o = 0
    for s in small:
        nel = s[1].size
        small_res[s[0]] = [t.reshape(-1)[o:o + nel].reshape(s[1].shape) for t in sm_out]
        o += nel

    r_pool, r_w1_0, r_w2_0, r_gw0, r_pw0 = _reduce_end(
        "rs0", pos, red0, [big["ssm_in_w"][1], big["ssm_out_w"][1], sm_out[1]] + [upper[nm][1] for nm in upper],
        [p.shape[1:] for p in parts0])
    grads0 = {"mlp_w1": r_w1_0, "mlp_w2": r_w2_0, "ple_w": r_pw0, "ple_gate_w": r_gw0}
    big["pool_w"] = adam("adam_pool_w", pool_w, m_pool_w, v_pool_w, r_pool)
    for nm in stacked:
        big[nm] = adam(f"adam_{nm}_l0", *stacked[nm], grads0[nm], layer=0, prev=upper[nm])
    shapes = {"pool_w": pool_w.shape, "ssm_in_w": ssm_in_w.shape, "ssm_out_w": ssm_out_w.shape,
              **{nm: stacked[nm][0].shape for nm in stacked}}
    big = {nm: [t.reshape(shapes[nm]) for t in big[nm]] for nm in big}

    order = ["pool_w", "pool_scale", "ssm_in_w", "ssm_conv_w", "ssm_conv_b", "ssm_dt_bias", "ssm_a_log", "ssm_d",
             "ssm_norm_w", "ssm_out_w", "mlp_w1", "mlp_w2", "ln_g", "ln_b", "ple_w", "ple_gate_w"]
    res = {**big, **small_res}
    outs = [loss, grad_x[None]]
    for kind in range(4):
        outs += [res[nm][kind] for nm in order]
    return tuple(outs)
```

```python
import jax
import jax.numpy as jnp
from jax import lax
from jax.experimental import pallas as pl
from jax.experimental.pallas import tpu as pltpu

F32 = jnp.float32
BF16 = jnp.bfloat16
MESH = pl.DeviceIdType.MESH

DEPTH = 2
ALPHA = (2.0 * DEPTH) ** 0.25
LN_EPS = 1e-5
RMS_EPS = 1e-5
POOL_WINDOWS = (2, 4, 8, 16)
POOL_HALO = 16
HEAD_DIM = 64
HEAD_PAIR = 2 * HEAD_DIM
N_GROUPS = 8
D_STATE = 128
CHUNK = 128
CONV_K = 4
CONV_HALO = 8
HEAD_PAD = 128
ROW_PAD = 8
N_CHIPS = 4
N_DEV = 8
ADAM_LR = 0.001
ADAM_B1 = 0.9
ADAM_B2 = 0.999
ADAM_EPS = 1e-08
ADAM_WD = 0.01
ADAM_STEP = 10
VMEM_LIMIT = 56 * 1024 * 1024
ADAM_BLOCK_BYTES = 1024 * 1024
MM_VMEM_BUDGET = 40 * 1024 * 1024


_ANY = pl.BlockSpec(memory_space=pl.ANY)


def _cp(*sem):
    return pltpu.CompilerParams(dimension_semantics=sem, vmem_limit_bytes=VMEM_LIMIT)


def _pick(dim, pref):
    t = pref
    while t >= 128:
        if dim % t == 0:
            return t
        t //= 2
    return dim


def _rows(rows, row_bytes, budget):
    t = rows
    while t * row_bytes > budget and t % 16 == 0:
        t //= 2
    return t


def _sigmoid(v):
    return 1.0 / (1.0 + jnp.exp(-v))


_DIMS = {"nn": (((1,), (0,)), ((), ())), "nt": (((1,), (1,)), ((), ())), "tn": (((0,), (0,)), ((), ()))}


def _pick_k(k_unit, fixed_bytes, per_k_bytes):
    for n in range(1, k_unit // 128 + 1):
        if k_unit % n == 0 and (n == 1 or (k_unit // n) % 128 == 0):
            if fixed_bytes + (k_unit // n) * per_k_bytes <= MM_VMEM_BUDGET:
                return k_unit // n
    return min(k_unit, 128)


def _mm(name, form, a, b, M, N, K, *, tm=1024, tn=1024, k_unit=None, a_spec=None, b_spec=None,
        o_shape=None, o_spec=None, out_dtype=F32, pro=None, epi=None, extras=(), deps=()):
    tm, tn = _pick(M, tm), _pick(N, tn)
    out_dtypes = out_dtype if isinstance(out_dtype, tuple) else (out_dtype,)
    n_out = len(out_dtypes)
    in_place = n_out == 1 and out_dtypes[0] == F32
    fixed = 2 * tm * tn * (sum(jnp.dtype(d).itemsize for d in out_dtypes) + 4 * len(extras))
    fixed += 0 if in_place else 4 * tm * tn
    tk = _pick_k(K if k_unit is None else k_unit, fixed,
                 2 * (tm * a.dtype.itemsize + tn * b.dtype.itemsize))
    nk = K // tk
    if a_spec is None:
        a_spec = (pl.BlockSpec((tk, tm), lambda i, j, k: (k, i)) if form == "tn"
                  else pl.BlockSpec((tm, tk), lambda i, j, k: (i, k)))
    else:
        a_spec = a_spec(tm, tn, tk)
    if b_spec is None:
        b_spec = (pl.BlockSpec((tn, tk), lambda i, j, k: (j, k)) if form == "nt"
                  else pl.BlockSpec((tk, tn), lambda i, j, k: (k, j)))
    else:
        b_spec = b_spec(tm, tn, tk)
    if o_spec is None:
        o_spec = pl.BlockSpec((tm, tn), lambda i, j, k: (i, j))
        o_shape = (M, N)
    else:
        o_spec = o_spec(tm, tn, tk)
    ex_arrays = [e for e in extras]
    ex_specs = [pl.BlockSpec((tm, tn), lambda i, j, k: (i, j)) for _ in extras]
    ne = len(ex_arrays)
    nd = len(deps)
    dep_specs = [pl.BlockSpec((8, 128), lambda i, j, k: (0, 0)) for _ in deps]
    dims = _DIMS[form]
    use_scratch = nk > 1 and not in_place

    def body(a_ref, b_ref, *rest):
        ex_refs = rest[:ne]
        o_refs = rest[ne + nd:ne + nd + n_out]
        at = a_ref[...]
        if pro is not None:
            at = pro(at)
        p = lax.dot_general(at.astype(BF16), b_ref[...].astype(BF16), dims, preferred_element_type=F32)

        def finish(acc):
            res = acc if epi is None else epi(acc, *[r[...] for r in ex_refs])
            res = res if isinstance(res, tuple) else (res,)
            for o_ref, r, d in zip(o_refs, res, out_dtypes):
                o_ref[...] = r.astype(d)

        if nk == 1:
            finish(p)
        else:
            acc_ref = rest[ne + nd + n_out] if use_scratch else o_refs[0]
            k = pl.program_id(2)

            @pl.when(k == 0)
            def _():
                acc_ref[...] = p

            @pl.when(jnp.logical_and(k > 0, k < nk - 1))
            def _():
                acc_ref[...] += p

            @pl.when(k == nk - 1)
            def _():
                finish(acc_ref[...] + p)

    res = pl.pallas_call(
        body, name=name, grid=(M // tm, N // tn, nk),
        in_specs=[a_spec, b_spec] + ex_specs + dep_specs, out_specs=[o_spec] * n_out,
        out_shape=[jax.ShapeDtypeStruct(o_shape, d) for d in out_dtypes],
        scratch_shapes=[pltpu.VMEM((tm, tn), F32)] if use_scratch else [],
        compiler_params=_cp("parallel", "parallel", "arbitrary"),
    )(a, b, *ex_arrays, *deps)
    return res if n_out > 1 else res[0]


def _colshard_b(l, n_per):
    def make(tm, tn, tk):
        nb = n_per // tn
        return pl.BlockSpec((None, None, tk, tn), lambda i, j, k: (j // nb, l, k, j % nb))
    return make


def _colshard_bt(l, n_per):
    def make(tm, tn, tk):
        nb = n_per // tk
        return pl.BlockSpec((None, None, tn, tk), lambda i, j, k: (k // nb, l, j, k % nb))
    return make


def _colshard_o(n_per):
    def make(tm, tn, tk):
        nb = n_per // tn
        return pl.BlockSpec((None, tm, tn), lambda i, j, k: (j // nb, i, j % nb))
    return make


def _rowshard_b(l, k_per):
    def make(tm, tn, tk):
        nb = k_per // tk
        return pl.BlockSpec((None, None, tk, tn), lambda i, j, k: (k // nb, l, k % nb, j))
    return make


def _rowshard_bt(l, k_per):
    def make(tm, tn, tk):
        nb = k_per // tn
        return pl.BlockSpec((None, None, tn, tk), lambda i, j, k: (j // nb, l, j % nb, k))
    return make


def _res_ln(name, xprev, h, g, b, scale=None):
    T, D = xprev.shape
    tr = _pick(T, 256)
    row = pl.BlockSpec((tr, D), lambda i: (i, 0))
    vec = pl.BlockSpec((1, D), lambda i: (0, 0))
    has_scale = scale is not None

    def body(*refs):
        if has_scale:
            x_ref, h_ref, s_ref, g_ref, b_ref, y_ref, yb_ref, xh_ref, rs_ref = refs
            hh = h_ref[...] * s_ref[...]
        else:
            x_ref, h_ref, g_ref, b_ref, y_ref, yb_ref, xh_ref, rs_ref = refs
            hh = h_ref[...]
        u = ALPHA * x_ref[...] + hh
        mu = jnp.mean(u, axis=-1, keepdims=True)
        d = u - mu
        var = jnp.mean(d * d, axis=-1, keepdims=True)
        rs = lax.rsqrt(var + LN_EPS)
        xh = d * rs
        y = xh * g_ref[...] + b_ref[...]
        y_ref[...] = y
        yb_ref[...] = y.astype(BF16)
        xh_ref[...] = xh
        rs_ref[...] = rs

    ins = [xprev, h] + ([scale] if has_scale else []) + [g, b]
    specs = [row, row] + ([vec] if has_scale else []) + [vec, vec]
    return pl.pallas_call(
        body, name=name, grid=(T // tr,), in_specs=specs,
        out_specs=[row, row, row, pl.BlockSpec((tr, 1), lambda i: (i, 0))],
        out_shape=[jax.ShapeDtypeStruct((T, D), F32), jax.ShapeDtypeStruct((T, D), BF16),
                   jax.ShapeDtypeStruct((T, D), F32), jax.ShapeDtypeStruct((T, 1), F32)],
        compiler_params=_cp("parallel"),
    )(*ins)


def _accum(ref, part, first):
    @pl.when(first)
    def _():
        ref[...] = part

    @pl.when(jnp.logical_not(first))
    def _():
        ref[...] += part


def _ln_bwd(name, dy, xh, rs, g, hraw=None, scale=None):
    T, D = dy.shape
    tr = _pick(T, 256)
    row = pl.BlockSpec((tr, D), lambda i: (i, 0))
    vec = pl.BlockSpec((1, D), lambda i: (0, 0))
    has_scale = scale is not None

    def body(*refs):
        if has_scale:
            dy_ref, xh_ref, rs_ref, g_ref, hr_ref, s_ref, du_ref, dub_ref, dg_ref, db_ref, ds_ref = refs
        else:
            dy_ref, xh_ref, rs_ref, g_ref, du_ref, dub_ref, dg_ref, db_ref = refs
        first = pl.program_id(0) == 0
        dyv = dy_ref[...]
        xhv = xh_ref[...]
        dxh = dyv * g_ref[...]
        m1 = jnp.mean(dxh, axis=-1, keepdims=True)
        m2 = jnp.mean(dxh * xhv, axis=-1, keepdims=True)
        du = rs_ref[...] * (dxh - m1 - xhv * m2)
        du_ref[...] = du
        if has_scale:
            dub_ref[...] = (du * s_ref[...]).astype(BF16)
            _accum(ds_ref, jnp.sum(du * hr_ref[...], axis=0, keepdims=True), first)
        else:
            dub_ref[...] = du.astype(BF16)
        _accum(dg_ref, jnp.sum(dyv * xhv, axis=0, keepdims=True), first)
        _accum(db_ref, jnp.sum(dyv, axis=0, keepdims=True), first)

    ins = [dy, xh, rs, g] + ([hraw, scale] if has_scale else [])
    specs = [row, row, pl.BlockSpec((tr, 1), lambda i: (i, 0)), vec] + ([row, vec] if has_scale else [])
    n_vec = 3 if has_scale else 2
    return pl.pallas_call(
        body, name=name, grid=(T // tr,), in_specs=specs,
        out_specs=[row, row] + [vec] * n_vec,
        out_shape=[jax.ShapeDtypeStruct((T, D), F32), jax.ShapeDtypeStruct((T, D), BF16)]
        + [jax.ShapeDtypeStruct((1, D), F32)] * n_vec,
        compiler_params=_cp("arbitrary"),
    )(*ins)


def _gate_fwd(name, x, gl, e):
    T, D = x.shape
    tr = _pick(T, 256)
    row = pl.BlockSpec((tr, D), lambda i: (i, 0))

    def body(x_ref, gl_ref, e_ref, xo_ref, xob_ref, gate_ref):
        gate = _sigmoid(gl_ref[...])
        xo = x_ref[...] + gate * e_ref[...]
        xo_ref[...] = xo
        xob_ref[...] = xo.astype(BF16)
        gate_ref[...] = gate

    return pl.pallas_call(
        body, name=name, grid=(T // tr,), in_specs=[row, row, row], out_specs=[row, row, row],
        out_shape=[jax.ShapeDtypeStruct((T, D), F32), jax.ShapeDtypeStruct((T, D), BF16),
                   jax.ShapeDtypeStruct((T, D), F32)],
        compiler_params=_cp("parallel"),
    )(x, gl, e)


def _gate_loss(name, x, gl, e, tgt):
    T, D = x.shape
    tr = _pick(T, 256)
    row = pl.BlockSpec((tr, D), lambda i: (i, 0))

    def body(x_ref, gl_ref, e_ref, t_ref, dy_ref, gate_ref, lp_ref):
        gate = _sigmoid(gl_ref[...])
        err = x_ref[...] + gate * e_ref[...] - t_ref[...]
        dy_ref[...] = err * (1.0 / D)
        gate_ref[...] = gate
        s = jnp.sum(jnp.mean(err * err, axis=-1, keepdims=True), axis=0, keepdims=True)
        lp_ref[...] = jnp.broadcast_to(0.5 * s, (8, 128))

    return pl.pallas_call(
        body, name=name, grid=(T // tr,), in_specs=[row] * 4,
        out_specs=[row, row, pl.BlockSpec((8, 128), lambda i: (i, 0))],
        out_shape=[jax.ShapeDtypeStruct((T, D), F32), jax.ShapeDtypeStruct((T, D), F32),
                   jax.ShapeDtypeStruct((T // tr * 8, 128), F32)],
        compiler_params=_cp("parallel"),
    )(x, gl, e, tgt)


def _gate_bwd(name, dxo, gate, e):
    T, D = dxo.shape
    tr = _pick(T, 256)
    row = pl.BlockSpec((tr, D), lambda i: (i, 0))

    def body(d_ref, gate_ref, e_ref, dgl_ref, de_ref):
        d = d_ref[...]
        gate = gate_ref[...]
        dgl_ref[...] = (d * e_ref[...] * gate * (1.0 - gate)).astype(BF16)
        de_ref[...] = (d * gate).astype(BF16)

    return pl.pallas_call(
        body, name=name, grid=(T // tr,), in_specs=[row] * 3, out_specs=[row, row],
        out_shape=[jax.ShapeDtypeStruct((T, D), BF16)] * 2,
        compiler_params=_cp("parallel"),
    )(dxo, gate, e)


def _gnorm_fwd(name, y, z, w):
    T, DI = y.shape
    tr = _pick(T, 128)
    row = pl.BlockSpec((tr, DI), lambda i: (i, 0))
    vec = pl.BlockSpec((1, DI), lambda i: (0, 0))
    col = pl.BlockSpec((tr, 1), lambda i: (i, 0))

    def body(y_ref, z_ref, w_ref, o_ref, rs_ref):
        zv = z_ref[...]
        yz = y_ref[...] * (zv * _sigmoid(zv))
        rs = lax.rsqrt(jnp.mean(yz * yz, axis=-1, keepdims=True) + RMS_EPS)
        o_ref[...] = (yz * rs * w_ref[...]).astype(BF16)
        rs_ref[...] = rs

    return pl.pallas_call(
        body, name=name, grid=(T // tr,), in_specs=[row, row, vec], out_specs=[row, col],
        out_shape=[jax.ShapeDtypeStruct((T, DI), BF16), jax.ShapeDtypeStruct((T, 1), F32)],
        compiler_params=_cp("parallel"),
    )(y, z, w)


def _gnorm_bwd(name, dyn, y, z, w, rs):
    T, DI = y.shape
    tr = _pick(T, 128)
    row = pl.BlockSpec((tr, DI), lambda i: (i, 0))
    vec = pl.BlockSpec((1, DI), lambda i: (0, 0))
    col = pl.BlockSpec((tr, 1), lambda i: (i, 0))

    def body(d_ref, y_ref, z_ref, w_ref, rs_ref, dy_ref, dz_ref, dw_ref):
        first = pl.program_id(0) == 0
        zv = z_ref[...]
        yv = y_ref[...]
        sg = _sigmoid(zv)
        sz = zv * sg
        rsv = rs_ref[...]
        yzh = yv * sz * rsv
        dv = d_ref[...]
        gw = dv * w_ref[...]
        m = jnp.mean(gw * yzh, axis=-1, keepdims=True)
        dyz = rsv * (gw - yzh * m)
        dy_ref[...] = dyz * sz
        dz_ref[...] = (dyz * yv * (sg * (1.0 + zv * (1.0 - sg)))).astype(BF16)
        _accum(dw_ref, jnp.sum(dv * yzh, axis=0, keepdims=True), first)

    return pl.pallas_call(
        body, name=name, grid=(T // tr,), in_specs=[row, row, row, vec, col], out_specs=[row, row, vec],
        out_shape=[jax.ShapeDtypeStruct((T, DI), F32), jax.ShapeDtypeStruct((T, DI), BF16),
                   jax.ShapeDtypeStruct((1, DI), F32)],
        compiler_params=_cp("arbitrary"),
    )(dyn, y, z, w, rs)


def _sel4(j, vals):
    return jnp.where(j == 0, vals[0], jnp.where(j == 1, vals[1], jnp.where(j == 2, vals[2], vals[3])))


def _pool_cnt(i, j, tr, rows, offset):
    t = i * tr + offset + lax.broadcasted_iota(jnp.int32, (rows, 1), 0)
    win = _sel4(j, POOL_WINDOWS)
    return jnp.minimum(t + 1, win).astype(F32)


def _pool_fwd(name, x):
    T, D = x.shape
    gd = D // len(POOL_WINDOWS)
    tr = _pick(T, 512)
    hb = tr // POOL_HALO

    def body(x_ref, h_ref, o_ref):
        i, j = pl.program_id(0), pl.program_id(1)
        xv = x_ref[...]
        halo = jnp.where(i > 0, h_ref[...], 0.0)
        cat = jnp.concatenate([halo, xv], axis=0)
        s2 = cat + pltpu.roll(cat, 1, 0)
        s4 = s2 + pltpu.roll(s2, 2, 0)
        s8 = s4 + pltpu.roll(s4, 4, 0)
        s16 = s8 + pltpu.roll(s8, 8, 0)
        sel = _sel4(j, (s2, s4, s8, s16))[POOL_HALO:]
        o_ref[...] = (sel / _pool_cnt(i, j, tr, tr, 0) - xv).astype(BF16)

    return pl.pallas_call(
        body, name=name, grid=(T // tr, len(POOL_WINDOWS)),
        in_specs=[pl.BlockSpec((tr, gd), lambda i, j: (i, j)),
                  pl.BlockSpec((POOL_HALO, gd), lambda i, j: (jnp.maximum(i * hb - 1, 0), j))],
        out_specs=pl.BlockSpec((tr, gd), lambda i, j: (i, j)),
        out_shape=jax.ShapeDtypeStruct((T, D), BF16),
        compiler_params=_cp("parallel", "parallel"),
    )(x, x)


def _pool_bwd(name, dp, du):
    T, D = dp.shape
    gd = D // len(POOL_WINDOWS)
    tr = _pick(T, 512)
    hb = tr // POOL_HALO
    last_h = T // POOL_HALO - 1
    n = tr + POOL_HALO

    def body(dp_ref, h_ref, du_ref, o_ref):
        i, j = pl.program_id(0), pl.program_id(1)
        dpv = dp_ref[...]
        q = dpv / _pool_cnt(i, j, tr, tr, 0)
        qh = jnp.where(i < pl.num_programs(0) - 1, h_ref[...] / _pool_cnt(i, j, tr, POOL_HALO, tr), 0.0)
        cat = jnp.concatenate([q, qh], axis=0)
        f2 = cat + pltpu.roll(cat, n - 1, 0)
        f4 = f2 + pltpu.roll(f2, n - 2, 0)
        f8 = f4 + pltpu.roll(f4, n - 4, 0)
        f16 = f8 + pltpu.roll(f8, n - 8, 0)
        sel = _sel4(j, (f2, f4, f8, f16))[:tr]
        o_ref[...] = ALPHA * du_ref[...] + sel - dpv

    return pl.pallas_call(
        body, name=name, grid=(T // tr, len(POOL_WINDOWS)),
        in_specs=[pl.BlockSpec((tr, gd), lambda i, j: (i, j)),
                  pl.BlockSpec((POOL_HALO, gd), lambda i, j: (jnp.minimum((i + 1) * hb, last_h), j)),
                  pl.BlockSpec((tr, gd), lambda i, j: (i, j))],
        out_specs=pl.BlockSpec((tr, gd), lambda i, j: (i, j)),
        out_shape=jax.ShapeDtypeStruct((T, D), F32),
        compiler_params=_cp("parallel", "parallel"),
    )(dp, dp, du)


def _conv_taps(cat, wv, rows):
    shifted = [cat[CONV_HALO:] if s == 0 else pltpu.roll(cat, s, 0)[CONV_HALO:] for s in range(CONV_K)]
    acc = shifted[0] * wv[CONV_K - 1:CONV_K]
    for s in range(1, CONV_K):
        acc = acc + shifted[s] * wv[CONV_K - 1 - s:CONV_K - s]
    return acc, shifted


def _conv_fwd(name, xp, w, b):
    T, CD = xp.shape
    tr, tc = _pick(T, 512), _pick(CD, 512)
    hb = tr // CONV_HALO

    def body(x_ref, h_ref, w_ref, b_ref, o_ref):
        i = pl.program_id(0)
        halo = jnp.where(i > 0, h_ref[...], 0.0)
        cat = jnp.concatenate([halo, x_ref[...]], axis=0)
        acc, _ = _conv_taps(cat, w_ref[...], tr)
        acc = acc + b_ref[...]
        o_ref[...] = acc * _sigmoid(acc)

    return pl.pallas_call(
        body, name=name, grid=(T // tr, CD // tc),
        in_specs=[pl.BlockSpec((tr, tc), lambda i, j: (i, j)),
                  pl.BlockSpec((CONV_HALO, tc), lambda i, j: (jnp.maximum(i * hb - 1, 0), j)),
                  pl.BlockSpec((CONV_K, tc), lambda i, j: (0, j)),
                  pl.BlockSpec((1, tc), lambda i, j: (0, j))],
        out_specs=pl.BlockSpec((tr, tc), lambda i, j: (i, j)),
        out_shape=jax.ShapeDtypeStruct((T, CD), F32),
        compiler_params=_cp("parallel", "parallel"),
    )(xp, xp, w, b)


def _conv_bwd_a(name, dxs, dbm, dcm, xp, w, b):
    T, CD = xp.shape
    tr = _pick(T, 512)
    tc = _pick(dbm.shape[1], 512)
    hb = tr // CONV_HALO
    nx, nb = dxs.shape[1] // tc, dbm.shape[1] // tc

    def part_spec(lo, n):
        def imap(j, i):
            inside = jnp.logical_and(j >= lo, j < lo + n)
            return (jnp.where(inside, i, 0), jnp.clip(j - lo, 0, n - 1))
        return pl.BlockSpec((tr, tc), imap)

    def body(dx_ref, db_ref, dc_ref, x_ref, h_ref, w_ref, b_ref, o_ref, dw_ref, dbias_ref):
        j, i = pl.program_id(0), pl.program_id(1)
        first = i == 0
        d = jnp.where(j < nx, dx_ref[...], jnp.where(j < nx + nb, db_ref[...], dc_ref[...]))
        halo = jnp.where(i > 0, h_ref[...], 0.0)
        cat = jnp.concatenate([halo, x_ref[...]], axis=0)
        acc, shifted = _conv_taps(cat, w_ref[...], tr)
        acc = acc + b_ref[...]
        sg = _sigmoid(acc)
        dconv = d * (sg * (1.0 + acc * (1.0 - sg)))
        o_ref[...] = dconv
        _accum(dbias_ref, jnp.sum(dconv, axis=0, keepdims=True), first)
        tap = lax.broadcasted_iota(jnp.int32, (CONV_K, tc), 0)
        dwv = jnp.zeros((CONV_K, tc), F32)
        for s in range(CONV_K):
            dwv = jnp.where(tap == CONV_K - 1 - s, jnp.sum(dconv * shifted[s], axis=0, keepdims=True), dwv)
        _accum(dw_ref, dwv, first)

    return pl.pallas_call(
        body, name=name, grid=(CD // tc, T // tr),
        in_specs=[part_spec(0, nx), part_spec(nx, nb), part_spec(nx + nb, nb),
                  pl.BlockSpec((tr, tc), lambda j, i: (i, j)),
                  pl.BlockSpec((CONV_HALO, tc), lambda j, i: (jnp.maximum(i * hb - 1, 0), j)),
                  pl.BlockSpec((CONV_K, tc), lambda j, i: (0, j)),
                  pl.BlockSpec((1, tc), lambda j, i: (0, j))],
        out_specs=[pl.BlockSpec((tr, tc), lambda j, i: (i, j)),
                   pl.BlockSpec((CONV_K, tc), lambda j, i: (0, j)),
                   pl.BlockSpec((1, tc), lambda j, i: (0, j))],
        out_shape=[jax.ShapeDtypeStruct((T, CD), F32), jax.ShapeDtypeStruct((CONV_K, CD), F32),
                   jax.ShapeDtypeStruct((1, CD), F32)],
        compiler_params=_cp("parallel", "arbitrary"),
    )(dxs, dbm, dcm, xp, xp, w, b)


def _conv_bwd_b(name, dconv, w):
    T, CD = dconv.shape
    tr, tc = _pick(T, 512), _pick(CD, 512)
    hb = tr // CONV_HALO
    last_h = T // CONV_HALO - 1
    n = tr + CONV_HALO

    def body(d_ref, h_ref, w_ref, o_ref):
        i = pl.program_id(0)
        halo = jnp.where(i < pl.num_programs(0) - 1, h_ref[...], 0.0)
        cat = jnp.concatenate([d_ref[...], halo], axis=0)
        wv = w_ref[...]
        acc = cat[:tr] * wv[CONV_K - 1:CONV_K]
        for s in range(1, CONV_K):
            acc = acc + pltpu.roll(cat, n - s, 0)[:tr] * wv[CONV_K - 1 - s:CONV_K - s]
        o_ref[...] = acc.astype(BF16)

    return pl.pallas_call(
        body, name=name, grid=(T // tr, CD // tc),
        in_specs=[pl.BlockSpec((tr, tc), lambda i, j: (i, j)),
                  pl.BlockSpec((CONV_HALO, tc), lambda i, j: (jnp.minimum((i + 1) * hb, last_h), j)),
                  pl.BlockSpec((CONV_K, tc), lambda i, j: (0, j))],
        out_specs=pl.BlockSpec((tr, tc), lambda i, j: (i, j)),
        out_shape=jax.ShapeDtypeStruct((T, CD), BF16),
        compiler_params=_cp("parallel", "parallel"),
    )(dconv, dconv, w)


def _dt_fwd(name, dtp, bias):
    T, H = dtp.shape

    def body(x_ref, b_ref, o_ref):
        v = x_ref[...] + b_ref[...]
        u = jnp.exp(-jnp.abs(v))
        w1 = 1.0 + u
        lp = jnp.where(w1 == 1.0, u, jnp.log(w1) * (u / jnp.where(w1 == 1.0, 1.0, w1 - 1.0)))
        o_ref[...] = jnp.maximum(v, 0.0) + lp

    return pl.pallas_call(body, name=name, out_shape=jax.ShapeDtypeStruct((T, H), F32))(dtp, bias)


def _dt_bwd(name, ddt, dtp, bias):
    T, H = dtp.shape

    def body(d_ref, x_ref, b_ref, o_ref, ob_ref, db_ref):
        g = d_ref[...] * _sigmoid(x_ref[...] + b_ref[...])
        o_ref[...] = g
        ob_ref[...] = g.astype(BF16)
        db_ref[...] = jnp.sum(g, axis=0, keepdims=True)

    return pl.pallas_call(
        body, name=name,
        out_shape=[jax.ShapeDtypeStruct((T, H), F32), jax.ShapeDtypeStruct((T, H), BF16),
                   jax.ShapeDtypeStruct((1, H), F32)],
    )(ddt, dtp, bias)


def _split(v):
    hi = v.astype(BF16)
    return hi, (v - hi.astype(F32)).astype(BF16)


def _dot01(form, a, b, mask):
    if mask == "b":
        hi, lo = _split(a)
        mb = b.astype(BF16)
        return (lax.dot_general(hi, mb, _DIMS[form], preferred_element_type=F32)
                + lax.dot_general(lo, mb, _DIMS[form], preferred_element_type=F32))
    hi, lo = _split(b)
    ma = a.astype(BF16)
    return (lax.dot_general(ma, hi, _DIMS[form], preferred_element_type=F32)
            + lax.dot_general(ma, lo, _DIMS[form], preferred_element_type=F32))


def _dotb(form, a, b):
    return lax.dot_general(a.astype(BF16), b.astype(BF16), _DIMS[form], preferred_element_type=F32)


def _ssd_common(dtc, dtr, alr, alc, gw):
    li = lax.broadcasted_iota(jnp.int32, (CHUNK, CHUNK), 0)
    si = lax.broadcasted_iota(jnp.int32, (CHUNK, CHUNK), 1)
    tri = (li >= si).astype(F32)
    trit = (li <= si).astype(F32)
    a_row = -jnp.exp(alr)
    a_col = -jnp.exp(alc)
    acs_c = _dot01("nn", tri, dtc * a_row, "a")
    acs_r = _dot01("nn", dtr * a_col, trit, "b")
    eh = lax.broadcasted_iota(jnp.int32, (HEAD_PAD, gw), 0)
    ec = lax.broadcasted_iota(jnp.int32, (HEAD_PAD, gw), 1) // HEAD_DIM
    expand = (eh == ec).astype(F32)
    th = lax.broadcasted_iota(jnp.int32, (gw, HEAD_PAD), 1)
    tc = lax.broadcasted_iota(jnp.int32, (gw, HEAD_PAD), 0) // HEAD_DIM
    reduce_ = (th == tc).astype(F32)
    acs_last = acs_c[CHUNK - 1:CHUNK, :]
    acs_e = _dot01("nn", acs_c, expand, "b")
    return dict(li=li, si=si, tri=tri, trit=trit, a_row=a_row, acs_c=acs_c, acs_r=acs_r,
                reduce=reduce_, acs_last=acs_last, dt_e=_dot01("nn", dtc, expand, "b"),
                eacs_e=jnp.exp(acs_e), dec_e=jnp.exp(acs_e[CHUNK - 1:CHUNK, :] - acs_e),
                dec_h=jnp.exp(acs_last - acs_c))


def _ssd_specs(T, DI, gw, nc, rev):
    nsb = DI // D_STATE
    cidx = (lambda c: nc - 1 - c) if rev else (lambda c: c)
    return dict(
        xs=pl.BlockSpec((CHUNK, gw), lambda g, c: (cidx(c), g)),
        bm=pl.BlockSpec((CHUNK, D_STATE), lambda g, c: (cidx(c), nsb + g)),
        cm=pl.BlockSpec((CHUNK, D_STATE), lambda g, c: (cidx(c), nsb + N_GROUPS + g)),
        dtc=pl.BlockSpec((None, CHUNK, HEAD_PAD), lambda g, c: (g, cidx(c), 0)),
        dtr=pl.BlockSpec((None, ROW_PAD, CHUNK), lambda g, c: (g, 0, cidx(c))),
        alr=pl.BlockSpec((None, 1, HEAD_PAD), lambda g, c: (g, 0, 0)),
        alc=pl.BlockSpec((None, ROW_PAD, 1), lambda g, c: (g, 0, 0)),
        de=pl.BlockSpec((None, 1, gw), lambda g, c: (g, 0, 0)),
        hp=pl.BlockSpec((None, None, gw, D_STATE), lambda g, c: (cidx(c), g, 0, 0)),
        bc=pl.BlockSpec((CHUNK, D_STATE), lambda g, c: (cidx(c), g)),
        acc=pl.BlockSpec((None, 1, HEAD_PAD), lambda g, c: (g, 0, 0)),
    )


def _ssd_fwd(name, xbc, dtc, dtr, alr, alc, d_e, DI):
    T = xbc.shape[0]
    nc = T // CHUNK
    gw = DI // N_GROUPS
    sp = _ssd_specs(T, DI, gw, nc, False)

    def body(xs_ref, b_ref, c_ref, dtc_ref, dtr_ref, alr_ref, alc_ref, de_ref, y_ref, hp_ref, h_scr):
        @pl.when(pl.program_id(1) == 0)
        def _():
            h_scr[...] = jnp.zeros_like(h_scr)

        hpv = h_scr[...]
        hp_ref[...] = hpv
        xs = xs_ref[...]
        bb = b_ref[...].astype(BF16)
        cb_ = c_ref[...].astype(BF16)
        cm = _ssd_common(dtc_ref[...], dtr_ref[...], alr_ref[...], alc_ref[...], gw)
        x = xs * cm["dt_e"]
        xb = x.astype(BF16)
        cbm = _dotb("nt", cb_, bb)
        causal = cm["li"] >= cm["si"]
        second = lax.broadcasted_iota(jnp.int32, (1, HEAD_PAIR), 1) >= HEAD_DIM
        pieces = []
        for pr in range(gw // HEAD_PAIR):
            xp = xb[:, pr * HEAD_PAIR:(pr + 1) * HEAD_PAIR]
            for h2 in range(2):
                j = 2 * pr + h2
                seg = cm["acs_c"][:, j:j + 1] - cm["acs_r"][j:j + 1, :]
                lmat = jnp.exp(jnp.where(causal, seg, -1e30))
                yj = _dotb("nn", cbm * lmat, xp)
                yp = yj if h2 == 0 else jnp.where(second, yj, yp)
            pieces.append(yp)
        ydiag = pieces[0] if len(pieces) == 1 else jnp.concatenate(pieces, axis=1)
        states = _dotb("tn", x * cm["dec_e"], bb)
        yoff = _dotb("nt", cb_, hpv) * cm["eacs_e"]
        y_ref[...] = ydiag + yoff + xs * de_ref[...]
        cd_rows = jnp.sum(cm["reduce"] * jnp.exp(cm["acs_last"]), axis=1, keepdims=True)
        h_scr[...] = cd_rows * hpv + states

    return pl.pallas_call(
        body, name=name, grid=(N_GROUPS, nc),
        in_specs=[sp["xs"], sp["bm"], sp["cm"], sp["dtc"], sp["dtr"], sp["alr"], sp["alc"], sp["de"]],
        out_specs=[sp["xs"], sp["hp"]],
        out_shape=[jax.ShapeDtypeStruct((T, DI), F32), jax.ShapeDtypeStruct((nc, N_GROUPS, gw, D_STATE), F32)],
        scratch_shapes=[pltpu.VMEM((gw, D_STATE), F32)],
        compiler_params=_cp("parallel", "arbitrary"),
    )(xbc, xbc, xbc, dtc, dtr, alr, alc, d_e)


def _ssd_bwd(name, dy, xbc, dtc, dtr, alr, alc, d_e, hprev, DI):
    T = xbc.shape[0]
    nc = T // CHUNK
    gw = DI // N_GROUPS
    sp = _ssd_specs(T, DI, gw, nc, True)

    def body(dy_ref, xs_ref, b_ref, c_ref, dtc_ref, dtr_ref, alr_ref, alc_ref, de_ref, hp_ref,
             dxs_ref, db_ref, dc_ref, ddt_ref, dal_ref, dd_ref, dh_scr):
        first = pl.program_id(1) == 0

        @pl.when(first)
        def _():
            dh_scr[...] = jnp.zeros_like(dh_scr)

        xs = xs_ref[...]
        dyv = dy_ref[...]
        bb = b_ref[...].astype(BF16)
        cb_ = c_ref[...].astype(BF16)
        dtc_v = dtc_ref[...]
        cm = _ssd_common(dtc_v, dtr_ref[...], alr_ref[...], alc_ref[...], gw)
        hpv = hp_ref[...]
        hpb = hpv.astype(BF16)
        dhn = dh_scr[...]
        dhnb = dhn.astype(BF16)
        x = xs * cm["dt_e"]
        xb = x.astype(BF16)
        cbm = _dotb("nt", cb_, bb)
        causal = cm["li"] >= cm["si"]
        second = lax.broadcasted_iota(jnp.int32, (1, HEAD_PAIR), 1) >= HEAD_DIM
        lane_pad = lax.broadcasted_iota(jnp.int32, (1, HEAD_PAD), 1)
        sub_pad = lax.broadcasted_iota(jnp.int32, (ROW_PAD, 1), 0)

        dxs = dyv * de_ref[...]
        dd_part = jnp.sum(_dot01("nn", dyv * xs, cm["reduce"], "b"), axis=0, keepdims=True)
        _accum(dd_ref, dd_part, first)

        dcb = jnp.zeros((CHUNK, CHUNK), F32)
        dacs_c = jnp.zeros((CHUNK, HEAD_PAD), F32)
        dacs_r = jnp.zeros((ROW_PAD, CHUNK), F32)
        pieces = []
        for pr in range(gw // HEAD_PAIR):
            xp = xb[:, pr * HEAD_PAIR:(pr + 1) * HEAD_PAIR]
            dyp = dyv[:, pr * HEAD_PAIR:(pr + 1) * HEAD_PAIR]
            for h2 in range(2):
                j = 2 * pr + h2
                seg = cm["acs_c"][:, j:j + 1] - cm["acs_r"][j:j + 1, :]
                lmat = jnp.exp(jnp.where(causal, seg, -1e30))
                mmat = cbm * lmat
                dym = jnp.where(second if h2 == 1 else jnp.logical_not(second), dyp, 0.0).astype(BF16)
                dm = _dotb("nt", dym, xp)
                dxj = _dotb("tn", mmat, dym)
                dxp = dxj if h2 == 0 else dxp + dxj
                dcb = dcb + dm * lmat
                dseg = dm * mmat
                dacs_c = dacs_c + jnp.where(lane_pad == j, jnp.sum(dseg, axis=1, keepdims=True), 0.0)
                dacs_r = dacs_r - jnp.where(sub_pad == j, jnp.sum(dseg, axis=0, keepdims=True), 0.0)
            pieces.append(dxp)
        dx = pieces[0] if len(pieces) == 1 else jnp.concatenate(pieces, axis=1)
        dc = _dotb("nn", dcb, bb)
        db = _dotb("tn", dcb, cb_)

        gm = _dotb("nt", cb_, hpb)
        dgm = dyv * cm["eacs_e"]
        dacs_c = dacs_c + _dot01("nn", dgm * gm, cm["reduce"], "b")
        dc = dc + _dotb("nn", dgm, hpb)
        dhp = _dotb("tn", dgm, cb_)

        cd_row = jnp.exp(cm["acs_last"])
        cd_rows = jnp.sum(cm["reduce"] * cd_row, axis=1, keepdims=True)
        dhp = dhp + cd_rows * dhn
        rsum = jnp.sum(dhn * hpv, axis=1, keepdims=True)
        dacs_last = jnp.sum(cm["reduce"] * rsum, axis=0, keepdims=True) * cd_row
        dec_e = cm["dec_e"]
        xdec = x * dec_e
        dxdec = _dotb("nt", bb, dhnb)
        db = db + _dotb("nn", xdec, dhnb)
        dx = dx + dxdec * dec_e
        tdec = _dot01("nn", dxdec * x, cm["reduce"], "b") * cm["dec_h"]
        dacs_c = dacs_c - tdec
        dacs_last = dacs_last + jnp.sum(tdec, axis=0, keepdims=True)
        row_id = lax.broadcasted_iota(jnp.int32, (CHUNK, 1), 0)
        dacs_c = dacs_c + jnp.where(row_id == CHUNK - 1, dacs_last, 0.0)

        dxs_ref[...] = dxs + dx * cm["dt_e"]
        ddt = _dot01("nn", dx * xs, cm["reduce"], "b")
        dda = _dot01("nn", cm["trit"], dacs_c, "a")
        dda_r = _dot01("nn", dacs_r, cm["tri"], "b")
        dda_rp = jnp.concatenate([dda_r, jnp.zeros((HEAD_PAD - ROW_PAD, CHUNK), F32)], axis=0)
        eye = (cm["li"] == cm["si"]).astype(F32)
        dda = dda + _dot01("nt", eye, dda_rp, "a")
        ddt_ref[...] = ddt + dda * cm["a_row"]
        _accum(dal_ref, jnp.sum(dda * dtc_v, axis=0, keepdims=True) * cm["a_row"], first)
        db_ref[...] = db
        dc_ref[...] = dc
        dh_scr[...] = dhp

    gs = D_STATE * N_GROUPS
    return pl.pallas_call(
        body, name=name, grid=(N_GROUPS, nc),
        in_specs=[sp["xs"], sp["xs"], sp["bm"], sp["cm"], sp["dtc"], sp["dtr"], sp["alr"], sp["alc"],
                  sp["de"], sp["hp"]],
        out_specs=[sp["xs"], sp["bc"], sp["bc"], sp["dtc"], sp["acc"], sp["acc"]],
        out_shape=[jax.ShapeDtypeStruct((T, DI), F32), jax.ShapeDtypeStruct((T, gs), F32),
                   jax.ShapeDtypeStruct((T, gs), F32), jax.ShapeDtypeStruct((N_GROUPS, T, HEAD_PAD), F32),
                   jax.ShapeDtypeStruct((N_GROUPS, 1, HEAD_PAD), F32),
                   jax.ShapeDtypeStruct((N_GROUPS, 1, HEAD_PAD), F32)],
        scratch_shapes=[pltpu.VMEM((gw, D_STATE), F32)],
        compiler_params=_cp("parallel", "arbitrary"),
    )(dy, xbc, xbc, xbc, dtc, dtr, alr, alc, d_e, hprev)


def _adam_math(w, g, m, v):
    m = ADAM_B1 * m + (1.0 - ADAM_B1) * g
    v = ADAM_B2 * v + (1.0 - ADAM_B2) * (g * g)
    m_hat = m / (1.0 - ADAM_B1 ** ADAM_STEP)
    v_hat = v / (1.0 - ADAM_B2 ** ADAM_STEP)
    delta = -ADAM_LR * (m_hat / (jnp.sqrt(v_hat) + ADAM_EPS) + ADAM_WD * w)
    return delta, m, v


def _adamw(name, w, m, v, g, layer=0, prev=None):
    L, R, C = w.shape
    tr = _rows(R, C * 4, ADAM_BLOCK_BYTES)
    tc = C
    while tr * tc * 4 > ADAM_BLOCK_BYTES and tc % 256 == 0:
        tc //= 2
    blk = pl.BlockSpec((None, tr, tc), lambda i, j: (layer, i, j))
    prev = [] if prev is None else list(prev)

    def body(w_ref, m_ref, v_ref, g_ref, *rest):
        go_ref, d_ref, mo_ref, vo_ref = rest[len(prev):]
        gv = g_ref[...]
        delta, mn, vn = _adam_math(w_ref[...], gv, m_ref[...], v_ref[...])
        go_ref[...] = gv
        d_ref[...] = delta
        mo_ref[...] = mn
        vo_ref[...] = vn

    return pl.pallas_call(
        body, name=name, grid=(R // tr, C // tc),
        in_specs=[blk] * 3 + [pl.BlockSpec((tr, tc), lambda i, j: (i, j))] + [_ANY] * len(prev),
        out_specs=[blk] * 4, out_shape=[jax.ShapeDtypeStruct((L, R, C), F32)] * 4,
        input_output_aliases={4 + k: k for k in range(len(prev))},
        compiler_params=_cp("parallel", "parallel"),
    )(w, m, v, g, *prev)


def _sum_peers(name, gathered):
    n, R, C = gathered.shape

    def body(g_ref, o_ref):
        acc = g_ref[0]
        for d in range(1, n):
            acc = acc + g_ref[d]
        o_ref[...] = acc

    return pl.pallas_call(body, name=name, out_shape=jax.ShapeDtypeStruct((R, C), F32))(gathered)


def _place():
    x, y, c = lax.axis_index("x"), lax.axis_index("y"), lax.axis_index("c")
    chips = [(1 - x, y), (x, 1 - y), (1 - x, 1 - y)]
    return x, y, c, chips


def _allgather_small(name, v):
    R, C = v.shape

    def body(v_ref, o_ref, send_sems, recv_sems):
        x, y, c, _ = _place()
        me = 4 * x + 2 * y + c
        o_ref[me] = v_ref[...]
        copies = []
        for k in range(1, N_DEV):
            px, py, pc = x ^ (k >> 2), y ^ ((k >> 1) & 1), c ^ (k & 1)
            copies.append(pltpu.make_async_remote_copy(
                src_ref=v_ref, dst_ref=o_ref.at[me], send_sem=send_sems.at[k - 1], recv_sem=recv_sems.at[k - 1],
                device_id=(px, py, pc), device_id_type=MESH))
        for cp in copies:
            cp.start()
        for cp in copies:
            cp.wait()

    return pl.pallas_call(
        body, name=name, out_shape=jax.ShapeDtypeStruct((N_DEV, R, C), F32),
        in_specs=[pl.BlockSpec(memory_space=pltpu.VMEM)], out_specs=pl.BlockSpec(memory_space=pltpu.VMEM),
        scratch_shapes=[pltpu.SemaphoreType.DMA((N_DEV - 1,)), pltpu.SemaphoreType.DMA((N_DEV - 1,))],
    )(v)


_HBM = pl.BlockSpec(memory_space=pltpu.HBM)
_SEM = pl.BlockSpec(memory_space=pltpu.SEMAPHORE)
_VMEM = pl.BlockSpec(memory_space=pltpu.VMEM)
_EFFECT = pltpu.SideEffectType.DATAFLOW_SIDE_EFFECTING


def _in_hbm(v):
    return pltpu.with_memory_space_constraint(v, pltpu.HBM)


def _remote(src, dst, send_sem, recv_sem, device):
    return pltpu.make_async_remote_copy(src_ref=src, dst_ref=dst, send_sem=send_sem, recv_sem=recv_sem,
                                        device_id=device, device_id_type=MESH)


def _gather_copies(src, land, ici_s, ici_r, own_s, own_r, arrivals=True):
    x, y, c, chips = _place()
    a = 2 * x + y
    sends, lands = [], []
    for i in range(len(src)):
        own = _remote(src[i], land[i].at[a], own_s.at[i], own_r.at[i], (x, y, 1 - c))
        sends.append(own)
        if arrivals:
            lands.append(own)
        for k, (px, py) in enumerate(chips):
            s, r = ici_s.at[3 * i + k], ici_r.at[3 * i + k]
            sends.append(_remote(src[i].at[c], land[i].at[a, c], s, r, (px, py, c)))
            if arrivals:
                lands.append(_remote(src[i].at[c], land[i].at[2 * px + py, c], s, r, (px, py, c)))
    return sends, lands


def _gather_start(name, shards, after=None):
    n = len(shards)
    lands = [lax.empty((N_CHIPS,) + s.shape, s.dtype) for s in shards]
    n_in = 2 * n + (0 if after is None else 1)

    def body(*refs):
        src, land = refs[:n], refs[n:2 * n]
        sems = refs[n_in:n_in + 4]
        token = refs[-1]
        sends, _ = _gather_copies(src, land, *sems, arrivals=False)
        for cp in sends:
            cp.start()
        token[...] = jnp.zeros_like(token)

    outs = pl.pallas_call(
        body, name=name,
        out_shape=(pltpu.SemaphoreType.DMA((3 * n,)), pltpu.SemaphoreType.DMA((3 * n,)),
                   pltpu.SemaphoreType.DMA((n,)), pltpu.SemaphoreType.DMA((n,)),
                   *[pltpu.HBM(s.shape, s.dtype) for s in shards], *[pltpu.HBM(l.shape, l.dtype) for l in lands],
                   jax.ShapeDtypeStruct((8, 128), F32)),
        in_specs=[_HBM] * (2 * n) + ([] if after is None else [_ANY]),
        out_specs=(_SEM,) * 4 + (_HBM,) * (2 * n) + (_VMEM,),
        input_output_aliases={i: 4 + i for i in range(2 * n)},
        compiler_params=pltpu.CompilerParams(has_side_effects=_EFFECT),
    )(*[_in_hbm(s) for s in shards], *[_in_hbm(l) for l in lands], *([] if after is None else [after]))
    return dict(sems=outs[:4], src=outs[4:4 + n], land=outs[4 + n:4 + 2 * n], token=outs[-1])


def _gather_wait(name, started, after):
    n = len(started["src"])
    after = list(after)

    def body(*refs):
        src, land = refs[:n], refs[n:2 * n]
        sems = refs[2 * n:2 * n + 4]
        sends, lands = _gather_copies(src, land, *sems)
        for cp in sends:
            cp.wait_send()
        for cp in lands:
            cp.wait_recv()

    outs = pl.pallas_call(
        body, name=name,
        out_shape=[pltpu.HBM(v.shape, v.dtype) for v in list(started["src"]) + list(started["land"])],
        in_specs=[_HBM] * (2 * n) + [_SEM] * 4 + [_ANY] * len(after), out_specs=[_HBM] * (2 * n),
        input_output_aliases={i: i for i in range(2 * n)},
        compiler_params=pltpu.CompilerParams(has_side_effects=_EFFECT),
    )(*started["src"], *started["land"], *started["sems"], *after)
    return outs[n:]


def _gather_forward(name, lands):
    n = len(lands)

    def body(*refs):
        buf = refs[n:2 * n]
        send_sems, recv_sems = refs[2 * n:]
        x, y, c, chips = _place()
        sends, lands_ = [], []
        for i in range(n):
            for k, (px, py) in enumerate(chips):
                b = 2 * px + py
                sends.append(_remote(buf[i].at[b, c], buf[i].at[b, c], send_sems.at[i, k], recv_sems.at[i, k],
                                     (x, y, 1 - c)))
                lands_.append(_remote(buf[i].at[b, 1 - c], buf[i].at[b, 1 - c], send_sems.at[i, k],
                                      recv_sems.at[i, k], (x, y, 1 - c)))
        for cp in sends:
            cp.start()
        for cp in sends:
            cp.wait_send()
        for cp in lands_:
            cp.wait_recv()

    return pl.pallas_call(
        body, name=name, in_specs=[_ANY] * n, out_specs=[_ANY] * n,
        out_shape=[jax.ShapeDtypeStruct(l.shape, l.dtype) for l in lands],
        input_output_aliases={i: i for i in range(n)},
        scratch_shapes=[pltpu.SemaphoreType.DMA((n, 3)), pltpu.SemaphoreType.DMA((n, 3))],
    )(*lands)


def _pair_exchange(name, parts):
    n = len(parts)

    def body(*refs):
        src = refs[:n]
        dst = refs[n:2 * n]
        send_sems, recv_sems = refs[2 * n:]
        x, y, c, _ = _place()
        copies = [pltpu.make_async_remote_copy(
            src_ref=src[i].at[:, 1 - c], dst_ref=dst[i], send_sem=send_sems.at[i], recv_sem=recv_sems.at[i],
            device_id=(x, y, 1 - c), device_id_type=MESH) for i in range(n)]
        for cp in copies:
            cp.start()
        for cp in copies:
            cp.wait()

    return pl.pallas_call(
        body, name=name, in_specs=[_ANY] * n, out_specs=[_ANY] * n,
        out_shape=[jax.ShapeDtypeStruct((p.shape[0],) + p.shape[2:], p.dtype) for p in parts],
        scratch_shapes=[pltpu.SemaphoreType.DMA((n,)), pltpu.SemaphoreType.DMA((n,))],
    )(*parts)


def _chip_copies(src, land, send_sems, recv_sems):
    x, y, c, chips = _place()
    return [_remote(src[i].at[2 * px + py], land[i].at[k], send_sems.at[3 * i + k], recv_sems.at[3 * i + k],
                    (px, py, c))
            for i in range(len(src)) for k, (px, py) in enumerate(chips)]


def _chip_start(name, parts):
    n = len(parts)
    lands = [lax.empty((3,) + p.shape[1:], p.dtype) for p in parts]

    def body(*refs):
        src, land = refs[:n], refs[n:2 * n]
        send_sems, recv_sems = refs[2 * n:2 * n + 2]
        token = refs[-1]
        for cp in _chip_copies(src, land, send_sems, recv_sems):
            cp.start()
        token[...] = jnp.zeros_like(token)

    outs = pl.pallas_call(
        body, name=name,
        out_shape=(pltpu.SemaphoreType.DMA((3 * n,)), pltpu.SemaphoreType.DMA((3 * n,)),
                   *[pltpu.HBM(p.shape, p.dtype) for p in parts], *[pltpu.HBM(l.shape, l.dtype) for l in lands],
                   jax.ShapeDtypeStruct((8, 128), F32)),
        in_specs=[_HBM] * (2 * n), out_specs=(_SEM,) * 2 + (_HBM,) * (2 * n) + (_VMEM,),
        input_output_aliases={i: 2 + i for i in range(2 * n)},
        compiler_params=pltpu.CompilerParams(has_side_effects=_EFFECT),
    )(*[_in_hbm(p) for p in parts], *[_in_hbm(l) for l in lands])
    return dict(sems=outs[:2], src=outs[2:2 + n], land=outs[2 + n:2 + 2 * n], token=outs[-1])


def _chip_wait(name, started, after):
    n = len(started["src"])
    after = list(after)

    def body(*refs):
        src, land = refs[:n], refs[n:2 * n]
        send_sems, recv_sems = refs[2 * n:2 * n + 2]
        copies = _chip_copies(src, land, send_sems, recv_sems)
        for cp in copies:
            cp.wait_send()
        for cp in copies:
            cp.wait_recv()

    outs = pl.pallas_call(
        body, name=name,
        out_shape=[pltpu.HBM(v.shape, v.dtype) for v in list(started["src"]) + list(started["land"])],
        in_specs=[_HBM] * (2 * n) + [_SEM] * 2 + [_ANY] * len(after), out_specs=[_HBM] * (2 * n),
        input_output_aliases={i: i for i in range(2 * n)},
        compiler_params=pltpu.CompilerParams(has_side_effects=_EFFECT),
    )(*started["src"], *started["land"], *started["sems"], *after)
    return outs[:n], outs[n:]


def _half_exchange(name, shards):
    n = len(shards)

    def body(*refs):
        buf = refs[n:2 * n]
        send_sems, recv_sems = refs[2 * n:]
        x, y, c, _ = _place()
        copies = [pltpu.make_async_remote_copy(
            src_ref=buf[i].at[c], dst_ref=buf[i].at[c], send_sem=send_sems.at[i], recv_sem=recv_sems.at[i],
            device_id=(x, y, 1 - c), device_id_type=MESH) for i in range(n)]
        for cp in copies:
            cp.start()
        for cp in copies:
            cp.wait()

    return pl.pallas_call(
        body, name=name, in_specs=[_ANY] * n, out_specs=[_ANY] * n,
        out_shape=[jax.ShapeDtypeStruct(s.shape, s.dtype) for s in shards],
        input_output_aliases={i: i for i in range(n)},
        scratch_shapes=[pltpu.SemaphoreType.DMA((n,)), pltpu.SemaphoreType.DMA((n,))],
    )(*shards)


def _pair_add(name, pos, part, sib):
    Q, _, R2, C = part.shape
    tr = _pick(R2, 256)

    def body(pos_ref, p_ref, s_ref, o_ref):
        o_ref[...] = (p_ref[...].astype(F32) + s_ref[...].astype(F32)).astype(BF16)

    return pl.pallas_call(
        body, name=name,
        grid_spec=pltpu.PrefetchScalarGridSpec(
            num_scalar_prefetch=1, grid=(Q, R2 // tr),
            in_specs=[pl.BlockSpec((None, None, tr, C), lambda q, i, pos_ref: (q, pos_ref[1], i, 0)),
                      pl.BlockSpec((None, tr, C), lambda q, i, pos_ref: (q, i, 0))],
            out_specs=pl.BlockSpec((None, tr, C), lambda q, i, pos_ref: (q, i, 0))),
        out_shape=jax.ShapeDtypeStruct((Q, R2, C), BF16),
        compiler_params=_cp("parallel", "parallel"),
    )(pos, part, sib)


def _chip_sum(name, pos, own, got):
    _, R2, C = own.shape
    tr = _pick(R2, 256)

    def body(pos_ref, o_ref, g_ref, out_ref):
        acc = o_ref[...].astype(F32)
        for k in range(3):
            acc = acc + g_ref[k].astype(F32)
        out_ref[...] = acc

    return pl.pallas_call(
        body, name=name,
        grid_spec=pltpu.PrefetchScalarGridSpec(
            num_scalar_prefetch=1, grid=(R2 // tr,),
            in_specs=[pl.BlockSpec((None, tr, C), lambda i, pos_ref: (pos_ref[0], i, 0)),
                      pl.BlockSpec((3, tr, C), lambda i, pos_ref: (0, i, 0))],
            out_specs=pl.BlockSpec((None, tr, C), lambda i, pos_ref: (pos_ref[1], i, 0))),
        out_shape=jax.ShapeDtypeStruct((2, R2, C), F32),
        compiler_params=_cp("parallel"),
    )(pos, own, got)


def _reduce_begin(tag, pos, parts):
    split = [p.reshape(p.shape[0], 2, p.shape[1] // 2, p.shape[2]) for p in parts]
    sib = _pair_exchange(tag + "_pair", split)
    chip = [_pair_add(f"{tag}_add{i}", pos, split[i], sib[i]) for i in range(len(parts))]
    return _chip_start(tag + "_chip_start", chip)


def _reduce_end(tag, pos, started, after, shapes):
    chip, got = _chip_wait(tag + "_chip_wait", started, after)
    halves = [_chip_sum(f"{tag}_sum{i}", pos, chip[i], got[i]) for i in range(len(chip))]
    full = _half_exchange(tag + "_half", halves)
    return [f.reshape(shp) for f, shp in zip(full, shapes)]


def _pad_to(v, mult):
    n = v.shape[0]
    return jnp.pad(v, (0, (-n) % mult))


def kernel(x, p, pool_w, pool_scale, ssm_in_w, ssm_conv_w, ssm_conv_b, ssm_dt_bias, ssm_a_log, ssm_d, ssm_norm_w, ssm_out_w, mlp_w1, mlp_w2, ln_g, ln_b, ple_w, ple_gate_w, loss_target, m_pool_w, m_pool_scale, m_ssm_in_w, m_ssm_conv_w, m_ssm_conv_b, m_ssm_dt_bias, m_ssm_a_log, m_ssm_d, m_ssm_norm_w, m_ssm_out_w, m_mlp_w1, m_mlp_w2, m_ln_g, m_ln_b, m_ple_w, m_ple_gate_w, v_pool_w, v_pool_scale, v_ssm_in_w, v_ssm_conv_w, v_ssm_conv_b, v_ssm_dt_bias, v_ssm_a_log, v_ssm_d, v_ssm_norm_w, v_ssm_out_w, v_mlp_w1, v_mlp_w2, v_ln_g, v_ln_b, v_ple_w, v_ple_gate_w):
    T, D = x.shape[1], x.shape[2]
    NG = len(POOL_WINDOWS)
    GD = D // NG
    DI = ssm_out_w.shape[1] * N_CHIPS
    H = ssm_dt_bias.shape[1]
    HPG = H // N_GROUPS
    GW = DI // N_GROUPS
    GS = N_GROUPS * D_STATE
    CD = DI + 2 * GS
    DF = mlp_w1.shape[2] * N_CHIPS
    PD = ple_w.shape[1]
    NIN = ssm_in_w.shape[2]

    xi, yi, ci = lax.axis_index("x"), lax.axis_index("y"), lax.axis_index("c")
    chip = 2 * xi + yi
    pos = jnp.stack([chip, ci]).astype(jnp.int32)

    x0 = x[0]
    tgt = loss_target[0]
    p0b, p1b = p[0, 0].astype(BF16), p[1, 0].astype(BF16)

    small_sh = jnp.concatenate([ssm_conv_w[0].reshape(-1), ssm_conv_b[0], ssm_norm_w[0],
                                ln_g.reshape(-1), ln_b.reshape(-1)])
    n_sh = small_sh.shape[0]
    small_all = _allgather_small("gather_small", _pad_to(small_sh, 1024).reshape(-1, 128))

    def halves(w, zero=None):
        w = w if zero is None else w + zero
        return w.astype(BF16).reshape((2, w.shape[0] // 2) + w.shape[1:])

    sh_pool = pool_w[0].astype(BF16)
    sh_pool = sh_pool.reshape((2, NG // 2) + sh_pool.shape[1:])
    started = {"ag0a": _gather_start("ag0a_start", [sh_pool, halves(mlp_w1[0])], after=small_all)}
    z0 = started["ag0a"]["token"][0, 0]
    groups = [("ag0b", [halves(mlp_w2[0], z0), halves(ple_gate_w[0], z0), halves(ple_w[0], z0)]),
              ("ag1a", [halves(ssm_in_w[0], z0)]),
              ("ag1b", [halves(ssm_out_w[0], z0), halves(mlp_w1[1], z0)]),
              ("ag1c", [halves(mlp_w2[1], z0), halves(ple_gate_w[1], z0), halves(ple_w[1], z0)])]

    def gather_end(tag, after):
        return _gather_forward(tag + "_fwd", _gather_wait(tag + "_wait", started[tag], after))

    g_pool, g = gather_end("ag0a", [sh for _, shards in groups for sh in shards])
    prev = g_pool
    for tag, shards in groups:
        started[tag] = _gather_start(tag + "_start", shards, after=prev)
        prev = started[tag]["token"]
    all_started = prev

    g_w1, g_w2, g_pw, g_gw = {}, {}, {}, {}

    def set_w1(l, g):
        g_w1[l] = g.reshape(N_CHIPS, 1, D, DF // N_CHIPS)

    def set_rest(l, g2, ggw, gpw):
        g_w2[l] = g2.reshape(N_CHIPS, 1, DF // N_CHIPS, D)
        g_gw[l] = ggw.reshape(N_CHIPS, 1, D // N_CHIPS, D)
        g_pw[l] = gpw.reshape(N_CHIPS, 1, PD, D // N_CHIPS)

    set_w1(0, g)
    w_pool = jnp.transpose(g_pool.reshape(N_CHIPS, NG, GD // N_CHIPS, GD), (1, 0, 2, 3)).reshape(NG, GD, GD)

    small_all = small_all.reshape(N_DEV, -1)[0::2, :n_sh]
    cdq, niq, dq = CD // N_CHIPS, DI // N_CHIPS, D // N_CHIPS
    o = 0
    conv_w = jnp.concatenate([small_all[q, o:o + CONV_K * cdq].reshape(CONV_K, cdq) for q in range(N_CHIPS)], axis=1)
    o += CONV_K * cdq
    conv_b = small_all[:, o:o + cdq].reshape(1, CD)
    o += cdq
    norm_w = small_all[:, o:o + niq].reshape(1, DI)
    o += niq
    lng = jnp.transpose(small_all[:, o:o + 4 * dq].reshape(N_CHIPS, 2, 2, dq), (1, 2, 0, 3)).reshape(2, 2, 1, D)
    o += 4 * dq
    lnb = jnp.transpose(small_all[:, o:o + 4 * dq].reshape(N_CHIPS, 2, 2, dq), (1, 2, 0, 3)).reshape(2, 2, 1, D)

    pooled = _pool_fwd("pool_fwd", x0)
    hraw = _mm("pool_mm", "nn", pooled, w_pool, T, D, GD, tn=GD, deps=[all_started],
               a_spec=lambda tm, tn, tk: pl.BlockSpec((tm, tk), lambda i, j, k: (i, j)),
               b_spec=lambda tm, tn, tk: pl.BlockSpec((None, tk, tn), lambda i, j, k: (j, 0, 0)))
    x1, x1b, xh1, rs1 = _res_ln("ln00", x0, hraw, lng[0, 0], lnb[0, 0], scale=pool_scale)

    def mlp_fwd(l, xb, rest_tag):
        a, h2b = _mm(f"mlp{l}_up", "nn", xb, g_w1[l], T, DF, D, tn=min(1024, DF // N_CHIPS),
                     b_spec=_colshard_b(0, DF // N_CHIPS), out_dtype=(F32, BF16),
                     epi=lambda acc: (acc, jnp.square(jnp.maximum(acc, 0.0))))
        set_rest(l, *gather_end(rest_tag, [a]))
        h = _mm(f"mlp{l}_down", "nn", h2b, g_w2[l], T, D, DF, k_unit=DF // N_CHIPS,
                b_spec=_rowshard_b(0, DF // N_CHIPS))
        return a, h2b, h

    def ple_fwd(l, xb, pb):
        gl = _mm(f"gate{l}_logit", "nn", xb, g_gw[l], T, D, D, k_unit=D // N_CHIPS,
                 b_spec=_rowshard_b(0, D // N_CHIPS))
        e = _mm(f"gate{l}_emb", "nn", pb, g_pw[l], T, D, PD, tn=min(1024, D // N_CHIPS),
                b_spec=_colshard_b(0, D // N_CHIPS))
        return gl, e

    a0, h2b0, h0 = mlp_fwd(0, x1b, "ag0b")
    x2, x2b, xh2, rs2 = _res_ln("ln01", x1, h0, lng[0, 1], lnb[0, 1])
    gl0, e0 = ple_fwd(0, x2b, p0b)
    x3, x3b, gate0 = _gate_fwd("gate0", x2, gl0, e0)

    g_in, = gather_end("ag1a", [x3b])

    def in_proj_cols(lo, hi):
        parts = [g_in[q].reshape(D, NIN)[:, max(lo - q * NIN, 0):min(hi - q * NIN, NIN)]
                 for q in range(N_CHIPS) if lo < (q + 1) * NIN and hi > q * NIN]
        return parts[0] if len(parts) == 1 else jnp.concatenate(parts, axis=1)

    w_z, w_xbc, w_dt = in_proj_cols(0, DI), in_proj_cols(DI, DI + CD), in_proj_cols(DI + CD, N_CHIPS * NIN)
    z = _mm("ssm_in_z", "nn", x3b, w_z, T, DI, D)
    xbc_pre = _mm("ssm_in_xbc", "nn", x3b, w_xbc, T, CD, D)
    dt_pre = _mm("ssm_in_dt", "nn", x3b, w_dt, T, H, D)
    xbc = _conv_fwd("conv_fwd", xbc_pre, conv_w, conv_b)
    dt = _dt_fwd("dt_fwd", dt_pre, ssm_dt_bias)
    dt_g = jnp.transpose(dt.reshape(T, N_GROUPS, HPG), (1, 0, 2))
    dtc = jnp.pad(dt_g, ((0, 0), (0, 0), (0, HEAD_PAD - HPG)))
    dtr = jnp.pad(jnp.transpose(dt_g, (0, 2, 1)), ((0, 0), (0, ROW_PAD - HPG), (0, 0)))
    al_g = ssm_a_log.reshape(N_GROUPS, HPG)
    alr = jnp.pad(al_g, ((0, 0), (0, HEAD_PAD - HPG)))[:, None, :]
    alc = jnp.pad(al_g, ((0, 0), (0, ROW_PAD - HPG)))[:, :, None]
    d_e = jnp.repeat(ssm_d.reshape(N_GROUPS, HPG), HEAD_DIM, axis=1)[:, None, :]
    ysc, hprev = _ssd_fwd("ssd_fwd", xbc, dtc, dtr, alr, alc, d_e, DI)
    g_out, g = gather_end("ag1b", [ysc])
    set_w1(1, g)
    w_out = g_out.reshape(DI, D)
    ynb, rsn = _gnorm_fwd("gnorm_fwd", ysc, z, norm_w)
    h1 = _mm("ssm_out", "nn", ynb, w_out, T, D, DI)
    x4, x4b, xh4, rs4 = _res_ln("ln10", x3, h1, lng[1, 0], lnb[1, 0])
    a1, h2b1, h2 = mlp_fwd(1, x4b, "ag1c")
    x5, x5b, xh5, rs5 = _res_ln("ln11", x4, h2, lng[1, 1], lnb[1, 1])
    gl1, e1 = ple_fwd(1, x5b, p1b)
    dx6, gate1, loss_parts = _gate_loss("gate1_loss", x5, gl1, e1, tgt)
    loss_local = jnp.sum(loss_parts[0::8, 0])

    def ple_bwd(l, dxo, gate, e, xb, pb, deps=()):
        dgl, de = _gate_bwd(f"gate{l}_bwd", dxo, gate, e)
        d_gw = _mm(f"gate{l}_dw", "tn", xb, dgl, D, D, T, out_dtype=BF16, deps=deps).reshape(N_CHIPS, D // N_CHIPS, D)
        d_pw = _mm(f"gate{l}_dpw", "tn", pb, de, PD, D, T, out_dtype=BF16, tn=min(1024, D // N_CHIPS),
                   o_shape=(N_CHIPS, PD, D // N_CHIPS), o_spec=_colshard_o(D // N_CHIPS))
        dx = _mm(f"gate{l}_dx", "nt", dgl, g_gw[l], T, D, D, tn=min(1024, D // N_CHIPS), deps=deps,
                 b_spec=_rowshard_bt(0, D // N_CHIPS), epi=lambda acc, r: acc + r, extras=[dxo])
        return dx, d_gw, d_pw

    def mlp_bwd(l, du, dub, a, h2b, xb):
        d_w2 = _mm(f"mlp{l}_dw2", "tn", h2b, dub, DF, D, T, out_dtype=BF16).reshape(N_CHIPS, DF // N_CHIPS, D)
        da = _mm(f"mlp{l}_da", "nt", dub, g_w2[l], T, DF, D, tn=min(1024, DF // N_CHIPS),
                 b_spec=_rowshard_bt(0, DF // N_CHIPS), out_dtype=BF16,
                 epi=lambda acc, av: acc * (2.0 * jnp.maximum(av, 0.0)), extras=[a])
        d_w1 = _mm(f"mlp{l}_dw1", "tn", xb, da, D, DF, T, out_dtype=BF16, tn=min(1024, DF // N_CHIPS),
                   o_shape=(N_CHIPS, D, DF // N_CHIPS), o_spec=_colshard_o(DF // N_CHIPS))
        dx = _mm(f"mlp{l}_dx", "nt", da, g_w1[l], T, D, DF, k_unit=DF // N_CHIPS,
                 b_spec=_colshard_bt(0, DF // N_CHIPS), epi=lambda acc, r: acc + ALPHA * r, extras=[du])
        return dx, d_w1, d_w2

    dx5, d_gw1, d_pw1 = ple_bwd(1, dx6, gate1, e1, x5b, p1b)
    du5, du5b, dg11, db11 = _ln_bwd("ln11_bwd", dx5, xh5, rs5, lng[1, 1])
    dx4, d_w1_1, d_w2_1 = mlp_bwd(1, du5, du5b, a1, h2b1, x4b)
    parts1a = [d_w1_1, d_w2_1, d_gw1, d_pw1]
    red1a = _reduce_begin("rs1a", pos, parts1a)
    du4, du4b, dg10, db10 = _ln_bwd("ln10_bwd", dx4, xh4, rs4, lng[1, 0])
    d_wout = _mm("ssm_out_dw", "tn", ynb, du4b, DI, D, T, out_dtype=BF16,
                 deps=[red1a["token"]]).reshape(N_CHIPS, DI // N_CHIPS, D)
    dyn = _mm("ssm_out_dx", "nt", du4b, w_out, T, DI, D, deps=[red1a["token"]])
    dysc, dzb, dnorm_w = _gnorm_bwd("gnorm_bwd", dyn, ysc, z, norm_w, rsn)
    dxs, dbm, dcm, ddt_g, dalog_g, dd_g = _ssd_bwd("ssd_bwd", dysc, xbc, dtc, dtr, alr, alc, d_e, hprev, DI)
    dconv, dconv_w, dconv_b = _conv_bwd_a("conv_bwd_a", dxs, dbm, dcm, xbc_pre, conv_w, conv_b)
    dxbcb = _conv_bwd_b("conv_bwd_b", dconv, conv_w)
    ddt = jnp.transpose(ddt_g[:, :, :HPG], (1, 0, 2)).reshape(T, H)
    _, ddtpb, ddt_bias = _dt_bwd("dt_bwd", ddt, dt_pre, ssm_dt_bias)
    da_log = dalog_g[:, 0, :HPG].reshape(1, H)
    dd_skip = dd_g[:, 0, :HPG].reshape(1, H)
    d_wz = _mm("ssm_in_dwz", "tn", x3b, dzb, D, DI, T, out_dtype=BF16)
    d_wx = _mm("ssm_in_dwx", "tn", x3b, dxbcb, D, CD, T, out_dtype=BF16)
    d_wdt = _mm("ssm_in_dwdt", "tn", x3b, ddtpb, D, H, T, out_dtype=BF16)

    def in_proj_shard(q):
        parts = []
        for piece, start in ((d_wz, 0), (d_wx, DI), (d_wdt, DI + CD)):
            lo, hi = max(q * NIN - start, 0), min((q + 1) * NIN - start, piece.shape[1])
            if lo < hi:
                parts.append(piece[:, lo:hi])
        return parts[0] if len(parts) == 1 else jnp.concatenate(parts, axis=1)

    d_win = jnp.stack([in_proj_shard(q) for q in range(N_CHIPS)])
    dx3 = _mm("ssm_in_dxz", "nt", dzb, w_z, T, D, DI, epi=lambda acc, r: acc + ALPHA * r, extras=[du4])
    dx3 = _mm("ssm_in_dxx", "nt", dxbcb, w_xbc, T, D, CD, epi=lambda acc, r: acc + r, extras=[dx3])
    dx3 = _mm("ssm_in_dxdt", "nt", ddtpb, w_dt, T, D, H, epi=lambda acc, r: acc + r, extras=[dx3])

    parts1b = [d_win, d_wout]
    red1b = _reduce_begin("rs1b", pos, parts1b)

    dx2, d_gw0, d_pw0 = ple_bwd(0, dx3, gate0, e0, x2b, p0b, deps=[red1b["token"]])
    du2, du2b, dg01, db01 = _ln_bwd("ln01_bwd", dx2, xh2, rs2, lng[0, 1])
    dx1, d_w1_0, d_w2_0 = mlp_bwd(0, du2, du2b, a0, h2b0, x1b)
    du1, dhrb, dg00, db00, dscale = _ln_bwd("ln00_bwd", dx1, xh1, rs1, lng[0, 0], hraw=hraw, scale=pool_scale)
    d_wpool = _mm("pool_dw", "tn", pooled, dhrb, D, GD, T, tm=GD, tn=GD, out_dtype=BF16,
                  b_spec=lambda tm, tn, tk: pl.BlockSpec((tk, tn), lambda i, j, k: (k, i)),
                  o_shape=(NG, GD, GD), o_spec=lambda tm, tn, tk: pl.BlockSpec((None, tm, tn), lambda i, j, k: (i, 0, 0)))
    dpooled = _mm("pool_dx", "nt", dhrb, w_pool, T, D, GD, tn=GD,
                  a_spec=lambda tm, tn, tk: pl.BlockSpec((tm, tk), lambda i, j, k: (i, j)),
                  b_spec=lambda tm, tn, tk: pl.BlockSpec((None, tn, tk), lambda i, j, k: (j, 0, 0)))
    grad_x = _pool_bwd("pool_bwd", dpooled, du1)
    d_wpool = jnp.transpose(d_wpool.reshape(NG, N_CHIPS, GD // N_CHIPS, GD), (1, 0, 2, 3)).reshape(N_CHIPS, NG * GD // N_CHIPS, GD)

    parts0 = [d_wpool, d_w1_0, d_w2_0, d_gw0, d_pw0]
    red0 = _reduce_begin("rs0", pos, parts0)

    def adam(name, w, m, v, g, layer=0, prev=None):
        w3, m3, v3 = (t.reshape((t.shape[0], -1, t.shape[-1])) for t in (w, m, v))
        return _adamw(name, w3, m3, v3, g, layer, prev)

    r_w1_1, r_w2_1, r_gw1, r_pw1 = _reduce_end("rs1a", pos, red1a, [red0["token"]], [p.shape[1:] for p in parts1a])
    stacked = {"mlp_w1": (mlp_w1, m_mlp_w1, v_mlp_w1), "mlp_w2": (mlp_w2, m_mlp_w2, v_mlp_w2),
               "ple_w": (ple_w, m_ple_w, v_ple_w), "ple_gate_w": (ple_gate_w, m_ple_gate_w, v_ple_gate_w)}
    grads1 = {"mlp_w1": r_w1_1, "mlp_w2": r_w2_1, "ple_w": r_pw1, "ple_gate_w": r_gw1}
    upper = {nm: adam(f"adam_{nm}_l1", *stacked[nm], grads1[nm], layer=1) for nm in stacked}
    r_in, r_out = _reduce_end("rs1b", pos, red1b, [upper["mlp_w1"][1]], [p.shape[1:] for p in parts1b])
    big = {"ssm_in_w": [jnp.swapaxes(t, 1, 2) for t in adam(
               "adam_ssm_in_w", *(jnp.swapaxes(t, 1, 2) for t in (ssm_in_w, m_ssm_in_w, v_ssm_in_w)), r_in.T)],
           "ssm_out_w": adam("adam_ssm_out_w", ssm_out_w, m_ssm_out_w, v_ssm_out_w, r_out)}

    dln_g = jnp.stack([jnp.stack([dg00, dg01]), jnp.stack([dg10, dg11])]).reshape(-1)
    dln_b = jnp.stack([jnp.stack([db00, db01]), jnp.stack([db10, db11])]).reshape(-1)
    small_g = jnp.concatenate([dscale.reshape(-1), dconv_w.reshape(-1), dconv_b.reshape(-1), ddt_bias.reshape(-1),
                               da_log.reshape(-1), dd_skip.reshape(-1), dnorm_w.reshape(-1), dln_g, dln_b,
                               loss_local.reshape(1)])
    n_sg = small_g.shape[0]
    sg_all = _allgather_small("gather_small_grads", _pad_to(small_g, 1024).reshape(-1, 128))
    sg = _sum_peers("sum_small_grads", sg_all).reshape(-1)[:n_sg]
    o = 0

    def take(nel):
        nonlocal o
        v = sg[o:o + nel]
        o += nel
        return v

    g_scale = take(D).reshape(1, D)
    g_conv_w_full = take(CONV_K * CD).reshape(CONV_K, CD)
    g_conv_b_full = take(CD).reshape(1, CD)
    g_dt_bias = take(H).reshape(1, H)
    g_a_log = take(H).reshape(1, H)
    g_d = take(H).reshape(1, H)
    g_norm_full = take(DI).reshape(1, DI)
    g_lng_full = take(4 * D).reshape(2, 2, D)
    g_lnb_full = take(4 * D).reshape(2, 2, D)
    loss = take(1).reshape(())
    g_conv_w = lax.dynamic_slice_in_dim(g_conv_w_full, chip * cdq, cdq, axis=1)[None]
    g_conv_b = lax.dynamic_slice_in_dim(g_conv_b_full, chip * cdq, cdq, axis=1)
    g_norm = lax.dynamic_slice_in_dim(g_norm_full, chip * niq, niq, axis=1)
    g_lng = lax.dynamic_slice_in_dim(g_lng_full, chip * dq, dq, axis=2)
    g_lnb = lax.dynamic_slice_in_dim(g_lnb_full, chip * dq, dq, axis=2)

    small = [("pool_scale", pool_scale, m_pool_scale, v_pool_scale, g_scale),
             ("ssm_conv_w", ssm_conv_w, m_ssm_conv_w, v_ssm_conv_w, g_conv_w),
             ("ssm_conv_b", ssm_conv_b, m_ssm_conv_b, v_ssm_conv_b, g_conv_b),
             ("ssm_dt_bias", ssm_dt_bias, m_ssm_dt_bias, v_ssm_dt_bias, g_dt_bias),
             ("ssm_a_log", ssm_a_log, m_ssm_a_log, v_ssm_a_log, g_a_log),
             ("ssm_d", ssm_d, m_ssm_d, v_ssm_d, g_d),
             ("ssm_norm_w", ssm_norm_w, m_ssm_norm_w, v_ssm_norm_w, g_norm),
             ("ln_g", ln_g, m_ln_g, v_ln_g, g_lng),
             ("ln_b", ln_b, m_ln_b, v_ln_b, g_lnb)]

    def pack(idx):
        flat = _pad_to(jnp.concatenate([s[idx].reshape(-1) for s in small]), 1024)
        return flat.reshape(1, -1, 128)

    sm_out = _adamw("adam_small", pack(1), pack(2), pack(3), pack(4)[0])
    small_res = {}
    o = 0
    for s in small:
        nel = s[1].size
        small_res[s[0]] = [t.reshape(-1)[o:o + nel].reshape(s[1].shape) for t in sm_out]
        o += nel

    r_pool, r_w1_0, r_w2_0, r_gw0, r_pw0 = _reduce_end(
        "rs0", pos, red0, [big["ssm_in_w"][1], big["ssm_out_w"][1], sm_out[1]] + [upper[nm][1] for nm in upper],
        [p.shape[1:] for p in parts0])
    grads0 = {"mlp_w1": r_w1_0, "mlp_w2": r_w2_0, "ple_w": r_pw0, "ple_gate_w": r_gw0}
    big["pool_w"] = adam("adam_pool_w", pool_w, m_pool_w, v_pool_w, r_pool)
    for nm in stacked:
        big[nm] = adam(f"adam_{nm}_l0", *stacked[nm], grads0[nm], layer=0, prev=upper[nm])
    shapes = {"pool_w": pool_w.shape, "ssm_in_w": ssm_in_w.shape, "ssm_out_w": ssm_out_w.shape,
              **{nm: stacked[nm][0].shape for nm in stacked}}
    big = {nm: [t.reshape(shapes[nm]) for t in big[nm]] for nm in big}

    order = ["pool_w", "pool_scale", "ssm_in_w", "ssm_conv_w", "ssm_conv_b", "ssm_dt_bias", "ssm_a_log", "ssm_d",
             "ssm_norm_w", "ssm_out_w", "mlp_w1", "mlp_w2", "ln_g", "ln_b", "ple_w", "ple_gate_w"]
    res = {**big, **small_res}
    outs = [loss, grad_x[None]]
    for kind in range(4):
        outs += [res[nm][kind] for nm in order]
    return tuple(outs)
```

```python
import jax
import jax.numpy as jnp
from jax import lax
from jax.experimental import pallas as pl
from jax.experimental.pallas import tpu as pltpu

F32 = jnp.float32
BF16 = jnp.bfloat16
MESH = pl.DeviceIdType.MESH

DEPTH = 2
ALPHA = (2.0 * DEPTH) ** 0.25
LN_EPS = 1e-5
RMS_EPS = 1e-5
POOL_WINDOWS = (2, 4, 8, 16)
POOL_HALO = 16
HEAD_DIM = 64
HEAD_PAIR = 2 * HEAD_DIM
N_GROUPS = 8
D_STATE = 128
CHUNK = 128
CONV_K = 4
CONV_HALO = 8
HEAD_PAD = 128
ROW_PAD = 8
N_CHIPS = 4
N_DEV = 8
ADAM_LR = 0.001
ADAM_B1 = 0.9
ADAM_B2 = 0.999
ADAM_EPS = 1e-08
ADAM_WD = 0.01
ADAM_STEP = 10
VMEM_LIMIT = 56 * 1024 * 1024
ADAM_BLOCK_BYTES = 1024 * 1024
MM_VMEM_BUDGET = 40 * 1024 * 1024


_ANY = pl.BlockSpec(memory_space=pl.ANY)


def _cp(*sem):
    return pltpu.CompilerParams(dimension_semantics=sem, vmem_limit_bytes=VMEM_LIMIT)


def _pick(dim, pref):
    t = pref
    while t >= 128:
        if dim % t == 0:
            return t
        t //= 2
    return dim


def _rows(rows, row_bytes, budget):
    t = rows
    while t * row_bytes > budget and t % 16 == 0:
        t //= 2
    return t


def _sigmoid(v):
    return 1.0 / (1.0 + jnp.exp(-v))


_DIMS = {"nn": (((1,), (0,)), ((), ())), "nt": (((1,), (1,)), ((), ())), "tn": (((0,), (0,)), ((), ()))}


def _pick_k(k_unit, fixed_bytes, per_k_bytes):
    for n in range(1, k_unit // 128 + 1):
        if k_unit % n == 0 and (n == 1 or (k_unit // n) % 128 == 0):
            if fixed_bytes + (k_unit // n) * per_k_bytes <= MM_VMEM_BUDGET:
                return k_unit // n
    return min(k_unit, 128)


def _mm(name, form, a, b, M, N, K, *, tm=1024, tn=1024, k_unit=None, a_spec=None, b_spec=None,
        o_shape=None, o_spec=None, out_dtype=F32, pro=None, epi=None, extras=(), deps=()):
    tm, tn = _pick(M, tm), _pick(N, tn)
    out_dtypes = out_dtype if isinstance(out_dtype, tuple) else (out_dtype,)
    n_out = len(out_dtypes)
    in_place = n_out == 1 and out_dtypes[0] == F32
    fixed = 2 * tm * tn * (sum(jnp.dtype(d).itemsize for d in out_dtypes) + 4 * len(extras))
    fixed += 0 if in_place else 4 * tm * tn
    tk = _pick_k(K if k_unit is None else k_unit, fixed,
                 2 * (tm * a.dtype.itemsize + tn * b.dtype.itemsize))
    nk = K // tk
    if a_spec is None:
        a_spec = (pl.BlockSpec((tk, tm), lambda i, j, k: (k, i)) if form == "tn"
                  else pl.BlockSpec((tm, tk), lambda i, j, k: (i, k)))
    else:
        a_spec = a_spec(tm, tn, tk)
    if b_spec is None:
        b_spec = (pl.BlockSpec((tn, tk), lambda i, j, k: (j, k)) if form == "nt"
                  else pl.BlockSpec((tk, tn), lambda i, j, k: (k, j)))
    else:
        b_spec = b_spec(tm, tn, tk)
    if o_spec is None:
        o_spec = pl.BlockSpec((tm, tn), lambda i, j, k: (i, j))
        o_shape = (M, N)
    else:
        o_spec = o_spec(tm, tn, tk)
    ex_arrays = [e for e in extras]
    ex_specs = [pl.BlockSpec((tm, tn), lambda i, j, k: (i, j)) for _ in extras]
    ne = len(ex_arrays)
    nd = len(deps)
    dep_specs = [pl.BlockSpec((8, 128), lambda i, j, k: (0, 0)) for _ in deps]
    dims = _DIMS[form]
    use_scratch = nk > 1 and not in_place

    def body(a_ref, b_ref, *rest):
        ex_refs = rest[:ne]
        o_refs = rest[ne + nd:ne + nd + n_out]
        at = a_ref[...]
        if pro is not None:
            at = pro(at)
        p = lax.dot_general(at.astype(BF16), b_ref[...].astype(BF16), dims, preferred_element_type=F32)

        def finish(acc):
            res = acc if epi is None else epi(acc, *[r[...] for r in ex_refs])
            res = res if isinstance(res, tuple) else (res,)
            for o_ref, r, d in zip(o_refs, res, out_dtypes):
                o_ref[...] = r.astype(d)

        if nk == 1:
            finish(p)
        else:
            acc_ref = rest[ne + nd + n_out] if use_scratch else o_refs[0]
            k = pl.program_id(2)

            @pl.when(k == 0)
            def _():
                acc_ref[...] = p

            @pl.when(jnp.logical_and(k > 0, k < nk - 1))
            def _():
                acc_ref[...] += p

            @pl.when(k == nk - 1)
            def _():
                finish(acc_ref[...] + p)

    res = pl.pallas_call(
        body, name=name, grid=(M // tm, N // tn, nk),
        in_specs=[a_spec, b_spec] + ex_specs + dep_specs, out_specs=[o_spec] * n_out,
        out_shape=[jax.ShapeDtypeStruct(o_shape, d) for d in out_dtypes],
        scratch_shapes=[pltpu.VMEM((tm, tn), F32)] if use_scratch else [],
        compiler_params=_cp("parallel", "parallel", "arbitrary"),
    )(a, b, *ex_arrays, *deps)
    return res if n_out > 1 else res[0]


def _colshard_b(l, n_per):
    def make(tm, tn, tk):
        nb = n_per // tn
        return pl.BlockSpec((None, None, tk, tn), lambda i, j, k: (j // nb, l, k, j % nb))
    return make


def _colshard_bt(l, n_per):
    def make(tm, tn, tk):
        nb = n_per // tk
        return pl.BlockSpec((None, None, tn, tk), lambda i, j, k: (k // nb, l, j, k % nb))
    return make


def _colshard_o(n_per):
    def make(tm, tn, tk):
        nb = n_per // tn
        return pl.BlockSpec((None, tm, tn), lambda i, j, k: (j // nb, i, j % nb))
    return make


def _rowshard_b(l, k_per):
    def make(tm, tn, tk):
        nb = k_per // tk
        return pl.BlockSpec((None, None, tk, tn), lambda i, j, k: (k // nb, l, k % nb, j))
    return make


def _rowshard_bt(l, k_per):
    def make(tm, tn, tk):
        nb = k_per // tn
        return pl.BlockSpec((None, None, tn, tk), lambda i, j, k: (j // nb, l, j % nb, k))
    return make


def _res_ln(name, xprev, h, g, b, scale=None):
    T, D = xprev.shape
    tr = _pick(T, 256)
    row = pl.BlockSpec((tr, D), lambda i: (i, 0))
    vec = pl.BlockSpec((1, D), lambda i: (0, 0))
    has_scale = scale is not None

    def body(*refs):
        if has_scale:
            x_ref, h_ref, s_ref, g_ref, b_ref, y_ref, yb_ref, xh_ref, rs_ref = refs
            hh = h_ref[...] * s_ref[...]
        else:
            x_ref, h_ref, g_ref, b_ref, y_ref, yb_ref, xh_ref, rs_ref = refs
            hh = h_ref[...]
        u = ALPHA * x_ref[...] + hh
        mu = jnp.mean(u, axis=-1, keepdims=True)
        d = u - mu
        var = jnp.mean(d * d, axis=-1, keepdims=True)
        rs = lax.rsqrt(var + LN_EPS)
        xh = d * rs
        y = xh * g_ref[...] + b_ref[...]
        y_ref[...] = y
        yb_ref[...] = y.astype(BF16)
        xh_ref[...] = xh
        rs_ref[...] = rs

    ins = [xprev, h] + ([scale] if has_scale else []) + [g, b]
    specs = [row, row] + ([vec] if has_scale else []) + [vec, vec]
    return pl.pallas_call(
        body, name=name, grid=(T // tr,), in_specs=specs,
        out_specs=[row, row, row, pl.BlockSpec((tr, 1), lambda i: (i, 0))],
        out_shape=[jax.ShapeDtypeStruct((T, D), F32), jax.ShapeDtypeStruct((T, D), BF16),
                   jax.ShapeDtypeStruct((T, D), F32), jax.ShapeDtypeStruct((T, 1), F32)],
        compiler_params=_cp("parallel"),
    )(*ins)


def _accum(ref, part, first):
    @pl.when(first)
    def _():
        ref[...] = part

    @pl.when(jnp.logical_not(first))
    def _():
        ref[...] += part


def _ln_bwd(name, dy, xh, rs, g, hraw=None, scale=None):
    T, D = dy.shape
    tr = _pick(T, 256)
    row = pl.BlockSpec((tr, D), lambda i: (i, 0))
    vec = pl.BlockSpec((1, D), lambda i: (0, 0))
    has_scale = scale is not None

    def body(*refs):
        if has_scale:
            dy_ref, xh_ref, rs_ref, g_ref, hr_ref, s_ref, du_ref, dub_ref, dg_ref, db_ref, ds_ref = refs
        else:
            dy_ref, xh_ref, rs_ref, g_ref, du_ref, dub_ref, dg_ref, db_ref = refs
        first = pl.program_id(0) == 0
        dyv = dy_ref[...]
        xhv = xh_ref[...]
        dxh = dyv * g_ref[...]
        m1 = jnp.mean(dxh, axis=-1, keepdims=True)
        m2 = jnp.mean(dxh * xhv, axis=-1, keepdims=True)
        du = rs_ref[...] * (dxh - m1 - xhv * m2)
        du_ref[...] = du
        if has_scale:
            dub_ref[...] = (du * s_ref[...]).astype(BF16)
            _accum(ds_ref, jnp.sum(du * hr_ref[...], axis=0, keepdims=True), first)
        else:
            dub_ref[...] = du.astype(BF16)
        _accum(dg_ref, jnp.sum(dyv * xhv, axis=0, keepdims=True), first)
        _accum(db_ref, jnp.sum(dyv, axis=0, keepdims=True), first)

    ins = [dy, xh, rs, g] + ([hraw, scale] if has_scale else [])
    specs = [row, row, pl.BlockSpec((tr, 1), lambda i: (i, 0)), vec] + ([row, vec] if has_scale else [])
    n_vec = 3 if has_scale else 2
    return pl.pallas_call(
        body, name=name, grid=(T // tr,), in_specs=specs,
        out_specs=[row, row] + [vec] * n_vec,
        out_shape=[jax.ShapeDtypeStruct((T, D), F32), jax.ShapeDtypeStruct((T, D), BF16)]
        + [jax.ShapeDtypeStruct((1, D), F32)] * n_vec,
        compiler_params=_cp("arbitrary"),
    )(*ins)


def _gate_fwd(name, x, gl, e):
    T, D = x.shape
    tr = _pick(T, 256)
    row = pl.BlockSpec((tr, D), lambda i: (i, 0))

    def body(x_ref, gl_ref, e_ref, xo_ref, xob_ref, gate_ref):
        gate = _sigmoid(gl_ref[...])
        xo = x_ref[...] + gate * e_ref[...]
        xo_ref[...] = xo
        xob_ref[...] = xo.astype(BF16)
        gate_ref[...] = gate

    return pl.pallas_call(
        body, name=name, grid=(T // tr,), in_specs=[row, row, row], out_specs=[row, row, row],
        out_shape=[jax.ShapeDtypeStruct((T, D), F32), jax.ShapeDtypeStruct((T, D), BF16),
                   jax.ShapeDtypeStruct((T, D), F32)],
        compiler_params=_cp("parallel"),
    )(x, gl, e)


def _gate_loss(name, x, gl, e, tgt):
    T, D = x.shape
    tr = _pick(T, 256)
    row = pl.BlockSpec((tr, D), lambda i: (i, 0))

    def body(x_ref, gl_ref, e_ref, t_ref, dy_ref, gate_ref, lp_ref):
        gate = _sigmoid(gl_ref[...])
        err = x_ref[...] + gate * e_ref[...] - t_ref[...]
        dy_ref[...] = err * (1.0 / D)
        gate_ref[...] = gate
        s = jnp.sum(jnp.mean(err * err, axis=-1, keepdims=True), axis=0, keepdims=True)
        lp_ref[...] = jnp.broadcast_to(0.5 * s, (8, 128))

    return pl.pallas_call(
        body, name=name, grid=(T // tr,), in_specs=[row] * 4,
        out_specs=[row, row, pl.BlockSpec((8, 128), lambda i: (i, 0))],
        out_shape=[jax.ShapeDtypeStruct((T, D), F32), jax.ShapeDtypeStruct((T, D), F32),
                   jax.ShapeDtypeStruct((T // tr * 8, 128), F32)],
        compiler_params=_cp("parallel"),
    )(x, gl, e, tgt)


def _gate_bwd(name, dxo, gate, e):
    T, D = dxo.shape
    tr = _pick(T, 256)
    row = pl.BlockSpec((tr, D), lambda i: (i, 0))

    def body(d_ref, gate_ref, e_ref, dgl_ref, de_ref):
        d = d_ref[...]
        gate = gate_ref[...]
        dgl_ref[...] = (d * e_ref[...] * gate * (1.0 - gate)).astype(BF16)
        de_ref[...] = (d * gate).astype(BF16)

    return pl.pallas_call(
        body, name=name, grid=(T // tr,), in_specs=[row] * 3, out_specs=[row, row],
        out_shape=[jax.ShapeDtypeStruct((T, D), BF16)] * 2,
        compiler_params=_cp("parallel"),
    )(dxo, gate, e)


def _gnorm_fwd(name, y, z, w):
    T, DI = y.shape
    tr = _pick(T, 128)
    row = pl.BlockSpec((tr, DI), lambda i: (i, 0))
    vec = pl.BlockSpec((1, DI), lambda i: (0, 0))
    col = pl.BlockSpec((tr, 1), lambda i: (i, 0))

    def body(y_ref, z_ref, w_ref, o_ref, rs_ref):
        zv = z_ref[...]
        yz = y_ref[...] * (zv * _sigmoid(zv))
        rs = lax.rsqrt(jnp.mean(yz * yz, axis=-1, keepdims=True) + RMS_EPS)
        o_ref[...] = (yz * rs * w_ref[...]).astype(BF16)
        rs_ref[...] = rs

    return pl.pallas_call(
        body, name=name, grid=(T // tr,), in_specs=[row, row, vec], out_specs=[row, col],
        out_shape=[jax.ShapeDtypeStruct((T, DI), BF16), jax.ShapeDtypeStruct((T, 1), F32)],
        compiler_params=_cp("parallel"),
    )(y, z, w)


def _gnorm_bwd(name, dyn, y, z, w, rs):
    T, DI = y.shape
    tr = _pick(T, 128)
    row = pl.BlockSpec((tr, DI), lambda i: (i, 0))
    vec = pl.BlockSpec((1, DI), lambda i: (0, 0))
    col = pl.BlockSpec((tr, 1), lambda i: (i, 0))

    def body(d_ref, y_ref, z_ref, w_ref, rs_ref, dy_ref, dz_ref, dw_ref):
        first = pl.program_id(0) == 0
        zv = z_ref[...]
        yv = y_ref[...]
        sg = _sigmoid(zv)
        sz = zv * sg
        rsv = rs_ref[...]
        yzh = yv * sz * rsv
        dv = d_ref[...]
        gw = dv * w_ref[...]
        m = jnp.mean(gw * yzh, axis=-1, keepdims=True)
        dyz = rsv * (gw - yzh * m)
        dy_ref[...] = dyz * sz
        dz_ref[...] = (dyz * yv * (sg * (1.0 + zv * (1.0 - sg)))).astype(BF16)
        _accum(dw_ref, jnp.sum(dv * yzh, axis=0, keepdims=True), first)

    return pl.pallas_call(
        body, name=name, grid=(T // tr,), in_specs=[row, row, row, vec, col], out_specs=[row, row, vec],
        out_shape=[jax.ShapeDtypeStruct((T, DI), F32), jax.ShapeDtypeStruct((T, DI), BF16),
                   jax.ShapeDtypeStruct((1, DI), F32)],
        compiler_params=_cp("arbitrary"),
    )(dyn, y, z, w, rs)


def _sel4(j, vals):
    return jnp.where(j == 0, vals[0], jnp.where(j == 1, vals[1], jnp.where(j == 2, vals[2], vals[3])))


def _pool_cnt(i, j, tr, rows, offset):
    t = i * tr + offset + lax.broadcasted_iota(jnp.int32, (rows, 1), 0)
    win = _sel4(j, POOL_WINDOWS)
    return jnp.minimum(t + 1, win).astype(F32)


def _pool_fwd(name, x):
    T, D = x.shape
    gd = D // len(POOL_WINDOWS)
    tr = _pick(T, 512)
    hb = tr // POOL_HALO

    def body(x_ref, h_ref, o_ref):
        i, j = pl.program_id(0), pl.program_id(1)
        xv = x_ref[...]
        halo = jnp.where(i > 0, h_ref[...], 0.0)
        cat = jnp.concatenate([halo, xv], axis=0)
        s2 = cat + pltpu.roll(cat, 1, 0)
        s4 = s2 + pltpu.roll(s2, 2, 0)
        s8 = s4 + pltpu.roll(s4, 4, 0)
        s16 = s8 + pltpu.roll(s8, 8, 0)
        sel = _sel4(j, (s2, s4, s8, s16))[POOL_HALO:]
        o_ref[...] = (sel / _pool_cnt(i, j, tr, tr, 0) - xv).astype(BF16)

    return pl.pallas_call(
        body, name=name, grid=(T // tr, len(POOL_WINDOWS)),
        in_specs=[pl.BlockSpec((tr, gd), lambda i, j: (i, j)),
                  pl.BlockSpec((POOL_HALO, gd), lambda i, j: (jnp.maximum(i * hb - 1, 0), j))],
        out_specs=pl.BlockSpec((tr, gd), lambda i, j: (i, j)),
        out_shape=jax.ShapeDtypeStruct((T, D), BF16),
        compiler_params=_cp("parallel", "parallel"),
    )(x, x)


def _pool_bwd(name, dp, du):
    T, D = dp.shape
    gd = D // len(POOL_WINDOWS)
    tr = _pick(T, 512)
    hb = tr // POOL_HALO
    last_h = T // POOL_HALO - 1
    n = tr + POOL_HALO

    def body(dp_ref, h_ref, du_ref, o_ref):
        i, j = pl.program_id(0), pl.program_id(1)
        dpv = dp_ref[...]
        q = dpv / _pool_cnt(i, j, tr, tr, 0)
        qh = jnp.where(i < pl.num_programs(0) - 1, h_ref[...] / _pool_cnt(i, j, tr, POOL_HALO, tr), 0.0)
        cat = jnp.concatenate([q, qh], axis=0)
        f2 = cat + pltpu.roll(cat, n - 1, 0)
        f4 = f2 + pltpu.roll(f2, n - 2, 0)
        f8 = f4 + pltpu.roll(f4, n - 4, 0)
        f16 = f8 + pltpu.roll(f8, n - 8, 0)
        sel = _sel4(j, (f2, f4, f8, f16))[:tr]
        o_ref[...] = ALPHA * du_ref[...] + sel - dpv

    return pl.pallas_call(
        body, name=name, grid=(T // tr, len(POOL_WINDOWS)),
        in_specs=[pl.BlockSpec((tr, gd), lambda i, j: (i, j)),
                  pl.BlockSpec((POOL_HALO, gd), lambda i, j: (jnp.minimum((i + 1) * hb, last_h), j)),
                  pl.BlockSpec((tr, gd), lambda i, j: (i, j))],
        out_specs=pl.BlockSpec((tr, gd), lambda i, j: (i, j)),
        out_shape=jax.ShapeDtypeStruct((T, D), F32),
        compiler_params=_cp("parallel", "parallel"),
    )(dp, dp, du)


def _conv_taps(cat, wv, rows):
    shifted = [cat[CONV_HALO:] if s == 0 else pltpu.roll(cat, s, 0)[CONV_HALO:] for s in range(CONV_K)]
    acc = shifted[0] * wv[CONV_K - 1:CONV_K]
    for s in range(1, CONV_K):
        acc = acc + shifted[s] * wv[CONV_K - 1 - s:CONV_K - s]
    return acc, shifted


def _conv_fwd(name, xp, w, b):
    T, CD = xp.shape
    tr, tc = _pick(T, 512), _pick(CD, 512)
    hb = tr // CONV_HALO

    def body(x_ref, h_ref, w_ref, b_ref, o_ref):
        i = pl.program_id(0)
        halo = jnp.where(i > 0, h_ref[...], 0.0)
        cat = jnp.concatenate([halo, x_ref[...]], axis=0)
        acc, _ = _conv_taps(cat, w_ref[...], tr)
        acc = acc + b_ref[...]
        o_ref[...] = acc * _sigmoid(acc)

    return pl.pallas_call(
        body, name=name, grid=(T // tr, CD // tc),
        in_specs=[pl.BlockSpec((tr, tc), lambda i, j: (i, j)),
                  pl.BlockSpec((CONV_HALO, tc), lambda i, j: (jnp.maximum(i * hb - 1, 0), j)),
                  pl.BlockSpec((CONV_K, tc), lambda i, j: (0, j)),
                  pl.BlockSpec((1, tc), lambda i, j: (0, j))],
        out_specs=pl.BlockSpec((tr, tc), lambda i, j: (i, j)),
        out_shape=jax.ShapeDtypeStruct((T, CD), F32),
        compiler_params=_cp("parallel", "parallel"),
    )(xp, xp, w, b)


def _conv_bwd_a(name, dxs, dbm, dcm, xp, w, b):
    T, CD = xp.shape
    tr = _pick(T, 512)
    tc = _pick(dbm.shape[1], 512)
    hb = tr // CONV_HALO
    nx, nb = dxs.shape[1] // tc, dbm.shape[1] // tc

    def part_spec(lo, n):
        def imap(j, i):
            inside = jnp.logical_and(j >= lo, j < lo + n)
            return (jnp.where(inside, i, 0), jnp.clip(j - lo, 0, n - 1))
        return pl.BlockSpec((tr, tc), imap)

    def body(dx_ref, db_ref, dc_ref, x_ref, h_ref, w_ref, b_ref, o_ref, dw_ref, dbias_ref):
        j, i = pl.program_id(0), pl.program_id(1)
        first = i == 0
        d = jnp.where(j < nx, dx_ref[...], jnp.where(j < nx + nb, db_ref[...], dc_ref[...]))
        halo = jnp.where(i > 0, h_ref[...], 0.0)
        cat = jnp.concatenate([halo, x_ref[...]], axis=0)
        acc, shifted = _conv_taps(cat, w_ref[...], tr)
        acc = acc + b_ref[...]
        sg = _sigmoid(acc)
        dconv = d * (sg * (1.0 + acc * (1.0 - sg)))
        o_ref[...] = dconv
        _accum(dbias_ref, jnp.sum(dconv, axis=0, keepdims=True), first)
        tap = lax.broadcasted_iota(jnp.int32, (CONV_K, tc), 0)
        dwv = jnp.zeros((CONV_K, tc), F32)
        for s in range(CONV_K):
            dwv = jnp.where(tap == CONV_K - 1 - s, jnp.sum(dconv * shifted[s], axis=0, keepdims=True), dwv)
        _accum(dw_ref, dwv, first)

    return pl.pallas_call(
        body, name=name, grid=(CD // tc, T // tr),
        in_specs=[part_spec(0, nx), part_spec(nx, nb), part_spec(nx + nb, nb),
                  pl.BlockSpec((tr, tc), lambda j, i: (i, j)),
                  pl.BlockSpec((CONV_HALO, tc), lambda j, i: (jnp.maximum(i * hb - 1, 0), j)),
                  pl.BlockSpec((CONV_K, tc), lambda j, i: (0, j)),
                  pl.BlockSpec((1, tc), lambda j, i: (0, j))],
        out_specs=[pl.BlockSpec((tr, tc), lambda j, i: (i, j)),
                   pl.BlockSpec((CONV_K, tc), lambda j, i: (0, j)),
                   pl.BlockSpec((1, tc), lambda j, i: (0, j))],
        out_shape=[jax.ShapeDtypeStruct((T, CD), F32), jax.ShapeDtypeStruct((CONV_K, CD), F32),
                   jax.ShapeDtypeStruct((1, CD), F32)],
        compiler_params=_cp("parallel", "arbitrary"),
    )(dxs, dbm, dcm, xp, xp, w, b)


def _conv_bwd_b(name, dconv, w):
    T, CD = dconv.shape
    tr, tc = _pick(T, 512), _pick(CD, 512)
    hb = tr // CONV_HALO
    last_h = T // CONV_HALO - 1
    n = tr + CONV_HALO

    def body(d_ref, h_ref, w_ref, o_ref):
        i = pl.program_id(0)
        halo = jnp.where(i < pl.num_programs(0) - 1, h_ref[...], 0.0)
        cat = jnp.concatenate([d_ref[...], halo], axis=0)
        wv = w_ref[...]
        acc = cat[:tr] * wv[CONV_K - 1:CONV_K]
        for s in range(1, CONV_K):
            acc = acc + pltpu.roll(cat, n - s, 0)[:tr] * wv[CONV_K - 1 - s:CONV_K - s]
        o_ref[...] = acc.astype(BF16)

    return pl.pallas_call(
        body, name=name, grid=(T // tr, CD // tc),
        in_specs=[pl.BlockSpec((tr, tc), lambda i, j: (i, j)),
                  pl.BlockSpec((CONV_HALO, tc), lambda i, j: (jnp.minimum((i + 1) * hb, last_h), j)),
                  pl.BlockSpec((CONV_K, tc), lambda i, j: (0, j))],
        out_specs=pl.BlockSpec((tr, tc), lambda i, j: (i, j)),
        out_shape=jax.ShapeDtypeStruct((T, CD), BF16),
        compiler_params=_cp("parallel", "parallel"),
    )(dconv, dconv, w)


def _dt_fwd(name, dtp, bias):
    T, H = dtp.shape

    def body(x_ref, b_ref, o_ref):
        v = x_ref[...] + b_ref[...]
        u = jnp.exp(-jnp.abs(v))
        w1 = 1.0 + u
        lp = jnp.where(w1 == 1.0, u, jnp.log(w1) * (u / jnp.where(w1 == 1.0, 1.0, w1 - 1.0)))
        o_ref[...] = jnp.maximum(v, 0.0) + lp

    return pl.pallas_call(body, name=name, out_shape=jax.ShapeDtypeStruct((T, H), F32))(dtp, bias)


def _dt_bwd(name, ddt, dtp, bias):
    T, H = dtp.shape

    def body(d_ref, x_ref, b_ref, o_ref, ob_ref, db_ref):
        g = d_ref[...] * _sigmoid(x_ref[...] + b_ref[...])
        o_ref[...] = g
        ob_ref[...] = g.astype(BF16)
        db_ref[...] = jnp.sum(g, axis=0, keepdims=True)

    return pl.pallas_call(
        body, name=name,
        out_shape=[jax.ShapeDtypeStruct((T, H), F32), jax.ShapeDtypeStruct((T, H), BF16),
                   jax.ShapeDtypeStruct((1, H), F32)],
    )(ddt, dtp, bias)


def _split(v):
    hi = v.astype(BF16)
    return hi, (v - hi.astype(F32)).astype(BF16)


def _dot01(form, a, b, mask):
    if mask == "b":
        hi, lo = _split(a)
        mb = b.astype(BF16)
        return (lax.dot_general(hi, mb, _DIMS[form], preferred_element_type=F32)
                + lax.dot_general(lo, mb, _DIMS[form], preferred_element_type=F32))
    hi, lo = _split(b)
    ma = a.astype(BF16)
    return (lax.dot_general(ma, hi, _DIMS[form], preferred_element_type=F32)
            + lax.dot_general(ma, lo, _DIMS[form], preferred_element_type=F32))


def _dotb(form, a, b):
    return lax.dot_general(a.astype(BF16), b.astype(BF16), _DIMS[form], preferred_element_type=F32)


def _ssd_common(dtc, dtr, alr, alc, gw):
    li = lax.broadcasted_iota(jnp.int32, (CHUNK, CHUNK), 0)
    si = lax.broadcasted_iota(jnp.int32, (CHUNK, CHUNK), 1)
    tri = (li >= si).astype(F32)
    trit = (li <= si).astype(F32)
    a_row = -jnp.exp(alr)
    a_col = -jnp.exp(alc)
    acs_c = _dot01("nn", tri, dtc * a_row, "a")
    acs_r = _dot01("nn", dtr * a_col, trit, "b")
    eh = lax.broadcasted_iota(jnp.int32, (HEAD_PAD, gw), 0)
    ec = lax.broadcasted_iota(jnp.int32, (HEAD_PAD, gw), 1) // HEAD_DIM
    expand = (eh == ec).astype(F32)
    th = lax.broadcasted_iota(jnp.int32, (gw, HEAD_PAD), 1)
    tc = lax.broadcasted_iota(jnp.int32, (gw, HEAD_PAD), 0) // HEAD_DIM
    reduce_ = (th == tc).astype(F32)
    acs_last = acs_c[CHUNK - 1:CHUNK, :]
    acs_e = _dot01("nn", acs_c, expand, "b")
    return dict(li=li, si=si, tri=tri, trit=trit, a_row=a_row, acs_c=acs_c, acs_r=acs_r,
                reduce=reduce_, acs_last=acs_last, dt_e=_dot01("nn", dtc, expand, "b"),
                eacs_e=jnp.exp(acs_e), dec_e=jnp.exp(acs_e[CHUNK - 1:CHUNK, :] - acs_e),
                dec_h=jnp.exp(acs_last - acs_c))


def _ssd_specs(T, DI, gw, nc, rev):
    nsb = DI // D_STATE
    cidx = (lambda c: nc - 1 - c) if rev else (lambda c: c)
    return dict(
        xs=pl.BlockSpec((CHUNK, gw), lambda g, c: (cidx(c), g)),
        bm=pl.BlockSpec((CHUNK, D_STATE), lambda g, c: (cidx(c), nsb + g)),
        cm=pl.BlockSpec((CHUNK, D_STATE), lambda g, c: (cidx(c), nsb + N_GROUPS + g)),
        dtc=pl.BlockSpec((None, CHUNK, HEAD_PAD), lambda g, c: (g, cidx(c), 0)),
        dtr=pl.BlockSpec((None, ROW_PAD, CHUNK), lambda g, c: (g, 0, cidx(c))),
        alr=pl.BlockSpec((None, 1, HEAD_PAD), lambda g, c: (g, 0, 0)),
        alc=pl.BlockSpec((None, ROW_PAD, 1), lambda g, c: (g, 0, 0)),
        de=pl.BlockSpec((None, 1, gw), lambda g, c: (g, 0, 0)),
        hp=pl.BlockSpec((None, None, gw, D_STATE), lambda g, c: (cidx(c), g, 0, 0)),
        bc=pl.BlockSpec((CHUNK, D_STATE), lambda g, c: (cidx(c), g)),
        acc=pl.BlockSpec((None, 1, HEAD_PAD), lambda g, c: (g, 0, 0)),
    )


def _ssd_fwd(name, xbc, dtc, dtr, alr, alc, d_e, DI, deps=()):
    T = xbc.shape[0]
    nc = T // CHUNK
    gw = DI // N_GROUPS
    sp = _ssd_specs(T, DI, gw, nc, False)

    def body(xs_ref, b_ref, c_ref, dtc_ref, dtr_ref, alr_ref, alc_ref, de_ref, *rest):
        y_ref, hp_ref, h_scr = rest[len(deps):]

        @pl.when(pl.program_id(1) == 0)
        def _():
            h_scr[...] = jnp.zeros_like(h_scr)

        hpv = h_scr[...]
        hp_ref[...] = hpv
        xs = xs_ref[...]
        bb = b_ref[...].astype(BF16)
        cb_ = c_ref[...].astype(BF16)
        cm = _ssd_common(dtc_ref[...], dtr_ref[...], alr_ref[...], alc_ref[...], gw)
        x = xs * cm["dt_e"]
        xb = x.astype(BF16)
        cbm = _dotb("nt", cb_, bb)
        causal = cm["li"] >= cm["si"]
        second = lax.broadcasted_iota(jnp.int32, (1, HEAD_PAIR), 1) >= HEAD_DIM
        pieces = []
        for pr in range(gw // HEAD_PAIR):
            xp = xb[:, pr * HEAD_PAIR:(pr + 1) * HEAD_PAIR]
            for h2 in range(2):
                j = 2 * pr + h2
                seg = cm["acs_c"][:, j:j + 1] - cm["acs_r"][j:j + 1, :]
                lmat = jnp.exp(jnp.where(causal, seg, -1e30))
                yj = _dotb("nn", cbm * lmat, xp)
                yp = yj if h2 == 0 else jnp.where(second, yj, yp)
            pieces.append(yp)
        ydiag = pieces[0] if len(pieces) == 1 else jnp.concatenate(pieces, axis=1)
        states = _dotb("tn", x * cm["dec_e"], bb)
        yoff = _dotb("nt", cb_, hpv) * cm["eacs_e"]
        y_ref[...] = ydiag + yoff + xs * de_ref[...]
        cd_rows = jnp.sum(cm["reduce"] * jnp.exp(cm["acs_last"]), axis=1, keepdims=True)
        h_scr[...] = cd_rows * hpv + states

    return pl.pallas_call(
        body, name=name, grid=(N_GROUPS, nc),
        in_specs=[sp["xs"], sp["bm"], sp["cm"], sp["dtc"], sp["dtr"], sp["alr"], sp["alc"], sp["de"]]
        + [pl.BlockSpec((8, 128), lambda g, c: (0, 0)) for _ in deps],
        out_specs=[sp["xs"], sp["hp"]],
        out_shape=[jax.ShapeDtypeStruct((T, DI), F32), jax.ShapeDtypeStruct((nc, N_GROUPS, gw, D_STATE), F32)],
        scratch_shapes=[pltpu.VMEM((gw, D_STATE), F32)],
        compiler_params=_cp("parallel", "arbitrary"),
    )(xbc, xbc, xbc, dtc, dtr, alr, alc, d_e, *deps)


def _ssd_bwd(name, dy, xbc, dtc, dtr, alr, alc, d_e, hprev, DI):
    T = xbc.shape[0]
    nc = T // CHUNK
    gw = DI // N_GROUPS
    sp = _ssd_specs(T, DI, gw, nc, True)

    def body(dy_ref, xs_ref, b_ref, c_ref, dtc_ref, dtr_ref, alr_ref, alc_ref, de_ref, hp_ref,
             dxs_ref, db_ref, dc_ref, ddt_ref, dal_ref, dd_ref, dh_scr):
        first = pl.program_id(1) == 0

        @pl.when(first)
        def _():
            dh_scr[...] = jnp.zeros_like(dh_scr)

        xs = xs_ref[...]
        dyv = dy_ref[...]
        bb = b_ref[...].astype(BF16)
        cb_ = c_ref[...].astype(BF16)
        dtc_v = dtc_ref[...]
        cm = _ssd_common(dtc_v, dtr_ref[...], alr_ref[...], alc_ref[...], gw)
        hpv = hp_ref[...]
        hpb = hpv.astype(BF16)
        dhn = dh_scr[...]
        dhnb = dhn.astype(BF16)
        x = xs * cm["dt_e"]
        xb = x.astype(BF16)
        cbm = _dotb("nt", cb_, bb)
        causal = cm["li"] >= cm["si"]
        second = lax.broadcasted_iota(jnp.int32, (1, HEAD_PAIR), 1) >= HEAD_DIM
        lane_pad = lax.broadcasted_iota(jnp.int32, (1, HEAD_PAD), 1)
        sub_pad = lax.broadcasted_iota(jnp.int32, (ROW_PAD, 1), 0)

        dxs = dyv * de_ref[...]
        dd_part = jnp.sum(_dot01("nn", dyv * xs, cm["reduce"], "b"), axis=0, keepdims=True)
        _accum(dd_ref, dd_part, first)

        dcb = jnp.zeros((CHUNK, CHUNK), F32)
        dacs_c = jnp.zeros((CHUNK, HEAD_PAD), F32)
        dacs_r = jnp.zeros((ROW_PAD, CHUNK), F32)
        pieces = []
        for pr in range(gw // HEAD_PAIR):
            xp = xb[:, pr * HEAD_PAIR:(pr + 1) * HEAD_PAIR]
            dyp = dyv[:, pr * HEAD_PAIR:(pr + 1) * HEAD_PAIR]
            for h2 in range(2):
                j = 2 * pr + h2
                seg = cm["acs_c"][:, j:j + 1] - cm["acs_r"][j:j + 1, :]
                lmat = jnp.exp(jnp.where(causal, seg, -1e30))
                mmat = cbm * lmat
                dym = jnp.where(second if h2 == 1 else jnp.logical_not(second), dyp, 0.0).astype(BF16)
                dm = _dotb("nt", dym, xp)
                dxj = _dotb("tn", mmat, dym)
                dxp = dxj if h2 == 0 else dxp + dxj
                dcb = dcb + dm * lmat
                dseg = dm * mmat
                dacs_c = dacs_c + jnp.where(lane_pad == j, jnp.sum(dseg, axis=1, keepdims=True), 0.0)
                dacs_r = dacs_r - jnp.where(sub_pad == j, jnp.sum(dseg, axis=0, keepdims=True), 0.0)
            pieces.append(dxp)
        dx = pieces[0] if len(pieces) == 1 else jnp.concatenate(pieces, axis=1)
        dc = _dotb("nn", dcb, bb)
        db = _dotb("tn", dcb, cb_)

        gm = _dotb("nt", cb_, hpb)
        dgm = dyv * cm["eacs_e"]
        dacs_c = dacs_c + _dot01("nn", dgm * gm, cm["reduce"], "b")
        dc = dc + _dotb("nn", dgm, hpb)
        dhp = _dotb("tn", dgm, cb_)

        cd_row = jnp.exp(cm["acs_last"])
        cd_rows = jnp.sum(cm["reduce"] * cd_row, axis=1, keepdims=True)
        dhp = dhp + cd_rows * dhn
        rsum = jnp.sum(dhn * hpv, axis=1, keepdims=True)
        dacs_last = jnp.sum(cm["reduce"] * rsum, axis=0, keepdims=True) * cd_row
        dec_e = cm["dec_e"]
        xdec = x * dec_e
        dxdec = _dotb("nt", bb, dhnb)
        db = db + _dotb("nn", xdec, dhnb)
        dx = dx + dxdec * dec_e
        tdec = _dot01("nn", dxdec * x, cm["reduce"], "b") * cm["dec_h"]
        dacs_c = dacs_c - tdec
        dacs_last = dacs_last + jnp.sum(tdec, axis=0, keepdims=True)
        row_id = lax.broadcasted_iota(jnp.int32, (CHUNK, 1), 0)
        dacs_c = dacs_c + jnp.where(row_id == CHUNK - 1, dacs_last, 0.0)

        dxs_ref[...] = dxs + dx * cm["dt_e"]
        ddt = _dot01("nn", dx * xs, cm["reduce"], "b")
        dda = _dot01("nn", cm["trit"], dacs_c, "a")
        dda_r = _dot01("nn", dacs_r, cm["tri"], "b")
        dda_rp = jnp.concatenate([dda_r, jnp.zeros((HEAD_PAD - ROW_PAD, CHUNK), F32)], axis=0)
        eye = (cm["li"] == cm["si"]).astype(F32)
        dda = dda + _dot01("nt", eye, dda_rp, "a")
        ddt_ref[...] = ddt + dda * cm["a_row"]
        _accum(dal_ref, jnp.sum(dda * dtc_v, axis=0, keepdims=True) * cm["a_row"], first)
        db_ref[...] = db
        dc_ref[...] = dc
        dh_scr[...] = dhp

    gs = D_STATE * N_GROUPS
    return pl.pallas_call(
        body, name=name, grid=(N_GROUPS, nc),
        in_specs=[sp["xs"], sp["xs"], sp["bm"], sp["cm"], sp["dtc"], sp["dtr"], sp["alr"], sp["alc"],
                  sp["de"], sp["hp"]],
        out_specs=[sp["xs"], sp["bc"], sp["bc"], sp["dtc"], sp["acc"], sp["acc"]],
        out_shape=[jax.ShapeDtypeStruct((T, DI), F32), jax.ShapeDtypeStruct((T, gs), F32),
                   jax.ShapeDtypeStruct((T, gs), F32), jax.ShapeDtypeStruct((N_GROUPS, T, HEAD_PAD), F32),
                   jax.ShapeDtypeStruct((N_GROUPS, 1, HEAD_PAD), F32),
                   jax.ShapeDtypeStruct((N_GROUPS, 1, HEAD_PAD), F32)],
        scratch_shapes=[pltpu.VMEM((gw, D_STATE), F32)],
        compiler_params=_cp("parallel", "arbitrary"),
    )(dy, xbc, xbc, xbc, dtc, dtr, alr, alc, d_e, hprev)


def _adam_math(w, g, m, v):
    m = ADAM_B1 * m + (1.0 - ADAM_B1) * g
    v = ADAM_B2 * v + (1.0 - ADAM_B2) * (g * g)
    m_hat = m / (1.0 - ADAM_B1 ** ADAM_STEP)
    v_hat = v / (1.0 - ADAM_B2 ** ADAM_STEP)
    delta = -ADAM_LR * (m_hat / (jnp.sqrt(v_hat) + ADAM_EPS) + ADAM_WD * w)
    return delta, m, v


def _adamw(name, w, m, v, g, layer=0, prev=None):
    L, R, C = w.shape
    tr = _rows(R, C * 4, ADAM_BLOCK_BYTES)
    tc = C
    while tr * tc * 4 > ADAM_BLOCK_BYTES and tc % 256 == 0:
        tc //= 2
    blk = pl.BlockSpec((None, tr, tc), lambda i, j: (layer, i, j))
    prev = [] if prev is None else list(prev)

    def body(w_ref, m_ref, v_ref, g_ref, *rest):
        go_ref, d_ref, mo_ref, vo_ref = rest[len(prev):]
        gv = g_ref[...]
        delta, mn, vn = _adam_math(w_ref[...], gv, m_ref[...], v_ref[...])
        go_ref[...] = gv
        d_ref[...] = delta
        mo_ref[...] = mn
        vo_ref[...] = vn

    return pl.pallas_call(
        body, name=name, grid=(R // tr, C // tc),
        in_specs=[blk] * 3 + [pl.BlockSpec((tr, tc), lambda i, j: (i, j))] + [_ANY] * len(prev),
        out_specs=[blk] * 4, out_shape=[jax.ShapeDtypeStruct((L, R, C), F32)] * 4,
        input_output_aliases={4 + k: k for k in range(len(prev))},
        compiler_params=_cp("parallel", "parallel"),
    )(w, m, v, g, *prev)


def _sum_peers(name, gathered):
    n, R, C = gathered.shape

    def body(g_ref, o_ref):
        acc = g_ref[0]
        for d in range(1, n):
            acc = acc + g_ref[d]
        o_ref[...] = acc

    return pl.pallas_call(body, name=name, out_shape=jax.ShapeDtypeStruct((R, C), F32))(gathered)


def _place():
    x, y, c = lax.axis_index("x"), lax.axis_index("y"), lax.axis_index("c")
    chips = [(1 - x, y), (x, 1 - y), (1 - x, 1 - y)]
    return x, y, c, chips


def _allgather_small(name, v, after=()):
    R, C = v.shape
    after = list(after)

    def body(v_ref, *rest):
        o_ref, send_sems, recv_sems = rest[len(after):]
        x, y, c, _ = _place()
        me = 4 * x + 2 * y + c
        o_ref[me] = v_ref[...]
        copies = []
        for k in range(1, N_DEV):
            px, py, pc = x ^ (k >> 2), y ^ ((k >> 1) & 1), c ^ (k & 1)
            copies.append(pltpu.make_async_remote_copy(
                src_ref=v_ref, dst_ref=o_ref.at[me], send_sem=send_sems.at[k - 1], recv_sem=recv_sems.at[k - 1],
                device_id=(px, py, pc), device_id_type=MESH))
        for cp in copies:
            cp.start()
        for cp in copies:
            cp.wait()

    return pl.pallas_call(
        body, name=name, out_shape=jax.ShapeDtypeStruct((N_DEV, R, C), F32),
        in_specs=[pl.BlockSpec(memory_space=pltpu.VMEM)] + [_ANY] * len(after),
        out_specs=pl.BlockSpec(memory_space=pltpu.VMEM),
        scratch_shapes=[pltpu.SemaphoreType.DMA((N_DEV - 1,)), pltpu.SemaphoreType.DMA((N_DEV - 1,))],
    )(v, *after)


_HBM = pl.BlockSpec(memory_space=pltpu.HBM)
_SEM = pl.BlockSpec(memory_space=pltpu.SEMAPHORE)
_VMEM = pl.BlockSpec(memory_space=pltpu.VMEM)
_EFFECT = pltpu.SideEffectType.DATAFLOW_SIDE_EFFECTING


def _in_hbm(v):
    return pltpu.with_memory_space_constraint(v, pltpu.HBM)


def _remote(src, dst, send_sem, recv_sem, device):
    return pltpu.make_async_remote_copy(src_ref=src, dst_ref=dst, send_sem=send_sem, recv_sem=recv_sem,
                                        device_id=device, device_id_type=MESH)


def _gather_copies(src, land, ici_s, ici_r, own_s, own_r, arrivals=True):
    x, y, c, chips = _place()
    a = 2 * x + y
    sends, lands = [], []
    for i in range(len(src)):
        own = _remote(src[i], land[i].at[a], own_s.at[i], own_r.at[i], (x, y, 1 - c))
        sends.append(own)
        if arrivals:
            lands.append(own)
        for k, (px, py) in enumerate(chips):
            s, r = ici_s.at[3 * i + k], ici_r.at[3 * i + k]
            sends.append(_remote(src[i].at[c], land[i].at[a, c], s, r, (px, py, c)))
            if arrivals:
                lands.append(_remote(src[i].at[c], land[i].at[2 * px + py, c], s, r, (px, py, c)))
    return sends, lands


def _gather_start(name, shards, after=None):
    n = len(shards)
    lands = [lax.empty((N_CHIPS,) + s.shape, s.dtype) for s in shards]
    n_in = 2 * n + (0 if after is None else 1)

    def body(*refs):
        src, land = refs[:n], refs[n:2 * n]
        sems = refs[n_in:n_in + 4]
        token = refs[-1]
        sends, _ = _gather_copies(src, land, *sems, arrivals=False)
        for cp in sends:
            cp.start()
        token[...] = jnp.zeros_like(token)

    outs = pl.pallas_call(
        body, name=name,
        out_shape=(pltpu.SemaphoreType.DMA((3 * n,)), pltpu.SemaphoreType.DMA((3 * n,)),
                   pltpu.SemaphoreType.DMA((n,)), pltpu.SemaphoreType.DMA((n,)),
                   *[pltpu.HBM(s.shape, s.dtype) for s in shards], *[pltpu.HBM(l.shape, l.dtype) for l in lands],
                   jax.ShapeDtypeStruct((8, 128), F32)),
        in_specs=[_HBM] * (2 * n) + ([] if after is None else [_ANY]),
        out_specs=(_SEM,) * 4 + (_HBM,) * (2 * n) + (_VMEM,),
        input_output_aliases={i: 4 + i for i in range(2 * n)},
        compiler_params=pltpu.CompilerParams(has_side_effects=_EFFECT),
    )(*[_in_hbm(s) for s in shards], *[_in_hbm(l) for l in lands], *([] if after is None else [after]))
    return dict(sems=outs[:4], src=outs[4:4 + n], land=outs[4 + n:4 + 2 * n], token=outs[-1])


def _gather_wait(name, started, after):
    n = len(started["src"])
    after = list(after)

    def body(*refs):
        src, land = refs[:n], refs[n:2 * n]
        sems = refs[2 * n:2 * n + 4]
        sends, lands = _gather_copies(src, land, *sems)
        for cp in sends:
            cp.wait_send()
        for cp in lands:
            cp.wait_recv()

    outs = pl.pallas_call(
        body, name=name,
        out_shape=[pltpu.HBM(v.shape, v.dtype) for v in list(started["src"]) + list(started["land"])],
        in_specs=[_HBM] * (2 * n) + [_SEM] * 4 + [_ANY] * len(after), out_specs=[_HBM] * (2 * n),
        input_output_aliases={i: i for i in range(2 * n)},
        compiler_params=pltpu.CompilerParams(has_side_effects=_EFFECT),
    )(*started["src"], *started["land"], *started["sems"], *after)
    return outs[n:]


def _gather_forward(name, lands):
    n = len(lands)

    def body(*refs):
        buf = refs[n:2 * n]
        send_sems, recv_sems = refs[2 * n:]
        x, y, c, chips = _place()
        sends, lands_ = [], []
        for i in range(n):
            for k, (px, py) in enumerate(chips):
                b = 2 * px + py
                sends.append(_remote(buf[i].at[b, c], buf[i].at[b, c], send_sems.at[i, k], recv_sems.at[i, k],
                                     (x, y, 1 - c)))
                lands_.append(_remote(buf[i].at[b, 1 - c], buf[i].at[b, 1 - c], send_sems.at[i, k],
                                      recv_sems.at[i, k], (x, y, 1 - c)))
        for cp in sends:
            cp.start()
        for cp in sends:
            cp.wait_send()
        for cp in lands_:
            cp.wait_recv()

    return pl.pallas_call(
        body, name=name, in_specs=[_ANY] * n, out_specs=[_ANY] * n,
        out_shape=[jax.ShapeDtypeStruct(l.shape, l.dtype) for l in lands],
        input_output_aliases={i: i for i in range(n)},
        scratch_shapes=[pltpu.SemaphoreType.DMA((n, 3)), pltpu.SemaphoreType.DMA((n, 3))],
    )(*lands)


def _split_start(name, bufs, copies, n_sems, after=()):
    nb = len(bufs)
    after = list(after)

    def body(*refs):
        send_sems, recv_sems = refs[nb + len(after):nb + len(after) + 2]
        token = refs[-1]
        sends, _ = copies(refs[:nb], send_sems, recv_sems, False)
        for cp in sends:
            cp.start()
        token[...] = jnp.zeros_like(token)

    outs = pl.pallas_call(
        body, name=name,
        out_shape=(pltpu.SemaphoreType.DMA((n_sems,)), pltpu.SemaphoreType.DMA((n_sems,)),
                   *[pltpu.HBM(b.shape, b.dtype) for b in bufs], jax.ShapeDtypeStruct((8, 128), F32)),
        in_specs=[_HBM] * nb + [_ANY] * len(after), out_specs=(_SEM,) * 2 + (_HBM,) * nb + (_VMEM,),
        input_output_aliases={i: 2 + i for i in range(nb)},
        compiler_params=pltpu.CompilerParams(has_side_effects=_EFFECT),
    )(*[_in_hbm(b) for b in bufs], *after)
    return dict(sems=outs[:2], bufs=outs[2:2 + nb], token=outs[-1])


def _split_wait(name, started, copies, after):
    nb = len(started["bufs"])
    after = list(after)

    def body(*refs):
        sends, arrivals = copies(refs[:nb], refs[nb], refs[nb + 1], True)
        for cp in sends:
            cp.wait_send()
        for cp in arrivals:
            cp.wait_recv()

    return pl.pallas_call(
        body, name=name, out_shape=[pltpu.HBM(b.shape, b.dtype) for b in started["bufs"]],
        in_specs=[_HBM] * nb + [_SEM] * 2 + [_ANY] * len(after), out_specs=[_HBM] * nb,
        input_output_aliases={i: i for i in range(nb)},
        compiler_params=pltpu.CompilerParams(has_side_effects=_EFFECT),
    )(*started["bufs"], *started["sems"], *after)


def _forward_copies(bufs, send_sems, recv_sems, arrivals):
    x, y, c, chips = _place()
    sends, lands = [], []
    for i, buf in enumerate(bufs):
        for k, (px, py) in enumerate(chips):
            b, s, r = 2 * px + py, send_sems.at[3 * i + k], recv_sems.at[3 * i + k]
            sends.append(_remote(buf.at[b, c], buf.at[b, c], s, r, (x, y, 1 - c)))
            if arrivals:
                lands.append(_remote(buf.at[b, 1 - c], buf.at[b, 1 - c], s, r, (x, y, 1 - c)))
    return sends, lands


def _pair_copies(bufs, send_sems, recv_sems, arrivals):
    x, y, c, _ = _place()
    n = len(bufs) // 2
    copies = [_remote(bufs[i].at[:, 1 - c], bufs[n + i], send_sems.at[i], recv_sems.at[i], (x, y, 1 - c))
              for i in range(n)]
    return copies, copies


def _pair_exchange(name, parts):
    n = len(parts)

    def body(*refs):
        src = refs[:n]
        dst = refs[n:2 * n]
        send_sems, recv_sems = refs[2 * n:]
        x, y, c, _ = _place()
        copies = [pltpu.make_async_remote_copy(
            src_ref=src[i].at[:, 1 - c], dst_ref=dst[i], send_sem=send_sems.at[i], recv_sem=recv_sems.at[i],
            device_id=(x, y, 1 - c), device_id_type=MESH) for i in range(n)]
        for cp in copies:
            cp.start()
        for cp in copies:
            cp.wait()

    return pl.pallas_call(
        body, name=name, in_specs=[_ANY] * n, out_specs=[_ANY] * n,
        out_shape=[jax.ShapeDtypeStruct((p.shape[0],) + p.shape[2:], p.dtype) for p in parts],
        scratch_shapes=[pltpu.SemaphoreType.DMA((n,)), pltpu.SemaphoreType.DMA((n,))],
    )(*parts)


def _chip_copies(src, land, send_sems, recv_sems):
    x, y, c, chips = _place()
    return [_remote(src[i].at[2 * px + py], land[i].at[k], send_sems.at[3 * i + k], recv_sems.at[3 * i + k],
                    (px, py, c))
            for i in range(len(src)) for k, (px, py) in enumerate(chips)]


def _chip_start(name, parts):
    n = len(parts)
    lands = [lax.empty((3,) + p.shape[1:], p.dtype) for p in parts]

    def body(*refs):
        src, land = refs[:n], refs[n:2 * n]
        send_sems, recv_sems = refs[2 * n:2 * n + 2]
        token = refs[-1]
        for cp in _chip_copies(src, land, send_sems, recv_sems):
            cp.start()
        token[...] = jnp.zeros_like(token)

    outs = pl.pallas_call(
        body, name=name,
        out_shape=(pltpu.SemaphoreType.DMA((3 * n,)), pltpu.SemaphoreType.DMA((3 * n,)),
                   *[pltpu.HBM(p.shape, p.dtype) for p in parts], *[pltpu.HBM(l.shape, l.dtype) for l in lands],
                   jax.ShapeDtypeStruct((8, 128), F32)),
        in_specs=[_HBM] * (2 * n), out_specs=(_SEM,) * 2 + (_HBM,) * (2 * n) + (_VMEM,),
        input_output_aliases={i: 2 + i for i in range(2 * n)},
        compiler_params=pltpu.CompilerParams(has_side_effects=_EFFECT),
    )(*[_in_hbm(p) for p in parts], *[_in_hbm(l) for l in lands])
    return dict(sems=outs[:2], src=outs[2:2 + n], land=outs[2 + n:2 + 2 * n], token=outs[-1])


def _chip_wait(name, started, after):
    n = len(started["src"])
    after = list(after)

    def body(*refs):
        src, land = refs[:n], refs[n:2 * n]
        send_sems, recv_sems = refs[2 * n:2 * n + 2]
        copies = _chip_copies(src, land, send_sems, recv_sems)
        for cp in copies:
            cp.wait_send()
        for cp in copies:
            cp.wait_recv()

    outs = pl.pallas_call(
        body, name=name,
        out_shape=[pltpu.HBM(v.shape, v.dtype) for v in list(started["src"]) + list(started["land"])],
        in_specs=[_HBM] * (2 * n) + [_SEM] * 2 + [_ANY] * len(after), out_specs=[_HBM] * (2 * n),
        input_output_aliases={i: i for i in range(2 * n)},
        compiler_params=pltpu.CompilerParams(has_side_effects=_EFFECT),
    )(*started["src"], *started["land"], *started["sems"], *after)
    return outs[:n], outs[n:]


def _half_exchange(name, shards):
    n = len(shards)

    def body(*refs):
        buf = refs[n:2 * n]
        send_sems, recv_sems = refs[2 * n:]
        x, y, c, _ = _place()
        copies = [pltpu.make_async_remote_copy(
            src_ref=buf[i].at[c], dst_ref=buf[i].at[c], send_sem=send_sems.at[i], recv_sem=recv_sems.at[i],
            device_id=(x, y, 1 - c), device_id_type=MESH) for i in range(n)]
        for cp in copies:
            cp.start()
        for cp in copies:
            cp.wait()

    return pl.pallas_call(
        body, name=name, in_specs=[_ANY] * n, out_specs=[_ANY] * n,
        out_shape=[jax.ShapeDtypeStruct(s.shape, s.dtype) for s in shards],
        input_output_aliases={i: i for i in range(n)},
        scratch_shapes=[pltpu.SemaphoreType.DMA((n,)), pltpu.SemaphoreType.DMA((n,))],
    )(*shards)


def _pair_add(name, pos, part, sib):
    Q, _, R2, C = part.shape
    tr = _pick(R2, 256)

    def body(pos_ref, p_ref, s_ref, o_ref):
        o_ref[...] = (p_ref[...].astype(F32) + s_ref[...].astype(F32)).astype(BF16)

    return pl.pallas_call(
        body, name=name,
        grid_spec=pltpu.PrefetchScalarGridSpec(
            num_scalar_prefetch=1, grid=(Q, R2 // tr),
            in_specs=[pl.BlockSpec((None, None, tr, C), lambda q, i, pos_ref: (q, pos_ref[1], i, 0)),
                      pl.BlockSpec((None, tr, C), lambda q, i, pos_ref: (q, i, 0))],
            out_specs=pl.BlockSpec((None, tr, C), lambda q, i, pos_ref: (q, i, 0))),
        out_shape=jax.ShapeDtypeStruct((Q, R2, C), BF16),
        compiler_params=_cp("parallel", "parallel"),
    )(pos, part, sib)


def _chip_sum(name, pos, own, got):
    _, R2, C = own.shape
    tr = _pick(R2, 256)

    def body(pos_ref, o_ref, g_ref, out_ref):
        acc = o_ref[...].astype(F32)
        for k in range(3):
            acc = acc + g_ref[k].astype(F32)
        out_ref[...] = acc

    return pl.pallas_call(
        body, name=name,
        grid_spec=pltpu.PrefetchScalarGridSpec(
            num_scalar_prefetch=1, grid=(R2 // tr,),
            in_specs=[pl.BlockSpec((None, tr, C), lambda i, pos_ref: (pos_ref[0], i, 0)),
                      pl.BlockSpec((3, tr, C), lambda i, pos_ref: (0, i, 0))],
            out_specs=pl.BlockSpec((None, tr, C), lambda i, pos_ref: (pos_ref[1], i, 0))),
        out_shape=jax.ShapeDtypeStruct((2, R2, C), F32),
        compiler_params=_cp("parallel"),
    )(pos, own, got)


def _reduce_begin(tag, pos, parts):
    split = [p.reshape(p.shape[0], 2, p.shape[1] // 2, p.shape[2]) for p in parts]
    sib = _pair_exchange(tag + "_pair", split)
    chip = [_pair_add(f"{tag}_add{i}", pos, split[i], sib[i]) for i in range(len(parts))]
    return _chip_start(tag + "_chip_start", chip)


def _reduce_pair_start(tag, parts):
    split = [p.reshape(p.shape[0], 2, p.shape[1] // 2, p.shape[2]) for p in parts]
    lands = [lax.empty((p.shape[0],) + p.shape[2:], p.dtype) for p in split]
    return _split_start(tag + "_pair_start", split + lands, _pair_copies, len(parts))


def _reduce_pair_end(tag, pos, started, after):
    n = len(started["bufs"]) // 2
    bufs = _split_wait(tag + "_pair_wait", started, _pair_copies, after)
    chip = [_pair_add(f"{tag}_add{i}", pos, bufs[i], bufs[n + i]) for i in range(n)]
    return _chip_start(tag + "_chip_start", chip)


def _half_copies(bufs, send_sems, recv_sems, arrivals):
    x, y, c, _ = _place()
    sends = [_remote(b.at[c], b.at[c], send_sems.at[i], recv_sems.at[i], (x, y, 1 - c)) for i, b in enumerate(bufs)]
    lands = [_remote(b.at[1 - c], b.at[1 - c], send_sems.at[i], recv_sems.at[i], (x, y, 1 - c))
             for i, b in enumerate(bufs)] if arrivals else []
    return sends, lands


def _reduce_sum_start(tag, pos, started, after):
    chip, got = _chip_wait(tag + "_chip_wait", started, after)
    halves = [_chip_sum(f"{tag}_sum{i}", pos, chip[i], got[i]) for i in range(len(chip))]
    return _split_start(tag + "_half_start", halves, _half_copies, len(halves))


def _reduce_sum_end(tag, started, after, shapes):
    full = _split_wait(tag + "_half_wait", started, _half_copies, after)
    return [f.reshape(shp) for f, shp in zip(full, shapes)]


def _reduce_end(tag, pos, started, after, shapes):
    chip, got = _chip_wait(tag + "_chip_wait", started, after)
    halves = [_chip_sum(f"{tag}_sum{i}", pos, chip[i], got[i]) for i in range(len(chip))]
    full = _half_exchange(tag + "_half", halves)
    return [f.reshape(shp) for f, shp in zip(full, shapes)]


def _pad_to(v, mult):
    n = v.shape[0]
    return jnp.pad(v, (0, (-n) % mult))


def kernel(x, p, pool_w, pool_scale, ssm_in_w, ssm_conv_w, ssm_conv_b, ssm_dt_bias, ssm_a_log, ssm_d, ssm_norm_w, ssm_out_w, mlp_w1, mlp_w2, ln_g, ln_b, ple_w, ple_gate_w, loss_target, m_pool_w, m_pool_scale, m_ssm_in_w, m_ssm_conv_w, m_ssm_conv_b, m_ssm_dt_bias, m_ssm_a_log, m_ssm_d, m_ssm_norm_w, m_ssm_out_w, m_mlp_w1, m_mlp_w2, m_ln_g, m_ln_b, m_ple_w, m_ple_gate_w, v_pool_w, v_pool_scale, v_ssm_in_w, v_ssm_conv_w, v_ssm_conv_b, v_ssm_dt_bias, v_ssm_a_log, v_ssm_d, v_ssm_norm_w, v_ssm_out_w, v_mlp_w1, v_mlp_w2, v_ln_g, v_ln_b, v_ple_w, v_ple_gate_w):
    T, D = x.shape[1], x.shape[2]
    NG = len(POOL_WINDOWS)
    GD = D // NG
    DI = ssm_out_w.shape[1] * N_CHIPS
    H = ssm_dt_bias.shape[1]
    HPG = H // N_GROUPS
    GW = DI // N_GROUPS
    GS = N_GROUPS * D_STATE
    CD = DI + 2 * GS
    DF = mlp_w1.shape[2] * N_CHIPS
    PD = ple_w.shape[1]
    NIN = ssm_in_w.shape[2]

    xi, yi, ci = lax.axis_index("x"), lax.axis_index("y"), lax.axis_index("c")
    chip = 2 * xi + yi
    pos = jnp.stack([chip, ci]).astype(jnp.int32)

    x0 = x[0]
    tgt = loss_target[0]
    p0b, p1b = p[0, 0].astype(BF16), p[1, 0].astype(BF16)

    small_sh = jnp.concatenate([ssm_conv_w[0].reshape(-1), ssm_conv_b[0], ssm_norm_w[0],
                                ln_g.reshape(-1), ln_b.reshape(-1)])
    n_sh = small_sh.shape[0]
    small_all = _allgather_small("gather_small", _pad_to(small_sh, 1024).reshape(-1, 128))

    def halves(w, zero=None):
        w = w if zero is None else w + zero
        return w.astype(BF16).reshape((2, w.shape[0] // 2) + w.shape[1:])

    sh_pool = pool_w[0].astype(BF16)
    sh_pool = sh_pool.reshape((2, NG // 2) + sh_pool.shape[1:])
    started = {"ag0a": _gather_start("ag0a_start", [sh_pool, halves(mlp_w1[0])], after=small_all)}
    z0 = started["ag0a"]["token"][0, 0]
    groups = [("ag0b", [halves(mlp_w2[0], z0), halves(ple_gate_w[0], z0), halves(ple_w[0], z0)]),
              ("ag1a", [halves(ssm_in_w[0], z0)]),
              ("ag1b", [halves(ssm_out_w[0], z0), halves(mlp_w1[1], z0)]),
              ("ag1c", [halves(mlp_w2[1], z0), halves(ple_gate_w[1], z0), halves(ple_w[1], z0)])]

    def gather_end(tag, after):
        return _gather_forward(tag + "_fwd", _gather_wait(tag + "_wait", started[tag], after))

    def gather_land(tag, after):
        lands = _gather_wait(tag + "_wait", started[tag], after)
        return _split_start(tag + "_fwd_start", lands, _forward_copies, 3 * len(lands))

    def gather_done(tag, forwarding, after):
        return _split_wait(tag + "_fwd_wait", forwarding, _forward_copies, after)

    g_pool, g = gather_end("ag0a", [sh for _, shards in groups for sh in shards])
    prev = g_pool
    for tag, shards in groups:
        started[tag] = _gather_start(tag + "_start", shards, after=prev)
        prev = started[tag]["token"]
    all_started = prev

    g_w1, g_w2, g_pw, g_gw = {}, {}, {}, {}

    def set_w1(l, g):
        g_w1[l] = g.reshape(N_CHIPS, 1, D, DF // N_CHIPS)

    def set_rest(l, g2, ggw, gpw):
        g_w2[l] = g2.reshape(N_CHIPS, 1, DF // N_CHIPS, D)
        g_gw[l] = ggw.reshape(N_CHIPS, 1, D // N_CHIPS, D)
        g_pw[l] = gpw.reshape(N_CHIPS, 1, PD, D // N_CHIPS)

    set_w1(0, g)
    w_pool = jnp.transpose(g_pool.reshape(N_CHIPS, NG, GD // N_CHIPS, GD), (1, 0, 2, 3)).reshape(NG, GD, GD)

    small_all = small_all.reshape(N_DEV, -1)[0::2, :n_sh]
    cdq, niq, dq = CD // N_CHIPS, DI // N_CHIPS, D // N_CHIPS
    o = 0
    conv_w = jnp.concatenate([small_all[q, o:o + CONV_K * cdq].reshape(CONV_K, cdq) for q in range(N_CHIPS)], axis=1)
    o += CONV_K * cdq
    conv_b = small_all[:, o:o + cdq].reshape(1, CD)
    o += cdq
    norm_w = small_all[:, o:o + niq].reshape(1, DI)
    o += niq
    lng = jnp.transpose(small_all[:, o:o + 4 * dq].reshape(N_CHIPS, 2, 2, dq), (1, 2, 0, 3)).reshape(2, 2, 1, D)
    o += 4 * dq
    lnb = jnp.transpose(small_all[:, o:o + 4 * dq].reshape(N_CHIPS, 2, 2, dq), (1, 2, 0, 3)).reshape(2, 2, 1, D)

    pooled = _pool_fwd("pool_fwd", x0)
    hraw = _mm("pool_mm", "nn", pooled, w_pool, T, D, GD, tn=GD, deps=[all_started],
               a_spec=lambda tm, tn, tk: pl.BlockSpec((tm, tk), lambda i, j, k: (i, j)),
               b_spec=lambda tm, tn, tk: pl.BlockSpec((None, tk, tn), lambda i, j, k: (j, 0, 0)))
    x1, x1b, xh1, rs1 = _res_ln("ln00", x0, hraw, lng[0, 0], lnb[0, 0], scale=pool_scale)

    def mlp_fwd(l, xb, rest, deps=()):
        a, h2b = _mm(f"mlp{l}_up", "nn", xb, g_w1[l], T, DF, D, tn=min(1024, DF // N_CHIPS), deps=deps,
                     b_spec=_colshard_b(0, DF // N_CHIPS), out_dtype=(F32, BF16),
                     epi=lambda acc: (acc, jnp.square(jnp.maximum(acc, 0.0))))
        set_rest(l, *rest(a))
        h = _mm(f"mlp{l}_down", "nn", h2b, g_w2[l], T, D, DF, k_unit=DF // N_CHIPS,
                b_spec=_rowshard_b(0, DF // N_CHIPS))
        return a, h2b, h

    def ple_fwd(l, xb, pb, deps=()):
        gl = _mm(f"gate{l}_logit", "nn", xb, g_gw[l], T, D, D, k_unit=D // N_CHIPS, deps=deps,
                 b_spec=_rowshard_b(0, D // N_CHIPS))
        e = _mm(f"gate{l}_emb", "nn", pb, g_pw[l], T, D, PD, tn=min(1024, D // N_CHIPS),
                b_spec=_colshard_b(0, D // N_CHIPS))
        return gl, e

    a0, h2b0, h0 = mlp_fwd(0, x1b, lambda a: gather_end("ag0b", [a]))
    x2, x2b, xh2, rs2 = _res_ln("ln01", x1, h0, lng[0, 1], lnb[0, 1])
    gl0, e0 = ple_fwd(0, x2b, p0b)
    x3, x3b, gate0 = _gate_fwd("gate0", x2, gl0, e0)

    g_in, = gather_end("ag1a", [x3b])

    def in_proj_cols(lo, hi):
        parts = [g_in[q].reshape(D, NIN)[:, max(lo - q * NIN, 0):min(hi - q * NIN, NIN)]
                 for q in range(N_CHIPS) if lo < (q + 1) * NIN and hi > q * NIN]
        return parts[0] if len(parts) == 1 else jnp.concatenate(parts, axis=1)

    w_z, w_xbc, w_dt = in_proj_cols(0, DI), in_proj_cols(DI, DI + CD), in_proj_cols(DI + CD, N_CHIPS * NIN)
    z = _mm("ssm_in_z", "nn", x3b, w_z, T, DI, D)
    xbc_pre = _mm("ssm_in_xbc", "nn", x3b, w_xbc, T, CD, D)
    dt_pre = _mm("ssm_in_dt", "nn", x3b, w_dt, T, H, D)
    xbc = _conv_fwd("conv_fwd", xbc_pre, conv_w, conv_b)
    dt = _dt_fwd("dt_fwd", dt_pre, ssm_dt_bias)
    dt_g = jnp.transpose(dt.reshape(T, N_GROUPS, HPG), (1, 0, 2))
    dtc = jnp.pad(dt_g, ((0, 0), (0, 0), (0, HEAD_PAD - HPG)))
    dtr = jnp.pad(jnp.transpose(dt_g, (0, 2, 1)), ((0, 0), (0, ROW_PAD - HPG), (0, 0)))
    al_g = ssm_a_log.reshape(N_GROUPS, HPG)
    alr = jnp.pad(al_g, ((0, 0), (0, HEAD_PAD - HPG)))[:, None, :]
    alc = jnp.pad(al_g, ((0, 0), (0, ROW_PAD - HPG)))[:, :, None]
    d_e = jnp.repeat(ssm_d.reshape(N_GROUPS, HPG), HEAD_DIM, axis=1)[:, None, :]
    fw1b = gather_land("ag1b", [xbc])
    ysc, hprev = _ssd_fwd("ssd_fwd", xbc, dtc, dtr, alr, alc, d_e, DI, deps=[fw1b["token"]])
    g_out, g = gather_done("ag1b", fw1b, [ysc])
    set_w1(1, g)
    w_out = g_out.reshape(DI, D)
    ynb, rsn = _gnorm_fwd("gnorm_fwd", ysc, z, norm_w)
    h1 = _mm("ssm_out", "nn", ynb, w_out, T, D, DI)
    fw1c = gather_land("ag1c", [h1])
    x4, x4b, xh4, rs4 = _res_ln("ln10", x3, h1, lng[1, 0], lnb[1, 0])
    a1, h2b1, h2 = mlp_fwd(1, x4b, lambda a: gather_done("ag1c", fw1c, [a]), deps=[fw1c["token"]])
    x5, x5b, xh5, rs5 = _res_ln("ln11", x4, h2, lng[1, 1], lnb[1, 1])
    gl1, e1 = ple_fwd(1, x5b, p1b)
    dx6, gate1, loss_parts = _gate_loss("gate1_loss", x5, gl1, e1, tgt)
    loss_local = jnp.sum(loss_parts[0::8, 0])

    def ple_bwd(l, dxo, gate, e, xb, pb, deps=()):
        dgl, de = _gate_bwd(f"gate{l}_bwd", dxo, gate, e)
        d_gw = _mm(f"gate{l}_dw", "tn", xb, dgl, D, D, T, out_dtype=BF16, deps=deps).reshape(N_CHIPS, D // N_CHIPS, D)
        d_pw = _mm(f"gate{l}_dpw", "tn", pb, de, PD, D, T, out_dtype=BF16, tn=min(1024, D // N_CHIPS),
                   o_shape=(N_CHIPS, PD, D // N_CHIPS), o_spec=_colshard_o(D // N_CHIPS))
        dx = _mm(f"gate{l}_dx", "nt", dgl, g_gw[l], T, D, D, tn=min(1024, D // N_CHIPS), deps=deps,
                 b_spec=_rowshard_bt(0, D // N_CHIPS), epi=lambda acc, r: acc + r, extras=[dxo])
        return dx, d_gw, d_pw

    def mlp_bwd(l, du, dub, a, h2b, xb):
        d_w2 = _mm(f"mlp{l}_dw2", "tn", h2b, dub, DF, D, T, out_dtype=BF16).reshape(N_CHIPS, DF // N_CHIPS, D)
        da = _mm(f"mlp{l}_da", "nt", dub, g_w2[l], T, DF, D, tn=min(1024, DF // N_CHIPS),
                 b_spec=_rowshard_bt(0, DF // N_CHIPS), out_dtype=BF16,
                 epi=lambda acc, av: acc * (2.0 * jnp.maximum(av, 0.0)), extras=[a])
        d_w1 = _mm(f"mlp{l}_dw1", "tn", xb, da, D, DF, T, out_dtype=BF16, tn=min(1024, DF // N_CHIPS),
                   o_shape=(N_CHIPS, D, DF // N_CHIPS), o_spec=_colshard_o(DF // N_CHIPS))
        dx = _mm(f"mlp{l}_dx", "nt", da, g_w1[l], T, D, DF, k_unit=DF // N_CHIPS,
                 b_spec=_colshard_bt(0, DF // N_CHIPS), epi=lambda acc, r: acc + ALPHA * r, extras=[du])
        return dx, d_w1, d_w2

    dx5, d_gw1, d_pw1 = ple_bwd(1, dx6, gate1, e1, x5b, p1b)
    du5, du5b, dg11, db11 = _ln_bwd("ln11_bwd", dx5, xh5, rs5, lng[1, 1])
    dx4, d_w1_1, d_w2_1 = mlp_bwd(1, du5, du5b, a1, h2b1, x4b)
    parts1a = [d_w1_1, d_w2_1, d_gw1, d_pw1]
    pair1a = _reduce_pair_start("rs1a", parts1a)
    du4, du4b, dg10, db10 = _ln_bwd("ln10_bwd", dx4, xh4, rs4, lng[1, 0])
    d_wout = _mm("ssm_out_dw", "tn", ynb, du4b, DI, D, T, out_dtype=BF16,
                 deps=[pair1a["token"]]).reshape(N_CHIPS, DI // N_CHIPS, D)
    red1a = _reduce_pair_end("rs1a", pos, pair1a, [d_wout])
    dyn = _mm("ssm_out_dx", "nt", du4b, w_out, T, DI, D, deps=[red1a["token"]])
    dysc, dzb, dnorm_w = _gnorm_bwd("gnorm_bwd", dyn, ysc, z, norm_w, rsn)
    dxs, dbm, dcm, ddt_g, dalog_g, dd_g = _ssd_bwd("ssd_bwd", dysc, xbc, dtc, dtr, alr, alc, d_e, hprev, DI)
    dconv, dconv_w, dconv_b = _conv_bwd_a("conv_bwd_a", dxs, dbm, dcm, xbc_pre, conv_w, conv_b)
    dxbcb = _conv_bwd_b("conv_bwd_b", dconv, conv_w)
    ddt = jnp.transpose(ddt_g[:, :, :HPG], (1, 0, 2)).reshape(T, H)
    _, ddtpb, ddt_bias = _dt_bwd("dt_bwd", ddt, dt_pre, ssm_dt_bias)
    da_log = dalog_g[:, 0, :HPG].reshape(1, H)
    dd_skip = dd_g[:, 0, :HPG].reshape(1, H)
    d_wz = _mm("ssm_in_dwz", "tn", x3b, dzb, D, DI, T, out_dtype=BF16)
    d_wx = _mm("ssm_in_dwx", "tn", x3b, dxbcb, D, CD, T, out_dtype=BF16)
    d_wdt = _mm("ssm_in_dwdt", "tn", x3b, ddtpb, D, H, T, out_dtype=BF16)

    def in_proj_shard(q):
        parts = []
        for piece, start in ((d_wz, 0), (d_wx, DI), (d_wdt, DI + CD)):
            lo, hi = max(q * NIN - start, 0), min((q + 1) * NIN - start, piece.shape[1])
            if lo < hi:
                parts.append(piece[:, lo:hi])
        return parts[0] if len(parts) == 1 else jnp.concatenate(parts, axis=1)

    d_win = jnp.stack([in_proj_shard(q) for q in range(N_CHIPS)])
    parts1b = [d_win, d_wout]
    pair1b = _reduce_pair_start("rs1b", parts1b)
    dx3 = _mm("ssm_in_dxz", "nt", dzb, w_z, T, D, DI, epi=lambda acc, r: acc + ALPHA * r, extras=[du4],
              deps=[pair1b["token"]])
    dx3 = _mm("ssm_in_dxx", "nt", dxbcb, w_xbc, T, D, CD, epi=lambda acc, r: acc + r, extras=[dx3])
    dx3 = _mm("ssm_in_dxdt", "nt", ddtpb, w_dt, T, D, H, epi=lambda acc, r: acc + r, extras=[dx3])

    red1b = _reduce_pair_end("rs1b", pos, pair1b, [dx3])

    dx2, d_gw0, d_pw0 = ple_bwd(0, dx3, gate0, e0, x2b, p0b, deps=[red1b["token"]])
    du2, du2b, dg01, db01 = _ln_bwd("ln01_bwd", dx2, xh2, rs2, lng[0, 1])
    dx1, d_w1_0, d_w2_0 = mlp_bwd(0, du2, du2b, a0, h2b0, x1b)
    du1, dhrb, dg00, db00, dscale = _ln_bwd("ln00_bwd", dx1, xh1, rs1, lng[0, 0], hraw=hraw, scale=pool_scale)
    d_wpool = _mm("pool_dw", "tn", pooled, dhrb, D, GD, T, tm=GD, tn=GD, out_dtype=BF16,
                  b_spec=lambda tm, tn, tk: pl.BlockSpec((tk, tn), lambda i, j, k: (k, i)),
                  o_shape=(NG, GD, GD), o_spec=lambda tm, tn, tk: pl.BlockSpec((None, tm, tn), lambda i, j, k: (i, 0, 0)))
    dpooled = _mm("pool_dx", "nt", dhrb, w_pool, T, D, GD, tn=GD,
                  a_spec=lambda tm, tn, tk: pl.BlockSpec((tm, tk), lambda i, j, k: (i, j)),
                  b_spec=lambda tm, tn, tk: pl.BlockSpec((None, tn, tk), lambda i, j, k: (j, 0, 0)))
    grad_x = _pool_bwd("pool_bwd", dpooled, du1)
    d_wpool = jnp.transpose(d_wpool.reshape(NG, N_CHIPS, GD // N_CHIPS, GD), (1, 0, 2, 3)).reshape(N_CHIPS, NG * GD // N_CHIPS, GD)

    parts0 = [d_wpool, d_w1_0, d_w2_0, d_gw0, d_pw0]
    red0 = _reduce_begin("rs0", pos, parts0)

    def adam(name, w, m, v, g, layer=0, prev=None):
        w3, m3, v3 = (t.reshape((t.shape[0], -1, t.shape[-1])) for t in (w, m, v))
        return _adamw(name, w3, m3, v3, g, layer, prev)

    sum1a = _reduce_sum_start("rs1a", pos, red1a, [red0["token"]])
    sum1b = _reduce_sum_start("rs1b", pos, red1b, [sum1a["token"]])
    r_w1_1, r_w2_1, r_gw1, r_pw1 = _reduce_sum_end("rs1a", sum1a, [sum1b["token"]], [p.shape[1:] for p in parts1a])
    stacked = {"mlp_w1": (mlp_w1, m_mlp_w1, v_mlp_w1), "mlp_w2": (mlp_w2, m_mlp_w2, v_mlp_w2),
               "ple_w": (ple_w, m_ple_w, v_ple_w), "ple_gate_w": (ple_gate_w, m_ple_gate_w, v_ple_gate_w)}
    grads1 = {"mlp_w1": r_w1_1, "mlp_w2": r_w2_1, "ple_w": r_pw1, "ple_gate_w": r_gw1}
    upper = {nm: adam(f"adam_{nm}_l1", *stacked[nm], grads1[nm], layer=1) for nm in stacked}
    r_in, r_out = _reduce_sum_end("rs1b", sum1b, [t[1] for t in upper.values()], [p.shape[1:] for p in parts1b])
    big = {"ssm_in_w": [jnp.swapaxes(t, 1, 2) for t in adam(
               "adam_ssm_in_w", *(jnp.swapaxes(t, 1, 2) for t in (ssm_in_w, m_ssm_in_w, v_ssm_in_w)), r_in.T)],
           "ssm_out_w": adam("adam_ssm_out_w", ssm_out_w, m_ssm_out_w, v_ssm_out_w, r_out)}

    dln_g = jnp.stack([jnp.stack([dg00, dg01]), jnp.stack([dg10, dg11])]).reshape(-1)
    dln_b = jnp.stack([jnp.stack([db00, db01]), jnp.stack([db10, db11])]).reshape(-1)
    small_g = jnp.concatenate([dscale.reshape(-1), dconv_w.reshape(-1), dconv_b.reshape(-1), ddt_bias.reshape(-1),
                               da_log.reshape(-1), dd_skip.reshape(-1), dnorm_w.reshape(-1), dln_g, dln_b,
                               loss_local.reshape(1)])
    n_sg = small_g.shape[0]
    sg_all = _allgather_small("gather_small_grads", _pad_to(small_g, 1024).reshape(-1, 128),
                              after=[d_w1_0, d_w2_0])
    sg = _sum_peers("sum_small_grads", sg_all).reshape(-1)[:n_sg]
    o = 0

    def take(nel):
        nonlocal o
        v = sg[o:o + nel]
        o += nel
        return v

    g_scale = take(D).reshape(1, D)
    g_conv_w_full = take(CONV_K * CD).reshape(CONV_K, CD)
    g_conv_b_full = take(CD).reshape(1, CD)
    g_dt_bias = take(H).reshape(1, H)
    g_a_log = take(H).reshape(1, H)
    g_d = take(H).reshape(1, H)
    g_norm_full = take(DI).reshape(1, DI)
    g_lng_full = take(4 * D).reshape(2, 2, D)
    g_lnb_full = take(4 * D).reshape(2, 2, D)
    loss = take(1).reshape(())
    g_conv_w = lax.dynamic_slice_in_dim(g_conv_w_full, chip * cdq, cdq, axis=1)[None]
    g_conv_b = lax.dynamic_slice_in_dim(g_conv_b_full, chip * cdq, cdq, axis=1)
    g_norm = lax.dynamic_slice_in_dim(g_norm_full, chip * niq, niq, axis=1)
    g_lng = lax.dynamic_slice_in_dim(g_lng_full, chip * dq, dq, axis=2)
    g_lnb = lax.dynamic_slice_in_dim(g_lnb_full, chip * dq, dq, axis=2)

    small = [("pool_scale", pool_scale, m_pool_scale, v_pool_scale, g_scale),
             ("ssm_conv_w", ssm_conv_w, m_ssm_conv_w, v_ssm_conv_w, g_conv_w),
             ("ssm_conv_b", ssm_conv_b, m_ssm_conv_b, v_ssm_conv_b, g_conv_b),
             ("ssm_dt_bias", ssm_dt_bias, m_ssm_dt_bias, v_ssm_dt_bias, g_dt_bias),
             ("ssm_a_log", ssm_a_log, m_ssm_a_log, v_ssm_a_log, g_a_log),
             ("ssm_d", ssm_d, m_ssm_d, v_ssm_d, g_d),
             ("ssm_norm_w", ssm_norm_w, m_ssm_norm_w, v_ssm_norm_w, g_norm),
             ("ln_g", ln_g, m_ln_g, v_ln_g, g_lng),
             ("ln_b", ln_b, m_ln_b, v_ln_b, g_lnb)]

    def pack(idx):
        flat = _pad_to(jnp.concatenate([s[idx].reshape(-1) for s in small]), 1024)
        return flat.reshape(1, -1, 128)

    sm_out = _adamw("adam_small", pack(1), pack(2), pack(3), pack(4)[0])
    small_res = {}
    o = 0
    for s in small:
        nel = s[1].size
        small_res[s[0]] = [t.reshape(-1)[o:o + nel].reshape(s[1].shape) for t in sm_out]
        o += nel

    r_pool, r_w1_0, r_w2_0, r_gw0, r_pw0 = _reduce_end(
        "rs0", pos, red0, [big["ssm_in_w"][1], big["ssm_out_w"][1], sm_out[1]] + [upper[nm][1] for nm in upper],
        [p.shape[1:] for p in parts0])
    grads0 = {"mlp_w1": r_w1_0, "mlp_w2": r_w2_0, "ple_w": r_pw0, "ple_gate_w": r_gw0}
    big["pool_w"] = adam("adam_pool_w", pool_w, m_pool_w, v_pool_w, r_pool)
    for nm in stacked:
        big[nm] = adam(f"adam_{nm}_l0", *stacked[nm], grads0[nm], layer=0, prev=upper[nm])
    shapes = {"pool_w": pool_w.shape, "ssm_in_w": ssm_in_w.shape, "ssm_out_w": ssm_out_w.shape,
              **{nm: stacked[nm][0].shape for nm in stacked}}
    big = {nm: [t.reshape(shapes[nm]) for t in big[nm]] for nm in big}

    order = ["pool_w", "pool_scale", "ssm_in_w", "ssm_conv_w", "ssm_conv_b", "ssm_dt_bias", "ssm_a_log", "ssm_d",
             "ssm_norm_w", "ssm_out_w", "mlp_w1", "mlp_w2", "ln_g", "ln_b", "ple_w", "ple_gate_w"]
    res = {**big, **small_res}
    outs = [loss, grad_x[None]]
    for kind in range(4):
        outs += [res[nm][kind] for nm in order]
    return tuple(outs)
```

```python
import jax
import jax.numpy as jnp
from jax import lax
from jax.experimental import pallas as pl
from jax.experimental.pallas import tpu as pltpu

F32 = jnp.float32
BF16 = jnp.bfloat16
MESH = pl.DeviceIdType.MESH

DEPTH = 2
ALPHA = (2.0 * DEPTH) ** 0.25
LN_EPS = 1e-5
RMS_EPS = 1e-5
POOL_WINDOWS = (2, 4, 8, 16)
POOL_HALO = 16
HEAD_DIM = 64
HEAD_PAIR = 2 * HEAD_DIM
N_GROUPS = 8
D_STATE = 128
CHUNK = 128
CONV_K = 4
CONV_HALO = 8
HEAD_PAD = 128
ROW_PAD = 8
N_CHIPS = 4
N_DEV = 8
ADAM_LR = 0.001
ADAM_B1 = 0.9
ADAM_B2 = 0.999
ADAM_EPS = 1e-08
ADAM_WD = 0.01
ADAM_STEP = 10
VMEM_LIMIT = 56 * 1024 * 1024
ADAM_BLOCK_BYTES = 1024 * 1024
MM_VMEM_BUDGET = 40 * 1024 * 1024


_ANY = pl.BlockSpec(memory_space=pl.ANY)


def _cp(*sem):
    return pltpu.CompilerParams(dimension_semantics=sem, vmem_limit_bytes=VMEM_LIMIT)


def _pick(dim, pref):
    t = pref
    while t >= 128:
        if dim % t == 0:
            return t
        t //= 2
    return dim


def _rows(rows, row_bytes, budget):
    t = rows
    while t * row_bytes > budget and t % 16 == 0:
        t //= 2
    return t


def _sigmoid(v):
    return 1.0 / (1.0 + jnp.exp(-v))


_DIMS = {"nn": (((1,), (0,)), ((), ())), "nt": (((1,), (1,)), ((), ())), "tn": (((0,), (0,)), ((), ()))}


def _pick_k(k_unit, fixed_bytes, per_k_bytes):
    for n in range(1, k_unit // 128 + 1):
        if k_unit % n == 0 and (n == 1 or (k_unit // n) % 128 == 0):
            if fixed_bytes + (k_unit // n) * per_k_bytes <= MM_VMEM_BUDGET:
                return k_unit // n
    return min(k_unit, 128)


def _mm(name, form, a, b, M, N, K, *, tm=1024, tn=1024, k_unit=None, a_spec=None, b_spec=None,
        o_shape=None, o_spec=None, out_dtype=F32, pro=None, epi=None, extras=(), deps=()):
    tm, tn = _pick(M, tm), _pick(N, tn)
    out_dtypes = out_dtype if isinstance(out_dtype, tuple) else (out_dtype,)
    n_out = len(out_dtypes)
    in_place = n_out == 1 and out_dtypes[0] == F32
    fixed = 2 * tm * tn * (sum(jnp.dtype(d).itemsize for d in out_dtypes) + 4 * len(extras))
    fixed += 0 if in_place else 4 * tm * tn
    tk = _pick_k(K if k_unit is None else k_unit, fixed,
                 2 * (tm * a.dtype.itemsize + tn * b.dtype.itemsize))
    nk = K // tk
    if a_spec is None:
        a_spec = (pl.BlockSpec((tk, tm), lambda i, j, k: (k, i)) if form == "tn"
                  else pl.BlockSpec((tm, tk), lambda i, j, k: (i, k)))
    else:
        a_spec = a_spec(tm, tn, tk)
    if b_spec is None:
        b_spec = (pl.BlockSpec((tn, tk), lambda i, j, k: (j, k)) if form == "nt"
                  else pl.BlockSpec((tk, tn), lambda i, j, k: (k, j)))
    else:
        b_spec = b_spec(tm, tn, tk)
    if o_spec is None:
        o_spec = pl.BlockSpec((tm, tn), lambda i, j, k: (i, j))
        o_shape = (M, N)
    else:
        o_spec = o_spec(tm, tn, tk)
    ex_arrays = [e for e in extras]
    ex_specs = [pl.BlockSpec((tm, tn), lambda i, j, k: (i, j)) for _ in extras]
    ne = len(ex_arrays)
    nd = len(deps)
    dep_specs = [pl.BlockSpec((8, 128), lambda i, j, k: (0, 0)) for _ in deps]
    dims = _DIMS[form]
    use_scratch = nk > 1 and not in_place

    def body(a_ref, b_ref, *rest):
        ex_refs = rest[:ne]
        o_refs = rest[ne + nd:ne + nd + n_out]
        at = a_ref[...]
        if pro is not None:
            at = pro(at)
        p = lax.dot_general(at.astype(BF16), b_ref[...].astype(BF16), dims, preferred_element_type=F32)

        def finish(acc):
            res = acc if epi is None else epi(acc, *[r[...] for r in ex_refs])
            res = res if isinstance(res, tuple) else (res,)
            for o_ref, r, d in zip(o_refs, res, out_dtypes):
                o_ref[...] = r.astype(d)

        if nk == 1:
            finish(p)
        else:
            acc_ref = rest[ne + nd + n_out] if use_scratch else o_refs[0]
            k = pl.program_id(2)

            @pl.when(k == 0)
            def _():
                acc_ref[...] = p

            @pl.when(jnp.logical_and(k > 0, k < nk - 1))
            def _():
                acc_ref[...] += p

            @pl.when(k == nk - 1)
            def _():
                finish(acc_ref[...] + p)

    res = pl.pallas_call(
        body, name=name, grid=(M // tm, N // tn, nk),
        in_specs=[a_spec, b_spec] + ex_specs + dep_specs, out_specs=[o_spec] * n_out,
        out_shape=[jax.ShapeDtypeStruct(o_shape, d) for d in out_dtypes],
        scratch_shapes=[pltpu.VMEM((tm, tn), F32)] if use_scratch else [],
        compiler_params=_cp("parallel", "parallel", "arbitrary"),
    )(a, b, *ex_arrays, *deps)
    return res if n_out > 1 else res[0]


def _colshard_b(l, n_per):
    def make(tm, tn, tk):
        nb = n_per // tn
        return pl.BlockSpec((None, None, tk, tn), lambda i, j, k: (j // nb, l, k, j % nb))
    return make


def _colshard_bt(l, n_per):
    def make(tm, tn, tk):
        nb = n_per // tk
        return pl.BlockSpec((None, None, tn, tk), lambda i, j, k: (k // nb, l, j, k % nb))
    return make


def _colshard_o(n_per):
    def make(tm, tn, tk):
        nb = n_per // tn
        return pl.BlockSpec((None, tm, tn), lambda i, j, k: (j // nb, i, j % nb))
    return make


def _rowshard_b(l, k_per):
    def make(tm, tn, tk):
        nb = k_per // tk
        return pl.BlockSpec((None, None, tk, tn), lambda i, j, k: (k // nb, l, k % nb, j))
    return make


def _rowshard_bt(l, k_per):
    def make(tm, tn, tk):
        nb = k_per // tn
        return pl.BlockSpec((None, None, tn, tk), lambda i, j, k: (j // nb, l, j % nb, k))
    return make


def _res_ln(name, xprev, h, g, b, scale=None):
    T, D = xprev.shape
    tr = _pick(T, 256)
    row = pl.BlockSpec((tr, D), lambda i: (i, 0))
    vec = pl.BlockSpec((1, D), lambda i: (0, 0))
    has_scale = scale is not None

    def body(*refs):
        if has_scale:
            x_ref, h_ref, s_ref, g_ref, b_ref, y_ref, yb_ref, xh_ref, rs_ref = refs
            hh = h_ref[...] * s_ref[...]
        else:
            x_ref, h_ref, g_ref, b_ref, y_ref, yb_ref, xh_ref, rs_ref = refs
            hh = h_ref[...]
        u = ALPHA * x_ref[...] + hh
        mu = jnp.mean(u, axis=-1, keepdims=True)
        d = u - mu
        var = jnp.mean(d * d, axis=-1, keepdims=True)
        rs = lax.rsqrt(var + LN_EPS)
        xh = d * rs
        y = xh * g_ref[...] + b_ref[...]
        y_ref[...] = y
        yb_ref[...] = y.astype(BF16)
        xh_ref[...] = xh
        rs_ref[...] = rs

    ins = [xprev, h] + ([scale] if has_scale else []) + [g, b]
    specs = [row, row] + ([vec] if has_scale else []) + [vec, vec]
    return pl.pallas_call(
        body, name=name, grid=(T // tr,), in_specs=specs,
        out_specs=[row, row, row, pl.BlockSpec((tr, 1), lambda i: (i, 0))],
        out_shape=[jax.ShapeDtypeStruct((T, D), F32), jax.ShapeDtypeStruct((T, D), BF16),
                   jax.ShapeDtypeStruct((T, D), F32), jax.ShapeDtypeStruct((T, 1), F32)],
        compiler_params=_cp("parallel"),
    )(*ins)


def _accum(ref, part, first):
    @pl.when(first)
    def _():
        ref[...] = part

    @pl.when(jnp.logical_not(first))
    def _():
        ref[...] += part


def _ln_bwd(name, dy, xh, rs, g, hraw=None, scale=None):
    T, D = dy.shape
    tr = _pick(T, 256)
    row = pl.BlockSpec((tr, D), lambda i: (i, 0))
    vec = pl.BlockSpec((1, D), lambda i: (0, 0))
    has_scale = scale is not None

    def body(*refs):
        if has_scale:
            dy_ref, xh_ref, rs_ref, g_ref, hr_ref, s_ref, du_ref, dub_ref, dg_ref, db_ref, ds_ref = refs
        else:
            dy_ref, xh_ref, rs_ref, g_ref, du_ref, dub_ref, dg_ref, db_ref = refs
        first = pl.program_id(0) == 0
        dyv = dy_ref[...]
        xhv = xh_ref[...]
        dxh = dyv * g_ref[...]
        m1 = jnp.mean(dxh, axis=-1, keepdims=True)
        m2 = jnp.mean(dxh * xhv, axis=-1, keepdims=True)
        du = rs_ref[...] * (dxh - m1 - xhv * m2)
        du_ref[...] = du
        if has_scale:
            dub_ref[...] = (du * s_ref[...]).astype(BF16)
            _accum(ds_ref, jnp.sum(du * hr_ref[...], axis=0, keepdims=True), first)
        else:
            dub_ref[...] = du.astype(BF16)
        _accum(dg_ref, jnp.sum(dyv * xhv, axis=0, keepdims=True), first)
        _accum(db_ref, jnp.sum(dyv, axis=0, keepdims=True), first)

    ins = [dy, xh, rs, g] + ([hraw, scale] if has_scale else [])
    specs = [row, row, pl.BlockSpec((tr, 1), lambda i: (i, 0)), vec] + ([row, vec] if has_scale else [])
    n_vec = 3 if has_scale else 2
    return pl.pallas_call(
        body, name=name, grid=(T // tr,), in_specs=specs,
        out_specs=[row, row] + [vec] * n_vec,
        out_shape=[jax.ShapeDtypeStruct((T, D), F32), jax.ShapeDtypeStruct((T, D), BF16)]
        + [jax.ShapeDtypeStruct((1, D), F32)] * n_vec,
        compiler_params=_cp("arbitrary"),
    )(*ins)


def _gate_fwd(name, x, gl, e):
    T, D = x.shape
    tr = _pick(T, 256)
    row = pl.BlockSpec((tr, D), lambda i: (i, 0))

    def body(x_ref, gl_ref, e_ref, xo_ref, xob_ref, gate_ref):
        gate = _sigmoid(gl_ref[...])
        xo = x_ref[...] + gate * e_ref[...]
        xo_ref[...] = xo
        xob_ref[...] = xo.astype(BF16)
        gate_ref[...] = gate

    return pl.pallas_call(
        body, name=name, grid=(T // tr,), in_specs=[row, row, row], out_specs=[row, row, row],
        out_shape=[jax.ShapeDtypeStruct((T, D), F32), jax.ShapeDtypeStruct((T, D), BF16),
                   jax.ShapeDtypeStruct((T, D), F32)],
        compiler_params=_cp("parallel"),
    )(x, gl, e)


def _gate_loss(name, x, gl, e, tgt):
    T, D = x.shape
    tr = _pick(T, 256)
    row = pl.BlockSpec((tr, D), lambda i: (i, 0))

    def body(x_ref, gl_ref, e_ref, t_ref, dy_ref, gate_ref, lp_ref):
        gate = _sigmoid(gl_ref[...])
        err = x_ref[...] + gate * e_ref[...] - t_ref[...]
        dy_ref[...] = err * (1.0 / D)
        gate_ref[...] = gate
        s = jnp.sum(jnp.mean(err * err, axis=-1, keepdims=True), axis=0, keepdims=True)
        lp_ref[...] = jnp.broadcast_to(0.5 * s, (8, 128))

    return pl.pallas_call(
        body, name=name, grid=(T // tr,), in_specs=[row] * 4,
        out_specs=[row, row, pl.BlockSpec((8, 128), lambda i: (i, 0))],
        out_shape=[jax.ShapeDtypeStruct((T, D), F32), jax.ShapeDtypeStruct((T, D), F32),
                   jax.ShapeDtypeStruct((T // tr * 8, 128), F32)],
        compiler_params=_cp("parallel"),
    )(x, gl, e, tgt)


def _gate_bwd(name, dxo, gate, e):
    T, D = dxo.shape
    tr = _pick(T, 256)
    row = pl.BlockSpec((tr, D), lambda i: (i, 0))

    def body(d_ref, gate_ref, e_ref, dgl_ref, de_ref):
        d = d_ref[...]
        gate = gate_ref[...]
        dgl_ref[...] = (d * e_ref[...] * gate * (1.0 - gate)).astype(BF16)
        de_ref[...] = (d * gate).astype(BF16)

    return pl.pallas_call(
        body, name=name, grid=(T // tr,), in_specs=[row] * 3, out_specs=[row, row],
        out_shape=[jax.ShapeDtypeStruct((T, D), BF16)] * 2,
        compiler_params=_cp("parallel"),
    )(dxo, gate, e)


def _gnorm_fwd(name, y, z, w):
    T, DI = y.shape
    tr = _pick(T, 128)
    row = pl.BlockSpec((tr, DI), lambda i: (i, 0))
    vec = pl.BlockSpec((1, DI), lambda i: (0, 0))
    col = pl.BlockSpec((tr, 1), lambda i: (i, 0))

    def body(y_ref, z_ref, w_ref, o_ref, rs_ref):
        zv = z_ref[...]
        yz = y_ref[...] * (zv * _sigmoid(zv))
        rs = lax.rsqrt(jnp.mean(yz * yz, axis=-1, keepdims=True) + RMS_EPS)
        o_ref[...] = (yz * rs * w_ref[...]).astype(BF16)
        rs_ref[...] = rs

    return pl.pallas_call(
        body, name=name, grid=(T // tr,), in_specs=[row, row, vec], out_specs=[row, col],
        out_shape=[jax.ShapeDtypeStruct((T, DI), BF16), jax.ShapeDtypeStruct((T, 1), F32)],
        compiler_params=_cp("parallel"),
    )(y, z, w)


def _gnorm_bwd(name, dyn, y, z, w, rs):
    T, DI = y.shape
    tr = _pick(T, 128)
    row = pl.BlockSpec((tr, DI), lambda i: (i, 0))
    vec = pl.BlockSpec((1, DI), lambda i: (0, 0))
    col = pl.BlockSpec((tr, 1), lambda i: (i, 0))

    def body(d_ref, y_ref, z_ref, w_ref, rs_ref, dy_ref, dz_ref, dw_ref):
        first = pl.program_id(0) == 0
        zv = z_ref[...]
        yv = y_ref[...]
        sg = _sigmoid(zv)
        sz = zv * sg
        rsv = rs_ref[...]
        yzh = yv * sz * rsv
        dv = d_ref[...]
        gw = dv * w_ref[...]
        m = jnp.mean(gw * yzh, axis=-1, keepdims=True)
        dyz = rsv * (gw - yzh * m)
        dy_ref[...] = dyz * sz
        dz_ref[...] = (dyz * yv * (sg * (1.0 + zv * (1.0 - sg)))).astype(BF16)
        _accum(dw_ref, jnp.sum(dv * yzh, axis=0, keepdims=True), first)

    return pl.pallas_call(
        body, name=name, grid=(T // tr,), in_specs=[row, row, row, vec, col], out_specs=[row, row, vec],
        out_shape=[jax.ShapeDtypeStruct((T, DI), F32), jax.ShapeDtypeStruct((T, DI), BF16),
                   jax.ShapeDtypeStruct((1, DI), F32)],
        compiler_params=_cp("arbitrary"),
    )(dyn, y, z, w, rs)


def _sel4(j, vals):
    return jnp.where(j == 0, vals[0], jnp.where(j == 1, vals[1], jnp.where(j == 2, vals[2], vals[3])))


def _pool_cnt(i, j, tr, rows, offset):
    t = i * tr + offset + lax.broadcasted_iota(jnp.int32, (rows, 1), 0)
    win = _sel4(j, POOL_WINDOWS)
    return jnp.minimum(t + 1, win).astype(F32)


def _pool_fwd(name, x):
    T, D = x.shape
    gd = D // len(POOL_WINDOWS)
    tr = _pick(T, 512)
    hb = tr // POOL_HALO

    def body(x_ref, h_ref, o_ref):
        i, j = pl.program_id(0), pl.program_id(1)
        xv = x_ref[...]
        halo = jnp.where(i > 0, h_ref[...], 0.0)
        cat = jnp.concatenate([halo, xv], axis=0)
        s2 = cat + pltpu.roll(cat, 1, 0)
        s4 = s2 + pltpu.roll(s2, 2, 0)
        s8 = s4 + pltpu.roll(s4, 4, 0)
        s16 = s8 + pltpu.roll(s8, 8, 0)
        sel = _sel4(j, (s2, s4, s8, s16))[POOL_HALO:]
        o_ref[...] = (sel / _pool_cnt(i, j, tr, tr, 0) - xv).astype(BF16)

    return pl.pallas_call(
        body, name=name, grid=(T // tr, len(POOL_WINDOWS)),
        in_specs=[pl.BlockSpec((tr, gd), lambda i, j: (i, j)),
                  pl.BlockSpec((POOL_HALO, gd), lambda i, j: (jnp.maximum(i * hb - 1, 0), j))],
        out_specs=pl.BlockSpec((tr, gd), lambda i, j: (i, j)),
        out_shape=jax.ShapeDtypeStruct((T, D), BF16),
        compiler_params=_cp("parallel", "parallel"),
    )(x, x)


def _pool_bwd(name, dp, du):
    T, D = dp.shape
    gd = D // len(POOL_WINDOWS)
    tr = _pick(T, 512)
    hb = tr // POOL_HALO
    last_h = T // POOL_HALO - 1
    n = tr + POOL_HALO

    def body(dp_ref, h_ref, du_ref, o_ref):
        i, j = pl.program_id(0), pl.program_id(1)
        dpv = dp_ref[...]
        q = dpv / _pool_cnt(i, j, tr, tr, 0)
        qh = jnp.where(i < pl.num_programs(0) - 1, h_ref[...] / _pool_cnt(i, j, tr, POOL_HALO, tr), 0.0)
        cat = jnp.concatenate([q, qh], axis=0)
        f2 = cat + pltpu.roll(cat, n - 1, 0)
        f4 = f2 + pltpu.roll(f2, n - 2, 0)
        f8 = f4 + pltpu.roll(f4, n - 4, 0)
        f16 = f8 + pltpu.roll(f8, n - 8, 0)
        sel = _sel4(j, (f2, f4, f8, f16))[:tr]
        o_ref[...] = ALPHA * du_ref[...] + sel - dpv

    return pl.pallas_call(
        body, name=name, grid=(T // tr, len(POOL_WINDOWS)),
        in_specs=[pl.BlockSpec((tr, gd), lambda i, j: (i, j)),
                  pl.BlockSpec((POOL_HALO, gd), lambda i, j: (jnp.minimum((i + 1) * hb, last_h), j)),
                  pl.BlockSpec((tr, gd), lambda i, j: (i, j))],
        out_specs=pl.BlockSpec((tr, gd), lambda i, j: (i, j)),
        out_shape=jax.ShapeDtypeStruct((T, D), F32),
        compiler_params=_cp("parallel", "parallel"),
    )(dp, dp, du)


def _conv_taps(cat, wv, rows):
    shifted = [cat[CONV_HALO:] if s == 0 else pltpu.roll(cat, s, 0)[CONV_HALO:] for s in range(CONV_K)]
    acc = shifted[0] * wv[CONV_K - 1:CONV_K]
    for s in range(1, CONV_K):
        acc = acc + shifted[s] * wv[CONV_K - 1 - s:CONV_K - s]
    return acc, shifted


def _conv_fwd(name, xp, w, b):
    T, CD = xp.shape
    tr, tc = _pick(T, 512), _pick(CD, 512)
    hb = tr // CONV_HALO

    def body(x_ref, h_ref, w_ref, b_ref, o_ref):
        i = pl.program_id(0)
        halo = jnp.where(i > 0, h_ref[...], 0.0)
        cat = jnp.concatenate([halo, x_ref[...]], axis=0)
        acc, _ = _conv_taps(cat, w_ref[...], tr)
        acc = acc + b_ref[...]
        o_ref[...] = acc * _sigmoid(acc)

    return pl.pallas_call(
        body, name=name, grid=(T // tr, CD // tc),
        in_specs=[pl.BlockSpec((tr, tc), lambda i, j: (i, j)),
                  pl.BlockSpec((CONV_HALO, tc), lambda i, j: (jnp.maximum(i * hb - 1, 0), j)),
                  pl.BlockSpec((CONV_K, tc), lambda i, j: (0, j)),
                  pl.BlockSpec((1, tc), lambda i, j: (0, j))],
        out_specs=pl.BlockSpec((tr, tc), lambda i, j: (i, j)),
        out_shape=jax.ShapeDtypeStruct((T, CD), F32),
        compiler_params=_cp("parallel", "parallel"),
    )(xp, xp, w, b)


def _conv_bwd_a(name, dxs, dbm, dcm, xp, w, b):
    T, CD = xp.shape
    tr = _pick(T, 512)
    tc = _pick(dbm.shape[1], 512)
    hb = tr // CONV_HALO
    nx, nb = dxs.shape[1] // tc, dbm.shape[1] // tc

    def part_spec(lo, n):
        def imap(j, i):
            inside = jnp.logical_and(j >= lo, j < lo + n)
            return (jnp.where(inside, i, 0), jnp.clip(j - lo, 0, n - 1))
        return pl.BlockSpec((tr, tc), imap)

    def body(dx_ref, db_ref, dc_ref, x_ref, h_ref, w_ref, b_ref, o_ref, dw_ref, dbias_ref):
        j, i = pl.program_id(0), pl.program_id(1)
        first = i == 0
        d = jnp.where(j < nx, dx_ref[...], jnp.where(j < nx + nb, db_ref[...], dc_ref[...]))
        halo = jnp.where(i > 0, h_ref[...], 0.0)
        cat = jnp.concatenate([halo, x_ref[...]], axis=0)
        acc, shifted = _conv_taps(cat, w_ref[...], tr)
        acc = acc + b_ref[...]
        sg = _sigmoid(acc)
        dconv = d * (sg * (1.0 + acc * (1.0 - sg)))
        o_ref[...] = dconv
        _accum(dbias_ref, jnp.sum(dconv, axis=0, keepdims=True), first)
        tap = lax.broadcasted_iota(jnp.int32, (CONV_K, tc), 0)
        dwv = jnp.zeros((CONV_K, tc), F32)
        for s in range(CONV_K):
            dwv = jnp.where(tap == CONV_K - 1 - s, jnp.sum(dconv * shifted[s], axis=0, keepdims=True), dwv)
        _accum(dw_ref, dwv, first)

    return pl.pallas_call(
        body, name=name, grid=(CD // tc, T // tr),
        in_specs=[part_spec(0, nx), part_spec(nx, nb), part_spec(nx + nb, nb),
                  pl.BlockSpec((tr, tc), lambda j, i: (i, j)),
                  pl.BlockSpec((CONV_HALO, tc), lambda j, i: (jnp.maximum(i * hb - 1, 0), j)),
                  pl.BlockSpec((CONV_K, tc), lambda j, i: (0, j)),
                  pl.BlockSpec((1, tc), lambda j, i: (0, j))],
        out_specs=[pl.BlockSpec((tr, tc), lambda j, i: (i, j)),
                   pl.BlockSpec((CONV_K, tc), lambda j, i: (0, j)),
                   pl.BlockSpec((1, tc), lambda j, i: (0, j))],
        out_shape=[jax.ShapeDtypeStruct((T, CD), F32), jax.ShapeDtypeStruct((CONV_K, CD), F32),
                   jax.ShapeDtypeStruct((1, CD), F32)],
        compiler_params=_cp("parallel", "arbitrary"),
    )(dxs, dbm, dcm, xp, xp, w, b)


def _conv_bwd_b(name, dconv, w):
    T, CD = dconv.shape
    tr, tc = _pick(T, 512), _pick(CD, 512)
    hb = tr // CONV_HALO
    last_h = T // CONV_HALO - 1
    n = tr + CONV_HALO

    def body(d_ref, h_ref, w_ref, o_ref):
        i = pl.program_id(0)
        halo = jnp.where(i < pl.num_programs(0) - 1, h_ref[...], 0.0)
        cat = jnp.concatenate([d_ref[...], halo], axis=0)
        wv = w_ref[...]
        acc = cat[:tr] * wv[CONV_K - 1:CONV_K]
        for s in range(1, CONV_K):
            acc = acc + pltpu.roll(cat, n - s, 0)[:tr] * wv[CONV_K - 1 - s:CONV_K - s]
        o_ref[...] = acc.astype(BF16)

    return pl.pallas_call(
        body, name=name, grid=(T // tr, CD // tc),
        in_specs=[pl.BlockSpec((tr, tc), lambda i, j: (i, j)),
                  pl.BlockSpec((CONV_HALO, tc), lambda i, j: (jnp.minimum((i + 1) * hb, last_h), j)),
                  pl.BlockSpec((CONV_K, tc), lambda i, j: (0, j))],
        out_specs=pl.BlockSpec((tr, tc), lambda i, j: (i, j)),
        out_shape=jax.ShapeDtypeStruct((T, CD), BF16),
        compiler_params=_cp("parallel", "parallel"),
    )(dconv, dconv, w)


def _dt_fwd(name, dtp, bias):
    T, H = dtp.shape

    def body(x_ref, b_ref, o_ref):
        v = x_ref[...] + b_ref[...]
        u = jnp.exp(-jnp.abs(v))
        w1 = 1.0 + u
        lp = jnp.where(w1 == 1.0, u, jnp.log(w1) * (u / jnp.where(w1 == 1.0, 1.0, w1 - 1.0)))
        o_ref[...] = jnp.maximum(v, 0.0) + lp

    return pl.pallas_call(body, name=name, out_shape=jax.ShapeDtypeStruct((T, H), F32))(dtp, bias)


def _dt_bwd(name, ddt, dtp, bias):
    T, H = dtp.shape

    def body(d_ref, x_ref, b_ref, o_ref, ob_ref, db_ref):
        g = d_ref[...] * _sigmoid(x_ref[...] + b_ref[...])
        o_ref[...] = g
        ob_ref[...] = g.astype(BF16)
        db_ref[...] = jnp.sum(g, axis=0, keepdims=True)

    return pl.pallas_call(
        body, name=name,
        out_shape=[jax.ShapeDtypeStruct((T, H), F32), jax.ShapeDtypeStruct((T, H), BF16),
                   jax.ShapeDtypeStruct((1, H), F32)],
    )(ddt, dtp, bias)


def _split(v):
    hi = v.astype(BF16)
    return hi, (v - hi.astype(F32)).astype(BF16)


def _dot01(form, a, b, mask):
    if mask == "b":
        hi, lo = _split(a)
        mb = b.astype(BF16)
        return (lax.dot_general(hi, mb, _DIMS[form], preferred_element_type=F32)
                + lax.dot_general(lo, mb, _DIMS[form], preferred_element_type=F32))
    hi, lo = _split(b)
    ma = a.astype(BF16)
    return (lax.dot_general(ma, hi, _DIMS[form], preferred_element_type=F32)
            + lax.dot_general(ma, lo, _DIMS[form], preferred_element_type=F32))


def _dotb(form, a, b):
    return lax.dot_general(a.astype(BF16), b.astype(BF16), _DIMS[form], preferred_element_type=F32)


def _ssd_common(dtc, dtr, alr, alc, gw):
    li = lax.broadcasted_iota(jnp.int32, (CHUNK, CHUNK), 0)
    si = lax.broadcasted_iota(jnp.int32, (CHUNK, CHUNK), 1)
    tri = (li >= si).astype(F32)
    trit = (li <= si).astype(F32)
    a_row = -jnp.exp(alr)
    a_col = -jnp.exp(alc)
    acs_c = _dot01("nn", tri, dtc * a_row, "a")
    acs_r = _dot01("nn", dtr * a_col, trit, "b")
    eh = lax.broadcasted_iota(jnp.int32, (HEAD_PAD, gw), 0)
    ec = lax.broadcasted_iota(jnp.int32, (HEAD_PAD, gw), 1) // HEAD_DIM
    expand = (eh == ec).astype(F32)
    th = lax.broadcasted_iota(jnp.int32, (gw, HEAD_PAD), 1)
    tc = lax.broadcasted_iota(jnp.int32, (gw, HEAD_PAD), 0) // HEAD_DIM
    reduce_ = (th == tc).astype(F32)
    acs_last = acs_c[CHUNK - 1:CHUNK, :]
    acs_e = _dot01("nn", acs_c, expand, "b")
    return dict(li=li, si=si, tri=tri, trit=trit, a_row=a_row, acs_c=acs_c, acs_r=acs_r,
                reduce=reduce_, acs_last=acs_last, dt_e=_dot01("nn", dtc, expand, "b"),
                eacs_e=jnp.exp(acs_e), dec_e=jnp.exp(acs_e[CHUNK - 1:CHUNK, :] - acs_e),
                dec_h=jnp.exp(acs_last - acs_c))


def _ssd_specs(T, DI, gw, nc, rev):
    nsb = DI // D_STATE
    cidx = (lambda c: nc - 1 - c) if rev else (lambda c: c)
    return dict(
        xs=pl.BlockSpec((CHUNK, gw), lambda g, c: (cidx(c), g)),
        bm=pl.BlockSpec((CHUNK, D_STATE), lambda g, c: (cidx(c), nsb + g)),
        cm=pl.BlockSpec((CHUNK, D_STATE), lambda g, c: (cidx(c), nsb + N_GROUPS + g)),
        dtc=pl.BlockSpec((None, CHUNK, HEAD_PAD), lambda g, c: (g, cidx(c), 0)),
        dtr=pl.BlockSpec((None, ROW_PAD, CHUNK), lambda g, c: (g, 0, cidx(c))),
        alr=pl.BlockSpec((None, 1, HEAD_PAD), lambda g, c: (g, 0, 0)),
        alc=pl.BlockSpec((None, ROW_PAD, 1), lambda g, c: (g, 0, 0)),
        de=pl.BlockSpec((None, 1, gw), lambda g, c: (g, 0, 0)),
        hp=pl.BlockSpec((None, None, gw, D_STATE), lambda g, c: (cidx(c), g, 0, 0)),
        bc=pl.BlockSpec((CHUNK, D_STATE), lambda g, c: (cidx(c), g)),
        acc=pl.BlockSpec((None, 1, HEAD_PAD), lambda g, c: (g, 0, 0)),
    )


def _ssd_fwd(name, xbc, dtc, dtr, alr, alc, d_e, DI, deps=()):
    T = xbc.shape[0]
    nc = T // CHUNK
    gw = DI // N_GROUPS
    sp = _ssd_specs(T, DI, gw, nc, False)

    def body(xs_ref, b_ref, c_ref, dtc_ref, dtr_ref, alr_ref, alc_ref, de_ref, *rest):
        y_ref, hp_ref, h_scr = rest[len(deps):]

        @pl.when(pl.program_id(1) == 0)
        def _():
            h_scr[...] = jnp.zeros_like(h_scr)

        hpv = h_scr[...]
        hp_ref[...] = hpv
        xs = xs_ref[...]
        bb = b_ref[...].astype(BF16)
        cb_ = c_ref[...].astype(BF16)
        cm = _ssd_common(dtc_ref[...], dtr_ref[...], alr_ref[...], alc_ref[...], gw)
        x = xs * cm["dt_e"]
        xb = x.astype(BF16)
        cbm = _dotb("nt", cb_, bb)
        causal = cm["li"] >= cm["si"]
        second = lax.broadcasted_iota(jnp.int32, (1, HEAD_PAIR), 1) >= HEAD_DIM
        pieces = []
        for pr in range(gw // HEAD_PAIR):
            xp = xb[:, pr * HEAD_PAIR:(pr + 1) * HEAD_PAIR]
            for h2 in range(2):
                j = 2 * pr + h2
                seg = cm["acs_c"][:, j:j + 1] - cm["acs_r"][j:j + 1, :]
                lmat = jnp.exp(jnp.where(causal, seg, -1e30))
                yj = _dotb("nn", cbm * lmat, xp)
                yp = yj if h2 == 0 else jnp.where(second, yj, yp)
            pieces.append(yp)
        ydiag = pieces[0] if len(pieces) == 1 else jnp.concatenate(pieces, axis=1)
        states = _dotb("tn", x * cm["dec_e"], bb)
        yoff = _dotb("nt", cb_, hpv) * cm["eacs_e"]
        y_ref[...] = ydiag + yoff + xs * de_ref[...]
        cd_rows = jnp.sum(cm["reduce"] * jnp.exp(cm["acs_last"]), axis=1, keepdims=True)
        h_scr[...] = cd_rows * hpv + states

    return pl.pallas_call(
        body, name=name, grid=(N_GROUPS, nc),
        in_specs=[sp["xs"], sp["bm"], sp["cm"], sp["dtc"], sp["dtr"], sp["alr"], sp["alc"], sp["de"]]
        + [pl.BlockSpec((8, 128), lambda g, c: (0, 0)) for _ in deps],
        out_specs=[sp["xs"], sp["hp"]],
        out_shape=[jax.ShapeDtypeStruct((T, DI), F32), jax.ShapeDtypeStruct((nc, N_GROUPS, gw, D_STATE), F32)],
        scratch_shapes=[pltpu.VMEM((gw, D_STATE), F32)],
        compiler_params=_cp("parallel", "arbitrary"),
    )(xbc, xbc, xbc, dtc, dtr, alr, alc, d_e, *deps)


def _ssd_bwd(name, dy, xbc, dtc, dtr, alr, alc, d_e, hprev, DI):
    T = xbc.shape[0]
    nc = T // CHUNK
    gw = DI // N_GROUPS
    sp = _ssd_specs(T, DI, gw, nc, True)

    def body(dy_ref, xs_ref, b_ref, c_ref, dtc_ref, dtr_ref, alr_ref, alc_ref, de_ref, hp_ref,
             dxs_ref, db_ref, dc_ref, ddt_ref, dal_ref, dd_ref, dh_scr):
        first = pl.program_id(1) == 0

        @pl.when(first)
        def _():
            dh_scr[...] = jnp.zeros_like(dh_scr)

        xs = xs_ref[...]
        dyv = dy_ref[...]
        bb = b_ref[...].astype(BF16)
        cb_ = c_ref[...].astype(BF16)
        dtc_v = dtc_ref[...]
        cm = _ssd_common(dtc_v, dtr_ref[...], alr_ref[...], alc_ref[...], gw)
        hpv = hp_ref[...]
        hpb = hpv.astype(BF16)
        dhn = dh_scr[...]
        dhnb = dhn.astype(BF16)
        x = xs * cm["dt_e"]
        xb = x.astype(BF16)
        cbm = _dotb("nt", cb_, bb)
        causal = cm["li"] >= cm["si"]
        second = lax.broadcasted_iota(jnp.int32, (1, HEAD_PAIR), 1) >= HEAD_DIM
        lane_pad = lax.broadcasted_iota(jnp.int32, (1, HEAD_PAD), 1)
        sub_pad = lax.broadcasted_iota(jnp.int32, (ROW_PAD, 1), 0)

        dxs = dyv * de_ref[...]
        dd_part = jnp.sum(_dot01("nn", dyv * xs, cm["reduce"], "b"), axis=0, keepdims=True)
        _accum(dd_ref, dd_part, first)

        dcb = jnp.zeros((CHUNK, CHUNK), F32)
        dacs_c = jnp.zeros((CHUNK, HEAD_PAD), F32)
        dacs_r = jnp.zeros((ROW_PAD, CHUNK), F32)
        pieces = []
        for pr in range(gw // HEAD_PAIR):
            xp = xb[:, pr * HEAD_PAIR:(pr + 1) * HEAD_PAIR]
            dyp = dyv[:, pr * HEAD_PAIR:(pr + 1) * HEAD_PAIR]
            for h2 in range(2):
                j = 2 * pr + h2
                seg = cm["acs_c"][:, j:j + 1] - cm["acs_r"][j:j + 1, :]
                lmat = jnp.exp(jnp.where(causal, seg, -1e30))
                mmat = cbm * lmat
                dym = jnp.where(second if h2 == 1 else jnp.logical_not(second), dyp, 0.0).astype(BF16)
                dm = _dotb("nt", dym, xp)
                dxj = _dotb("tn", mmat, dym)
                dxp = dxj if h2 == 0 else dxp + dxj
                dcb = dcb + dm * lmat
                dseg = dm * mmat
                dacs_c = dacs_c + jnp.where(lane_pad == j, jnp.sum(dseg, axis=1, keepdims=True), 0.0)
                dacs_r = dacs_r - jnp.where(sub_pad == j, jnp.sum(dseg, axis=0, keepdims=True), 0.0)
            pieces.append(dxp)
        dx = pieces[0] if len(pieces) == 1 else jnp.concatenate(pieces, axis=1)
        dc = _dotb("nn", dcb, bb)
        db = _dotb("tn", dcb, cb_)

        gm = _dotb("nt", cb_, hpb)
        dgm = dyv * cm["eacs_e"]
        dacs_c = dacs_c + _dot01("nn", dgm * gm, cm["reduce"], "b")
        dc = dc + _dotb("nn", dgm, hpb)
        dhp = _dotb("tn", dgm, cb_)

        cd_row = jnp.exp(cm["acs_last"])
        cd_rows = jnp.sum(cm["reduce"] * cd_row, axis=1, keepdims=True)
        dhp = dhp + cd_rows * dhn
        rsum = jnp.sum(dhn * hpv, axis=1, keepdims=True)
        dacs_last = jnp.sum(cm["reduce"] * rsum, axis=0, keepdims=True) * cd_row
        dec_e = cm["dec_e"]
        xdec = x * dec_e
        dxdec = _dotb("nt", bb, dhnb)
        db = db + _dotb("nn", xdec, dhnb)
        dx = dx + dxdec * dec_e
        tdec = _dot01("nn", dxdec * x, cm["reduce"], "b") * cm["dec_h"]
        dacs_c = dacs_c - tdec
        dacs_last = dacs_last + jnp.sum(tdec, axis=0, keepdims=True)
        row_id = lax.broadcasted_iota(jnp.int32, (CHUNK, 1), 0)
        dacs_c = dacs_c + jnp.where(row_id == CHUNK - 1, dacs_last, 0.0)

        dxs_ref[...] = dxs + dx * cm["dt_e"]
        ddt = _dot01("nn", dx * xs, cm["reduce"], "b")
        dda = _dot01("nn", cm["trit"], dacs_c, "a")
        dda_r = _dot01("nn", dacs_r, cm["tri"], "b")
        dda_rp = jnp.concatenate([dda_r, jnp.zeros((HEAD_PAD - ROW_PAD, CHUNK), F32)], axis=0)
        eye = (cm["li"] == cm["si"]).astype(F32)
        dda = dda + _dot01("nt", eye, dda_rp, "a")
        ddt_ref[...] = ddt + dda * cm["a_row"]
        _accum(dal_ref, jnp.sum(dda * dtc_v, axis=0, keepdims=True) * cm["a_row"], first)
        db_ref[...] = db
        dc_ref[...] = dc
        dh_scr[...] = dhp

    gs = D_STATE * N_GROUPS
    return pl.pallas_call(
        body, name=name, grid=(N_GROUPS, nc),
        in_specs=[sp["xs"], sp["xs"], sp["bm"], sp["cm"], sp["dtc"], sp["dtr"], sp["alr"], sp["alc"],
                  sp["de"], sp["hp"]],
        out_specs=[sp["xs"], sp["bc"], sp["bc"], sp["dtc"], sp["acc"], sp["acc"]],
        out_shape=[jax.ShapeDtypeStruct((T, DI), F32), jax.ShapeDtypeStruct((T, gs), F32),
                   jax.ShapeDtypeStruct((T, gs), F32), jax.ShapeDtypeStruct((N_GROUPS, T, HEAD_PAD), F32),
                   jax.ShapeDtypeStruct((N_GROUPS, 1, HEAD_PAD), F32),
                   jax.ShapeDtypeStruct((N_GROUPS, 1, HEAD_PAD), F32)],
        scratch_shapes=[pltpu.VMEM((gw, D_STATE), F32)],
        compiler_params=_cp("parallel", "arbitrary"),
    )(dy, xbc, xbc, xbc, dtc, dtr, alr, alc, d_e, hprev)


def _adam_math(w, g, m, v):
    m = ADAM_B1 * m + (1.0 - ADAM_B1) * g
    v = ADAM_B2 * v + (1.0 - ADAM_B2) * (g * g)
    m_hat = m / (1.0 - ADAM_B1 ** ADAM_STEP)
    v_hat = v / (1.0 - ADAM_B2 ** ADAM_STEP)
    delta = -ADAM_LR * (m_hat / (jnp.sqrt(v_hat) + ADAM_EPS) + ADAM_WD * w)
    return delta, m, v


def _adamw(name, w, m, v, g, layer=0, prev=None):
    L, R, C = w.shape
    tr = _rows(R, C * 4, ADAM_BLOCK_BYTES)
    tc = C
    while tr * tc * 4 > ADAM_BLOCK_BYTES and tc % 256 == 0:
        tc //= 2
    blk = pl.BlockSpec((None, tr, tc), lambda i, j: (layer, i, j))
    prev = [] if prev is None else list(prev)

    def body(w_ref, m_ref, v_ref, g_ref, *rest):
        go_ref, d_ref, mo_ref, vo_ref = rest[len(prev):]
        gv = g_ref[...]
        delta, mn, vn = _adam_math(w_ref[...], gv, m_ref[...], v_ref[...])
        go_ref[...] = gv
        d_ref[...] = delta
        mo_ref[...] = mn
        vo_ref[...] = vn

    return pl.pallas_call(
        body, name=name, grid=(R // tr, C // tc),
        in_specs=[blk] * 3 + [pl.BlockSpec((tr, tc), lambda i, j: (i, j))] + [_ANY] * len(prev),
        out_specs=[blk] * 4, out_shape=[jax.ShapeDtypeStruct((L, R, C), F32)] * 4,
        input_output_aliases={4 + k: k for k in range(len(prev))},
        compiler_params=_cp("parallel", "parallel"),
    )(w, m, v, g, *prev)


def _sum_peers(name, gathered):
    n, R, C = gathered.shape

    def body(g_ref, o_ref):
        acc = g_ref[0]
        for d in range(1, n):
            acc = acc + g_ref[d]
        o_ref[...] = acc

    return pl.pallas_call(body, name=name, out_shape=jax.ShapeDtypeStruct((R, C), F32))(gathered)


def _place():
    x, y, c = lax.axis_index("x"), lax.axis_index("y"), lax.axis_index("c")
    chips = [(1 - x, y), (x, 1 - y), (1 - x, 1 - y)]
    return x, y, c, chips


def _allgather_small(name, v, after=()):
    R, C = v.shape
    after = list(after)

    def body(v_ref, *rest):
        o_ref, send_sems, recv_sems = rest[len(after):]
        x, y, c, _ = _place()
        me = 4 * x + 2 * y + c
        o_ref[me] = v_ref[...]
        copies = []
        for k in range(1, N_DEV):
            px, py, pc = x ^ (k >> 2), y ^ ((k >> 1) & 1), c ^ (k & 1)
            copies.append(pltpu.make_async_remote_copy(
                src_ref=v_ref, dst_ref=o_ref.at[me], send_sem=send_sems.at[k - 1], recv_sem=recv_sems.at[k - 1],
                device_id=(px, py, pc), device_id_type=MESH))
        for cp in copies:
            cp.start()
        for cp in copies:
            cp.wait()

    return pl.pallas_call(
        body, name=name, out_shape=jax.ShapeDtypeStruct((N_DEV, R, C), F32),
        in_specs=[pl.BlockSpec(memory_space=pltpu.VMEM)] + [_ANY] * len(after),
        out_specs=pl.BlockSpec(memory_space=pltpu.VMEM),
        scratch_shapes=[pltpu.SemaphoreType.DMA((N_DEV - 1,)), pltpu.SemaphoreType.DMA((N_DEV - 1,))],
    )(v, *after)


_HBM = pl.BlockSpec(memory_space=pltpu.HBM)
_SEM = pl.BlockSpec(memory_space=pltpu.SEMAPHORE)
_VMEM = pl.BlockSpec(memory_space=pltpu.VMEM)
_EFFECT = pltpu.SideEffectType.DATAFLOW_SIDE_EFFECTING


def _in_hbm(v):
    return pltpu.with_memory_space_constraint(v, pltpu.HBM)


def _remote(src, dst, send_sem, recv_sem, device):
    return pltpu.make_async_remote_copy(src_ref=src, dst_ref=dst, send_sem=send_sem, recv_sem=recv_sem,
                                        device_id=device, device_id_type=MESH)


def _gather_copies(src, land, ici_s, ici_r, own_s, own_r, arrivals=True):
    x, y, c, chips = _place()
    a = 2 * x + y
    sends, lands = [], []
    for i in range(len(src)):
        own = _remote(src[i], land[i].at[a], own_s.at[i], own_r.at[i], (x, y, 1 - c))
        sends.append(own)
        if arrivals:
            lands.append(own)
        for k, (px, py) in enumerate(chips):
            s, r = ici_s.at[3 * i + k], ici_r.at[3 * i + k]
            sends.append(_remote(src[i].at[c], land[i].at[a, c], s, r, (px, py, c)))
            if arrivals:
                lands.append(_remote(src[i].at[c], land[i].at[2 * px + py, c], s, r, (px, py, c)))
    return sends, lands


def _gather_start(name, shards, after=None):
    n = len(shards)
    lands = [lax.empty((N_CHIPS,) + s.shape, s.dtype) for s in shards]
    n_in = 2 * n + (0 if after is None else 1)

    def body(*refs):
        src, land = refs[:n], refs[n:2 * n]
        sems = refs[n_in:n_in + 4]
        token = refs[-1]
        sends, _ = _gather_copies(src, land, *sems, arrivals=False)
        for cp in sends:
            cp.start()
        token[...] = jnp.zeros_like(token)

    outs = pl.pallas_call(
        body, name=name,
        out_shape=(pltpu.SemaphoreType.DMA((3 * n,)), pltpu.SemaphoreType.DMA((3 * n,)),
                   pltpu.SemaphoreType.DMA((n,)), pltpu.SemaphoreType.DMA((n,)),
                   *[pltpu.HBM(s.shape, s.dtype) for s in shards], *[pltpu.HBM(l.shape, l.dtype) for l in lands],
                   jax.ShapeDtypeStruct((8, 128), F32)),
        in_specs=[_HBM] * (2 * n) + ([] if after is None else [_ANY]),
        out_specs=(_SEM,) * 4 + (_HBM,) * (2 * n) + (_VMEM,),
        input_output_aliases={i: 4 + i for i in range(2 * n)},
        compiler_params=pltpu.CompilerParams(has_side_effects=_EFFECT),
    )(*[_in_hbm(s) for s in shards], *[_in_hbm(l) for l in lands], *([] if after is None else [after]))
    return dict(sems=outs[:4], src=outs[4:4 + n], land=outs[4 + n:4 + 2 * n], token=outs[-1])


def _gather_wait(name, started, after):
    n = len(started["src"])
    after = list(after)

    def body(*refs):
        src, land = refs[:n], refs[n:2 * n]
        sems = refs[2 * n:2 * n + 4]
        sends, lands = _gather_copies(src, land, *sems)
        for cp in sends:
            cp.wait_send()
        for cp in lands:
            cp.wait_recv()

    outs = pl.pallas_call(
        body, name=name,
        out_shape=[pltpu.HBM(v.shape, v.dtype) for v in list(started["src"]) + list(started["land"])],
        in_specs=[_HBM] * (2 * n) + [_SEM] * 4 + [_ANY] * len(after), out_specs=[_HBM] * (2 * n),
        input_output_aliases={i: i for i in range(2 * n)},
        compiler_params=pltpu.CompilerParams(has_side_effects=_EFFECT),
    )(*started["src"], *started["land"], *started["sems"], *after)
    return outs[n:]


def _gather_forward(name, lands):
    n = len(lands)

    def body(*refs):
        buf = refs[n:2 * n]
        send_sems, recv_sems = refs[2 * n:]
        x, y, c, chips = _place()
        sends, lands_ = [], []
        for i in range(n):
            for k, (px, py) in enumerate(chips):
                b = 2 * px + py
                sends.append(_remote(buf[i].at[b, c], buf[i].at[b, c], send_sems.at[i, k], recv_sems.at[i, k],
                                     (x, y, 1 - c)))
                lands_.append(_remote(buf[i].at[b, 1 - c], buf[i].at[b, 1 - c], send_sems.at[i, k],
                                      recv_sems.at[i, k], (x, y, 1 - c)))
        for cp in sends:
            cp.start()
        for cp in sends:
            cp.wait_send()
        for cp in lands_:
            cp.wait_recv()

    return pl.pallas_call(
        body, name=name, in_specs=[_ANY] * n, out_specs=[_ANY] * n,
        out_shape=[jax.ShapeDtypeStruct(l.shape, l.dtype) for l in lands],
        input_output_aliases={i: i for i in range(n)},
        scratch_shapes=[pltpu.SemaphoreType.DMA((n, 3)), pltpu.SemaphoreType.DMA((n, 3))],
    )(*lands)


def _split_start(name, bufs, copies, n_sems, after=()):
    nb = len(bufs)
    after = list(after)

    def body(*refs):
        send_sems, recv_sems = refs[nb + len(after):nb + len(after) + 2]
        token = refs[-1]
        sends, _ = copies(refs[:nb], send_sems, recv_sems, False)
        for cp in sends:
            cp.start()
        token[...] = jnp.zeros_like(token)

    outs = pl.pallas_call(
        body, name=name,
        out_shape=(pltpu.SemaphoreType.DMA((n_sems,)), pltpu.SemaphoreType.DMA((n_sems,)),
                   *[pltpu.HBM(b.shape, b.dtype) for b in bufs], jax.ShapeDtypeStruct((8, 128), F32)),
        in_specs=[_HBM] * nb + [_ANY] * len(after), out_specs=(_SEM,) * 2 + (_HBM,) * nb + (_VMEM,),
        input_output_aliases={i: 2 + i for i in range(nb)},
        compiler_params=pltpu.CompilerParams(has_side_effects=_EFFECT),
    )(*[_in_hbm(b) for b in bufs], *after)
    return dict(sems=outs[:2], bufs=outs[2:2 + nb], token=outs[-1])


def _split_wait(name, started, copies, after):
    nb = len(started["bufs"])
    after = list(after)

    def body(*refs):
        sends, arrivals = copies(refs[:nb], refs[nb], refs[nb + 1], True)
        for cp in sends:
            cp.wait_send()
        for cp in arrivals:
            cp.wait_recv()

    return pl.pallas_call(
        body, name=name, out_shape=[pltpu.HBM(b.shape, b.dtype) for b in started["bufs"]],
        in_specs=[_HBM] * nb + [_SEM] * 2 + [_ANY] * len(after), out_specs=[_HBM] * nb,
        input_output_aliases={i: i for i in range(nb)},
        compiler_params=pltpu.CompilerParams(has_side_effects=_EFFECT),
    )(*started["bufs"], *started["sems"], *after)


def _forward_copies(bufs, send_sems, recv_sems, arrivals):
    x, y, c, chips = _place()
    sends, lands = [], []
    for i, buf in enumerate(bufs):
        for k, (px, py) in enumerate(chips):
            b, s, r = 2 * px + py, send_sems.at[3 * i + k], recv_sems.at[3 * i + k]
            sends.append(_remote(buf.at[b, c], buf.at[b, c], s, r, (x, y, 1 - c)))
            if arrivals:
                lands.append(_remote(buf.at[b, 1 - c], buf.at[b, 1 - c], s, r, (x, y, 1 - c)))
    return sends, lands


def _pair_copies(bufs, send_sems, recv_sems, arrivals):
    x, y, c, _ = _place()
    n = len(bufs) // 2
    copies = [_remote(bufs[i].at[:, 1 - c], bufs[n + i], send_sems.at[i], recv_sems.at[i], (x, y, 1 - c))
              for i in range(n)]
    return copies, copies


def _chip_copies(src, land, send_sems, recv_sems):
    x, y, c, chips = _place()
    return [_remote(src[i].at[2 * px + py], land[i].at[k], send_sems.at[3 * i + k], recv_sems.at[3 * i + k],
                    (px, py, c))
            for i in range(len(src)) for k, (px, py) in enumerate(chips)]


def _chip_start(name, parts):
    n = len(parts)
    lands = [lax.empty((3,) + p.shape[1:], p.dtype) for p in parts]

    def body(*refs):
        src, land = refs[:n], refs[n:2 * n]
        send_sems, recv_sems = refs[2 * n:2 * n + 2]
        token = refs[-1]
        for cp in _chip_copies(src, land, send_sems, recv_sems):
            cp.start()
        token[...] = jnp.zeros_like(token)

    outs = pl.pallas_call(
        body, name=name,
        out_shape=(pltpu.SemaphoreType.DMA((3 * n,)), pltpu.SemaphoreType.DMA((3 * n,)),
                   *[pltpu.HBM(p.shape, p.dtype) for p in parts], *[pltpu.HBM(l.shape, l.dtype) for l in lands],
                   jax.ShapeDtypeStruct((8, 128), F32)),
        in_specs=[_HBM] * (2 * n), out_specs=(_SEM,) * 2 + (_HBM,) * (2 * n) + (_VMEM,),
        input_output_aliases={i: 2 + i for i in range(2 * n)},
        compiler_params=pltpu.CompilerParams(has_side_effects=_EFFECT),
    )(*[_in_hbm(p) for p in parts], *[_in_hbm(l) for l in lands])
    return dict(sems=outs[:2], src=outs[2:2 + n], land=outs[2 + n:2 + 2 * n], token=outs[-1])


def _chip_wait(name, started, after):
    n = len(started["src"])
    after = list(after)

    def body(*refs):
        src, land = refs[:n], refs[n:2 * n]
        send_sems, recv_sems = refs[2 * n:2 * n + 2]
        copies = _chip_copies(src, land, send_sems, recv_sems)
        for cp in copies:
            cp.wait_send()
        for cp in copies:
            cp.wait_recv()

    outs = pl.pallas_call(
        body, name=name,
        out_shape=[pltpu.HBM(v.shape, v.dtype) for v in list(started["src"]) + list(started["land"])],
        in_specs=[_HBM] * (2 * n) + [_SEM] * 2 + [_ANY] * len(after), out_specs=[_HBM] * (2 * n),
        input_output_aliases={i: i for i in range(2 * n)},
        compiler_params=pltpu.CompilerParams(has_side_effects=_EFFECT),
    )(*started["src"], *started["land"], *started["sems"], *after)
    return outs[:n], outs[n:]


def _half_exchange(name, shards):
    n = len(shards)

    def body(*refs):
        buf = refs[n:2 * n]
        send_sems, recv_sems = refs[2 * n:]
        x, y, c, _ = _place()
        copies = [pltpu.make_async_remote_copy(
            src_ref=buf[i].at[c], dst_ref=buf[i].at[c], send_sem=send_sems.at[i], recv_sem=recv_sems.at[i],
            device_id=(x, y, 1 - c), device_id_type=MESH) for i in range(n)]
        for cp in copies:
            cp.start()
        for cp in copies:
            cp.wait()

    return pl.pallas_call(
        body, name=name, in_specs=[_ANY] * n, out_specs=[_ANY] * n,
        out_shape=[jax.ShapeDtypeStruct(s.shape, s.dtype) for s in shards],
        input_output_aliases={i: i for i in range(n)},
        scratch_shapes=[pltpu.SemaphoreType.DMA((n,)), pltpu.SemaphoreType.DMA((n,))],
    )(*shards)


def _pair_add(name, pos, part, sib):
    Q, _, R2, C = part.shape
    tr = _pick(R2, 256)

    def body(pos_ref, p_ref, s_ref, o_ref):
        o_ref[...] = (p_ref[...].astype(F32) + s_ref[...].astype(F32)).astype(BF16)

    return pl.pallas_call(
        body, name=name,
        grid_spec=pltpu.PrefetchScalarGridSpec(
            num_scalar_prefetch=1, grid=(Q, R2 // tr),
            in_specs=[pl.BlockSpec((None, None, tr, C), lambda q, i, pos_ref: (q, pos_ref[1], i, 0)),
                      pl.BlockSpec((None, tr, C), lambda q, i, pos_ref: (q, i, 0))],
            out_specs=pl.BlockSpec((None, tr, C), lambda q, i, pos_ref: (q, i, 0))),
        out_shape=jax.ShapeDtypeStruct((Q, R2, C), BF16),
        compiler_params=_cp("parallel", "parallel"),
    )(pos, part, sib)


def _chip_sum(name, pos, own, got):
    _, R2, C = own.shape
    tr = _pick(R2, 256)

    def body(pos_ref, o_ref, g_ref, out_ref):
        acc = o_ref[...].astype(F32)
        for k in range(3):
            acc = acc + g_ref[k].astype(F32)
        out_ref[...] = acc

    return pl.pallas_call(
        body, name=name,
        grid_spec=pltpu.PrefetchScalarGridSpec(
            num_scalar_prefetch=1, grid=(R2 // tr,),
            in_specs=[pl.BlockSpec((None, tr, C), lambda i, pos_ref: (pos_ref[0], i, 0)),
                      pl.BlockSpec((3, tr, C), lambda i, pos_ref: (0, i, 0))],
            out_specs=pl.BlockSpec((None, tr, C), lambda i, pos_ref: (pos_ref[1], i, 0))),
        out_shape=jax.ShapeDtypeStruct((2, R2, C), F32),
        compiler_params=_cp("parallel"),
    )(pos, own, got)


def _reduce_pair_start(tag, parts):
    split = [p.reshape(p.shape[0], 2, p.shape[1] // 2, p.shape[2]) for p in parts]
    lands = [lax.empty((p.shape[0],) + p.shape[2:], p.dtype) for p in split]
    return _split_start(tag + "_pair_start", split + lands, _pair_copies, len(parts))


def _reduce_pair_end(tag, pos, started, after):
    n = len(started["bufs"]) // 2
    bufs = _split_wait(tag + "_pair_wait", started, _pair_copies, after)
    chip = [_pair_add(f"{tag}_add{i}", pos, bufs[i], bufs[n + i]) for i in range(n)]
    return _chip_start(tag + "_chip_start", chip)


def _half_copies(bufs, send_sems, recv_sems, arrivals):
    x, y, c, _ = _place()
    sends = [_remote(b.at[c], b.at[c], send_sems.at[i], recv_sems.at[i], (x, y, 1 - c)) for i, b in enumerate(bufs)]
    lands = [_remote(b.at[1 - c], b.at[1 - c], send_sems.at[i], recv_sems.at[i], (x, y, 1 - c))
             for i, b in enumerate(bufs)] if arrivals else []
    return sends, lands


def _reduce_sum_start(tag, pos, started, after):
    chip, got = _chip_wait(tag + "_chip_wait", started, after)
    halves = [_chip_sum(f"{tag}_sum{i}", pos, chip[i], got[i]) for i in range(len(chip))]
    return _split_start(tag + "_half_start", halves, _half_copies, len(halves))


def _reduce_sum_end(tag, started, after, shapes):
    full = _split_wait(tag + "_half_wait", started, _half_copies, after)
    return [f.reshape(shp) for f, shp in zip(full, shapes)]


def _reduce_end(tag, pos, started, after, shapes):
    chip, got = _chip_wait(tag + "_chip_wait", started, after)
    halves = [_chip_sum(f"{tag}_sum{i}", pos, chip[i], got[i]) for i in range(len(chip))]
    full = _half_exchange(tag + "_half", halves)
    return [f.reshape(shp) for f, shp in zip(full, shapes)]


def _pad_to(v, mult):
    n = v.shape[0]
    return jnp.pad(v, (0, (-n) % mult))


def kernel(x, p, pool_w, pool_scale, ssm_in_w, ssm_conv_w, ssm_conv_b, ssm_dt_bias, ssm_a_log, ssm_d, ssm_norm_w, ssm_out_w, mlp_w1, mlp_w2, ln_g, ln_b, ple_w, ple_gate_w, loss_target, m_pool_w, m_pool_scale, m_ssm_in_w, m_ssm_conv_w, m_ssm_conv_b, m_ssm_dt_bias, m_ssm_a_log, m_ssm_d, m_ssm_norm_w, m_ssm_out_w, m_mlp_w1, m_mlp_w2, m_ln_g, m_ln_b, m_ple_w, m_ple_gate_w, v_pool_w, v_pool_scale, v_ssm_in_w, v_ssm_conv_w, v_ssm_conv_b, v_ssm_dt_bias, v_ssm_a_log, v_ssm_d, v_ssm_norm_w, v_ssm_out_w, v_mlp_w1, v_mlp_w2, v_ln_g, v_ln_b, v_ple_w, v_ple_gate_w):
    T, D = x.shape[1], x.shape[2]
    NG = len(POOL_WINDOWS)
    GD = D // NG
    DI = ssm_out_w.shape[1] * N_CHIPS
    H = ssm_dt_bias.shape[1]
    HPG = H // N_GROUPS
    GW = DI // N_GROUPS
    GS = N_GROUPS * D_STATE
    CD = DI + 2 * GS
    DF = mlp_w1.shape[2] * N_CHIPS
    PD = ple_w.shape[1]
    NIN = ssm_in_w.shape[2]

    xi, yi, ci = lax.axis_index("x"), lax.axis_index("y"), lax.axis_index("c")
    chip = 2 * xi + yi
    pos = jnp.stack([chip, ci]).astype(jnp.int32)

    x0 = x[0]
    tgt = loss_target[0]
    p0b, p1b = p[0, 0].astype(BF16), p[1, 0].astype(BF16)

    small_sh = jnp.concatenate([ssm_conv_w[0].reshape(-1), ssm_conv_b[0], ssm_norm_w[0],
                                ln_g.reshape(-1), ln_b.reshape(-1)])
    n_sh = small_sh.shape[0]
    small_all = _allgather_small("gather_small", _pad_to(small_sh, 1024).reshape(-1, 128))

    def halves(w, zero=None):
        w = w if zero is None else w + zero
        return w.astype(BF16).reshape((2, w.shape[0] // 2) + w.shape[1:])

    sh_pool = pool_w[0].astype(BF16)
    sh_pool = sh_pool.reshape((2, NG // 2) + sh_pool.shape[1:])
    started = {"ag0a": _gather_start("ag0a_start", [sh_pool, halves(mlp_w1[0])], after=small_all)}
    z0 = started["ag0a"]["token"][0, 0]
    groups = [("ag0b", [halves(mlp_w2[0], z0)]),
              ("ag0c", [halves(ple_gate_w[0], z0), halves(ple_w[0], z0)]),
              ("ag1a", [halves(ssm_in_w[0], z0)]),
              ("ag1b", [halves(ssm_out_w[0], z0), halves(mlp_w1[1], z0)]),
              ("ag1c", [halves(mlp_w2[1], z0), halves(ple_gate_w[1], z0), halves(ple_w[1], z0)])]

    def gather_end(tag, after):
        return _gather_forward(tag + "_fwd", _gather_wait(tag + "_wait", started[tag], after))

    def gather_land(tag, after):
        lands = _gather_wait(tag + "_wait", started[tag], after)
        return _split_start(tag + "_fwd_start", lands, _forward_copies, 3 * len(lands))

    def gather_done(tag, forwarding, after):
        return _split_wait(tag + "_fwd_wait", forwarding, _forward_copies, after)

    g_pool, g = gather_end("ag0a", [sh for _, shards in groups for sh in shards])
    prev = g_pool
    for tag, shards in groups:
        started[tag] = _gather_start(tag + "_start", shards, after=prev)
        prev = started[tag]["token"]
    all_started = prev

    g_w1, g_w2, g_pw, g_gw = {}, {}, {}, {}

    def set_w1(l, g):
        g_w1[l] = g.reshape(N_CHIPS, 1, D, DF // N_CHIPS)

    def set_w2(l, g2):
        g_w2[l] = g2.reshape(N_CHIPS, 1, DF // N_CHIPS, D)

    def set_gate(l, ggw, gpw):
        g_gw[l] = ggw.reshape(N_CHIPS, 1, D // N_CHIPS, D)
        g_pw[l] = gpw.reshape(N_CHIPS, 1, PD, D // N_CHIPS)

    set_w1(0, g)
    w_pool = jnp.transpose(g_pool.reshape(N_CHIPS, NG, GD // N_CHIPS, GD), (1, 0, 2, 3)).reshape(NG, GD, GD)

    small_all = small_all.reshape(N_DEV, -1)[0::2, :n_sh]
    cdq, niq, dq = CD // N_CHIPS, DI // N_CHIPS, D // N_CHIPS
    o = 0
    conv_w = jnp.concatenate([small_all[q, o:o + CONV_K * cdq].reshape(CONV_K, cdq) for q in range(N_CHIPS)], axis=1)
    o += CONV_K * cdq
    conv_b = small_all[:, o:o + cdq].reshape(1, CD)
    o += cdq
    norm_w = small_all[:, o:o + niq].reshape(1, DI)
    o += niq
    lng = jnp.transpose(small_all[:, o:o + 4 * dq].reshape(N_CHIPS, 2, 2, dq), (1, 2, 0, 3)).reshape(2, 2, 1, D)
    o += 4 * dq
    lnb = jnp.transpose(small_all[:, o:o + 4 * dq].reshape(N_CHIPS, 2, 2, dq), (1, 2, 0, 3)).reshape(2, 2, 1, D)

    pooled = _pool_fwd("pool_fwd", x0)
    hraw = _mm("pool_mm", "nn", pooled, w_pool, T, D, GD, tn=GD, deps=[all_started],
               a_spec=lambda tm, tn, tk: pl.BlockSpec((tm, tk), lambda i, j, k: (i, j)),
               b_spec=lambda tm, tn, tk: pl.BlockSpec((None, tk, tn), lambda i, j, k: (j, 0, 0)))
    x1, x1b, xh1, rs1 = _res_ln("ln00", x0, hraw, lng[0, 0], lnb[0, 0], scale=pool_scale)

    def mlp_fwd(l, xb, land_rest, deps=()):
        a, h2b = _mm(f"mlp{l}_up", "nn", xb, g_w1[l], T, DF, D, tn=min(1024, DF // N_CHIPS), deps=deps,
                     b_spec=_colshard_b(0, DF // N_CHIPS), out_dtype=(F32, BF16),
                     epi=lambda acc: (acc, jnp.square(jnp.maximum(acc, 0.0))))
        land_rest(a)
        h = _mm(f"mlp{l}_down", "nn", h2b, g_w2[l], T, D, DF, k_unit=DF // N_CHIPS,
                b_spec=_rowshard_b(0, DF // N_CHIPS))
        return a, h2b, h

    def ple_fwd(l, xb, pb, deps=()):
        gl = _mm(f"gate{l}_logit", "nn", xb, g_gw[l], T, D, D, k_unit=D // N_CHIPS, deps=deps,
                 b_spec=_rowshard_b(0, D // N_CHIPS))
        e = _mm(f"gate{l}_emb", "nn", pb, g_pw[l], T, D, PD, tn=min(1024, D // N_CHIPS),
                b_spec=_colshard_b(0, D // N_CHIPS))
        return gl, e

    a0, h2b0, h0 = mlp_fwd(0, x1b, lambda a: set_w2(0, *gather_end("ag0b", [a])))
    set_gate(0, *gather_end("ag0c", [h0]))
    x2, x2b, xh2, rs2 = _res_ln("ln01", x1, h0, lng[0, 1], lnb[0, 1])
    gl0, e0 = ple_fwd(0, x2b, p0b)
    x3, x3b, gate0 = _gate_fwd("gate0", x2, gl0, e0)

    g_in, = gather_end("ag1a", [x3b])

    def in_proj_cols(lo, hi):
        parts = [g_in[q].reshape(D, NIN)[:, max(lo - q * NIN, 0):min(hi - q * NIN, NIN)]
                 for q in range(N_CHIPS) if lo < (q + 1) * NIN and hi > q * NIN]
        return parts[0] if len(parts) == 1 else jnp.concatenate(parts, axis=1)

    w_z, w_xbc, w_dt = in_proj_cols(0, DI), in_proj_cols(DI, DI + CD), in_proj_cols(DI + CD, N_CHIPS * NIN)
    z = _mm("ssm_in_z", "nn", x3b, w_z, T, DI, D)
    xbc_pre = _mm("ssm_in_xbc", "nn", x3b, w_xbc, T, CD, D)
    dt_pre = _mm("ssm_in_dt", "nn", x3b, w_dt, T, H, D)
    xbc = _conv_fwd("conv_fwd", xbc_pre, conv_w, conv_b)
    dt = _dt_fwd("dt_fwd", dt_pre, ssm_dt_bias)
    dt_g = jnp.transpose(dt.reshape(T, N_GROUPS, HPG), (1, 0, 2))
    dtc = jnp.pad(dt_g, ((0, 0), (0, 0), (0, HEAD_PAD - HPG)))
    dtr = jnp.pad(jnp.transpose(dt_g, (0, 2, 1)), ((0, 0), (0, ROW_PAD - HPG), (0, 0)))
    al_g = ssm_a_log.reshape(N_GROUPS, HPG)
    alr = jnp.pad(al_g, ((0, 0), (0, HEAD_PAD - HPG)))[:, None, :]
    alc = jnp.pad(al_g, ((0, 0), (0, ROW_PAD - HPG)))[:, :, None]
    d_e = jnp.repeat(ssm_d.reshape(N_GROUPS, HPG), HEAD_DIM, axis=1)[:, None, :]
    fw1b = gather_land("ag1b", [xbc])
    ysc, hprev = _ssd_fwd("ssd_fwd", xbc, dtc, dtr, alr, alc, d_e, DI, deps=[fw1b["token"]])
    g_out, g = gather_done("ag1b", fw1b, [ysc])
    set_w1(1, g)
    w_out = g_out.reshape(DI, D)
    ynb, rsn = _gnorm_fwd("gnorm_fwd", ysc, z, norm_w)
    h1 = _mm("ssm_out", "nn", ynb, w_out, T, D, DI)
    fw1c = gather_land("ag1c", [h1])
    x4, x4b, xh4, rs4 = _res_ln("ln10", x3, h1, lng[1, 0], lnb[1, 0])

    def land_1c(a):
        g2, ggw, gpw = gather_done("ag1c", fw1c, [a])
        set_w2(1, g2)
        set_gate(1, ggw, gpw)

    a1, h2b1, h2 = mlp_fwd(1, x4b, land_1c, deps=[fw1c["token"]])
    x5, x5b, xh5, rs5 = _res_ln("ln11", x4, h2, lng[1, 1], lnb[1, 1])
    gl1, e1 = ple_fwd(1, x5b, p1b)
    dx6, gate1, loss_parts = _gate_loss("gate1_loss", x5, gl1, e1, tgt)
    loss_local = jnp.sum(loss_parts[0::8, 0])

    def ple_bwd(l, dxo, gate, e, xb, pb, deps=()):
        dgl, de = _gate_bwd(f"gate{l}_bwd", dxo, gate, e)
        d_gw = _mm(f"gate{l}_dw", "tn", xb, dgl, D, D, T, out_dtype=BF16, deps=deps).reshape(N_CHIPS, D // N_CHIPS, D)
        d_pw = _mm(f"gate{l}_dpw", "tn", pb, de, PD, D, T, out_dtype=BF16, tn=min(1024, D // N_CHIPS),
                   o_shape=(N_CHIPS, PD, D // N_CHIPS), o_spec=_colshard_o(D // N_CHIPS))
        dx = _mm(f"gate{l}_dx", "nt", dgl, g_gw[l], T, D, D, tn=min(1024, D // N_CHIPS), deps=deps,
                 b_spec=_rowshard_bt(0, D // N_CHIPS), epi=lambda acc, r: acc + r, extras=[dxo])
        return dx, d_gw, d_pw

    def mlp_bwd(l, du, dub, a, h2b, xb):
        d_w2 = _mm(f"mlp{l}_dw2", "tn", h2b, dub, DF, D, T, out_dtype=BF16).reshape(N_CHIPS, DF // N_CHIPS, D)
        da = _mm(f"mlp{l}_da", "nt", dub, g_w2[l], T, DF, D, tn=min(1024, DF // N_CHIPS),
                 b_spec=_rowshard_bt(0, DF // N_CHIPS), out_dtype=BF16,
                 epi=lambda acc, av: acc * (2.0 * jnp.maximum(av, 0.0)), extras=[a])
        d_w1 = _mm(f"mlp{l}_dw1", "tn", xb, da, D, DF, T, out_dtype=BF16, tn=min(1024, DF // N_CHIPS),
                   o_shape=(N_CHIPS, D, DF // N_CHIPS), o_spec=_colshard_o(DF // N_CHIPS))
        dx = _mm(f"mlp{l}_dx", "nt", da, g_w1[l], T, D, DF, k_unit=DF // N_CHIPS,
                 b_spec=_colshard_bt(0, DF // N_CHIPS), epi=lambda acc, r: acc + ALPHA * r, extras=[du])
        return dx, d_w1, d_w2

    dx5, d_gw1, d_pw1 = ple_bwd(1, dx6, gate1, e1, x5b, p1b)
    du5, du5b, dg11, db11 = _ln_bwd("ln11_bwd", dx5, xh5, rs5, lng[1, 1])
    dx4, d_w1_1, d_w2_1 = mlp_bwd(1, du5, du5b, a1, h2b1, x4b)
    parts1a = [d_w1_1, d_w2_1, d_gw1, d_pw1]
    pair1a = _reduce_pair_start("rs1a", parts1a)
    du4, du4b, dg10, db10 = _ln_bwd("ln10_bwd", dx4, xh4, rs4, lng[1, 0])
    d_wout = _mm("ssm_out_dw", "tn", ynb, du4b, DI, D, T, out_dtype=BF16,
                 deps=[pair1a["token"]]).reshape(N_CHIPS, DI // N_CHIPS, D)
    red1a = _reduce_pair_end("rs1a", pos, pair1a, [d_wout])
    dyn = _mm("ssm_out_dx", "nt", du4b, w_out, T, DI, D, deps=[red1a["token"]])
    dysc, dzb, dnorm_w = _gnorm_bwd("gnorm_bwd", dyn, ysc, z, norm_w, rsn)
    dxs, dbm, dcm, ddt_g, dalog_g, dd_g = _ssd_bwd("ssd_bwd", dysc, xbc, dtc, dtr, alr, alc, d_e, hprev, DI)
    dconv, dconv_w, dconv_b = _conv_bwd_a("conv_bwd_a", dxs, dbm, dcm, xbc_pre, conv_w, conv_b)
    dxbcb = _conv_bwd_b("conv_bwd_b", dconv, conv_w)
    ddt = jnp.transpose(ddt_g[:, :, :HPG], (1, 0, 2)).reshape(T, H)
    _, ddtpb, ddt_bias = _dt_bwd("dt_bwd", ddt, dt_pre, ssm_dt_bias)
    da_log = dalog_g[:, 0, :HPG].reshape(1, H)
    dd_skip = dd_g[:, 0, :HPG].reshape(1, H)
    d_wz = _mm("ssm_in_dwz", "tn", x3b, dzb, D, DI, T, out_dtype=BF16)
    d_wx = _mm("ssm_in_dwx", "tn", x3b, dxbcb, D, CD, T, out_dtype=BF16)
    d_wdt = _mm("ssm_in_dwdt", "tn", x3b, ddtpb, D, H, T, out_dtype=BF16)

    def in_proj_shard(q):
        parts = []
        for piece, start in ((d_wz, 0), (d_wx, DI), (d_wdt, DI + CD)):
            lo, hi = max(q * NIN - start, 0), min((q + 1) * NIN - start, piece.shape[1])
            if lo < hi:
                parts.append(piece[:, lo:hi])
        return parts[0] if len(parts) == 1 else jnp.concatenate(parts, axis=1)

    d_win = jnp.stack([in_proj_shard(q) for q in range(N_CHIPS)])
    parts1b = [d_win, d_wout]
    pair1b = _reduce_pair_start("rs1b", parts1b)
    dx3 = _mm("ssm_in_dxz", "nt", dzb, w_z, T, D, DI, epi=lambda acc, r: acc + ALPHA * r, extras=[du4],
              deps=[pair1b["token"]])
    dx3 = _mm("ssm_in_dxx", "nt", dxbcb, w_xbc, T, D, CD, epi=lambda acc, r: acc + r, extras=[dx3])
    dx3 = _mm("ssm_in_dxdt", "nt", ddtpb, w_dt, T, D, H, epi=lambda acc, r: acc + r, extras=[dx3])

    red1b = _reduce_pair_end("rs1b", pos, pair1b, [dx3])

    dx2, d_gw0, d_pw0 = ple_bwd(0, dx3, gate0, e0, x2b, p0b, deps=[red1b["token"]])
    du2, du2b, dg01, db01 = _ln_bwd("ln01_bwd", dx2, xh2, rs2, lng[0, 1])
    dx1, d_w1_0, d_w2_0 = mlp_bwd(0, du2, du2b, a0, h2b0, x1b)
    parts0a = [d_w1_0, d_w2_0, d_gw0, d_pw0]
    pair0a = _reduce_pair_start("rs0a", parts0a)
    du1, dhrb, dg00, db00, dscale = _ln_bwd("ln00_bwd", dx1, xh1, rs1, lng[0, 0], hraw=hraw, scale=pool_scale)
    dln_g = jnp.stack([jnp.stack([dg00, dg01]), jnp.stack([dg10, dg11])]).reshape(-1)
    dln_b = jnp.stack([jnp.stack([db00, db01]), jnp.stack([db10, db11])]).reshape(-1)
    small_g = jnp.concatenate([dscale.reshape(-1), dconv_w.reshape(-1), dconv_b.reshape(-1), ddt_bias.reshape(-1),
                               da_log.reshape(-1), dd_skip.reshape(-1), dnorm_w.reshape(-1), dln_g, dln_b,
                               loss_local.reshape(1)])
    n_sg = small_g.shape[0]
    sg_all = _allgather_small("gather_small_grads", _pad_to(small_g, 1024).reshape(-1, 128),
                              after=[d_w1_0, d_w2_0])
    d_wpool = _mm("pool_dw", "tn", pooled, dhrb, D, GD, T, tm=GD, tn=GD, out_dtype=BF16, deps=[pair0a["token"]],
                  b_spec=lambda tm, tn, tk: pl.BlockSpec((tk, tn), lambda i, j, k: (k, i)),
                  o_shape=(NG, GD, GD), o_spec=lambda tm, tn, tk: pl.BlockSpec((None, tm, tn), lambda i, j, k: (i, 0, 0)))
    d_wpool = jnp.transpose(d_wpool.reshape(NG, N_CHIPS, GD // N_CHIPS, GD), (1, 0, 2, 3)).reshape(N_CHIPS, NG * GD // N_CHIPS, GD)
    red0a = _reduce_pair_end("rs0a", pos, pair0a, [d_wpool, sg_all])
    parts0b = [d_wpool]
    pair0b = _reduce_pair_start("rs0b", parts0b)
    dpooled = _mm("pool_dx", "nt", dhrb, w_pool, T, D, GD, tn=GD, deps=[red0a["token"], pair0b["token"]],
                  a_spec=lambda tm, tn, tk: pl.BlockSpec((tm, tk), lambda i, j, k: (i, j)),
                  b_spec=lambda tm, tn, tk: pl.BlockSpec((None, tn, tk), lambda i, j, k: (j, 0, 0)))
    grad_x = _pool_bwd("pool_bwd", dpooled, du1)
    red0b = _reduce_pair_end("rs0b", pos, pair0b, [grad_x])

    def adam(name, w, m, v, g, layer=0, prev=None):
        w3, m3, v3 = (t.reshape((t.shape[0], -1, t.shape[-1])) for t in (w, m, v))
        return _adamw(name, w3, m3, v3, g, layer, prev)

    sum1a = _reduce_sum_start("rs1a", pos, red1a, [red0b["token"]])
    sum1b = _reduce_sum_start("rs1b", pos, red1b, [sum1a["token"]])
    r_w1_1, r_w2_1, r_gw1, r_pw1 = _reduce_sum_end("rs1a", sum1a, [sum1b["token"]], [p.shape[1:] for p in parts1a])
    stacked = {"mlp_w1": (mlp_w1, m_mlp_w1, v_mlp_w1), "mlp_w2": (mlp_w2, m_mlp_w2, v_mlp_w2),
               "ple_w": (ple_w, m_ple_w, v_ple_w), "ple_gate_w": (ple_gate_w, m_ple_gate_w, v_ple_gate_w)}
    grads1 = {"mlp_w1": r_w1_1, "mlp_w2": r_w2_1, "ple_w": r_pw1, "ple_gate_w": r_gw1}
    upper = {nm: adam(f"adam_{nm}_l1", *stacked[nm], grads1[nm], layer=1) for nm in stacked}
    r_in, r_out = _reduce_sum_end("rs1b", sum1b, [t[1] for t in upper.values()], [p.shape[1:] for p in parts1b])
    big = {"ssm_in_w": [jnp.swapaxes(t, 1, 2) for t in adam(
               "adam_ssm_in_w", *(jnp.swapaxes(t, 1, 2) for t in (ssm_in_w, m_ssm_in_w, v_ssm_in_w)), r_in.T)],
           "ssm_out_w": adam("adam_ssm_out_w", ssm_out_w, m_ssm_out_w, v_ssm_out_w, r_out)}

    sg = _sum_peers("sum_small_grads", sg_all).reshape(-1)[:n_sg]
    o = 0

    def take(nel):
        nonlocal o
        v = sg[o:o + nel]
        o += nel
        return v

    g_scale = take(D).reshape(1, D)
    g_conv_w_full = take(CONV_K * CD).reshape(CONV_K, CD)
    g_conv_b_full = take(CD).reshape(1, CD)
    g_dt_bias = take(H).reshape(1, H)
    g_a_log = take(H).reshape(1, H)
    g_d = take(H).reshape(1, H)
    g_norm_full = take(DI).reshape(1, DI)
    g_lng_full = take(4 * D).reshape(2, 2, D)
    g_lnb_full = take(4 * D).reshape(2, 2, D)
    loss = take(1).reshape(())
    g_conv_w = lax.dynamic_slice_in_dim(g_conv_w_full, chip * cdq, cdq, axis=1)[None]
    g_conv_b = lax.dynamic_slice_in_dim(g_conv_b_full, chip * cdq, cdq, axis=1)
    g_norm = lax.dynamic_slice_in_dim(g_norm_full, chip * niq, niq, axis=1)
    g_lng = lax.dynamic_slice_in_dim(g_lng_full, chip * dq, dq, axis=2)
    g_lnb = lax.dynamic_slice_in_dim(g_lnb_full, chip * dq, dq, axis=2)

    small = [("pool_scale", pool_scale, m_pool_scale, v_pool_scale, g_scale),
             ("ssm_conv_w", ssm_conv_w, m_ssm_conv_w, v_ssm_conv_w, g_conv_w),
             ("ssm_conv_b", ssm_conv_b, m_ssm_conv_b, v_ssm_conv_b, g_conv_b),
             ("ssm_dt_bias", ssm_dt_bias, m_ssm_dt_bias, v_ssm_dt_bias, g_dt_bias),
             ("ssm_a_log", ssm_a_log, m_ssm_a_log, v_ssm_a_log, g_a_log),
             ("ssm_d", ssm_d, m_ssm_d, v_ssm_d, g_d),
             ("ssm_norm_w", ssm_norm_w, m_ssm_norm_w, v_ssm_norm_w, g_norm),
             ("ln_g", ln_g, m_ln_g, v_ln_g, g_lng),
             ("ln_b", ln_b, m_ln_b, v_ln_b, g_lnb)]

    def pack(idx):
        flat = _pad_to(jnp.concatenate([s[idx].reshape(-1) for s in small]), 1024)
        return flat.reshape(1, -1, 128)

    sm_out = _adamw("adam_small", pack(1), pack(2), pack(3), pack(4)[0])
    small_res = {}
    o = 0
    for s in small:
        nel = s[1].size
        small_res[s[0]] = [t.reshape(-1)[o:o + nel].reshape(s[1].shape) for t in sm_out]
        o += nel

    r_w1_0, r_w2_0, r_gw0, r_pw0 = _reduce_end(
        "rs0a", pos, red0a, [big["ssm_in_w"][1], big["ssm_out_w"][1], sm_out[1]] + [upper[nm][1] for nm in upper],
        [p.shape[1:] for p in parts0a])
    r_pool, = _reduce_end("rs0b", pos, red0b, [r_w1_0], [p.shape[1:] for p in parts0b])
    grads0 = {"mlp_w1": r_w1_0, "mlp_w2": r_w2_0, "ple_w": r_pw0, "ple_gate_w": r_gw0}
    big["pool_w"] = adam("adam_pool_w", pool_w, m_pool_w, v_pool_w, r_pool)
    for nm in stacked:
        big[nm] = adam(f"adam_{nm}_l0", *stacked[nm], grads0[nm], layer=0, prev=upper[nm])
    shapes = {"pool_w": pool_w.shape, "ssm_in_w": ssm_in_w.shape, "ssm_out_w": ssm_out_w.shape,
              **{nm: stacked[nm][0].shape for nm in stacked}}
    big = {nm: [t.reshape(shapes[nm]) for t in big[nm]] for nm in big}

    order = ["pool_w", "pool_scale", "ssm_in_w", "ssm_conv_w", "ssm_conv_b", "ssm_dt_bias", "ssm_a_log", "ssm_d",
             "ssm_norm_w", "ssm_out_w", "mlp_w1", "mlp_w2", "ln_g", "ln_b", "ple_w", "ple_gate_w"]
    res = {**big, **small_res}
    outs = [loss, grad_x[None]]
    for kind in range(4):
        outs += [res[nm][kind] for nm in order]
    return tuple(outs)
```

```python
import jax
import jax.numpy as jnp
from jax import lax
from jax.experimental import pallas as pl
from jax.experimental.pallas import tpu as pltpu

F32 = jnp.float32
BF16 = jnp.bfloat16
MESH = pl.DeviceIdType.MESH

DEPTH = 2
ALPHA = (2.0 * DEPTH) ** 0.25
LN_EPS = 1e-5
RMS_EPS = 1e-5
POOL_WINDOWS = (2, 4, 8, 16)
POOL_HALO = 16
HEAD_DIM = 64
HEAD_PAIR = 2 * HEAD_DIM
N_GROUPS = 8
D_STATE = 128
CHUNK = 128
CONV_K = 4
CONV_HALO = 8
HEAD_PAD = 128
ROW_PAD = 8
N_CHIPS = 4
N_DEV = 8
ADAM_LR = 0.001
ADAM_B1 = 0.9
ADAM_B2 = 0.999
ADAM_EPS = 1e-08
ADAM_WD = 0.01
ADAM_STEP = 10
VMEM_LIMIT = 56 * 1024 * 1024
ADAM_BLOCK_BYTES = 1024 * 1024
MM_VMEM_BUDGET = 40 * 1024 * 1024


_ANY = pl.BlockSpec(memory_space=pl.ANY)


def _cp(*sem):
    return pltpu.CompilerParams(dimension_semantics=sem, vmem_limit_bytes=VMEM_LIMIT)


def _pick(dim, pref):
    t = pref
    while t >= 128:
        if dim % t == 0:
            return t
        t //= 2
    return dim


def _rows(rows, row_bytes, budget):
    t = rows
    while t * row_bytes > budget and t % 16 == 0:
        t //= 2
    return t


def _sigmoid(v):
    return 1.0 / (1.0 + jnp.exp(-v))


_DIMS = {"nn": (((1,), (0,)), ((), ())), "nt": (((1,), (1,)), ((), ())), "tn": (((0,), (0,)), ((), ()))}


def _pick_k(k_unit, fixed_bytes, per_k_bytes):
    for n in range(1, k_unit // 128 + 1):
        if k_unit % n == 0 and (n == 1 or (k_unit // n) % 128 == 0):
            if fixed_bytes + (k_unit // n) * per_k_bytes <= MM_VMEM_BUDGET:
                return k_unit // n
    return min(k_unit, 128)


def _mm(name, form, a, b, M, N, K, *, tm=1024, tn=1024, k_unit=None, a_spec=None, b_spec=None,
        o_shape=None, o_spec=None, out_dtype=F32, pro=None, epi=None, extras=(), deps=()):
    tm, tn = _pick(M, tm), _pick(N, tn)
    out_dtypes = out_dtype if isinstance(out_dtype, tuple) else (out_dtype,)
    n_out = len(out_dtypes)
    in_place = n_out == 1 and out_dtypes[0] == F32
    fixed = 2 * tm * tn * (sum(jnp.dtype(d).itemsize for d in out_dtypes) + 4 * len(extras))
    fixed += 0 if in_place else 4 * tm * tn
    tk = _pick_k(K if k_unit is None else k_unit, fixed,
                 2 * (tm * a.dtype.itemsize + tn * b.dtype.itemsize))
    nk = K // tk
    if a_spec is None:
        a_spec = (pl.BlockSpec((tk, tm), lambda i, j, k: (k, i)) if form == "tn"
                  else pl.BlockSpec((tm, tk), lambda i, j, k: (i, k)))
    else:
        a_spec = a_spec(tm, tn, tk)
    if b_spec is None:
        b_spec = (pl.BlockSpec((tn, tk), lambda i, j, k: (j, k)) if form == "nt"
                  else pl.BlockSpec((tk, tn), lambda i, j, k: (k, j)))
    else:
        b_spec = b_spec(tm, tn, tk)
    if o_spec is None:
        o_spec = pl.BlockSpec((tm, tn), lambda i, j, k: (i, j))
        o_shape = (M, N)
    else:
        o_spec = o_spec(tm, tn, tk)
    ex_arrays = [e for e in extras]
    ex_specs = [pl.BlockSpec((tm, tn), lambda i, j, k: (i, j)) for _ in extras]
    ne = len(ex_arrays)
    nd = len(deps)
    dep_specs = [pl.BlockSpec((8, 128), lambda i, j, k: (0, 0)) for _ in deps]
    dims = _DIMS[form]
    use_scratch = nk > 1 and not in_place

    def body(a_ref, b_ref, *rest):
        ex_refs = rest[:ne]
        o_refs = rest[ne + nd:ne + nd + n_out]
        at = a_ref[...]
        if pro is not None:
            at = pro(at)
        p = lax.dot_general(at.astype(BF16), b_ref[...].astype(BF16), dims, preferred_element_type=F32)

        def finish(acc):
            res = acc if epi is None else epi(acc, *[r[...] for r in ex_refs])
            res = res if isinstance(res, tuple) else (res,)
            for o_ref, r, d in zip(o_refs, res, out_dtypes):
                o_ref[...] = r.astype(d)

        if nk == 1:
            finish(p)
        else:
            acc_ref = rest[ne + nd + n_out] if use_scratch else o_refs[0]
            k = pl.program_id(2)

            @pl.when(k == 0)
            def _():
                acc_ref[...] = p

            @pl.when(jnp.logical_and(k > 0, k < nk - 1))
            def _():
                acc_ref[...] += p

            @pl.when(k == nk - 1)
            def _():
                finish(acc_ref[...] + p)

    res = pl.pallas_call(
        body, name=name, grid=(M // tm, N // tn, nk),
        in_specs=[a_spec, b_spec] + ex_specs + dep_specs, out_specs=[o_spec] * n_out,
        out_shape=[jax.ShapeDtypeStruct(o_shape, d) for d in out_dtypes],
        scratch_shapes=[pltpu.VMEM((tm, tn), F32)] if use_scratch else [],
        compiler_params=_cp("parallel", "parallel", "arbitrary"),
    )(a, b, *ex_arrays, *deps)
    return res if n_out > 1 else res[0]


def _colshard_b(l, n_per):
    def make(tm, tn, tk):
        nb = n_per // tn
        return pl.BlockSpec((None, None, tk, tn), lambda i, j, k: (j // nb, l, k, j % nb))
    return make


def _colshard_bt(l, n_per):
    def make(tm, tn, tk):
        nb = n_per // tk
        return pl.BlockSpec((None, None, tn, tk), lambda i, j, k: (k // nb, l, j, k % nb))
    return make


def _colshard_o(n_per):
    def make(tm, tn, tk):
        nb = n_per // tn
        return pl.BlockSpec((None, tm, tn), lambda i, j, k: (j // nb, i, j % nb))
    return make


def _rowshard_b(l, k_per):
    def make(tm, tn, tk):
        nb = k_per // tk
        return pl.BlockSpec((None, None, tk, tn), lambda i, j, k: (k // nb, l, k % nb, j))
    return make


def _rowshard_bt(l, k_per):
    def make(tm, tn, tk):
        nb = k_per // tn
        return pl.BlockSpec((None, None, tn, tk), lambda i, j, k: (j // nb, l, j % nb, k))
    return make


def _res_ln(name, xprev, h, g, b, scale=None):
    T, D = xprev.shape
    tr = _pick(T, 256)
    row = pl.BlockSpec((tr, D), lambda i: (i, 0))
    vec = pl.BlockSpec((1, D), lambda i: (0, 0))
    has_scale = scale is not None

    def body(*refs):
        if has_scale:
            x_ref, h_ref, s_ref, g_ref, b_ref, y_ref, yb_ref, xh_ref, rs_ref = refs
            hh = h_ref[...] * s_ref[...]
        else:
            x_ref, h_ref, g_ref, b_ref, y_ref, yb_ref, xh_ref, rs_ref = refs
            hh = h_ref[...]
        u = ALPHA * x_ref[...] + hh
        mu = jnp.mean(u, axis=-1, keepdims=True)
        d = u - mu
        var = jnp.mean(d * d, axis=-1, keepdims=True)
        rs = lax.rsqrt(var + LN_EPS)
        xh = d * rs
        y = xh * g_ref[...] + b_ref[...]
        y_ref[...] = y
        yb_ref[...] = y.astype(BF16)
        xh_ref[...] = xh
        rs_ref[...] = rs

    ins = [xprev, h] + ([scale] if has_scale else []) + [g, b]
    specs = [row, row] + ([vec] if has_scale else []) + [vec, vec]
    return pl.pallas_call(
        body, name=name, grid=(T // tr,), in_specs=specs,
        out_specs=[row, row, row, pl.BlockSpec((tr, 1), lambda i: (i, 0))],
        out_shape=[jax.ShapeDtypeStruct((T, D), F32), jax.ShapeDtypeStruct((T, D), BF16),
                   jax.ShapeDtypeStruct((T, D), F32), jax.ShapeDtypeStruct((T, 1), F32)],
        compiler_params=_cp("parallel"),
    )(*ins)


def _accum(ref, part, first):
    @pl.when(first)
    def _():
        ref[...] = part

    @pl.when(jnp.logical_not(first))
    def _():
        ref[...] += part


def _ln_bwd(name, dy, xh, rs, g, hraw=None, scale=None):
    T, D = dy.shape
    tr = _pick(T, 256)
    row = pl.BlockSpec((tr, D), lambda i: (i, 0))
    vec = pl.BlockSpec((1, D), lambda i: (0, 0))
    has_scale = scale is not None

    def body(*refs):
        if has_scale:
            dy_ref, xh_ref, rs_ref, g_ref, hr_ref, s_ref, du_ref, dub_ref, dg_ref, db_ref, ds_ref = refs
        else:
            dy_ref, xh_ref, rs_ref, g_ref, du_ref, dub_ref, dg_ref, db_ref = refs
        first = pl.program_id(0) == 0
        dyv = dy_ref[...]
        xhv = xh_ref[...]
        dxh = dyv * g_ref[...]
        m1 = jnp.mean(dxh, axis=-1, keepdims=True)
        m2 = jnp.mean(dxh * xhv, axis=-1, keepdims=True)
        du = rs_ref[...] * (dxh - m1 - xhv * m2)
        du_ref[...] = du
        if has_scale:
            dub_ref[...] = (du * s_ref[...]).astype(BF16)
            _accum(ds_ref, jnp.sum(du * hr_ref[...], axis=0, keepdims=True), first)
        else:
            dub_ref[...] = du.astype(BF16)
        _accum(dg_ref, jnp.sum(dyv * xhv, axis=0, keepdims=True), first)
        _accum(db_ref, jnp.sum(dyv, axis=0, keepdims=True), first)

    ins = [dy, xh, rs, g] + ([hraw, scale] if has_scale else [])
    specs = [row, row, pl.BlockSpec((tr, 1), lambda i: (i, 0)), vec] + ([row, vec] if has_scale else [])
    n_vec = 3 if has_scale else 2
    return pl.pallas_call(
        body, name=name, grid=(T // tr,), in_specs=specs,
        out_specs=[row, row] + [vec] * n_vec,
        out_shape=[jax.ShapeDtypeStruct((T, D), F32), jax.ShapeDtypeStruct((T, D), BF16)]
        + [jax.ShapeDtypeStruct((1, D), F32)] * n_vec,
        compiler_params=_cp("arbitrary"),
    )(*ins)


def _gate_fwd(name, x, gl, e):
    T, D = x.shape
    tr = _pick(T, 256)
    row = pl.BlockSpec((tr, D), lambda i: (i, 0))

    def body(x_ref, gl_ref, e_ref, xo_ref, xob_ref, gate_ref):
        gate = _sigmoid(gl_ref[...])
        xo = x_ref[...] + gate * e_ref[...]
        xo_ref[...] = xo
        xob_ref[...] = xo.astype(BF16)
        gate_ref[...] = gate

    return pl.pallas_call(
        body, name=name, grid=(T // tr,), in_specs=[row, row, row], out_specs=[row, row, row],
        out_shape=[jax.ShapeDtypeStruct((T, D), F32), jax.ShapeDtypeStruct((T, D), BF16),
                   jax.ShapeDtypeStruct((T, D), F32)],
        compiler_params=_cp("parallel"),
    )(x, gl, e)


def _gate_loss(name, x, gl, e, tgt):
    T, D = x.shape
    tr = _pick(T, 256)
    row = pl.BlockSpec((tr, D), lambda i: (i, 0))

    def body(x_ref, gl_ref, e_ref, t_ref, dy_ref, gate_ref, lp_ref):
        gate = _sigmoid(gl_ref[...])
        err = x_ref[...] + gate * e_ref[...] - t_ref[...]
        dy_ref[...] = err * (1.0 / D)
        gate_ref[...] = gate
        s = jnp.sum(jnp.mean(err * err, axis=-1, keepdims=True), axis=0, keepdims=True)
        lp_ref[...] = jnp.broadcast_to(0.5 * s, (8, 128))

    return pl.pallas_call(
        body, name=name, grid=(T // tr,), in_specs=[row] * 4,
        out_specs=[row, row, pl.BlockSpec((8, 128), lambda i: (i, 0))],
        out_shape=[jax.ShapeDtypeStruct((T, D), F32), jax.ShapeDtypeStruct((T, D), F32),
                   jax.ShapeDtypeStruct((T // tr * 8, 128), F32)],
        compiler_params=_cp("parallel"),
    )(x, gl, e, tgt)


def _gate_bwd(name, dxo, gate, e):
    T, D = dxo.shape
    tr = _pick(T, 256)
    row = pl.BlockSpec((tr, D), lambda i: (i, 0))

    def body(d_ref, gate_ref, e_ref, dgl_ref, de_ref):
        d = d_ref[...]
        gate = gate_ref[...]
        dgl_ref[...] = (d * e_ref[...] * gate * (1.0 - gate)).astype(BF16)
        de_ref[...] = (d * gate).astype(BF16)

    return pl.pallas_call(
        body, name=name, grid=(T // tr,), in_specs=[row] * 3, out_specs=[row, row],
        out_shape=[jax.ShapeDtypeStruct((T, D), BF16)] * 2,
        compiler_params=_cp("parallel"),
    )(dxo, gate, e)


def _gnorm_fwd(name, y, z, w):
    T, DI = y.shape
    tr = _pick(T, 128)
    row = pl.BlockSpec((tr, DI), lambda i: (i, 0))
    vec = pl.BlockSpec((1, DI), lambda i: (0, 0))
    col = pl.BlockSpec((tr, 1), lambda i: (i, 0))

    def body(y_ref, z_ref, w_ref, o_ref, rs_ref):
        zv = z_ref[...]
        yz = y_ref[...] * (zv * _sigmoid(zv))
        rs = lax.rsqrt(jnp.mean(yz * yz, axis=-1, keepdims=True) + RMS_EPS)
        o_ref[...] = (yz * rs * w_ref[...]).astype(BF16)
        rs_ref[...] = rs

    return pl.pallas_call(
        body, name=name, grid=(T // tr,), in_specs=[row, row, vec], out_specs=[row, col],
        out_shape=[jax.ShapeDtypeStruct((T, DI), BF16), jax.ShapeDtypeStruct((T, 1), F32)],
        compiler_params=_cp("parallel"),
    )(y, z, w)


def _gnorm_bwd(name, dyn, y, z, w, rs):
    T, DI = y.shape
    tr = _pick(T, 128)
    row = pl.BlockSpec((tr, DI), lambda i: (i, 0))
    vec = pl.BlockSpec((1, DI), lambda i: (0, 0))
    col = pl.BlockSpec((tr, 1), lambda i: (i, 0))

    def body(d_ref, y_ref, z_ref, w_ref, rs_ref, dy_ref, dz_ref, dw_ref):
        first = pl.program_id(0) == 0
        zv = z_ref[...]
        yv = y_ref[...]
        sg = _sigmoid(zv)
        sz = zv * sg
        rsv = rs_ref[...]
        yzh = yv * sz * rsv
        dv = d_ref[...]
        gw = dv * w_ref[...]
        m = jnp.mean(gw * yzh, axis=-1, keepdims=True)
        dyz = rsv * (gw - yzh * m)
        dy_ref[...] = dyz * sz
        dz_ref[...] = (dyz * yv * (sg * (1.0 + zv * (1.0 - sg)))).astype(BF16)
        _accum(dw_ref, jnp.sum(dv * yzh, axis=0, keepdims=True), first)

    return pl.pallas_call(
        body, name=name, grid=(T // tr,), in_specs=[row, row, row, vec, col], out_specs=[row, row, vec],
        out_shape=[jax.ShapeDtypeStruct((T, DI), F32), jax.ShapeDtypeStruct((T, DI), BF16),
                   jax.ShapeDtypeStruct((1, DI), F32)],
        compiler_params=_cp("arbitrary"),
    )(dyn, y, z, w, rs)


def _sel4(j, vals):
    return jnp.where(j == 0, vals[0], jnp.where(j == 1, vals[1], jnp.where(j == 2, vals[2], vals[3])))


def _pool_cnt(i, j, tr, rows, offset):
    t = i * tr + offset + lax.broadcasted_iota(jnp.int32, (rows, 1), 0)
    win = _sel4(j, POOL_WINDOWS)
    return jnp.minimum(t + 1, win).astype(F32)


def _pool_fwd(name, x):
    T, D = x.shape
    gd = D // len(POOL_WINDOWS)
    tr = _pick(T, 512)
    hb = tr // POOL_HALO

    def body(x_ref, h_ref, o_ref):
        i, j = pl.program_id(0), pl.program_id(1)
        xv = x_ref[...]
        halo = jnp.where(i > 0, h_ref[...], 0.0)
        cat = jnp.concatenate([halo, xv], axis=0)
        s2 = cat + pltpu.roll(cat, 1, 0)
        s4 = s2 + pltpu.roll(s2, 2, 0)
        s8 = s4 + pltpu.roll(s4, 4, 0)
        s16 = s8 + pltpu.roll(s8, 8, 0)
        sel = _sel4(j, (s2, s4, s8, s16))[POOL_HALO:]
        o_ref[...] = (sel / _pool_cnt(i, j, tr, tr, 0) - xv).astype(BF16)

    return pl.pallas_call(
        body, name=name, grid=(T // tr, len(POOL_WINDOWS)),
        in_specs=[pl.BlockSpec((tr, gd), lambda i, j: (i, j)),
                  pl.BlockSpec((POOL_HALO, gd), lambda i, j: (jnp.maximum(i * hb - 1, 0), j))],
        out_specs=pl.BlockSpec((tr, gd), lambda i, j: (i, j)),
        out_shape=jax.ShapeDtypeStruct((T, D), BF16),
        compiler_params=_cp("parallel", "parallel"),
    )(x, x)


def _pool_bwd(name, dp, du):
    T, D = dp.shape
    gd = D // len(POOL_WINDOWS)
    tr = _pick(T, 512)
    hb = tr // POOL_HALO
    last_h = T // POOL_HALO - 1
    n = tr + POOL_HALO

    def body(dp_ref, h_ref, du_ref, o_ref):
        i, j = pl.program_id(0), pl.program_id(1)
        dpv = dp_ref[...]
        q = dpv / _pool_cnt(i, j, tr, tr, 0)
        qh = jnp.where(i < pl.num_programs(0) - 1, h_ref[...] / _pool_cnt(i, j, tr, POOL_HALO, tr), 0.0)
        cat = jnp.concatenate([q, qh], axis=0)
        f2 = cat + pltpu.roll(cat, n - 1, 0)
        f4 = f2 + pltpu.roll(f2, n - 2, 0)
        f8 = f4 + pltpu.roll(f4, n - 4, 0)
        f16 = f8 + pltpu.roll(f8, n - 8, 0)
        sel = _sel4(j, (f2, f4, f8, f16))[:tr]
        o_ref[...] = ALPHA * du_ref[...] + sel - dpv

    return pl.pallas_call(
        body, name=name, grid=(T // tr, len(POOL_WINDOWS)),
        in_specs=[pl.BlockSpec((tr, gd), lambda i, j: (i, j)),
                  pl.BlockSpec((POOL_HALO, gd), lambda i, j: (jnp.minimum((i + 1) * hb, last_h), j)),
                  pl.BlockSpec((tr, gd), lambda i, j: (i, j))],
        out_specs=pl.BlockSpec((tr, gd), lambda i, j: (i, j)),
        out_shape=jax.ShapeDtypeStruct((T, D), F32),
        compiler_params=_cp("parallel", "parallel"),
    )(dp, dp, du)


def _conv_taps(cat, wv, rows):
    shifted = [cat[CONV_HALO:] if s == 0 else pltpu.roll(cat, s, 0)[CONV_HALO:] for s in range(CONV_K)]
    acc = shifted[0] * wv[CONV_K - 1:CONV_K]
    for s in range(1, CONV_K):
        acc = acc + shifted[s] * wv[CONV_K - 1 - s:CONV_K - s]
    return acc, shifted


def _conv_fwd(name, xp, w, b):
    T, CD = xp.shape
    tr, tc = _pick(T, 512), _pick(CD, 512)
    hb = tr // CONV_HALO

    def body(x_ref, h_ref, w_ref, b_ref, o_ref):
        i = pl.program_id(0)
        halo = jnp.where(i > 0, h_ref[...], 0.0)
        cat = jnp.concatenate([halo, x_ref[...]], axis=0)
        acc, _ = _conv_taps(cat, w_ref[...], tr)
        acc = acc + b_ref[...]
        o_ref[...] = acc * _sigmoid(acc)

    return pl.pallas_call(
        body, name=name, grid=(T // tr, CD // tc),
        in_specs=[pl.BlockSpec((tr, tc), lambda i, j: (i, j)),
                  pl.BlockSpec((CONV_HALO, tc), lambda i, j: (jnp.maximum(i * hb - 1, 0), j)),
                  pl.BlockSpec((CONV_K, tc), lambda i, j: (0, j)),
                  pl.BlockSpec((1, tc), lambda i, j: (0, j))],
        out_specs=pl.BlockSpec((tr, tc), lambda i, j: (i, j)),
        out_shape=jax.ShapeDtypeStruct((T, CD), F32),
        compiler_params=_cp("parallel", "parallel"),
    )(xp, xp, w, b)


def _conv_bwd_a(name, dxs, dbm, dcm, xp, w, b):
    T, CD = xp.shape
    tr = _pick(T, 512)
    tc = _pick(dbm.shape[1], 512)
    hb = tr // CONV_HALO
    nx, nb = dxs.shape[1] // tc, dbm.shape[1] // tc

    def part_spec(lo, n):
        def imap(j, i):
            inside = jnp.logical_and(j >= lo, j < lo + n)
            return (jnp.where(inside, i, 0), jnp.clip(j - lo, 0, n - 1))
        return pl.BlockSpec((tr, tc), imap)

    def body(dx_ref, db_ref, dc_ref, x_ref, h_ref, w_ref, b_ref, o_ref, dw_ref, dbias_ref):
        j, i = pl.program_id(0), pl.program_id(1)
        first = i == 0
        d = jnp.where(j < nx, dx_ref[...], jnp.where(j < nx + nb, db_ref[...], dc_ref[...]))
        halo = jnp.where(i > 0, h_ref[...], 0.0)
        cat = jnp.concatenate([halo, x_ref[...]], axis=0)
        acc, shifted = _conv_taps(cat, w_ref[...], tr)
        acc = acc + b_ref[...]
        sg = _sigmoid(acc)
        dconv = d * (sg * (1.0 + acc * (1.0 - sg)))
        o_ref[...] = dconv
        _accum(dbias_ref, jnp.sum(dconv, axis=0, keepdims=True), first)
        tap = lax.broadcasted_iota(jnp.int32, (CONV_K, tc), 0)
        dwv = jnp.zeros((CONV_K, tc), F32)
        for s in range(CONV_K):
            dwv = jnp.where(tap == CONV_K - 1 - s, jnp.sum(dconv * shifted[s], axis=0, keepdims=True), dwv)
        _accum(dw_ref, dwv, first)

    return pl.pallas_call(
        body, name=name, grid=(CD // tc, T // tr),
        in_specs=[part_spec(0, nx), part_spec(nx, nb), part_spec(nx + nb, nb),
                  pl.BlockSpec((tr, tc), lambda j, i: (i, j)),
                  pl.BlockSpec((CONV_HALO, tc), lambda j, i: (jnp.maximum(i * hb - 1, 0), j)),
                  pl.BlockSpec((CONV_K, tc), lambda j, i: (0, j)),
                  pl.BlockSpec((1, tc), lambda j, i: (0, j))],
        out_specs=[pl.BlockSpec((tr, tc), lambda j, i: (i, j)),
                   pl.BlockSpec((CONV_K, tc), lambda j, i: (0, j)),
                   pl.BlockSpec((1, tc), lambda j, i: (0, j))],
        out_shape=[jax.ShapeDtypeStruct((T, CD), F32), jax.ShapeDtypeStruct((CONV_K, CD), F32),
                   jax.ShapeDtypeStruct((1, CD), F32)],
        compiler_params=_cp("parallel", "arbitrary"),
    )(dxs, dbm, dcm, xp, xp, w, b)


def _conv_bwd_b(name, dconv, w):
    T, CD = dconv.shape
    tr, tc = _pick(T, 512), _pick(CD, 512)
    hb = tr // CONV_HALO
    last_h = T // CONV_HALO - 1
    n = tr + CONV_HALO

    def body(d_ref, h_ref, w_ref, o_ref):
        i = pl.program_id(0)
        halo = jnp.where(i < pl.num_programs(0) - 1, h_ref[...], 0.0)
        cat = jnp.concatenate([d_ref[...], halo], axis=0)
        wv = w_ref[...]
        acc = cat[:tr] * wv[CONV_K - 1:CONV_K]
        for s in range(1, CONV_K):
            acc = acc + pltpu.roll(cat, n - s, 0)[:tr] * wv[CONV_K - 1 - s:CONV_K - s]
        o_ref[...] = acc.astype(BF16)

    return pl.pallas_call(
        body, name=name, grid=(T // tr, CD // tc),
        in_specs=[pl.BlockSpec((tr, tc), lambda i, j: (i, j)),
                  pl.BlockSpec((CONV_HALO, tc), lambda i, j: (jnp.minimum((i + 1) * hb, last_h), j)),
                  pl.BlockSpec((CONV_K, tc), lambda i, j: (0, j))],
        out_specs=pl.BlockSpec((tr, tc), lambda i, j: (i, j)),
        out_shape=jax.ShapeDtypeStruct((T, CD), BF16),
        compiler_params=_cp("parallel", "parallel"),
    )(dconv, dconv, w)


def _dt_fwd(name, dtp, bias):
    T, H = dtp.shape

    def body(x_ref, b_ref, o_ref):
        v = x_ref[...] + b_ref[...]
        u = jnp.exp(-jnp.abs(v))
        w1 = 1.0 + u
        lp = jnp.where(w1 == 1.0, u, jnp.log(w1) * (u / jnp.where(w1 == 1.0, 1.0, w1 - 1.0)))
        o_ref[...] = jnp.maximum(v, 0.0) + lp

    return pl.pallas_call(body, name=name, out_shape=jax.ShapeDtypeStruct((T, H), F32))(dtp, bias)


def _dt_bwd(name, ddt, dtp, bias):
    T, H = dtp.shape

    def body(d_ref, x_ref, b_ref, o_ref, ob_ref, db_ref):
        g = d_ref[...] * _sigmoid(x_ref[...] + b_ref[...])
        o_ref[...] = g
        ob_ref[...] = g.astype(BF16)
        db_ref[...] = jnp.sum(g, axis=0, keepdims=True)

    return pl.pallas_call(
        body, name=name,
        out_shape=[jax.ShapeDtypeStruct((T, H), F32), jax.ShapeDtypeStruct((T, H), BF16),
                   jax.ShapeDtypeStruct((1, H), F32)],
    )(ddt, dtp, bias)


def _split(v):
    hi = v.astype(BF16)
    return hi, (v - hi.astype(F32)).astype(BF16)


def _dot01(form, a, b, mask):
    if mask == "b":
        hi, lo = _split(a)
        mb = b.astype(BF16)
        return (lax.dot_general(hi, mb, _DIMS[form], preferred_element_type=F32)
                + lax.dot_general(lo, mb, _DIMS[form], preferred_element_type=F32))
    hi, lo = _split(b)
    ma = a.astype(BF16)
    return (lax.dot_general(ma, hi, _DIMS[form], preferred_element_type=F32)
            + lax.dot_general(ma, lo, _DIMS[form], preferred_element_type=F32))


def _dotb(form, a, b):
    return lax.dot_general(a.astype(BF16), b.astype(BF16), _DIMS[form], preferred_element_type=F32)


def _ssd_common(dtc, dtr, alr, alc, gw):
    li = lax.broadcasted_iota(jnp.int32, (CHUNK, CHUNK), 0)
    si = lax.broadcasted_iota(jnp.int32, (CHUNK, CHUNK), 1)
    tri = (li >= si).astype(F32)
    trit = (li <= si).astype(F32)
    a_row = -jnp.exp(alr)
    a_col = -jnp.exp(alc)
    acs_c = _dot01("nn", tri, dtc * a_row, "a")
    acs_r = _dot01("nn", dtr * a_col, trit, "b")
    eh = lax.broadcasted_iota(jnp.int32, (HEAD_PAD, gw), 0)
    ec = lax.broadcasted_iota(jnp.int32, (HEAD_PAD, gw), 1) // HEAD_DIM
    expand = (eh == ec).astype(F32)
    th = lax.broadcasted_iota(jnp.int32, (gw, HEAD_PAD), 1)
    tc = lax.broadcasted_iota(jnp.int32, (gw, HEAD_PAD), 0) // HEAD_DIM
    reduce_ = (th == tc).astype(F32)
    acs_last = acs_c[CHUNK - 1:CHUNK, :]
    both = _dot01("nn", jnp.concatenate([dtc, acs_c], axis=0), expand, "b")
    acs_e = both[CHUNK:]
    return dict(li=li, si=si, tri=tri, trit=trit, a_row=a_row, acs_c=acs_c, acs_r=acs_r,
                reduce=reduce_, acs_last=acs_last, dt_e=both[:CHUNK],
                eacs_e=jnp.exp(acs_e), dec_e=jnp.exp(acs_e[CHUNK - 1:CHUNK, :] - acs_e),
                dec_h=jnp.exp(acs_last - acs_c))


def _ssd_specs(T, DI, gw, nc, rev):
    nsb = DI // D_STATE
    cidx = (lambda c: nc - 1 - c) if rev else (lambda c: c)
    return dict(
        xs=pl.BlockSpec((CHUNK, gw), lambda g, c: (cidx(c), g)),
        bm=pl.BlockSpec((CHUNK, D_STATE), lambda g, c: (cidx(c), nsb + g)),
        cm=pl.BlockSpec((CHUNK, D_STATE), lambda g, c: (cidx(c), nsb + N_GROUPS + g)),
        dtc=pl.BlockSpec((None, CHUNK, HEAD_PAD), lambda g, c: (g, cidx(c), 0)),
        dtr=pl.BlockSpec((None, ROW_PAD, CHUNK), lambda g, c: (g, 0, cidx(c))),
        alr=pl.BlockSpec((None, 1, HEAD_PAD), lambda g, c: (g, 0, 0)),
        alc=pl.BlockSpec((None, ROW_PAD, 1), lambda g, c: (g, 0, 0)),
        de=pl.BlockSpec((None, 1, gw), lambda g, c: (g, 0, 0)),
        hp=pl.BlockSpec((None, None, gw, D_STATE), lambda g, c: (cidx(c), g, 0, 0)),
        bc=pl.BlockSpec((CHUNK, D_STATE), lambda g, c: (cidx(c), g)),
        acc=pl.BlockSpec((None, 1, HEAD_PAD), lambda g, c: (g, 0, 0)),
    )


def _ssd_fwd(name, xbc, dtc, dtr, alr, alc, d_e, DI, deps=()):
    T = xbc.shape[0]
    nc = T // CHUNK
    gw = DI // N_GROUPS
    sp = _ssd_specs(T, DI, gw, nc, False)

    def body(xs_ref, b_ref, c_ref, dtc_ref, dtr_ref, alr_ref, alc_ref, de_ref, *rest):
        y_ref, hp_ref, h_scr = rest[len(deps):]

        @pl.when(pl.program_id(1) == 0)
        def _():
            h_scr[...] = jnp.zeros_like(h_scr)

        hpv = h_scr[...]
        hp_ref[...] = hpv
        xs = xs_ref[...]
        bb = b_ref[...].astype(BF16)
        cb_ = c_ref[...].astype(BF16)
        cm = _ssd_common(dtc_ref[...], dtr_ref[...], alr_ref[...], alc_ref[...], gw)
        x = xs * cm["dt_e"]
        xb = x.astype(BF16)
        cbm = _dotb("nt", cb_, bb)
        causal = cm["li"] >= cm["si"]
        second = lax.broadcasted_iota(jnp.int32, (1, HEAD_PAIR), 1) >= HEAD_DIM
        pieces = []
        for pr in range(gw // HEAD_PAIR):
            xp = xb[:, pr * HEAD_PAIR:(pr + 1) * HEAD_PAIR]
            for h2 in range(2):
                j = 2 * pr + h2
                seg = cm["acs_c"][:, j:j + 1] - cm["acs_r"][j:j + 1, :]
                lmat = jnp.exp(jnp.where(causal, seg, -1e30))
                yj = _dotb("nn", cbm * lmat, xp)
                yp = yj if h2 == 0 else jnp.where(second, yj, yp)
            pieces.append(yp)
        ydiag = pieces[0] if len(pieces) == 1 else jnp.concatenate(pieces, axis=1)
        states = _dotb("tn", x * cm["dec_e"], bb)
        yoff = _dotb("nt", cb_, hpv) * cm["eacs_e"]
        y_ref[...] = ydiag + yoff + xs * de_ref[...]
        cd_rows = jnp.sum(cm["reduce"] * jnp.exp(cm["acs_last"]), axis=1, keepdims=True)
        h_scr[...] = cd_rows * hpv + states

    return pl.pallas_call(
        body, name=name, grid=(N_GROUPS, nc),
        in_specs=[sp["xs"], sp["bm"], sp["cm"], sp["dtc"], sp["dtr"], sp["alr"], sp["alc"], sp["de"]]
        + [pl.BlockSpec((8, 128), lambda g, c: (0, 0)) for _ in deps],
        out_specs=[sp["xs"], sp["hp"]],
        out_shape=[jax.ShapeDtypeStruct((T, DI), F32), jax.ShapeDtypeStruct((nc, N_GROUPS, gw, D_STATE), F32)],
        scratch_shapes=[pltpu.VMEM((gw, D_STATE), F32)],
        compiler_params=_cp("parallel", "arbitrary"),
    )(xbc, xbc, xbc, dtc, dtr, alr, alc, d_e, *deps)


def _ssd_bwd(name, dy, xbc, dtc, dtr, alr, alc, d_e, hprev, DI):
    T = xbc.shape[0]
    nc = T // CHUNK
    gw = DI // N_GROUPS
    sp = _ssd_specs(T, DI, gw, nc, True)

    def body(dy_ref, xs_ref, b_ref, c_ref, dtc_ref, dtr_ref, alr_ref, alc_ref, de_ref, hp_ref,
             dxs_ref, db_ref, dc_ref, ddt_ref, dal_ref, dd_ref, dh_scr):
        first = pl.program_id(1) == 0

        @pl.when(first)
        def _():
            dh_scr[...] = jnp.zeros_like(dh_scr)

        xs = xs_ref[...]
        dyv = dy_ref[...]
        bb = b_ref[...].astype(BF16)
        cb_ = c_ref[...].astype(BF16)
        dtc_v = dtc_ref[...]
        cm = _ssd_common(dtc_v, dtr_ref[...], alr_ref[...], alc_ref[...], gw)
        hpv = hp_ref[...]
        hpb = hpv.astype(BF16)
        dhn = dh_scr[...]
        dhnb = dhn.astype(BF16)
        x = xs * cm["dt_e"]
        xb = x.astype(BF16)
        cbm = _dotb("nt", cb_, bb)
        causal = cm["li"] >= cm["si"]
        second = lax.broadcasted_iota(jnp.int32, (1, HEAD_PAIR), 1) >= HEAD_DIM
        lane_pad = lax.broadcasted_iota(jnp.int32, (1, HEAD_PAD), 1)
        sub_pad = lax.broadcasted_iota(jnp.int32, (ROW_PAD, 1), 0)

        dxs = dyv * de_ref[...]

        dcb = jnp.zeros((CHUNK, CHUNK), F32)
        dacs_c = jnp.zeros((CHUNK, HEAD_PAD), F32)
        dacs_r = jnp.zeros((ROW_PAD, CHUNK), F32)
        pieces = []
        for pr in range(gw // HEAD_PAIR):
            xp = xb[:, pr * HEAD_PAIR:(pr + 1) * HEAD_PAIR]
            dyp = dyv[:, pr * HEAD_PAIR:(pr + 1) * HEAD_PAIR]
            for h2 in range(2):
                j = 2 * pr + h2
                seg = cm["acs_c"][:, j:j + 1] - cm["acs_r"][j:j + 1, :]
                lmat = jnp.exp(jnp.where(causal, seg, -1e30))
                mmat = cbm * lmat
                dym = jnp.where(second if h2 == 1 else jnp.logical_not(second), dyp, 0.0).astype(BF16)
                dm = _dotb("nt", dym, xp)
                dxj = _dotb("tn", mmat, dym)
                dxp = dxj if h2 == 0 else dxp + dxj
                dcb = dcb + dm * lmat
                dseg = dm * mmat
                dacs_c = dacs_c + jnp.where(lane_pad == j, jnp.sum(dseg, axis=1, keepdims=True), 0.0)
                dacs_r = dacs_r - jnp.where(sub_pad == j, jnp.sum(dseg, axis=0, keepdims=True), 0.0)
            pieces.append(dxp)
        dx = pieces[0] if len(pieces) == 1 else jnp.concatenate(pieces, axis=1)
        dc = _dotb("nn", dcb, bb)
        db = _dotb("tn", dcb, cb_)

        gm = _dotb("nt", cb_, hpb)
        dgm = dyv * cm["eacs_e"]
        dc = dc + _dotb("nn", dgm, hpb)
        dhp = _dotb("tn", dgm, cb_)

        cd_row = jnp.exp(cm["acs_last"])
        cd_rows = jnp.sum(cm["reduce"] * cd_row, axis=1, keepdims=True)
        dhp = dhp + cd_rows * dhn
        rsum = jnp.sum(dhn * hpv, axis=1, keepdims=True)
        dacs_last = jnp.sum(cm["reduce"] * rsum, axis=0, keepdims=True) * cd_row
        dec_e = cm["dec_e"]
        xdec = x * dec_e
        dxdec = _dotb("nt", bb, dhnb)
        db = db + _dotb("nn", xdec, dhnb)
        dx = dx + dxdec * dec_e
        sums = _dot01("nn", jnp.concatenate([dyv * xs, dgm * gm, dxdec * x, dx * xs], axis=0), cm["reduce"], "b")
        _accum(dd_ref, jnp.sum(sums[:CHUNK], axis=0, keepdims=True), first)
        tdec = sums[2 * CHUNK:3 * CHUNK] * cm["dec_h"]
        dacs_c = dacs_c + sums[CHUNK:2 * CHUNK] - tdec
        dacs_last = dacs_last + jnp.sum(tdec, axis=0, keepdims=True)
        row_id = lax.broadcasted_iota(jnp.int32, (CHUNK, 1), 0)
        dacs_c = dacs_c + jnp.where(row_id == CHUNK - 1, dacs_last, 0.0)

        dxs_ref[...] = dxs + dx * cm["dt_e"]
        ddt = sums[3 * CHUNK:]
        dda = _dot01("nn", cm["trit"], dacs_c, "a")
        dda_r = _dot01("nn", dacs_r, cm["tri"], "b")
        dda_rp = jnp.concatenate([dda_r, jnp.zeros((HEAD_PAD - ROW_PAD, CHUNK), F32)], axis=0)
        eye = (cm["li"] == cm["si"]).astype(F32)
        dda = dda + _dot01("nt", eye, dda_rp, "a")
        ddt_ref[...] = ddt + dda * cm["a_row"]
        _accum(dal_ref, jnp.sum(dda * dtc_v, axis=0, keepdims=True) * cm["a_row"], first)
        db_ref[...] = db
        dc_ref[...] = dc
        dh_scr[...] = dhp

    gs = D_STATE * N_GROUPS
    return pl.pallas_call(
        body, name=name, grid=(N_GROUPS, nc),
        in_specs=[sp["xs"], sp["xs"], sp["bm"], sp["cm"], sp["dtc"], sp["dtr"], sp["alr"], sp["alc"],
                  sp["de"], sp["hp"]],
        out_specs=[sp["xs"], sp["bc"], sp["bc"], sp["dtc"], sp["acc"], sp["acc"]],
        out_shape=[jax.ShapeDtypeStruct((T, DI), F32), jax.ShapeDtypeStruct((T, gs), F32),
                   jax.ShapeDtypeStruct((T, gs), F32), jax.ShapeDtypeStruct((N_GROUPS, T, HEAD_PAD), F32),
                   jax.ShapeDtypeStruct((N_GROUPS, 1, HEAD_PAD), F32),
                   jax.ShapeDtypeStruct((N_GROUPS, 1, HEAD_PAD), F32)],
        scratch_shapes=[pltpu.VMEM((gw, D_STATE), F32)],
        compiler_params=_cp("parallel", "arbitrary"),
    )(dy, xbc, xbc, xbc, dtc, dtr, alr, alc, d_e, hprev)


def _adam_math(w, g, m, v):
    m = ADAM_B1 * m + (1.0 - ADAM_B1) * g
    v = ADAM_B2 * v + (1.0 - ADAM_B2) * (g * g)
    m_hat = m / (1.0 - ADAM_B1 ** ADAM_STEP)
    v_hat = v / (1.0 - ADAM_B2 ** ADAM_STEP)
    delta = -ADAM_LR * (m_hat / (jnp.sqrt(v_hat) + ADAM_EPS) + ADAM_WD * w)
    return delta, m, v


def _adamw(name, w, m, v, g, layer=0, prev=None):
    L, R, C = w.shape
    tr = _rows(R, C * 4, ADAM_BLOCK_BYTES)
    tc = C
    while tr * tc * 4 > ADAM_BLOCK_BYTES and tc % 256 == 0:
        tc //= 2
    blk = pl.BlockSpec((None, tr, tc), lambda i, j: (layer, i, j))
    prev = [] if prev is None else list(prev)

    def body(w_ref, m_ref, v_ref, g_ref, *rest):
        go_ref, d_ref, mo_ref, vo_ref = rest[len(prev):]
        gv = g_ref[...]
        delta, mn, vn = _adam_math(w_ref[...], gv, m_ref[...], v_ref[...])
        go_ref[...] = gv
        d_ref[...] = delta
        mo_ref[...] = mn
        vo_ref[...] = vn

    return pl.pallas_call(
        body, name=name, grid=(R // tr, C // tc),
        in_specs=[blk] * 3 + [pl.BlockSpec((tr, tc), lambda i, j: (i, j))] + [_ANY] * len(prev),
        out_specs=[blk] * 4, out_shape=[jax.ShapeDtypeStruct((L, R, C), F32)] * 4,
        input_output_aliases={4 + k: k for k in range(len(prev))},
        compiler_params=_cp("parallel", "parallel"),
    )(w, m, v, g, *prev)


def _sum_peers(name, gathered):
    n, R, C = gathered.shape

    def body(g_ref, o_ref):
        acc = g_ref[0]
        for d in range(1, n):
            acc = acc + g_ref[d]
        o_ref[...] = acc

    return pl.pallas_call(body, name=name, out_shape=jax.ShapeDtypeStruct((R, C), F32))(gathered)


def _place():
    x, y, c = lax.axis_index("x"), lax.axis_index("y"), lax.axis_index("c")
    chips = [(1 - x, y), (x, 1 - y), (1 - x, 1 - y)]
    return x, y, c, chips


def _allgather_small(name, v, after=()):
    R, C = v.shape
    after = list(after)

    def body(v_ref, *rest):
        o_ref, send_sems, recv_sems = rest[len(after):]
        x, y, c, _ = _place()
        me = 4 * x + 2 * y + c
        o_ref[me] = v_ref[...]
        copies = []
        for k in range(1, N_DEV):
            px, py, pc = x ^ (k >> 2), y ^ ((k >> 1) & 1), c ^ (k & 1)
            copies.append(pltpu.make_async_remote_copy(
                src_ref=v_ref, dst_ref=o_ref.at[me], send_sem=send_sems.at[k - 1], recv_sem=recv_sems.at[k - 1],
                device_id=(px, py, pc), device_id_type=MESH))
        for cp in copies:
            cp.start()
        for cp in copies:
            cp.wait()

    return pl.pallas_call(
        body, name=name, out_shape=jax.ShapeDtypeStruct((N_DEV, R, C), F32),
        in_specs=[pl.BlockSpec(memory_space=pltpu.VMEM)] + [_ANY] * len(after),
        out_specs=pl.BlockSpec(memory_space=pltpu.VMEM),
        scratch_shapes=[pltpu.SemaphoreType.DMA((N_DEV - 1,)), pltpu.SemaphoreType.DMA((N_DEV - 1,))],
    )(v, *after)


_HBM = pl.BlockSpec(memory_space=pltpu.HBM)
_SEM = pl.BlockSpec(memory_space=pltpu.SEMAPHORE)
_VMEM = pl.BlockSpec(memory_space=pltpu.VMEM)
_EFFECT = pltpu.SideEffectType.DATAFLOW_SIDE_EFFECTING


def _in_hbm(v):
    return pltpu.with_memory_space_constraint(v, pltpu.HBM)


def _remote(src, dst, send_sem, recv_sem, device):
    return pltpu.make_async_remote_copy(src_ref=src, dst_ref=dst, send_sem=send_sem, recv_sem=recv_sem,
                                        device_id=device, device_id_type=MESH)


def _gather_copies(src, land, ici_s, ici_r, own_s, own_r, arrivals=True):
    x, y, c, chips = _place()
    a = 2 * x + y
    sends, lands = [], []
    for i in range(len(src)):
        own = _remote(src[i], land[i].at[a], own_s.at[i], own_r.at[i], (x, y, 1 - c))
        sends.append(own)
        if arrivals:
            lands.append(own)
        for k, (px, py) in enumerate(chips):
            s, r = ici_s.at[3 * i + k], ici_r.at[3 * i + k]
            sends.append(_remote(src[i].at[c], land[i].at[a, c], s, r, (px, py, c)))
            if arrivals:
                lands.append(_remote(src[i].at[c], land[i].at[2 * px + py, c], s, r, (px, py, c)))
    return sends, lands


def _gather_start(name, shards, after=None):
    n = len(shards)
    lands = [lax.empty((N_CHIPS,) + s.shape, s.dtype) for s in shards]
    n_in = 2 * n + (0 if after is None else 1)

    def body(*refs):
        src, land = refs[:n], refs[n:2 * n]
        sems = refs[n_in:n_in + 4]
        token = refs[-1]
        sends, _ = _gather_copies(src, land, *sems, arrivals=False)
        for cp in sends:
            cp.start()
        token[...] = jnp.zeros_like(token)

    outs = pl.pallas_call(
        body, name=name,
        out_shape=(pltpu.SemaphoreType.DMA((3 * n,)), pltpu.SemaphoreType.DMA((3 * n,)),
                   pltpu.SemaphoreType.DMA((n,)), pltpu.SemaphoreType.DMA((n,)),
                   *[pltpu.HBM(s.shape, s.dtype) for s in shards], *[pltpu.HBM(l.shape, l.dtype) for l in lands],
                   jax.ShapeDtypeStruct((8, 128), F32)),
        in_specs=[_HBM] * (2 * n) + ([] if after is None else [_ANY]),
        out_specs=(_SEM,) * 4 + (_HBM,) * (2 * n) + (_VMEM,),
        input_output_aliases={i: 4 + i for i in range(2 * n)},
        compiler_params=pltpu.CompilerParams(has_side_effects=_EFFECT),
    )(*[_in_hbm(s) for s in shards], *[_in_hbm(l) for l in lands], *([] if after is None else [after]))
    return dict(sems=outs[:4], src=outs[4:4 + n], land=outs[4 + n:4 + 2 * n], token=outs[-1])


def _gather_wait(name, started, after):
    n = len(started["src"])
    after = list(after)

    def body(*refs):
        src, land = refs[:n], refs[n:2 * n]
        sems = refs[2 * n:2 * n + 4]
        sends, lands = _gather_copies(src, land, *sems)
        for cp in sends:
            cp.wait_send()
        for cp in lands:
            cp.wait_recv()

    outs = pl.pallas_call(
        body, name=name,
        out_shape=[pltpu.HBM(v.shape, v.dtype) for v in list(started["src"]) + list(started["land"])],
        in_specs=[_HBM] * (2 * n) + [_SEM] * 4 + [_ANY] * len(after), out_specs=[_HBM] * (2 * n),
        input_output_aliases={i: i for i in range(2 * n)},
        compiler_params=pltpu.CompilerParams(has_side_effects=_EFFECT),
    )(*started["src"], *started["land"], *started["sems"], *after)
    return outs[n:]


def _gather_forward(name, lands):
    n = len(lands)

    def body(*refs):
        buf = refs[n:2 * n]
        send_sems, recv_sems = refs[2 * n:]
        x, y, c, chips = _place()
        sends, lands_ = [], []
        for i in range(n):
            for k, (px, py) in enumerate(chips):
                b = 2 * px + py
                sends.append(_remote(buf[i].at[b, c], buf[i].at[b, c], send_sems.at[i, k], recv_sems.at[i, k],
                                     (x, y, 1 - c)))
                lands_.append(_remote(buf[i].at[b, 1 - c], buf[i].at[b, 1 - c], send_sems.at[i, k],
                                      recv_sems.at[i, k], (x, y, 1 - c)))
        for cp in sends:
            cp.start()
        for cp in sends:
            cp.wait_send()
        for cp in lands_:
            cp.wait_recv()

    return pl.pallas_call(
        body, name=name, in_specs=[_ANY] * n, out_specs=[_ANY] * n,
        out_shape=[jax.ShapeDtypeStruct(l.shape, l.dtype) for l in lands],
        input_output_aliases={i: i for i in range(n)},
        scratch_shapes=[pltpu.SemaphoreType.DMA((n, 3)), pltpu.SemaphoreType.DMA((n, 3))],
    )(*lands)


def _split_start(name, bufs, copies, n_sems, after=()):
    nb = len(bufs)
    after = list(after)

    def body(*refs):
        send_sems, recv_sems = refs[nb + len(after):nb + len(after) + 2]
        token = refs[-1]
        sends, _ = copies(refs[:nb], send_sems, recv_sems, False)
        for cp in sends:
            cp.start()
        token[...] = jnp.zeros_like(token)

    outs = pl.pallas_call(
        body, name=name,
        out_shape=(pltpu.SemaphoreType.DMA((n_sems,)), pltpu.SemaphoreType.DMA((n_sems,)),
                   *[pltpu.HBM(b.shape, b.dtype) for b in bufs], jax.ShapeDtypeStruct((8, 128), F32)),
        in_specs=[_HBM] * nb + [_ANY] * len(after), out_specs=(_SEM,) * 2 + (_HBM,) * nb + (_VMEM,),
        input_output_aliases={i: 2 + i for i in range(nb)},
        compiler_params=pltpu.CompilerParams(has_side_effects=_EFFECT),
    )(*[_in_hbm(b) for b in bufs], *after)
    return dict(sems=outs[:2], bufs=outs[2:2 + nb], token=outs[-1])


def _split_wait(name, started, copies, after):
    nb = len(started["bufs"])
    after = list(after)

    def body(*refs):
        sends, arrivals = copies(refs[:nb], refs[nb], refs[nb + 1], True)
        for cp in sends:
            cp.wait_send()
        for cp in arrivals:
            cp.wait_recv()

    return pl.pallas_call(
        body, name=name, out_shape=[pltpu.HBM(b.shape, b.dtype) for b in started["bufs"]],
        in_specs=[_HBM] * nb + [_SEM] * 2 + [_ANY] * len(after), out_specs=[_HBM] * nb,
        input_output_aliases={i: i for i in range(nb)},
        compiler_params=pltpu.CompilerParams(has_side_effects=_EFFECT),
    )(*started["bufs"], *started["sems"], *after)


def _forward_copies(bufs, send_sems, recv_sems, arrivals):
    x, y, c, chips = _place()
    sends, lands = [], []
    for i, buf in enumerate(bufs):
        for k, (px, py) in enumerate(chips):
            b, s, r = 2 * px + py, send_sems.at[3 * i + k], recv_sems.at[3 * i + k]
            sends.append(_remote(buf.at[b, c], buf.at[b, c], s, r, (x, y, 1 - c)))
            if arrivals:
                lands.append(_remote(buf.at[b, 1 - c], buf.at[b, 1 - c], s, r, (x, y, 1 - c)))
    return sends, lands


def _pair_copies(bufs, send_sems, recv_sems, arrivals):
    x, y, c, _ = _place()
    n = len(bufs) // 2
    copies = [_remote(bufs[i].at[:, 1 - c], bufs[n + i], send_sems.at[i], recv_sems.at[i], (x, y, 1 - c))
              for i in range(n)]
    return copies, copies


def _chip_copies(src, land, send_sems, recv_sems):
    x, y, c, chips = _place()
    return [_remote(src[i].at[2 * px + py], land[i].at[k], send_sems.at[3 * i + k], recv_sems.at[3 * i + k],
                    (px, py, c))
            for i in range(len(src)) for k, (px, py) in enumerate(chips)]


def _chip_start(name, parts):
    n = len(parts)
    lands = [lax.empty((3,) + p.shape[1:], p.dtype) for p in parts]

    def body(*refs):
        src, land = refs[:n], refs[n:2 * n]
        send_sems, recv_sems = refs[2 * n:2 * n + 2]
        token = refs[-1]
        for cp in _chip_copies(src, land, send_sems, recv_sems):
            cp.start()
        token[...] = jnp.zeros_like(token)

    outs = pl.pallas_call(
        body, name=name,
        out_shape=(pltpu.SemaphoreType.DMA((3 * n,)), pltpu.SemaphoreType.DMA((3 * n,)),
                   *[pltpu.HBM(p.shape, p.dtype) for p in parts], *[pltpu.HBM(l.shape, l.dtype) for l in lands],
                   jax.ShapeDtypeStruct((8, 128), F32)),
        in_specs=[_HBM] * (2 * n), out_specs=(_SEM,) * 2 + (_HBM,) * (2 * n) + (_VMEM,),
        input_output_aliases={i: 2 + i for i in range(2 * n)},
        compiler_params=pltpu.CompilerParams(has_side_effects=_EFFECT),
    )(*[_in_hbm(p) for p in parts], *[_in_hbm(l) for l in lands])
    return dict(sems=outs[:2], src=outs[2:2 + n], land=outs[2 + n:2 + 2 * n], token=outs[-1])


def _chip_wait(name, started, after):
    n = len(started["src"])
    after = list(after)

    def body(*refs):
        src, land = refs[:n], refs[n:2 * n]
        send_sems, recv_sems = refs[2 * n:2 * n + 2]
        copies = _chip_copies(src, land, send_sems, recv_sems)
        for cp in copies:
            cp.wait_send()
        for cp in copies:
            cp.wait_recv()

    outs = pl.pallas_call(
        body, name=name,
        out_shape=[pltpu.HBM(v.shape, v.dtype) for v in list(started["src"]) + list(started["land"])],
        in_specs=[_HBM] * (2 * n) + [_SEM] * 2 + [_ANY] * len(after), out_specs=[_HBM] * (2 * n),
        input_output_aliases={i: i for i in range(2 * n)},
        compiler_params=pltpu.CompilerParams(has_side_effects=_EFFECT),
    )(*started["src"], *started["land"], *started["sems"], *after)
    return outs[:n], outs[n:]


def _half_exchange(name, shards):
    n = len(shards)

    def body(*refs):
        buf = refs[n:2 * n]
        send_sems, recv_sems = refs[2 * n:]
        x, y, c, _ = _place()
        copies = [pltpu.make_async_remote_copy(
            src_ref=buf[i].at[c], dst_ref=buf[i].at[c], send_sem=send_sems.at[i], recv_sem=recv_sems.at[i],
            device_id=(x, y, 1 - c), device_id_type=MESH) for i in range(n)]
        for cp in copies:
            cp.start()
        for cp in copies:
            cp.wait()

    return pl.pallas_call(
        body, name=name, in_specs=[_ANY] * n, out_specs=[_ANY] * n,
        out_shape=[jax.ShapeDtypeStruct(s.shape, s.dtype) for s in shards],
        input_output_aliases={i: i for i in range(n)},
        scratch_shapes=[pltpu.SemaphoreType.DMA((n,)), pltpu.SemaphoreType.DMA((n,))],
    )(*shards)


def _pair_add(name, pos, part, sib):
    Q, _, R2, C = part.shape
    tr = _pick(R2, 256)

    def body(pos_ref, p_ref, s_ref, o_ref):
        o_ref[...] = (p_ref[...].astype(F32) + s_ref[...].astype(F32)).astype(BF16)

    return pl.pallas_call(
        body, name=name,
        grid_spec=pltpu.PrefetchScalarGridSpec(
            num_scalar_prefetch=1, grid=(Q, R2 // tr),
            in_specs=[pl.BlockSpec((None, None, tr, C), lambda q, i, pos_ref: (q, pos_ref[1], i, 0)),
                      pl.BlockSpec((None, tr, C), lambda q, i, pos_ref: (q, i, 0))],
            out_specs=pl.BlockSpec((None, tr, C), lambda q, i, pos_ref: (q, i, 0))),
        out_shape=jax.ShapeDtypeStruct((Q, R2, C), BF16),
        compiler_params=_cp("parallel", "parallel"),
    )(pos, part, sib)


def _chip_sum(name, pos, own, got):
    _, R2, C = own.shape
    tr = _pick(R2, 256)

    def body(pos_ref, o_ref, g_ref, out_ref):
        acc = o_ref[...].astype(F32)
        for k in range(3):
            acc = acc + g_ref[k].astype(F32)
        out_ref[...] = acc

    return pl.pallas_call(
        body, name=name,
        grid_spec=pltpu.PrefetchScalarGridSpec(
            num_scalar_prefetch=1, grid=(R2 // tr,),
            in_specs=[pl.BlockSpec((None, tr, C), lambda i, pos_ref: (pos_ref[0], i, 0)),
                      pl.BlockSpec((3, tr, C), lambda i, pos_ref: (0, i, 0))],
            out_specs=pl.BlockSpec((None, tr, C), lambda i, pos_ref: (pos_ref[1], i, 0))),
        out_shape=jax.ShapeDtypeStruct((2, R2, C), F32),
        compiler_params=_cp("parallel"),
    )(pos, own, got)


def _reduce_pair_start(tag, parts):
    split = [p.reshape(p.shape[0], 2, p.shape[1] // 2, p.shape[2]) for p in parts]
    lands = [lax.empty((p.shape[0],) + p.shape[2:], p.dtype) for p in split]
    return _split_start(tag + "_pair_start", split + lands, _pair_copies, len(parts))


def _reduce_pair_end(tag, pos, started, after):
    n = len(started["bufs"]) // 2
    bufs = _split_wait(tag + "_pair_wait", started, _pair_copies, after)
    chip = [_pair_add(f"{tag}_add{i}", pos, bufs[i], bufs[n + i]) for i in range(n)]
    return _chip_start(tag + "_chip_start", chip)


def _half_copies(bufs, send_sems, recv_sems, arrivals):
    x, y, c, _ = _place()
    sends = [_remote(b.at[c], b.at[c], send_sems.at[i], recv_sems.at[i], (x, y, 1 - c)) for i, b in enumerate(bufs)]
    lands = [_remote(b.at[1 - c], b.at[1 - c], send_sems.at[i], recv_sems.at[i], (x, y, 1 - c))
             for i, b in enumerate(bufs)] if arrivals else []
    return sends, lands


def _reduce_sum_start(tag, pos, started, after):
    chip, got = _chip_wait(tag + "_chip_wait", started, after)
    halves = [_chip_sum(f"{tag}_sum{i}", pos, chip[i], got[i]) for i in range(len(chip))]
    return _split_start(tag + "_half_start", halves, _half_copies, len(halves))


def _reduce_sum_end(tag, started, after, shapes):
    full = _split_wait(tag + "_half_wait", started, _half_copies, after)
    return [f.reshape(shp) for f, shp in zip(full, shapes)]


def _reduce_end(tag, pos, started, after, shapes):
    chip, got = _chip_wait(tag + "_chip_wait", started, after)
    halves = [_chip_sum(f"{tag}_sum{i}", pos, chip[i], got[i]) for i in range(len(chip))]
    full = _half_exchange(tag + "_half", halves)
    return [f.reshape(shp) for f, shp in zip(full, shapes)]


def _pad_to(v, mult):
    n = v.shape[0]
    return jnp.pad(v, (0, (-n) % mult))


def kernel(x, p, pool_w, pool_scale, ssm_in_w, ssm_conv_w, ssm_conv_b, ssm_dt_bias, ssm_a_log, ssm_d, ssm_norm_w, ssm_out_w, mlp_w1, mlp_w2, ln_g, ln_b, ple_w, ple_gate_w, loss_target, m_pool_w, m_pool_scale, m_ssm_in_w, m_ssm_conv_w, m_ssm_conv_b, m_ssm_dt_bias, m_ssm_a_log, m_ssm_d, m_ssm_norm_w, m_ssm_out_w, m_mlp_w1, m_mlp_w2, m_ln_g, m_ln_b, m_ple_w, m_ple_gate_w, v_pool_w, v_pool_scale, v_ssm_in_w, v_ssm_conv_w, v_ssm_conv_b, v_ssm_dt_bias, v_ssm_a_log, v_ssm_d, v_ssm_norm_w, v_ssm_out_w, v_mlp_w1, v_mlp_w2, v_ln_g, v_ln_b, v_ple_w, v_ple_gate_w):
    T, D = x.shape[1], x.shape[2]
    NG = len(POOL_WINDOWS)
    GD = D // NG
    DI = ssm_out_w.shape[1] * N_CHIPS
    H = ssm_dt_bias.shape[1]
    HPG = H // N_GROUPS
    GW = DI // N_GROUPS
    GS = N_GROUPS * D_STATE
    CD = DI + 2 * GS
    DF = mlp_w1.shape[2] * N_CHIPS
    PD = ple_w.shape[1]
    NIN = ssm_in_w.shape[2]

    xi, yi, ci = lax.axis_index("x"), lax.axis_index("y"), lax.axis_index("c")
    chip = 2 * xi + yi
    pos = jnp.stack([chip, ci]).astype(jnp.int32)

    x0 = x[0]
    tgt = loss_target[0]
    p0b, p1b = p[0, 0].astype(BF16), p[1, 0].astype(BF16)

    small_sh = jnp.concatenate([ssm_conv_w[0].reshape(-1), ssm_conv_b[0], ssm_norm_w[0],
                                ln_g.reshape(-1), ln_b.reshape(-1)])
    n_sh = small_sh.shape[0]
    small_all = _allgather_small("gather_small", _pad_to(small_sh, 1024).reshape(-1, 128))

    def halves(w, zero=None):
        w = w if zero is None else w + zero
        return w.astype(BF16).reshape((2, w.shape[0] // 2) + w.shape[1:])

    sh_pool = pool_w[0].astype(BF16)
    sh_pool = sh_pool.reshape((2, NG // 2) + sh_pool.shape[1:])
    started = {"ag0a": _gather_start("ag0a_start", [sh_pool, halves(mlp_w1[0])], after=small_all)}
    z0 = started["ag0a"]["token"][0, 0]
    groups = [("ag0b", [halves(mlp_w2[0], z0)]),
              ("ag0c", [halves(ple_gate_w[0], z0), halves(ple_w[0], z0)]),
              ("ag1a", [halves(ssm_in_w[0], z0)]),
              ("ag1b", [halves(ssm_out_w[0], z0), halves(mlp_w1[1], z0)]),
              ("ag1c", [halves(mlp_w2[1], z0), halves(ple_gate_w[1], z0), halves(ple_w[1], z0)])]

    def gather_end(tag, after):
        return _gather_forward(tag + "_fwd", _gather_wait(tag + "_wait", started[tag], after))

    def gather_land(tag, after):
        lands = _gather_wait(tag + "_wait", started[tag], after)
        return _split_start(tag + "_fwd_start", lands, _forward_copies, 3 * len(lands))

    def gather_done(tag, forwarding, after):
        return _split_wait(tag + "_fwd_wait", forwarding, _forward_copies, after)

    g_pool, g = gather_end("ag0a", [sh for _, shards in groups for sh in shards])
    prev = g_pool
    for tag, shards in groups:
        started[tag] = _gather_start(tag + "_start", shards, after=prev)
        prev = started[tag]["token"]
    all_started = prev

    g_w1, g_w2, g_pw, g_gw = {}, {}, {}, {}

    def set_w1(l, g):
        g_w1[l] = g.reshape(N_CHIPS, 1, D, DF // N_CHIPS)

    def set_w2(l, g2):
        g_w2[l] = g2.reshape(N_CHIPS, 1, DF // N_CHIPS, D)

    def set_gate(l, ggw, gpw):
        g_gw[l] = ggw.reshape(N_CHIPS, 1, D // N_CHIPS, D)
        g_pw[l] = gpw.reshape(N_CHIPS, 1, PD, D // N_CHIPS)

    set_w1(0, g)
    w_pool = jnp.transpose(g_pool.reshape(N_CHIPS, NG, GD // N_CHIPS, GD), (1, 0, 2, 3)).reshape(NG, GD, GD)

    small_all = small_all.reshape(N_DEV, -1)[0::2, :n_sh]
    cdq, niq, dq = CD // N_CHIPS, DI // N_CHIPS, D // N_CHIPS
    o = 0
    conv_w = jnp.concatenate([small_all[q, o:o + CONV_K * cdq].reshape(CONV_K, cdq) for q in range(N_CHIPS)], axis=1)
    o += CONV_K * cdq
    conv_b = small_all[:, o:o + cdq].reshape(1, CD)
    o += cdq
    norm_w = small_all[:, o:o + niq].reshape(1, DI)
    o += niq
    lng = jnp.transpose(small_all[:, o:o + 4 * dq].reshape(N_CHIPS, 2, 2, dq), (1, 2, 0, 3)).reshape(2, 2, 1, D)
    o += 4 * dq
    lnb = jnp.transpose(small_all[:, o:o + 4 * dq].reshape(N_CHIPS, 2, 2, dq), (1, 2, 0, 3)).reshape(2, 2, 1, D)

    pooled = _pool_fwd("pool_fwd", x0)
    hraw = _mm("pool_mm", "nn", pooled, w_pool, T, D, GD, tn=GD, deps=[all_started],
               a_spec=lambda tm, tn, tk: pl.BlockSpec((tm, tk), lambda i, j, k: (i, j)),
               b_spec=lambda tm, tn, tk: pl.BlockSpec((None, tk, tn), lambda i, j, k: (j, 0, 0)))
    x1, x1b, xh1, rs1 = _res_ln("ln00", x0, hraw, lng[0, 0], lnb[0, 0], scale=pool_scale)

    def mlp_fwd(l, xb, land_rest, deps=()):
        a, h2b = _mm(f"mlp{l}_up", "nn", xb, g_w1[l], T, DF, D, tn=min(1024, DF // N_CHIPS), deps=deps,
                     b_spec=_colshard_b(0, DF // N_CHIPS), out_dtype=(F32, BF16),
                     epi=lambda acc: (acc, jnp.square(jnp.maximum(acc, 0.0))))
        land_rest(a)
        h = _mm(f"mlp{l}_down", "nn", h2b, g_w2[l], T, D, DF, k_unit=DF // N_CHIPS,
                b_spec=_rowshard_b(0, DF // N_CHIPS))
        return a, h2b, h

    def ple_fwd(l, xb, pb, deps=()):
        gl = _mm(f"gate{l}_logit", "nn", xb, g_gw[l], T, D, D, k_unit=D // N_CHIPS, deps=deps,
                 b_spec=_rowshard_b(0, D // N_CHIPS))
        e = _mm(f"gate{l}_emb", "nn", pb, g_pw[l], T, D, PD, tn=min(1024, D // N_CHIPS),
                b_spec=_colshard_b(0, D // N_CHIPS))
        return gl, e

    a0, h2b0, h0 = mlp_fwd(0, x1b, lambda a: set_w2(0, *gather_end("ag0b", [a])))
    set_gate(0, *gather_end("ag0c", [h0]))
    x2, x2b, xh2, rs2 = _res_ln("ln01", x1, h0, lng[0, 1], lnb[0, 1])
    gl0, e0 = ple_fwd(0, x2b, p0b)
    x3, x3b, gate0 = _gate_fwd("gate0", x2, gl0, e0)

    g_in, = gather_end("ag1a", [x3b])

    def in_proj_cols(lo, hi):
        parts = [g_in[q].reshape(D, NIN)[:, max(lo - q * NIN, 0):min(hi - q * NIN, NIN)]
                 for q in range(N_CHIPS) if lo < (q + 1) * NIN and hi > q * NIN]
        return parts[0] if len(parts) == 1 else jnp.concatenate(parts, axis=1)

    w_z, w_xbc, w_dt = in_proj_cols(0, DI), in_proj_cols(DI, DI + CD), in_proj_cols(DI + CD, N_CHIPS * NIN)
    z = _mm("ssm_in_z", "nn", x3b, w_z, T, DI, D)
    xbc_pre = _mm("ssm_in_xbc", "nn", x3b, w_xbc, T, CD, D)
    dt_pre = _mm("ssm_in_dt", "nn", x3b, w_dt, T, H, D)
    xbc = _conv_fwd("conv_fwd", xbc_pre, conv_w, conv_b)
    dt = _dt_fwd("dt_fwd", dt_pre, ssm_dt_bias)
    dt_g = jnp.transpose(dt.reshape(T, N_GROUPS, HPG), (1, 0, 2))
    dtc = jnp.pad(dt_g, ((0, 0), (0, 0), (0, HEAD_PAD - HPG)))
    dtr = jnp.pad(jnp.transpose(dt_g, (0, 2, 1)), ((0, 0), (0, ROW_PAD - HPG), (0, 0)))
    al_g = ssm_a_log.reshape(N_GROUPS, HPG)
    alr = jnp.pad(al_g, ((0, 0), (0, HEAD_PAD - HPG)))[:, None, :]
    alc = jnp.pad(al_g, ((0, 0), (0, ROW_PAD - HPG)))[:, :, None]
    d_e = jnp.repeat(ssm_d.reshape(N_GROUPS, HPG), HEAD_DIM, axis=1)[:, None, :]
    fw1b = gather_land("ag1b", [xbc])
    ysc, hprev = _ssd_fwd("ssd_fwd", xbc, dtc, dtr, alr, alc, d_e, DI, deps=[fw1b["token"]])
    g_out, g = gather_done("ag1b", fw1b, [ysc])
    set_w1(1, g)
    w_out = g_out.reshape(DI, D)
    ynb, rsn = _gnorm_fwd("gnorm_fwd", ysc, z, norm_w)
    h1 = _mm("ssm_out", "nn", ynb, w_out, T, D, DI)
    fw1c = gather_land("ag1c", [h1])
    x4, x4b, xh4, rs4 = _res_ln("ln10", x3, h1, lng[1, 0], lnb[1, 0])

    def land_1c(a):
        g2, ggw, gpw = gather_done("ag1c", fw1c, [a])
        set_w2(1, g2)
        set_gate(1, ggw, gpw)

    a1, h2b1, h2 = mlp_fwd(1, x4b, land_1c, deps=[fw1c["token"]])
    x5, x5b, xh5, rs5 = _res_ln("ln11", x4, h2, lng[1, 1], lnb[1, 1])
    gl1, e1 = ple_fwd(1, x5b, p1b)
    dx6, gate1, loss_parts = _gate_loss("gate1_loss", x5, gl1, e1, tgt)
    loss_local = jnp.sum(loss_parts[0::8, 0])

    def ple_bwd(l, dxo, gate, e, xb, pb, deps=()):
        dgl, de = _gate_bwd(f"gate{l}_bwd", dxo, gate, e)
        d_gw = _mm(f"gate{l}_dw", "tn", xb, dgl, D, D, T, out_dtype=BF16, deps=deps).reshape(N_CHIPS, D // N_CHIPS, D)
        d_pw = _mm(f"gate{l}_dpw", "tn", pb, de, PD, D, T, out_dtype=BF16, tn=min(1024, D // N_CHIPS),
                   o_shape=(N_CHIPS, PD, D // N_CHIPS), o_spec=_colshard_o(D // N_CHIPS))
        dx = _mm(f"gate{l}_dx", "nt", dgl, g_gw[l], T, D, D, tn=min(1024, D // N_CHIPS), deps=deps,
                 b_spec=_rowshard_bt(0, D // N_CHIPS), epi=lambda acc, r: acc + r, extras=[dxo])
        return dx, d_gw, d_pw

    def mlp_bwd(l, du, dub, a, h2b, xb):
        d_w2 = _mm(f"mlp{l}_dw2", "tn", h2b, dub, DF, D, T, out_dtype=BF16).reshape(N_CHIPS, DF // N_CHIPS, D)
        da = _mm(f"mlp{l}_da", "nt", dub, g_w2[l], T, DF, D, tn=min(1024, DF // N_CHIPS),
                 b_spec=_rowshard_bt(0, DF // N_CHIPS), out_dtype=BF16,
                 epi=lambda acc, av: acc * (2.0 * jnp.maximum(av, 0.0)), extras=[a])
        d_w1 = _mm(f"mlp{l}_dw1", "tn", xb, da, D, DF, T, out_dtype=BF16, tn=min(1024, DF // N_CHIPS),
                   o_shape=(N_CHIPS, D, DF // N_CHIPS), o_spec=_colshard_o(DF // N_CHIPS))
        dx = _mm(f"mlp{l}_dx", "nt", da, g_w1[l], T, D, DF, k_unit=DF // N_CHIPS,
                 b_spec=_colshard_bt(0, DF // N_CHIPS), epi=lambda acc, r: acc + ALPHA * r, extras=[du])
        return dx, d_w1, d_w2

    dx5, d_gw1, d_pw1 = ple_bwd(1, dx6, gate1, e1, x5b, p1b)
    du5, du5b, dg11, db11 = _ln_bwd("ln11_bwd", dx5, xh5, rs5, lng[1, 1])
    dx4, d_w1_1, d_w2_1 = mlp_bwd(1, du5, du5b, a1, h2b1, x4b)
    parts1a = [d_w1_1, d_w2_1, d_gw1, d_pw1]
    pair1a = _reduce_pair_start("rs1a", parts1a)
    du4, du4b, dg10, db10 = _ln_bwd("ln10_bwd", dx4, xh4, rs4, lng[1, 0])
    d_wout = _mm("ssm_out_dw", "tn", ynb, du4b, DI, D, T, out_dtype=BF16,
                 deps=[pair1a["token"]]).reshape(N_CHIPS, DI // N_CHIPS, D)
    red1a = _reduce_pair_end("rs1a", pos, pair1a, [d_wout])
    dyn = _mm("ssm_out_dx", "nt", du4b, w_out, T, DI, D, deps=[red1a["token"]])
    dysc, dzb, dnorm_w = _gnorm_bwd("gnorm_bwd", dyn, ysc, z, norm_w, rsn)
    dxs, dbm, dcm, ddt_g, dalog_g, dd_g = _ssd_bwd("ssd_bwd", dysc, xbc, dtc, dtr, alr, alc, d_e, hprev, DI)
    dconv, dconv_w, dconv_b = _conv_bwd_a("conv_bwd_a", dxs, dbm, dcm, xbc_pre, conv_w, conv_b)
    dxbcb = _conv_bwd_b("conv_bwd_b", dconv, conv_w)
    ddt = jnp.transpose(ddt_g[:, :, :HPG], (1, 0, 2)).reshape(T, H)
    _, ddtpb, ddt_bias = _dt_bwd("dt_bwd", ddt, dt_pre, ssm_dt_bias)
    da_log = dalog_g[:, 0, :HPG].reshape(1, H)
    dd_skip = dd_g[:, 0, :HPG].reshape(1, H)
    d_wz = _mm("ssm_in_dwz", "tn", x3b, dzb, D, DI, T, out_dtype=BF16)
    d_wx = _mm("ssm_in_dwx", "tn", x3b, dxbcb, D, CD, T, out_dtype=BF16)
    d_wdt = _mm("ssm_in_dwdt", "tn", x3b, ddtpb, D, H, T, out_dtype=BF16)

    def in_proj_shard(q):
        parts = []
        for piece, start in ((d_wz, 0), (d_wx, DI), (d_wdt, DI + CD)):
            lo, hi = max(q * NIN - start, 0), min((q + 1) * NIN - start, piece.shape[1])
            if lo < hi:
                parts.append(piece[:, lo:hi])
        return parts[0] if len(parts) == 1 else jnp.concatenate(parts, axis=1)

    d_win = jnp.stack([in_proj_shard(q) for q in range(N_CHIPS)])
    parts1b = [d_win, d_wout]
    pair1b = _reduce_pair_start("rs1b", parts1b)
    dx3 = _mm("ssm_in_dxz", "nt", dzb, w_z, T, D, DI, epi=lambda acc, r: acc + ALPHA * r, extras=[du4],
              deps=[pair1b["token"]])
    dx3 = _mm("ssm_in_dxx", "nt", dxbcb, w_xbc, T, D, CD, epi=lambda acc, r: acc + r, extras=[dx3])
    dx3 = _mm("ssm_in_dxdt", "nt", ddtpb, w_dt, T, D, H, epi=lambda acc, r: acc + r, extras=[dx3])

    red1b = _reduce_pair_end("rs1b", pos, pair1b, [dx3])

    dx2, d_gw0, d_pw0 = ple_bwd(0, dx3, gate0, e0, x2b, p0b, deps=[red1b["token"]])
    du2, du2b, dg01, db01 = _ln_bwd("ln01_bwd", dx2, xh2, rs2, lng[0, 1])
    dx1, d_w1_0, d_w2_0 = mlp_bwd(0, du2, du2b, a0, h2b0, x1b)
    parts0a = [d_w1_0, d_w2_0, d_gw0, d_pw0]
    pair0a = _reduce_pair_start("rs0a", parts0a)
    du1, dhrb, dg00, db00, dscale = _ln_bwd("ln00_bwd", dx1, xh1, rs1, lng[0, 0], hraw=hraw, scale=pool_scale)
    dln_g = jnp.stack([jnp.stack([dg00, dg01]), jnp.stack([dg10, dg11])]).reshape(-1)
    dln_b = jnp.stack([jnp.stack([db00, db01]), jnp.stack([db10, db11])]).reshape(-1)
    small_g = jnp.concatenate([dscale.reshape(-1), dconv_w.reshape(-1), dconv_b.reshape(-1), ddt_bias.reshape(-1),
                               da_log.reshape(-1), dd_skip.reshape(-1), dnorm_w.reshape(-1), dln_g, dln_b,
                               loss_local.reshape(1)])
    n_sg = small_g.shape[0]
    sg_all = _allgather_small("gather_small_grads", _pad_to(small_g, 1024).reshape(-1, 128),
                              after=[d_w1_0, d_w2_0])
    d_wpool = _mm("pool_dw", "tn", pooled, dhrb, D, GD, T, tm=GD, tn=GD, out_dtype=BF16, deps=[pair0a["token"]],
                  b_spec=lambda tm, tn, tk: pl.BlockSpec((tk, tn), lambda i, j, k: (k, i)),
                  o_shape=(NG, GD, GD), o_spec=lambda tm, tn, tk: pl.BlockSpec((None, tm, tn), lambda i, j, k: (i, 0, 0)))
    d_wpool = jnp.transpose(d_wpool.reshape(NG, N_CHIPS, GD // N_CHIPS, GD), (1, 0, 2, 3)).reshape(N_CHIPS, NG * GD // N_CHIPS, GD)
    red0a = _reduce_pair_end("rs0a", pos, pair0a, [d_wpool, sg_all])
    parts0b = [d_wpool]
    pair0b = _reduce_pair_start("rs0b", parts0b)
    dpooled = _mm("pool_dx", "nt", dhrb, w_pool, T, D, GD, tn=GD, deps=[red0a["token"], pair0b["token"]],
                  a_spec=lambda tm, tn, tk: pl.BlockSpec((tm, tk), lambda i, j, k: (i, j)),
                  b_spec=lambda tm, tn, tk: pl.BlockSpec((None, tn, tk), lambda i, j, k: (j, 0, 0)))
    grad_x = _pool_bwd("pool_bwd", dpooled, du1)
    red0b = _reduce_pair_end("rs0b", pos, pair0b, [grad_x])

    def adam(name, w, m, v, g, layer=0, prev=None):
        w3, m3, v3 = (t.reshape((t.shape[0], -1, t.shape[-1])) for t in (w, m, v))
        return _adamw(name, w3, m3, v3, g, layer, prev)

    sum1a = _reduce_sum_start("rs1a", pos, red1a, [red0b["token"]])
    sum1b = _reduce_sum_start("rs1b", pos, red1b, [sum1a["token"]])
    r_w1_1, r_w2_1, r_gw1, r_pw1 = _reduce_sum_end("rs1a", sum1a, [sum1b["token"]], [p.shape[1:] for p in parts1a])
    stacked = {"mlp_w1": (mlp_w1, m_mlp_w1, v_mlp_w1), "mlp_w2": (mlp_w2, m_mlp_w2, v_mlp_w2),
               "ple_w": (ple_w, m_ple_w, v_ple_w), "ple_gate_w": (ple_gate_w, m_ple_gate_w, v_ple_gate_w)}
    grads1 = {"mlp_w1": r_w1_1, "mlp_w2": r_w2_1, "ple_w": r_pw1, "ple_gate_w": r_gw1}
    upper = {nm: adam(f"adam_{nm}_l1", *stacked[nm], grads1[nm], layer=1) for nm in stacked}
    r_in, r_out = _reduce_sum_end("rs1b", sum1b, [t[1] for t in upper.values()], [p.shape[1:] for p in parts1b])
    big = {"ssm_in_w": [jnp.swapaxes(t, 1, 2) for t in adam(
               "adam_ssm_in_w", *(jnp.swapaxes(t, 1, 2) for t in (ssm_in_w, m_ssm_in_w, v_ssm_in_w)), r_in.T)],
           "ssm_out_w": adam("adam_ssm_out_w", ssm_out_w, m_ssm_out_w, v_ssm_out_w, r_out)}

    sg = _sum_peers("sum_small_grads", sg_all).reshape(-1)[:n_sg]
    o = 0

    def take(nel):
        nonlocal o
        v = sg[o:o + nel]
        o += nel
        return v

    g_scale = take(D).reshape(1, D)
    g_conv_w_full = take(CONV_K * CD).reshape(CONV_K, CD)
    g_conv_b_full = take(CD).reshape(1, CD)
    g_dt_bias = take(H).reshape(1, H)
    g_a_log = take(H).reshape(1, H)
    g_d = take(H).reshape(1, H)
    g_norm_full = take(DI).reshape(1, DI)
    g_lng_full = take(4 * D).reshape(2, 2, D)
    g_lnb_full = take(4 * D).reshape(2, 2, D)
    loss = take(1).reshape(())
    g_conv_w = lax.dynamic_slice_in_dim(g_conv_w_full, chip * cdq, cdq, axis=1)[None]
    g_conv_b = lax.dynamic_slice_in_dim(g_conv_b_full, chip * cdq, cdq, axis=1)
    g_norm = lax.dynamic_slice_in_dim(g_norm_full, chip * niq, niq, axis=1)
    g_lng = lax.dynamic_slice_in_dim(g_lng_full, chip * dq, dq, axis=2)
    g_lnb = lax.dynamic_slice_in_dim(g_lnb_full, chip * dq, dq, axis=2)

    small = [("pool_scale", pool_scale, m_pool_scale, v_pool_scale, g_scale),
             ("ssm_conv_w", ssm_conv_w, m_ssm_conv_w, v_ssm_conv_w, g_conv_w),
             ("ssm_conv_b", ssm_conv_b, m_ssm_conv_b, v_ssm_conv_b, g_conv_b),
             ("ssm_dt_bias", ssm_dt_bias, m_ssm_dt_bias, v_ssm_dt_bias, g_dt_bias),
             ("ssm_a_log", ssm_a_log, m_ssm_a_log, v_ssm_a_log, g_a_log),
             ("ssm_d", ssm_d, m_ssm_d, v_ssm_d, g_d),
             ("ssm_norm_w", ssm_norm_w, m_ssm_norm_w, v_ssm_norm_w, g_norm),
             ("ln_g", ln_g, m_ln_g, v_ln_g, g_lng),
             ("ln_b", ln_b, m_ln_b, v_ln_b, g_lnb)]

    def pack(idx):
        flat = _pad_to(jnp.concatenate([s[idx].reshape(-1) for s in small]), 1024)
        return flat.reshape(1, -1, 128)

    sm_out = _adamw("adam_small", pack(1), pack(2), pack(3), pack(4)[0])
    small_res = {}
    o = 0
    for s in small:
        nel = s[1].size
        small_res[s[0]] = [t.reshape(-1)[o:o + nel].reshape(s[1].shape) for t in sm_out]
        o += nel

    r_w1_0, r_w2_0, r_gw0, r_pw0 = _reduce_end(
        "rs0a", pos, red0a, [big["ssm_in_w"][1], big["ssm_out_w"][1], sm_out[1]] + [upper[nm][1] for nm in upper],
        [p.shape[1:] for p in parts0a])
    r_pool, = _reduce_end("rs0b", pos, red0b, [r_w1_0], [p.shape[1:] for p in parts0b])
    grads0 = {"mlp_w1": r_w1_0, "mlp_w2": r_w2_0, "ple_w": r_pw0, "ple_gate_w": r_gw0}
    big["pool_w"] = adam("adam_pool_w", pool_w, m_pool_w, v_pool_w, r_pool)
    for nm in stacked:
        big[nm] = adam(f"adam_{nm}_l0", *stacked[nm], grads0[nm], layer=0, prev=upper[nm])
    shapes = {"pool_w": pool_w.shape, "ssm_in_w": ssm_in_w.shape, "ssm_out_w": ssm_out_w.shape,
              **{nm: stacked[nm][0].shape for nm in stacked}}
    big = {nm: [t.reshape(shapes[nm]) for t in big[nm]] for nm in big}

    order = ["pool_w", "pool_scale", "ssm_in_w", "ssm_conv_w", "ssm_conv_b", "ssm_dt_bias", "ssm_a_log", "ssm_d",
             "ssm_norm_w", "ssm_out_w", "mlp_w1", "mlp_w2", "ln_g", "ln_b", "ple_w", "ple_gate_w"]
    res = {**big, **small_res}
    outs = [loss, grad_x[None]]
    for kind in range(4):
        outs += [res[nm][kind] for nm in order]
    return tuple(outs)
```

```python
import jax
import jax.numpy as jnp
from jax import lax
from jax.experimental import pallas as pl
from jax.experimental.pallas import tpu as pltpu

F32 = jnp.float32
BF16 = jnp.bfloat16
MESH = pl.DeviceIdType.MESH

DEPTH = 2
ALPHA = (2.0 * DEPTH) ** 0.25
LN_EPS = 1e-5
RMS_EPS = 1e-5
POOL_WINDOWS = (2, 4, 8, 16)
POOL_HALO = 16
HEAD_DIM = 64
HEAD_PAIR = 2 * HEAD_DIM
N_GROUPS = 8
D_STATE = 128
CHUNK = 128
CONV_K = 4
CONV_HALO = 8
HEAD_PAD = 128
ROW_PAD = 8
N_CHIPS = 4
N_DEV = 8
ADAM_LR = 0.001
ADAM_B1 = 0.9
ADAM_B2 = 0.999
ADAM_EPS = 1e-08
ADAM_WD = 0.01
ADAM_STEP = 10
VMEM_LIMIT = 56 * 1024 * 1024
ADAM_BLOCK_BYTES = 1024 * 1024
MM_VMEM_BUDGET = 40 * 1024 * 1024


_ANY = pl.BlockSpec(memory_space=pl.ANY)


def _cp(*sem):
    return pltpu.CompilerParams(dimension_semantics=sem, vmem_limit_bytes=VMEM_LIMIT)


def _pick(dim, pref):
    t = pref
    while t >= 128:
        if dim % t == 0:
            return t
        t //= 2
    return dim


def _rows(rows, row_bytes, budget):
    t = rows
    while t * row_bytes > budget and t % 16 == 0:
        t //= 2
    return t


def _sigmoid(v):
    return 1.0 / (1.0 + jnp.exp(-v))


_DIMS = {"nn": (((1,), (0,)), ((), ())), "nt": (((1,), (1,)), ((), ())), "tn": (((0,), (0,)), ((), ()))}


def _pick_k(k_unit, fixed_bytes, per_k_bytes):
    for n in range(1, k_unit // 128 + 1):
        if k_unit % n == 0 and (n == 1 or (k_unit // n) % 128 == 0):
            if fixed_bytes + (k_unit // n) * per_k_bytes <= MM_VMEM_BUDGET:
                return k_unit // n
    return min(k_unit, 128)


def _mm(name, form, a, b, M, N, K, *, tm=1024, tn=1024, k_unit=None, a_spec=None, b_spec=None,
        o_shape=None, o_spec=None, out_dtype=F32, pro=None, epi=None, extras=(), deps=()):
    tm, tn = _pick(M, tm), _pick(N, tn)
    out_dtypes = out_dtype if isinstance(out_dtype, tuple) else (out_dtype,)
    n_out = len(out_dtypes)
    in_place = n_out == 1 and out_dtypes[0] == F32
    fixed = 2 * tm * tn * (sum(jnp.dtype(d).itemsize for d in out_dtypes) + 4 * len(extras))
    fixed += 0 if in_place else 4 * tm * tn
    tk = _pick_k(K if k_unit is None else k_unit, fixed,
                 2 * (tm * a.dtype.itemsize + tn * b.dtype.itemsize))
    nk = K // tk
    if a_spec is None:
        a_spec = (pl.BlockSpec((tk, tm), lambda i, j, k: (k, i)) if form == "tn"
                  else pl.BlockSpec((tm, tk), lambda i, j, k: (i, k)))
    else:
        a_spec = a_spec(tm, tn, tk)
    if b_spec is None:
        b_spec = (pl.BlockSpec((tn, tk), lambda i, j, k: (j, k)) if form == "nt"
                  else pl.BlockSpec((tk, tn), lambda i, j, k: (k, j)))
    else:
        b_spec = b_spec(tm, tn, tk)
    if o_spec is None:
        o_spec = pl.BlockSpec((tm, tn), lambda i, j, k: (i, j))
        o_shape = (M, N)
    else:
        o_spec = o_spec(tm, tn, tk)
    ex_arrays = [e for e in extras]
    ex_specs = [pl.BlockSpec((tm, tn), lambda i, j, k: (i, j)) for _ in extras]
    ne = len(ex_arrays)
    nd = len(deps)
    dep_specs = [pl.BlockSpec((8, 128), lambda i, j, k: (0, 0)) for _ in deps]
    dims = _DIMS[form]
    use_scratch = nk > 1 and not in_place

    def body(a_ref, b_ref, *rest):
        ex_refs = rest[:ne]
        o_refs = rest[ne + nd:ne + nd + n_out]
        at = a_ref[...]
        if pro is not None:
            at = pro(at)
        p = lax.dot_general(at.astype(BF16), b_ref[...].astype(BF16), dims, preferred_element_type=F32)

        def finish(acc):
            res = acc if epi is None else epi(acc, *[r[...] for r in ex_refs])
            res = res if isinstance(res, tuple) else (res,)
            for o_ref, r, d in zip(o_refs, res, out_dtypes):
                o_ref[...] = r.astype(d)

        if nk == 1:
            finish(p)
        else:
            acc_ref = rest[ne + nd + n_out] if use_scratch else o_refs[0]
            k = pl.program_id(2)

            @pl.when(k == 0)
            def _():
                acc_ref[...] = p

            @pl.when(jnp.logical_and(k > 0, k < nk - 1))
            def _():
                acc_ref[...] += p

            @pl.when(k == nk - 1)
            def _():
                finish(acc_ref[...] + p)

    res = pl.pallas_call(
        body, name=name, grid=(M // tm, N // tn, nk),
        in_specs=[a_spec, b_spec] + ex_specs + dep_specs, out_specs=[o_spec] * n_out,
        out_shape=[jax.ShapeDtypeStruct(o_shape, d) for d in out_dtypes],
        scratch_shapes=[pltpu.VMEM((tm, tn), F32)] if use_scratch else [],
        compiler_params=_cp("parallel", "parallel", "arbitrary"),
    )(a, b, *ex_arrays, *deps)
    return res if n_out > 1 else res[0]


def _colshard_b(l, n_per):
    def make(tm, tn, tk):
        nb = n_per // tn
        return pl.BlockSpec((None, None, tk, tn), lambda i, j, k: (j // nb, l, k, j % nb))
    return make


def _colshard_bt(l, n_per):
    def make(tm, tn, tk):
        nb = n_per // tk
        return pl.BlockSpec((None, None, tn, tk), lambda i, j, k: (k // nb, l, j, k % nb))
    return make


def _colshard_o(n_per):
    def make(tm, tn, tk):
        nb = n_per // tn
        return pl.BlockSpec((None, tm, tn), lambda i, j, k: (j // nb, i, j % nb))
    return make


def _rowshard_b(l, k_per):
    def make(tm, tn, tk):
        nb = k_per // tk
        return pl.BlockSpec((None, None, tk, tn), lambda i, j, k: (k // nb, l, k % nb, j))
    return make


def _rowshard_bt(l, k_per):
    def make(tm, tn, tk):
        nb = k_per // tn
        return pl.BlockSpec((None, None, tn, tk), lambda i, j, k: (j // nb, l, j % nb, k))
    return make


def _res_ln(name, xprev, h, g, b, scale=None):
    T, D = xprev.shape
    tr = _pick(T, 256)
    row = pl.BlockSpec((tr, D), lambda i: (i, 0))
    vec = pl.BlockSpec((1, D), lambda i: (0, 0))
    has_scale = scale is not None

    def body(*refs):
        if has_scale:
            x_ref, h_ref, s_ref, g_ref, b_ref, y_ref, yb_ref, xh_ref, rs_ref = refs
            hh = h_ref[...] * s_ref[...]
        else:
            x_ref, h_ref, g_ref, b_ref, y_ref, yb_ref, xh_ref, rs_ref = refs
            hh = h_ref[...]
        u = ALPHA * x_ref[...] + hh
        mu = jnp.mean(u, axis=-1, keepdims=True)
        d = u - mu
        var = jnp.mean(d * d, axis=-1, keepdims=True)
        rs = lax.rsqrt(var + LN_EPS)
        xh = d * rs
        y = xh * g_ref[...] + b_ref[...]
        y_ref[...] = y
        yb_ref[...] = y.astype(BF16)
        xh_ref[...] = xh
        rs_ref[...] = rs

    ins = [xprev, h] + ([scale] if has_scale else []) + [g, b]
    specs = [row, row] + ([vec] if has_scale else []) + [vec, vec]
    return pl.pallas_call(
        body, name=name, grid=(T // tr,), in_specs=specs,
        out_specs=[row, row, row, pl.BlockSpec((tr, 1), lambda i: (i, 0))],
        out_shape=[jax.ShapeDtypeStruct((T, D), F32), jax.ShapeDtypeStruct((T, D), BF16),
                   jax.ShapeDtypeStruct((T, D), F32), jax.ShapeDtypeStruct((T, 1), F32)],
        compiler_params=_cp("parallel"),
    )(*ins)


def _accum(ref, part, first):
    @pl.when(first)
    def _():
        ref[...] = part

    @pl.when(jnp.logical_not(first))
    def _():
        ref[...] += part


def _ln_bwd(name, dy, xh, rs, g, hraw=None, scale=None):
    T, D = dy.shape
    tr = _pick(T, 256)
    row = pl.BlockSpec((tr, D), lambda i: (i, 0))
    vec = pl.BlockSpec((1, D), lambda i: (0, 0))
    has_scale = scale is not None

    def body(*refs):
        if has_scale:
            dy_ref, xh_ref, rs_ref, g_ref, hr_ref, s_ref, du_ref, dub_ref, dg_ref, db_ref, ds_ref = refs
        else:
            dy_ref, xh_ref, rs_ref, g_ref, du_ref, dub_ref, dg_ref, db_ref = refs
        first = pl.program_id(0) == 0
        dyv = dy_ref[...]
        xhv = xh_ref[...]
        dxh = dyv * g_ref[...]
        m1 = jnp.mean(dxh, axis=-1, keepdims=True)
        m2 = jnp.mean(dxh * xhv, axis=-1, keepdims=True)
        du = rs_ref[...] * (dxh - m1 - xhv * m2)
        du_ref[...] = du
        if has_scale:
            dub_ref[...] = (du * s_ref[...]).astype(BF16)
            _accum(ds_ref, jnp.sum(du * hr_ref[...], axis=0, keepdims=True), first)
        else:
            dub_ref[...] = du.astype(BF16)
        _accum(dg_ref, jnp.sum(dyv * xhv, axis=0, keepdims=True), first)
        _accum(db_ref, jnp.sum(dyv, axis=0, keepdims=True), first)

    ins = [dy, xh, rs, g] + ([hraw, scale] if has_scale else [])
    specs = [row, row, pl.BlockSpec((tr, 1), lambda i: (i, 0)), vec] + ([row, vec] if has_scale else [])
    n_vec = 3 if has_scale else 2
    return pl.pallas_call(
        body, name=name, grid=(T // tr,), in_specs=specs,
        out_specs=[row, row] + [vec] * n_vec,
        out_shape=[jax.ShapeDtypeStruct((T, D), F32), jax.ShapeDtypeStruct((T, D), BF16)]
        + [jax.ShapeDtypeStruct((1, D), F32)] * n_vec,
        compiler_params=_cp("arbitrary"),
    )(*ins)


def _gate_fwd(name, x, gl, e):
    T, D = x.shape
    tr = _pick(T, 256)
    row = pl.BlockSpec((tr, D), lambda i: (i, 0))

    def body(x_ref, gl_ref, e_ref, xo_ref, xob_ref, gate_ref):
        gate = _sigmoid(gl_ref[...])
        xo = x_ref[...] + gate * e_ref[...]
        xo_ref[...] = xo
        xob_ref[...] = xo.astype(BF16)
        gate_ref[...] = gate

    return pl.pallas_call(
        body, name=name, grid=(T // tr,), in_specs=[row, row, row], out_specs=[row, row, row],
        out_shape=[jax.ShapeDtypeStruct((T, D), F32), jax.ShapeDtypeStruct((T, D), BF16),
                   jax.ShapeDtypeStruct((T, D), F32)],
        compiler_params=_cp("parallel"),
    )(x, gl, e)


def _gate_loss(name, x, gl, e, tgt):
    T, D = x.shape
    tr = _pick(T, 256)
    row = pl.BlockSpec((tr, D), lambda i: (i, 0))

    def body(x_ref, gl_ref, e_ref, t_ref, dy_ref, gate_ref, lp_ref):
        gate = _sigmoid(gl_ref[...])
        err = x_ref[...] + gate * e_ref[...] - t_ref[...]
        dy_ref[...] = err * (1.0 / D)
        gate_ref[...] = gate
        s = jnp.sum(jnp.mean(err * err, axis=-1, keepdims=True), axis=0, keepdims=True)
        lp_ref[...] = jnp.broadcast_to(0.5 * s, (8, 128))

    return pl.pallas_call(
        body, name=name, grid=(T // tr,), in_specs=[row] * 4,
        out_specs=[row, row, pl.BlockSpec((8, 128), lambda i: (i, 0))],
        out_shape=[jax.ShapeDtypeStruct((T, D), F32), jax.ShapeDtypeStruct((T, D), F32),
                   jax.ShapeDtypeStruct((T // tr * 8, 128), F32)],
        compiler_params=_cp("parallel"),
    )(x, gl, e, tgt)


def _gate_bwd(name, dxo, gate, e):
    T, D = dxo.shape
    tr = _pick(T, 256)
    row = pl.BlockSpec((tr, D), lambda i: (i, 0))

    def body(d_ref, gate_ref, e_ref, dgl_ref, de_ref):
        d = d_ref[...]
        gate = gate_ref[...]
        dgl_ref[...] = (d * e_ref[...] * gate * (1.0 - gate)).astype(BF16)
        de_ref[...] = (d * gate).astype(BF16)

    return pl.pallas_call(
        body, name=name, grid=(T // tr,), in_specs=[row] * 3, out_specs=[row, row],
        out_shape=[jax.ShapeDtypeStruct((T, D), BF16)] * 2,
        compiler_params=_cp("parallel"),
    )(dxo, gate, e)


def _gnorm_fwd(name, y, z, w):
    T, DI = y.shape
    tr = _pick(T, 128)
    row = pl.BlockSpec((tr, DI), lambda i: (i, 0))
    vec = pl.BlockSpec((1, DI), lambda i: (0, 0))
    col = pl.BlockSpec((tr, 1), lambda i: (i, 0))

    def body(y_ref, z_ref, w_ref, o_ref, rs_ref):
        zv = z_ref[...]
        yz = y_ref[...] * (zv * _sigmoid(zv))
        rs = lax.rsqrt(jnp.mean(yz * yz, axis=-1, keepdims=True) + RMS_EPS)
        o_ref[...] = (yz * rs * w_ref[...]).astype(BF16)
        rs_ref[...] = rs

    return pl.pallas_call(
        body, name=name, grid=(T // tr,), in_specs=[row, row, vec], out_specs=[row, col],
        out_shape=[jax.ShapeDtypeStruct((T, DI), BF16), jax.ShapeDtypeStruct((T, 1), F32)],
        compiler_params=_cp("parallel"),
    )(y, z, w)


def _gnorm_bwd(name, dyn, y, z, w, rs):
    T, DI = y.shape
    tr = _pick(T, 128)
    row = pl.BlockSpec((tr, DI), lambda i: (i, 0))
    vec = pl.BlockSpec((1, DI), lambda i: (0, 0))
    col = pl.BlockSpec((tr, 1), lambda i: (i, 0))

    def body(d_ref, y_ref, z_ref, w_ref, rs_ref, dy_ref, dz_ref, dw_ref):
        first = pl.program_id(0) == 0
        zv = z_ref[...]
        yv = y_ref[...]
        sg = _sigmoid(zv)
        sz = zv * sg
        rsv = rs_ref[...]
        yzh = yv * sz * rsv
        dv = d_ref[...]
        gw = dv * w_ref[...]
        m = jnp.mean(gw * yzh, axis=-1, keepdims=True)
        dyz = rsv * (gw - yzh * m)
        dy_ref[...] = dyz * sz
        dz_ref[...] = (dyz * yv * (sg * (1.0 + zv * (1.0 - sg)))).astype(BF16)
        _accum(dw_ref, jnp.sum(dv * yzh, axis=0, keepdims=True), first)

    return pl.pallas_call(
        body, name=name, grid=(T // tr,), in_specs=[row, row, row, vec, col], out_specs=[row, row, vec],
        out_shape=[jax.ShapeDtypeStruct((T, DI), F32), jax.ShapeDtypeStruct((T, DI), BF16),
                   jax.ShapeDtypeStruct((1, DI), F32)],
        compiler_params=_cp("arbitrary"),
    )(dyn, y, z, w, rs)


def _sel4(j, vals):
    return jnp.where(j == 0, vals[0], jnp.where(j == 1, vals[1], jnp.where(j == 2, vals[2], vals[3])))


def _pool_cnt(i, j, tr, rows, offset):
    t = i * tr + offset + lax.broadcasted_iota(jnp.int32, (rows, 1), 0)
    win = _sel4(j, POOL_WINDOWS)
    return jnp.minimum(t + 1, win).astype(F32)


def _pool_fwd(name, x):
    T, D = x.shape
    gd = D // len(POOL_WINDOWS)
    tr = _pick(T, 512)
    hb = tr // POOL_HALO

    def body(x_ref, h_ref, o_ref):
        i, j = pl.program_id(0), pl.program_id(1)
        xv = x_ref[...]
        halo = jnp.where(i > 0, h_ref[...], 0.0)
        cat = jnp.concatenate([halo, xv], axis=0)
        s2 = cat + pltpu.roll(cat, 1, 0)
        s4 = s2 + pltpu.roll(s2, 2, 0)
        s8 = s4 + pltpu.roll(s4, 4, 0)
        s16 = s8 + pltpu.roll(s8, 8, 0)
        sel = _sel4(j, (s2, s4, s8, s16))[POOL_HALO:]
        o_ref[...] = (sel / _pool_cnt(i, j, tr, tr, 0) - xv).astype(BF16)

    return pl.pallas_call(
        body, name=name, grid=(T // tr, len(POOL_WINDOWS)),
        in_specs=[pl.BlockSpec((tr, gd), lambda i, j: (i, j)),
                  pl.BlockSpec((POOL_HALO, gd), lambda i, j: (jnp.maximum(i * hb - 1, 0), j))],
        out_specs=pl.BlockSpec((tr, gd), lambda i, j: (i, j)),
        out_shape=jax.ShapeDtypeStruct((T, D), BF16),
        compiler_params=_cp("parallel", "parallel"),
    )(x, x)


def _pool_bwd(name, dp, du):
    T, D = dp.shape
    gd = D // len(POOL_WINDOWS)
    tr = _pick(T, 512)
    hb = tr // POOL_HALO
    last_h = T // POOL_HALO - 1
    n = tr + POOL_HALO

    def body(dp_ref, h_ref, du_ref, o_ref):
        i, j = pl.program_id(0), pl.program_id(1)
        dpv = dp_ref[...]
        q = dpv / _pool_cnt(i, j, tr, tr, 0)
        qh = jnp.where(i < pl.num_programs(0) - 1, h_ref[...] / _pool_cnt(i, j, tr, POOL_HALO, tr), 0.0)
        cat = jnp.concatenate([q, qh], axis=0)
        f2 = cat + pltpu.roll(cat, n - 1, 0)
        f4 = f2 + pltpu.roll(f2, n - 2, 0)
        f8 = f4 + pltpu.roll(f4, n - 4, 0)
        f16 = f8 + pltpu.roll(f8, n - 8, 0)
        sel = _sel4(j, (f2, f4, f8, f16))[:tr]
        o_ref[...] = ALPHA * du_ref[...] + sel - dpv

    return pl.pallas_call(
        body, name=name, grid=(T // tr, len(POOL_WINDOWS)),
        in_specs=[pl.BlockSpec((tr, gd), lambda i, j: (i, j)),
                  pl.BlockSpec((POOL_HALO, gd), lambda i, j: (jnp.minimum((i + 1) * hb, last_h), j)),
                  pl.BlockSpec((tr, gd), lambda i, j: (i, j))],
        out_specs=pl.BlockSpec((tr, gd), lambda i, j: (i, j)),
        out_shape=jax.ShapeDtypeStruct((T, D), F32),
        compiler_params=_cp("parallel", "parallel"),
    )(dp, dp, du)


def _conv_taps(cat, wv, rows):
    shifted = [cat[CONV_HALO:] if s == 0 else pltpu.roll(cat, s, 0)[CONV_HALO:] for s in range(CONV_K)]
    acc = shifted[0] * wv[CONV_K - 1:CONV_K]
    for s in range(1, CONV_K):
        acc = acc + shifted[s] * wv[CONV_K - 1 - s:CONV_K - s]
    return acc, shifted


def _conv_fwd(name, xp, w, b):
    T, CD = xp.shape
    tr, tc = _pick(T, 512), _pick(CD, 512)
    hb = tr // CONV_HALO

    def body(x_ref, h_ref, w_ref, b_ref, o_ref):
        i = pl.program_id(0)
        halo = jnp.where(i > 0, h_ref[...], 0.0)
        cat = jnp.concatenate([halo, x_ref[...]], axis=0)
        acc, _ = _conv_taps(cat, w_ref[...], tr)
        acc = acc + b_ref[...]
        o_ref[...] = acc * _sigmoid(acc)

    return pl.pallas_call(
        body, name=name, grid=(T // tr, CD // tc),
        in_specs=[pl.BlockSpec((tr, tc), lambda i, j: (i, j)),
                  pl.BlockSpec((CONV_HALO, tc), lambda i, j: (jnp.maximum(i * hb - 1, 0), j)),
                  pl.BlockSpec((CONV_K, tc), lambda i, j: (0, j)),
                  pl.BlockSpec((1, tc), lambda i, j: (0, j))],
        out_specs=pl.BlockSpec((tr, tc), lambda i, j: (i, j)),
        out_shape=jax.ShapeDtypeStruct((T, CD), F32),
        compiler_params=_cp("parallel", "parallel"),
    )(xp, xp, w, b)


def _conv_bwd_a(name, dxs, dbm, dcm, xp, w, b):
    T, CD = xp.shape
    tr = _pick(T, 512)
    tc = _pick(dbm.shape[1], 512)
    hb = tr // CONV_HALO
    nx, nb = dxs.shape[1] // tc, dbm.shape[1] // tc

    def part_spec(lo, n):
        def imap(j, i):
            inside = jnp.logical_and(j >= lo, j < lo + n)
            return (jnp.where(inside, i, 0), jnp.clip(j - lo, 0, n - 1))
        return pl.BlockSpec((tr, tc), imap)

    def body(dx_ref, db_ref, dc_ref, x_ref, h_ref, w_ref, b_ref, o_ref, dw_ref, dbias_ref):
        j, i = pl.program_id(0), pl.program_id(1)
        first = i == 0
        d = jnp.where(j < nx, dx_ref[...], jnp.where(j < nx + nb, db_ref[...], dc_ref[...]))
        halo = jnp.where(i > 0, h_ref[...], 0.0)
        cat = jnp.concatenate([halo, x_ref[...]], axis=0)
        acc, shifted = _conv_taps(cat, w_ref[...], tr)
        acc = acc + b_ref[...]
        sg = _sigmoid(acc)
        dconv = d * (sg * (1.0 + acc * (1.0 - sg)))
        o_ref[...] = dconv
        _accum(dbias_ref, jnp.sum(dconv, axis=0, keepdims=True), first)
        tap = lax.broadcasted_iota(jnp.int32, (CONV_K, tc), 0)
        dwv = jnp.zeros((CONV_K, tc), F32)
        for s in range(CONV_K):
            dwv = jnp.where(tap == CONV_K - 1 - s, jnp.sum(dconv * shifted[s], axis=0, keepdims=True), dwv)
        _accum(dw_ref, dwv, first)

    return pl.pallas_call(
        body, name=name, grid=(CD // tc, T // tr),
        in_specs=[part_spec(0, nx), part_spec(nx, nb), part_spec(nx + nb, nb),
                  pl.BlockSpec((tr, tc), lambda j, i: (i, j)),
                  pl.BlockSpec((CONV_HALO, tc), lambda j, i: (jnp.maximum(i * hb - 1, 0), j)),
                  pl.BlockSpec((CONV_K, tc), lambda j, i: (0, j)),
                  pl.BlockSpec((1, tc), lambda j, i: (0, j))],
        out_specs=[pl.BlockSpec((tr, tc), lambda j, i: (i, j)),
                   pl.BlockSpec((CONV_K, tc), lambda j, i: (0, j)),
                   pl.BlockSpec((1, tc), lambda j, i: (0, j))],
        out_shape=[jax.ShapeDtypeStruct((T, CD), F32), jax.ShapeDtypeStruct((CONV_K, CD), F32),
                   jax.ShapeDtypeStruct((1, CD), F32)],
        compiler_params=_cp("parallel", "arbitrary"),
    )(dxs, dbm, dcm, xp, xp, w, b)


def _conv_bwd_b(name, dconv, w):
    T, CD = dconv.shape
    tr, tc = _pick(T, 512), _pick(CD, 512)
    hb = tr // CONV_HALO
    last_h = T // CONV_HALO - 1
    n = tr + CONV_HALO

    def body(d_ref, h_ref, w_ref, o_ref):
        i = pl.program_id(0)
        halo = jnp.where(i < pl.num_programs(0) - 1, h_ref[...], 0.0)
        cat = jnp.concatenate([d_ref[...], halo], axis=0)
        wv = w_ref[...]
        acc = cat[:tr] * wv[CONV_K - 1:CONV_K]
        for s in range(1, CONV_K):
            acc = acc + pltpu.roll(cat, n - s, 0)[:tr] * wv[CONV_K - 1 - s:CONV_K - s]
        o_ref[...] = acc.astype(BF16)

    return pl.pallas_call(
        body, name=name, grid=(T // tr, CD // tc),
        in_specs=[pl.BlockSpec((tr, tc), lambda i, j: (i, j)),
                  pl.BlockSpec((CONV_HALO, tc), lambda i, j: (jnp.minimum((i + 1) * hb, last_h), j)),
                  pl.BlockSpec((CONV_K, tc), lambda i, j: (0, j))],
        out_specs=pl.BlockSpec((tr, tc), lambda i, j: (i, j)),
        out_shape=jax.ShapeDtypeStruct((T, CD), BF16),
        compiler_params=_cp("parallel", "parallel"),
    )(dconv, dconv, w)


def _dt_fwd(name, dtp, bias):
    T, H = dtp.shape

    def body(x_ref, b_ref, o_ref):
        v = x_ref[...] + b_ref[...]
        u = jnp.exp(-jnp.abs(v))
        w1 = 1.0 + u
        lp = jnp.where(w1 == 1.0, u, jnp.log(w1) * (u / jnp.where(w1 == 1.0, 1.0, w1 - 1.0)))
        o_ref[...] = jnp.maximum(v, 0.0) + lp

    return pl.pallas_call(body, name=name, out_shape=jax.ShapeDtypeStruct((T, H), F32))(dtp, bias)


def _dt_bwd(name, ddt, dtp, bias):
    T, H = dtp.shape

    def body(d_ref, x_ref, b_ref, o_ref, ob_ref, db_ref):
        g = d_ref[...] * _sigmoid(x_ref[...] + b_ref[...])
        o_ref[...] = g
        ob_ref[...] = g.astype(BF16)
        db_ref[...] = jnp.sum(g, axis=0, keepdims=True)

    return pl.pallas_call(
        body, name=name,
        out_shape=[jax.ShapeDtypeStruct((T, H), F32), jax.ShapeDtypeStruct((T, H), BF16),
                   jax.ShapeDtypeStruct((1, H), F32)],
    )(ddt, dtp, bias)


def _split(v):
    hi = v.astype(BF16)
    return hi, (v - hi.astype(F32)).astype(BF16)


def _dot01(form, a, b, mask):
    if mask == "b":
        hi, lo = _split(a)
        mb = b.astype(BF16)
        return (lax.dot_general(hi, mb, _DIMS[form], preferred_element_type=F32)
                + lax.dot_general(lo, mb, _DIMS[form], preferred_element_type=F32))
    hi, lo = _split(b)
    ma = a.astype(BF16)
    return (lax.dot_general(ma, hi, _DIMS[form], preferred_element_type=F32)
            + lax.dot_general(ma, lo, _DIMS[form], preferred_element_type=F32))


def _dotb(form, a, b):
    return lax.dot_general(a.astype(BF16), b.astype(BF16), _DIMS[form], preferred_element_type=F32)


def _ssd_common(dtc, dtr, alr, alc, gw):
    li = lax.broadcasted_iota(jnp.int32, (CHUNK, CHUNK), 0)
    si = lax.broadcasted_iota(jnp.int32, (CHUNK, CHUNK), 1)
    tri = (li >= si).astype(F32)
    trit = (li <= si).astype(F32)
    a_row = -jnp.exp(alr)
    a_col = -jnp.exp(alc)
    acs_c = _dot01("nn", tri, dtc * a_row, "a")
    acs_r = _dot01("nn", dtr * a_col, trit, "b")
    eh = lax.broadcasted_iota(jnp.int32, (HEAD_PAD, gw), 0)
    ec = lax.broadcasted_iota(jnp.int32, (HEAD_PAD, gw), 1) // HEAD_DIM
    expand = (eh == ec).astype(F32)
    th = lax.broadcasted_iota(jnp.int32, (gw, HEAD_PAD), 1)
    tc = lax.broadcasted_iota(jnp.int32, (gw, HEAD_PAD), 0) // HEAD_DIM
    reduce_ = (th == tc).astype(F32)
    acs_last = acs_c[CHUNK - 1:CHUNK, :]
    both = _dot01("nn", jnp.concatenate([dtc, acs_c], axis=0), expand, "b")
    acs_e = both[CHUNK:]
    return dict(li=li, si=si, tri=tri, trit=trit, a_row=a_row, acs_c=acs_c, acs_r=acs_r,
                reduce=reduce_, acs_last=acs_last, dt_e=both[:CHUNK],
                eacs_e=jnp.exp(acs_e), dec_e=jnp.exp(acs_e[CHUNK - 1:CHUNK, :] - acs_e),
                dec_h=jnp.exp(acs_last - acs_c))


def _ssd_specs(T, DI, gw, nc, rev):
    nsb = DI // D_STATE
    cidx = (lambda c: nc - 1 - c) if rev else (lambda c: c)
    return dict(
        xs=pl.BlockSpec((CHUNK, gw), lambda g, c: (cidx(c), g)),
        bm=pl.BlockSpec((CHUNK, D_STATE), lambda g, c: (cidx(c), nsb + g)),
        cm=pl.BlockSpec((CHUNK, D_STATE), lambda g, c: (cidx(c), nsb + N_GROUPS + g)),
        dtc=pl.BlockSpec((None, CHUNK, HEAD_PAD), lambda g, c: (g, cidx(c), 0)),
        dtr=pl.BlockSpec((None, ROW_PAD, CHUNK), lambda g, c: (g, 0, cidx(c))),
        alr=pl.BlockSpec((None, 1, HEAD_PAD), lambda g, c: (g, 0, 0)),
        alc=pl.BlockSpec((None, ROW_PAD, 1), lambda g, c: (g, 0, 0)),
        de=pl.BlockSpec((None, 1, gw), lambda g, c: (g, 0, 0)),
        hp=pl.BlockSpec((None, None, gw, D_STATE), lambda g, c: (cidx(c), g, 0, 0)),
        bc=pl.BlockSpec((CHUNK, D_STATE), lambda g, c: (cidx(c), g)),
        acc=pl.BlockSpec((None, 1, HEAD_PAD), lambda g, c: (g, 0, 0)),
    )


def _ssd_fwd(name, xbc, dtc, dtr, alr, alc, d_e, DI, deps=()):
    T = xbc.shape[0]
    nc = T // CHUNK
    gw = DI // N_GROUPS
    sp = _ssd_specs(T, DI, gw, nc, False)

    def body(xs_ref, b_ref, c_ref, dtc_ref, dtr_ref, alr_ref, alc_ref, de_ref, *rest):
        y_ref, hp_ref, h_scr = rest[len(deps):]

        @pl.when(pl.program_id(1) == 0)
        def _():
            h_scr[...] = jnp.zeros_like(h_scr)

        hpv = h_scr[...]
        hp_ref[...] = hpv
        xs = xs_ref[...]
        bb = b_ref[...].astype(BF16)
        cb_ = c_ref[...].astype(BF16)
        cm = _ssd_common(dtc_ref[...], dtr_ref[...], alr_ref[...], alc_ref[...], gw)
        x = xs * cm["dt_e"]
        xb = x.astype(BF16)
        cbm = _dotb("nt", cb_, bb)
        causal = cm["li"] >= cm["si"]
        second = lax.broadcasted_iota(jnp.int32, (1, HEAD_PAIR), 1) >= HEAD_DIM
        pieces = []
        for pr in range(gw // HEAD_PAIR):
            xp = xb[:, pr * HEAD_PAIR:(pr + 1) * HEAD_PAIR]
            for h2 in range(2):
                j = 2 * pr + h2
                seg = cm["acs_c"][:, j:j + 1] - cm["acs_r"][j:j + 1, :]
                lmat = jnp.exp(jnp.where(causal, jnp.minimum(seg, 0.0), -1e30))
                yj = _dotb("nn", cbm * lmat, xp)
                yp = yj if h2 == 0 else jnp.where(second, yj, yp)
            pieces.append(yp)
        ydiag = pieces[0] if len(pieces) == 1 else jnp.concatenate(pieces, axis=1)
        states = _dotb("tn", x * cm["dec_e"], bb)
        yoff = _dotb("nt", cb_, hpv) * cm["eacs_e"]
        y_ref[...] = ydiag + yoff + xs * de_ref[...]
        cd_rows = jnp.sum(cm["reduce"] * jnp.exp(cm["acs_last"]), axis=1, keepdims=True)
        h_scr[...] = cd_rows * hpv + states

    return pl.pallas_call(
        body, name=name, grid=(N_GROUPS, nc),
        in_specs=[sp["xs"], sp["bm"], sp["cm"], sp["dtc"], sp["dtr"], sp["alr"], sp["alc"], sp["de"]]
        + [pl.BlockSpec((8, 128), lambda g, c: (0, 0)) for _ in deps],
        out_specs=[sp["xs"], sp["hp"]],
        out_shape=[jax.ShapeDtypeStruct((T, DI), F32), jax.ShapeDtypeStruct((nc, N_GROUPS, gw, D_STATE), F32)],
        scratch_shapes=[pltpu.VMEM((gw, D_STATE), F32)],
        compiler_params=_cp("parallel", "arbitrary"),
    )(xbc, xbc, xbc, dtc, dtr, alr, alc, d_e, *deps)


def _ssd_bwd(name, dy, xbc, dtc, dtr, alr, alc, d_e, hprev, DI):
    T = xbc.shape[0]
    nc = T // CHUNK
    gw = DI // N_GROUPS
    sp = _ssd_specs(T, DI, gw, nc, True)

    def body(dy_ref, xs_ref, b_ref, c_ref, dtc_ref, dtr_ref, alr_ref, alc_ref, de_ref, hp_ref,
             dxs_ref, db_ref, dc_ref, ddt_ref, dal_ref, dd_ref, dh_scr):
        first = pl.program_id(1) == 0

        @pl.when(first)
        def _():
            dh_scr[...] = jnp.zeros_like(dh_scr)

        xs = xs_ref[...]
        dyv = dy_ref[...]
        bb = b_ref[...].astype(BF16)
        cb_ = c_ref[...].astype(BF16)
        dtc_v = dtc_ref[...]
        cm = _ssd_common(dtc_v, dtr_ref[...], alr_ref[...], alc_ref[...], gw)
        hpv = hp_ref[...]
        hpb = hpv.astype(BF16)
        dhn = dh_scr[...]
        dhnb = dhn.astype(BF16)
        x = xs * cm["dt_e"]
        xb = x.astype(BF16)
        cbm = _dotb("nt", cb_, bb)
        causal = cm["li"] >= cm["si"]
        second = lax.broadcasted_iota(jnp.int32, (1, HEAD_PAIR), 1) >= HEAD_DIM
        lane_pad = lax.broadcasted_iota(jnp.int32, (1, HEAD_PAD), 1)
        sub_pad = lax.broadcasted_iota(jnp.int32, (ROW_PAD, 1), 0)

        dxs = dyv * de_ref[...]

        dcb = jnp.zeros((CHUNK, CHUNK), F32)
        dacs_c = jnp.zeros((CHUNK, HEAD_PAD), F32)
        dacs_r = jnp.zeros((ROW_PAD, CHUNK), F32)
        pieces = []
        for pr in range(gw // HEAD_PAIR):
            xp = xb[:, pr * HEAD_PAIR:(pr + 1) * HEAD_PAIR]
            dyp = dyv[:, pr * HEAD_PAIR:(pr + 1) * HEAD_PAIR]
            for h2 in range(2):
                j = 2 * pr + h2
                seg = cm["acs_c"][:, j:j + 1] - cm["acs_r"][j:j + 1, :]
                lmat = jnp.exp(jnp.where(causal, jnp.minimum(seg, 0.0), -1e30))
                mmat = cbm * lmat
                dym = jnp.where(second if h2 == 1 else jnp.logical_not(second), dyp, 0.0).astype(BF16)
                dm = _dotb("nt", dym, xp)
                dxj = _dotb("tn", mmat, dym)
                dxp = dxj if h2 == 0 else dxp + dxj
                dcb = dcb + dm * lmat
                dseg = dm * mmat
                dacs_c = dacs_c + jnp.where(lane_pad == j, jnp.sum(dseg, axis=1, keepdims=True), 0.0)
                dacs_r = dacs_r - jnp.where(sub_pad == j, jnp.sum(dseg, axis=0, keepdims=True), 0.0)
            pieces.append(dxp)
        dx = pieces[0] if len(pieces) == 1 else jnp.concatenate(pieces, axis=1)
        dc = _dotb("nn", dcb, bb)
        db = _dotb("tn", dcb, cb_)

        gm = _dotb("nt", cb_, hpb)
        dgm = dyv * cm["eacs_e"]
        dc = dc + _dotb("nn", dgm, hpb)
        dhp = _dotb("tn", dgm, cb_)

        cd_row = jnp.exp(cm["acs_last"])
        cd_rows = jnp.sum(cm["reduce"] * cd_row, axis=1, keepdims=True)
        dhp = dhp + cd_rows * dhn
        rsum = jnp.sum(dhn * hpv, axis=1, keepdims=True)
        dacs_last = jnp.sum(cm["reduce"] * rsum, axis=0, keepdims=True) * cd_row
        dec_e = cm["dec_e"]
        xdec = x * dec_e
        dxdec = _dotb("nt", bb, dhnb)
        db = db + _dotb("nn", xdec, dhnb)
        dx = dx + dxdec * dec_e
        sums = _dot01("nn", jnp.concatenate([dyv * xs, dgm * gm, dxdec * x, dx * xs], axis=0), cm["reduce"], "b")
        _accum(dd_ref, jnp.sum(sums[:CHUNK], axis=0, keepdims=True), first)
        tdec = sums[2 * CHUNK:3 * CHUNK] * cm["dec_h"]
        dacs_c = dacs_c + sums[CHUNK:2 * CHUNK] - tdec
        dacs_last = dacs_last + jnp.sum(tdec, axis=0, keepdims=True)
        row_id = lax.broadcasted_iota(jnp.int32, (CHUNK, 1), 0)
        dacs_c = dacs_c + jnp.where(row_id == CHUNK - 1, dacs_last, 0.0)

        dxs_ref[...] = dxs + dx * cm["dt_e"]
        ddt = sums[3 * CHUNK:]
        dda = _dot01("nn", cm["trit"], dacs_c, "a")
        dda_r = _dot01("nn", dacs_r, cm["tri"], "b")
        dda_rp = jnp.concatenate([dda_r, jnp.zeros((HEAD_PAD - ROW_PAD, CHUNK), F32)], axis=0)
        eye = (cm["li"] == cm["si"]).astype(F32)
        dda = dda + _dot01("nt", eye, dda_rp, "a")
        ddt_ref[...] = ddt + dda * cm["a_row"]
        _accum(dal_ref, jnp.sum(dda * dtc_v, axis=0, keepdims=True) * cm["a_row"], first)
        db_ref[...] = db
        dc_ref[...] = dc
        dh_scr[...] = dhp

    gs = D_STATE * N_GROUPS
    return pl.pallas_call(
        body, name=name, grid=(N_GROUPS, nc),
        in_specs=[sp["xs"], sp["xs"], sp["bm"], sp["cm"], sp["dtc"], sp["dtr"], sp["alr"], sp["alc"],
                  sp["de"], sp["hp"]],
        out_specs=[sp["xs"], sp["bc"], sp["bc"], sp["dtc"], sp["acc"], sp["acc"]],
        out_shape=[jax.ShapeDtypeStruct((T, DI), F32), jax.ShapeDtypeStruct((T, gs), F32),
                   jax.ShapeDtypeStruct((T, gs), F32), jax.ShapeDtypeStruct((N_GROUPS, T, HEAD_PAD), F32),
                   jax.ShapeDtypeStruct((N_GROUPS, 1, HEAD_PAD), F32),
                   jax.ShapeDtypeStruct((N_GROUPS, 1, HEAD_PAD), F32)],
        scratch_shapes=[pltpu.VMEM((gw, D_STATE), F32)],
        compiler_params=_cp("parallel", "arbitrary"),
    )(dy, xbc, xbc, xbc, dtc, dtr, alr, alc, d_e, hprev)


def _adam_math(w, g, m, v):
    m = ADAM_B1 * m + (1.0 - ADAM_B1) * g
    v = ADAM_B2 * v + (1.0 - ADAM_B2) * (g * g)
    m_hat = m / (1.0 - ADAM_B1 ** ADAM_STEP)
    v_hat = v / (1.0 - ADAM_B2 ** ADAM_STEP)
    delta = -ADAM_LR * (m_hat / (jnp.sqrt(v_hat) + ADAM_EPS) + ADAM_WD * w)
    return delta, m, v


def _adamw(name, w, m, v, g, layer=0, prev=None):
    L, R, C = w.shape
    tr = _rows(R, C * 4, ADAM_BLOCK_BYTES)
    tc = C
    while tr * tc * 4 > ADAM_BLOCK_BYTES and tc % 256 == 0:
        tc //= 2
    blk = pl.BlockSpec((None, tr, tc), lambda i, j: (layer, i, j))
    prev = [] if prev is None else list(prev)

    def body(w_ref, m_ref, v_ref, g_ref, *rest):
        go_ref, d_ref, mo_ref, vo_ref = rest[len(prev):]
        gv = g_ref[...]
        delta, mn, vn = _adam_math(w_ref[...], gv, m_ref[...], v_ref[...])
        go_ref[...] = gv
        d_ref[...] = delta
        mo_ref[...] = mn
        vo_ref[...] = vn

    return pl.pallas_call(
        body, name=name, grid=(R // tr, C // tc),
        in_specs=[blk] * 3 + [pl.BlockSpec((tr, tc), lambda i, j: (i, j))] + [_ANY] * len(prev),
        out_specs=[blk] * 4, out_shape=[jax.ShapeDtypeStruct((L, R, C), F32)] * 4,
        input_output_aliases={4 + k: k for k in range(len(prev))},
        compiler_params=_cp("parallel", "parallel"),
    )(w, m, v, g, *prev)


def _sum_peers(name, gathered):
    n, R, C = gathered.shape

    def body(g_ref, o_ref):
        acc = g_ref[0]
        for d in range(1, n):
            acc = acc + g_ref[d]
        o_ref[...] = acc

    return pl.pallas_call(body, name=name, out_shape=jax.ShapeDtypeStruct((R, C), F32))(gathered)


def _place():
    x, y, c = lax.axis_index("x"), lax.axis_index("y"), lax.axis_index("c")
    chips = [(1 - x, y), (x, 1 - y), (1 - x, 1 - y)]
    return x, y, c, chips


def _allgather_small(name, v, after=()):
    R, C = v.shape
    after = list(after)

    def body(v_ref, *rest):
        o_ref, send_sems, recv_sems = rest[len(after):]
        x, y, c, _ = _place()
        me = 4 * x + 2 * y + c
        o_ref[me] = v_ref[...]
        copies = []
        for k in range(1, N_DEV):
            px, py, pc = x ^ (k >> 2), y ^ ((k >> 1) & 1), c ^ (k & 1)
            copies.append(pltpu.make_async_remote_copy(
                src_ref=v_ref, dst_ref=o_ref.at[me], send_sem=send_sems.at[k - 1], recv_sem=recv_sems.at[k - 1],
                device_id=(px, py, pc), device_id_type=MESH))
        for cp in copies:
            cp.start()
        for cp in copies:
            cp.wait()

    return pl.pallas_call(
        body, name=name, out_shape=jax.ShapeDtypeStruct((N_DEV, R, C), F32),
        in_specs=[pl.BlockSpec(memory_space=pltpu.VMEM)] + [_ANY] * len(after),
        out_specs=pl.BlockSpec(memory_space=pltpu.VMEM),
        scratch_shapes=[pltpu.SemaphoreType.DMA((N_DEV - 1,)), pltpu.SemaphoreType.DMA((N_DEV - 1,))],
    )(v, *after)


_HBM = pl.BlockSpec(memory_space=pltpu.HBM)
_SEM = pl.BlockSpec(memory_space=pltpu.SEMAPHORE)
_VMEM = pl.BlockSpec(memory_space=pltpu.VMEM)
_EFFECT = pltpu.SideEffectType.DATAFLOW_SIDE_EFFECTING


def _in_hbm(v):
    return pltpu.with_memory_space_constraint(v, pltpu.HBM)


def _remote(src, dst, send_sem, recv_sem, device):
    return pltpu.make_async_remote_copy(src_ref=src, dst_ref=dst, send_sem=send_sem, recv_sem=recv_sem,
                                        device_id=device, device_id_type=MESH)


def _gather_copies(src, land, ici_s, ici_r, own_s, own_r, arrivals=True):
    x, y, c, chips = _place()
    a = 2 * x + y
    sends, lands = [], []
    for i in range(len(src)):
        own = _remote(src[i], land[i].at[a], own_s.at[i], own_r.at[i], (x, y, 1 - c))
        sends.append(own)
        if arrivals:
            lands.append(own)
        for k, (px, py) in enumerate(chips):
            s, r = ici_s.at[3 * i + k], ici_r.at[3 * i + k]
            sends.append(_remote(src[i].at[c], land[i].at[a, c], s, r, (px, py, c)))
            if arrivals:
                lands.append(_remote(src[i].at[c], land[i].at[2 * px + py, c], s, r, (px, py, c)))
    return sends, lands


def _gather_start(name, shards, after=None):
    n = len(shards)
    lands = [lax.empty((N_CHIPS,) + s.shape, s.dtype) for s in shards]
    n_in = 2 * n + (0 if after is None else 1)

    def body(*refs):
        src, land = refs[:n], refs[n:2 * n]
        sems = refs[n_in:n_in + 4]
        token = refs[-1]
        sends, _ = _gather_copies(src, land, *sems, arrivals=False)
        for cp in sends:
            cp.start()
        token[...] = jnp.zeros_like(token)

    outs = pl.pallas_call(
        body, name=name,
        out_shape=(pltpu.SemaphoreType.DMA((3 * n,)), pltpu.SemaphoreType.DMA((3 * n,)),
                   pltpu.SemaphoreType.DMA((n,)), pltpu.SemaphoreType.DMA((n,)),
                   *[pltpu.HBM(s.shape, s.dtype) for s in shards], *[pltpu.HBM(l.shape, l.dtype) for l in lands],
                   jax.ShapeDtypeStruct((8, 128), F32)),
        in_specs=[_HBM] * (2 * n) + ([] if after is None else [_ANY]),
        out_specs=(_SEM,) * 4 + (_HBM,) * (2 * n) + (_VMEM,),
        input_output_aliases={i: 4 + i for i in range(2 * n)},
        compiler_params=pltpu.CompilerParams(has_side_effects=_EFFECT),
    )(*[_in_hbm(s) for s in shards], *[_in_hbm(l) for l in lands], *([] if after is None else [after]))
    return dict(sems=outs[:4], src=outs[4:4 + n], land=outs[4 + n:4 + 2 * n], token=outs[-1])


def _gather_wait(name, started, after):
    n = len(started["src"])
    after = list(after)

    def body(*refs):
        src, land = refs[:n], refs[n:2 * n]
        sems = refs[2 * n:2 * n + 4]
        sends, lands = _gather_copies(src, land, *sems)
        for cp in sends:
            cp.wait_send()
        for cp in lands:
            cp.wait_recv()

    outs = pl.pallas_call(
        body, name=name,
        out_shape=[pltpu.HBM(v.shape, v.dtype) for v in list(started["src"]) + list(started["land"])],
        in_specs=[_HBM] * (2 * n) + [_SEM] * 4 + [_ANY] * len(after), out_specs=[_HBM] * (2 * n),
        input_output_aliases={i: i for i in range(2 * n)},
        compiler_params=pltpu.CompilerParams(has_side_effects=_EFFECT),
    )(*started["src"], *started["land"], *started["sems"], *after)
    return outs[n:]


def _gather_forward(name, lands):
    n = len(lands)

    def body(*refs):
        buf = refs[n:2 * n]
        send_sems, recv_sems = refs[2 * n:]
        x, y, c, chips = _place()
        sends, lands_ = [], []
        for i in range(n):
            for k, (px, py) in enumerate(chips):
                b = 2 * px + py
                sends.append(_remote(buf[i].at[b, c], buf[i].at[b, c], send_sems.at[i, k], recv_sems.at[i, k],
                                     (x, y, 1 - c)))
                lands_.append(_remote(buf[i].at[b, 1 - c], buf[i].at[b, 1 - c], send_sems.at[i, k],
                                      recv_sems.at[i, k], (x, y, 1 - c)))
        for cp in sends:
            cp.start()
        for cp in sends:
            cp.wait_send()
        for cp in lands_:
            cp.wait_recv()

    return pl.pallas_call(
        body, name=name, in_specs=[_ANY] * n, out_specs=[_ANY] * n,
        out_shape=[jax.ShapeDtypeStruct(l.shape, l.dtype) for l in lands],
        input_output_aliases={i: i for i in range(n)},
        scratch_shapes=[pltpu.SemaphoreType.DMA((n, 3)), pltpu.SemaphoreType.DMA((n, 3))],
    )(*lands)


def _split_start(name, bufs, copies, n_sems, after=()):
    nb = len(bufs)
    after = list(after)

    def body(*refs):
        send_sems, recv_sems = refs[nb + len(after):nb + len(after) + 2]
        token = refs[-1]
        sends, _ = copies(refs[:nb], send_sems, recv_sems, False)
        for cp in sends:
            cp.start()
        token[...] = jnp.zeros_like(token)

    outs = pl.pallas_call(
        body, name=name,
        out_shape=(pltpu.SemaphoreType.DMA((n_sems,)), pltpu.SemaphoreType.DMA((n_sems,)),
                   *[pltpu.HBM(b.shape, b.dtype) for b in bufs], jax.ShapeDtypeStruct((8, 128), F32)),
        in_specs=[_HBM] * nb + [_ANY] * len(after), out_specs=(_SEM,) * 2 + (_HBM,) * nb + (_VMEM,),
        input_output_aliases={i: 2 + i for i in range(nb)},
        compiler_params=pltpu.CompilerParams(has_side_effects=_EFFECT),
    )(*[_in_hbm(b) for b in bufs], *after)
    return dict(sems=outs[:2], bufs=outs[2:2 + nb], token=outs[-1])


def _split_wait(name, started, copies, after):
    nb = len(started["bufs"])
    after = list(after)

    def body(*refs):
        sends, arrivals = copies(refs[:nb], refs[nb], refs[nb + 1], True)
        for cp in sends:
            cp.wait_send()
        for cp in arrivals:
            cp.wait_recv()

    return pl.pallas_call(
        body, name=name, out_shape=[pltpu.HBM(b.shape, b.dtype) for b in started["bufs"]],
        in_specs=[_HBM] * nb + [_SEM] * 2 + [_ANY] * len(after), out_specs=[_HBM] * nb,
        input_output_aliases={i: i for i in range(nb)},
        compiler_params=pltpu.CompilerParams(has_side_effects=_EFFECT),
    )(*started["bufs"], *started["sems"], *after)


def _forward_copies(bufs, send_sems, recv_sems, arrivals):
    x, y, c, chips = _place()
    sends, lands = [], []
    for i, buf in enumerate(bufs):
        for k, (px, py) in enumerate(chips):
            b, s, r = 2 * px + py, send_sems.at[3 * i + k], recv_sems.at[3 * i + k]
            sends.append(_remote(buf.at[b, c], buf.at[b, c], s, r, (x, y, 1 - c)))
            if arrivals:
                lands.append(_remote(buf.at[b, 1 - c], buf.at[b, 1 - c], s, r, (x, y, 1 - c)))
    return sends, lands


def _pair_copies(bufs, send_sems, recv_sems, arrivals):
    x, y, c, _ = _place()
    n = len(bufs) // 2
    copies = [_remote(bufs[i].at[:, 1 - c], bufs[n + i], send_sems.at[i], recv_sems.at[i], (x, y, 1 - c))
              for i in range(n)]
    return copies, copies


def _chip_copies(src, land, send_sems, recv_sems):
    x, y, c, chips = _place()
    return [_remote(src[i].at[2 * px + py], land[i].at[k], send_sems.at[3 * i + k], recv_sems.at[3 * i + k],
                    (px, py, c))
            for i in range(len(src)) for k, (px, py) in enumerate(chips)]


def _chip_start(name, parts):
    n = len(parts)
    lands = [lax.empty((3,) + p.shape[1:], p.dtype) for p in parts]

    def body(*refs):
        src, land = refs[:n], refs[n:2 * n]
        send_sems, recv_sems = refs[2 * n:2 * n + 2]
        token = refs[-1]
        for cp in _chip_copies(src, land, send_sems, recv_sems):
            cp.start()
        token[...] = jnp.zeros_like(token)

    outs = pl.pallas_call(
        body, name=name,
        out_shape=(pltpu.SemaphoreType.DMA((3 * n,)), pltpu.SemaphoreType.DMA((3 * n,)),
                   *[pltpu.HBM(p.shape, p.dtype) for p in parts], *[pltpu.HBM(l.shape, l.dtype) for l in lands],
                   jax.ShapeDtypeStruct((8, 128), F32)),
        in_specs=[_HBM] * (2 * n), out_specs=(_SEM,) * 2 + (_HBM,) * (2 * n) + (_VMEM,),
        input_output_aliases={i: 2 + i for i in range(2 * n)},
        compiler_params=pltpu.CompilerParams(has_side_effects=_EFFECT),
    )(*[_in_hbm(p) for p in parts], *[_in_hbm(l) for l in lands])
    return dict(sems=outs[:2], src=outs[2:2 + n], land=outs[2 + n:2 + 2 * n], token=outs[-1])


def _chip_wait(name, started, after):
    n = len(started["src"])
    after = list(after)

    def body(*refs):
        src, land = refs[:n], refs[n:2 * n]
        send_sems, recv_sems = refs[2 * n:2 * n + 2]
        copies = _chip_copies(src, land, send_sems, recv_sems)
        for cp in copies:
            cp.wait_send()
        for cp in copies:
            cp.wait_recv()

    outs = pl.pallas_call(
        body, name=name,
        out_shape=[pltpu.HBM(v.shape, v.dtype) for v in list(started["src"]) + list(started["land"])],
        in_specs=[_HBM] * (2 * n) + [_SEM] * 2 + [_ANY] * len(after), out_specs=[_HBM] * (2 * n),
        input_output_aliases={i: i for i in range(2 * n)},
        compiler_params=pltpu.CompilerParams(has_side_effects=_EFFECT),
    )(*started["src"], *started["land"], *started["sems"], *after)
    return outs[:n], outs[n:]


def _half_exchange(name, shards):
    n = len(shards)

    def body(*refs):
        buf = refs[n:2 * n]
        send_sems, recv_sems = refs[2 * n:]
        x, y, c, _ = _place()
        copies = [pltpu.make_async_remote_copy(
            src_ref=buf[i].at[c], dst_ref=buf[i].at[c], send_sem=send_sems.at[i], recv_sem=recv_sems.at[i],
            device_id=(x, y, 1 - c), device_id_type=MESH) for i in range(n)]
        for cp in copies:
            cp.start()
        for cp in copies:
            cp.wait()

    return pl.pallas_call(
        body, name=name, in_specs=[_ANY] * n, out_specs=[_ANY] * n,
        out_shape=[jax.ShapeDtypeStruct(s.shape, s.dtype) for s in shards],
        input_output_aliases={i: i for i in range(n)},
        scratch_shapes=[pltpu.SemaphoreType.DMA((n,)), pltpu.SemaphoreType.DMA((n,))],
    )(*shards)


def _pair_add(name, pos, part, sib):
    Q, _, R2, C = part.shape
    tr = _pick(R2, 256)

    def body(pos_ref, p_ref, s_ref, o_ref):
        o_ref[...] = (p_ref[...].astype(F32) + s_ref[...].astype(F32)).astype(BF16)

    return pl.pallas_call(
        body, name=name,
        grid_spec=pltpu.PrefetchScalarGridSpec(
            num_scalar_prefetch=1, grid=(Q, R2 // tr),
            in_specs=[pl.BlockSpec((None, None, tr, C), lambda q, i, pos_ref: (q, pos_ref[1], i, 0)),
                      pl.BlockSpec((None, tr, C), lambda q, i, pos_ref: (q, i, 0))],
            out_specs=pl.BlockSpec((None, tr, C), lambda q, i, pos_ref: (q, i, 0))),
        out_shape=jax.ShapeDtypeStruct((Q, R2, C), BF16),
        compiler_params=_cp("parallel", "parallel"),
    )(pos, part, sib)


def _chip_sum(name, pos, own, got):
    _, R2, C = own.shape
    tr = _pick(R2, 256)

    def body(pos_ref, o_ref, g_ref, out_ref):
        acc = o_ref[...].astype(F32)
        for k in range(3):
            acc = acc + g_ref[k].astype(F32)
        out_ref[...] = acc

    return pl.pallas_call(
        body, name=name,
        grid_spec=pltpu.PrefetchScalarGridSpec(
            num_scalar_prefetch=1, grid=(R2 // tr,),
            in_specs=[pl.BlockSpec((None, tr, C), lambda i, pos_ref: (pos_ref[0], i, 0)),
                      pl.BlockSpec((3, tr, C), lambda i, pos_ref: (0, i, 0))],
            out_specs=pl.BlockSpec((None, tr, C), lambda i, pos_ref: (pos_ref[1], i, 0))),
        out_shape=jax.ShapeDtypeStruct((2, R2, C), F32),
        compiler_params=_cp("parallel"),
    )(pos, own, got)


def _reduce_pair_start(tag, parts):
    split = [p.reshape(p.shape[0], 2, p.shape[1] // 2, p.shape[2]) for p in parts]
    lands = [lax.empty((p.shape[0],) + p.shape[2:], p.dtype) for p in split]
    return _split_start(tag + "_pair_start", split + lands, _pair_copies, len(parts))


def _reduce_pair_end(tag, pos, started, after):
    n = len(started["bufs"]) // 2
    bufs = _split_wait(tag + "_pair_wait", started, _pair_copies, after)
    chip = [_pair_add(f"{tag}_add{i}", pos, bufs[i], bufs[n + i]) for i in range(n)]
    return _chip_start(tag + "_chip_start", chip)


def _half_copies(bufs, send_sems, recv_sems, arrivals):
    x, y, c, _ = _place()
    sends = [_remote(b.at[c], b.at[c], send_sems.at[i], recv_sems.at[i], (x, y, 1 - c)) for i, b in enumerate(bufs)]
    lands = [_remote(b.at[1 - c], b.at[1 - c], send_sems.at[i], recv_sems.at[i], (x, y, 1 - c))
             for i, b in enumerate(bufs)] if arrivals else []
    return sends, lands


def _reduce_sum_start(tag, pos, started, after):
    chip, got = _chip_wait(tag + "_chip_wait", started, after)
    halves = [_chip_sum(f"{tag}_sum{i}", pos, chip[i], got[i]) for i in range(len(chip))]
    return _split_start(tag + "_half_start", halves, _half_copies, len(halves))


def _reduce_sum_end(tag, started, after, shapes):
    full = _split_wait(tag + "_half_wait", started, _half_copies, after)
    return [f.reshape(shp) for f, shp in zip(full, shapes)]


def _reduce_end(tag, pos, started, after, shapes):
    chip, got = _chip_wait(tag + "_chip_wait", started, after)
    halves = [_chip_sum(f"{tag}_sum{i}", pos, chip[i], got[i]) for i in range(len(chip))]
    full = _half_exchange(tag + "_half", halves)
    return [f.reshape(shp) for f, shp in zip(full, shapes)]


def _pad_to(v, mult):
    n = v.shape[0]
    return jnp.pad(v, (0, (-n) % mult))


def kernel(x, p, pool_w, pool_scale, ssm_in_w, ssm_conv_w, ssm_conv_b, ssm_dt_bias, ssm_a_log, ssm_d, ssm_norm_w, ssm_out_w, mlp_w1, mlp_w2, ln_g, ln_b, ple_w, ple_gate_w, loss_target, m_pool_w, m_pool_scale, m_ssm_in_w, m_ssm_conv_w, m_ssm_conv_b, m_ssm_dt_bias, m_ssm_a_log, m_ssm_d, m_ssm_norm_w, m_ssm_out_w, m_mlp_w1, m_mlp_w2, m_ln_g, m_ln_b, m_ple_w, m_ple_gate_w, v_pool_w, v_pool_scale, v_ssm_in_w, v_ssm_conv_w, v_ssm_conv_b, v_ssm_dt_bias, v_ssm_a_log, v_ssm_d, v_ssm_norm_w, v_ssm_out_w, v_mlp_w1, v_mlp_w2, v_ln_g, v_ln_b, v_ple_w, v_ple_gate_w):
    T, D = x.shape[1], x.shape[2]
    NG = len(POOL_WINDOWS)
    GD = D // NG
    DI = ssm_out_w.shape[1] * N_CHIPS
    H = ssm_dt_bias.shape[1]
    HPG = H // N_GROUPS
    GW = DI // N_GROUPS
    GS = N_GROUPS * D_STATE
    CD = DI + 2 * GS
    DF = mlp_w1.shape[2] * N_CHIPS
    PD = ple_w.shape[1]
    NIN = ssm_in_w.shape[2]

    xi, yi, ci = lax.axis_index("x"), lax.axis_index("y"), lax.axis_index("c")
    chip = 2 * xi + yi
    pos = jnp.stack([chip, ci]).astype(jnp.int32)

    x0 = x[0]
    tgt = loss_target[0]
    p0b, p1b = p[0, 0].astype(BF16), p[1, 0].astype(BF16)

    small_sh = jnp.concatenate([ssm_conv_w[0].reshape(-1), ssm_conv_b[0], ssm_norm_w[0],
                                ln_g.reshape(-1), ln_b.reshape(-1)])
    n_sh = small_sh.shape[0]
    small_all = _allgather_small("gather_small", _pad_to(small_sh, 1024).reshape(-1, 128))

    def halves(w, zero=None):
        w = w if zero is None else w + zero
        return w.astype(BF16).reshape((2, w.shape[0] // 2) + w.shape[1:])

    sh_pool = pool_w[0].astype(BF16)
    sh_pool = sh_pool.reshape((2, NG // 2) + sh_pool.shape[1:])
    started = {"ag0a": _gather_start("ag0a_start", [sh_pool, halves(mlp_w1[0])], after=small_all)}
    z0 = started["ag0a"]["token"][0, 0]
    groups = [("ag0b", [halves(mlp_w2[0], z0)]),
              ("ag0c", [halves(ple_gate_w[0], z0), halves(ple_w[0], z0)]),
              ("ag1a", [halves(ssm_in_w[0], z0)]),
              ("ag1b", [halves(ssm_out_w[0], z0), halves(mlp_w1[1], z0)]),
              ("ag1c", [halves(mlp_w2[1], z0), halves(ple_gate_w[1], z0), halves(ple_w[1], z0)])]

    def gather_end(tag, after):
        return _gather_forward(tag + "_fwd", _gather_wait(tag + "_wait", started[tag], after))

    def gather_land(tag, after):
        lands = _gather_wait(tag + "_wait", started[tag], after)
        return _split_start(tag + "_fwd_start", lands, _forward_copies, 3 * len(lands))

    def gather_done(tag, forwarding, after):
        return _split_wait(tag + "_fwd_wait", forwarding, _forward_copies, after)

    g_pool, g = gather_end("ag0a", [sh for _, shards in groups for sh in shards])
    prev = g_pool
    for tag, shards in groups:
        started[tag] = _gather_start(tag + "_start", shards, after=prev)
        prev = started[tag]["token"]
    all_started = prev

    g_w1, g_w2, g_pw, g_gw = {}, {}, {}, {}

    def set_w1(l, g):
        g_w1[l] = g.reshape(N_CHIPS, 1, D, DF // N_CHIPS)

    def set_w2(l, g2):
        g_w2[l] = g2.reshape(N_CHIPS, 1, DF // N_CHIPS, D)

    def set_gate(l, ggw, gpw):
        g_gw[l] = ggw.reshape(N_CHIPS, 1, D // N_CHIPS, D)
        g_pw[l] = gpw.reshape(N_CHIPS, 1, PD, D // N_CHIPS)

    set_w1(0, g)
    w_pool = jnp.transpose(g_pool.reshape(N_CHIPS, NG, GD // N_CHIPS, GD), (1, 0, 2, 3)).reshape(NG, GD, GD)

    small_all = small_all.reshape(N_DEV, -1)[0::2, :n_sh]
    cdq, niq, dq = CD // N_CHIPS, DI // N_CHIPS, D // N_CHIPS
    o = 0
    conv_w = jnp.concatenate([small_all[q, o:o + CONV_K * cdq].reshape(CONV_K, cdq) for q in range(N_CHIPS)], axis=1)
    o += CONV_K * cdq
    conv_b = small_all[:, o:o + cdq].reshape(1, CD)
    o += cdq
    norm_w = small_all[:, o:o + niq].reshape(1, DI)
    o += niq
    lng = jnp.transpose(small_all[:, o:o + 4 * dq].reshape(N_CHIPS, 2, 2, dq), (1, 2, 0, 3)).reshape(2, 2, 1, D)
    o += 4 * dq
    lnb = jnp.transpose(small_all[:, o:o + 4 * dq].reshape(N_CHIPS, 2, 2, dq), (1, 2, 0, 3)).reshape(2, 2, 1, D)

    pooled = _pool_fwd("pool_fwd", x0)
    hraw = _mm("pool_mm", "nn", pooled, w_pool, T, D, GD, tn=GD, deps=[all_started],
               a_spec=lambda tm, tn, tk: pl.BlockSpec((tm, tk), lambda i, j, k: (i, j)),
               b_spec=lambda tm, tn, tk: pl.BlockSpec((None, tk, tn), lambda i, j, k: (j, 0, 0)))
    x1, x1b, xh1, rs1 = _res_ln("ln00", x0, hraw, lng[0, 0], lnb[0, 0], scale=pool_scale)

    def mlp_fwd(l, xb, land_rest, deps=()):
        a, h2b = _mm(f"mlp{l}_up", "nn", xb, g_w1[l], T, DF, D, tn=min(1024, DF // N_CHIPS), deps=deps,
                     b_spec=_colshard_b(0, DF // N_CHIPS), out_dtype=(F32, BF16),
                     epi=lambda acc: (acc, jnp.square(jnp.maximum(acc, 0.0))))
        land_rest(a)
        h = _mm(f"mlp{l}_down", "nn", h2b, g_w2[l], T, D, DF, k_unit=DF // N_CHIPS,
                b_spec=_rowshard_b(0, DF // N_CHIPS))
        return a, h2b, h

    def ple_fwd(l, xb, pb):
        gl = _mm(f"gate{l}_logit", "nn", xb, g_gw[l], T, D, D, k_unit=D // N_CHIPS,
                 b_spec=_rowshard_b(0, D // N_CHIPS))
        e = _mm(f"gate{l}_emb", "nn", pb, g_pw[l], T, D, PD, tn=min(1024, D // N_CHIPS),
                b_spec=_colshard_b(0, D // N_CHIPS))
        return gl, e

    a0, h2b0, h0 = mlp_fwd(0, x1b, lambda a: set_w2(0, *gather_end("ag0b", [a])))
    set_gate(0, *gather_end("ag0c", [h0]))
    x2, x2b, xh2, rs2 = _res_ln("ln01", x1, h0, lng[0, 1], lnb[0, 1])
    gl0, e0 = ple_fwd(0, x2b, p0b)
    x3, x3b, gate0 = _gate_fwd("gate0", x2, gl0, e0)

    g_in, = gather_end("ag1a", [x3b])

    def in_proj_cols(lo, hi):
        parts = [g_in[q].reshape(D, NIN)[:, max(lo - q * NIN, 0):min(hi - q * NIN, NIN)]
                 for q in range(N_CHIPS) if lo < (q + 1) * NIN and hi > q * NIN]
        return parts[0] if len(parts) == 1 else jnp.concatenate(parts, axis=1)

    w_z, w_xbc, w_dt = in_proj_cols(0, DI), in_proj_cols(DI, DI + CD), in_proj_cols(DI + CD, N_CHIPS * NIN)
    z = _mm("ssm_in_z", "nn", x3b, w_z, T, DI, D)
    xbc_pre = _mm("ssm_in_xbc", "nn", x3b, w_xbc, T, CD, D)
    dt_pre = _mm("ssm_in_dt", "nn", x3b, w_dt, T, H, D)
    xbc = _conv_fwd("conv_fwd", xbc_pre, conv_w, conv_b)
    dt = _dt_fwd("dt_fwd", dt_pre, ssm_dt_bias)
    dt_g = jnp.transpose(dt.reshape(T, N_GROUPS, HPG), (1, 0, 2))
    dtc = jnp.pad(dt_g, ((0, 0), (0, 0), (0, HEAD_PAD - HPG)))
    dtr = jnp.pad(jnp.transpose(dt_g, (0, 2, 1)), ((0, 0), (0, ROW_PAD - HPG), (0, 0)))
    al_g = ssm_a_log.reshape(N_GROUPS, HPG)
    alr = jnp.pad(al_g, ((0, 0), (0, HEAD_PAD - HPG)))[:, None, :]
    alc = jnp.pad(al_g, ((0, 0), (0, ROW_PAD - HPG)))[:, :, None]
    d_e = jnp.repeat(ssm_d.reshape(N_GROUPS, HPG), HEAD_DIM, axis=1)[:, None, :]
    fw1b = gather_land("ag1b", [xbc])
    ysc, hprev = _ssd_fwd("ssd_fwd", xbc, dtc, dtr, alr, alc, d_e, DI, deps=[fw1b["token"]])
    g_out, g = gather_done("ag1b", fw1b, [ysc])
    set_w1(1, g)
    w_out = g_out.reshape(DI, D)
    ynb, rsn = _gnorm_fwd("gnorm_fwd", ysc, z, norm_w)
    h1 = _mm("ssm_out", "nn", ynb, w_out, T, D, DI)
    fw1c = gather_land("ag1c", [h1])
    x4, x4b, xh4, rs4 = _res_ln("ln10", x3, h1, lng[1, 0], lnb[1, 0])

    def land_1c(a):
        g2, ggw, gpw = gather_done("ag1c", fw1c, [a])
        set_w2(1, g2)
        set_gate(1, ggw, gpw)

    a1, h2b1, h2 = mlp_fwd(1, x4b, land_1c, deps=[fw1c["token"]])
    x5, x5b, xh5, rs5 = _res_ln("ln11", x4, h2, lng[1, 1], lnb[1, 1])
    gl1, e1 = ple_fwd(1, x5b, p1b)
    dx6, gate1, loss_parts = _gate_loss("gate1_loss", x5, gl1, e1, tgt)
    loss_local = jnp.sum(loss_parts[0::8, 0])

    def ple_bwd(l, dxo, gate, e, xb, pb, deps=()):
        dgl, de = _gate_bwd(f"gate{l}_bwd", dxo, gate, e)
        d_gw = _mm(f"gate{l}_dw", "tn", xb, dgl, D, D, T, out_dtype=BF16, deps=deps).reshape(N_CHIPS, D // N_CHIPS, D)
        d_pw = _mm(f"gate{l}_dpw", "tn", pb, de, PD, D, T, out_dtype=BF16, tn=min(1024, D // N_CHIPS),
                   o_shape=(N_CHIPS, PD, D // N_CHIPS), o_spec=_colshard_o(D // N_CHIPS))
        dx = _mm(f"gate{l}_dx", "nt", dgl, g_gw[l], T, D, D, tn=min(1024, D // N_CHIPS), deps=deps,
                 b_spec=_rowshard_bt(0, D // N_CHIPS), epi=lambda acc, r: acc + r, extras=[dxo])
        return dx, d_gw, d_pw

    def mlp_bwd(l, du, dub, a, h2b, xb):
        d_w2 = _mm(f"mlp{l}_dw2", "tn", h2b, dub, DF, D, T, out_dtype=BF16).reshape(N_CHIPS, DF // N_CHIPS, D)
        da = _mm(f"mlp{l}_da", "nt", dub, g_w2[l], T, DF, D, tn=min(1024, DF // N_CHIPS),
                 b_spec=_rowshard_bt(0, DF // N_CHIPS), out_dtype=BF16,
                 epi=lambda acc, av: acc * (2.0 * jnp.maximum(av, 0.0)), extras=[a])
        d_w1 = _mm(f"mlp{l}_dw1", "tn", xb, da, D, DF, T, out_dtype=BF16, tn=min(1024, DF // N_CHIPS),
                   o_shape=(N_CHIPS, D, DF // N_CHIPS), o_spec=_colshard_o(DF // N_CHIPS))
        dx = _mm(f"mlp{l}_dx", "nt", da, g_w1[l], T, D, DF, k_unit=DF // N_CHIPS,
                 b_spec=_colshard_bt(0, DF // N_CHIPS), epi=lambda acc, r: acc + ALPHA * r, extras=[du])
        return dx, d_w1, d_w2

    dx5, d_gw1, d_pw1 = ple_bwd(1, dx6, gate1, e1, x5b, p1b)
    du5, du5b, dg11, db11 = _ln_bwd("ln11_bwd", dx5, xh5, rs5, lng[1, 1])
    dx4, d_w1_1, d_w2_1 = mlp_bwd(1, du5, du5b, a1, h2b1, x4b)
    parts1a = [d_w1_1, d_w2_1, d_gw1, d_pw1]
    pair1a = _reduce_pair_start("rs1a", parts1a)
    du4, du4b, dg10, db10 = _ln_bwd("ln10_bwd", dx4, xh4, rs4, lng[1, 0])
    d_wout = _mm("ssm_out_dw", "tn", ynb, du4b, DI, D, T, out_dtype=BF16,
                 deps=[pair1a["token"]]).reshape(N_CHIPS, DI // N_CHIPS, D)
    red1a = _reduce_pair_end("rs1a", pos, pair1a, [d_wout])
    dyn = _mm("ssm_out_dx", "nt", du4b, w_out, T, DI, D, deps=[red1a["token"]])
    dysc, dzb, dnorm_w = _gnorm_bwd("gnorm_bwd", dyn, ysc, z, norm_w, rsn)
    dxs, dbm, dcm, ddt_g, dalog_g, dd_g = _ssd_bwd("ssd_bwd", dysc, xbc, dtc, dtr, alr, alc, d_e, hprev, DI)
    dconv, dconv_w, dconv_b = _conv_bwd_a("conv_bwd_a", dxs, dbm, dcm, xbc_pre, conv_w, conv_b)
    dxbcb = _conv_bwd_b("conv_bwd_b", dconv, conv_w)
    ddt = jnp.transpose(ddt_g[:, :, :HPG], (1, 0, 2)).reshape(T, H)
    _, ddtpb, ddt_bias = _dt_bwd("dt_bwd", ddt, dt_pre, ssm_dt_bias)
    da_log = dalog_g[:, 0, :HPG].reshape(1, H)
    dd_skip = dd_g[:, 0, :HPG].reshape(1, H)
    d_wz = _mm("ssm_in_dwz", "tn", x3b, dzb, D, DI, T, out_dtype=BF16)
    d_wx = _mm("ssm_in_dwx", "tn", x3b, dxbcb, D, CD, T, out_dtype=BF16)
    d_wdt = _mm("ssm_in_dwdt", "tn", x3b, ddtpb, D, H, T, out_dtype=BF16)

    def in_proj_shard(q):
        parts = []
        for piece, start in ((d_wz, 0), (d_wx, DI), (d_wdt, DI + CD)):
            lo, hi = max(q * NIN - start, 0), min((q + 1) * NIN - start, piece.shape[1])
            if lo < hi:
                parts.append(piece[:, lo:hi])
        return parts[0] if len(parts) == 1 else jnp.concatenate(parts, axis=1)

    d_win = jnp.stack([in_proj_shard(q) for q in range(N_CHIPS)])
    parts1b = [d_win, d_wout]
    pair1b = _reduce_pair_start("rs1b", parts1b)
    dx3 = _mm("ssm_in_dxz", "nt", dzb, w_z, T, D, DI, epi=lambda acc, r: acc + ALPHA * r, extras=[du4],
              deps=[pair1b["token"]])
    dx3 = _mm("ssm_in_dxx", "nt", dxbcb, w_xbc, T, D, CD, epi=lambda acc, r: acc + r, extras=[dx3])
    dx3 = _mm("ssm_in_dxdt", "nt", ddtpb, w_dt, T, D, H, epi=lambda acc, r: acc + r, extras=[dx3])

    red1b = _reduce_pair_end("rs1b", pos, pair1b, [dx3])

    dx2, d_gw0, d_pw0 = ple_bwd(0, dx3, gate0, e0, x2b, p0b, deps=[red1b["token"]])
    du2, du2b, dg01, db01 = _ln_bwd("ln01_bwd", dx2, xh2, rs2, lng[0, 1])
    dx1, d_w1_0, d_w2_0 = mlp_bwd(0, du2, du2b, a0, h2b0, x1b)
    parts0a = [d_w1_0, d_w2_0, d_gw0, d_pw0]
    pair0a = _reduce_pair_start("rs0a", parts0a)
    du1, dhrb, dg00, db00, dscale = _ln_bwd("ln00_bwd", dx1, xh1, rs1, lng[0, 0], hraw=hraw, scale=pool_scale)
    dln_g = jnp.stack([jnp.stack([dg00, dg01]), jnp.stack([dg10, dg11])]).reshape(-1)
    dln_b = jnp.stack([jnp.stack([db00, db01]), jnp.stack([db10, db11])]).reshape(-1)
    small_g = jnp.concatenate([dscale.reshape(-1), dconv_w.reshape(-1), dconv_b.reshape(-1), ddt_bias.reshape(-1),
                               da_log.reshape(-1), dd_skip.reshape(-1), dnorm_w.reshape(-1), dln_g, dln_b,
                               loss_local.reshape(1)])
    n_sg = small_g.shape[0]
    sg_all = _allgather_small("gather_small_grads", _pad_to(small_g, 1024).reshape(-1, 128),
                              after=[d_w1_0, d_w2_0])
    d_wpool = _mm("pool_dw", "tn", pooled, dhrb, D, GD, T, tm=GD, tn=GD, out_dtype=BF16, deps=[pair0a["token"]],
                  b_spec=lambda tm, tn, tk: pl.BlockSpec((tk, tn), lambda i, j, k: (k, i)),
                  o_shape=(NG, GD, GD), o_spec=lambda tm, tn, tk: pl.BlockSpec((None, tm, tn), lambda i, j, k: (i, 0, 0)))
    d_wpool = jnp.transpose(d_wpool.reshape(NG, N_CHIPS, GD // N_CHIPS, GD), (1, 0, 2, 3)).reshape(N_CHIPS, NG * GD // N_CHIPS, GD)
    red0a = _reduce_pair_end("rs0a", pos, pair0a, [d_wpool, sg_all])
    parts0b = [d_wpool]
    pair0b = _reduce_pair_start("rs0b", parts0b)
    dpooled = _mm("pool_dx", "nt", dhrb, w_pool, T, D, GD, tn=GD, deps=[red0a["token"], pair0b["token"]],
                  a_spec=lambda tm, tn, tk: pl.BlockSpec((tm, tk), lambda i, j, k: (i, j)),
                  b_spec=lambda tm, tn, tk: pl.BlockSpec((None, tn, tk), lambda i, j, k: (j, 0, 0)))
    grad_x = _pool_bwd("pool_bwd", dpooled, du1)
    red0b = _reduce_pair_end("rs0b", pos, pair0b, [grad_x])

    def adam(name, w, m, v, g, layer=0, prev=None):
        w3, m3, v3 = (t.reshape((t.shape[0], -1, t.shape[-1])) for t in (w, m, v))
        return _adamw(name, w3, m3, v3, g, layer, prev)

    sum1a = _reduce_sum_start("rs1a", pos, red1a, [red0b["token"]])
    sum1b = _reduce_sum_start("rs1b", pos, red1b, [sum1a["token"]])
    r_w1_1, r_w2_1, r_gw1, r_pw1 = _reduce_sum_end("rs1a", sum1a, [sum1b["token"]], [p.shape[1:] for p in parts1a])
    stacked = {"mlp_w1": (mlp_w1, m_mlp_w1, v_mlp_w1), "mlp_w2": (mlp_w2, m_mlp_w2, v_mlp_w2),
               "ple_w": (ple_w, m_ple_w, v_ple_w), "ple_gate_w": (ple_gate_w, m_ple_gate_w, v_ple_gate_w)}
    grads1 = {"mlp_w1": r_w1_1, "mlp_w2": r_w2_1, "ple_w": r_pw1, "ple_gate_w": r_gw1}
    upper = {nm: adam(f"adam_{nm}_l1", *stacked[nm], grads1[nm], layer=1) for nm in stacked}
    r_in, r_out = _reduce_sum_end("rs1b", sum1b, [t[1] for t in upper.values()], [p.shape[1:] for p in parts1b])
    big = {"ssm_in_w": [jnp.swapaxes(t, 1, 2) for t in adam(
               "adam_ssm_in_w", *(jnp.swapaxes(t, 1, 2) for t in (ssm_in_w, m_ssm_in_w, v_ssm_in_w)), r_in.T)],
           "ssm_out_w": adam("adam_ssm_out_w", ssm_out_w, m_ssm_out_w, v_ssm_out_w, r_out)}

    sg = _sum_peers("sum_small_grads", sg_all).reshape(-1)[:n_sg]
    o = 0

    def take(nel):
        nonlocal o
        v = sg[o:o + nel]
        o += nel
        return v

    g_scale = take(D).reshape(1, D)
    g_conv_w_full = take(CONV_K * CD).reshape(CONV_K, CD)
    g_conv_b_full = take(CD).reshape(1, CD)
    g_dt_bias = take(H).reshape(1, H)
    g_a_log = take(H).reshape(1, H)
    g_d = take(H).reshape(1, H)
    g_norm_full = take(DI).reshape(1, DI)
    g_lng_full = take(4 * D).reshape(2, 2, D)
    g_lnb_full = take(4 * D).reshape(2, 2, D)
    loss = take(1).reshape(())
    g_conv_w = lax.dynamic_slice_in_dim(g_conv_w_full, chip * cdq, cdq, axis=1)[None]
    g_conv_b = lax.dynamic_slice_in_dim(g_conv_b_full, chip * cdq, cdq, axis=1)
    g_norm = lax.dynamic_slice_in_dim(g_norm_full, chip * niq, niq, axis=1)
    g_lng = lax.dynamic_slice_in_dim(g_lng_full, chip * dq, dq, axis=2)
    g_lnb = lax.dynamic_slice_in_dim(g_lnb_full, chip * dq, dq, axis=2)

    small = [("pool_scale", pool_scale, m_pool_scale, v_pool_scale, g_scale),
             ("ssm_conv_w", ssm_conv_w, m_ssm_conv_w, v_ssm_conv_w, g_conv_w),
             ("ssm_conv_b", ssm_conv_b, m_ssm_conv_b, v_ssm_conv_b, g_conv_b),
             ("ssm_dt_bias", ssm_dt_bias, m_ssm_dt_bias, v_ssm_dt_bias, g_dt_bias),
             ("ssm_a_log", ssm_a_log, m_ssm_a_log, v_ssm_a_log, g_a_log),
             ("ssm_d", ssm_d, m_ssm_d, v_ssm_d, g_d),
             ("ssm_norm_w", ssm_norm_w, m_ssm_norm_w, v_ssm_norm_w, g_norm),
             ("ln_g", ln_g, m_ln_g, v_ln_g, g_lng),
             ("ln_b", ln_b, m_ln_b, v_ln_b, g_lnb)]

    def pack(idx):
        flat = _pad_to(jnp.concatenate([s[idx].reshape(-1) for s in small]), 1024)
        return flat.reshape(1, -1, 128)

    sm_out = _adamw("adam_small", pack(1), pack(2), pack(3), pack(4)[0])
    small_res = {}
    o = 0
    for s in small:
        nel = s[1].size
        small_res[s[0]] = [t.reshape(-1)[o:o + nel].reshape(s[1].shape) for t in sm_out]
        o += nel

    r_w1_0, r_w2_0, r_gw0, r_pw0 = _reduce_end(
        "rs0a", pos, red0a, [big["ssm_in_w"][1], big["ssm_out_w"][1], sm_out[1]] + [upper[nm][1] for nm in upper],
        [p.shape[1:] for p in parts0a])
    r_pool, = _reduce_end("rs0b", pos, red0b, [r_w1_0], [p.shape[1:] for p in parts0b])
    grads0 = {"mlp_w1": r_w1_0, "mlp_w2": r_w2_0, "ple_w": r_pw0, "ple_gate_w": r_gw0}
    big["pool_w"] = adam("adam_pool_w", pool_w, m_pool_w, v_pool_w, r_pool)
    for nm in stacked:
        big[nm] = adam(f"adam_{nm}_l0", *stacked[nm], grads0[nm], layer=0, prev=upper[nm])
    shapes = {"pool_w": pool_w.shape, "ssm_in_w": ssm_in_w.shape, "ssm_out_w": ssm_out_w.shape,
              **{nm: stacked[nm][0].shape for nm in stacked}}
    big = {nm: [t.reshape(shapes[nm]) for t in big[nm]] for nm in big}

    order = ["pool_w", "pool_scale", "ssm_in_w", "ssm_conv_w", "ssm_conv_b", "ssm_dt_bias", "ssm_a_log", "ssm_d",
             "ssm_norm_w", "ssm_out_w", "mlp_w1", "mlp_w2", "ln_g", "ln_b", "ple_w", "ple_gate_w"]
    res = {**big, **small_res}
    outs = [loss, grad_x[None]]
    for kind in range(4):
        outs += [res[nm][kind] for nm in order]
    return tuple(outs)
```

```python
import jax
import jax.numpy as jnp
from jax import lax
from jax.experimental import pallas as pl
from jax.experimental.pallas import tpu as pltpu

F32 = jnp.float32
BF16 = jnp.bfloat16
MESH = pl.DeviceIdType.MESH

DEPTH = 2
ALPHA = (2.0 * DEPTH) ** 0.25
LN_EPS = 1e-5
RMS_EPS = 1e-5
POOL_WINDOWS = (2, 4, 8, 16)
POOL_HALO = 16
HEAD_DIM = 64
HEAD_PAIR = 2 * HEAD_DIM
N_GROUPS = 8
D_STATE = 128
CHUNK = 128
CONV_K = 4
CONV_HALO = 8
HEAD_PAD = 128
ROW_PAD = 8
N_CHIPS = 4
N_DEV = 8
ADAM_LR = 0.001
ADAM_B1 = 0.9
ADAM_B2 = 0.999
ADAM_EPS = 1e-08
ADAM_WD = 0.01
ADAM_STEP = 10
VMEM_LIMIT = 56 * 1024 * 1024
ADAM_BLOCK_BYTES = 1024 * 1024
MM_VMEM_BUDGET = 40 * 1024 * 1024


_ANY = pl.BlockSpec(memory_space=pl.ANY)


def _cp(*sem):
    return pltpu.CompilerParams(dimension_semantics=sem, vmem_limit_bytes=VMEM_LIMIT)


def _pick(dim, pref):
    t = pref
    while t >= 128:
        if dim % t == 0:
            return t
        t //= 2
    return dim


def _rows(rows, row_bytes, budget):
    t = rows
    while t * row_bytes > budget and t % 16 == 0:
        t //= 2
    return t


def _sigmoid(v):
    return 1.0 / (1.0 + jnp.exp(-v))


_DIMS = {"nn": (((1,), (0,)), ((), ())), "nt": (((1,), (1,)), ((), ())), "tn": (((0,), (0,)), ((), ()))}


def _pick_k(k_unit, fixed_bytes, per_k_bytes):
    for n in range(1, k_unit // 128 + 1):
        if k_unit % n == 0 and (n == 1 or (k_unit // n) % 128 == 0):
            if fixed_bytes + (k_unit // n) * per_k_bytes <= MM_VMEM_BUDGET:
                return k_unit // n
    return min(k_unit, 128)


def _mm(name, form, a, b, M, N, K, *, tm=1024, tn=1024, k_unit=None, a_spec=None, b_spec=None,
        o_shape=None, o_spec=None, out_dtype=F32, pro=None, epi=None, extras=(), deps=()):
    tm, tn = _pick(M, tm), _pick(N, tn)
    out_dtypes = out_dtype if isinstance(out_dtype, tuple) else (out_dtype,)
    n_out = len(out_dtypes)
    in_place = n_out == 1 and out_dtypes[0] == F32
    fixed = 2 * tm * tn * (sum(jnp.dtype(d).itemsize for d in out_dtypes) + 4 * len(extras))
    fixed += 0 if in_place else 4 * tm * tn
    tk = _pick_k(K if k_unit is None else k_unit, fixed,
                 2 * (tm * a.dtype.itemsize + tn * b.dtype.itemsize))
    nk = K // tk
    if a_spec is None:
        a_spec = (pl.BlockSpec((tk, tm), lambda i, j, k: (k, i)) if form == "tn"
                  else pl.BlockSpec((tm, tk), lambda i, j, k: (i, k)))
    else:
        a_spec = a_spec(tm, tn, tk)
    if b_spec is None:
        b_spec = (pl.BlockSpec((tn, tk), lambda i, j, k: (j, k)) if form == "nt"
                  else pl.BlockSpec((tk, tn), lambda i, j, k: (k, j)))
    else:
        b_spec = b_spec(tm, tn, tk)
    if o_spec is None:
        o_spec = pl.BlockSpec((tm, tn), lambda i, j, k: (i, j))
        o_shape = (M, N)
    else:
        o_spec = o_spec(tm, tn, tk)
    ex_arrays = [e for e in extras]
    ex_specs = [pl.BlockSpec((tm, tn), lambda i, j, k: (i, j)) for _ in extras]
    ne = len(ex_arrays)
    nd = len(deps)
    dep_specs = [pl.BlockSpec((8, 128), lambda i, j, k: (0, 0)) for _ in deps]
    dims = _DIMS[form]
    use_scratch = nk > 1 and not in_place

    def body(a_ref, b_ref, *rest):
        ex_refs = rest[:ne]
        o_refs = rest[ne + nd:ne + nd + n_out]
        at = a_ref[...]
        if pro is not None:
            at = pro(at)
        p = lax.dot_general(at.astype(BF16), b_ref[...].astype(BF16), dims, preferred_element_type=F32)

        def finish(acc):
            res = acc if epi is None else epi(acc, *[r[...] for r in ex_refs])
            res = res if isinstance(res, tuple) else (res,)
            for o_ref, r, d in zip(o_refs, res, out_dtypes):
                o_ref[...] = r.astype(d)

        if nk == 1:
            finish(p)
        else:
            acc_ref = rest[ne + nd + n_out] if use_scratch else o_refs[0]
            k = pl.program_id(2)

            @pl.when(k == 0)
            def _():
                acc_ref[...] = p

            @pl.when(jnp.logical_and(k > 0, k < nk - 1))
            def _():
                acc_ref[...] += p

            @pl.when(k == nk - 1)
            def _():
                finish(acc_ref[...] + p)

    res = pl.pallas_call(
        body, name=name, grid=(M // tm, N // tn, nk),
        in_specs=[a_spec, b_spec] + ex_specs + dep_specs, out_specs=[o_spec] * n_out,
        out_shape=[jax.ShapeDtypeStruct(o_shape, d) for d in out_dtypes],
        scratch_shapes=[pltpu.VMEM((tm, tn), F32)] if use_scratch else [],
        compiler_params=_cp("parallel", "parallel", "arbitrary"),
    )(a, b, *ex_arrays, *deps)
    return res if n_out > 1 else res[0]


def _colshard_b(l, n_per):
    def make(tm, tn, tk):
        nb = n_per // tn
        return pl.BlockSpec((None, None, tk, tn), lambda i, j, k: (j // nb, l, k, j % nb))
    return make


def _colshard_bt(l, n_per):
    def make(tm, tn, tk):
        nb = n_per // tk
        return pl.BlockSpec((None, None, tn, tk), lambda i, j, k: (k // nb, l, j, k % nb))
    return make


def _colshard_o(n_per):
    def make(tm, tn, tk):
        nb = n_per // tn
        return pl.BlockSpec((None, tm, tn), lambda i, j, k: (j // nb, i, j % nb))
    return make


def _res_ln(name, xprev, h, g, b, scale=None):
    T, D = xprev.shape
    tr = _pick(T, 256)
    row = pl.BlockSpec((tr, D), lambda i: (i, 0))
    vec = pl.BlockSpec((1, D), lambda i: (0, 0))
    has_scale = scale is not None

    def body(*refs):
        if has_scale:
            x_ref, h_ref, s_ref, g_ref, b_ref, y_ref, yb_ref, xh_ref, rs_ref = refs
            hh = h_ref[...] * s_ref[...]
        else:
            x_ref, h_ref, g_ref, b_ref, y_ref, yb_ref, xh_ref, rs_ref = refs
            hh = h_ref[...]
        u = ALPHA * x_ref[...] + hh
        mu = jnp.mean(u, axis=-1, keepdims=True)
        d = u - mu
        var = jnp.mean(d * d, axis=-1, keepdims=True)
        rs = lax.rsqrt(var + LN_EPS)
        xh = d * rs
        y = xh * g_ref[...] + b_ref[...]
        y_ref[...] = y
        yb_ref[...] = y.astype(BF16)
        xh_ref[...] = xh
        rs_ref[...] = rs

    ins = [xprev, h] + ([scale] if has_scale else []) + [g, b]
    specs = [row, row] + ([vec] if has_scale else []) + [vec, vec]
    return pl.pallas_call(
        body, name=name, grid=(T // tr,), in_specs=specs,
        out_specs=[row, row, row, pl.BlockSpec((tr, 1), lambda i: (i, 0))],
        out_shape=[jax.ShapeDtypeStruct((T, D), F32), jax.ShapeDtypeStruct((T, D), BF16),
                   jax.ShapeDtypeStruct((T, D), F32), jax.ShapeDtypeStruct((T, 1), F32)],
        compiler_params=_cp("parallel"),
    )(*ins)


def _accum(ref, part, first):
    @pl.when(first)
    def _():
        ref[...] = part

    @pl.when(jnp.logical_not(first))
    def _():
        ref[...] += part


def _ln_bwd(name, dy, xh, rs, g, hraw=None, scale=None):
    T, D = dy.shape
    tr = _pick(T, 256)
    row = pl.BlockSpec((tr, D), lambda i: (i, 0))
    vec = pl.BlockSpec((1, D), lambda i: (0, 0))
    has_scale = scale is not None

    def body(*refs):
        if has_scale:
            dy_ref, xh_ref, rs_ref, g_ref, hr_ref, s_ref, du_ref, dub_ref, dg_ref, db_ref, ds_ref = refs
        else:
            dy_ref, xh_ref, rs_ref, g_ref, du_ref, dub_ref, dg_ref, db_ref = refs
        first = pl.program_id(0) == 0
        dyv = dy_ref[...]
        xhv = xh_ref[...]
        dxh = dyv * g_ref[...]
        m1 = jnp.mean(dxh, axis=-1, keepdims=True)
        m2 = jnp.mean(dxh * xhv, axis=-1, keepdims=True)
        du = rs_ref[...] * (dxh - m1 - xhv * m2)
        du_ref[...] = du
        if has_scale:
            dub_ref[...] = (du * s_ref[...]).astype(BF16)
            _accum(ds_ref, jnp.sum(du * hr_ref[...], axis=0, keepdims=True), first)
        else:
            dub_ref[...] = du.astype(BF16)
        _accum(dg_ref, jnp.sum(dyv * xhv, axis=0, keepdims=True), first)
        _accum(db_ref, jnp.sum(dyv, axis=0, keepdims=True), first)

    ins = [dy, xh, rs, g] + ([hraw, scale] if has_scale else [])
    specs = [row, row, pl.BlockSpec((tr, 1), lambda i: (i, 0)), vec] + ([row, vec] if has_scale else [])
    n_vec = 3 if has_scale else 2
    return pl.pallas_call(
        body, name=name, grid=(T // tr,), in_specs=specs,
        out_specs=[row, row] + [vec] * n_vec,
        out_shape=[jax.ShapeDtypeStruct((T, D), F32), jax.ShapeDtypeStruct((T, D), BF16)]
        + [jax.ShapeDtypeStruct((1, D), F32)] * n_vec,
        compiler_params=_cp("arbitrary"),
    )(*ins)


def _gate_fwd(name, x, gl, e):
    T, D = x.shape
    tr = _pick(T, 256)
    row = pl.BlockSpec((tr, D), lambda i: (i, 0))

    def body(x_ref, gl_ref, e_ref, xo_ref, xob_ref, gate_ref):
        gate = _sigmoid(gl_ref[...])
        xo = x_ref[...] + gate * e_ref[...]
        xo_ref[...] = xo
        xob_ref[...] = xo.astype(BF16)
        gate_ref[...] = gate

    return pl.pallas_call(
        body, name=name, grid=(T // tr,), in_specs=[row, row, row], out_specs=[row, row, row],
        out_shape=[jax.ShapeDtypeStruct((T, D), F32), jax.ShapeDtypeStruct((T, D), BF16),
                   jax.ShapeDtypeStruct((T, D), F32)],
        compiler_params=_cp("parallel"),
    )(x, gl, e)


def _gate_loss(name, x, gl, e, tgt):
    T, D = x.shape
    tr = _pick(T, 256)
    row = pl.BlockSpec((tr, D), lambda i: (i, 0))

    def body(x_ref, gl_ref, e_ref, t_ref, dy_ref, gate_ref, lp_ref):
        gate = _sigmoid(gl_ref[...])
        err = x_ref[...] + gate * e_ref[...] - t_ref[...]
        dy_ref[...] = err * (1.0 / D)
        gate_ref[...] = gate
        s = jnp.sum(jnp.mean(err * err, axis=-1, keepdims=True), axis=0, keepdims=True)
        lp_ref[...] = jnp.broadcast_to(0.5 * s, (8, 128))

    return pl.pallas_call(
        body, name=name, grid=(T // tr,), in_specs=[row] * 4,
        out_specs=[row, row, pl.BlockSpec((8, 128), lambda i: (i, 0))],
        out_shape=[jax.ShapeDtypeStruct((T, D), F32), jax.ShapeDtypeStruct((T, D), F32),
                   jax.ShapeDtypeStruct((T // tr * 8, 128), F32)],
        compiler_params=_cp("parallel"),
    )(x, gl, e, tgt)


def _gate_bwd(name, dxo, gate, e):
    T, D = dxo.shape
    tr = _pick(T, 256)
    row = pl.BlockSpec((tr, D), lambda i: (i, 0))

    def body(d_ref, gate_ref, e_ref, dgl_ref, de_ref):
        d = d_ref[...]
        gate = gate_ref[...]
        dgl_ref[...] = (d * e_ref[...] * gate * (1.0 - gate)).astype(BF16)
        de_ref[...] = (d * gate).astype(BF16)

    return pl.pallas_call(
        body, name=name, grid=(T // tr,), in_specs=[row] * 3, out_specs=[row, row],
        out_shape=[jax.ShapeDtypeStruct((T, D), BF16)] * 2,
        compiler_params=_cp("parallel"),
    )(dxo, gate, e)


def _gnorm_fwd(name, y, z, w):
    T, DI = y.shape
    tr = _pick(T, 128)
    row = pl.BlockSpec((tr, DI), lambda i: (i, 0))
    vec = pl.BlockSpec((1, DI), lambda i: (0, 0))
    col = pl.BlockSpec((tr, 1), lambda i: (i, 0))

    def body(y_ref, z_ref, w_ref, o_ref, rs_ref):
        zv = z_ref[...]
        yz = y_ref[...] * (zv * _sigmoid(zv))
        rs = lax.rsqrt(jnp.mean(yz * yz, axis=-1, keepdims=True) + RMS_EPS)
        o_ref[...] = (yz * rs * w_ref[...]).astype(BF16)
        rs_ref[...] = rs

    return pl.pallas_call(
        body, name=name, grid=(T // tr,), in_specs=[row, row, vec], out_specs=[row, col],
        out_shape=[jax.ShapeDtypeStruct((T, DI), BF16), jax.ShapeDtypeStruct((T, 1), F32)],
        compiler_params=_cp("parallel"),
    )(y, z, w)


def _gnorm_bwd(name, dyn, y, z, w, rs):
    T, DI = y.shape
    tr = _pick(T, 128)
    row = pl.BlockSpec((tr, DI), lambda i: (i, 0))
    vec = pl.BlockSpec((1, DI), lambda i: (0, 0))
    col = pl.BlockSpec((tr, 1), lambda i: (i, 0))

    def body(d_ref, y_ref, z_ref, w_ref, rs_ref, dy_ref, dz_ref, dw_ref):
        first = pl.program_id(0) == 0
        zv = z_ref[...]
        yv = y_ref[...]
        sg = _sigmoid(zv)
        sz = zv * sg
        rsv = rs_ref[...]
        yzh = yv * sz * rsv
        dv = d_ref[...]
        gw = dv * w_ref[...]
        m = jnp.mean(gw * yzh, axis=-1, keepdims=True)
        dyz = rsv * (gw - yzh * m)
        dy_ref[...] = dyz * sz
        dz_ref[...] = (dyz * yv * (sg * (1.0 + zv * (1.0 - sg)))).astype(BF16)
        _accum(dw_ref, jnp.sum(dv * yzh, axis=0, keepdims=True), first)

    return pl.pallas_call(
        body, name=name, grid=(T // tr,), in_specs=[row, row, row, vec, col], out_specs=[row, row, vec],
        out_shape=[jax.ShapeDtypeStruct((T, DI), F32), jax.ShapeDtypeStruct((T, DI), BF16),
                   jax.ShapeDtypeStruct((1, DI), F32)],
        compiler_params=_cp("arbitrary"),
    )(dyn, y, z, w, rs)


def _sel4(j, vals):
    return jnp.where(j == 0, vals[0], jnp.where(j == 1, vals[1], jnp.where(j == 2, vals[2], vals[3])))


def _pool_cnt(i, j, tr, rows, offset):
    t = i * tr + offset + lax.broadcasted_iota(jnp.int32, (rows, 1), 0)
    win = _sel4(j, POOL_WINDOWS)
    return jnp.minimum(t + 1, win).astype(F32)


def _pool_fwd(name, x):
    T, D = x.shape
    gd = D // len(POOL_WINDOWS)
    tr = _pick(T, 512)
    hb = tr // POOL_HALO

    def body(x_ref, h_ref, o_ref):
        i, j = pl.program_id(0), pl.program_id(1)
        xv = x_ref[...]
        halo = jnp.where(i > 0, h_ref[...], 0.0)
        cat = jnp.concatenate([halo, xv], axis=0)
        s2 = cat + pltpu.roll(cat, 1, 0)
        s4 = s2 + pltpu.roll(s2, 2, 0)
        s8 = s4 + pltpu.roll(s4, 4, 0)
        s16 = s8 + pltpu.roll(s8, 8, 0)
        sel = _sel4(j, (s2, s4, s8, s16))[POOL_HALO:]
        o_ref[...] = (sel / _pool_cnt(i, j, tr, tr, 0) - xv).astype(BF16)

    return pl.pallas_call(
        body, name=name, grid=(T // tr, len(POOL_WINDOWS)),
        in_specs=[pl.BlockSpec((tr, gd), lambda i, j: (i, j)),
                  pl.BlockSpec((POOL_HALO, gd), lambda i, j: (jnp.maximum(i * hb - 1, 0), j))],
        out_specs=pl.BlockSpec((tr, gd), lambda i, j: (i, j)),
        out_shape=jax.ShapeDtypeStruct((T, D), BF16),
        compiler_params=_cp("parallel", "parallel"),
    )(x, x)


def _pool_bwd(name, dp, du):
    T, D = dp.shape
    gd = D // len(POOL_WINDOWS)
    tr = _pick(T, 512)
    hb = tr // POOL_HALO
    last_h = T // POOL_HALO - 1
    n = tr + POOL_HALO

    def body(dp_ref, h_ref, du_ref, o_ref):
        i, j = pl.program_id(0), pl.program_id(1)
        dpv = dp_ref[...]
        q = dpv / _pool_cnt(i, j, tr, tr, 0)
        qh = jnp.where(i < pl.num_programs(0) - 1, h_ref[...] / _pool_cnt(i, j, tr, POOL_HALO, tr), 0.0)
        cat = jnp.concatenate([q, qh], axis=0)
        f2 = cat + pltpu.roll(cat, n - 1, 0)
        f4 = f2 + pltpu.roll(f2, n - 2, 0)
        f8 = f4 + pltpu.roll(f4, n - 4, 0)
        f16 = f8 + pltpu.roll(f8, n - 8, 0)
        sel = _sel4(j, (f2, f4, f8, f16))[:tr]
        o_ref[...] = ALPHA * du_ref[...] + sel - dpv

    return pl.pallas_call(
        body, name=name, grid=(T // tr, len(POOL_WINDOWS)),
        in_specs=[pl.BlockSpec((tr, gd), lambda i, j: (i, j)),
                  pl.BlockSpec((POOL_HALO, gd), lambda i, j: (jnp.minimum((i + 1) * hb, last_h), j)),
                  pl.BlockSpec((tr, gd), lambda i, j: (i, j))],
        out_specs=pl.BlockSpec((tr, gd), lambda i, j: (i, j)),
        out_shape=jax.ShapeDtypeStruct((T, D), F32),
        compiler_params=_cp("parallel", "parallel"),
    )(dp, dp, du)


def _conv_taps(cat, wv, rows):
    shifted = [cat[CONV_HALO:] if s == 0 else pltpu.roll(cat, s, 0)[CONV_HALO:] for s in range(CONV_K)]
    acc = shifted[0] * wv[CONV_K - 1:CONV_K]
    for s in range(1, CONV_K):
        acc = acc + shifted[s] * wv[CONV_K - 1 - s:CONV_K - s]
    return acc, shifted


def _conv_fwd(name, xp, w, b):
    T, CD = xp.shape
    tr, tc = _pick(T, 512), _pick(CD, 512)
    hb = tr // CONV_HALO

    def body(x_ref, h_ref, w_ref, b_ref, o_ref):
        i = pl.program_id(0)
        halo = jnp.where(i > 0, h_ref[...], 0.0)
        cat = jnp.concatenate([halo, x_ref[...]], axis=0)
        acc, _ = _conv_taps(cat, w_ref[...], tr)
        acc = acc + b_ref[...]
        o_ref[...] = acc * _sigmoid(acc)

    return pl.pallas_call(
        body, name=name, grid=(T // tr, CD // tc),
        in_specs=[pl.BlockSpec((tr, tc), lambda i, j: (i, j)),
                  pl.BlockSpec((CONV_HALO, tc), lambda i, j: (jnp.maximum(i * hb - 1, 0), j)),
                  pl.BlockSpec((CONV_K, tc), lambda i, j: (0, j)),
                  pl.BlockSpec((1, tc), lambda i, j: (0, j))],
        out_specs=pl.BlockSpec((tr, tc), lambda i, j: (i, j)),
        out_shape=jax.ShapeDtypeStruct((T, CD), F32),
        compiler_params=_cp("parallel", "parallel"),
    )(xp, xp, w, b)


def _conv_bwd_a(name, dxs, dbm, dcm, xp, w, b):
    T, CD = xp.shape
    tr = _pick(T, 512)
    tc = _pick(dbm.shape[1], 512)
    hb = tr // CONV_HALO
    nx, nb = dxs.shape[1] // tc, dbm.shape[1] // tc

    def part_spec(lo, n):
        def imap(j, i):
            inside = jnp.logical_and(j >= lo, j < lo + n)
            return (jnp.where(inside, i, 0), jnp.clip(j - lo, 0, n - 1))
        return pl.BlockSpec((tr, tc), imap)

    def body(dx_ref, db_ref, dc_ref, x_ref, h_ref, w_ref, b_ref, o_ref, dw_ref, dbias_ref):
        j, i = pl.program_id(0), pl.program_id(1)
        first = i == 0
        d = jnp.where(j < nx, dx_ref[...], jnp.where(j < nx + nb, db_ref[...], dc_ref[...]))
        halo = jnp.where(i > 0, h_ref[...], 0.0)
        cat = jnp.concatenate([halo, x_ref[...]], axis=0)
        acc, shifted = _conv_taps(cat, w_ref[...], tr)
        acc = acc + b_ref[...]
        sg = _sigmoid(acc)
        dconv = d * (sg * (1.0 + acc * (1.0 - sg)))
        o_ref[...] = dconv
        _accum(dbias_ref, jnp.sum(dconv, axis=0, keepdims=True), first)
        tap = lax.broadcasted_iota(jnp.int32, (CONV_K, tc), 0)
        dwv = jnp.zeros((CONV_K, tc), F32)
        for s in range(CONV_K):
            dwv = jnp.where(tap == CONV_K - 1 - s, jnp.sum(dconv * shifted[s], axis=0, keepdims=True), dwv)
        _accum(dw_ref, dwv, first)

    return pl.pallas_call(
        body, name=name, grid=(CD // tc, T // tr),
        in_specs=[part_spec(0, nx), part_spec(nx, nb), part_spec(nx + nb, nb),
                  pl.BlockSpec((tr, tc), lambda j, i: (i, j)),
                  pl.BlockSpec((CONV_HALO, tc), lambda j, i: (jnp.maximum(i * hb - 1, 0), j)),
                  pl.BlockSpec((CONV_K, tc), lambda j, i: (0, j)),
                  pl.BlockSpec((1, tc), lambda j, i: (0, j))],
        out_specs=[pl.BlockSpec((tr, tc), lambda j, i: (i, j)),
                   pl.BlockSpec((CONV_K, tc), lambda j, i: (0, j)),
                   pl.BlockSpec((1, tc), lambda j, i: (0, j))],
        out_shape=[jax.ShapeDtypeStruct((T, CD), F32), jax.ShapeDtypeStruct((CONV_K, CD), F32),
                   jax.ShapeDtypeStruct((1, CD), F32)],
        compiler_params=_cp("parallel", "arbitrary"),
    )(dxs, dbm, dcm, xp, xp, w, b)


def _conv_bwd_b(name, dconv, w):
    T, CD = dconv.shape
    tr, tc = _pick(T, 512), _pick(CD, 512)
    hb = tr // CONV_HALO
    last_h = T // CONV_HALO - 1
    n = tr + CONV_HALO

    def body(d_ref, h_ref, w_ref, o_ref):
        i = pl.program_id(0)
        halo = jnp.where(i < pl.num_programs(0) - 1, h_ref[...], 0.0)
        cat = jnp.concatenate([d_ref[...], halo], axis=0)
        wv = w_ref[...]
        acc = cat[:tr] * wv[CONV_K - 1:CONV_K]
        for s in range(1, CONV_K):
            acc = acc + pltpu.roll(cat, n - s, 0)[:tr] * wv[CONV_K - 1 - s:CONV_K - s]
        o_ref[...] = acc.astype(BF16)

    return pl.pallas_call(
        body, name=name, grid=(T // tr, CD // tc),
        in_specs=[pl.BlockSpec((tr, tc), lambda i, j: (i, j)),
                  pl.BlockSpec((CONV_HALO, tc), lambda i, j: (jnp.minimum((i + 1) * hb, last_h), j)),
                  pl.BlockSpec((CONV_K, tc), lambda i, j: (0, j))],
        out_specs=pl.BlockSpec((tr, tc), lambda i, j: (i, j)),
        out_shape=jax.ShapeDtypeStruct((T, CD), BF16),
        compiler_params=_cp("parallel", "parallel"),
    )(dconv, dconv, w)


def _dt_fwd(name, dtp, bias):
    T, H = dtp.shape

    def body(x_ref, b_ref, o_ref):
        v = x_ref[...] + b_ref[...]
        u = jnp.exp(-jnp.abs(v))
        w1 = 1.0 + u
        lp = jnp.where(w1 == 1.0, u, jnp.log(w1) * (u / jnp.where(w1 == 1.0, 1.0, w1 - 1.0)))
        o_ref[...] = jnp.maximum(v, 0.0) + lp

    return pl.pallas_call(body, name=name, out_shape=jax.ShapeDtypeStruct((T, H), F32))(dtp, bias)


def _dt_bwd(name, ddt, dtp, bias):
    T, H = dtp.shape

    def body(d_ref, x_ref, b_ref, o_ref, ob_ref, db_ref):
        g = d_ref[...] * _sigmoid(x_ref[...] + b_ref[...])
        o_ref[...] = g
        ob_ref[...] = g.astype(BF16)
        db_ref[...] = jnp.sum(g, axis=0, keepdims=True)

    return pl.pallas_call(
        body, name=name,
        out_shape=[jax.ShapeDtypeStruct((T, H), F32), jax.ShapeDtypeStruct((T, H), BF16),
                   jax.ShapeDtypeStruct((1, H), F32)],
    )(ddt, dtp, bias)


def _split(v):
    hi = v.astype(BF16)
    return hi, (v - hi.astype(F32)).astype(BF16)


def _dot01(form, a, b, mask):
    if mask == "b":
        hi, lo = _split(a)
        mb = b.astype(BF16)
        return (lax.dot_general(hi, mb, _DIMS[form], preferred_element_type=F32)
                + lax.dot_general(lo, mb, _DIMS[form], preferred_element_type=F32))
    hi, lo = _split(b)
    ma = a.astype(BF16)
    return (lax.dot_general(ma, hi, _DIMS[form], preferred_element_type=F32)
            + lax.dot_general(ma, lo, _DIMS[form], preferred_element_type=F32))


def _dotb(form, a, b):
    return lax.dot_general(a.astype(BF16), b.astype(BF16), _DIMS[form], preferred_element_type=F32)


def _ssd_common(dtc, dtr, alr, alc, gw):
    li = lax.broadcasted_iota(jnp.int32, (CHUNK, CHUNK), 0)
    si = lax.broadcasted_iota(jnp.int32, (CHUNK, CHUNK), 1)
    tri = (li >= si).astype(F32)
    trit = (li <= si).astype(F32)
    a_row = -jnp.exp(alr)
    a_col = -jnp.exp(alc)
    acs_c = _dot01("nn", tri, dtc * a_row, "a")
    acs_r = _dot01("nn", dtr * a_col, trit, "b")
    eh = lax.broadcasted_iota(jnp.int32, (HEAD_PAD, gw), 0)
    ec = lax.broadcasted_iota(jnp.int32, (HEAD_PAD, gw), 1) // HEAD_DIM
    expand = (eh == ec).astype(F32)
    th = lax.broadcasted_iota(jnp.int32, (gw, HEAD_PAD), 1)
    tc = lax.broadcasted_iota(jnp.int32, (gw, HEAD_PAD), 0) // HEAD_DIM
    reduce_ = (th == tc).astype(F32)
    acs_last = acs_c[CHUNK - 1:CHUNK, :]
    both = _dot01("nn", jnp.concatenate([dtc, acs_c], axis=0), expand, "b")
    acs_e = both[CHUNK:]
    return dict(li=li, si=si, tri=tri, trit=trit, a_row=a_row, acs_c=acs_c, acs_r=acs_r,
                reduce=reduce_, acs_last=acs_last, dt_e=both[:CHUNK],
                eacs_e=jnp.exp(acs_e), dec_e=jnp.exp(acs_e[CHUNK - 1:CHUNK, :] - acs_e),
                dec_h=jnp.exp(acs_last - acs_c))


def _ssd_specs(T, DI, gw, nc, rev):
    nsb = DI // D_STATE
    cidx = (lambda c: nc - 1 - c) if rev else (lambda c: c)
    return dict(
        xs=pl.BlockSpec((CHUNK, gw), lambda g, c: (cidx(c), g)),
        bm=pl.BlockSpec((CHUNK, D_STATE), lambda g, c: (cidx(c), nsb + g)),
        cm=pl.BlockSpec((CHUNK, D_STATE), lambda g, c: (cidx(c), nsb + N_GROUPS + g)),
        dtc=pl.BlockSpec((None, CHUNK, HEAD_PAD), lambda g, c: (g, cidx(c), 0)),
        dtr=pl.BlockSpec((None, ROW_PAD, CHUNK), lambda g, c: (g, 0, cidx(c))),
        alr=pl.BlockSpec((None, 1, HEAD_PAD), lambda g, c: (g, 0, 0)),
        alc=pl.BlockSpec((None, ROW_PAD, 1), lambda g, c: (g, 0, 0)),
        de=pl.BlockSpec((None, 1, gw), lambda g, c: (g, 0, 0)),
        hp=pl.BlockSpec((None, None, gw, D_STATE), lambda g, c: (cidx(c), g, 0, 0)),
        bc=pl.BlockSpec((CHUNK, D_STATE), lambda g, c: (cidx(c), g)),
        acc=pl.BlockSpec((None, 1, HEAD_PAD), lambda g, c: (g, 0, 0)),
    )


def _ssd_fwd(name, xbc, dtc, dtr, alr, alc, d_e, DI, deps=()):
    T = xbc.shape[0]
    nc = T // CHUNK
    gw = DI // N_GROUPS
    sp = _ssd_specs(T, DI, gw, nc, False)

    def body(xs_ref, b_ref, c_ref, dtc_ref, dtr_ref, alr_ref, alc_ref, de_ref, *rest):
        y_ref, hp_ref, h_scr = rest[len(deps):]

        @pl.when(pl.program_id(1) == 0)
        def _():
            h_scr[...] = jnp.zeros_like(h_scr)

        hpv = h_scr[...]
        hp_ref[...] = hpv
        xs = xs_ref[...]
        bb = b_ref[...].astype(BF16)
        cb_ = c_ref[...].astype(BF16)
        cm = _ssd_common(dtc_ref[...], dtr_ref[...], alr_ref[...], alc_ref[...], gw)
        x = xs * cm["dt_e"]
        xb = x.astype(BF16)
        cbm = _dotb("nt", cb_, bb)
        causal = cm["li"] >= cm["si"]
        second = lax.broadcasted_iota(jnp.int32, (1, HEAD_PAIR), 1) >= HEAD_DIM
        pieces = []
        for pr in range(gw // HEAD_PAIR):
            xp = xb[:, pr * HEAD_PAIR:(pr + 1) * HEAD_PAIR]
            for h2 in range(2):
                j = 2 * pr + h2
                seg = cm["acs_c"][:, j:j + 1] - cm["acs_r"][j:j + 1, :]
                lmat = jnp.exp(jnp.where(causal, jnp.minimum(seg, 0.0), -1e30))
                yj = _dotb("nn", cbm * lmat, xp)
                yp = yj if h2 == 0 else jnp.where(second, yj, yp)
            pieces.append(yp)
        ydiag = pieces[0] if len(pieces) == 1 else jnp.concatenate(pieces, axis=1)
        states = _dotb("tn", x * cm["dec_e"], bb)
        yoff = _dotb("nt", cb_, hpv) * cm["eacs_e"]
        y_ref[...] = ydiag + yoff + xs * de_ref[...]
        cd_rows = jnp.sum(cm["reduce"] * jnp.exp(cm["acs_last"]), axis=1, keepdims=True)
        h_scr[...] = cd_rows * hpv + states

    return pl.pallas_call(
        body, name=name, grid=(N_GROUPS, nc),
        in_specs=[sp["xs"], sp["bm"], sp["cm"], sp["dtc"], sp["dtr"], sp["alr"], sp["alc"], sp["de"]]
        + [pl.BlockSpec((8, 128), lambda g, c: (0, 0)) for _ in deps],
        out_specs=[sp["xs"], sp["hp"]],
        out_shape=[jax.ShapeDtypeStruct((T, DI), F32), jax.ShapeDtypeStruct((nc, N_GROUPS, gw, D_STATE), F32)],
        scratch_shapes=[pltpu.VMEM((gw, D_STATE), F32)],
        compiler_params=_cp("parallel", "arbitrary"),
    )(xbc, xbc, xbc, dtc, dtr, alr, alc, d_e, *deps)


def _ssd_bwd(name, dy, xbc, dtc, dtr, alr, alc, d_e, hprev, DI):
    T = xbc.shape[0]
    nc = T // CHUNK
    gw = DI // N_GROUPS
    sp = _ssd_specs(T, DI, gw, nc, True)

    def body(dy_ref, xs_ref, b_ref, c_ref, dtc_ref, dtr_ref, alr_ref, alc_ref, de_ref, hp_ref,
             dxs_ref, db_ref, dc_ref, ddt_ref, dal_ref, dd_ref, dh_scr):
        first = pl.program_id(1) == 0

        @pl.when(first)
        def _():
            dh_scr[...] = jnp.zeros_like(dh_scr)

        xs = xs_ref[...]
        dyv = dy_ref[...]
        bb = b_ref[...].astype(BF16)
        cb_ = c_ref[...].astype(BF16)
        dtc_v = dtc_ref[...]
        cm = _ssd_common(dtc_v, dtr_ref[...], alr_ref[...], alc_ref[...], gw)
        hpv = hp_ref[...]
        hpb = hpv.astype(BF16)
        dhn = dh_scr[...]
        dhnb = dhn.astype(BF16)
        x = xs * cm["dt_e"]
        xb = x.astype(BF16)
        cbm = _dotb("nt", cb_, bb)
        causal = cm["li"] >= cm["si"]
        second = lax.broadcasted_iota(jnp.int32, (1, HEAD_PAIR), 1) >= HEAD_DIM
        lane_pad = lax.broadcasted_iota(jnp.int32, (1, HEAD_PAD), 1)
        sub_pad = lax.broadcasted_iota(jnp.int32, (ROW_PAD, 1), 0)

        dxs = dyv * de_ref[...]

        dcb = jnp.zeros((CHUNK, CHUNK), F32)
        dacs_c = jnp.zeros((CHUNK, HEAD_PAD), F32)
        dacs_r = jnp.zeros((ROW_PAD, CHUNK), F32)
        pieces = []
        for pr in range(gw // HEAD_PAIR):
            xp = xb[:, pr * HEAD_PAIR:(pr + 1) * HEAD_PAIR]
            dyp = dyv[:, pr * HEAD_PAIR:(pr + 1) * HEAD_PAIR]
            for h2 in range(2):
                j = 2 * pr + h2
                seg = cm["acs_c"][:, j:j + 1] - cm["acs_r"][j:j + 1, :]
                lmat = jnp.exp(jnp.where(causal, jnp.minimum(seg, 0.0), -1e30))
                mmat = cbm * lmat
                dym = jnp.where(second if h2 == 1 else jnp.logical_not(second), dyp, 0.0).astype(BF16)
                dm = _dotb("nt", dym, xp)
                dxj = _dotb("tn", mmat, dym)
                dxp = dxj if h2 == 0 else dxp + dxj
                dcb = dcb + dm * lmat
                dseg = dm * mmat
                dacs_c = dacs_c + jnp.where(lane_pad == j, jnp.sum(dseg, axis=1, keepdims=True), 0.0)
                dacs_r = dacs_r - jnp.where(sub_pad == j, jnp.sum(dseg, axis=0, keepdims=True), 0.0)
            pieces.append(dxp)
        dx = pieces[0] if len(pieces) == 1 else jnp.concatenate(pieces, axis=1)
        dc = _dotb("nn", dcb, bb)
        db = _dotb("tn", dcb, cb_)

        gm = _dotb("nt", cb_, hpb)
        dgm = dyv * cm["eacs_e"]
        dc = dc + _dotb("nn", dgm, hpb)
        dhp = _dotb("tn", dgm, cb_)

        cd_row = jnp.exp(cm["acs_last"])
        cd_rows = jnp.sum(cm["reduce"] * cd_row, axis=1, keepdims=True)
        dhp = dhp + cd_rows * dhn
        rsum = jnp.sum(dhn * hpv, axis=1, keepdims=True)
        dacs_last = jnp.sum(cm["reduce"] * rsum, axis=0, keepdims=True) * cd_row
        dec_e = cm["dec_e"]
        xdec = x * dec_e
        dxdec = _dotb("nt", bb, dhnb)
        db = db + _dotb("nn", xdec, dhnb)
        dx = dx + dxdec * dec_e
        sums = _dot01("nn", jnp.concatenate([dyv * xs, dgm * gm, dxdec * x, dx * xs], axis=0), cm["reduce"], "b")
        _accum(dd_ref, jnp.sum(sums[:CHUNK], axis=0, keepdims=True), first)
        tdec = sums[2 * CHUNK:3 * CHUNK] * cm["dec_h"]
        dacs_c = dacs_c + sums[CHUNK:2 * CHUNK] - tdec
        dacs_last = dacs_last + jnp.sum(tdec, axis=0, keepdims=True)
        row_id = lax.broadcasted_iota(jnp.int32, (CHUNK, 1), 0)
        dacs_c = dacs_c + jnp.where(row_id == CHUNK - 1, dacs_last, 0.0)

        dxs_ref[...] = dxs + dx * cm["dt_e"]
        ddt = sums[3 * CHUNK:]
        dda = _dot01("nn", cm["trit"], dacs_c, "a")
        dda_r = _dot01("nn", dacs_r, cm["tri"], "b")
        dda_rp = jnp.concatenate([dda_r, jnp.zeros((HEAD_PAD - ROW_PAD, CHUNK), F32)], axis=0)
        eye = (cm["li"] == cm["si"]).astype(F32)
        dda = dda + _dot01("nt", eye, dda_rp, "a")
        ddt_ref[...] = ddt + dda * cm["a_row"]
        _accum(dal_ref, jnp.sum(dda * dtc_v, axis=0, keepdims=True) * cm["a_row"], first)
        db_ref[...] = db
        dc_ref[...] = dc
        dh_scr[...] = dhp

    gs = D_STATE * N_GROUPS
    return pl.pallas_call(
        body, name=name, grid=(N_GROUPS, nc),
        in_specs=[sp["xs"], sp["xs"], sp["bm"], sp["cm"], sp["dtc"], sp["dtr"], sp["alr"], sp["alc"],
                  sp["de"], sp["hp"]],
        out_specs=[sp["xs"], sp["bc"], sp["bc"], sp["dtc"], sp["acc"], sp["acc"]],
        out_shape=[jax.ShapeDtypeStruct((T, DI), F32), jax.ShapeDtypeStruct((T, gs), F32),
                   jax.ShapeDtypeStruct((T, gs), F32), jax.ShapeDtypeStruct((N_GROUPS, T, HEAD_PAD), F32),
                   jax.ShapeDtypeStruct((N_GROUPS, 1, HEAD_PAD), F32),
                   jax.ShapeDtypeStruct((N_GROUPS, 1, HEAD_PAD), F32)],
        scratch_shapes=[pltpu.VMEM((gw, D_STATE), F32)],
        compiler_params=_cp("parallel", "arbitrary"),
    )(dy, xbc, xbc, xbc, dtc, dtr, alr, alc, d_e, hprev)


def _adam_math(w, g, m, v):
    m = ADAM_B1 * m + (1.0 - ADAM_B1) * g
    v = ADAM_B2 * v + (1.0 - ADAM_B2) * (g * g)
    m_hat = m / (1.0 - ADAM_B1 ** ADAM_STEP)
    v_hat = v / (1.0 - ADAM_B2 ** ADAM_STEP)
    delta = -ADAM_LR * (m_hat / (jnp.sqrt(v_hat) + ADAM_EPS) + ADAM_WD * w)
    return delta, m, v


def _adamw(name, w, m, v, g, layer=0, prev=None):
    L, R, C = w.shape
    tr = _rows(R, C * 4, ADAM_BLOCK_BYTES)
    tc = C
    while tr * tc * 4 > ADAM_BLOCK_BYTES and tc % 256 == 0:
        tc //= 2
    blk = pl.BlockSpec((None, tr, tc), lambda i, j: (layer, i, j))
    prev = [] if prev is None else list(prev)

    def body(w_ref, m_ref, v_ref, g_ref, *rest):
        go_ref, d_ref, mo_ref, vo_ref = rest[len(prev):]
        gv = g_ref[...]
        delta, mn, vn = _adam_math(w_ref[...], gv, m_ref[...], v_ref[...])
        go_ref[...] = gv
        d_ref[...] = delta
        mo_ref[...] = mn
        vo_ref[...] = vn

    return pl.pallas_call(
        body, name=name, grid=(R // tr, C // tc),
        in_specs=[blk] * 3 + [pl.BlockSpec((tr, tc), lambda i, j: (i, j))] + [_ANY] * len(prev),
        out_specs=[blk] * 4, out_shape=[jax.ShapeDtypeStruct((L, R, C), F32)] * 4,
        input_output_aliases={4 + k: k for k in range(len(prev))},
        compiler_params=_cp("parallel", "parallel"),
    )(w, m, v, g, *prev)


def _sum_peers(name, gathered):
    n, R, C = gathered.shape

    def body(g_ref, o_ref):
        acc = g_ref[0]
        for d in range(1, n):
            acc = acc + g_ref[d]
        o_ref[...] = acc

    return pl.pallas_call(body, name=name, out_shape=jax.ShapeDtypeStruct((R, C), F32))(gathered)


def _place():
    x, y, c = lax.axis_index("x"), lax.axis_index("y"), lax.axis_index("c")
    chips = [(1 - x, y), (x, 1 - y), (1 - x, 1 - y)]
    return x, y, c, chips


def _allgather_small(name, v, after=()):
    R, C = v.shape
    after = list(after)

    def body(v_ref, *rest):
        o_ref, send_sems, recv_sems = rest[len(after):]
        x, y, c, _ = _place()
        me = 4 * x + 2 * y + c
        o_ref[me] = v_ref[...]
        copies = []
        for k in range(1, N_DEV):
            px, py, pc = x ^ (k >> 2), y ^ ((k >> 1) & 1), c ^ (k & 1)
            copies.append(pltpu.make_async_remote_copy(
                src_ref=v_ref, dst_ref=o_ref.at[me], send_sem=send_sems.at[k - 1], recv_sem=recv_sems.at[k - 1],
                device_id=(px, py, pc), device_id_type=MESH))
        for cp in copies:
            cp.start()
        for cp in copies:
            cp.wait()

    return pl.pallas_call(
        body, name=name, out_shape=jax.ShapeDtypeStruct((N_DEV, R, C), F32),
        in_specs=[pl.BlockSpec(memory_space=pltpu.VMEM)] + [_ANY] * len(after),
        out_specs=pl.BlockSpec(memory_space=pltpu.VMEM),
        scratch_shapes=[pltpu.SemaphoreType.DMA((N_DEV - 1,)), pltpu.SemaphoreType.DMA((N_DEV - 1,))],
    )(v, *after)


_HBM = pl.BlockSpec(memory_space=pltpu.HBM)
_SEM = pl.BlockSpec(memory_space=pltpu.SEMAPHORE)
_VMEM = pl.BlockSpec(memory_space=pltpu.VMEM)
_EFFECT = pltpu.SideEffectType.DATAFLOW_SIDE_EFFECTING


def _in_hbm(v):
    return pltpu.with_memory_space_constraint(v, pltpu.HBM)


def _remote(src, dst, send_sem, recv_sem, device):
    return pltpu.make_async_remote_copy(src_ref=src, dst_ref=dst, send_sem=send_sem, recv_sem=recv_sem,
                                        device_id=device, device_id_type=MESH)


def _gather_copies(src, land, ici_s, ici_r, own_s, own_r, arrivals=True):
    x, y, c, chips = _place()
    a = 2 * x + y
    sends, lands = [], []
    for i in range(len(src)):
        own = _remote(src[i], land[i].at[a], own_s.at[i], own_r.at[i], (x, y, 1 - c))
        sends.append(own)
        if arrivals:
            lands.append(own)
        for k, (px, py) in enumerate(chips):
            s, r = ici_s.at[3 * i + k], ici_r.at[3 * i + k]
            sends.append(_remote(src[i].at[c], land[i].at[a, c], s, r, (px, py, c)))
            if arrivals:
                lands.append(_remote(src[i].at[c], land[i].at[2 * px + py, c], s, r, (px, py, c)))
    return sends, lands


def _gather_start(name, shards, after=None):
    n = len(shards)
    lands = [lax.empty((N_CHIPS,) + s.shape, s.dtype) for s in shards]
    n_in = 2 * n + (0 if after is None else 1)

    def body(*refs):
        src, land = refs[:n], refs[n:2 * n]
        sems = refs[n_in:n_in + 4]
        token = refs[-1]
        sends, _ = _gather_copies(src, land, *sems, arrivals=False)
        for cp in sends:
            cp.start()
        token[...] = jnp.zeros_like(token)

    outs = pl.pallas_call(
        body, name=name,
        out_shape=(pltpu.SemaphoreType.DMA((3 * n,)), pltpu.SemaphoreType.DMA((3 * n,)),
                   pltpu.SemaphoreType.DMA((n,)), pltpu.SemaphoreType.DMA((n,)),
                   *[pltpu.HBM(s.shape, s.dtype) for s in shards], *[pltpu.HBM(l.shape, l.dtype) for l in lands],
                   jax.ShapeDtypeStruct((8, 128), F32)),
        in_specs=[_HBM] * (2 * n) + ([] if after is None else [_ANY]),
        out_specs=(_SEM,) * 4 + (_HBM,) * (2 * n) + (_VMEM,),
        input_output_aliases={i: 4 + i for i in range(2 * n)},
        compiler_params=pltpu.CompilerParams(has_side_effects=_EFFECT),
    )(*[_in_hbm(s) for s in shards], *[_in_hbm(l) for l in lands], *([] if after is None else [after]))
    return dict(sems=outs[:4], src=outs[4:4 + n], land=outs[4 + n:4 + 2 * n], token=outs[-1])


def _gather_wait(name, started, after):
    n = len(started["src"])
    after = list(after)

    def body(*refs):
        src, land = refs[:n], refs[n:2 * n]
        sems = refs[2 * n:2 * n + 4]
        sends, lands = _gather_copies(src, land, *sems)
        for cp in sends:
            cp.wait_send()
        for cp in lands:
            cp.wait_recv()

    outs = pl.pallas_call(
        body, name=name,
        out_shape=[pltpu.HBM(v.shape, v.dtype) for v in list(started["src"]) + list(started["land"])],
        in_specs=[_HBM] * (2 * n) + [_SEM] * 4 + [_ANY] * len(after), out_specs=[_HBM] * (2 * n),
        input_output_aliases={i: i for i in range(2 * n)},
        compiler_params=pltpu.CompilerParams(has_side_effects=_EFFECT),
    )(*started["src"], *started["land"], *started["sems"], *after)
    return outs[n:]


def _gather_forward(name, lands):
    n = len(lands)

    def body(*refs):
        buf = refs[n:2 * n]
        send_sems, recv_sems = refs[2 * n:]
        x, y, c, chips = _place()
        sends, lands_ = [], []
        for i in range(n):
            for k, (px, py) in enumerate(chips):
                b = 2 * px + py
                sends.append(_remote(buf[i].at[b, c], buf[i].at[b, c], send_sems.at[i, k], recv_sems.at[i, k],
                                     (x, y, 1 - c)))
                lands_.append(_remote(buf[i].at[b, 1 - c], buf[i].at[b, 1 - c], send_sems.at[i, k],
                                      recv_sems.at[i, k], (x, y, 1 - c)))
        for cp in sends:
            cp.start()
        for cp in sends:
            cp.wait_send()
        for cp in lands_:
            cp.wait_recv()

    return pl.pallas_call(
        body, name=name, in_specs=[_ANY] * n, out_specs=[_ANY] * n,
        out_shape=[jax.ShapeDtypeStruct(l.shape, l.dtype) for l in lands],
        input_output_aliases={i: i for i in range(n)},
        scratch_shapes=[pltpu.SemaphoreType.DMA((n, 3)), pltpu.SemaphoreType.DMA((n, 3))],
    )(*lands)


def _split_start(name, bufs, copies, n_sems, after=()):
    nb = len(bufs)
    after = list(after)

    def body(*refs):
        send_sems, recv_sems = refs[nb + len(after):nb + len(after) + 2]
        token = refs[-1]
        sends, _ = copies(refs[:nb], send_sems, recv_sems, False)
        for cp in sends:
            cp.start()
        token[...] = jnp.zeros_like(token)

    outs = pl.pallas_call(
        body, name=name,
        out_shape=(pltpu.SemaphoreType.DMA((n_sems,)), pltpu.SemaphoreType.DMA((n_sems,)),
                   *[pltpu.HBM(b.shape, b.dtype) for b in bufs], jax.ShapeDtypeStruct((8, 128), F32)),
        in_specs=[_HBM] * nb + [_ANY] * len(after), out_specs=(_SEM,) * 2 + (_HBM,) * nb + (_VMEM,),
        input_output_aliases={i: 2 + i for i in range(nb)},
        compiler_params=pltpu.CompilerParams(has_side_effects=_EFFECT),
    )(*[_in_hbm(b) for b in bufs], *after)
    return dict(sems=outs[:2], bufs=outs[2:2 + nb], token=outs[-1])


def _split_wait(name, started, copies, after):
    nb = len(started["bufs"])
    after = list(after)

    def body(*refs):
        sends, arrivals = copies(refs[:nb], refs[nb], refs[nb + 1], True)
        for cp in sends:
            cp.wait_send()
        for cp in arrivals:
            cp.wait_recv()

    return pl.pallas_call(
        body, name=name, out_shape=[pltpu.HBM(b.shape, b.dtype) for b in started["bufs"]],
        in_specs=[_HBM] * nb + [_SEM] * 2 + [_ANY] * len(after), out_specs=[_HBM] * nb,
        input_output_aliases={i: i for i in range(nb)},
        compiler_params=pltpu.CompilerParams(has_side_effects=_EFFECT),
    )(*started["bufs"], *started["sems"], *after)


def _forward_copies(bufs, send_sems, recv_sems, arrivals):
    x, y, c, chips = _place()
    sends, lands = [], []
    for i, buf in enumerate(bufs):
        for k, (px, py) in enumerate(chips):
            b, s, r = 2 * px + py, send_sems.at[3 * i + k], recv_sems.at[3 * i + k]
            sends.append(_remote(buf.at[b, c], buf.at[b, c], s, r, (x, y, 1 - c)))
            if arrivals:
                lands.append(_remote(buf.at[b, 1 - c], buf.at[b, 1 - c], s, r, (x, y, 1 - c)))
    return sends, lands


def _pair_copies(bufs, send_sems, recv_sems, arrivals):
    x, y, c, _ = _place()
    n = len(bufs) // 2
    copies = [_remote(bufs[i].at[:, 1 - c], bufs[n + i], send_sems.at[i], recv_sems.at[i], (x, y, 1 - c))
              for i in range(n)]
    return copies, copies


def _chip_copies(src, land, send_sems, recv_sems):
    x, y, c, chips = _place()
    return [_remote(src[i].at[2 * px + py], land[i].at[k], send_sems.at[3 * i + k], recv_sems.at[3 * i + k],
                    (px, py, c))
            for i in range(len(src)) for k, (px, py) in enumerate(chips)]


def _chip_start(name, parts):
    n = len(parts)
    lands = [lax.empty((3,) + p.shape[1:], p.dtype) for p in parts]

    def body(*refs):
        src, land = refs[:n], refs[n:2 * n]
        send_sems, recv_sems = refs[2 * n:2 * n + 2]
        token = refs[-1]
        for cp in _chip_copies(src, land, send_sems, recv_sems):
            cp.start()
        token[...] = jnp.zeros_like(token)

    outs = pl.pallas_call(
        body, name=name,
        out_shape=(pltpu.SemaphoreType.DMA((3 * n,)), pltpu.SemaphoreType.DMA((3 * n,)),
                   *[pltpu.HBM(p.shape, p.dtype) for p in parts], *[pltpu.HBM(l.shape, l.dtype) for l in lands],
                   jax.ShapeDtypeStruct((8, 128), F32)),
        in_specs=[_HBM] * (2 * n), out_specs=(_SEM,) * 2 + (_HBM,) * (2 * n) + (_VMEM,),
        input_output_aliases={i: 2 + i for i in range(2 * n)},
        compiler_params=pltpu.CompilerParams(has_side_effects=_EFFECT),
    )(*[_in_hbm(p) for p in parts], *[_in_hbm(l) for l in lands])
    return dict(sems=outs[:2], src=outs[2:2 + n], land=outs[2 + n:2 + 2 * n], token=outs[-1])


def _chip_wait(name, started, after):
    n = len(started["src"])
    after = list(after)

    def body(*refs):
        src, land = refs[:n], refs[n:2 * n]
        send_sems, recv_sems = refs[2 * n:2 * n + 2]
        copies = _chip_copies(src, land, send_sems, recv_sems)
        for cp in copies:
            cp.wait_send()
        for cp in copies:
            cp.wait_recv()

    outs = pl.pallas_call(
        body, name=name,
        out_shape=[pltpu.HBM(v.shape, v.dtype) for v in list(started["src"]) + list(started["land"])],
        in_specs=[_HBM] * (2 * n) + [_SEM] * 2 + [_ANY] * len(after), out_specs=[_HBM] * (2 * n),
        input_output_aliases={i: i for i in range(2 * n)},
        compiler_params=pltpu.CompilerParams(has_side_effects=_EFFECT),
    )(*started["src"], *started["land"], *started["sems"], *after)
    return outs[:n], outs[n:]


def _half_exchange(name, shards):
    n = len(shards)

    def body(*refs):
        buf = refs[n:2 * n]
        send_sems, recv_sems = refs[2 * n:]
        x, y, c, _ = _place()
        copies = [pltpu.make_async_remote_copy(
            src_ref=buf[i].at[c], dst_ref=buf[i].at[c], send_sem=send_sems.at[i], recv_sem=recv_sems.at[i],
            device_id=(x, y, 1 - c), device_id_type=MESH) for i in range(n)]
        for cp in copies:
            cp.start()
        for cp in copies:
            cp.wait()

    return pl.pallas_call(
        body, name=name, in_specs=[_ANY] * n, out_specs=[_ANY] * n,
        out_shape=[jax.ShapeDtypeStruct(s.shape, s.dtype) for s in shards],
        input_output_aliases={i: i for i in range(n)},
        scratch_shapes=[pltpu.SemaphoreType.DMA((n,)), pltpu.SemaphoreType.DMA((n,))],
    )(*shards)


def _pair_add(name, pos, part, sib):
    Q, _, R2, C = part.shape
    tr = _pick(R2, 256)

    def body(pos_ref, p_ref, s_ref, o_ref):
        o_ref[...] = (p_ref[...].astype(F32) + s_ref[...].astype(F32)).astype(BF16)

    return pl.pallas_call(
        body, name=name,
        grid_spec=pltpu.PrefetchScalarGridSpec(
            num_scalar_prefetch=1, grid=(Q, R2 // tr),
            in_specs=[pl.BlockSpec((None, None, tr, C), lambda q, i, pos_ref: (q, pos_ref[1], i, 0)),
                      pl.BlockSpec((None, tr, C), lambda q, i, pos_ref: (q, i, 0))],
            out_specs=pl.BlockSpec((None, tr, C), lambda q, i, pos_ref: (q, i, 0))),
        out_shape=jax.ShapeDtypeStruct((Q, R2, C), BF16),
        compiler_params=_cp("parallel", "parallel"),
    )(pos, part, sib)


def _chip_sum(name, pos, own, got):
    _, R2, C = own.shape
    tr = _pick(R2, 256)

    def body(pos_ref, o_ref, g_ref, out_ref):
        acc = o_ref[...].astype(F32)
        for k in range(3):
            acc = acc + g_ref[k].astype(F32)
        out_ref[...] = acc

    return pl.pallas_call(
        body, name=name,
        grid_spec=pltpu.PrefetchScalarGridSpec(
            num_scalar_prefetch=1, grid=(R2 // tr,),
            in_specs=[pl.BlockSpec((None, tr, C), lambda i, pos_ref: (pos_ref[0], i, 0)),
                      pl.BlockSpec((3, tr, C), lambda i, pos_ref: (0, i, 0))],
            out_specs=pl.BlockSpec((None, tr, C), lambda i, pos_ref: (pos_ref[1], i, 0))),
        out_shape=jax.ShapeDtypeStruct((2, R2, C), F32),
        compiler_params=_cp("parallel"),
    )(pos, own, got)


def _reduce_pair_start(tag, parts):
    split = [p.reshape(p.shape[0], 2, p.shape[1] // 2, p.shape[2]) for p in parts]
    lands = [lax.empty((p.shape[0],) + p.shape[2:], p.dtype) for p in split]
    return _split_start(tag + "_pair_start", split + lands, _pair_copies, len(parts))


def _reduce_pair_end(tag, pos, started, after):
    n = len(started["bufs"]) // 2
    bufs = _split_wait(tag + "_pair_wait", started, _pair_copies, after)
    chip = [_pair_add(f"{tag}_add{i}", pos, bufs[i], bufs[n + i]) for i in range(n)]
    return _chip_start(tag + "_chip_start", chip)


def _half_copies(bufs, send_sems, recv_sems, arrivals):
    x, y, c, _ = _place()
    sends = [_remote(b.at[c], b.at[c], send_sems.at[i], recv_sems.at[i], (x, y, 1 - c)) for i, b in enumerate(bufs)]
    lands = [_remote(b.at[1 - c], b.at[1 - c], send_sems.at[i], recv_sems.at[i], (x, y, 1 - c))
             for i, b in enumerate(bufs)] if arrivals else []
    return sends, lands


def _reduce_sum_start(tag, pos, started, after):
    chip, got = _chip_wait(tag + "_chip_wait", started, after)
    halves = [_chip_sum(f"{tag}_sum{i}", pos, chip[i], got[i]) for i in range(len(chip))]
    return _split_start(tag + "_half_start", halves, _half_copies, len(halves))


def _reduce_sum_end(tag, started, after, shapes):
    full = _split_wait(tag + "_half_wait", started, _half_copies, after)
    return [f.reshape(shp) for f, shp in zip(full, shapes)]


def _reduce_end(tag, pos, started, after, shapes):
    chip, got = _chip_wait(tag + "_chip_wait", started, after)
    halves = [_chip_sum(f"{tag}_sum{i}", pos, chip[i], got[i]) for i in range(len(chip))]
    full = _half_exchange(tag + "_half", halves)
    return [f.reshape(shp) for f, shp in zip(full, shapes)]


def _pad_to(v, mult):
    n = v.shape[0]
    return jnp.pad(v, (0, (-n) % mult))


def kernel(x, p, pool_w, pool_scale, ssm_in_w, ssm_conv_w, ssm_conv_b, ssm_dt_bias, ssm_a_log, ssm_d, ssm_norm_w, ssm_out_w, mlp_w1, mlp_w2, ln_g, ln_b, ple_w, ple_gate_w, loss_target, m_pool_w, m_pool_scale, m_ssm_in_w, m_ssm_conv_w, m_ssm_conv_b, m_ssm_dt_bias, m_ssm_a_log, m_ssm_d, m_ssm_norm_w, m_ssm_out_w, m_mlp_w1, m_mlp_w2, m_ln_g, m_ln_b, m_ple_w, m_ple_gate_w, v_pool_w, v_pool_scale, v_ssm_in_w, v_ssm_conv_w, v_ssm_conv_b, v_ssm_dt_bias, v_ssm_a_log, v_ssm_d, v_ssm_norm_w, v_ssm_out_w, v_mlp_w1, v_mlp_w2, v_ln_g, v_ln_b, v_ple_w, v_ple_gate_w):
    T, D = x.shape[1], x.shape[2]
    NG = len(POOL_WINDOWS)
    GD = D // NG
    DI = ssm_out_w.shape[1] * N_CHIPS
    H = ssm_dt_bias.shape[1]
    HPG = H // N_GROUPS
    GW = DI // N_GROUPS
    GS = N_GROUPS * D_STATE
    CD = DI + 2 * GS
    DF = mlp_w1.shape[2] * N_CHIPS
    PD = ple_w.shape[1]
    NIN = ssm_in_w.shape[2]

    xi, yi, ci = lax.axis_index("x"), lax.axis_index("y"), lax.axis_index("c")
    chip = 2 * xi + yi
    pos = jnp.stack([chip, ci]).astype(jnp.int32)

    x0 = x[0]
    tgt = loss_target[0]
    p0b, p1b = p[0, 0].astype(BF16), p[1, 0].astype(BF16)

    small_sh = jnp.concatenate([ssm_conv_w[0].reshape(-1), ssm_conv_b[0], ssm_norm_w[0],
                                ln_g.reshape(-1), ln_b.reshape(-1)])
    n_sh = small_sh.shape[0]
    small_all = _allgather_small("gather_small", _pad_to(small_sh, 1024).reshape(-1, 128))

    def halves(w, zero=None):
        w = w if zero is None else w + zero
        return w.astype(BF16).reshape((2, w.shape[0] // 2) + w.shape[1:])

    sh_pool = pool_w[0].astype(BF16)
    sh_pool = sh_pool.reshape((2, NG // 2) + sh_pool.shape[1:])
    started = {"ag0a": _gather_start("ag0a_start", [sh_pool, halves(mlp_w1[0])], after=small_all)}
    z0 = started["ag0a"]["token"][0, 0]
    groups = [("ag0b", [halves(mlp_w2[0], z0)]),
              ("ag0c", [halves(ple_gate_w[0], z0), halves(ple_w[0], z0)]),
              ("ag1a", [halves(ssm_in_w[0], z0)]),
              ("ag1b", [halves(ssm_out_w[0], z0), halves(mlp_w1[1], z0)]),
              ("ag1c", [halves(mlp_w2[1], z0), halves(ple_gate_w[1], z0), halves(ple_w[1], z0)])]

    def gather_end(tag, after):
        return _gather_forward(tag + "_fwd", _gather_wait(tag + "_wait", started[tag], after))

    def gather_land(tag, after):
        lands = _gather_wait(tag + "_wait", started[tag], after)
        return _split_start(tag + "_fwd_start", lands, _forward_copies, 3 * len(lands))

    def gather_done(tag, forwarding, after):
        return _split_wait(tag + "_fwd_wait", forwarding, _forward_copies, after)

    g_pool, g = gather_end("ag0a", [sh for _, shards in groups for sh in shards])
    prev = g_pool
    for tag, shards in groups:
        started[tag] = _gather_start(tag + "_start", shards, after=prev)
        prev = started[tag]["token"]
    all_started = prev

    g_w1, g_w2, g_pw, g_gw = {}, {}, {}, {}

    def set_w1(l, g):
        g_w1[l] = g.reshape(N_CHIPS, 1, D, DF // N_CHIPS)

    def set_w2(l, g2):
        g_w2[l] = g2.reshape(DF, D)

    def set_gate(l, ggw, gpw):
        g_gw[l] = ggw.reshape(D, D)
        g_pw[l] = gpw.reshape(N_CHIPS, 1, PD, D // N_CHIPS)

    set_w1(0, g)
    w_pool = jnp.transpose(g_pool.reshape(N_CHIPS, NG, GD // N_CHIPS, GD), (1, 0, 2, 3)).reshape(NG, GD, GD)

    small_all = small_all.reshape(N_DEV, -1)[0::2, :n_sh]
    cdq, niq, dq = CD // N_CHIPS, DI // N_CHIPS, D // N_CHIPS
    o = 0
    conv_w = jnp.concatenate([small_all[q, o:o + CONV_K * cdq].reshape(CONV_K, cdq) for q in range(N_CHIPS)], axis=1)
    o += CONV_K * cdq
    conv_b = small_all[:, o:o + cdq].reshape(1, CD)
    o += cdq
    norm_w = small_all[:, o:o + niq].reshape(1, DI)
    o += niq
    lng = jnp.transpose(small_all[:, o:o + 4 * dq].reshape(N_CHIPS, 2, 2, dq), (1, 2, 0, 3)).reshape(2, 2, 1, D)
    o += 4 * dq
    lnb = jnp.transpose(small_all[:, o:o + 4 * dq].reshape(N_CHIPS, 2, 2, dq), (1, 2, 0, 3)).reshape(2, 2, 1, D)

    pooled = _pool_fwd("pool_fwd", x0)
    hraw = _mm("pool_mm", "nn", pooled, w_pool, T, D, GD, tn=GD, deps=[all_started],
               a_spec=lambda tm, tn, tk: pl.BlockSpec((tm, tk), lambda i, j, k: (i, j)),
               b_spec=lambda tm, tn, tk: pl.BlockSpec((None, tk, tn), lambda i, j, k: (j, 0, 0)))
    x1, x1b, xh1, rs1 = _res_ln("ln00", x0, hraw, lng[0, 0], lnb[0, 0], scale=pool_scale)

    def mlp_fwd(l, xb, land_rest, deps=()):
        a, h2b = _mm(f"mlp{l}_up", "nn", xb, g_w1[l], T, DF, D, tn=min(1024, DF // N_CHIPS), deps=deps,
                     b_spec=_colshard_b(0, DF // N_CHIPS), out_dtype=(F32, BF16),
                     epi=lambda acc: (acc, jnp.square(jnp.maximum(acc, 0.0))))
        land_rest(a)
        h = _mm(f"mlp{l}_down", "nn", h2b, g_w2[l], T, D, DF)
        return a, h2b, h

    def ple_fwd(l, xb, pb):
        gl = _mm(f"gate{l}_logit", "nn", xb, g_gw[l], T, D, D)
        e = _mm(f"gate{l}_emb", "nn", pb, g_pw[l], T, D, PD, tn=min(1024, D // N_CHIPS),
                b_spec=_colshard_b(0, D // N_CHIPS))
        return gl, e

    a0, h2b0, h0 = mlp_fwd(0, x1b, lambda a: set_w2(0, *gather_end("ag0b", [a])))
    set_gate(0, *gather_end("ag0c", [h0]))
    x2, x2b, xh2, rs2 = _res_ln("ln01", x1, h0, lng[0, 1], lnb[0, 1])
    gl0, e0 = ple_fwd(0, x2b, p0b)
    x3, x3b, gate0 = _gate_fwd("gate0", x2, gl0, e0)

    g_in, = gather_end("ag1a", [x3b])

    def in_proj_cols(lo, hi):
        parts = [g_in[q].reshape(D, NIN)[:, max(lo - q * NIN, 0):min(hi - q * NIN, NIN)]
                 for q in range(N_CHIPS) if lo < (q + 1) * NIN and hi > q * NIN]
        return parts[0] if len(parts) == 1 else jnp.concatenate(parts, axis=1)

    w_z, w_xbc, w_dt = in_proj_cols(0, DI), in_proj_cols(DI, DI + CD), in_proj_cols(DI + CD, N_CHIPS * NIN)
    z = _mm("ssm_in_z", "nn", x3b, w_z, T, DI, D)
    xbc_pre = _mm("ssm_in_xbc", "nn", x3b, w_xbc, T, CD, D)
    dt_pre = _mm("ssm_in_dt", "nn", x3b, w_dt, T, H, D)
    xbc = _conv_fwd("conv_fwd", xbc_pre, conv_w, conv_b)
    dt = _dt_fwd("dt_fwd", dt_pre, ssm_dt_bias)
    dt_g = jnp.transpose(dt.reshape(T, N_GROUPS, HPG), (1, 0, 2))
    dtc = jnp.pad(dt_g, ((0, 0), (0, 0), (0, HEAD_PAD - HPG)))
    dtr = jnp.pad(jnp.transpose(dt_g, (0, 2, 1)), ((0, 0), (0, ROW_PAD - HPG), (0, 0)))
    al_g = ssm_a_log.reshape(N_GROUPS, HPG)
    alr = jnp.pad(al_g, ((0, 0), (0, HEAD_PAD - HPG)))[:, None, :]
    alc = jnp.pad(al_g, ((0, 0), (0, ROW_PAD - HPG)))[:, :, None]
    d_e = jnp.repeat(ssm_d.reshape(N_GROUPS, HPG), HEAD_DIM, axis=1)[:, None, :]
    fw1b = gather_land("ag1b", [xbc])
    ysc, hprev = _ssd_fwd("ssd_fwd", xbc, dtc, dtr, alr, alc, d_e, DI, deps=[fw1b["token"]])
    g_out, g = gather_done("ag1b", fw1b, [ysc])
    set_w1(1, g)
    w_out = g_out.reshape(DI, D)
    ynb, rsn = _gnorm_fwd("gnorm_fwd", ysc, z, norm_w)
    h1 = _mm("ssm_out", "nn", ynb, w_out, T, D, DI)
    fw1c = gather_land("ag1c", [h1])
    x4, x4b, xh4, rs4 = _res_ln("ln10", x3, h1, lng[1, 0], lnb[1, 0])

    def land_1c(a):
        g2, ggw, gpw = gather_done("ag1c", fw1c, [a])
        set_w2(1, g2)
        set_gate(1, ggw, gpw)

    a1, h2b1, h2 = mlp_fwd(1, x4b, land_1c, deps=[fw1c["token"]])
    x5, x5b, xh5, rs5 = _res_ln("ln11", x4, h2, lng[1, 1], lnb[1, 1])
    gl1, e1 = ple_fwd(1, x5b, p1b)
    dx6, gate1, loss_parts = _gate_loss("gate1_loss", x5, gl1, e1, tgt)
    loss_local = jnp.sum(loss_parts[0::8, 0])

    def ple_bwd(l, dxo, gate, e, xb, pb, deps=()):
        dgl, de = _gate_bwd(f"gate{l}_bwd", dxo, gate, e)
        d_gw = _mm(f"gate{l}_dw", "tn", xb, dgl, D, D, T, out_dtype=BF16, deps=deps).reshape(N_CHIPS, D // N_CHIPS, D)
        d_pw = _mm(f"gate{l}_dpw", "tn", pb, de, PD, D, T, out_dtype=BF16, tn=min(1024, D // N_CHIPS),
                   o_shape=(N_CHIPS, PD, D // N_CHIPS), o_spec=_colshard_o(D // N_CHIPS))
        dx = _mm(f"gate{l}_dx", "nt", dgl, g_gw[l], T, D, D, deps=deps,
                 epi=lambda acc, r: acc + r, extras=[dxo])
        return dx, d_gw, d_pw

    def mlp_bwd(l, du, dub, a, h2b, xb):
        d_w2 = _mm(f"mlp{l}_dw2", "tn", h2b, dub, DF, D, T, out_dtype=BF16).reshape(N_CHIPS, DF // N_CHIPS, D)
        da = _mm(f"mlp{l}_da", "nt", dub, g_w2[l], T, DF, D, out_dtype=BF16,
                 epi=lambda acc, av: acc * (2.0 * jnp.maximum(av, 0.0)), extras=[a])
        d_w1 = _mm(f"mlp{l}_dw1", "tn", xb, da, D, DF, T, out_dtype=BF16, tn=min(1024, DF // N_CHIPS),
                   o_shape=(N_CHIPS, D, DF // N_CHIPS), o_spec=_colshard_o(DF // N_CHIPS))
        dx = _mm(f"mlp{l}_dx", "nt", da, g_w1[l], T, D, DF, k_unit=DF // N_CHIPS,
                 b_spec=_colshard_bt(0, DF // N_CHIPS), epi=lambda acc, r: acc + ALPHA * r, extras=[du])
        return dx, d_w1, d_w2

    dx5, d_gw1, d_pw1 = ple_bwd(1, dx6, gate1, e1, x5b, p1b)
    du5, du5b, dg11, db11 = _ln_bwd("ln11_bwd", dx5, xh5, rs5, lng[1, 1])
    dx4, d_w1_1, d_w2_1 = mlp_bwd(1, du5, du5b, a1, h2b1, x4b)
    parts1a = [d_w1_1, d_w2_1, d_gw1, d_pw1]
    pair1a = _reduce_pair_start("rs1a", parts1a)
    du4, du4b, dg10, db10 = _ln_bwd("ln10_bwd", dx4, xh4, rs4, lng[1, 0])
    d_wout = _mm("ssm_out_dw", "tn", ynb, du4b, DI, D, T, out_dtype=BF16,
                 deps=[pair1a["token"]]).reshape(N_CHIPS, DI // N_CHIPS, D)
    red1a = _reduce_pair_end("rs1a", pos, pair1a, [d_wout])
    dyn = _mm("ssm_out_dx", "nt", du4b, w_out, T, DI, D, deps=[red1a["token"]])
    dysc, dzb, dnorm_w = _gnorm_bwd("gnorm_bwd", dyn, ysc, z, norm_w, rsn)
    dxs, dbm, dcm, ddt_g, dalog_g, dd_g = _ssd_bwd("ssd_bwd", dysc, xbc, dtc, dtr, alr, alc, d_e, hprev, DI)
    dconv, dconv_w, dconv_b = _conv_bwd_a("conv_bwd_a", dxs, dbm, dcm, xbc_pre, conv_w, conv_b)
    dxbcb = _conv_bwd_b("conv_bwd_b", dconv, conv_w)
    ddt = jnp.transpose(ddt_g[:, :, :HPG], (1, 0, 2)).reshape(T, H)
    _, ddtpb, ddt_bias = _dt_bwd("dt_bwd", ddt, dt_pre, ssm_dt_bias)
    da_log = dalog_g[:, 0, :HPG].reshape(1, H)
    dd_skip = dd_g[:, 0, :HPG].reshape(1, H)
    d_wz = _mm("ssm_in_dwz", "tn", x3b, dzb, D, DI, T, out_dtype=BF16)
    d_wx = _mm("ssm_in_dwx", "tn", x3b, dxbcb, D, CD, T, out_dtype=BF16)
    d_wdt = _mm("ssm_in_dwdt", "tn", x3b, ddtpb, D, H, T, out_dtype=BF16)

    def in_proj_shard(q):
        parts = []
        for piece, start in ((d_wz, 0), (d_wx, DI), (d_wdt, DI + CD)):
            lo, hi = max(q * NIN - start, 0), min((q + 1) * NIN - start, piece.shape[1])
            if lo < hi:
                parts.append(piece[:, lo:hi])
        return parts[0] if len(parts) == 1 else jnp.concatenate(parts, axis=1)

    d_win = jnp.stack([in_proj_shard(q) for q in range(N_CHIPS)])
    parts1b = [d_win, d_wout]
    pair1b = _reduce_pair_start("rs1b", parts1b)
    dx3 = _mm("ssm_in_dxz", "nt", dzb, w_z, T, D, DI, epi=lambda acc, r: acc + ALPHA * r, extras=[du4],
              deps=[pair1b["token"]])
    dx3 = _mm("ssm_in_dxx", "nt", dxbcb, w_xbc, T, D, CD, epi=lambda acc, r: acc + r, extras=[dx3])
    dx3 = _mm("ssm_in_dxdt", "nt", ddtpb, w_dt, T, D, H, epi=lambda acc, r: acc + r, extras=[dx3])

    red1b = _reduce_pair_end("rs1b", pos, pair1b, [dx3])

    dx2, d_gw0, d_pw0 = ple_bwd(0, dx3, gate0, e0, x2b, p0b, deps=[red1b["token"]])
    du2, du2b, dg01, db01 = _ln_bwd("ln01_bwd", dx2, xh2, rs2, lng[0, 1])
    dx1, d_w1_0, d_w2_0 = mlp_bwd(0, du2, du2b, a0, h2b0, x1b)
    parts0a = [d_w1_0, d_w2_0, d_gw0, d_pw0]
    pair0a = _reduce_pair_start("rs0a", parts0a)
    du1, dhrb, dg00, db00, dscale = _ln_bwd("ln00_bwd", dx1, xh1, rs1, lng[0, 0], hraw=hraw, scale=pool_scale)
    dln_g = jnp.stack([jnp.stack([dg00, dg01]), jnp.stack([dg10, dg11])]).reshape(-1)
    dln_b = jnp.stack([jnp.stack([db00, db01]), jnp.stack([db10, db11])]).reshape(-1)
    small_g = jnp.concatenate([dscale.reshape(-1), dconv_w.reshape(-1), dconv_b.reshape(-1), ddt_bias.reshape(-1),
                               da_log.reshape(-1), dd_skip.reshape(-1), dnorm_w.reshape(-1), dln_g, dln_b,
                               loss_local.reshape(1)])
    n_sg = small_g.shape[0]
    sg_all = _allgather_small("gather_small_grads", _pad_to(small_g, 1024).reshape(-1, 128),
                              after=[d_w1_0, d_w2_0])
    d_wpool = _mm("pool_dw", "tn", pooled, dhrb, D, GD, T, tm=GD, tn=GD, out_dtype=BF16, deps=[pair0a["token"]],
                  b_spec=lambda tm, tn, tk: pl.BlockSpec((tk, tn), lambda i, j, k: (k, i)),
                  o_shape=(NG, GD, GD), o_spec=lambda tm, tn, tk: pl.BlockSpec((None, tm, tn), lambda i, j, k: (i, 0, 0)))
    d_wpool = jnp.transpose(d_wpool.reshape(NG, N_CHIPS, GD // N_CHIPS, GD), (1, 0, 2, 3)).reshape(N_CHIPS, NG * GD // N_CHIPS, GD)
    red0a = _reduce_pair_end("rs0a", pos, pair0a, [d_wpool, sg_all])
    parts0b = [d_wpool]
    pair0b = _reduce_pair_start("rs0b", parts0b)
    dpooled = _mm("pool_dx", "nt", dhrb, w_pool, T, D, GD, tn=GD, deps=[red0a["token"], pair0b["token"]],
                  a_spec=lambda tm, tn, tk: pl.BlockSpec((tm, tk), lambda i, j, k: (i, j)),
                  b_spec=lambda tm, tn, tk: pl.BlockSpec((None, tn, tk), lambda i, j, k: (j, 0, 0)))
    grad_x = _pool_bwd("pool_bwd", dpooled, du1)
    red0b = _reduce_pair_end("rs0b", pos, pair0b, [grad_x])

    def adam(name, w, m, v, g, layer=0, prev=None):
        w3, m3, v3 = (t.reshape((t.shape[0], -1, t.shape[-1])) for t in (w, m, v))
        return _adamw(name, w3, m3, v3, g, layer, prev)

    sum1a = _reduce_sum_start("rs1a", pos, red1a, [red0b["token"]])
    sum1b = _reduce_sum_start("rs1b", pos, red1b, [sum1a["token"]])
    r_w1_1, r_w2_1, r_gw1, r_pw1 = _reduce_sum_end("rs1a", sum1a, [sum1b["token"]], [p.shape[1:] for p in parts1a])
    stacked = {"mlp_w1": (mlp_w1, m_mlp_w1, v_mlp_w1), "mlp_w2": (mlp_w2, m_mlp_w2, v_mlp_w2),
               "ple_w": (ple_w, m_ple_w, v_ple_w), "ple_gate_w": (ple_gate_w, m_ple_gate_w, v_ple_gate_w)}
    grads1 = {"mlp_w1": r_w1_1, "mlp_w2": r_w2_1, "ple_w": r_pw1, "ple_gate_w": r_gw1}
    upper = {nm: adam(f"adam_{nm}_l1", *stacked[nm], grads1[nm], layer=1) for nm in stacked}
    r_in, r_out = _reduce_sum_end("rs1b", sum1b, [t[1] for t in upper.values()], [p.shape[1:] for p in parts1b])
    big = {"ssm_in_w": [jnp.swapaxes(t, 1, 2) for t in adam(
               "adam_ssm_in_w", *(jnp.swapaxes(t, 1, 2) for t in (ssm_in_w, m_ssm_in_w, v_ssm_in_w)), r_in.T)],
           "ssm_out_w": adam("adam_ssm_out_w", ssm_out_w, m_ssm_out_w, v_ssm_out_w, r_out)}

    sg = _sum_peers("sum_small_grads", sg_all).reshape(-1)[:n_sg]
    o = 0

    def take(nel):
        nonlocal o
        v = sg[o:o + nel]
        o += nel
        return v

    g_scale = take(D).reshape(1, D)
    g_conv_w_full = take(CONV_K * CD).reshape(CONV_K, CD)
    g_conv_b_full = take(CD).reshape(1, CD)
    g_dt_bias = take(H).reshape(1, H)
    g_a_log = take(H).reshape(1, H)
    g_d = take(H).reshape(1, H)
    g_norm_full = take(DI).reshape(1, DI)
    g_lng_full = take(4 * D).reshape(2, 2, D)
    g_lnb_full = take(4 * D).reshape(2, 2, D)
    loss = take(1).reshape(())
    g_conv_w = lax.dynamic_slice_in_dim(g_conv_w_full, chip * cdq, cdq, axis=1)[None]
    g_conv_b = lax.dynamic_slice_in_dim(g_conv_b_full, chip * cdq, cdq, axis=1)
    g_norm = lax.dynamic_slice_in_dim(g_norm_full, chip * niq, niq, axis=1)
    g_lng = lax.dynamic_slice_in_dim(g_lng_full, chip * dq, dq, axis=2)
    g_lnb = lax.dynamic_slice_in_dim(g_lnb_full, chip * dq, dq, axis=2)

    small = [("pool_scale", pool_scale, m_pool_scale, v_pool_scale, g_scale),
             ("ssm_conv_w", ssm_conv_w, m_ssm_conv_w, v_ssm_conv_w, g_conv_w),
             ("ssm_conv_b", ssm_conv_b, m_ssm_conv_b, v_ssm_conv_b, g_conv_b),
             ("ssm_dt_bias", ssm_dt_bias, m_ssm_dt_bias, v_ssm_dt_bias, g_dt_bias),
             ("ssm_a_log", ssm_a_log, m_ssm_a_log, v_ssm_a_log, g_a_log),
             ("ssm_d", ssm_d, m_ssm_d, v_ssm_d, g_d),
             ("ssm_norm_w", ssm_norm_w, m_ssm_norm_w, v_ssm_norm_w, g_norm),
             ("ln_g", ln_g, m_ln_g, v_ln_g, g_lng),
             ("ln_b", ln_b, m_ln_b, v_ln_b, g_lnb)]

    def pack(idx):
        flat = _pad_to(jnp.concatenate([s[idx].reshape(-1) for s in small]), 1024)
        return flat.reshape(1, -1, 128)

    sm_out = _adamw("adam_small", pack(1), pack(2), pack(3), pack(4)[0])
    small_res = {}
    o = 0
    for s in small:
        nel = s[1].size
        small_res[s[0]] = [t.reshape(-1)[o:o + nel].reshape(s[1].shape) for t in sm_out]
        o += nel

    r_w1_0, r_w2_0, r_gw0, r_pw0 = _reduce_end(
        "rs0a", pos, red0a, [big["ssm_in_w"][1], big["ssm_out_w"][1], sm_out[1]] + [upper[nm][1] for nm in upper],
        [p.shape[1:] for p in parts0a])
    r_pool, = _reduce_end("rs0b", pos, red0b, [r_w1_0], [p.shape[1:] for p in parts0b])
    grads0 = {"mlp_w1": r_w1_0, "mlp_w2": r_w2_0, "ple_w": r_pw0, "ple_gate_w": r_gw0}
    big["pool_w"] = adam("adam_pool_w", pool_w, m_pool_w, v_pool_w, r_pool)
    for nm in stacked:
        big[nm] = adam(f"adam_{nm}_l0", *stacked[nm], grads0[nm], layer=0, prev=upper[nm])
    shapes = {"pool_w": pool_w.shape, "ssm_in_w": ssm_in_w.shape, "ssm_out_w": ssm_out_w.shape,
              **{nm: stacked[nm][0].shape for nm in stacked}}
    big = {nm: [t.reshape(shapes[nm]) for t in big[nm]] for nm in big}

    order = ["pool_w", "pool_scale", "ssm_in_w", "ssm_conv_w", "ssm_conv_b", "ssm_dt_bias", "ssm_a_log", "ssm_d",
             "ssm_norm_w", "ssm_out_w", "mlp_w1", "mlp_w2", "ln_g", "ln_b", "ple_w", "ple_gate_w"]
    res = {**big, **small_res}
    outs = [loss, grad_x[None]]
    for kind in range(4):
        outs += [res[nm][kind] for nm in order]
    return tuple(outs)
```

```python
import jax
import jax.numpy as jnp
from jax import lax
from jax.experimental import pallas as pl
from jax.experimental.pallas import tpu as pltpu
from jax.experimental.pallas import tpu_sc as plsc

F32 = jnp.float32
BF16 = jnp.bfloat16
MESH = pl.DeviceIdType.MESH

DEPTH = 2
ALPHA = (2.0 * DEPTH) ** 0.25
LN_EPS = 1e-5
RMS_EPS = 1e-5
POOL_WINDOWS = (2, 4, 8, 16)
POOL_HALO = 16
HEAD_DIM = 64
HEAD_PAIR = 2 * HEAD_DIM
N_GROUPS = 8
D_STATE = 128
CHUNK = 128
CONV_K = 4
CONV_HALO = 8
HEAD_PAD = 128
ROW_PAD = 8
N_CHIPS = 4
N_DEV = 8
ADAM_LR = 0.001
ADAM_B1 = 0.9
ADAM_B2 = 0.999
ADAM_EPS = 1e-08
ADAM_WD = 0.01
ADAM_STEP = 10
VMEM_LIMIT = 56 * 1024 * 1024
ADAM_BLOCK_BYTES = 1024 * 1024
MM_VMEM_BUDGET = 40 * 1024 * 1024


_ANY = pl.BlockSpec(memory_space=pl.ANY)


def _cp(*sem):
    return pltpu.CompilerParams(dimension_semantics=sem, vmem_limit_bytes=VMEM_LIMIT)


def _pick(dim, pref):
    t = pref
    while t >= 128:
        if dim % t == 0:
            return t
        t //= 2
    return dim


def _rows(rows, row_bytes, budget):
    t = rows
    while t * row_bytes > budget and t % 16 == 0:
        t //= 2
    return t


def _sigmoid(v):
    return 1.0 / (1.0 + jnp.exp(-v))


_DIMS = {"nn": (((1,), (0,)), ((), ())), "nt": (((1,), (1,)), ((), ())), "tn": (((0,), (0,)), ((), ()))}


def _pick_k(k_unit, fixed_bytes, per_k_bytes):
    for n in range(1, k_unit // 128 + 1):
        if k_unit % n == 0 and (n == 1 or (k_unit // n) % 128 == 0):
            if fixed_bytes + (k_unit // n) * per_k_bytes <= MM_VMEM_BUDGET:
                return k_unit // n
    return min(k_unit, 128)


def _mm(name, form, a, b, M, N, K, *, tm=1024, tn=1024, k_unit=None, a_spec=None, b_spec=None,
        o_shape=None, o_spec=None, out_dtype=F32, pro=None, epi=None, extras=(), deps=()):
    tm, tn = _pick(M, tm), _pick(N, tn)
    out_dtypes = out_dtype if isinstance(out_dtype, tuple) else (out_dtype,)
    n_out = len(out_dtypes)
    in_place = n_out == 1 and out_dtypes[0] == F32
    fixed = 2 * tm * tn * (sum(jnp.dtype(d).itemsize for d in out_dtypes) + 4 * len(extras))
    fixed += 0 if in_place else 4 * tm * tn
    tk = _pick_k(K if k_unit is None else k_unit, fixed,
                 2 * (tm * a.dtype.itemsize + tn * b.dtype.itemsize))
    nk = K // tk
    if a_spec is None:
        a_spec = (pl.BlockSpec((tk, tm), lambda i, j, k: (k, i)) if form == "tn"
                  else pl.BlockSpec((tm, tk), lambda i, j, k: (i, k)))
    else:
        a_spec = a_spec(tm, tn, tk)
    if b_spec is None:
        b_spec = (pl.BlockSpec((tn, tk), lambda i, j, k: (j, k)) if form == "nt"
                  else pl.BlockSpec((tk, tn), lambda i, j, k: (k, j)))
    else:
        b_spec = b_spec(tm, tn, tk)
    if o_spec is None:
        o_spec = pl.BlockSpec((tm, tn), lambda i, j, k: (i, j))
        o_shape = (M, N)
    else:
        o_spec = o_spec(tm, tn, tk)
    ex_arrays = [e for e in extras]
    ex_specs = [pl.BlockSpec((tm, tn), lambda i, j, k: (i, j)) for _ in extras]
    ne = len(ex_arrays)
    nd = len(deps)
    dep_specs = [pl.BlockSpec((8, 128), lambda i, j, k: (0, 0)) for _ in deps]
    dims = _DIMS[form]
    use_scratch = nk > 1 and not in_place

    def body(a_ref, b_ref, *rest):
        ex_refs = rest[:ne]
        o_refs = rest[ne + nd:ne + nd + n_out]
        at = a_ref[...]
        if pro is not None:
            at = pro(at)
        p = lax.dot_general(at.astype(BF16), b_ref[...].astype(BF16), dims, preferred_element_type=F32)

        def finish(acc):
            res = acc if epi is None else epi(acc, *[r[...] for r in ex_refs])
            res = res if isinstance(res, tuple) else (res,)
            for o_ref, r, d in zip(o_refs, res, out_dtypes):
                o_ref[...] = r.astype(d)

        if nk == 1:
            finish(p)
        else:
            acc_ref = rest[ne + nd + n_out] if use_scratch else o_refs[0]
            k = pl.program_id(2)

            @pl.when(k == 0)
            def _():
                acc_ref[...] = p

            @pl.when(jnp.logical_and(k > 0, k < nk - 1))
            def _():
                acc_ref[...] += p

            @pl.when(k == nk - 1)
            def _():
                finish(acc_ref[...] + p)

    res = pl.pallas_call(
        body, name=name, grid=(M // tm, N // tn, nk),
        in_specs=[a_spec, b_spec] + ex_specs + dep_specs, out_specs=[o_spec] * n_out,
        out_shape=[jax.ShapeDtypeStruct(o_shape, d) for d in out_dtypes],
        scratch_shapes=[pltpu.VMEM((tm, tn), F32)] if use_scratch else [],
        compiler_params=_cp("parallel", "parallel", "arbitrary"),
    )(a, b, *ex_arrays, *deps)
    return res if n_out > 1 else res[0]


def _colshard_b(l, n_per):
    def make(tm, tn, tk):
        nb = n_per // tn
        return pl.BlockSpec((None, None, tk, tn), lambda i, j, k: (j // nb, l, k, j % nb))
    return make


def _colshard_bt(l, n_per):
    def make(tm, tn, tk):
        nb = n_per // tk
        return pl.BlockSpec((None, None, tn, tk), lambda i, j, k: (k // nb, l, j, k % nb))
    return make


def _colshard_o(n_per):
    def make(tm, tn, tk):
        nb = n_per // tn
        return pl.BlockSpec((None, tm, tn), lambda i, j, k: (j // nb, i, j % nb))
    return make


def _res_ln(name, xprev, h, g, b, scale=None):
    T, D = xprev.shape
    tr = _pick(T, 256)
    row = pl.BlockSpec((tr, D), lambda i: (i, 0))
    vec = pl.BlockSpec((1, D), lambda i: (0, 0))
    has_scale = scale is not None

    def body(*refs):
        if has_scale:
            x_ref, h_ref, s_ref, g_ref, b_ref, y_ref, yb_ref, xh_ref, rs_ref = refs
            hh = h_ref[...] * s_ref[...]
        else:
            x_ref, h_ref, g_ref, b_ref, y_ref, yb_ref, xh_ref, rs_ref = refs
            hh = h_ref[...]
        u = ALPHA * x_ref[...] + hh
        mu = jnp.mean(u, axis=-1, keepdims=True)
        d = u - mu
        var = jnp.mean(d * d, axis=-1, keepdims=True)
        rs = lax.rsqrt(var + LN_EPS)
        xh = d * rs
        y = xh * g_ref[...] + b_ref[...]
        y_ref[...] = y
        yb_ref[...] = y.astype(BF16)
        xh_ref[...] = xh
        rs_ref[...] = rs

    ins = [xprev, h] + ([scale] if has_scale else []) + [g, b]
    specs = [row, row] + ([vec] if has_scale else []) + [vec, vec]
    return pl.pallas_call(
        body, name=name, grid=(T // tr,), in_specs=specs,
        out_specs=[row, row, row, pl.BlockSpec((tr, 1), lambda i: (i, 0))],
        out_shape=[jax.ShapeDtypeStruct((T, D), F32), jax.ShapeDtypeStruct((T, D), BF16),
                   jax.ShapeDtypeStruct((T, D), F32), jax.ShapeDtypeStruct((T, 1), F32)],
        compiler_params=_cp("parallel"),
    )(*ins)


def _accum(ref, part, first):
    @pl.when(first)
    def _():
        ref[...] = part

    @pl.when(jnp.logical_not(first))
    def _():
        ref[...] += part


def _ln_bwd(name, dy, xh, rs, g, hraw=None, scale=None):
    T, D = dy.shape
    tr = _pick(T, 256)
    row = pl.BlockSpec((tr, D), lambda i: (i, 0))
    vec = pl.BlockSpec((1, D), lambda i: (0, 0))
    has_scale = scale is not None

    def body(*refs):
        if has_scale:
            dy_ref, xh_ref, rs_ref, g_ref, hr_ref, s_ref, du_ref, dub_ref, dg_ref, db_ref, ds_ref = refs
        else:
            dy_ref, xh_ref, rs_ref, g_ref, du_ref, dub_ref, dg_ref, db_ref = refs
        first = pl.program_id(0) == 0
        dyv = dy_ref[...]
        xhv = xh_ref[...]
        dxh = dyv * g_ref[...]
        m1 = jnp.mean(dxh, axis=-1, keepdims=True)
        m2 = jnp.mean(dxh * xhv, axis=-1, keepdims=True)
        du = rs_ref[...] * (dxh - m1 - xhv * m2)
        du_ref[...] = du
        if has_scale:
            dub_ref[...] = (du * s_ref[...]).astype(BF16)
            _accum(ds_ref, jnp.sum(du * hr_ref[...], axis=0, keepdims=True), first)
        else:
            dub_ref[...] = du.astype(BF16)
        _accum(dg_ref, jnp.sum(dyv * xhv, axis=0, keepdims=True), first)
        _accum(db_ref, jnp.sum(dyv, axis=0, keepdims=True), first)

    ins = [dy, xh, rs, g] + ([hraw, scale] if has_scale else [])
    specs = [row, row, pl.BlockSpec((tr, 1), lambda i: (i, 0)), vec] + ([row, vec] if has_scale else [])
    n_vec = 3 if has_scale else 2
    return pl.pallas_call(
        body, name=name, grid=(T // tr,), in_specs=specs,
        out_specs=[row, row] + [vec] * n_vec,
        out_shape=[jax.ShapeDtypeStruct((T, D), F32), jax.ShapeDtypeStruct((T, D), BF16)]
        + [jax.ShapeDtypeStruct((1, D), F32)] * n_vec,
        compiler_params=_cp("arbitrary"),
    )(*ins)


def _gate_fwd(name, x, gl, e):
    T, D = x.shape
    tr = _pick(T, 256)
    row = pl.BlockSpec((tr, D), lambda i: (i, 0))

    def body(x_ref, gl_ref, e_ref, xo_ref, xob_ref, gate_ref):
        gate = _sigmoid(gl_ref[...])
        xo = x_ref[...] + gate * e_ref[...]
        xo_ref[...] = xo
        xob_ref[...] = xo.astype(BF16)
        gate_ref[...] = gate

    return pl.pallas_call(
        body, name=name, grid=(T // tr,), in_specs=[row, row, row], out_specs=[row, row, row],
        out_shape=[jax.ShapeDtypeStruct((T, D), F32), jax.ShapeDtypeStruct((T, D), BF16),
                   jax.ShapeDtypeStruct((T, D), F32)],
        compiler_params=_cp("parallel"),
    )(x, gl, e)


def _gate_loss(name, x, gl, e, tgt):
    T, D = x.shape
    tr = _pick(T, 256)
    row = pl.BlockSpec((tr, D), lambda i: (i, 0))

    def body(x_ref, gl_ref, e_ref, t_ref, dy_ref, gate_ref, lp_ref):
        gate = _sigmoid(gl_ref[...])
        err = x_ref[...] + gate * e_ref[...] - t_ref[...]
        dy_ref[...] = err * (1.0 / D)
        gate_ref[...] = gate
        s = jnp.sum(jnp.mean(err * err, axis=-1, keepdims=True), axis=0, keepdims=True)
        lp_ref[...] = jnp.broadcast_to(0.5 * s, (8, 128))

    return pl.pallas_call(
        body, name=name, grid=(T // tr,), in_specs=[row] * 4,
        out_specs=[row, row, pl.BlockSpec((8, 128), lambda i: (i, 0))],
        out_shape=[jax.ShapeDtypeStruct((T, D), F32), jax.ShapeDtypeStruct((T, D), F32),
                   jax.ShapeDtypeStruct((T // tr * 8, 128), F32)],
        compiler_params=_cp("parallel"),
    )(x, gl, e, tgt)


def _gate_bwd(name, dxo, gate, e):
    T, D = dxo.shape
    tr = _pick(T, 256)
    row = pl.BlockSpec((tr, D), lambda i: (i, 0))

    def body(d_ref, gate_ref, e_ref, dgl_ref, de_ref):
        d = d_ref[...]
        gate = gate_ref[...]
        dgl_ref[...] = (d * e_ref[...] * gate * (1.0 - gate)).astype(BF16)
        de_ref[...] = (d * gate).astype(BF16)

    return pl.pallas_call(
        body, name=name, grid=(T // tr,), in_specs=[row] * 3, out_specs=[row, row],
        out_shape=[jax.ShapeDtypeStruct((T, D), BF16)] * 2,
        compiler_params=_cp("parallel"),
    )(dxo, gate, e)


def _gnorm_fwd(name, y, z, w):
    T, DI = y.shape
    tr = _pick(T, 128)
    row = pl.BlockSpec((tr, DI), lambda i: (i, 0))
    vec = pl.BlockSpec((1, DI), lambda i: (0, 0))
    col = pl.BlockSpec((tr, 1), lambda i: (i, 0))

    def body(y_ref, z_ref, w_ref, o_ref, rs_ref):
        zv = z_ref[...]
        yz = y_ref[...] * (zv * _sigmoid(zv))
        rs = lax.rsqrt(jnp.mean(yz * yz, axis=-1, keepdims=True) + RMS_EPS)
        o_ref[...] = (yz * rs * w_ref[...]).astype(BF16)
        rs_ref[...] = rs

    return pl.pallas_call(
        body, name=name, grid=(T // tr,), in_specs=[row, row, vec], out_specs=[row, col],
        out_shape=[jax.ShapeDtypeStruct((T, DI), BF16), jax.ShapeDtypeStruct((T, 1), F32)],
        compiler_params=_cp("parallel"),
    )(y, z, w)


def _gnorm_bwd(name, dyn, y, z, w, rs):
    T, DI = y.shape
    tr = _pick(T, 128)
    row = pl.BlockSpec((tr, DI), lambda i: (i, 0))
    vec = pl.BlockSpec((1, DI), lambda i: (0, 0))
    col = pl.BlockSpec((tr, 1), lambda i: (i, 0))

    def body(d_ref, y_ref, z_ref, w_ref, rs_ref, dy_ref, dz_ref, dw_ref):
        first = pl.program_id(0) == 0
        zv = z_ref[...]
        yv = y_ref[...]
        sg = _sigmoid(zv)
        sz = zv * sg
        rsv = rs_ref[...]
        yzh = yv * sz * rsv
        dv = d_ref[...]
        gw = dv * w_ref[...]
        m = jnp.mean(gw * yzh, axis=-1, keepdims=True)
        dyz = rsv * (gw - yzh * m)
        dy_ref[...] = dyz * sz
        dz_ref[...] = (dyz * yv * (sg * (1.0 + zv * (1.0 - sg)))).astype(BF16)
        _accum(dw_ref, jnp.sum(dv * yzh, axis=0, keepdims=True), first)

    return pl.pallas_call(
        body, name=name, grid=(T // tr,), in_specs=[row, row, row, vec, col], out_specs=[row, row, vec],
        out_shape=[jax.ShapeDtypeStruct((T, DI), F32), jax.ShapeDtypeStruct((T, DI), BF16),
                   jax.ShapeDtypeStruct((1, DI), F32)],
        compiler_params=_cp("arbitrary"),
    )(dyn, y, z, w, rs)


def _sel4(j, vals):
    return jnp.where(j == 0, vals[0], jnp.where(j == 1, vals[1], jnp.where(j == 2, vals[2], vals[3])))


def _pool_cnt(i, j, tr, rows, offset):
    t = i * tr + offset + lax.broadcasted_iota(jnp.int32, (rows, 1), 0)
    win = _sel4(j, POOL_WINDOWS)
    return jnp.minimum(t + 1, win).astype(F32)


def _pool_fwd(name, x):
    T, D = x.shape
    gd = D // len(POOL_WINDOWS)
    tr = _pick(T, 512)
    hb = tr // POOL_HALO

    def body(x_ref, h_ref, o_ref):
        i, j = pl.program_id(0), pl.program_id(1)
        xv = x_ref[...]
        halo = jnp.where(i > 0, h_ref[...], 0.0)
        cat = jnp.concatenate([halo, xv], axis=0)
        s2 = cat + pltpu.roll(cat, 1, 0)
        s4 = s2 + pltpu.roll(s2, 2, 0)
        s8 = s4 + pltpu.roll(s4, 4, 0)
        s16 = s8 + pltpu.roll(s8, 8, 0)
        sel = _sel4(j, (s2, s4, s8, s16))[POOL_HALO:]
        o_ref[...] = (sel / _pool_cnt(i, j, tr, tr, 0) - xv).astype(BF16)

    return pl.pallas_call(
        body, name=name, grid=(T // tr, len(POOL_WINDOWS)),
        in_specs=[pl.BlockSpec((tr, gd), lambda i, j: (i, j)),
                  pl.BlockSpec((POOL_HALO, gd), lambda i, j: (jnp.maximum(i * hb - 1, 0), j))],
        out_specs=pl.BlockSpec((tr, gd), lambda i, j: (i, j)),
        out_shape=jax.ShapeDtypeStruct((T, D), BF16),
        compiler_params=_cp("parallel", "parallel"),
    )(x, x)


def _pool_bwd(name, dp, du):
    T, D = dp.shape
    gd = D // len(POOL_WINDOWS)
    tr = _pick(T, 512)
    hb = tr // POOL_HALO
    last_h = T // POOL_HALO - 1
    n = tr + POOL_HALO

    def body(dp_ref, h_ref, du_ref, o_ref):
        i, j = pl.program_id(0), pl.program_id(1)
        dpv = dp_ref[...]
        q = dpv / _pool_cnt(i, j, tr, tr, 0)
        qh = jnp.where(i < pl.num_programs(0) - 1, h_ref[...] / _pool_cnt(i, j, tr, POOL_HALO, tr), 0.0)
        cat = jnp.concatenate([q, qh], axis=0)
        f2 = cat + pltpu.roll(cat, n - 1, 0)
        f4 = f2 + pltpu.roll(f2, n - 2, 0)
        f8 = f4 + pltpu.roll(f4, n - 4, 0)
        f16 = f8 + pltpu.roll(f8, n - 8, 0)
        sel = _sel4(j, (f2, f4, f8, f16))[:tr]
        o_ref[...] = ALPHA * du_ref[...] + sel - dpv

    return pl.pallas_call(
        body, name=name, grid=(T // tr, len(POOL_WINDOWS)),
        in_specs=[pl.BlockSpec((tr, gd), lambda i, j: (i, j)),
                  pl.BlockSpec((POOL_HALO, gd), lambda i, j: (jnp.minimum((i + 1) * hb, last_h), j)),
                  pl.BlockSpec((tr, gd), lambda i, j: (i, j))],
        out_specs=pl.BlockSpec((tr, gd), lambda i, j: (i, j)),
        out_shape=jax.ShapeDtypeStruct((T, D), F32),
        compiler_params=_cp("parallel", "parallel"),
    )(dp, dp, du)


def _conv_taps(cat, wv, rows):
    shifted = [cat[CONV_HALO:] if s == 0 else pltpu.roll(cat, s, 0)[CONV_HALO:] for s in range(CONV_K)]
    acc = shifted[0] * wv[CONV_K - 1:CONV_K]
    for s in range(1, CONV_K):
        acc = acc + shifted[s] * wv[CONV_K - 1 - s:CONV_K - s]
    return acc, shifted


def _conv_fwd(name, xp, w, b):
    T, CD = xp.shape
    tr, tc = _pick(T, 512), _pick(CD, 512)
    hb = tr // CONV_HALO

    def body(x_ref, h_ref, w_ref, b_ref, o_ref):
        i = pl.program_id(0)
        halo = jnp.where(i > 0, h_ref[...], 0.0)
        cat = jnp.concatenate([halo, x_ref[...]], axis=0)
        acc, _ = _conv_taps(cat, w_ref[...], tr)
        acc = acc + b_ref[...]
        o_ref[...] = acc * _sigmoid(acc)

    return pl.pallas_call(
        body, name=name, grid=(T // tr, CD // tc),
        in_specs=[pl.BlockSpec((tr, tc), lambda i, j: (i, j)),
                  pl.BlockSpec((CONV_HALO, tc), lambda i, j: (jnp.maximum(i * hb - 1, 0), j)),
                  pl.BlockSpec((CONV_K, tc), lambda i, j: (0, j)),
                  pl.BlockSpec((1, tc), lambda i, j: (0, j))],
        out_specs=pl.BlockSpec((tr, tc), lambda i, j: (i, j)),
        out_shape=jax.ShapeDtypeStruct((T, CD), F32),
        compiler_params=_cp("parallel", "parallel"),
    )(xp, xp, w, b)


def _conv_bwd_a(name, dxs, dbm, dcm, xp, w, b):
    T, CD = xp.shape
    tr = _pick(T, 512)
    tc = _pick(dbm.shape[1], 512)
    hb = tr // CONV_HALO
    nx, nb = dxs.shape[1] // tc, dbm.shape[1] // tc

    def part_spec(lo, n):
        def imap(j, i):
            inside = jnp.logical_and(j >= lo, j < lo + n)
            return (jnp.where(inside, i, 0), jnp.clip(j - lo, 0, n - 1))
        return pl.BlockSpec((tr, tc), imap)

    def body(dx_ref, db_ref, dc_ref, x_ref, h_ref, w_ref, b_ref, o_ref, dw_ref, dbias_ref):
        j, i = pl.program_id(0), pl.program_id(1)
        first = i == 0
        d = jnp.where(j < nx, dx_ref[...], jnp.where(j < nx + nb, db_ref[...], dc_ref[...]))
        halo = jnp.where(i > 0, h_ref[...], 0.0)
        cat = jnp.concatenate([halo, x_ref[...]], axis=0)
        acc, shifted = _conv_taps(cat, w_ref[...], tr)
        acc = acc + b_ref[...]
        sg = _sigmoid(acc)
        dconv = d * (sg * (1.0 + acc * (1.0 - sg)))
        o_ref[...] = dconv
        _accum(dbias_ref, jnp.sum(dconv, axis=0, keepdims=True), first)
        tap = lax.broadcasted_iota(jnp.int32, (CONV_K, tc), 0)
        dwv = jnp.zeros((CONV_K, tc), F32)
        for s in range(CONV_K):
            dwv = jnp.where(tap == CONV_K - 1 - s, jnp.sum(dconv * shifted[s], axis=0, keepdims=True), dwv)
        _accum(dw_ref, dwv, first)

    return pl.pallas_call(
        body, name=name, grid=(CD // tc, T // tr),
        in_specs=[part_spec(0, nx), part_spec(nx, nb), part_spec(nx + nb, nb),
                  pl.BlockSpec((tr, tc), lambda j, i: (i, j)),
                  pl.BlockSpec((CONV_HALO, tc), lambda j, i: (jnp.maximum(i * hb - 1, 0), j)),
                  pl.BlockSpec((CONV_K, tc), lambda j, i: (0, j)),
                  pl.BlockSpec((1, tc), lambda j, i: (0, j))],
        out_specs=[pl.BlockSpec((tr, tc), lambda j, i: (i, j)),
                   pl.BlockSpec((CONV_K, tc), lambda j, i: (0, j)),
                   pl.BlockSpec((1, tc), lambda j, i: (0, j))],
        out_shape=[jax.ShapeDtypeStruct((T, CD), F32), jax.ShapeDtypeStruct((CONV_K, CD), F32),
                   jax.ShapeDtypeStruct((1, CD), F32)],
        compiler_params=_cp("parallel", "arbitrary"),
    )(dxs, dbm, dcm, xp, xp, w, b)


def _conv_bwd_b(name, dconv, w):
    T, CD = dconv.shape
    tr, tc = _pick(T, 512), _pick(CD, 512)
    hb = tr // CONV_HALO
    last_h = T // CONV_HALO - 1
    n = tr + CONV_HALO

    def body(d_ref, h_ref, w_ref, o_ref):
        i = pl.program_id(0)
        halo = jnp.where(i < pl.num_programs(0) - 1, h_ref[...], 0.0)
        cat = jnp.concatenate([d_ref[...], halo], axis=0)
        wv = w_ref[...]
        acc = cat[:tr] * wv[CONV_K - 1:CONV_K]
        for s in range(1, CONV_K):
            acc = acc + pltpu.roll(cat, n - s, 0)[:tr] * wv[CONV_K - 1 - s:CONV_K - s]
        o_ref[...] = acc.astype(BF16)

    return pl.pallas_call(
        body, name=name, grid=(T // tr, CD // tc),
        in_specs=[pl.BlockSpec((tr, tc), lambda i, j: (i, j)),
                  pl.BlockSpec((CONV_HALO, tc), lambda i, j: (jnp.minimum((i + 1) * hb, last_h), j)),
                  pl.BlockSpec((CONV_K, tc), lambda i, j: (0, j))],
        out_specs=pl.BlockSpec((tr, tc), lambda i, j: (i, j)),
        out_shape=jax.ShapeDtypeStruct((T, CD), BF16),
        compiler_params=_cp("parallel", "parallel"),
    )(dconv, dconv, w)


def _dt_fwd(name, dtp, bias):
    T, H = dtp.shape

    def body(x_ref, b_ref, o_ref):
        v = x_ref[...] + b_ref[...]
        u = jnp.exp(-jnp.abs(v))
        w1 = 1.0 + u
        lp = jnp.where(w1 == 1.0, u, jnp.log(w1) * (u / jnp.where(w1 == 1.0, 1.0, w1 - 1.0)))
        o_ref[...] = jnp.maximum(v, 0.0) + lp

    return pl.pallas_call(body, name=name, out_shape=jax.ShapeDtypeStruct((T, H), F32))(dtp, bias)


def _dt_bwd(name, ddt, dtp, bias):
    T, H = dtp.shape

    def body(d_ref, x_ref, b_ref, o_ref, ob_ref, db_ref):
        g = d_ref[...] * _sigmoid(x_ref[...] + b_ref[...])
        o_ref[...] = g
        ob_ref[...] = g.astype(BF16)
        db_ref[...] = jnp.sum(g, axis=0, keepdims=True)

    return pl.pallas_call(
        body, name=name,
        out_shape=[jax.ShapeDtypeStruct((T, H), F32), jax.ShapeDtypeStruct((T, H), BF16),
                   jax.ShapeDtypeStruct((1, H), F32)],
    )(ddt, dtp, bias)


def _split(v):
    hi = v.astype(BF16)
    return hi, (v - hi.astype(F32)).astype(BF16)


def _dot01(form, a, b, mask):
    if mask == "b":
        hi, lo = _split(a)
        mb = b.astype(BF16)
        return (lax.dot_general(hi, mb, _DIMS[form], preferred_element_type=F32)
                + lax.dot_general(lo, mb, _DIMS[form], preferred_element_type=F32))
    hi, lo = _split(b)
    ma = a.astype(BF16)
    return (lax.dot_general(ma, hi, _DIMS[form], preferred_element_type=F32)
            + lax.dot_general(ma, lo, _DIMS[form], preferred_element_type=F32))


def _dotb(form, a, b):
    return lax.dot_general(a.astype(BF16), b.astype(BF16), _DIMS[form], preferred_element_type=F32)


def _ssd_common(dtc, dtr, alr, alc, gw):
    li = lax.broadcasted_iota(jnp.int32, (CHUNK, CHUNK), 0)
    si = lax.broadcasted_iota(jnp.int32, (CHUNK, CHUNK), 1)
    tri = (li >= si).astype(F32)
    trit = (li <= si).astype(F32)
    a_row = -jnp.exp(alr)
    a_col = -jnp.exp(alc)
    acs_c = _dot01("nn", tri, dtc * a_row, "a")
    acs_r = _dot01("nn", dtr * a_col, trit, "b")
    eh = lax.broadcasted_iota(jnp.int32, (HEAD_PAD, gw), 0)
    ec = lax.broadcasted_iota(jnp.int32, (HEAD_PAD, gw), 1) // HEAD_DIM
    expand = (eh == ec).astype(F32)
    th = lax.broadcasted_iota(jnp.int32, (gw, HEAD_PAD), 1)
    tc = lax.broadcasted_iota(jnp.int32, (gw, HEAD_PAD), 0) // HEAD_DIM
    reduce_ = (th == tc).astype(F32)
    acs_last = acs_c[CHUNK - 1:CHUNK, :]
    both = _dot01("nn", jnp.concatenate([dtc, acs_c], axis=0), expand, "b")
    acs_e = both[CHUNK:]
    return dict(li=li, si=si, tri=tri, trit=trit, a_row=a_row, acs_c=acs_c, acs_r=acs_r,
                reduce=reduce_, acs_last=acs_last, dt_e=both[:CHUNK],
                eacs_e=jnp.exp(acs_e), dec_e=jnp.exp(acs_e[CHUNK - 1:CHUNK, :] - acs_e),
                dec_h=jnp.exp(acs_last - acs_c))


def _ssd_specs(T, DI, gw, nc, rev):
    nsb = DI // D_STATE
    cidx = (lambda c: nc - 1 - c) if rev else (lambda c: c)
    return dict(
        xs=pl.BlockSpec((CHUNK, gw), lambda g, c: (cidx(c), g)),
        bm=pl.BlockSpec((CHUNK, D_STATE), lambda g, c: (cidx(c), nsb + g)),
        cm=pl.BlockSpec((CHUNK, D_STATE), lambda g, c: (cidx(c), nsb + N_GROUPS + g)),
        dtc=pl.BlockSpec((None, CHUNK, HEAD_PAD), lambda g, c: (g, cidx(c), 0)),
        dtr=pl.BlockSpec((None, ROW_PAD, CHUNK), lambda g, c: (g, 0, cidx(c))),
        alr=pl.BlockSpec((None, 1, HEAD_PAD), lambda g, c: (g, 0, 0)),
        alc=pl.BlockSpec((None, ROW_PAD, 1), lambda g, c: (g, 0, 0)),
        de=pl.BlockSpec((None, 1, gw), lambda g, c: (g, 0, 0)),
        hp=pl.BlockSpec((None, None, gw, D_STATE), lambda g, c: (cidx(c), g, 0, 0)),
        bc=pl.BlockSpec((CHUNK, D_STATE), lambda g, c: (cidx(c), g)),
        acc=pl.BlockSpec((None, 1, HEAD_PAD), lambda g, c: (g, 0, 0)),
    )


def _ssd_fwd(name, xbc, dtc, dtr, alr, alc, d_e, DI, deps=()):
    T = xbc.shape[0]
    nc = T // CHUNK
    gw = DI // N_GROUPS
    sp = _ssd_specs(T, DI, gw, nc, False)

    def body(xs_ref, b_ref, c_ref, dtc_ref, dtr_ref, alr_ref, alc_ref, de_ref, *rest):
        y_ref, hp_ref, h_scr = rest[len(deps):]

        @pl.when(pl.program_id(1) == 0)
        def _():
            h_scr[...] = jnp.zeros_like(h_scr)

        hpv = h_scr[...]
        hp_ref[...] = hpv
        xs = xs_ref[...]
        bb = b_ref[...].astype(BF16)
        cb_ = c_ref[...].astype(BF16)
        cm = _ssd_common(dtc_ref[...], dtr_ref[...], alr_ref[...], alc_ref[...], gw)
        x = xs * cm["dt_e"]
        xb = x.astype(BF16)
        cbm = _dotb("nt", cb_, bb)
        causal = cm["li"] >= cm["si"]
        second = lax.broadcasted_iota(jnp.int32, (1, HEAD_PAIR), 1) >= HEAD_DIM
        pieces = []
        for pr in range(gw // HEAD_PAIR):
            xp = xb[:, pr * HEAD_PAIR:(pr + 1) * HEAD_PAIR]
            for h2 in range(2):
                j = 2 * pr + h2
                seg = cm["acs_c"][:, j:j + 1] - cm["acs_r"][j:j + 1, :]
                lmat = jnp.exp(jnp.where(causal, jnp.minimum(seg, 0.0), -1e30))
                yj = _dotb("nn", cbm * lmat, xp)
                yp = yj if h2 == 0 else jnp.where(second, yj, yp)
            pieces.append(yp)
        ydiag = pieces[0] if len(pieces) == 1 else jnp.concatenate(pieces, axis=1)
        states = _dotb("tn", x * cm["dec_e"], bb)
        yoff = _dotb("nt", cb_, hpv) * cm["eacs_e"]
        y_ref[...] = ydiag + yoff + xs * de_ref[...]
        cd_rows = jnp.sum(cm["reduce"] * jnp.exp(cm["acs_last"]), axis=1, keepdims=True)
        h_scr[...] = cd_rows * hpv + states

    return pl.pallas_call(
        body, name=name, grid=(N_GROUPS, nc),
        in_specs=[sp["xs"], sp["bm"], sp["cm"], sp["dtc"], sp["dtr"], sp["alr"], sp["alc"], sp["de"]]
        + [pl.BlockSpec((8, 128), lambda g, c: (0, 0)) for _ in deps],
        out_specs=[sp["xs"], sp["hp"]],
        out_shape=[jax.ShapeDtypeStruct((T, DI), F32), jax.ShapeDtypeStruct((nc, N_GROUPS, gw, D_STATE), F32)],
        scratch_shapes=[pltpu.VMEM((gw, D_STATE), F32)],
        compiler_params=_cp("parallel", "arbitrary"),
    )(xbc, xbc, xbc, dtc, dtr, alr, alc, d_e, *deps)


def _ssd_bwd(name, dy, xbc, dtc, dtr, alr, alc, d_e, hprev, DI):
    T = xbc.shape[0]
    nc = T // CHUNK
    gw = DI // N_GROUPS
    sp = _ssd_specs(T, DI, gw, nc, True)

    def body(dy_ref, xs_ref, b_ref, c_ref, dtc_ref, dtr_ref, alr_ref, alc_ref, de_ref, hp_ref,
             dxs_ref, db_ref, dc_ref, ddt_ref, dal_ref, dd_ref, dh_scr):
        first = pl.program_id(1) == 0

        @pl.when(first)
        def _():
            dh_scr[...] = jnp.zeros_like(dh_scr)

        xs = xs_ref[...]
        dyv = dy_ref[...]
        bb = b_ref[...].astype(BF16)
        cb_ = c_ref[...].astype(BF16)
        dtc_v = dtc_ref[...]
        cm = _ssd_common(dtc_v, dtr_ref[...], alr_ref[...], alc_ref[...], gw)
        hpv = hp_ref[...]
        hpb = hpv.astype(BF16)
        dhn = dh_scr[...]
        dhnb = dhn.astype(BF16)
        x = xs * cm["dt_e"]
        xb = x.astype(BF16)
        cbm = _dotb("nt", cb_, bb)
        causal = cm["li"] >= cm["si"]
        second = lax.broadcasted_iota(jnp.int32, (1, HEAD_PAIR), 1) >= HEAD_DIM
        lane_pad = lax.broadcasted_iota(jnp.int32, (1, HEAD_PAD), 1)
        sub_pad = lax.broadcasted_iota(jnp.int32, (ROW_PAD, 1), 0)

        dxs = dyv * de_ref[...]

        dcb = jnp.zeros((CHUNK, CHUNK), F32)
        dacs_c = jnp.zeros((CHUNK, HEAD_PAD), F32)
        dacs_r = jnp.zeros((ROW_PAD, CHUNK), F32)
        pieces = []
        for pr in range(gw // HEAD_PAIR):
            xp = xb[:, pr * HEAD_PAIR:(pr + 1) * HEAD_PAIR]
            dyp = dyv[:, pr * HEAD_PAIR:(pr + 1) * HEAD_PAIR]
            for h2 in range(2):
                j = 2 * pr + h2
                seg = cm["acs_c"][:, j:j + 1] - cm["acs_r"][j:j + 1, :]
                lmat = jnp.exp(jnp.where(causal, jnp.minimum(seg, 0.0), -1e30))
                mmat = cbm * lmat
                dym = jnp.where(second if h2 == 1 else jnp.logical_not(second), dyp, 0.0).astype(BF16)
                dm = _dotb("nt", dym, xp)
                dxj = _dotb("tn", mmat, dym)
                dxp = dxj if h2 == 0 else dxp + dxj
                dcb = dcb + dm * lmat
                dseg = dm * mmat
                dacs_c = dacs_c + jnp.where(lane_pad == j, jnp.sum(dseg, axis=1, keepdims=True), 0.0)
                dacs_r = dacs_r - jnp.where(sub_pad == j, jnp.sum(dseg, axis=0, keepdims=True), 0.0)
            pieces.append(dxp)
        dx = pieces[0] if len(pieces) == 1 else jnp.concatenate(pieces, axis=1)
        dc = _dotb("nn", dcb, bb)
        db = _dotb("tn", dcb, cb_)

        gm = _dotb("nt", cb_, hpb)
        dgm = dyv * cm["eacs_e"]
        dc = dc + _dotb("nn", dgm, hpb)
        dhp = _dotb("tn", dgm, cb_)

        cd_row = jnp.exp(cm["acs_last"])
        cd_rows = jnp.sum(cm["reduce"] * cd_row, axis=1, keepdims=True)
        dhp = dhp + cd_rows * dhn
        rsum = jnp.sum(dhn * hpv, axis=1, keepdims=True)
        dacs_last = jnp.sum(cm["reduce"] * rsum, axis=0, keepdims=True) * cd_row
        dec_e = cm["dec_e"]
        xdec = x * dec_e
        dxdec = _dotb("nt", bb, dhnb)
        db = db + _dotb("nn", xdec, dhnb)
        dx = dx + dxdec * dec_e
        sums = _dot01("nn", jnp.concatenate([dyv * xs, dgm * gm, dxdec * x, dx * xs], axis=0), cm["reduce"], "b")
        _accum(dd_ref, jnp.sum(sums[:CHUNK], axis=0, keepdims=True), first)
        tdec = sums[2 * CHUNK:3 * CHUNK] * cm["dec_h"]
        dacs_c = dacs_c + sums[CHUNK:2 * CHUNK] - tdec
        dacs_last = dacs_last + jnp.sum(tdec, axis=0, keepdims=True)
        row_id = lax.broadcasted_iota(jnp.int32, (CHUNK, 1), 0)
        dacs_c = dacs_c + jnp.where(row_id == CHUNK - 1, dacs_last, 0.0)

        dxs_ref[...] = dxs + dx * cm["dt_e"]
        ddt = sums[3 * CHUNK:]
        dda = _dot01("nn", cm["trit"], dacs_c, "a")
        dda_r = _dot01("nn", dacs_r, cm["tri"], "b")
        dda_rp = jnp.concatenate([dda_r, jnp.zeros((HEAD_PAD - ROW_PAD, CHUNK), F32)], axis=0)
        eye = (cm["li"] == cm["si"]).astype(F32)
        dda = dda + _dot01("nt", eye, dda_rp, "a")
        ddt_ref[...] = ddt + dda * cm["a_row"]
        _accum(dal_ref, jnp.sum(dda * dtc_v, axis=0, keepdims=True) * cm["a_row"], first)
        db_ref[...] = db
        dc_ref[...] = dc
        dh_scr[...] = dhp

    gs = D_STATE * N_GROUPS
    return pl.pallas_call(
        body, name=name, grid=(N_GROUPS, nc),
        in_specs=[sp["xs"], sp["xs"], sp["bm"], sp["cm"], sp["dtc"], sp["dtr"], sp["alr"], sp["alc"],
                  sp["de"], sp["hp"]],
        out_specs=[sp["xs"], sp["bc"], sp["bc"], sp["dtc"], sp["acc"], sp["acc"]],
        out_shape=[jax.ShapeDtypeStruct((T, DI), F32), jax.ShapeDtypeStruct((T, gs), F32),
                   jax.ShapeDtypeStruct((T, gs), F32), jax.ShapeDtypeStruct((N_GROUPS, T, HEAD_PAD), F32),
                   jax.ShapeDtypeStruct((N_GROUPS, 1, HEAD_PAD), F32),
                   jax.ShapeDtypeStruct((N_GROUPS, 1, HEAD_PAD), F32)],
        scratch_shapes=[pltpu.VMEM((gw, D_STATE), F32)],
        compiler_params=_cp("parallel", "arbitrary"),
    )(dy, xbc, xbc, xbc, dtc, dtr, alr, alc, d_e, hprev)


def _adam_math(w, g, m, v):
    m = ADAM_B1 * m + (1.0 - ADAM_B1) * g
    v = ADAM_B2 * v + (1.0 - ADAM_B2) * (g * g)
    m_hat = m / (1.0 - ADAM_B1 ** ADAM_STEP)
    v_hat = v / (1.0 - ADAM_B2 ** ADAM_STEP)
    delta = -ADAM_LR * (m_hat / (jnp.sqrt(v_hat) + ADAM_EPS) + ADAM_WD * w)
    return delta, m, v


def _adamw(name, w, m, v, g, layer=0, prev=None):
    L, R, C = w.shape
    tr = _rows(R, C * 4, ADAM_BLOCK_BYTES)
    tc = C
    while tr * tc * 4 > ADAM_BLOCK_BYTES and tc % 256 == 0:
        tc //= 2
    blk = pl.BlockSpec((None, tr, tc), lambda i, j: (layer, i, j))
    prev = [] if prev is None else list(prev)

    def body(w_ref, m_ref, v_ref, g_ref, *rest):
        go_ref, d_ref, mo_ref, vo_ref = rest[len(prev):]
        gv = g_ref[...]
        delta, mn, vn = _adam_math(w_ref[...], gv, m_ref[...], v_ref[...])
        go_ref[...] = gv
        d_ref[...] = delta
        mo_ref[...] = mn
        vo_ref[...] = vn

    return pl.pallas_call(
        body, name=name, grid=(R // tr, C // tc),
        in_specs=[blk] * 3 + [pl.BlockSpec((tr, tc), lambda i, j: (i, j))] + [_ANY] * len(prev),
        out_specs=[blk] * 4, out_shape=[jax.ShapeDtypeStruct((L, R, C), F32)] * 4,
        input_output_aliases={4 + k: k for k in range(len(prev))},
        compiler_params=_cp("parallel", "parallel"),
    )(w, m, v, g, *prev)


SC_TILES = 32
SC_LANES = 16
SC_CHUNK = 8192


def _adamw_sc(name, w, m, v, g):
    n = w.shape[0]
    per_tile = n // SC_TILES
    n_chunks = per_tile // SC_CHUNK

    def body(w_hbm, m_hbm, v_hbm, g_hbm, d_hbm, mo_hbm, vo_hbm, wb, mb, vb, gb):
        tile = lax.axis_index("sc_subcore") * 2 + lax.axis_index("sc_core")

        @pl.loop(0, n_chunks)
        def _(ci):
            piece = pl.ds(tile * per_tile + ci * SC_CHUNK, SC_CHUNK)
            pltpu.sync_copy(w_hbm.at[piece], wb)
            pltpu.sync_copy(m_hbm.at[piece], mb)
            pltpu.sync_copy(v_hbm.at[piece], vb)
            pltpu.sync_copy(g_hbm.at[piece], gb)

            @pl.loop(0, SC_CHUNK, step=SC_LANES)
            def _(i):
                s = pl.ds(i, SC_LANES)
                delta, mn, vn = _adam_math(wb[s], gb[s], mb[s], vb[s])
                wb[s] = delta
                mb[s] = mn
                vb[s] = vn

            pltpu.sync_copy(wb, d_hbm.at[piece])
            pltpu.sync_copy(mb, mo_hbm.at[piece])
            pltpu.sync_copy(vb, vo_hbm.at[piece])

    return pl.kernel(
        body, name=name, out_type=[jax.ShapeDtypeStruct((n,), F32)] * 3,
        mesh=plsc.VectorSubcoreMesh(core_axis_name="sc_core", subcore_axis_name="sc_subcore"),
        scratch_types=[pltpu.VMEM((SC_CHUNK,), F32)] * 4,
    )(w, m, v, g)


def _sum_peers(name, gathered):
    n, R, C = gathered.shape

    def body(g_ref, o_ref):
        acc = g_ref[0]
        for d in range(1, n):
            acc = acc + g_ref[d]
        o_ref[...] = acc

    return pl.pallas_call(body, name=name, out_shape=jax.ShapeDtypeStruct((R, C), F32))(gathered)


def _place():
    x, y, c = lax.axis_index("x"), lax.axis_index("y"), lax.axis_index("c")
    chips = [(1 - x, y), (x, 1 - y), (1 - x, 1 - y)]
    return x, y, c, chips


def _allgather_small(name, v, after=()):
    R, C = v.shape
    after = list(after)

    def body(v_ref, *rest):
        o_ref, send_sems, recv_sems = rest[len(after):]
        x, y, c, _ = _place()
        me = 4 * x + 2 * y + c
        o_ref[me] = v_ref[...]
        copies = []
        for k in range(1, N_DEV):
            px, py, pc = x ^ (k >> 2), y ^ ((k >> 1) & 1), c ^ (k & 1)
            copies.append(pltpu.make_async_remote_copy(
                src_ref=v_ref, dst_ref=o_ref.at[me], send_sem=send_sems.at[k - 1], recv_sem=recv_sems.at[k - 1],
                device_id=(px, py, pc), device_id_type=MESH))
        for cp in copies:
            cp.start()
        for cp in copies:
            cp.wait()

    return pl.pallas_call(
        body, name=name, out_shape=jax.ShapeDtypeStruct((N_DEV, R, C), F32),
        in_specs=[pl.BlockSpec(memory_space=pltpu.VMEM)] + [_ANY] * len(after),
        out_specs=pl.BlockSpec(memory_space=pltpu.VMEM),
        scratch_shapes=[pltpu.SemaphoreType.DMA((N_DEV - 1,)), pltpu.SemaphoreType.DMA((N_DEV - 1,))],
    )(v, *after)


_HBM = pl.BlockSpec(memory_space=pltpu.HBM)
_SEM = pl.BlockSpec(memory_space=pltpu.SEMAPHORE)
_VMEM = pl.BlockSpec(memory_space=pltpu.VMEM)
_EFFECT = pltpu.SideEffectType.DATAFLOW_SIDE_EFFECTING


def _in_hbm(v):
    return pltpu.with_memory_space_constraint(v, pltpu.HBM)


def _remote(src, dst, send_sem, recv_sem, device):
    return pltpu.make_async_remote_copy(src_ref=src, dst_ref=dst, send_sem=send_sem, recv_sem=recv_sem,
                                        device_id=device, device_id_type=MESH)


def _gather_copies(src, land, ici_s, ici_r, own_s, own_r, arrivals=True):
    x, y, c, chips = _place()
    a = 2 * x + y
    sends, lands = [], []
    for i in range(len(src)):
        own = _remote(src[i], land[i].at[a], own_s.at[i], own_r.at[i], (x, y, 1 - c))
        sends.append(own)
        if arrivals:
            lands.append(own)
        for k, (px, py) in enumerate(chips):
            s, r = ici_s.at[3 * i + k], ici_r.at[3 * i + k]
            sends.append(_remote(src[i].at[c], land[i].at[a, c], s, r, (px, py, c)))
            if arrivals:
                lands.append(_remote(src[i].at[c], land[i].at[2 * px + py, c], s, r, (px, py, c)))
    return sends, lands


def _gather_start(name, shards, after=None):
    n = len(shards)
    lands = [lax.empty((N_CHIPS,) + s.shape, s.dtype) for s in shards]
    n_in = 2 * n + (0 if after is None else 1)

    def body(*refs):
        src, land = refs[:n], refs[n:2 * n]
        sems = refs[n_in:n_in + 4]
        token = refs[-1]
        sends, _ = _gather_copies(src, land, *sems, arrivals=False)
        for cp in sends:
            cp.start()
        token[...] = jnp.zeros_like(token)

    outs = pl.pallas_call(
        body, name=name,
        out_shape=(pltpu.SemaphoreType.DMA((3 * n,)), pltpu.SemaphoreType.DMA((3 * n,)),
                   pltpu.SemaphoreType.DMA((n,)), pltpu.SemaphoreType.DMA((n,)),
                   *[pltpu.HBM(s.shape, s.dtype) for s in shards], *[pltpu.HBM(l.shape, l.dtype) for l in lands],
                   jax.ShapeDtypeStruct((8, 128), F32)),
        in_specs=[_HBM] * (2 * n) + ([] if after is None else [_ANY]),
        out_specs=(_SEM,) * 4 + (_HBM,) * (2 * n) + (_VMEM,),
        input_output_aliases={i: 4 + i for i in range(2 * n)},
        compiler_params=pltpu.CompilerParams(has_side_effects=_EFFECT),
    )(*[_in_hbm(s) for s in shards], *[_in_hbm(l) for l in lands], *([] if after is None else [after]))
    return dict(sems=outs[:4], src=outs[4:4 + n], land=outs[4 + n:4 + 2 * n], token=outs[-1])


def _gather_wait(name, started, after):
    n = len(started["src"])
    after = list(after)

    def body(*refs):
        src, land = refs[:n], refs[n:2 * n]
        sems = refs[2 * n:2 * n + 4]
        sends, lands = _gather_copies(src, land, *sems)
        for cp in sends:
            cp.wait_send()
        for cp in lands:
            cp.wait_recv()

    outs = pl.pallas_call(
        body, name=name,
        out_shape=[pltpu.HBM(v.shape, v.dtype) for v in list(started["src"]) + list(started["land"])],
        in_specs=[_HBM] * (2 * n) + [_SEM] * 4 + [_ANY] * len(after), out_specs=[_HBM] * (2 * n),
        input_output_aliases={i: i for i in range(2 * n)},
        compiler_params=pltpu.CompilerParams(has_side_effects=_EFFECT),
    )(*started["src"], *started["land"], *started["sems"], *after)
    return outs[n:]


def _gather_forward(name, lands):
    n = len(lands)

    def body(*refs):
        buf = refs[n:2 * n]
        send_sems, recv_sems = refs[2 * n:]
        x, y, c, chips = _place()
        sends, lands_ = [], []
        for i in range(n):
            for k, (px, py) in enumerate(chips):
                b = 2 * px + py
                sends.append(_remote(buf[i].at[b, c], buf[i].at[b, c], send_sems.at[i, k], recv_sems.at[i, k],
                                     (x, y, 1 - c)))
                lands_.append(_remote(buf[i].at[b, 1 - c], buf[i].at[b, 1 - c], send_sems.at[i, k],
                                      recv_sems.at[i, k], (x, y, 1 - c)))
        for cp in sends:
            cp.start()
        for cp in sends:
            cp.wait_send()
        for cp in lands_:
            cp.wait_recv()

    return pl.pallas_call(
        body, name=name, in_specs=[_ANY] * n, out_specs=[_ANY] * n,
        out_shape=[jax.ShapeDtypeStruct(l.shape, l.dtype) for l in lands],
        input_output_aliases={i: i for i in range(n)},
        scratch_shapes=[pltpu.SemaphoreType.DMA((n, 3)), pltpu.SemaphoreType.DMA((n, 3))],
    )(*lands)


def _split_start(name, bufs, copies, n_sems, after=()):
    nb = len(bufs)
    after = list(after)

    def body(*refs):
        send_sems, recv_sems = refs[nb + len(after):nb + len(after) + 2]
        token = refs[-1]
        sends, _ = copies(refs[:nb], send_sems, recv_sems, False)
        for cp in sends:
            cp.start()
        token[...] = jnp.zeros_like(token)

    outs = pl.pallas_call(
        body, name=name,
        out_shape=(pltpu.SemaphoreType.DMA((n_sems,)), pltpu.SemaphoreType.DMA((n_sems,)),
                   *[pltpu.HBM(b.shape, b.dtype) for b in bufs], jax.ShapeDtypeStruct((8, 128), F32)),
        in_specs=[_HBM] * nb + [_ANY] * len(after), out_specs=(_SEM,) * 2 + (_HBM,) * nb + (_VMEM,),
        input_output_aliases={i: 2 + i for i in range(nb)},
        compiler_params=pltpu.CompilerParams(has_side_effects=_EFFECT),
    )(*[_in_hbm(b) for b in bufs], *after)
    return dict(sems=outs[:2], bufs=outs[2:2 + nb], token=outs[-1])


def _split_wait(name, started, copies, after):
    nb = len(started["bufs"])
    after = list(after)

    def body(*refs):
        sends, arrivals = copies(refs[:nb], refs[nb], refs[nb + 1], True)
        for cp in sends:
            cp.wait_send()
        for cp in arrivals:
            cp.wait_recv()

    return pl.pallas_call(
        body, name=name, out_shape=[pltpu.HBM(b.shape, b.dtype) for b in started["bufs"]],
        in_specs=[_HBM] * nb + [_SEM] * 2 + [_ANY] * len(after), out_specs=[_HBM] * nb,
        input_output_aliases={i: i for i in range(nb)},
        compiler_params=pltpu.CompilerParams(has_side_effects=_EFFECT),
    )(*started["bufs"], *started["sems"], *after)


def _forward_copies(bufs, send_sems, recv_sems, arrivals):
    x, y, c, chips = _place()
    sends, lands = [], []
    for i, buf in enumerate(bufs):
        for k, (px, py) in enumerate(chips):
            b, s, r = 2 * px + py, send_sems.at[3 * i + k], recv_sems.at[3 * i + k]
            sends.append(_remote(buf.at[b, c], buf.at[b, c], s, r, (x, y, 1 - c)))
            if arrivals:
                lands.append(_remote(buf.at[b, 1 - c], buf.at[b, 1 - c], s, r, (x, y, 1 - c)))
    return sends, lands


def _pair_copies(bufs, send_sems, recv_sems, arrivals):
    x, y, c, _ = _place()
    n = len(bufs) // 2
    copies = [_remote(bufs[i].at[:, 1 - c], bufs[n + i], send_sems.at[i], recv_sems.at[i], (x, y, 1 - c))
              for i in range(n)]
    return copies, copies


def _chip_copies(src, land, send_sems, recv_sems):
    x, y, c, chips = _place()
    return [_remote(src[i].at[2 * px + py], land[i].at[k], send_sems.at[3 * i + k], recv_sems.at[3 * i + k],
                    (px, py, c))
            for i in range(len(src)) for k, (px, py) in enumerate(chips)]


def _chip_start(name, parts):
    n = len(parts)
    lands = [lax.empty((3,) + p.shape[1:], p.dtype) for p in parts]

    def body(*refs):
        src, land = refs[:n], refs[n:2 * n]
        send_sems, recv_sems = refs[2 * n:2 * n + 2]
        token = refs[-1]
        for cp in _chip_copies(src, land, send_sems, recv_sems):
            cp.start()
        token[...] = jnp.zeros_like(token)

    outs = pl.pallas_call(
        body, name=name,
        out_shape=(pltpu.SemaphoreType.DMA((3 * n,)), pltpu.SemaphoreType.DMA((3 * n,)),
                   *[pltpu.HBM(p.shape, p.dtype) for p in parts], *[pltpu.HBM(l.shape, l.dtype) for l in lands],
                   jax.ShapeDtypeStruct((8, 128), F32)),
        in_specs=[_HBM] * (2 * n), out_specs=(_SEM,) * 2 + (_HBM,) * (2 * n) + (_VMEM,),
        input_output_aliases={i: 2 + i for i in range(2 * n)},
        compiler_params=pltpu.CompilerParams(has_side_effects=_EFFECT),
    )(*[_in_hbm(p) for p in parts], *[_in_hbm(l) for l in lands])
    return dict(sems=outs[:2], src=outs[2:2 + n], land=outs[2 + n:2 + 2 * n], token=outs[-1])


def _chip_wait(name, started, after):
    n = len(started["src"])
    after = list(after)

    def body(*refs):
        src, land = refs[:n], refs[n:2 * n]
        send_sems, recv_sems = refs[2 * n:2 * n + 2]
        copies = _chip_copies(src, land, send_sems, recv_sems)
        for cp in copies:
            cp.wait_send()
        for cp in copies:
            cp.wait_recv()

    outs = pl.pallas_call(
        body, name=name,
        out_shape=[pltpu.HBM(v.shape, v.dtype) for v in list(started["src"]) + list(started["land"])],
        in_specs=[_HBM] * (2 * n) + [_SEM] * 2 + [_ANY] * len(after), out_specs=[_HBM] * (2 * n),
        input_output_aliases={i: i for i in range(2 * n)},
        compiler_params=pltpu.CompilerParams(has_side_effects=_EFFECT),
    )(*started["src"], *started["land"], *started["sems"], *after)
    return outs[:n], outs[n:]


def _half_exchange(name, shards):
    n = len(shards)

    def body(*refs):
        buf = refs[n:2 * n]
        send_sems, recv_sems = refs[2 * n:]
        x, y, c, _ = _place()
        copies = [pltpu.make_async_remote_copy(
            src_ref=buf[i].at[c], dst_ref=buf[i].at[c], send_sem=send_sems.at[i], recv_sem=recv_sems.at[i],
            device_id=(x, y, 1 - c), device_id_type=MESH) for i in range(n)]
        for cp in copies:
            cp.start()
        for cp in copies:
            cp.wait()

    return pl.pallas_call(
        body, name=name, in_specs=[_ANY] * n, out_specs=[_ANY] * n,
        out_shape=[jax.ShapeDtypeStruct(s.shape, s.dtype) for s in shards],
        input_output_aliases={i: i for i in range(n)},
        scratch_shapes=[pltpu.SemaphoreType.DMA((n,)), pltpu.SemaphoreType.DMA((n,))],
    )(*shards)


def _pair_add(name, pos, part, sib):
    Q, _, R2, C = part.shape
    tr = _pick(R2, 256)

    def body(pos_ref, p_ref, s_ref, o_ref):
        o_ref[...] = (p_ref[...].astype(F32) + s_ref[...].astype(F32)).astype(BF16)

    return pl.pallas_call(
        body, name=name,
        grid_spec=pltpu.PrefetchScalarGridSpec(
            num_scalar_prefetch=1, grid=(Q, R2 // tr),
            in_specs=[pl.BlockSpec((None, None, tr, C), lambda q, i, pos_ref: (q, pos_ref[1], i, 0)),
                      pl.BlockSpec((None, tr, C), lambda q, i, pos_ref: (q, i, 0))],
            out_specs=pl.BlockSpec((None, tr, C), lambda q, i, pos_ref: (q, i, 0))),
        out_shape=jax.ShapeDtypeStruct((Q, R2, C), BF16),
        compiler_params=_cp("parallel", "parallel"),
    )(pos, part, sib)


def _chip_sum(name, pos, own, got):
    _, R2, C = own.shape
    tr = _pick(R2, 256)

    def body(pos_ref, o_ref, g_ref, out_ref):
        acc = o_ref[...].astype(F32)
        for k in range(3):
            acc = acc + g_ref[k].astype(F32)
        out_ref[...] = acc

    return pl.pallas_call(
        body, name=name,
        grid_spec=pltpu.PrefetchScalarGridSpec(
            num_scalar_prefetch=1, grid=(R2 // tr,),
            in_specs=[pl.BlockSpec((None, tr, C), lambda i, pos_ref: (pos_ref[0], i, 0)),
                      pl.BlockSpec((3, tr, C), lambda i, pos_ref: (0, i, 0))],
            out_specs=pl.BlockSpec((None, tr, C), lambda i, pos_ref: (pos_ref[1], i, 0))),
        out_shape=jax.ShapeDtypeStruct((2, R2, C), F32),
        compiler_params=_cp("parallel"),
    )(pos, own, got)


def _reduce_pair_start(tag, parts):
    split = [p.reshape(p.shape[0], 2, p.shape[1] // 2, p.shape[2]) for p in parts]
    lands = [lax.empty((p.shape[0],) + p.shape[2:], p.dtype) for p in split]
    return _split_start(tag + "_pair_start", split + lands, _pair_copies, len(parts))


def _reduce_pair_end(tag, pos, started, after):
    n = len(started["bufs"]) // 2
    bufs = _split_wait(tag + "_pair_wait", started, _pair_copies, after)
    chip = [_pair_add(f"{tag}_add{i}", pos, bufs[i], bufs[n + i]) for i in range(n)]
    return _chip_start(tag + "_chip_start", chip)


def _half_copies(bufs, send_sems, recv_sems, arrivals):
    x, y, c, _ = _place()
    sends = [_remote(b.at[c], b.at[c], send_sems.at[i], recv_sems.at[i], (x, y, 1 - c)) for i, b in enumerate(bufs)]
    lands = [_remote(b.at[1 - c], b.at[1 - c], send_sems.at[i], recv_sems.at[i], (x, y, 1 - c))
             for i, b in enumerate(bufs)] if arrivals else []
    return sends, lands


def _reduce_sum_start(tag, pos, started, after):
    chip, got = _chip_wait(tag + "_chip_wait", started, after)
    halves = [_chip_sum(f"{tag}_sum{i}", pos, chip[i], got[i]) for i in range(len(chip))]
    return _split_start(tag + "_half_start", halves, _half_copies, len(halves))


def _reduce_sum_end(tag, started, after, shapes):
    full = _split_wait(tag + "_half_wait", started, _half_copies, after)
    return [f.reshape(shp) for f, shp in zip(full, shapes)]


def _reduce_end(tag, pos, started, after, shapes):
    chip, got = _chip_wait(tag + "_chip_wait", started, after)
    halves = [_chip_sum(f"{tag}_sum{i}", pos, chip[i], got[i]) for i in range(len(chip))]
    full = _half_exchange(tag + "_half", halves)
    return [f.reshape(shp) for f, shp in zip(full, shapes)]


def _pad_to(v, mult):
    n = v.shape[0]
    return jnp.pad(v, (0, (-n) % mult))


def kernel(x, p, pool_w, pool_scale, ssm_in_w, ssm_conv_w, ssm_conv_b, ssm_dt_bias, ssm_a_log, ssm_d, ssm_norm_w, ssm_out_w, mlp_w1, mlp_w2, ln_g, ln_b, ple_w, ple_gate_w, loss_target, m_pool_w, m_pool_scale, m_ssm_in_w, m_ssm_conv_w, m_ssm_conv_b, m_ssm_dt_bias, m_ssm_a_log, m_ssm_d, m_ssm_norm_w, m_ssm_out_w, m_mlp_w1, m_mlp_w2, m_ln_g, m_ln_b, m_ple_w, m_ple_gate_w, v_pool_w, v_pool_scale, v_ssm_in_w, v_ssm_conv_w, v_ssm_conv_b, v_ssm_dt_bias, v_ssm_a_log, v_ssm_d, v_ssm_norm_w, v_ssm_out_w, v_mlp_w1, v_mlp_w2, v_ln_g, v_ln_b, v_ple_w, v_ple_gate_w):
    T, D = x.shape[1], x.shape[2]
    NG = len(POOL_WINDOWS)
    GD = D // NG
    DI = ssm_out_w.shape[1] * N_CHIPS
    H = ssm_dt_bias.shape[1]
    HPG = H // N_GROUPS
    GW = DI // N_GROUPS
    GS = N_GROUPS * D_STATE
    CD = DI + 2 * GS
    DF = mlp_w1.shape[2] * N_CHIPS
    PD = ple_w.shape[1]
    NIN = ssm_in_w.shape[2]

    xi, yi, ci = lax.axis_index("x"), lax.axis_index("y"), lax.axis_index("c")
    chip = 2 * xi + yi
    pos = jnp.stack([chip, ci]).astype(jnp.int32)

    x0 = x[0]
    tgt = loss_target[0]
    p0b, p1b = p[0, 0].astype(BF16), p[1, 0].astype(BF16)

    small_sh = jnp.concatenate([ssm_conv_w[0].reshape(-1), ssm_conv_b[0], ssm_norm_w[0],
                                ln_g.reshape(-1), ln_b.reshape(-1)])
    n_sh = small_sh.shape[0]
    small_all = _allgather_small("gather_small", _pad_to(small_sh, 1024).reshape(-1, 128))

    def halves(w, zero=None):
        w = w if zero is None else w + zero
        return w.astype(BF16).reshape((2, w.shape[0] // 2) + w.shape[1:])

    sh_pool = pool_w[0].astype(BF16)
    sh_pool = sh_pool.reshape((2, NG // 2) + sh_pool.shape[1:])
    started = {"ag0a": _gather_start("ag0a_start", [sh_pool, halves(mlp_w1[0])], after=small_all)}
    z0 = started["ag0a"]["token"][0, 0]
    groups = [("ag0b", [halves(mlp_w2[0], z0)]),
              ("ag0c", [halves(ple_gate_w[0], z0), halves(ple_w[0], z0)]),
              ("ag1a", [halves(ssm_in_w[0], z0)]),
              ("ag1b", [halves(ssm_out_w[0], z0), halves(mlp_w1[1], z0)]),
              ("ag1c", [halves(mlp_w2[1], z0), halves(ple_gate_w[1], z0), halves(ple_w[1], z0)])]

    def gather_end(tag, after):
        return _gather_forward(tag + "_fwd", _gather_wait(tag + "_wait", started[tag], after))

    def gather_land(tag, after):
        lands = _gather_wait(tag + "_wait", started[tag], after)
        return _split_start(tag + "_fwd_start", lands, _forward_copies, 3 * len(lands))

    def gather_done(tag, forwarding, after):
        return _split_wait(tag + "_fwd_wait", forwarding, _forward_copies, after)

    g_pool, g = gather_end("ag0a", [sh for _, shards in groups for sh in shards])
    prev = g_pool
    for tag, shards in groups:
        started[tag] = _gather_start(tag + "_start", shards, after=prev)
        prev = started[tag]["token"]
    all_started = prev

    g_w1, g_w2, g_pw, g_gw = {}, {}, {}, {}

    def set_w1(l, g):
        g_w1[l] = g.reshape(N_CHIPS, 1, D, DF // N_CHIPS)

    def set_w2(l, g2):
        g_w2[l] = g2.reshape(DF, D)

    def set_gate(l, ggw, gpw):
        g_gw[l] = ggw.reshape(D, D)
        g_pw[l] = gpw.reshape(N_CHIPS, 1, PD, D // N_CHIPS)

    set_w1(0, g)
    w_pool = jnp.transpose(g_pool.reshape(N_CHIPS, NG, GD // N_CHIPS, GD), (1, 0, 2, 3)).reshape(NG, GD, GD)

    small_all = small_all.reshape(N_DEV, -1)[0::2, :n_sh]
    cdq, niq, dq = CD // N_CHIPS, DI // N_CHIPS, D // N_CHIPS
    o = 0
    conv_w = jnp.concatenate([small_all[q, o:o + CONV_K * cdq].reshape(CONV_K, cdq) for q in range(N_CHIPS)], axis=1)
    o += CONV_K * cdq
    conv_b = small_all[:, o:o + cdq].reshape(1, CD)
    o += cdq
    norm_w = small_all[:, o:o + niq].reshape(1, DI)
    o += niq
    lng = jnp.transpose(small_all[:, o:o + 4 * dq].reshape(N_CHIPS, 2, 2, dq), (1, 2, 0, 3)).reshape(2, 2, 1, D)
    o += 4 * dq
    lnb = jnp.transpose(small_all[:, o:o + 4 * dq].reshape(N_CHIPS, 2, 2, dq), (1, 2, 0, 3)).reshape(2, 2, 1, D)

    pooled = _pool_fwd("pool_fwd", x0)
    hraw = _mm("pool_mm", "nn", pooled, w_pool, T, D, GD, tn=GD, deps=[all_started],
               a_spec=lambda tm, tn, tk: pl.BlockSpec((tm, tk), lambda i, j, k: (i, j)),
               b_spec=lambda tm, tn, tk: pl.BlockSpec((None, tk, tn), lambda i, j, k: (j, 0, 0)))
    x1, x1b, xh1, rs1 = _res_ln("ln00", x0, hraw, lng[0, 0], lnb[0, 0], scale=pool_scale)

    def mlp_fwd(l, xb, land_rest, deps=()):
        a, h2b = _mm(f"mlp{l}_up", "nn", xb, g_w1[l], T, DF, D, tn=min(1024, DF // N_CHIPS), deps=deps,
                     b_spec=_colshard_b(0, DF // N_CHIPS), out_dtype=(F32, BF16),
                     epi=lambda acc: (acc, jnp.square(jnp.maximum(acc, 0.0))))
        land_rest(a)
        h = _mm(f"mlp{l}_down", "nn", h2b, g_w2[l], T, D, DF)
        return a, h2b, h

    def ple_fwd(l, xb, pb):
        gl = _mm(f"gate{l}_logit", "nn", xb, g_gw[l], T, D, D)
        e = _mm(f"gate{l}_emb", "nn", pb, g_pw[l], T, D, PD, tn=min(1024, D // N_CHIPS),
                b_spec=_colshard_b(0, D // N_CHIPS))
        return gl, e

    a0, h2b0, h0 = mlp_fwd(0, x1b, lambda a: set_w2(0, *gather_end("ag0b", [a])))
    set_gate(0, *gather_end("ag0c", [h0]))
    x2, x2b, xh2, rs2 = _res_ln("ln01", x1, h0, lng[0, 1], lnb[0, 1])
    gl0, e0 = ple_fwd(0, x2b, p0b)
    x3, x3b, gate0 = _gate_fwd("gate0", x2, gl0, e0)

    g_in, = gather_end("ag1a", [x3b])

    def in_proj_cols(lo, hi):
        parts = [g_in[q].reshape(D, NIN)[:, max(lo - q * NIN, 0):min(hi - q * NIN, NIN)]
                 for q in range(N_CHIPS) if lo < (q + 1) * NIN and hi > q * NIN]
        return parts[0] if len(parts) == 1 else jnp.concatenate(parts, axis=1)

    w_z, w_xbc, w_dt = in_proj_cols(0, DI), in_proj_cols(DI, DI + CD), in_proj_cols(DI + CD, N_CHIPS * NIN)
    z = _mm("ssm_in_z", "nn", x3b, w_z, T, DI, D)
    xbc_pre = _mm("ssm_in_xbc", "nn", x3b, w_xbc, T, CD, D)
    dt_pre = _mm("ssm_in_dt", "nn", x3b, w_dt, T, H, D)
    xbc = _conv_fwd("conv_fwd", xbc_pre, conv_w, conv_b)
    dt = _dt_fwd("dt_fwd", dt_pre, ssm_dt_bias)
    dt_g = jnp.transpose(dt.reshape(T, N_GROUPS, HPG), (1, 0, 2))
    dtc = jnp.pad(dt_g, ((0, 0), (0, 0), (0, HEAD_PAD - HPG)))
    dtr = jnp.pad(jnp.transpose(dt_g, (0, 2, 1)), ((0, 0), (0, ROW_PAD - HPG), (0, 0)))
    al_g = ssm_a_log.reshape(N_GROUPS, HPG)
    alr = jnp.pad(al_g, ((0, 0), (0, HEAD_PAD - HPG)))[:, None, :]
    alc = jnp.pad(al_g, ((0, 0), (0, ROW_PAD - HPG)))[:, :, None]
    d_e = jnp.repeat(ssm_d.reshape(N_GROUPS, HPG), HEAD_DIM, axis=1)[:, None, :]
    fw1b = gather_land("ag1b", [xbc])
    ysc, hprev = _ssd_fwd("ssd_fwd", xbc, dtc, dtr, alr, alc, d_e, DI, deps=[fw1b["token"]])
    g_out, g = gather_done("ag1b", fw1b, [ysc])
    set_w1(1, g)
    w_out = g_out.reshape(DI, D)
    ynb, rsn = _gnorm_fwd("gnorm_fwd", ysc, z, norm_w)
    h1 = _mm("ssm_out", "nn", ynb, w_out, T, D, DI)
    fw1c = gather_land("ag1c", [h1])
    x4, x4b, xh4, rs4 = _res_ln("ln10", x3, h1, lng[1, 0], lnb[1, 0])

    def land_1c(a):
        g2, ggw, gpw = gather_done("ag1c", fw1c, [a])
        set_w2(1, g2)
        set_gate(1, ggw, gpw)

    a1, h2b1, h2 = mlp_fwd(1, x4b, land_1c, deps=[fw1c["token"]])
    x5, x5b, xh5, rs5 = _res_ln("ln11", x4, h2, lng[1, 1], lnb[1, 1])
    gl1, e1 = ple_fwd(1, x5b, p1b)
    dx6, gate1, loss_parts = _gate_loss("gate1_loss", x5, gl1, e1, tgt)
    loss_local = jnp.sum(loss_parts[0::8, 0])

    def ple_bwd(l, dxo, gate, e, xb, pb, deps=()):
        dgl, de = _gate_bwd(f"gate{l}_bwd", dxo, gate, e)
        d_gw = _mm(f"gate{l}_dw", "tn", xb, dgl, D, D, T, out_dtype=BF16, deps=deps).reshape(N_CHIPS, D // N_CHIPS, D)
        d_pw = _mm(f"gate{l}_dpw", "tn", pb, de, PD, D, T, out_dtype=BF16, tn=min(1024, D // N_CHIPS),
                   o_shape=(N_CHIPS, PD, D // N_CHIPS), o_spec=_colshard_o(D // N_CHIPS))
        dx = _mm(f"gate{l}_dx", "nt", dgl, g_gw[l], T, D, D, deps=deps,
                 epi=lambda acc, r: acc + r, extras=[dxo])
        return dx, d_gw, d_pw

    def mlp_bwd(l, du, dub, a, h2b, xb):
        d_w2 = _mm(f"mlp{l}_dw2", "tn", h2b, dub, DF, D, T, out_dtype=BF16).reshape(N_CHIPS, DF // N_CHIPS, D)
        da = _mm(f"mlp{l}_da", "nt", dub, g_w2[l], T, DF, D, out_dtype=BF16,
                 epi=lambda acc, av: acc * (2.0 * jnp.maximum(av, 0.0)), extras=[a])
        d_w1 = _mm(f"mlp{l}_dw1", "tn", xb, da, D, DF, T, out_dtype=BF16, tn=min(1024, DF // N_CHIPS),
                   o_shape=(N_CHIPS, D, DF // N_CHIPS), o_spec=_colshard_o(DF // N_CHIPS))
        dx = _mm(f"mlp{l}_dx", "nt", da, g_w1[l], T, D, DF, k_unit=DF // N_CHIPS,
                 b_spec=_colshard_bt(0, DF // N_CHIPS), epi=lambda acc, r: acc + ALPHA * r, extras=[du])
        return dx, d_w1, d_w2

    dx5, d_gw1, d_pw1 = ple_bwd(1, dx6, gate1, e1, x5b, p1b)
    du5, du5b, dg11, db11 = _ln_bwd("ln11_bwd", dx5, xh5, rs5, lng[1, 1])
    dx4, d_w1_1, d_w2_1 = mlp_bwd(1, du5, du5b, a1, h2b1, x4b)
    parts1a = [d_w1_1, d_w2_1, d_gw1, d_pw1]
    pair1a = _reduce_pair_start("rs1a", parts1a)
    du4, du4b, dg10, db10 = _ln_bwd("ln10_bwd", dx4, xh4, rs4, lng[1, 0])
    d_wout = _mm("ssm_out_dw", "tn", ynb, du4b, DI, D, T, out_dtype=BF16,
                 deps=[pair1a["token"]]).reshape(N_CHIPS, DI // N_CHIPS, D)
    red1a = _reduce_pair_end("rs1a", pos, pair1a, [d_wout])
    dyn = _mm("ssm_out_dx", "nt", du4b, w_out, T, DI, D, deps=[red1a["token"]])
    dysc, dzb, dnorm_w = _gnorm_bwd("gnorm_bwd", dyn, ysc, z, norm_w, rsn)
    dxs, dbm, dcm, ddt_g, dalog_g, dd_g = _ssd_bwd("ssd_bwd", dysc, xbc, dtc, dtr, alr, alc, d_e, hprev, DI)
    dconv, dconv_w, dconv_b = _conv_bwd_a("conv_bwd_a", dxs, dbm, dcm, xbc_pre, conv_w, conv_b)
    dxbcb = _conv_bwd_b("conv_bwd_b", dconv, conv_w)
    ddt = jnp.transpose(ddt_g[:, :, :HPG], (1, 0, 2)).reshape(T, H)
    _, ddtpb, ddt_bias = _dt_bwd("dt_bwd", ddt, dt_pre, ssm_dt_bias)
    da_log = dalog_g[:, 0, :HPG].reshape(1, H)
    dd_skip = dd_g[:, 0, :HPG].reshape(1, H)
    d_wz = _mm("ssm_in_dwz", "tn", x3b, dzb, D, DI, T, out_dtype=BF16)
    d_wx = _mm("ssm_in_dwx", "tn", x3b, dxbcb, D, CD, T, out_dtype=BF16)
    d_wdt = _mm("ssm_in_dwdt", "tn", x3b, ddtpb, D, H, T, out_dtype=BF16)

    def in_proj_shard(q):
        parts = []
        for piece, start in ((d_wz, 0), (d_wx, DI), (d_wdt, DI + CD)):
            lo, hi = max(q * NIN - start, 0), min((q + 1) * NIN - start, piece.shape[1])
            if lo < hi:
                parts.append(piece[:, lo:hi])
        return parts[0] if len(parts) == 1 else jnp.concatenate(parts, axis=1)

    d_win = jnp.stack([in_proj_shard(q) for q in range(N_CHIPS)])
    parts1b = [d_win, d_wout]
    pair1b = _reduce_pair_start("rs1b", parts1b)
    dx3 = _mm("ssm_in_dxz", "nt", dzb, w_z, T, D, DI, epi=lambda acc, r: acc + ALPHA * r, extras=[du4],
              deps=[pair1b["token"]])
    dx3 = _mm("ssm_in_dxx", "nt", dxbcb, w_xbc, T, D, CD, epi=lambda acc, r: acc + r, extras=[dx3])
    dx3 = _mm("ssm_in_dxdt", "nt", ddtpb, w_dt, T, D, H, epi=lambda acc, r: acc + r, extras=[dx3])

    red1b = _reduce_pair_end("rs1b", pos, pair1b, [dx3])

    dx2, d_gw0, d_pw0 = ple_bwd(0, dx3, gate0, e0, x2b, p0b, deps=[red1b["token"]])
    du2, du2b, dg01, db01 = _ln_bwd("ln01_bwd", dx2, xh2, rs2, lng[0, 1])
    dx1, d_w1_0, d_w2_0 = mlp_bwd(0, du2, du2b, a0, h2b0, x1b)
    parts0a = [d_w1_0, d_w2_0, d_gw0, d_pw0]
    pair0a = _reduce_pair_start("rs0a", parts0a)
    du1, dhrb, dg00, db00, dscale = _ln_bwd("ln00_bwd", dx1, xh1, rs1, lng[0, 0], hraw=hraw, scale=pool_scale)
    dln_g = jnp.stack([jnp.stack([dg00, dg01]), jnp.stack([dg10, dg11])]).reshape(-1)
    dln_b = jnp.stack([jnp.stack([db00, db01]), jnp.stack([db10, db11])]).reshape(-1)
    small_g = jnp.concatenate([dscale.reshape(-1), dconv_w.reshape(-1), dconv_b.reshape(-1), ddt_bias.reshape(-1),
                               da_log.reshape(-1), dd_skip.reshape(-1), dnorm_w.reshape(-1), dln_g, dln_b,
                               loss_local.reshape(1)])
    n_sg = small_g.shape[0]
    sg_all = _allgather_small("gather_small_grads", _pad_to(small_g, 1024).reshape(-1, 128),
                              after=[d_w1_0, d_w2_0])
    d_wpool = _mm("pool_dw", "tn", pooled, dhrb, D, GD, T, tm=GD, tn=GD, out_dtype=BF16, deps=[pair0a["token"]],
                  b_spec=lambda tm, tn, tk: pl.BlockSpec((tk, tn), lambda i, j, k: (k, i)),
                  o_shape=(NG, GD, GD), o_spec=lambda tm, tn, tk: pl.BlockSpec((None, tm, tn), lambda i, j, k: (i, 0, 0)))
    d_wpool = jnp.transpose(d_wpool.reshape(NG, N_CHIPS, GD // N_CHIPS, GD), (1, 0, 2, 3)).reshape(N_CHIPS, NG * GD // N_CHIPS, GD)
    red0a = _reduce_pair_end("rs0a", pos, pair0a, [d_wpool, sg_all])
    parts0b = [d_wpool]
    pair0b = _reduce_pair_start("rs0b", parts0b)
    dpooled = _mm("pool_dx", "nt", dhrb, w_pool, T, D, GD, tn=GD, deps=[red0a["token"], pair0b["token"]],
                  a_spec=lambda tm, tn, tk: pl.BlockSpec((tm, tk), lambda i, j, k: (i, j)),
                  b_spec=lambda tm, tn, tk: pl.BlockSpec((None, tn, tk), lambda i, j, k: (j, 0, 0)))
    grad_x = _pool_bwd("pool_bwd", dpooled, du1)
    red0b = _reduce_pair_end("rs0b", pos, pair0b, [grad_x])

    def adam(name, w, m, v, g, layer=0, prev=None):
        w3, m3, v3 = (t.reshape((t.shape[0], -1, t.shape[-1])) for t in (w, m, v))
        return _adamw(name, w3, m3, v3, g, layer, prev)

    sum1a = _reduce_sum_start("rs1a", pos, red1a, [red0b["token"]])
    sum1b = _reduce_sum_start("rs1b", pos, red1b, [sum1a["token"]])
    r_w1_1, r_w2_1, r_gw1, r_pw1 = _reduce_sum_end("rs1a", sum1a, [sum1b["token"]], [p.shape[1:] for p in parts1a])
    stacked = {"mlp_w1": (mlp_w1, m_mlp_w1, v_mlp_w1), "mlp_w2": (mlp_w2, m_mlp_w2, v_mlp_w2),
               "ple_w": (ple_w, m_ple_w, v_ple_w), "ple_gate_w": (ple_gate_w, m_ple_gate_w, v_ple_gate_w)}
    grads1 = {"mlp_w1": r_w1_1, "mlp_w2": r_w2_1, "ple_w": r_pw1, "ple_gate_w": r_gw1}
    upper = {nm: adam(f"adam_{nm}_l1", *stacked[nm], grads1[nm], layer=1) for nm in stacked}
    r_in, r_out = _reduce_sum_end("rs1b", sum1b, [t[1] for t in upper.values()], [p.shape[1:] for p in parts1b])
    big = {"ssm_in_w": [jnp.swapaxes(t, 1, 2) for t in adam(
               "adam_ssm_in_w", *(jnp.swapaxes(t, 1, 2) for t in (ssm_in_w, m_ssm_in_w, v_ssm_in_w)), r_in.T)],
           "ssm_out_w": [r_out] + list(_adamw_sc("adam_ssm_out_w_sc", ssm_out_w.reshape(-1), m_ssm_out_w.reshape(-1),
                                                 v_ssm_out_w.reshape(-1), r_out.reshape(-1)))}

    sg = _sum_peers("sum_small_grads", sg_all).reshape(-1)[:n_sg]
    o = 0

    def take(nel):
        nonlocal o
        v = sg[o:o + nel]
        o += nel
        return v

    g_scale = take(D).reshape(1, D)
    g_conv_w_full = take(CONV_K * CD).reshape(CONV_K, CD)
    g_conv_b_full = take(CD).reshape(1, CD)
    g_dt_bias = take(H).reshape(1, H)
    g_a_log = take(H).reshape(1, H)
    g_d = take(H).reshape(1, H)
    g_norm_full = take(DI).reshape(1, DI)
    g_lng_full = take(4 * D).reshape(2, 2, D)
    g_lnb_full = take(4 * D).reshape(2, 2, D)
    loss = take(1).reshape(())
    g_conv_w = lax.dynamic_slice_in_dim(g_conv_w_full, chip * cdq, cdq, axis=1)[None]
    g_conv_b = lax.dynamic_slice_in_dim(g_conv_b_full, chip * cdq, cdq, axis=1)
    g_norm = lax.dynamic_slice_in_dim(g_norm_full, chip * niq, niq, axis=1)
    g_lng = lax.dynamic_slice_in_dim(g_lng_full, chip * dq, dq, axis=2)
    g_lnb = lax.dynamic_slice_in_dim(g_lnb_full, chip * dq, dq, axis=2)

    small = [("pool_scale", pool_scale, m_pool_scale, v_pool_scale, g_scale),
             ("ssm_conv_w", ssm_conv_w, m_ssm_conv_w, v_ssm_conv_w, g_conv_w),
             ("ssm_conv_b", ssm_conv_b, m_ssm_conv_b, v_ssm_conv_b, g_conv_b),
             ("ssm_dt_bias", ssm_dt_bias, m_ssm_dt_bias, v_ssm_dt_bias, g_dt_bias),
             ("ssm_a_log", ssm_a_log, m_ssm_a_log, v_ssm_a_log, g_a_log),
             ("ssm_d", ssm_d, m_ssm_d, v_ssm_d, g_d),
             ("ssm_norm_w", ssm_norm_w, m_ssm_norm_w, v_ssm_norm_w, g_norm),
             ("ln_g", ln_g, m_ln_g, v_ln_g, g_lng),
             ("ln_b", ln_b, m_ln_b, v_ln_b, g_lnb)]

    def pack(idx):
        flat = _pad_to(jnp.concatenate([s[idx].reshape(-1) for s in small]), 1024)
        return flat.reshape(1, -1, 128)

    sm_out = _adamw("adam_small", pack(1), pack(2), pack(3), pack(4)[0])
    small_res = {}
    o = 0
    for s in small:
        nel = s[1].size
        small_res[s[0]] = [t.reshape(-1)[o:o + nel].reshape(s[1].shape) for t in sm_out]
        o += nel

    r_w1_0, r_w2_0, r_gw0, r_pw0 = _reduce_end(
        "rs0a", pos, red0a, [big["ssm_in_w"][1], sm_out[1]] + [upper[nm][1] for nm in upper],
        [p.shape[1:] for p in parts0a])
    r_pool, = _reduce_end("rs0b", pos, red0b, [r_w1_0], [p.shape[1:] for p in parts0b])
    grads0 = {"mlp_w1": r_w1_0, "mlp_w2": r_w2_0, "ple_w": r_pw0, "ple_gate_w": r_gw0}
    big["pool_w"] = adam("adam_pool_w", pool_w, m_pool_w, v_pool_w, r_pool)
    for nm in stacked:
        big[nm] = adam(f"adam_{nm}_l0", *stacked[nm], grads0[nm], layer=0, prev=upper[nm])
    shapes = {"pool_w": pool_w.shape, "ssm_in_w": ssm_in_w.shape, "ssm_out_w": ssm_out_w.shape,
              **{nm: stacked[nm][0].shape for nm in stacked}}
    big = {nm: [t.reshape(shapes[nm]) for t in big[nm]] for nm in big}

    order = ["pool_w", "pool_scale", "ssm_in_w", "ssm_conv_w", "ssm_conv_b", "ssm_dt_bias", "ssm_a_log", "ssm_d",
             "ssm_norm_w", "ssm_out_w", "mlp_w1", "mlp_w2", "ln_g", "ln_b", "ple_w", "ple_gate_w"]
    res = {**big, **small_res}
    outs = [loss, grad_x[None]]
    for kind in range(4):
        outs += [res[nm][kind] for nm in order]
    return tuple(outs)
```

```python
import jax
import jax.numpy as jnp
from jax import lax
from jax.experimental import pallas as pl
from jax.experimental.pallas import tpu as pltpu

F32 = jnp.float32
BF16 = jnp.bfloat16
MESH = pl.DeviceIdType.MESH

DEPTH = 2
ALPHA = (2.0 * DEPTH) ** 0.25
LN_EPS = 1e-5
RMS_EPS = 1e-5
POOL_WINDOWS = (2, 4, 8, 16)
POOL_HALO = 16
HEAD_DIM = 64
HEAD_PAIR = 2 * HEAD_DIM
N_GROUPS = 8
D_STATE = 128
CHUNK = 128
CONV_K = 4
CONV_HALO = 8
HEAD_PAD = 128
ROW_PAD = 8
N_CHIPS = 4
N_DEV = 8
ADAM_LR = 0.001
ADAM_B1 = 0.9
ADAM_B2 = 0.999
ADAM_EPS = 1e-08
ADAM_WD = 0.01
ADAM_STEP = 10
VMEM_LIMIT = 56 * 1024 * 1024
ADAM_BLOCK_BYTES = 1024 * 1024
MM_VMEM_BUDGET = 40 * 1024 * 1024


_ANY = pl.BlockSpec(memory_space=pl.ANY)


def _cp(*sem):
    return pltpu.CompilerParams(dimension_semantics=sem, vmem_limit_bytes=VMEM_LIMIT)


def _pick(dim, pref):
    t = pref
    while t >= 128:
        if dim % t == 0:
            return t
        t //= 2
    return dim


def _rows(rows, row_bytes, budget):
    t = rows
    while t * row_bytes > budget and t % 16 == 0:
        t //= 2
    return t


def _sigmoid(v):
    return 1.0 / (1.0 + jnp.exp(-v))


_DIMS = {"nn": (((1,), (0,)), ((), ())), "nt": (((1,), (1,)), ((), ())), "tn": (((0,), (0,)), ((), ()))}


def _pick_k(k_unit, fixed_bytes, per_k_bytes):
    for n in range(1, k_unit // 128 + 1):
        if k_unit % n == 0 and (n == 1 or (k_unit // n) % 128 == 0):
            if fixed_bytes + (k_unit // n) * per_k_bytes <= MM_VMEM_BUDGET:
                return k_unit // n
    return min(k_unit, 128)


def _mm(name, form, a, b, M, N, K, *, tm=1024, tn=1024, k_unit=None, a_spec=None, b_spec=None,
        o_shape=None, o_spec=None, out_dtype=F32, pro=None, epi=None, extras=(), deps=()):
    tm, tn = _pick(M, tm), _pick(N, tn)
    out_dtypes = out_dtype if isinstance(out_dtype, tuple) else (out_dtype,)
    n_out = len(out_dtypes)
    in_place = n_out == 1 and out_dtypes[0] == F32
    fixed = 2 * tm * tn * (sum(jnp.dtype(d).itemsize for d in out_dtypes) + 4 * len(extras))
    fixed += 0 if in_place else 4 * tm * tn
    tk = _pick_k(K if k_unit is None else k_unit, fixed,
                 2 * (tm * a.dtype.itemsize + tn * b.dtype.itemsize))
    nk = K // tk
    if a_spec is None:
        a_spec = (pl.BlockSpec((tk, tm), lambda i, j, k: (k, i)) if form == "tn"
                  else pl.BlockSpec((tm, tk), lambda i, j, k: (i, k)))
    else:
        a_spec = a_spec(tm, tn, tk)
    if b_spec is None:
        b_spec = (pl.BlockSpec((tn, tk), lambda i, j, k: (j, k)) if form == "nt"
                  else pl.BlockSpec((tk, tn), lambda i, j, k: (k, j)))
    else:
        b_spec = b_spec(tm, tn, tk)
    if o_spec is None:
        o_spec = pl.BlockSpec((tm, tn), lambda i, j, k: (i, j))
        o_shape = (M, N)
    else:
        o_spec = o_spec(tm, tn, tk)
    ex_arrays = [e for e in extras]
    ex_specs = [pl.BlockSpec((tm, tn), lambda i, j, k: (i, j)) for _ in extras]
    ne = len(ex_arrays)
    nd = len(deps)
    dep_specs = [pl.BlockSpec((8, 128), lambda i, j, k: (0, 0)) for _ in deps]
    dims = _DIMS[form]
    use_scratch = nk > 1 and not in_place

    def body(a_ref, b_ref, *rest):
        ex_refs = rest[:ne]
        o_refs = rest[ne + nd:ne + nd + n_out]
        at = a_ref[...]
        if pro is not None:
            at = pro(at)
        p = lax.dot_general(at.astype(BF16), b_ref[...].astype(BF16), dims, preferred_element_type=F32)

        def finish(acc):
            res = acc if epi is None else epi(acc, *[r[...] for r in ex_refs])
            res = res if isinstance(res, tuple) else (res,)
            for o_ref, r, d in zip(o_refs, res, out_dtypes):
                o_ref[...] = r.astype(d)

        if nk == 1:
            finish(p)
        else:
            acc_ref = rest[ne + nd + n_out] if use_scratch else o_refs[0]
            k = pl.program_id(2)

            @pl.when(k == 0)
            def _():
                acc_ref[...] = p

            @pl.when(jnp.logical_and(k > 0, k < nk - 1))
            def _():
                acc_ref[...] += p

            @pl.when(k == nk - 1)
            def _():
                finish(acc_ref[...] + p)

    res = pl.pallas_call(
        body, name=name, grid=(M // tm, N // tn, nk),
        in_specs=[a_spec, b_spec] + ex_specs + dep_specs, out_specs=[o_spec] * n_out,
        out_shape=[jax.ShapeDtypeStruct(o_shape, d) for d in out_dtypes],
        scratch_shapes=[pltpu.VMEM((tm, tn), F32)] if use_scratch else [],
        compiler_params=_cp("parallel", "parallel", "arbitrary"),
    )(a, b, *ex_arrays, *deps)
    return res if n_out > 1 else res[0]


def _colshard_b(l, n_per):
    def make(tm, tn, tk):
        nb = n_per // tn
        return pl.BlockSpec((None, None, tk, tn), lambda i, j, k: (j // nb, l, k, j % nb))
    return make


def _colshard_bt(l, n_per):
    def make(tm, tn, tk):
        nb = n_per // tk
        return pl.BlockSpec((None, None, tn, tk), lambda i, j, k: (k // nb, l, j, k % nb))
    return make


def _colshard_o(n_per):
    def make(tm, tn, tk):
        nb = n_per // tn
        return pl.BlockSpec((None, tm, tn), lambda i, j, k: (j // nb, i, j % nb))
    return make


def _res_ln(name, xprev, h, g, b, scale=None):
    T, D = xprev.shape
    tr = _pick(T, 256)
    row = pl.BlockSpec((tr, D), lambda i: (i, 0))
    vec = pl.BlockSpec((1, D), lambda i: (0, 0))
    has_scale = scale is not None

    def body(*refs):
        if has_scale:
            x_ref, h_ref, s_ref, g_ref, b_ref, y_ref, yb_ref, xh_ref, rs_ref = refs
            hh = h_ref[...] * s_ref[...]
        else:
            x_ref, h_ref, g_ref, b_ref, y_ref, yb_ref, xh_ref, rs_ref = refs
            hh = h_ref[...]
        u = ALPHA * x_ref[...] + hh
        mu = jnp.mean(u, axis=-1, keepdims=True)
        d = u - mu
        var = jnp.mean(d * d, axis=-1, keepdims=True)
        rs = lax.rsqrt(var + LN_EPS)
        xh = d * rs
        y = xh * g_ref[...] + b_ref[...]
        y_ref[...] = y
        yb_ref[...] = y.astype(BF16)
        xh_ref[...] = xh
        rs_ref[...] = rs

    ins = [xprev, h] + ([scale] if has_scale else []) + [g, b]
    specs = [row, row] + ([vec] if has_scale else []) + [vec, vec]
    return pl.pallas_call(
        body, name=name, grid=(T // tr,), in_specs=specs,
        out_specs=[row, row, row, pl.BlockSpec((tr, 1), lambda i: (i, 0))],
        out_shape=[jax.ShapeDtypeStruct((T, D), F32), jax.ShapeDtypeStruct((T, D), BF16),
                   jax.ShapeDtypeStruct((T, D), F32), jax.ShapeDtypeStruct((T, 1), F32)],
        compiler_params=_cp("parallel"),
    )(*ins)


def _accum(ref, part, first):
    @pl.when(first)
    def _():
        ref[...] = part

    @pl.when(jnp.logical_not(first))
    def _():
        ref[...] += part


def _ln_bwd(name, dy, xh, rs, g, hraw=None, scale=None):
    T, D = dy.shape
    tr = _pick(T, 256)
    row = pl.BlockSpec((tr, D), lambda i: (i, 0))
    vec = pl.BlockSpec((1, D), lambda i: (0, 0))
    has_scale = scale is not None

    def body(*refs):
        if has_scale:
            dy_ref, xh_ref, rs_ref, g_ref, hr_ref, s_ref, du_ref, dub_ref, dg_ref, db_ref, ds_ref = refs
        else:
            dy_ref, xh_ref, rs_ref, g_ref, du_ref, dub_ref, dg_ref, db_ref = refs
        first = pl.program_id(0) == 0
        dyv = dy_ref[...]
        xhv = xh_ref[...]
        dxh = dyv * g_ref[...]
        m1 = jnp.mean(dxh, axis=-1, keepdims=True)
        m2 = jnp.mean(dxh * xhv, axis=-1, keepdims=True)
        du = rs_ref[...] * (dxh - m1 - xhv * m2)
        du_ref[...] = du
        if has_scale:
            dub_ref[...] = (du * s_ref[...]).astype(BF16)
            _accum(ds_ref, jnp.sum(du * hr_ref[...], axis=0, keepdims=True), first)
        else:
            dub_ref[...] = du.astype(BF16)
        _accum(dg_ref, jnp.sum(dyv * xhv, axis=0, keepdims=True), first)
        _accum(db_ref, jnp.sum(dyv, axis=0, keepdims=True), first)

    ins = [dy, xh, rs, g] + ([hraw, scale] if has_scale else [])
    specs = [row, row, pl.BlockSpec((tr, 1), lambda i: (i, 0)), vec] + ([row, vec] if has_scale else [])
    n_vec = 3 if has_scale else 2
    return pl.pallas_call(
        body, name=name, grid=(T // tr,), in_specs=specs,
        out_specs=[row, row] + [vec] * n_vec,
        out_shape=[jax.ShapeDtypeStruct((T, D), F32), jax.ShapeDtypeStruct((T, D), BF16)]
        + [jax.ShapeDtypeStruct((1, D), F32)] * n_vec,
        compiler_params=_cp("arbitrary"),
    )(*ins)


def _gate_fwd(name, x, gl, e):
    T, D = x.shape
    tr = _pick(T, 256)
    row = pl.BlockSpec((tr, D), lambda i: (i, 0))

    def body(x_ref, gl_ref, e_ref, xo_ref, xob_ref, gate_ref):
        gate = _sigmoid(gl_ref[...])
        xo = x_ref[...] + gate * e_ref[...]
        xo_ref[...] = xo
        xob_ref[...] = xo.astype(BF16)
        gate_ref[...] = gate

    return pl.pallas_call(
        body, name=name, grid=(T // tr,), in_specs=[row, row, row], out_specs=[row, row, row],
        out_shape=[jax.ShapeDtypeStruct((T, D), F32), jax.ShapeDtypeStruct((T, D), BF16),
                   jax.ShapeDtypeStruct((T, D), F32)],
        compiler_params=_cp("parallel"),
    )(x, gl, e)


def _gate_loss(name, x, gl, e, tgt):
    T, D = x.shape
    tr = _pick(T, 256)
    row = pl.BlockSpec((tr, D), lambda i: (i, 0))

    def body(x_ref, gl_ref, e_ref, t_ref, dy_ref, gate_ref, lp_ref):
        gate = _sigmoid(gl_ref[...])
        err = x_ref[...] + gate * e_ref[...] - t_ref[...]
        dy_ref[...] = err * (1.0 / D)
        gate_ref[...] = gate
        s = jnp.sum(jnp.mean(err * err, axis=-1, keepdims=True), axis=0, keepdims=True)
        lp_ref[...] = jnp.broadcast_to(0.5 * s, (8, 128))

    return pl.pallas_call(
        body, name=name, grid=(T // tr,), in_specs=[row] * 4,
        out_specs=[row, row, pl.BlockSpec((8, 128), lambda i: (i, 0))],
        out_shape=[jax.ShapeDtypeStruct((T, D), F32), jax.ShapeDtypeStruct((T, D), F32),
                   jax.ShapeDtypeStruct((T // tr * 8, 128), F32)],
        compiler_params=_cp("parallel"),
    )(x, gl, e, tgt)


def _gate_bwd(name, dxo, gate, e):
    T, D = dxo.shape
    tr = _pick(T, 256)
    row = pl.BlockSpec((tr, D), lambda i: (i, 0))

    def body(d_ref, gate_ref, e_ref, dgl_ref, de_ref):
        d = d_ref[...]
        gate = gate_ref[...]
        dgl_ref[...] = (d * e_ref[...] * gate * (1.0 - gate)).astype(BF16)
        de_ref[...] = (d * gate).astype(BF16)

    return pl.pallas_call(
        body, name=name, grid=(T // tr,), in_specs=[row] * 3, out_specs=[row, row],
        out_shape=[jax.ShapeDtypeStruct((T, D), BF16)] * 2,
        compiler_params=_cp("parallel"),
    )(dxo, gate, e)


def _gnorm_fwd(name, y, z, w):
    T, DI = y.shape
    tr = _pick(T, 128)
    row = pl.BlockSpec((tr, DI), lambda i: (i, 0))
    vec = pl.BlockSpec((1, DI), lambda i: (0, 0))
    col = pl.BlockSpec((tr, 1), lambda i: (i, 0))

    def body(y_ref, z_ref, w_ref, o_ref, rs_ref):
        zv = z_ref[...]
        yz = y_ref[...] * (zv * _sigmoid(zv))
        rs = lax.rsqrt(jnp.mean(yz * yz, axis=-1, keepdims=True) + RMS_EPS)
        o_ref[...] = (yz * rs * w_ref[...]).astype(BF16)
        rs_ref[...] = rs

    return pl.pallas_call(
        body, name=name, grid=(T // tr,), in_specs=[row, row, vec], out_specs=[row, col],
        out_shape=[jax.ShapeDtypeStruct((T, DI), BF16), jax.ShapeDtypeStruct((T, 1), F32)],
        compiler_params=_cp("parallel"),
    )(y, z, w)


def _gnorm_bwd(name, dyn, y, z, w, rs):
    T, DI = y.shape
    tr = _pick(T, 128)
    row = pl.BlockSpec((tr, DI), lambda i: (i, 0))
    vec = pl.BlockSpec((1, DI), lambda i: (0, 0))
    col = pl.BlockSpec((tr, 1), lambda i: (i, 0))

    def body(d_ref, y_ref, z_ref, w_ref, rs_ref, dy_ref, dz_ref, dw_ref):
        first = pl.program_id(0) == 0
        zv = z_ref[...]
        yv = y_ref[...]
        sg = _sigmoid(zv)
        sz = zv * sg
        rsv = rs_ref[...]
        yzh = yv * sz * rsv
        dv = d_ref[...]
        gw = dv * w_ref[...]
        m = jnp.mean(gw * yzh, axis=-1, keepdims=True)
        dyz = rsv * (gw - yzh * m)
        dy_ref[...] = dyz * sz
        dz_ref[...] = (dyz * yv * (sg * (1.0 + zv * (1.0 - sg)))).astype(BF16)
        _accum(dw_ref, jnp.sum(dv * yzh, axis=0, keepdims=True), first)

    return pl.pallas_call(
        body, name=name, grid=(T // tr,), in_specs=[row, row, row, vec, col], out_specs=[row, row, vec],
        out_shape=[jax.ShapeDtypeStruct((T, DI), F32), jax.ShapeDtypeStruct((T, DI), BF16),
                   jax.ShapeDtypeStruct((1, DI), F32)],
        compiler_params=_cp("arbitrary"),
    )(dyn, y, z, w, rs)


def _sel4(j, vals):
    return jnp.where(j == 0, vals[0], jnp.where(j == 1, vals[1], jnp.where(j == 2, vals[2], vals[3])))


def _pool_cnt(i, j, tr, rows, offset):
    t = i * tr + offset + lax.broadcasted_iota(jnp.int32, (rows, 1), 0)
    win = _sel4(j, POOL_WINDOWS)
    return jnp.minimum(t + 1, win).astype(F32)


def _pool_fwd(name, x):
    T, D = x.shape
    gd = D // len(POOL_WINDOWS)
    tr = _pick(T, 512)
    hb = tr // POOL_HALO

    def body(x_ref, h_ref, o_ref):
        i, j = pl.program_id(0), pl.program_id(1)
        xv = x_ref[...]
        halo = jnp.where(i > 0, h_ref[...], 0.0)
        cat = jnp.concatenate([halo, xv], axis=0)
        s2 = cat + pltpu.roll(cat, 1, 0)
        s4 = s2 + pltpu.roll(s2, 2, 0)
        s8 = s4 + pltpu.roll(s4, 4, 0)
        s16 = s8 + pltpu.roll(s8, 8, 0)
        sel = _sel4(j, (s2, s4, s8, s16))[POOL_HALO:]
        o_ref[...] = (sel / _pool_cnt(i, j, tr, tr, 0) - xv).astype(BF16)

    return pl.pallas_call(
        body, name=name, grid=(T // tr, len(POOL_WINDOWS)),
        in_specs=[pl.BlockSpec((tr, gd), lambda i, j: (i, j)),
                  pl.BlockSpec((POOL_HALO, gd), lambda i, j: (jnp.maximum(i * hb - 1, 0), j))],
        out_specs=pl.BlockSpec((tr, gd), lambda i, j: (i, j)),
        out_shape=jax.ShapeDtypeStruct((T, D), BF16),
        compiler_params=_cp("parallel", "parallel"),
    )(x, x)


def _pool_bwd(name, dp, du):
    T, D = dp.shape
    gd = D // len(POOL_WINDOWS)
    tr = _pick(T, 512)
    hb = tr // POOL_HALO
    last_h = T // POOL_HALO - 1
    n = tr + POOL_HALO

    def body(dp_ref, h_ref, du_ref, o_ref):
        i, j = pl.program_id(0), pl.program_id(1)
        dpv = dp_ref[...]
        q = dpv / _pool_cnt(i, j, tr, tr, 0)
        qh = jnp.where(i < pl.num_programs(0) - 1, h_ref[...] / _pool_cnt(i, j, tr, POOL_HALO, tr), 0.0)
        cat = jnp.concatenate([q, qh], axis=0)
        f2 = cat + pltpu.roll(cat, n - 1, 0)
        f4 = f2 + pltpu.roll(f2, n - 2, 0)
        f8 = f4 + pltpu.roll(f4, n - 4, 0)
        f16 = f8 + pltpu.roll(f8, n - 8, 0)
        sel = _sel4(j, (f2, f4, f8, f16))[:tr]
        o_ref[...] = ALPHA * du_ref[...] + sel - dpv

    return pl.pallas_call(
        body, name=name, grid=(T // tr, len(POOL_WINDOWS)),
        in_specs=[pl.BlockSpec((tr, gd), lambda i, j: (i, j)),
                  pl.BlockSpec((POOL_HALO, gd), lambda i, j: (jnp.minimum((i + 1) * hb, last_h), j)),
                  pl.BlockSpec((tr, gd), lambda i, j: (i, j))],
        out_specs=pl.BlockSpec((tr, gd), lambda i, j: (i, j)),
        out_shape=jax.ShapeDtypeStruct((T, D), F32),
        compiler_params=_cp("parallel", "parallel"),
    )(dp, dp, du)


def _conv_taps(cat, wv, rows):
    shifted = [cat[CONV_HALO:] if s == 0 else pltpu.roll(cat, s, 0)[CONV_HALO:] for s in range(CONV_K)]
    acc = shifted[0] * wv[CONV_K - 1:CONV_K]
    for s in range(1, CONV_K):
        acc = acc + shifted[s] * wv[CONV_K - 1 - s:CONV_K - s]
    return acc, shifted


def _conv_fwd(name, xp, w, b):
    T, CD = xp.shape
    tr, tc = _pick(T, 512), _pick(CD, 512)
    hb = tr // CONV_HALO

    def body(x_ref, h_ref, w_ref, b_ref, o_ref):
        i = pl.program_id(0)
        halo = jnp.where(i > 0, h_ref[...], 0.0)
        cat = jnp.concatenate([halo, x_ref[...]], axis=0)
        acc, _ = _conv_taps(cat, w_ref[...], tr)
        acc = acc + b_ref[...]
        o_ref[...] = acc * _sigmoid(acc)

    return pl.pallas_call(
        body, name=name, grid=(T // tr, CD // tc),
        in_specs=[pl.BlockSpec((tr, tc), lambda i, j: (i, j)),
                  pl.BlockSpec((CONV_HALO, tc), lambda i, j: (jnp.maximum(i * hb - 1, 0), j)),
                  pl.BlockSpec((CONV_K, tc), lambda i, j: (0, j)),
                  pl.BlockSpec((1, tc), lambda i, j: (0, j))],
        out_specs=pl.BlockSpec((tr, tc), lambda i, j: (i, j)),
        out_shape=jax.ShapeDtypeStruct((T, CD), F32),
        compiler_params=_cp("parallel", "parallel"),
    )(xp, xp, w, b)


def _conv_bwd_a(name, dxs, dbm, dcm, xp, w, b):
    T, CD = xp.shape
    tr = _pick(T, 512)
    tc = _pick(dbm.shape[1], 512)
    hb = tr // CONV_HALO
    nx, nb = dxs.shape[1] // tc, dbm.shape[1] // tc

    def part_spec(lo, n):
        def imap(j, i):
            inside = jnp.logical_and(j >= lo, j < lo + n)
            return (jnp.where(inside, i, 0), jnp.clip(j - lo, 0, n - 1))
        return pl.BlockSpec((tr, tc), imap)

    def body(dx_ref, db_ref, dc_ref, x_ref, h_ref, w_ref, b_ref, o_ref, dw_ref, dbias_ref):
        j, i = pl.program_id(0), pl.program_id(1)
        first = i == 0
        d = jnp.where(j < nx, dx_ref[...], jnp.where(j < nx + nb, db_ref[...], dc_ref[...]))
        halo = jnp.where(i > 0, h_ref[...], 0.0)
        cat = jnp.concatenate([halo, x_ref[...]], axis=0)
        acc, shifted = _conv_taps(cat, w_ref[...], tr)
        acc = acc + b_ref[...]
        sg = _sigmoid(acc)
        dconv = d * (sg * (1.0 + acc * (1.0 - sg)))
        o_ref[...] = dconv
        _accum(dbias_ref, jnp.sum(dconv, axis=0, keepdims=True), first)
        tap = lax.broadcasted_iota(jnp.int32, (CONV_K, tc), 0)
        dwv = jnp.zeros((CONV_K, tc), F32)
        for s in range(CONV_K):
            dwv = jnp.where(tap == CONV_K - 1 - s, jnp.sum(dconv * shifted[s], axis=0, keepdims=True), dwv)
        _accum(dw_ref, dwv, first)

    return pl.pallas_call(
        body, name=name, grid=(CD // tc, T // tr),
        in_specs=[part_spec(0, nx), part_spec(nx, nb), part_spec(nx + nb, nb),
                  pl.BlockSpec((tr, tc), lambda j, i: (i, j)),
                  pl.BlockSpec((CONV_HALO, tc), lambda j, i: (jnp.maximum(i * hb - 1, 0), j)),
                  pl.BlockSpec((CONV_K, tc), lambda j, i: (0, j)),
                  pl.BlockSpec((1, tc), lambda j, i: (0, j))],
        out_specs=[pl.BlockSpec((tr, tc), lambda j, i: (i, j)),
                   pl.BlockSpec((CONV_K, tc), lambda j, i: (0, j)),
                   pl.BlockSpec((1, tc), lambda j, i: (0, j))],
        out_shape=[jax.ShapeDtypeStruct((T, CD), F32), jax.ShapeDtypeStruct((CONV_K, CD), F32),
                   jax.ShapeDtypeStruct((1, CD), F32)],
        compiler_params=_cp("parallel", "arbitrary"),
    )(dxs, dbm, dcm, xp, xp, w, b)


def _conv_bwd_b(name, dconv, w):
    T, CD = dconv.shape
    tr, tc = _pick(T, 512), _pick(CD, 512)
    hb = tr // CONV_HALO
    last_h = T // CONV_HALO - 1
    n = tr + CONV_HALO

    def body(d_ref, h_ref, w_ref, o_ref):
        i = pl.program_id(0)
        halo = jnp.where(i < pl.num_programs(0) - 1, h_ref[...], 0.0)
        cat = jnp.concatenate([d_ref[...], halo], axis=0)
        wv = w_ref[...]
        acc = cat[:tr] * wv[CONV_K - 1:CONV_K]
        for s in range(1, CONV_K):
            acc = acc + pltpu.roll(cat, n - s, 0)[:tr] * wv[CONV_K - 1 - s:CONV_K - s]
        o_ref[...] = acc.astype(BF16)

    return pl.pallas_call(
        body, name=name, grid=(T // tr, CD // tc),
        in_specs=[pl.BlockSpec((tr, tc), lambda i, j: (i, j)),
                  pl.BlockSpec((CONV_HALO, tc), lambda i, j: (jnp.minimum((i + 1) * hb, last_h), j)),
                  pl.BlockSpec((CONV_K, tc), lambda i, j: (0, j))],
        out_specs=pl.BlockSpec((tr, tc), lambda i, j: (i, j)),
        out_shape=jax.ShapeDtypeStruct((T, CD), BF16),
        compiler_params=_cp("parallel", "parallel"),
    )(dconv, dconv, w)


def _dt_fwd(name, dtp, bias):
    T, H = dtp.shape

    def body(x_ref, b_ref, o_ref):
        v = x_ref[...] + b_ref[...]
        u = jnp.exp(-jnp.abs(v))
        w1 = 1.0 + u
        lp = jnp.where(w1 == 1.0, u, jnp.log(w1) * (u / jnp.where(w1 == 1.0, 1.0, w1 - 1.0)))
        o_ref[...] = jnp.maximum(v, 0.0) + lp

    return pl.pallas_call(body, name=name, out_shape=jax.ShapeDtypeStruct((T, H), F32))(dtp, bias)


def _dt_bwd(name, ddt, dtp, bias):
    T, H = dtp.shape

    def body(d_ref, x_ref, b_ref, o_ref, ob_ref, db_ref):
        g = d_ref[...] * _sigmoid(x_ref[...] + b_ref[...])
        o_ref[...] = g
        ob_ref[...] = g.astype(BF16)
        db_ref[...] = jnp.sum(g, axis=0, keepdims=True)

    return pl.pallas_call(
        body, name=name,
        out_shape=[jax.ShapeDtypeStruct((T, H), F32), jax.ShapeDtypeStruct((T, H), BF16),
                   jax.ShapeDtypeStruct((1, H), F32)],
    )(ddt, dtp, bias)


def _split(v):
    hi = v.astype(BF16)
    return hi, (v - hi.astype(F32)).astype(BF16)


def _dot01(form, a, b, mask):
    if mask == "b":
        hi, lo = _split(a)
        mb = b.astype(BF16)
        return (lax.dot_general(hi, mb, _DIMS[form], preferred_element_type=F32)
                + lax.dot_general(lo, mb, _DIMS[form], preferred_element_type=F32))
    hi, lo = _split(b)
    ma = a.astype(BF16)
    return (lax.dot_general(ma, hi, _DIMS[form], preferred_element_type=F32)
            + lax.dot_general(ma, lo, _DIMS[form], preferred_element_type=F32))


def _dotb(form, a, b):
    return lax.dot_general(a.astype(BF16), b.astype(BF16), _DIMS[form], preferred_element_type=F32)


def _ssd_common(dtc, dtr, alr, alc, gw):
    li = lax.broadcasted_iota(jnp.int32, (CHUNK, CHUNK), 0)
    si = lax.broadcasted_iota(jnp.int32, (CHUNK, CHUNK), 1)
    tri = (li >= si).astype(F32)
    trit = (li <= si).astype(F32)
    a_row = -jnp.exp(alr)
    a_col = -jnp.exp(alc)
    acs_c = _dot01("nn", tri, dtc * a_row, "a")
    acs_r = _dot01("nn", dtr * a_col, trit, "b")
    eh = lax.broadcasted_iota(jnp.int32, (HEAD_PAD, gw), 0)
    ec = lax.broadcasted_iota(jnp.int32, (HEAD_PAD, gw), 1) // HEAD_DIM
    expand = (eh == ec).astype(F32)
    th = lax.broadcasted_iota(jnp.int32, (gw, HEAD_PAD), 1)
    tc = lax.broadcasted_iota(jnp.int32, (gw, HEAD_PAD), 0) // HEAD_DIM
    reduce_ = (th == tc).astype(F32)
    acs_last = acs_c[CHUNK - 1:CHUNK, :]
    both = _dot01("nn", jnp.concatenate([dtc, acs_c], axis=0), expand, "b")
    acs_e = both[CHUNK:]
    return dict(li=li, si=si, tri=tri, trit=trit, a_row=a_row, acs_c=acs_c, acs_r=acs_r,
                reduce=reduce_, acs_last=acs_last, dt_e=both[:CHUNK],
                eacs_e=jnp.exp(acs_e), dec_e=jnp.exp(acs_e[CHUNK - 1:CHUNK, :] - acs_e),
                dec_h=jnp.exp(acs_last - acs_c))


def _ssd_specs(T, DI, gw, nc, rev):
    nsb = DI // D_STATE
    cidx = (lambda c: nc - 1 - c) if rev else (lambda c: c)
    return dict(
        xs=pl.BlockSpec((CHUNK, gw), lambda g, c: (cidx(c), g)),
        bm=pl.BlockSpec((CHUNK, D_STATE), lambda g, c: (cidx(c), nsb + g)),
        cm=pl.BlockSpec((CHUNK, D_STATE), lambda g, c: (cidx(c), nsb + N_GROUPS + g)),
        dtc=pl.BlockSpec((None, CHUNK, HEAD_PAD), lambda g, c: (g, cidx(c), 0)),
        dtr=pl.BlockSpec((None, ROW_PAD, CHUNK), lambda g, c: (g, 0, cidx(c))),
        alr=pl.BlockSpec((None, 1, HEAD_PAD), lambda g, c: (g, 0, 0)),
        alc=pl.BlockSpec((None, ROW_PAD, 1), lambda g, c: (g, 0, 0)),
        de=pl.BlockSpec((None, 1, gw), lambda g, c: (g, 0, 0)),
        hp=pl.BlockSpec((None, None, gw, D_STATE), lambda g, c: (cidx(c), g, 0, 0)),
        bc=pl.BlockSpec((CHUNK, D_STATE), lambda g, c: (cidx(c), g)),
        acc=pl.BlockSpec((None, 1, HEAD_PAD), lambda g, c: (g, 0, 0)),
    )


def _ssd_fwd(name, xbc, dtc, dtr, alr, alc, d_e, DI, deps=()):
    T = xbc.shape[0]
    nc = T // CHUNK
    gw = DI // N_GROUPS
    sp = _ssd_specs(T, DI, gw, nc, False)

    def body(xs_ref, b_ref, c_ref, dtc_ref, dtr_ref, alr_ref, alc_ref, de_ref, *rest):
        y_ref, hp_ref, h_scr = rest[len(deps):]

        @pl.when(pl.program_id(1) == 0)
        def _():
            h_scr[...] = jnp.zeros_like(h_scr)

        hpv = h_scr[...]
        hp_ref[...] = hpv
        xs = xs_ref[...]
        bb = b_ref[...].astype(BF16)
        cb_ = c_ref[...].astype(BF16)
        cm = _ssd_common(dtc_ref[...], dtr_ref[...], alr_ref[...], alc_ref[...], gw)
        x = xs * cm["dt_e"]
        xb = x.astype(BF16)
        cbm = _dotb("nt", cb_, bb)
        causal = cm["li"] >= cm["si"]
        second = lax.broadcasted_iota(jnp.int32, (1, HEAD_PAIR), 1) >= HEAD_DIM
        pieces = []
        for pr in range(gw // HEAD_PAIR):
            xp = xb[:, pr * HEAD_PAIR:(pr + 1) * HEAD_PAIR]
            for h2 in range(2):
                j = 2 * pr + h2
                seg = cm["acs_c"][:, j:j + 1] - cm["acs_r"][j:j + 1, :]
                lmat = jnp.exp(jnp.where(causal, jnp.minimum(seg, 0.0), -1e30))
                yj = _dotb("nn", cbm * lmat, xp)
                yp = yj if h2 == 0 else jnp.where(second, yj, yp)
            pieces.append(yp)
        ydiag = pieces[0] if len(pieces) == 1 else jnp.concatenate(pieces, axis=1)
        states = _dotb("tn", x * cm["dec_e"], bb)
        yoff = _dotb("nt", cb_, hpv) * cm["eacs_e"]
        y_ref[...] = ydiag + yoff + xs * de_ref[...]
        cd_rows = jnp.sum(cm["reduce"] * jnp.exp(cm["acs_last"]), axis=1, keepdims=True)
        h_scr[...] = cd_rows * hpv + states

    return pl.pallas_call(
        body, name=name, grid=(N_GROUPS, nc),
        in_specs=[sp["xs"], sp["bm"], sp["cm"], sp["dtc"], sp["dtr"], sp["alr"], sp["alc"], sp["de"]]
        + [pl.BlockSpec((8, 128), lambda g, c: (0, 0)) for _ in deps],
        out_specs=[sp["xs"], sp["hp"]],
        out_shape=[jax.ShapeDtypeStruct((T, DI), F32), jax.ShapeDtypeStruct((nc, N_GROUPS, gw, D_STATE), F32)],
        scratch_shapes=[pltpu.VMEM((gw, D_STATE), F32)],
        compiler_params=_cp("parallel", "arbitrary"),
    )(xbc, xbc, xbc, dtc, dtr, alr, alc, d_e, *deps)


def _ssd_bwd(name, dy, xbc, dtc, dtr, alr, alc, d_e, hprev, DI):
    T = xbc.shape[0]
    nc = T // CHUNK
    gw = DI // N_GROUPS
    sp = _ssd_specs(T, DI, gw, nc, True)

    def body(dy_ref, xs_ref, b_ref, c_ref, dtc_ref, dtr_ref, alr_ref, alc_ref, de_ref, hp_ref,
             dxs_ref, db_ref, dc_ref, ddt_ref, dal_ref, dd_ref, dh_scr):
        first = pl.program_id(1) == 0

        @pl.when(first)
        def _():
            dh_scr[...] = jnp.zeros_like(dh_scr)

        xs = xs_ref[...]
        dyv = dy_ref[...]
        bb = b_ref[...].astype(BF16)
        cb_ = c_ref[...].astype(BF16)
        dtc_v = dtc_ref[...]
        cm = _ssd_common(dtc_v, dtr_ref[...], alr_ref[...], alc_ref[...], gw)
        hpv = hp_ref[...]
        hpb = hpv.astype(BF16)
        dhn = dh_scr[...]
        dhnb = dhn.astype(BF16)
        x = xs * cm["dt_e"]
        xb = x.astype(BF16)
        cbm = _dotb("nt", cb_, bb)
        causal = cm["li"] >= cm["si"]
        second = lax.broadcasted_iota(jnp.int32, (1, HEAD_PAIR), 1) >= HEAD_DIM
        lane_pad = lax.broadcasted_iota(jnp.int32, (1, HEAD_PAD), 1)
        sub_pad = lax.broadcasted_iota(jnp.int32, (ROW_PAD, 1), 0)

        dxs = dyv * de_ref[...]

        dcb = jnp.zeros((CHUNK, CHUNK), F32)
        dacs_c = jnp.zeros((CHUNK, HEAD_PAD), F32)
        dacs_r = jnp.zeros((ROW_PAD, CHUNK), F32)
        pieces = []
        for pr in range(gw // HEAD_PAIR):
            xp = xb[:, pr * HEAD_PAIR:(pr + 1) * HEAD_PAIR]
            dyp = dyv[:, pr * HEAD_PAIR:(pr + 1) * HEAD_PAIR]
            for h2 in range(2):
                j = 2 * pr + h2
                seg = cm["acs_c"][:, j:j + 1] - cm["acs_r"][j:j + 1, :]
                lmat = jnp.exp(jnp.where(causal, jnp.minimum(seg, 0.0), -1e30))
                mmat = cbm * lmat
                dym = jnp.where(second if h2 == 1 else jnp.logical_not(second), dyp, 0.0).astype(BF16)
                dm = _dotb("nt", dym, xp)
                dxj = _dotb("tn", mmat, dym)
                dxp = dxj if h2 == 0 else dxp + dxj
                dcb = dcb + dm * lmat
                dseg = dm * mmat
                dacs_c = dacs_c + jnp.where(lane_pad == j, jnp.sum(dseg, axis=1, keepdims=True), 0.0)
                dacs_r = dacs_r - jnp.where(sub_pad == j, jnp.sum(dseg, axis=0, keepdims=True), 0.0)
            pieces.append(dxp)
        dx = pieces[0] if len(pieces) == 1 else jnp.concatenate(pieces, axis=1)
        dc = _dotb("nn", dcb, bb)
        db = _dotb("tn", dcb, cb_)

        gm = _dotb("nt", cb_, hpb)
        dgm = dyv * cm["eacs_e"]
        dc = dc + _dotb("nn", dgm, hpb)
        dhp = _dotb("tn", dgm, cb_)

        cd_row = jnp.exp(cm["acs_last"])
        cd_rows = jnp.sum(cm["reduce"] * cd_row, axis=1, keepdims=True)
        dhp = dhp + cd_rows * dhn
        rsum = jnp.sum(dhn * hpv, axis=1, keepdims=True)
        dacs_last = jnp.sum(cm["reduce"] * rsum, axis=0, keepdims=True) * cd_row
        dec_e = cm["dec_e"]
        xdec = x * dec_e
        dxdec = _dotb("nt", bb, dhnb)
        db = db + _dotb("nn", xdec, dhnb)
        dx = dx + dxdec * dec_e
        sums = _dot01("nn", jnp.concatenate([dyv * xs, dgm * gm, dxdec * x, dx * xs], axis=0), cm["reduce"], "b")
        _accum(dd_ref, jnp.sum(sums[:CHUNK], axis=0, keepdims=True), first)
        tdec = sums[2 * CHUNK:3 * CHUNK] * cm["dec_h"]
        dacs_c = dacs_c + sums[CHUNK:2 * CHUNK] - tdec
        dacs_last = dacs_last + jnp.sum(tdec, axis=0, keepdims=True)
        row_id = lax.broadcasted_iota(jnp.int32, (CHUNK, 1), 0)
        dacs_c = dacs_c + jnp.where(row_id == CHUNK - 1, dacs_last, 0.0)

        dxs_ref[...] = dxs + dx * cm["dt_e"]
        ddt = sums[3 * CHUNK:]
        dda = _dot01("nn", cm["trit"], dacs_c, "a")
        dda_r = _dot01("nn", dacs_r, cm["tri"], "b")
        dda_rp = jnp.concatenate([dda_r, jnp.zeros((HEAD_PAD - ROW_PAD, CHUNK), F32)], axis=0)
        eye = (cm["li"] == cm["si"]).astype(F32)
        dda = dda + _dot01("nt", eye, dda_rp, "a")
        ddt_ref[...] = ddt + dda * cm["a_row"]
        _accum(dal_ref, jnp.sum(dda * dtc_v, axis=0, keepdims=True) * cm["a_row"], first)
        db_ref[...] = db
        dc_ref[...] = dc
        dh_scr[...] = dhp

    gs = D_STATE * N_GROUPS
    return pl.pallas_call(
        body, name=name, grid=(N_GROUPS, nc),
        in_specs=[sp["xs"], sp["xs"], sp["bm"], sp["cm"], sp["dtc"], sp["dtr"], sp["alr"], sp["alc"],
                  sp["de"], sp["hp"]],
        out_specs=[sp["xs"], sp["bc"], sp["bc"], sp["dtc"], sp["acc"], sp["acc"]],
        out_shape=[jax.ShapeDtypeStruct((T, DI), F32), jax.ShapeDtypeStruct((T, gs), F32),
                   jax.ShapeDtypeStruct((T, gs), F32), jax.ShapeDtypeStruct((N_GROUPS, T, HEAD_PAD), F32),
                   jax.ShapeDtypeStruct((N_GROUPS, 1, HEAD_PAD), F32),
                   jax.ShapeDtypeStruct((N_GROUPS, 1, HEAD_PAD), F32)],
        scratch_shapes=[pltpu.VMEM((gw, D_STATE), F32)],
        compiler_params=_cp("parallel", "arbitrary"),
    )(dy, xbc, xbc, xbc, dtc, dtr, alr, alc, d_e, hprev)


def _adam_math(w, g, m, v):
    m = ADAM_B1 * m + (1.0 - ADAM_B1) * g
    v = ADAM_B2 * v + (1.0 - ADAM_B2) * (g * g)
    m_hat = m / (1.0 - ADAM_B1 ** ADAM_STEP)
    v_hat = v / (1.0 - ADAM_B2 ** ADAM_STEP)
    delta = -ADAM_LR * (m_hat / (jnp.sqrt(v_hat) + ADAM_EPS) + ADAM_WD * w)
    return delta, m, v


def _adamw(name, w, m, v, g, layer=0, prev=None):
    L, R, C = w.shape
    tr = _rows(R, C * 4, ADAM_BLOCK_BYTES)
    tc = C
    while tr * tc * 4 > ADAM_BLOCK_BYTES and tc % 256 == 0:
        tc //= 2
    blk = pl.BlockSpec((None, tr, tc), lambda i, j: (layer, i, j))
    prev = [] if prev is None else list(prev)

    def body(w_ref, m_ref, v_ref, g_ref, *rest):
        go_ref, d_ref, mo_ref, vo_ref = rest[len(prev):]
        gv = g_ref[...]
        delta, mn, vn = _adam_math(w_ref[...], gv, m_ref[...], v_ref[...])
        go_ref[...] = gv
        d_ref[...] = delta
        mo_ref[...] = mn
        vo_ref[...] = vn

    return pl.pallas_call(
        body, name=name, grid=(R // tr, C // tc),
        in_specs=[blk] * 3 + [pl.BlockSpec((tr, tc), lambda i, j: (i, j))] + [_ANY] * len(prev),
        out_specs=[blk] * 4, out_shape=[jax.ShapeDtypeStruct((L, R, C), F32)] * 4,
        input_output_aliases={4 + k: k for k in range(len(prev))},
        compiler_params=_cp("parallel", "parallel"),
    )(w, m, v, g, *prev)


def _place():
    x, y, c = lax.axis_index("x"), lax.axis_index("y"), lax.axis_index("c")
    chips = [(1 - x, y), (x, 1 - y), (1 - x, 1 - y)]
    return x, y, c, chips


def _allgather_small(name, v, after=(), reduce=False):
    R, C = v.shape
    after = list(after)

    def body(v_ref, *rest):
        o_ref = rest[len(after)]
        send_sems, recv_sems = rest[-2:]
        x, y, c, _ = _place()
        me = 4 * x + 2 * y + c
        o_ref[me] = v_ref[...]
        copies = []
        for k in range(1, N_DEV):
            px, py, pc = x ^ (k >> 2), y ^ ((k >> 1) & 1), c ^ (k & 1)
            copies.append(pltpu.make_async_remote_copy(
                src_ref=v_ref, dst_ref=o_ref.at[me], send_sem=send_sems.at[k - 1], recv_sem=recv_sems.at[k - 1],
                device_id=(px, py, pc), device_id_type=MESH))
        for cp in copies:
            cp.start()
        for cp in copies:
            cp.wait()
        if reduce:
            acc = o_ref[0]
            for d in range(1, N_DEV):
                acc = acc + o_ref[d]
            rest[len(after) + 1][...] = acc

    vmem = pl.BlockSpec(memory_space=pltpu.VMEM)
    gathered = jax.ShapeDtypeStruct((N_DEV, R, C), F32)
    return pl.pallas_call(
        body, name=name,
        out_shape=(gathered, jax.ShapeDtypeStruct((R, C), F32)) if reduce else gathered,
        in_specs=[vmem] + [_ANY] * len(after), out_specs=(vmem, vmem) if reduce else vmem,
        scratch_shapes=[pltpu.SemaphoreType.DMA((N_DEV - 1,)), pltpu.SemaphoreType.DMA((N_DEV - 1,))],
    )(v, *after)


_HBM = pl.BlockSpec(memory_space=pltpu.HBM)
_SEM = pl.BlockSpec(memory_space=pltpu.SEMAPHORE)
_VMEM = pl.BlockSpec(memory_space=pltpu.VMEM)
_EFFECT = pltpu.SideEffectType.DATAFLOW_SIDE_EFFECTING


def _in_hbm(v):
    return pltpu.with_memory_space_constraint(v, pltpu.HBM)


def _remote(src, dst, send_sem, recv_sem, device):
    return pltpu.make_async_remote_copy(src_ref=src, dst_ref=dst, send_sem=send_sem, recv_sem=recv_sem,
                                        device_id=device, device_id_type=MESH)


def _gather_copies(src, land, ici_s, ici_r, own_s, own_r, arrivals=True):
    x, y, c, chips = _place()
    a = 2 * x + y
    sends, lands = [], []
    for i in range(len(src)):
        own = _remote(src[i], land[i].at[a], own_s.at[i], own_r.at[i], (x, y, 1 - c))
        sends.append(own)
        if arrivals:
            lands.append(own)
        for k, (px, py) in enumerate(chips):
            s, r = ici_s.at[3 * i + k], ici_r.at[3 * i + k]
            sends.append(_remote(src[i].at[c], land[i].at[a, c], s, r, (px, py, c)))
            if arrivals:
                lands.append(_remote(src[i].at[c], land[i].at[2 * px + py, c], s, r, (px, py, c)))
    return sends, lands


def _gather_start(name, shards, after=None):
    n = len(shards)
    lands = [lax.empty((N_CHIPS,) + s.shape, s.dtype) for s in shards]
    n_in = 2 * n + (0 if after is None else 1)

    def body(*refs):
        src, land = refs[:n], refs[n:2 * n]
        sems = refs[n_in:n_in + 4]
        token = refs[-1]
        sends, _ = _gather_copies(src, land, *sems, arrivals=False)
        for cp in sends:
            cp.start()
        token[...] = jnp.zeros_like(token)

    outs = pl.pallas_call(
        body, name=name,
        out_shape=(pltpu.SemaphoreType.DMA((3 * n,)), pltpu.SemaphoreType.DMA((3 * n,)),
                   pltpu.SemaphoreType.DMA((n,)), pltpu.SemaphoreType.DMA((n,)),
                   *[pltpu.HBM(s.shape, s.dtype) for s in shards], *[pltpu.HBM(l.shape, l.dtype) for l in lands],
                   jax.ShapeDtypeStruct((8, 128), F32)),
        in_specs=[_HBM] * (2 * n) + ([] if after is None else [_ANY]),
        out_specs=(_SEM,) * 4 + (_HBM,) * (2 * n) + (_VMEM,),
        input_output_aliases={i: 4 + i for i in range(2 * n)},
        compiler_params=pltpu.CompilerParams(has_side_effects=_EFFECT),
    )(*[_in_hbm(s) for s in shards], *[_in_hbm(l) for l in lands], *([] if after is None else [after]))
    return dict(sems=outs[:4], src=outs[4:4 + n], land=outs[4 + n:4 + 2 * n], token=outs[-1])


def _gather_wait(name, started, after):
    n = len(started["src"])
    after = list(after)

    def body(*refs):
        src, land = refs[:n], refs[n:2 * n]
        sems = refs[2 * n:2 * n + 4]
        sends, lands = _gather_copies(src, land, *sems)
        for cp in sends:
            cp.wait_send()
        for cp in lands:
            cp.wait_recv()

    outs = pl.pallas_call(
        body, name=name,
        out_shape=[pltpu.HBM(v.shape, v.dtype) for v in list(started["src"]) + list(started["land"])],
        in_specs=[_HBM] * (2 * n) + [_SEM] * 4 + [_ANY] * len(after), out_specs=[_HBM] * (2 * n),
        input_output_aliases={i: i for i in range(2 * n)},
        compiler_params=pltpu.CompilerParams(has_side_effects=_EFFECT),
    )(*started["src"], *started["land"], *started["sems"], *after)
    return outs[n:]


def _gather_forward(name, lands):
    n = len(lands)

    def body(*refs):
        buf = refs[n:2 * n]
        send_sems, recv_sems = refs[2 * n:]
        x, y, c, chips = _place()
        sends, lands_ = [], []
        for i in range(n):
            for k, (px, py) in enumerate(chips):
                b = 2 * px + py
                sends.append(_remote(buf[i].at[b, c], buf[i].at[b, c], send_sems.at[i, k], recv_sems.at[i, k],
                                     (x, y, 1 - c)))
                lands_.append(_remote(buf[i].at[b, 1 - c], buf[i].at[b, 1 - c], send_sems.at[i, k],
                                      recv_sems.at[i, k], (x, y, 1 - c)))
        for cp in sends:
            cp.start()
        for cp in sends:
            cp.wait_send()
        for cp in lands_:
            cp.wait_recv()

    return pl.pallas_call(
        body, name=name, in_specs=[_ANY] * n, out_specs=[_ANY] * n,
        out_shape=[jax.ShapeDtypeStruct(l.shape, l.dtype) for l in lands],
        input_output_aliases={i: i for i in range(n)},
        scratch_shapes=[pltpu.SemaphoreType.DMA((n, 3)), pltpu.SemaphoreType.DMA((n, 3))],
    )(*lands)


def _split_start(name, bufs, copies, n_sems, after=()):
    nb = len(bufs)
    after = list(after)

    def body(*refs):
        send_sems, recv_sems = refs[nb + len(after):nb + len(after) + 2]
        token = refs[-1]
        sends, _ = copies(refs[:nb], send_sems, recv_sems, False)
        for cp in sends:
            cp.start()
        token[...] = jnp.zeros_like(token)

    outs = pl.pallas_call(
        body, name=name,
        out_shape=(pltpu.SemaphoreType.DMA((n_sems,)), pltpu.SemaphoreType.DMA((n_sems,)),
                   *[pltpu.HBM(b.shape, b.dtype) for b in bufs], jax.ShapeDtypeStruct((8, 128), F32)),
        in_specs=[_HBM] * nb + [_ANY] * len(after), out_specs=(_SEM,) * 2 + (_HBM,) * nb + (_VMEM,),
        input_output_aliases={i: 2 + i for i in range(nb)},
        compiler_params=pltpu.CompilerParams(has_side_effects=_EFFECT),
    )(*[_in_hbm(b) for b in bufs], *after)
    return dict(sems=outs[:2], bufs=outs[2:2 + nb], token=outs[-1])


def _split_wait(name, started, copies, after):
    nb = len(started["bufs"])
    after = list(after)

    def body(*refs):
        sends, arrivals = copies(refs[:nb], refs[nb], refs[nb + 1], True)
        for cp in sends:
            cp.wait_send()
        for cp in arrivals:
            cp.wait_recv()

    return pl.pallas_call(
        body, name=name, out_shape=[pltpu.HBM(b.shape, b.dtype) for b in started["bufs"]],
        in_specs=[_HBM] * nb + [_SEM] * 2 + [_ANY] * len(after), out_specs=[_HBM] * nb,
        input_output_aliases={i: i for i in range(nb)},
        compiler_params=pltpu.CompilerParams(has_side_effects=_EFFECT),
    )(*started["bufs"], *started["sems"], *after)


def _forward_copies(bufs, send_sems, recv_sems, arrivals):
    x, y, c, chips = _place()
    sends, lands = [], []
    for i, buf in enumerate(bufs):
        for k, (px, py) in enumerate(chips):
            b, s, r = 2 * px + py, send_sems.at[3 * i + k], recv_sems.at[3 * i + k]
            sends.append(_remote(buf.at[b, c], buf.at[b, c], s, r, (x, y, 1 - c)))
            if arrivals:
                lands.append(_remote(buf.at[b, 1 - c], buf.at[b, 1 - c], s, r, (x, y, 1 - c)))
    return sends, lands


def _pair_copies(bufs, send_sems, recv_sems, arrivals):
    x, y, c, _ = _place()
    n = len(bufs) // 2
    copies = [_remote(bufs[i].at[:, 1 - c], bufs[n + i], send_sems.at[i], recv_sems.at[i], (x, y, 1 - c))
              for i in range(n)]
    return copies, copies


def _chip_copies(src, land, send_sems, recv_sems):
    x, y, c, chips = _place()
    return [_remote(src[i].at[2 * px + py], land[i].at[k], send_sems.at[3 * i + k], recv_sems.at[3 * i + k],
                    (px, py, c))
            for i in range(len(src)) for k, (px, py) in enumerate(chips)]


def _chip_start(name, parts):
    n = len(parts)
    lands = [lax.empty((3,) + p.shape[1:], p.dtype) for p in parts]

    def body(*refs):
        src, land = refs[:n], refs[n:2 * n]
        send_sems, recv_sems = refs[2 * n:2 * n + 2]
        token = refs[-1]
        for cp in _chip_copies(src, land, send_sems, recv_sems):
            cp.start()
        token[...] = jnp.zeros_like(token)

    outs = pl.pallas_call(
        body, name=name,
        out_shape=(pltpu.SemaphoreType.DMA((3 * n,)), pltpu.SemaphoreType.DMA((3 * n,)),
                   *[pltpu.HBM(p.shape, p.dtype) for p in parts], *[pltpu.HBM(l.shape, l.dtype) for l in lands],
                   jax.ShapeDtypeStruct((8, 128), F32)),
        in_specs=[_HBM] * (2 * n), out_specs=(_SEM,) * 2 + (_HBM,) * (2 * n) + (_VMEM,),
        input_output_aliases={i: 2 + i for i in range(2 * n)},
        compiler_params=pltpu.CompilerParams(has_side_effects=_EFFECT),
    )(*[_in_hbm(p) for p in parts], *[_in_hbm(l) for l in lands])
    return dict(sems=outs[:2], src=outs[2:2 + n], land=outs[2 + n:2 + 2 * n], token=outs[-1])


def _chip_wait(name, started, after):
    n = len(started["src"])
    after = list(after)

    def body(*refs):
        src, land = refs[:n], refs[n:2 * n]
        send_sems, recv_sems = refs[2 * n:2 * n + 2]
        copies = _chip_copies(src, land, send_sems, recv_sems)
        for cp in copies:
            cp.wait_send()
        for cp in copies:
            cp.wait_recv()

    outs = pl.pallas_call(
        body, name=name,
        out_shape=[pltpu.HBM(v.shape, v.dtype) for v in list(started["src"]) + list(started["land"])],
        in_specs=[_HBM] * (2 * n) + [_SEM] * 2 + [_ANY] * len(after), out_specs=[_HBM] * (2 * n),
        input_output_aliases={i: i for i in range(2 * n)},
        compiler_params=pltpu.CompilerParams(has_side_effects=_EFFECT),
    )(*started["src"], *started["land"], *started["sems"], *after)
    return outs[:n], outs[n:]


def _half_exchange(name, shards):
    n = len(shards)

    def body(*refs):
        buf = refs[n:2 * n]
        send_sems, recv_sems = refs[2 * n:]
        x, y, c, _ = _place()
        copies = [pltpu.make_async_remote_copy(
            src_ref=buf[i].at[c], dst_ref=buf[i].at[c], send_sem=send_sems.at[i], recv_sem=recv_sems.at[i],
            device_id=(x, y, 1 - c), device_id_type=MESH) for i in range(n)]
        for cp in copies:
            cp.start()
        for cp in copies:
            cp.wait()

    return pl.pallas_call(
        body, name=name, in_specs=[_ANY] * n, out_specs=[_ANY] * n,
        out_shape=[jax.ShapeDtypeStruct(s.shape, s.dtype) for s in shards],
        input_output_aliases={i: i for i in range(n)},
        scratch_shapes=[pltpu.SemaphoreType.DMA((n,)), pltpu.SemaphoreType.DMA((n,))],
    )(*shards)


def _pair_add(name, pos, part, sib):
    Q, _, R2, C = part.shape
    tr = _pick(R2, 256)

    def body(pos_ref, p_ref, s_ref, o_ref):
        o_ref[...] = (p_ref[...].astype(F32) + s_ref[...].astype(F32)).astype(BF16)

    return pl.pallas_call(
        body, name=name,
        grid_spec=pltpu.PrefetchScalarGridSpec(
            num_scalar_prefetch=1, grid=(Q, R2 // tr),
            in_specs=[pl.BlockSpec((None, None, tr, C), lambda q, i, pos_ref: (q, pos_ref[1], i, 0)),
                      pl.BlockSpec((None, tr, C), lambda q, i, pos_ref: (q, i, 0))],
            out_specs=pl.BlockSpec((None, tr, C), lambda q, i, pos_ref: (q, i, 0))),
        out_shape=jax.ShapeDtypeStruct((Q, R2, C), BF16),
        compiler_params=_cp("parallel", "parallel"),
    )(pos, part, sib)


def _chip_sum(name, pos, own, got):
    _, R2, C = own.shape
    tr = _pick(R2, 256)

    def body(pos_ref, o_ref, g_ref, out_ref):
        acc = o_ref[...].astype(F32)
        for k in range(3):
            acc = acc + g_ref[k].astype(F32)
        out_ref[...] = acc

    return pl.pallas_call(
        body, name=name,
        grid_spec=pltpu.PrefetchScalarGridSpec(
            num_scalar_prefetch=1, grid=(R2 // tr,),
            in_specs=[pl.BlockSpec((None, tr, C), lambda i, pos_ref: (pos_ref[0], i, 0)),
                      pl.BlockSpec((3, tr, C), lambda i, pos_ref: (0, i, 0))],
            out_specs=pl.BlockSpec((None, tr, C), lambda i, pos_ref: (pos_ref[1], i, 0))),
        out_shape=jax.ShapeDtypeStruct((2, R2, C), F32),
        compiler_params=_cp("parallel"),
    )(pos, own, got)


def _reduce_pair_start(tag, parts):
    split = [p.reshape(p.shape[0], 2, p.shape[1] // 2, p.shape[2]) for p in parts]
    lands = [lax.empty((p.shape[0],) + p.shape[2:], p.dtype) for p in split]
    return _split_start(tag + "_pair_start", split + lands, _pair_copies, len(parts))


def _reduce_pair_end(tag, pos, started, after):
    n = len(started["bufs"]) // 2
    bufs = _split_wait(tag + "_pair_wait", started, _pair_copies, after)
    chip = [_pair_add(f"{tag}_add{i}", pos, bufs[i], bufs[n + i]) for i in range(n)]
    return _chip_start(tag + "_chip_start", chip)


def _half_copies(bufs, send_sems, recv_sems, arrivals):
    x, y, c, _ = _place()
    sends = [_remote(b.at[c], b.at[c], send_sems.at[i], recv_sems.at[i], (x, y, 1 - c)) for i, b in enumerate(bufs)]
    lands = [_remote(b.at[1 - c], b.at[1 - c], send_sems.at[i], recv_sems.at[i], (x, y, 1 - c))
             for i, b in enumerate(bufs)] if arrivals else []
    return sends, lands


def _reduce_sum_start(tag, pos, started, after):
    chip, got = _chip_wait(tag + "_chip_wait", started, after)
    halves = [_chip_sum(f"{tag}_sum{i}", pos, chip[i], got[i]) for i in range(len(chip))]
    return _split_start(tag + "_half_start", halves, _half_copies, len(halves))


def _reduce_sum_end(tag, started, after, shapes):
    full = _split_wait(tag + "_half_wait", started, _half_copies, after)
    return [f.reshape(shp) for f, shp in zip(full, shapes)]


def _reduce_end(tag, pos, started, after, shapes):
    chip, got = _chip_wait(tag + "_chip_wait", started, after)
    halves = [_chip_sum(f"{tag}_sum{i}", pos, chip[i], got[i]) for i in range(len(chip))]
    full = _half_exchange(tag + "_half", halves)
    return [f.reshape(shp) for f, shp in zip(full, shapes)]


def _pad_to(v, mult):
    n = v.shape[0]
    return jnp.pad(v, (0, (-n) % mult))


def kernel(x, p, pool_w, pool_scale, ssm_in_w, ssm_conv_w, ssm_conv_b, ssm_dt_bias, ssm_a_log, ssm_d, ssm_norm_w, ssm_out_w, mlp_w1, mlp_w2, ln_g, ln_b, ple_w, ple_gate_w, loss_target, m_pool_w, m_pool_scale, m_ssm_in_w, m_ssm_conv_w, m_ssm_conv_b, m_ssm_dt_bias, m_ssm_a_log, m_ssm_d, m_ssm_norm_w, m_ssm_out_w, m_mlp_w1, m_mlp_w2, m_ln_g, m_ln_b, m_ple_w, m_ple_gate_w, v_pool_w, v_pool_scale, v_ssm_in_w, v_ssm_conv_w, v_ssm_conv_b, v_ssm_dt_bias, v_ssm_a_log, v_ssm_d, v_ssm_norm_w, v_ssm_out_w, v_mlp_w1, v_mlp_w2, v_ln_g, v_ln_b, v_ple_w, v_ple_gate_w):
    T, D = x.shape[1], x.shape[2]
    NG = len(POOL_WINDOWS)
    GD = D // NG
    DI = ssm_out_w.shape[1] * N_CHIPS
    H = ssm_dt_bias.shape[1]
    HPG = H // N_GROUPS
    GW = DI // N_GROUPS
    GS = N_GROUPS * D_STATE
    CD = DI + 2 * GS
    DF = mlp_w1.shape[2] * N_CHIPS
    PD = ple_w.shape[1]
    NIN = ssm_in_w.shape[2]

    xi, yi, ci = lax.axis_index("x"), lax.axis_index("y"), lax.axis_index("c")
    chip = 2 * xi + yi
    pos = jnp.stack([chip, ci]).astype(jnp.int32)

    x0 = x[0]
    tgt = loss_target[0]
    p0b, p1b = p[0, 0].astype(BF16), p[1, 0].astype(BF16)

    small_sh = jnp.concatenate([ssm_conv_w[0].reshape(-1), ssm_conv_b[0], ssm_norm_w[0],
                                ln_g.reshape(-1), ln_b.reshape(-1)])
    n_sh = small_sh.shape[0]
    small_all = _allgather_small("gather_small", _pad_to(small_sh, 1024).reshape(-1, 128))

    def halves(w, zero=None):
        w = w if zero is None else w + zero
        return w.astype(BF16).reshape((2, w.shape[0] // 2) + w.shape[1:])

    sh_pool = pool_w[0].astype(BF16)
    sh_pool = sh_pool.reshape((2, NG // 2) + sh_pool.shape[1:])
    started = {"ag0a": _gather_start("ag0a_start", [sh_pool, halves(mlp_w1[0])], after=small_all)}
    z0 = started["ag0a"]["token"][0, 0]
    groups = [("ag0b", [halves(mlp_w2[0], z0)]),
              ("ag0c", [halves(ple_gate_w[0], z0), halves(ple_w[0], z0)]),
              ("ag1a", [halves(ssm_in_w[0], z0)]),
              ("ag1b", [halves(ssm_out_w[0], z0), halves(mlp_w1[1], z0)]),
              ("ag1c", [halves(mlp_w2[1], z0), halves(ple_gate_w[1], z0), halves(ple_w[1], z0)])]

    def gather_end(tag, after):
        return _gather_forward(tag + "_fwd", _gather_wait(tag + "_wait", started[tag], after))

    def gather_land(tag, after):
        lands = _gather_wait(tag + "_wait", started[tag], after)
        return _split_start(tag + "_fwd_start", lands, _forward_copies, 3 * len(lands))

    def gather_done(tag, forwarding, after):
        return _split_wait(tag + "_fwd_wait", forwarding, _forward_copies, after)

    g_pool, g = gather_end("ag0a", [sh for _, shards in groups for sh in shards])
    prev = g_pool
    for tag, shards in groups:
        started[tag] = _gather_start(tag + "_start", shards, after=prev)
        prev = started[tag]["token"]
    all_started = prev

    g_w1, g_w2, g_pw, g_gw = {}, {}, {}, {}

    def set_w1(l, g):
        g_w1[l] = g.reshape(N_CHIPS, 1, D, DF // N_CHIPS)

    def set_w2(l, g2):
        g_w2[l] = g2.reshape(DF, D)

    def set_gate(l, ggw, gpw):
        g_gw[l] = ggw.reshape(D, D)
        g_pw[l] = gpw.reshape(N_CHIPS, 1, PD, D // N_CHIPS)

    set_w1(0, g)
    w_pool = jnp.transpose(g_pool.reshape(N_CHIPS, NG, GD // N_CHIPS, GD), (1, 0, 2, 3)).reshape(NG, GD, GD)

    small_all = small_all.reshape(N_DEV, -1)[0::2, :n_sh]
    cdq, niq, dq = CD // N_CHIPS, DI // N_CHIPS, D // N_CHIPS
    o = 0
    conv_w = jnp.concatenate([small_all[q, o:o + CONV_K * cdq].reshape(CONV_K, cdq) for q in range(N_CHIPS)], axis=1)
    o += CONV_K * cdq
    conv_b = small_all[:, o:o + cdq].reshape(1, CD)
    o += cdq
    norm_w = small_all[:, o:o + niq].reshape(1, DI)
    o += niq
    lng = jnp.transpose(small_all[:, o:o + 4 * dq].reshape(N_CHIPS, 2, 2, dq), (1, 2, 0, 3)).reshape(2, 2, 1, D)
    o += 4 * dq
    lnb = jnp.transpose(small_all[:, o:o + 4 * dq].reshape(N_CHIPS, 2, 2, dq), (1, 2, 0, 3)).reshape(2, 2, 1, D)

    pooled = _pool_fwd("pool_fwd", x0)
    hraw = _mm("pool_mm", "nn", pooled, w_pool, T, D, GD, tn=GD, deps=[all_started],
               a_spec=lambda tm, tn, tk: pl.BlockSpec((tm, tk), lambda i, j, k: (i, j)),
               b_spec=lambda tm, tn, tk: pl.BlockSpec((None, tk, tn), lambda i, j, k: (j, 0, 0)))
    x1, x1b, xh1, rs1 = _res_ln("ln00", x0, hraw, lng[0, 0], lnb[0, 0], scale=pool_scale)

    def mlp_fwd(l, xb, land_rest, deps=()):
        a, h2b = _mm(f"mlp{l}_up", "nn", xb, g_w1[l], T, DF, D, tn=min(1024, DF // N_CHIPS), deps=deps,
                     b_spec=_colshard_b(0, DF // N_CHIPS), out_dtype=(F32, BF16),
                     epi=lambda acc: (acc, jnp.square(jnp.maximum(acc, 0.0))))
        land_rest(a)
        h = _mm(f"mlp{l}_down", "nn", h2b, g_w2[l], T, D, DF)
        return a, h2b, h

    def ple_fwd(l, xb, pb):
        gl = _mm(f"gate{l}_logit", "nn", xb, g_gw[l], T, D, D)
        e = _mm(f"gate{l}_emb", "nn", pb, g_pw[l], T, D, PD, tn=min(1024, D // N_CHIPS),
                b_spec=_colshard_b(0, D // N_CHIPS))
        return gl, e

    a0, h2b0, h0 = mlp_fwd(0, x1b, lambda a: set_w2(0, *gather_end("ag0b", [a])))
    set_gate(0, *gather_end("ag0c", [h0]))
    x2, x2b, xh2, rs2 = _res_ln("ln01", x1, h0, lng[0, 1], lnb[0, 1])
    gl0, e0 = ple_fwd(0, x2b, p0b)
    x3, x3b, gate0 = _gate_fwd("gate0", x2, gl0, e0)

    g_in, = gather_end("ag1a", [x3b])

    def in_proj_cols(lo, hi):
        parts = [g_in[q].reshape(D, NIN)[:, max(lo - q * NIN, 0):min(hi - q * NIN, NIN)]
                 for q in range(N_CHIPS) if lo < (q + 1) * NIN and hi > q * NIN]
        return parts[0] if len(parts) == 1 else jnp.concatenate(parts, axis=1)

    w_z, w_xbc, w_dt = in_proj_cols(0, DI), in_proj_cols(DI, DI + CD), in_proj_cols(DI + CD, N_CHIPS * NIN)
    z = _mm("ssm_in_z", "nn", x3b, w_z, T, DI, D)
    xbc_pre = _mm("ssm_in_xbc", "nn", x3b, w_xbc, T, CD, D)
    dt_pre = _mm("ssm_in_dt", "nn", x3b, w_dt, T, H, D)
    xbc = _conv_fwd("conv_fwd", xbc_pre, conv_w, conv_b)
    dt = _dt_fwd("dt_fwd", dt_pre, ssm_dt_bias)
    dt_g = jnp.transpose(dt.reshape(T, N_GROUPS, HPG), (1, 0, 2))
    dtc = jnp.pad(dt_g, ((0, 0), (0, 0), (0, HEAD_PAD - HPG)))
    dtr = jnp.pad(jnp.transpose(dt_g, (0, 2, 1)), ((0, 0), (0, ROW_PAD - HPG), (0, 0)))
    al_g = ssm_a_log.reshape(N_GROUPS, HPG)
    alr = jnp.pad(al_g, ((0, 0), (0, HEAD_PAD - HPG)))[:, None, :]
    alc = jnp.pad(al_g, ((0, 0), (0, ROW_PAD - HPG)))[:, :, None]
    d_e = jnp.repeat(ssm_d.reshape(N_GROUPS, HPG), HEAD_DIM, axis=1)[:, None, :]
    fw1b = gather_land("ag1b", [xbc])
    ysc, hprev = _ssd_fwd("ssd_fwd", xbc, dtc, dtr, alr, alc, d_e, DI, deps=[fw1b["token"]])
    g_out, g = gather_done("ag1b", fw1b, [ysc])
    set_w1(1, g)
    w_out = g_out.reshape(DI, D)
    ynb, rsn = _gnorm_fwd("gnorm_fwd", ysc, z, norm_w)
    h1 = _mm("ssm_out", "nn", ynb, w_out, T, D, DI)
    fw1c = gather_land("ag1c", [h1])
    x4, x4b, xh4, rs4 = _res_ln("ln10", x3, h1, lng[1, 0], lnb[1, 0])

    def land_1c(a):
        g2, ggw, gpw = gather_done("ag1c", fw1c, [a])
        set_w2(1, g2)
        set_gate(1, ggw, gpw)

    a1, h2b1, h2 = mlp_fwd(1, x4b, land_1c, deps=[fw1c["token"]])
    x5, x5b, xh5, rs5 = _res_ln("ln11", x4, h2, lng[1, 1], lnb[1, 1])
    gl1, e1 = ple_fwd(1, x5b, p1b)
    dx6, gate1, loss_parts = _gate_loss("gate1_loss", x5, gl1, e1, tgt)
    loss_local = jnp.sum(loss_parts[0::8, 0])

    def ple_bwd(l, dxo, gate, e, xb, pb, deps=()):
        dgl, de = _gate_bwd(f"gate{l}_bwd", dxo, gate, e)
        d_gw = _mm(f"gate{l}_dw", "tn", xb, dgl, D, D, T, out_dtype=BF16, deps=deps).reshape(N_CHIPS, D // N_CHIPS, D)
        d_pw = _mm(f"gate{l}_dpw", "tn", pb, de, PD, D, T, out_dtype=BF16, tn=min(1024, D // N_CHIPS),
                   o_shape=(N_CHIPS, PD, D // N_CHIPS), o_spec=_colshard_o(D // N_CHIPS))
        dx = _mm(f"gate{l}_dx", "nt", dgl, g_gw[l], T, D, D, deps=deps,
                 epi=lambda acc, r: acc + r, extras=[dxo])
        return dx, d_gw, d_pw

    def mlp_bwd(l, du, dub, a, h2b, xb):
        d_w2 = _mm(f"mlp{l}_dw2", "tn", h2b, dub, DF, D, T, out_dtype=BF16).reshape(N_CHIPS, DF // N_CHIPS, D)
        da = _mm(f"mlp{l}_da", "nt", dub, g_w2[l], T, DF, D, out_dtype=BF16,
                 epi=lambda acc, av: acc * (2.0 * jnp.maximum(av, 0.0)), extras=[a])
        d_w1 = _mm(f"mlp{l}_dw1", "tn", xb, da, D, DF, T, out_dtype=BF16, tn=min(1024, DF // N_CHIPS),
                   o_shape=(N_CHIPS, D, DF // N_CHIPS), o_spec=_colshard_o(DF // N_CHIPS))
        dx = _mm(f"mlp{l}_dx", "nt", da, g_w1[l], T, D, DF, k_unit=DF // N_CHIPS,
                 b_spec=_colshard_bt(0, DF // N_CHIPS), epi=lambda acc, r: acc + ALPHA * r, extras=[du])
        return dx, d_w1, d_w2

    dx5, d_gw1, d_pw1 = ple_bwd(1, dx6, gate1, e1, x5b, p1b)
    du5, du5b, dg11, db11 = _ln_bwd("ln11_bwd", dx5, xh5, rs5, lng[1, 1])
    dx4, d_w1_1, d_w2_1 = mlp_bwd(1, du5, du5b, a1, h2b1, x4b)
    parts1a = [d_w1_1, d_w2_1, d_gw1, d_pw1]
    pair1a = _reduce_pair_start("rs1a", parts1a)
    du4, du4b, dg10, db10 = _ln_bwd("ln10_bwd", dx4, xh4, rs4, lng[1, 0])
    d_wout = _mm("ssm_out_dw", "tn", ynb, du4b, DI, D, T, out_dtype=BF16,
                 deps=[pair1a["token"]]).reshape(N_CHIPS, DI // N_CHIPS, D)
    red1a = _reduce_pair_end("rs1a", pos, pair1a, [d_wout])
    dyn = _mm("ssm_out_dx", "nt", du4b, w_out, T, DI, D, deps=[red1a["token"]])
    dysc, dzb, dnorm_w = _gnorm_bwd("gnorm_bwd", dyn, ysc, z, norm_w, rsn)
    dxs, dbm, dcm, ddt_g, dalog_g, dd_g = _ssd_bwd("ssd_bwd", dysc, xbc, dtc, dtr, alr, alc, d_e, hprev, DI)
    dconv, dconv_w, dconv_b = _conv_bwd_a("conv_bwd_a", dxs, dbm, dcm, xbc_pre, conv_w, conv_b)
    dxbcb = _conv_bwd_b("conv_bwd_b", dconv, conv_w)
    ddt = jnp.transpose(ddt_g[:, :, :HPG], (1, 0, 2)).reshape(T, H)
    _, ddtpb, ddt_bias = _dt_bwd("dt_bwd", ddt, dt_pre, ssm_dt_bias)
    da_log = dalog_g[:, 0, :HPG].reshape(1, H)
    dd_skip = dd_g[:, 0, :HPG].reshape(1, H)
    d_wz = _mm("ssm_in_dwz", "tn", x3b, dzb, D, DI, T, out_dtype=BF16)
    d_wx = _mm("ssm_in_dwx", "tn", x3b, dxbcb, D, CD, T, out_dtype=BF16)
    d_wdt = _mm("ssm_in_dwdt", "tn", x3b, ddtpb, D, H, T, out_dtype=BF16)

    def in_proj_shard(q):
        parts = []
        for piece, start in ((d_wz, 0), (d_wx, DI), (d_wdt, DI + CD)):
            lo, hi = max(q * NIN - start, 0), min((q + 1) * NIN - start, piece.shape[1])
            if lo < hi:
                parts.append(piece[:, lo:hi])
        return parts[0] if len(parts) == 1 else jnp.concatenate(parts, axis=1)

    d_win = jnp.stack([in_proj_shard(q) for q in range(N_CHIPS)])
    parts1b = [d_win, d_wout]
    pair1b = _reduce_pair_start("rs1b", parts1b)
    dx3 = _mm("ssm_in_dxz", "nt", dzb, w_z, T, D, DI, epi=lambda acc, r: acc + ALPHA * r, extras=[du4],
              deps=[pair1b["token"]])
    dx3 = _mm("ssm_in_dxx", "nt", dxbcb, w_xbc, T, D, CD, epi=lambda acc, r: acc + r, extras=[dx3])
    dx3 = _mm("ssm_in_dxdt", "nt", ddtpb, w_dt, T, D, H, epi=lambda acc, r: acc + r, extras=[dx3])

    red1b = _reduce_pair_end("rs1b", pos, pair1b, [dx3])

    dx2, d_gw0, d_pw0 = ple_bwd(0, dx3, gate0, e0, x2b, p0b, deps=[red1b["token"]])
    du2, du2b, dg01, db01 = _ln_bwd("ln01_bwd", dx2, xh2, rs2, lng[0, 1])
    dx1, d_w1_0, d_w2_0 = mlp_bwd(0, du2, du2b, a0, h2b0, x1b)
    parts0a = [d_w1_0, d_w2_0, d_gw0, d_pw0]
    pair0a = _reduce_pair_start("rs0a", parts0a)
    du1, dhrb, dg00, db00, dscale = _ln_bwd("ln00_bwd", dx1, xh1, rs1, lng[0, 0], hraw=hraw, scale=pool_scale)
    dln_g = jnp.stack([jnp.stack([dg00, dg01]), jnp.stack([dg10, dg11])]).reshape(-1)
    dln_b = jnp.stack([jnp.stack([db00, db01]), jnp.stack([db10, db11])]).reshape(-1)
    small_g = jnp.concatenate([dscale.reshape(-1), dconv_w.reshape(-1), dconv_b.reshape(-1), ddt_bias.reshape(-1),
                               da_log.reshape(-1), dd_skip.reshape(-1), dnorm_w.reshape(-1), dln_g, dln_b,
                               loss_local.reshape(1)])
    n_sg = small_g.shape[0]
    sg_all, sg_sum = _allgather_small("gather_small_grads", _pad_to(small_g, 1024).reshape(-1, 128),
                                      after=[d_w1_0, d_w2_0], reduce=True)
    d_wpool = _mm("pool_dw", "tn", pooled, dhrb, D, GD, T, tm=GD, tn=GD, out_dtype=BF16, deps=[pair0a["token"]],
                  b_spec=lambda tm, tn, tk: pl.BlockSpec((tk, tn), lambda i, j, k: (k, i)),
                  o_shape=(NG, GD, GD), o_spec=lambda tm, tn, tk: pl.BlockSpec((None, tm, tn), lambda i, j, k: (i, 0, 0)))
    d_wpool = jnp.transpose(d_wpool.reshape(NG, N_CHIPS, GD // N_CHIPS, GD), (1, 0, 2, 3)).reshape(N_CHIPS, NG * GD // N_CHIPS, GD)
    red0a = _reduce_pair_end("rs0a", pos, pair0a, [d_wpool, sg_all])
    parts0b = [d_wpool]
    pair0b = _reduce_pair_start("rs0b", parts0b)
    dpooled = _mm("pool_dx", "nt", dhrb, w_pool, T, D, GD, tn=GD, deps=[red0a["token"], pair0b["token"]],
                  a_spec=lambda tm, tn, tk: pl.BlockSpec((tm, tk), lambda i, j, k: (i, j)),
                  b_spec=lambda tm, tn, tk: pl.BlockSpec((None, tn, tk), lambda i, j, k: (j, 0, 0)))
    grad_x = _pool_bwd("pool_bwd", dpooled, du1)
    red0b = _reduce_pair_end("rs0b", pos, pair0b, [grad_x])

    def adam(name, w, m, v, g, layer=0, prev=None):
        w3, m3, v3 = (t.reshape((t.shape[0], -1, t.shape[-1])) for t in (w, m, v))
        return _adamw(name, w3, m3, v3, g, layer, prev)

    sum1a = _reduce_sum_start("rs1a", pos, red1a, [red0b["token"]])
    sum1b = _reduce_sum_start("rs1b", pos, red1b, [sum1a["token"]])
    r_w1_1, r_w2_1, r_gw1, r_pw1 = _reduce_sum_end("rs1a", sum1a, [sum1b["token"]], [p.shape[1:] for p in parts1a])
    stacked = {"mlp_w1": (mlp_w1, m_mlp_w1, v_mlp_w1), "mlp_w2": (mlp_w2, m_mlp_w2, v_mlp_w2),
               "ple_w": (ple_w, m_ple_w, v_ple_w), "ple_gate_w": (ple_gate_w, m_ple_gate_w, v_ple_gate_w)}
    grads1 = {"mlp_w1": r_w1_1, "mlp_w2": r_w2_1, "ple_w": r_pw1, "ple_gate_w": r_gw1}
    upper = {nm: adam(f"adam_{nm}_l1", *stacked[nm], grads1[nm], layer=1) for nm in stacked}
    r_in, r_out = _reduce_sum_end("rs1b", sum1b, [t[1] for t in upper.values()], [p.shape[1:] for p in parts1b])
    big = {"ssm_in_w": [jnp.swapaxes(t, 1, 2) for t in adam(
               "adam_ssm_in_w", *(jnp.swapaxes(t, 1, 2) for t in (ssm_in_w, m_ssm_in_w, v_ssm_in_w)), r_in.T)],
           "ssm_out_w": adam("adam_ssm_out_w", ssm_out_w, m_ssm_out_w, v_ssm_out_w, r_out)}

    sg = sg_sum.reshape(-1)[:n_sg]
    o = 0

    def take(nel):
        nonlocal o
        v = sg[o:o + nel]
        o += nel
        return v

    g_scale = take(D).reshape(1, D)
    g_conv_w_full = take(CONV_K * CD).reshape(CONV_K, CD)
    g_conv_b_full = take(CD).reshape(1, CD)
    g_dt_bias = take(H).reshape(1, H)
    g_a_log = take(H).reshape(1, H)
    g_d = take(H).reshape(1, H)
    g_norm_full = take(DI).reshape(1, DI)
    g_lng_full = take(4 * D).reshape(2, 2, D)
    g_lnb_full = take(4 * D).reshape(2, 2, D)
    loss = take(1).reshape(())
    g_conv_w = lax.dynamic_slice_in_dim(g_conv_w_full, chip * cdq, cdq, axis=1)[None]
    g_conv_b = lax.dynamic_slice_in_dim(g_conv_b_full, chip * cdq, cdq, axis=1)
    g_norm = lax.dynamic_slice_in_dim(g_norm_full, chip * niq, niq, axis=1)
    g_lng = lax.dynamic_slice_in_dim(g_lng_full, chip * dq, dq, axis=2)
    g_lnb = lax.dynamic_slice_in_dim(g_lnb_full, chip * dq, dq, axis=2)

    small = [("pool_scale", pool_scale, m_pool_scale, v_pool_scale, g_scale),
             ("ssm_conv_w", ssm_conv_w, m_ssm_conv_w, v_ssm_conv_w, g_conv_w),
             ("ssm_conv_b", ssm_conv_b, m_ssm_conv_b, v_ssm_conv_b, g_conv_b),
             ("ssm_dt_bias", ssm_dt_bias, m_ssm_dt_bias, v_ssm_dt_bias, g_dt_bias),
             ("ssm_a_log", ssm_a_log, m_ssm_a_log, v_ssm_a_log, g_a_log),
             ("ssm_d", ssm_d, m_ssm_d, v_ssm_d, g_d),
             ("ssm_norm_w", ssm_norm_w, m_ssm_norm_w, v_ssm_norm_w, g_norm),
             ("ln_g", ln_g, m_ln_g, v_ln_g, g_lng),
             ("ln_b", ln_b, m_ln_b, v_ln_b, g_lnb)]

    def pack(idx):
        flat = _pad_to(jnp.concatenate([s[idx].reshape(-1) for s in small]), 1024)
        return flat.reshape(1, -1, 128)

    sm_out = _adamw("adam_small", pack(1), pack(2), pack(3), pack(4)[0])
    small_res = {}
    o = 0
    for s in small:
        nel = s[1].size
        small_res[s[0]] = [t.reshape(-1)[o:o + nel].reshape(s[1].shape) for t in sm_out]
        o += nel

    r_w1_0, r_w2_0, r_gw0, r_pw0 = _reduce_end(
        "rs0a", pos, red0a, [big["ssm_in_w"][1], big["ssm_out_w"][1], sm_out[1]] + [upper[nm][1] for nm in upper],
        [p.shape[1:] for p in parts0a])
    r_pool, = _reduce_end("rs0b", pos, red0b, [r_w1_0], [p.shape[1:] for p in parts0b])
    grads0 = {"mlp_w1": r_w1_0, "mlp_w2": r_w2_0, "ple_w": r_pw0, "ple_gate_w": r_gw0}
    big["pool_w"] = adam("adam_pool_w", pool_w, m_pool_w, v_pool_w, r_pool)
    for nm in stacked:
        big[nm] = adam(f"adam_{nm}_l0", *stacked[nm], grads0[nm], layer=0, prev=upper[nm])
    shapes = {"pool_w": pool_w.shape, "ssm_in_w": ssm_in_w.shape, "ssm_out_w": ssm_out_w.shape,
              **{nm: stacked[nm][0].shape for nm in stacked}}
    big = {nm: [t.reshape(shapes[nm]) for t in big[nm]] for nm in big}

    order = ["pool_w", "pool_scale", "ssm_in_w", "ssm_conv_w", "ssm_conv_b", "ssm_dt_bias", "ssm_a_log", "ssm_d",
             "ssm_norm_w", "ssm_out_w", "mlp_w1", "mlp_w2", "ln_g", "ln_b", "ple_w", "ple_gate_w"]
    res = {**big, **small_res}
    outs = [loss, grad_x[None]]
    for kind in range(4):
        outs += [res[nm][kind] for nm in order]
    return tuple(outs)
```
